```python
import math
import jax, jax.numpy as jnp
from jax import lax
import numpy as np

D_MODEL = 1024
BATCH = 4
SEQ = 4096
DEPTH = 2

GRID_W = 64
CTX_LEN = 256
HEAD_DIM = 64
EPS = 1e-6
NEG = -1e30
FOURIER_WIDTH = D_MODEL // 2
FOURIER_GROUPS = 4
FOURIER_GROUP_CH = FOURIER_WIDTH // FOURIER_GROUPS
WIN_Q_HEADS = (D_MODEL // 2) // HEAD_DIM
WIN_KV_HEADS = 2
WIN_RADIUS = 128
WIN_BLOCK = 128
EV_IN_WIDTH = FOURIER_WIDTH + (WIN_Q_HEADS + 2 * WIN_KV_HEADS) * HEAD_DIM
EV_OUT_WIDTH = FOURIER_WIDTH + WIN_Q_HEADS * HEAD_DIM
NA_HEADS = D_MODEL // HEAD_DIM
NA_KH = 8
NA_KW = 16
ROPE_THETA = 10000.0
ROPE_FREQS = HEAD_DIM // 4
D_FF = ((8 * D_MODEL // 3 + 255) // 256) * 256
N_EVEN = (DEPTH + 1) // 2
N_ODD = DEPTH // 2

kernel_name = "hybrid_fourier_window_natten_dit"


def _rms(x, g):
    x32 = x.astype(jnp.float32)
    y = x32 * lax.rsqrt(jnp.mean(x32 * x32, axis=-1, keepdims=True) + EPS)
    return (y * g.astype(jnp.float32)).astype(x.dtype)


def _axial_angles(L):
    t = jnp.arange(L, dtype=jnp.int32)
    row = (t // GRID_W).astype(jnp.float32)
    col = (t % GRID_W).astype(jnp.float32)
    inv = ROPE_THETA ** (-jnp.arange(ROPE_FREQS, dtype=jnp.float32) / ROPE_FREQS)
    return row[:, None] * inv[None, :], col[:, None] * inv[None, :]


def _rotate(xa, ang):
    cos = jnp.cos(ang)[:, None, :].astype(xa.dtype)
    sin = jnp.sin(ang)[:, None, :].astype(xa.dtype)
    x1, x2 = xa[..., :ROPE_FREQS], xa[..., ROPE_FREQS:]
    return jnp.concatenate([x1 * cos - x2 * sin, x2 * cos + x1 * sin], axis=-1)


def _rope_2d(x, ang_row, ang_col):
    h = HEAD_DIM // 2
    return jnp.concatenate([_rotate(x[..., :h], ang_row), _rotate(x[..., h:], ang_col)], axis=-1)


def _fourier_mix(f):
    B, L, _ = f.shape
    fg = f.astype(jnp.float32).reshape(B, L, FOURIER_GROUPS, FOURIER_GROUP_CH)
    mixed = jnp.fft.fft2(fg, axes=(1, 3), norm="ortho").real
    return mixed.reshape(B, L, FOURIER_WIDTH).astype(f.dtype)


def _ctx_attention(q, k, v, sink):
    B, C, H, d = q.shape
    KV = k.shape[2]
    G = H // KV
    qg = q.reshape(B, C, KV, G, d)
    s = jnp.einsum('bqkgd,bjkd->bkgqj', qg, k).astype(jnp.float32) * (1.0 / math.sqrt(d))
    if sink is not None:
        sk = jnp.broadcast_to(sink.astype(jnp.float32).reshape(1, KV, G, 1, 1), (B, KV, G, C, 1))
        p = jax.nn.softmax(jnp.concatenate([s, sk], axis=-1), axis=-1)[..., :C]
    else:
        p = jax.nn.softmax(s, axis=-1)
    o = jnp.einsum('bkgqj,bjkd->bqkgd', p.astype(v.dtype), v)
    return o.reshape(B, C, H * d)


def _window_attention(q, k, v, k_ctx, v_ctx, sink):
    B, L, H, d = q.shape
    KV = k.shape[2]
    G = H // KV
    C = k_ctx.shape[1]
    nb = L // WIN_BLOCK
    nw = 3 * WIN_BLOCK
    scale = 1.0 / math.sqrt(d)
    qb = q.reshape(B, nb, WIN_BLOCK, KV, G, d)
    pad = ((0, 0), (WIN_BLOCK, WIN_BLOCK), (0, 0), (0, 0))
    kp = jnp.pad(k, pad).reshape(B, nb + 2, WIN_BLOCK, KV, d)
    vp = jnp.pad(v, pad).reshape(B, nb + 2, WIN_BLOCK, KV, d)
    kw = jnp.concatenate([kp[:, :-2], kp[:, 1:-1], kp[:, 2:]], axis=2)
    vw = jnp.concatenate([vp[:, :-2], vp[:, 1:-1], vp[:, 2:]], axis=2)
    s_win = jnp.einsum('bnqkgd,bnjkd->bnkgqj', qb, kw).astype(jnp.float32) * scale
    s_ctx = jnp.einsum('bnqkgd,bjkd->bnkgqj', qb, k_ctx).astype(jnp.float32) * scale
    blk = jnp.arange(nb, dtype=jnp.int32)[:, None, None] * WIN_BLOCK
    qpos = blk + jnp.arange(WIN_BLOCK, dtype=jnp.int32)[None, :, None]
    kpos = blk - WIN_BLOCK + jnp.arange(nw, dtype=jnp.int32)[None, None, :]
    mask = (jnp.abs(kpos - qpos) <= WIN_RADIUS) & (kpos >= 0) & (kpos < L)
    s_win = jnp.where(mask[None, :, None, None], s_win, NEG)
    sk = jnp.broadcast_to(sink.astype(jnp.float32).reshape(1, 1, KV, G, 1, 1), (B, nb, KV, G, WIN_BLOCK, 1))
    p = jax.nn.softmax(jnp.concatenate([s_win, s_ctx, sk], axis=-1), axis=-1).astype(v.dtype)
    o = (jnp.einsum('bnkgqj,bnjkd->bnqkgd', p[..., :nw], vw)
         + jnp.einsum('bnkgqj,bjkd->bnqkgd', p[..., nw:nw + C], v_ctx))
    return o.reshape(B, L, H * d)


def _neighbourhood_attention(q, k, v, k_ctx, v_ctx, rel_bias):
    B, L, H, d = q.shape
    rows = L // GRID_W
    kh = min(NA_KH, rows)
    n = kh * GRID_W
    scale = 1.0 / math.sqrt(d)
    qg = q.reshape(B, rows, GRID_W, H, d)
    kg = k.reshape(B, rows, GRID_W, H, d)
    vg = v.reshape(B, rows, GRID_W, H, d)
    cq = jnp.arange(GRID_W, dtype=jnp.int32)
    c0 = jnp.clip(cq - NA_KW // 2, 0, GRID_W - NA_KW)
    col_ok = (cq[None, :] >= c0[:, None]) & (cq[None, :] < c0[:, None] + NA_KW)
    mask = jnp.tile(col_ok, (1, kh))
    dc_idx = jnp.clip(cq[None, :] - cq[:, None] + NA_KW - 1, 0, 2 * NA_KW - 2)

    def one_row(r):
        r0 = jnp.clip(r - kh // 2, 0, rows - kh)
        k_rows = lax.dynamic_slice_in_dim(kg, r0, kh, axis=1).reshape(B, n, H, d)
        v_rows = lax.dynamic_slice_in_dim(vg, r0, kh, axis=1).reshape(B, n, H, d)
        q_row = lax.dynamic_index_in_dim(qg, r, axis=1, keepdims=False)
        dr_idx = r0 + jnp.arange(kh, dtype=jnp.int32) - r + NA_KH - 1
        bias = rel_bias[:, dr_idx[None, :, None], dc_idx[:, None, :]]
        bias = bias.reshape(H, GRID_W, n).astype(jnp.float32)
        s_nb = jnp.einsum('bqhd,bjhd->bhqj', q_row, k_rows).astype(jnp.float32) * scale + bias[None]
        s_nb = jnp.where(mask[None, None], s_nb, NEG)
        s_cx = jnp.einsum('bqhd,bjhd->bhqj', q_row, k_ctx).astype(jnp.float32) * scale
        p = jax.nn.softmax(jnp.concatenate([s_nb, s_cx], axis=-1), axis=-1).astype(v.dtype)
        return (jnp.einsum('bhqj,bjhd->bqhd', p[..., :n], v_rows)
                + jnp.einsum('bhqj,bjhd->bqhd', p[..., n:], v_ctx))

    out = lax.map(one_row, jnp.arange(rows, dtype=jnp.int32))
    return out.transpose(1, 0, 2, 3, 4).reshape(B, L, H * d)


def _even_mixer(h, hc, w_in, w_out, q_g, k_g, sink, ang_row, ang_col, ctx_out):
    F = FOURIER_WIDTH
    QW = WIN_Q_HEADS * HEAD_DIM
    KW = WIN_KV_HEADS * HEAD_DIM

    def split(t):
        Bt, Lt = t.shape[0], t.shape[1]
        f = t[..., :F]
        q = _rms(t[..., F:F + QW].reshape(Bt, Lt, WIN_Q_HEADS, HEAD_DIM), q_g)
        k = _rms(t[..., F + QW:F + QW + KW].reshape(Bt, Lt, WIN_KV_HEADS, HEAD_DIM), k_g)
        v = t[..., F + QW + KW:].reshape(Bt, Lt, WIN_KV_HEADS, HEAD_DIM)
        return f, q, k, v

    f, q, k, v = split(h @ w_in)
    fc, qc, kc, vc = split(hc @ w_in)
    q = _rope_2d(q, ang_row, ang_col)
    k = _rope_2d(k, ang_row, ang_col)
    o = jnp.concatenate([_fourier_mix(f), _window_attention(q, k, v, kc, vc, sink)], axis=-1) @ w_out
    oc = None
    if ctx_out:
        oc = jnp.concatenate([_fourier_mix(fc), _ctx_attention(qc, kc, vc, sink)], axis=-1) @ w_out
    return o, oc


def _odd_mixer(h, hc, w_in, w_out, q_g, k_g, rel_bias, ctx_out):
    W = NA_HEADS * HEAD_DIM

    def split(t):
        Bt, Lt = t.shape[0], t.shape[1]
        q = _rms(t[..., :W].reshape(Bt, Lt, NA_HEADS, HEAD_DIM), q_g)
        k = _rms(t[..., W:2 * W].reshape(Bt, Lt, NA_HEADS, HEAD_DIM), k_g)
        v = t[..., 2 * W:].reshape(Bt, Lt, NA_HEADS, HEAD_DIM)
        return q, k, v

    q, k, v = split(h @ w_in)
    qc, kc, vc = split(hc @ w_in)
    o = _neighbourhood_attention(q, k, v, kc, vc, rel_bias) @ w_out
    oc = None
    if ctx_out:
        oc = _ctx_attention(qc, kc, vc, None) @ w_out
    return o, oc


def _swiglu(h, wg, wu, wd):
    return (jax.nn.silu(h @ wg) * (h @ wu)) @ wd


def setup_inputs(seed: int = 0) -> dict:
    key = jax.random.key(seed)
    ks = jax.random.split(key, 24)
    D = D_MODEL
    nrm = jax.random.normal
    f32 = jnp.float32
    return {
        "x": nrm(ks[0], (BATCH, SEQ, D), f32),
        "c": nrm(ks[1], (BATCH, D), f32),
        "ctx": nrm(ks[2], (BATCH, CTX_LEN, D), f32),
        "c_ctx": nrm(ks[3], (D,), f32),
        "ada_w": nrm(ks[4], (DEPTH, D, 6 * D), f32) * D ** -0.5,
        "ada_b": nrm(ks[5], (DEPTH, 6 * D), f32) * 0.01,
        "norm1_g": 1.0 + 0.01 * nrm(ks[6], (DEPTH, D), f32),
        "norm2_g": 1.0 + 0.01 * nrm(ks[7], (DEPTH, D), f32),
        "ffn_w_gate": nrm(ks[8], (DEPTH, D, D_FF), f32) * D ** -0.5,
        "ffn_w_up": nrm(ks[9], (DEPTH, D, D_FF), f32) * D ** -0.5,
        "ffn_w_down": nrm(ks[10], (DEPTH, D_FF, D), f32) * D_FF ** -0.5,
        "ev_w_in": nrm(ks[11], (N_EVEN, D, EV_IN_WIDTH), f32) * D ** -0.5,
        "ev_w_out": nrm(ks[12], (N_EVEN, EV_OUT_WIDTH, D), f32) * EV_OUT_WIDTH ** -0.5,
        "ev_q_norm": 1.0 + 0.01 * nrm(ks[13], (N_EVEN, HEAD_DIM), f32),
        "ev_k_norm": 1.0 + 0.01 * nrm(ks[14], (N_EVEN, HEAD_DIM), f32),
        "ev_sink": 0.5 * nrm(ks[15], (N_EVEN, WIN_Q_HEADS), f32),
        "od_w_in": nrm(ks[16], (N_ODD, D, 3 * NA_HEADS * HEAD_DIM), f32) * D ** -0.5,
        "od_w_out": nrm(ks[17], (N_ODD, NA_HEADS * HEAD_DIM, D), f32) * (NA_HEADS * HEAD_DIM) ** -0.5,
        "od_q_norm": 1.0 + 0.01 * nrm(ks[18], (N_ODD, HEAD_DIM), f32),
        "od_k_norm": 1.0 + 0.01 * nrm(ks[19], (N_ODD, HEAD_DIM), f32),
        "od_rel_bias": 0.1 * nrm(ks[20], (N_ODD, NA_HEADS, 2 * NA_KH - 1, 2 * NA_KW - 1), f32),
    }


def reference(x, c, ctx, c_ctx, ada_w, ada_b, norm1_g, norm2_g, ffn_w_gate, ffn_w_up, ffn_w_down,
              ev_w_in, ev_w_out, ev_q_norm, ev_k_norm, ev_sink,
              od_w_in, od_w_out, od_q_norm, od_k_norm, od_rel_bias):
    L = x.shape[1]
    ang_row, ang_col = _axial_angles(L)
    y = ctx
    for i in range(DEPTH):
        last = i == DEPTH - 1
        m_lat = (jax.nn.silu(c) @ ada_w[i] + ada_b[i])[:, None, :]
        m_ctx = jax.nn.silu(c_ctx) @ ada_w[i] + ada_b[i]
        sh1, sc1, g1, sh2, sc2, g2 = jnp.split(m_lat, 6, axis=-1)
        csh1, csc1, cg1, csh2, csc2, cg2 = jnp.split(m_ctx, 6, axis=-1)
        h = _rms(x, norm1_g[i]) * (1.0 + sc1) + sh1
        hc = _rms(y, norm1_g[i]) * (1.0 + csc1) + csh1
        if i % 2 == 0:
            j = i // 2
            o, oc = _even_mixer(h, hc, ev_w_in[j], ev_w_out[j], ev_q_norm[j], ev_k_norm[j],
                                ev_sink[j], ang_row, ang_col, not last)
        else:
            j = i // 2
            o, oc = _odd_mixer(h, hc, od_w_in[j], od_w_out[j], od_q_norm[j], od_k_norm[j],
                               od_rel_bias[j], not last)
        x = x + g1 * o
        h = _rms(x, norm2_g[i]) * (1.0 + sc2) + sh2
        x = x + g2 * _swiglu(h, ffn_w_gate[i], ffn_w_up[i], ffn_w_down[i])
        if not last:
            y = y + cg1 * oc
            hc = _rms(y, norm2_g[i]) * (1.0 + csc2) + csh2
            y = y + cg2 * _swiglu(hc, ffn_w_gate[i], ffn_w_up[i], ffn_w_down[i])
    return x
```

```python
import functools
import math

import numpy as np
import jax
import jax.numpy as jnp
from jax import lax
from jax.experimental import pallas as pl
from jax.experimental.pallas import tpu as pltpu

D_MODEL = 1024
BATCH = 4
SEQ = 4096
DEPTH = 2
GRID_W = 64
CTX_LEN = 256
HEAD_DIM = 64
EPS = 1e-6
NEG = -1e30
FOURIER_WIDTH = D_MODEL // 2
FOURIER_GROUPS = 4
FOURIER_GROUP_CH = FOURIER_WIDTH // FOURIER_GROUPS
WIN_Q_HEADS = (D_MODEL // 2) // HEAD_DIM
WIN_KV_HEADS = 2
WIN_GROUP = WIN_Q_HEADS // WIN_KV_HEADS
WIN_RADIUS = 128
WIN_BLOCK = 128
QW = WIN_Q_HEADS * HEAD_DIM
KW = WIN_KV_HEADS * HEAD_DIM
EV_IN_WIDTH = FOURIER_WIDTH + QW + 2 * KW
NA_HEADS = D_MODEL // HEAD_DIM
NA_KH = 8
NA_KW = 16
NA_WIDTH = NA_HEADS * HEAD_DIM
ROPE_THETA = 10000.0
ROPE_FREQS = HEAD_DIM // 4
D_FF = ((8 * D_MODEL // 3 + 255) // 256) * 256
GRID_ROWS = SEQ // GRID_W

LANES = 128
MOD_ROWS = 8
CTX_MOD_ROW = BATCH
FFT_N = 64
FF_CHUNK = 256
VMEM_LIMIT = 56 * 1024 * 1024

BF16 = jnp.bfloat16
F32 = jnp.float32

assert DEPTH == 2 and SEQ == FFT_N * FFT_N and D_FF % FF_CHUNK == 0


def _params(*sem):
    return pltpu.CompilerParams(dimension_semantics=sem, vmem_limit_bytes=VMEM_LIMIT)


def _dot(a, b):
    return jnp.dot(a, b, preferred_element_type=F32)


def _dot_nt(a, b):
    return lax.dot_general(a, b, (((1,), (1,)), ((), ())), preferred_element_type=F32)


def _silu(x):
    return x / (1.0 + jnp.exp(-x))


def _const_spec(shape):
    nd = len(shape)
    return pl.BlockSpec(shape, lambda *_: (0,) * nd, pipeline_mode=pl.Buffered(1))


def _dft_cos_sin(n):
    idx = (np.arange(n)[:, None] * np.arange(n)[None, :]) % n
    ang = 2.0 * np.pi * idx / n
    return np.cos(ang), np.sin(ang)


def _fourier_tables():
    cc, sc = _dft_cos_sin(FOURIER_GROUP_CH)
    wc = np.concatenate([cc, -sc], axis=1) / math.sqrt(FOURIER_GROUP_CH)
    c64, s64 = _dft_cos_sin(FFT_N)
    m1 = np.concatenate([c64, -s64], axis=0) / math.sqrt(FFT_N)
    m3 = np.concatenate([c64, s64], axis=1) / math.sqrt(FFT_N)
    tw = (np.arange(FFT_N)[:, None] * np.arange(FFT_N)[None, :]) % SEQ
    tw = 2.0 * np.pi * tw / SEQ
    twc = np.repeat(np.cos(tw)[:, :, None], LANES, axis=2)
    tws = np.repeat(np.sin(tw)[:, :, None], LANES, axis=2)
    cx, sx = _dft_cos_sin(CTX_LEN)
    mctx = np.concatenate([cx, sx], axis=1) / math.sqrt(CTX_LEN)
    as32 = lambda a: jnp.asarray(a, F32)
    return (as32(wc).astype(BF16), as32(m1).astype(BF16), as32(m3).astype(BF16),
            as32(twc), as32(tws), as32(mctx).astype(BF16))


def _head_mean_matrix():
    blk = np.kron(np.eye(LANES // HEAD_DIM), np.ones((HEAD_DIM, HEAD_DIM))) / HEAD_DIM
    return jnp.asarray(blk, BF16)


def _rope_tables():
    t = jnp.arange(SEQ, dtype=jnp.int32)
    row = (t // GRID_W).astype(F32)
    col = (t % GRID_W).astype(F32)
    inv = ROPE_THETA ** (-jnp.arange(ROPE_FREQS, dtype=F32) / ROPE_FREQS)
    ang_row = row[:, None] * inv[None, :]
    ang_col = col[:, None] * inv[None, :]
    zero = jnp.zeros_like(ang_row)
    cos = jnp.concatenate([jnp.cos(ang_row)] * 2 + [jnp.cos(ang_col)] * 2, axis=1)
    sin_hi = jnp.concatenate([-jnp.sin(ang_row), zero, -jnp.sin(ang_col), zero], axis=1)
    sin_lo = jnp.concatenate([zero, jnp.sin(ang_row), zero, jnp.sin(ang_col)], axis=1)
    rep = LANES // HEAD_DIM
    return jnp.tile(cos, (1, rep)), jnp.tile(sin_hi, (1, rep)), jnp.tile(sin_lo, (1, rep))


def _na_onehot():
    cq = np.arange(GRID_W)
    dc = np.clip(cq[None, :] - cq[:, None] + NA_KW - 1, 0, 2 * NA_KW - 2)
    oh = np.zeros((LANES, GRID_W * GRID_W), np.float32)
    oh[dc.reshape(-1), np.arange(GRID_W * GRID_W)] = 1.0
    return jnp.asarray(oh, BF16)


def _mod_kernel(cs_ref, w_ref, b_ref, o_ref):
    s = _silu(cs_ref[...]).astype(BF16)
    o_ref[0] = _dot(s, w_ref[0].astype(BF16)) + b_ref[0]


def _modulation(cs, ada_w, ada_b):
    tn = 1536
    return pl.pallas_call(
        _mod_kernel,
        grid=(DEPTH, 6 * D_MODEL // tn),
        in_specs=[pl.BlockSpec((MOD_ROWS, D_MODEL), lambda i, j: (0, 0)),
                  pl.BlockSpec((1, D_MODEL, tn), lambda i, j: (i, 0, j)),
                  pl.BlockSpec((1, 1, tn), lambda i, j: (i, 0, j))],
        out_specs=pl.BlockSpec((1, MOD_ROWS, tn), lambda i, j: (i, 0, j)),
        out_shape=jax.ShapeDtypeStruct((DEPTH, MOD_ROWS, 6 * D_MODEL), F32),
        compiler_params=_params("arbitrary", "arbitrary"),
        name="ada_modulation",
    )(cs, ada_w, ada_b.reshape(DEPTH, 1, 6 * D_MODEL))


def _mod_slice(mod_ref, k):
    return mod_ref[0, :, k * D_MODEL:(k + 1) * D_MODEL]


def _rms_mod(x, g, scale, shift):
    y = x * lax.rsqrt(jnp.mean(x * x, axis=-1, keepdims=True) + EPS)
    return (y * g) * (1.0 + scale) + shift


def _head_rms(t, gain, hm):
    sq = t * t
    hi = sq.astype(BF16)
    lo = (sq - hi.astype(F32)).astype(BF16)
    ms = _dot(hi, hm) + _dot(lo, hm)
    return t * lax.rsqrt(ms + EPS) * gain


def _rope(t, cos, sin_hi, sin_lo):
    up = pltpu.roll(t, LANES - ROPE_FREQS, axis=1)
    dn = pltpu.roll(t, ROPE_FREQS, axis=1)
    return t * cos + up * sin_hi + dn * sin_lo


def _in_even_kernel(rope, x_ref, mod_ref, g_ref, w_ref, qg_ref, kg_ref, hm_ref, wc_ref, *rest):
    if rope:
        cos_ref, sh_ref, sl_ref, y_ref, q_ref, k_ref, v_ref = rest
    else:
        y_ref, q_ref, k_ref, v_ref = rest
    h = _rms_mod(x_ref[0], g_ref[...], _mod_slice(mod_ref, 1), _mod_slice(mod_ref, 0)).astype(BF16)
    hm = hm_ref[...]
    F = FOURIER_WIDTH
    f = _dot(h, w_ref[:, :F]).astype(BF16)
    for g in range(FOURIER_GROUPS):
        yg = _dot(f[:, g * LANES:(g + 1) * LANES], wc_ref[...])
        y_ref[0, :, g * LANES:(g + 1) * LANES] = yg[:, :LANES].astype(y_ref.dtype)
        y_ref[0, :, F + g * LANES:F + (g + 1) * LANES] = yg[:, LANES:].astype(y_ref.dtype)

    def qk(t, gain):
        t = _head_rms(t, gain, hm)
        if rope:
            t = _rope(t, cos_ref[...], sh_ref[...], sl_ref[...])
        return t

    q = _dot(h, w_ref[:, F:F + QW])
    for j in range(QW // LANES):
        qj = qk(q[:, j * LANES:(j + 1) * LANES], qg_ref[...]) * (1.0 / math.sqrt(HEAD_DIM))
        q_ref[0, :, j * LANES:(j + 1) * LANES] = qj.astype(BF16)
    kv = _dot(h, w_ref[:, F + QW:])
    k_ref[0] = qk(kv[:, :KW], kg_ref[...]).astype(BF16)
    v_ref[0] = kv[:, KW:].astype(BF16)


def _in_even(x, mod, mod_row, g, w_in, q_g, k_g, hm, wc, rope_tabs, tm):
    B, L, _ = x.shape
    rope = rope_tabs is not None
    in_specs = [pl.BlockSpec((1, tm, D_MODEL), lambda b, t: (b, t, 0)),
                pl.BlockSpec((1, 1, 6 * D_MODEL), lambda b, t: (mod_row(b), 0, 0)),
                _const_spec((1, D_MODEL)),
                _const_spec((D_MODEL, EV_IN_WIDTH)),
                _const_spec((1, LANES)), _const_spec((1, LANES)),
                _const_spec((LANES, LANES)), _const_spec((LANES, 2 * LANES))]
    args = [x, mod, g, w_in, q_g, k_g, hm, wc]
    if rope:
        in_specs += [pl.BlockSpec((tm, LANES), lambda b, t: (t, 0))] * 3
        args += list(rope_tabs)
    return pl.pallas_call(
        functools.partial(_in_even_kernel, rope),
        grid=(B, L // tm),
        in_specs=in_specs,
        out_specs=[pl.BlockSpec((1, tm, 2 * FOURIER_WIDTH), lambda b, t: (b, t, 0)),
                   pl.BlockSpec((1, tm, QW), lambda b, t: (b, t, 0)),
                   pl.BlockSpec((1, tm, KW), lambda b, t: (b, t, 0)),
                   pl.BlockSpec((1, tm, KW), lambda b, t: (b, t, 0))],
        out_shape=[jax.ShapeDtypeStruct((B, L, 2 * FOURIER_WIDTH), BF16),
                   jax.ShapeDtypeStruct((B, L, QW), BF16),
                   jax.ShapeDtypeStruct((B, L, KW), BF16),
                   jax.ShapeDtypeStruct((B, L, KW), BF16)],
        compiler_params=_params("arbitrary", "arbitrary"),
        name="in_even_rope" if rope else "in_even_ctx",
    )(*args)


def _fft_stage1_kernel(n2t, y_ref, m1_ref, twc_ref, tws_ref, o_ref):
    F = FOURIER_WIDTH
    for j in range(n2t):
        a = _dot(m1_ref[...], y_ref[0, :, j * 2 * F:(j + 1) * 2 * F])
        top, bot = a[:FFT_N], a[FFT_N:]
        ar = top[:, :F] - bot[:, F:]
        ai = top[:, F:] + bot[:, :F]
        tc = jnp.tile(twc_ref[j], (1, F // LANES))
        ts = jnp.tile(tws_ref[j], (1, F // LANES))
        o_ref[0, 0, :, j * F:(j + 1) * F] = (ar * tc + ai * ts).astype(BF16)
        o_ref[0, 1, :, j * F:(j + 1) * F] = (ai * tc - ar * ts).astype(BF16)


def _fft_stage2_kernel(k1t, b_ref, m3_ref, o_ref):
    for j in range(k1t):
        rhs = jnp.concatenate([b_ref[0, 0, j], b_ref[0, 1, j]], axis=0)
        o_ref[0, j] = _dot(m3_ref[...], rhs)


def _fourier_latent(y, m1, m3, twc, tws):
    B = y.shape[0]
    F = FOURIER_WIDTH
    n2t = 8
    yv = y.reshape(B, FFT_N, FFT_N * 2 * F)
    bh = pl.pallas_call(
        functools.partial(_fft_stage1_kernel, n2t),
        grid=(B, FFT_N // n2t),
        in_specs=[pl.BlockSpec((1, FFT_N, n2t * 2 * F), lambda b, t: (b, 0, t)),
                  _const_spec((2 * FFT_N, FFT_N)),
                  pl.BlockSpec((n2t, FFT_N, LANES), lambda b, t: (t, 0, 0)),
                  pl.BlockSpec((n2t, FFT_N, LANES), lambda b, t: (t, 0, 0))],
        out_specs=pl.BlockSpec((1, 2, FFT_N, n2t * F), lambda b, t: (b, 0, 0, t)),
        out_shape=jax.ShapeDtypeStruct((B, 2, FFT_N, FFT_N * F), BF16),
        compiler_params=_params("arbitrary", "arbitrary"),
        name="fft_stage1",
    )(yv, m1, twc, tws)
    k1t = 8
    bv = bh.reshape(B, 2, FFT_N, FFT_N, F)
    return pl.pallas_call(
        functools.partial(_fft_stage2_kernel, k1t),
        grid=(B, FFT_N // k1t),
        in_specs=[pl.BlockSpec((1, 2, k1t, FFT_N, F), lambda b, t: (b, 0, t, 0, 0)),
                  _const_spec((FFT_N, 2 * FFT_N))],
        out_specs=pl.BlockSpec((1, k1t, FFT_N, F), lambda b, t: (b, t, 0, 0)),
        out_shape=jax.ShapeDtypeStruct((B, FFT_N, FFT_N, F), F32),
        compiler_params=_params("arbitrary", "arbitrary"),
        name="fft_stage2",
    )(bv, m3)


def _fourier_ctx_kernel(y_ref, m_ref, o_ref):
    F = FOURIER_WIDTH
    rhs = jnp.concatenate([y_ref[0, :, :F], y_ref[0, :, F:]], axis=0)
    o_ref[0] = _dot(m_ref[...], rhs)


def _fourier_ctx(y, mctx):
    B = y.shape[0]
    return pl.pallas_call(
        _fourier_ctx_kernel,
        grid=(B,),
        in_specs=[pl.BlockSpec((1, CTX_LEN, 2 * FOURIER_WIDTH), lambda b: (b, 0, 0)),
                  _const_spec((CTX_LEN, 2 * CTX_LEN))],
        out_specs=pl.BlockSpec((1, CTX_LEN, FOURIER_WIDTH), lambda b: (b, 0, 0)),
        out_shape=jax.ShapeDtypeStruct((B, CTX_LEN, FOURIER_WIDTH), F32),
        compiler_params=_params("arbitrary"),
        name="fourier_ctx",
    )(y, mctx)


def _gqa_heads(q, keys, vals, mask, sink_ref, rows):
    outs = []
    for kv in range(WIN_KV_HEADS):
        heads = range(kv * WIN_GROUP, (kv + 1) * WIN_GROUP)
        qs = jnp.concatenate([q[:, h * HEAD_DIM:(h + 1) * HEAD_DIM] for h in heads], axis=0)
        s = _dot_nt(qs, keys[:, kv * HEAD_DIM:(kv + 1) * HEAD_DIM])
        if mask is not None:
            s = jnp.where(jnp.concatenate([mask] * WIN_GROUP, axis=0), s, NEG)
        sk = jnp.concatenate([jnp.full((rows, 1), sink_ref[h], F32) for h in heads], axis=0)
        m = jnp.maximum(jnp.max(s, axis=-1, keepdims=True), sk)
        e = jnp.exp(s - m)
        den = jnp.sum(e, axis=-1, keepdims=True) + jnp.exp(sk - m)
        o = _dot(e.astype(BF16), vals[:, kv * HEAD_DIM:(kv + 1) * HEAD_DIM]) / den
        outs += [o[g * rows:(g + 1) * rows] for g in range(WIN_GROUP)]
    return jnp.concatenate(outs, axis=1)


def _win_attn_kernel(sink_ref, q_ref, kp_ref, kc_ref, kn_ref, vp_ref, vc_ref, vn_ref, kx_ref, vx_ref, o_ref):
    n = pl.program_id(1)
    nw = 3 * WIN_BLOCK
    keys = jnp.concatenate([kp_ref[0], kc_ref[0], kn_ref[0], kx_ref[0]], axis=0)
    vals = jnp.concatenate([vp_ref[0], vc_ref[0], vn_ref[0], vx_ref[0]], axis=0)
    i = lax.broadcasted_iota(jnp.int32, (WIN_BLOCK, nw + CTX_LEN), 0)
    j = lax.broadcasted_iota(jnp.int32, (WIN_BLOCK, nw + CTX_LEN), 1)
    kpos = n * WIN_BLOCK - WIN_BLOCK + j
    in_win = (jnp.abs(j - WIN_BLOCK - i) <= WIN_RADIUS) & (kpos >= 0) & (kpos < SEQ)
    mask = in_win | (j >= nw)
    o_ref[0] = _gqa_heads(q_ref[0], keys, vals, mask, sink_ref, WIN_BLOCK).astype(BF16)


def _win_attention(q, k, v, kx, vx, sink):
    B, L, _ = q.shape
    nb = L // WIN_BLOCK
    blk = lambda f: pl.BlockSpec((1, WIN_BLOCK, KW), f)
    prev = lambda b, n: (b, jnp.maximum(n - 1, 0), 0)
    cur = lambda b, n: (b, n, 0)
    nxt = lambda b, n: (b, jnp.minimum(n + 1, nb - 1), 0)
    ctx_spec = pl.BlockSpec((1, CTX_LEN, KW), lambda b, n: (b, 0, 0))
    return pl.pallas_call(
        _win_attn_kernel,
        grid=(B, nb),
        in_specs=[pl.BlockSpec(memory_space=pltpu.SMEM),
                  pl.BlockSpec((1, WIN_BLOCK, QW), cur),
                  blk(prev), blk(cur), blk(nxt), blk(prev), blk(cur), blk(nxt), ctx_spec, ctx_spec],
        out_specs=pl.BlockSpec((1, WIN_BLOCK, QW), cur),
        out_shape=jax.ShapeDtypeStruct((B, L, QW), BF16),
        compiler_params=_params("arbitrary", "arbitrary"),
        name="window_attention",
    )(sink, q, k, k, k, v, v, v, kx, vx)


def _ctx_attn_even_kernel(sink_ref, q_ref, k_ref, v_ref, o_ref):
    o_ref[0] = _gqa_heads(q_ref[0], k_ref[0], v_ref[0], None, sink_ref, CTX_LEN).astype(BF16)


def _ctx_attention_even(q, k, v, sink):
    B = q.shape[0]
    return pl.pallas_call(
        _ctx_attn_even_kernel,
        grid=(B,),
        in_specs=[pl.BlockSpec(memory_space=pltpu.SMEM),
                  pl.BlockSpec((1, CTX_LEN, QW), lambda b: (b, 0, 0)),
                  pl.BlockSpec((1, CTX_LEN, KW), lambda b: (b, 0, 0)),
                  pl.BlockSpec((1, CTX_LEN, KW), lambda b: (b, 0, 0))],
        out_specs=pl.BlockSpec((1, CTX_LEN, QW), lambda b: (b, 0, 0)),
        out_shape=jax.ShapeDtypeStruct((B, CTX_LEN, QW), BF16),
        compiler_params=_params("arbitrary"),
        name="ctx_attention_even",
    )(sink, q, k, v)


def _out_ffn_kernel(mode, *refs):
    if mode == "even_latent":
        f_ref, a_ref, x_ref, mod_ref, g_ref, wo_ref, wg_ref, wu_ref, wd_ref, o_ref = refs
        fm = jnp.concatenate([f_ref[0, :, j, :] for j in range(f_ref.shape[2])], axis=0).astype(BF16)
        o = _dot(fm, wo_ref[:FOURIER_WIDTH]) + _dot(a_ref[0], wo_ref[FOURIER_WIDTH:])
    elif mode == "even_ctx":
        f_ref, a_ref, x_ref, mod_ref, g_ref, wo_ref, wg_ref, wu_ref, wd_ref, o_ref = refs
        o = _dot(f_ref[0].astype(BF16), wo_ref[:FOURIER_WIDTH]) + _dot(a_ref[0], wo_ref[FOURIER_WIDTH:])
    else:
        a_ref, x_ref, mod_ref, g_ref, wo_ref, wg_ref, wu_ref, wd_ref, o_ref = refs
        o = _dot(a_ref[0], wo_ref[...])
    x1 = x_ref[0] + _mod_slice(mod_ref, 2) * o
    h = _rms_mod(x1, g_ref[...], _mod_slice(mod_ref, 4), _mod_slice(mod_ref, 3)).astype(BF16)
    acc = jnp.zeros_like(x1)
    for c in range(D_FF // FF_CHUNK):
        cs = slice(c * FF_CHUNK, (c + 1) * FF_CHUNK)
        a = _silu(_dot(h, wg_ref[:, cs])) * _dot(h, wu_ref[:, cs])
        acc = acc + _dot(a.astype(BF16), wd_ref[cs, :])
    o_ref[0] = x1 + _mod_slice(mod_ref, 5) * acc


def _out_ffn(mode, mix, x, mod, mod_row, g, w_out, wg, wu, wd, tm):
    B, L, _ = x.shape
    tok = lambda w: pl.BlockSpec((1, tm, w), lambda b, t: (b, t, 0))
    if mode == "even_latent":
        assert tm % FFT_N == 0
        mix_specs = [pl.BlockSpec((1, FFT_N, tm // FFT_N, FOURIER_WIDTH), lambda b, t: (b, 0, t, 0)), tok(QW)]
    elif mode == "even_ctx":
        mix_specs = [tok(FOURIER_WIDTH), tok(QW)]
    else:
        mix_specs = [tok(NA_WIDTH)]
    return pl.pallas_call(
        functools.partial(_out_ffn_kernel, mode),
        grid=(B, L // tm),
        in_specs=mix_specs + [tok(D_MODEL),
                              pl.BlockSpec((1, 1, 6 * D_MODEL), lambda b, t: (mod_row(b), 0, 0)),
                              _const_spec((1, D_MODEL)),
                              _const_spec((D_MODEL, D_MODEL)),
                              _const_spec((D_MODEL, D_FF)), _const_spec((D_MODEL, D_FF)),
                              _const_spec((D_FF, D_MODEL))],
        out_specs=tok(D_MODEL),
        out_shape=jax.ShapeDtypeStruct((B, L, D_MODEL), F32),
        compiler_params=_params("arbitrary", "arbitrary"),
        name="out_ffn_" + mode,
    )(*mix, x, mod, g, w_out, wg, wu, wd)


def _in_odd_kernel(with_q, x_ref, mod_ref, g_ref, w_ref, qg_ref, kg_ref, hm_ref, *outs):
    h = _rms_mod(x_ref[0], g_ref[...], _mod_slice(mod_ref, 1), _mod_slice(mod_ref, 0)).astype(BF16)
    hm = hm_ref[...]
    W = NA_WIDTH
    if with_q:
        q_ref, k_ref, v_ref = outs
        q = _dot(h, w_ref[:, :W])
        for j in range(W // LANES):
            qj = _head_rms(q[:, j * LANES:(j + 1) * LANES], qg_ref[...], hm) * (1.0 / math.sqrt(HEAD_DIM))
            q_ref[0, :, j * LANES:(j + 1) * LANES] = qj.astype(BF16)
    else:
        k_ref, v_ref = outs
    k = _dot(h, w_ref[:, W:2 * W])
    for j in range(W // LANES):
        kj = _head_rms(k[:, j * LANES:(j + 1) * LANES], kg_ref[...], hm)
        k_ref[0, :, j * LANES:(j + 1) * LANES] = kj.astype(BF16)
    v_ref[0] = _dot(h, w_ref[:, 2 * W:]).astype(BF16)


def _in_odd(x, mod, mod_row, g, w_in, q_g, k_g, hm, with_q, tm):
    B, L, _ = x.shape
    tok = pl.BlockSpec((1, tm, NA_WIDTH), lambda b, t: (b, t, 0))
    n_out = 3 if with_q else 2
    return pl.pallas_call(
        functools.partial(_in_odd_kernel, with_q),
        grid=(B, L // tm),
        in_specs=[pl.BlockSpec((1, tm, D_MODEL), lambda b, t: (b, t, 0)),
                  pl.BlockSpec((1, 1, 6 * D_MODEL), lambda b, t: (mod_row(b), 0, 0)),
                  _const_spec((1, D_MODEL)),
                  _const_spec((D_MODEL, 3 * NA_WIDTH)),
                  _const_spec((1, LANES)), _const_spec((1, LANES)), _const_spec((LANES, LANES))],
        out_specs=[tok] * n_out,
        out_shape=[jax.ShapeDtypeStruct((B, L, NA_WIDTH), BF16)] * n_out,
        compiler_params=_params("arbitrary", "arbitrary"),
        name="in_odd" if with_q else "in_odd_ctx",
    )(x, mod, g, w_in, q_g, k_g, hm)


def _bias_table_kernel(rb_ref, oh_ref, o_ref):
    x = rb_ref[...]
    hi = x.astype(BF16)
    r1 = x - hi.astype(F32)
    mid = r1.astype(BF16)
    lo = (r1 - mid.astype(F32)).astype(BF16)
    oh = oh_ref[...]
    o_ref[...] = (_dot(hi, oh) + _dot(mid, oh)) + _dot(lo, oh)


def _na_bias_table(rel_bias, onehot):
    H = NA_HEADS
    ndr = 2 * NA_KH - 1
    rows = H * ndr
    rb = jnp.pad(rel_bias.reshape(rows, 2 * NA_KW - 1), ((0, 0), (0, LANES - (2 * NA_KW - 1))))
    flat = pl.pallas_call(
        _bias_table_kernel,
        out_shape=jax.ShapeDtypeStruct((rows, GRID_W * GRID_W), F32),
        compiler_params=pltpu.CompilerParams(vmem_limit_bytes=VMEM_LIMIT),
        name="na_bias_table",
    )(rb, onehot)
    t = flat.reshape(H, ndr, GRID_W, GRID_W)
    return jnp.concatenate([t[:, :-1], t[:, 1:]], axis=-1)


def _na_kernel(q_ref, k_ref, v_ref, kx_ref, vx_ref, bias_ref, o_ref):
    r = pl.program_id(1)
    r0 = jnp.clip(r - NA_KH // 2, 0, GRID_ROWS - NA_KH)
    dr0 = r0 - r + NA_KH - 1
    n = NA_KH * GRID_W
    start = pl.multiple_of(r0 * GRID_W, GRID_W)
    k_rows = k_ref[0, pl.ds(start, n), :]
    v_rows = v_ref[0, pl.ds(start, n), :]
    cq = lax.broadcasted_iota(jnp.int32, (GRID_W, n), 0)
    ck = lax.broadcasted_iota(jnp.int32, (GRID_W, n), 1) % GRID_W
    c0 = jnp.clip(cq - NA_KW // 2, 0, GRID_W - NA_KW)
    mask = (ck >= c0) & (ck < c0 + NA_KW)
    q = q_ref[0]
    outs = []
    for h in range(NA_HEADS):
        hs = slice(h * HEAD_DIM, (h + 1) * HEAD_DIM)
        bias = jnp.concatenate([bias_ref[h, dr0 + 2 * p] for p in range(NA_KH // 2)], axis=1)
        s_nb = jnp.where(mask, _dot_nt(q[:, hs], k_rows[:, hs]) + bias, NEG)
        s_cx = _dot_nt(q[:, hs], kx_ref[0, :, hs])
        m = jnp.maximum(jnp.max(s_nb, axis=-1, keepdims=True), jnp.max(s_cx, axis=-1, keepdims=True))
        e_nb = jnp.exp(s_nb - m)
        e_cx = jnp.exp(s_cx - m)
        den = jnp.sum(e_nb, axis=-1, keepdims=True) + jnp.sum(e_cx, axis=-1, keepdims=True)
        o = _dot(e_nb.astype(BF16), v_rows[:, hs]) + _dot(e_cx.astype(BF16), vx_ref[0, :, hs])
        outs.append(o / den)
    o_ref[0] = jnp.concatenate(outs, axis=1).astype(BF16)


def _neighbourhood_attention(q, k, v, kx, vx, bias_tab):
    B, L, W = q.shape
    full = pl.BlockSpec((1, L, W), lambda b, r: (b, 0, 0))
    ctx_spec = pl.BlockSpec((1, CTX_LEN, W), lambda b, r: (b, 0, 0))
    row = pl.BlockSpec((1, GRID_W, W), lambda b, r: (b, r, 0))
    return pl.pallas_call(
        _na_kernel,
        grid=(B, GRID_ROWS),
        in_specs=[row, full, full, ctx_spec, ctx_spec, _const_spec(bias_tab.shape)],
        out_specs=row,
        out_shape=jax.ShapeDtypeStruct((B, L, W), BF16),
        compiler_params=_params("arbitrary", "arbitrary"),
        name="neighbourhood_attention",
    )(q, k, v, kx, vx, bias_tab)


def kernel(x, c, ctx, c_ctx, ada_w, ada_b, norm1_g, norm2_g, ffn_w_gate, ffn_w_up, ffn_w_down,
           ev_w_in, ev_w_out, ev_q_norm, ev_k_norm, ev_sink,
           od_w_in, od_w_out, od_q_norm, od_k_norm, od_rel_bias):
    assert x.shape == (BATCH, SEQ, D_MODEL) and ctx.shape == (BATCH, CTX_LEN, D_MODEL)
    wc, m1, m3, twc, tws, mctx = _fourier_tables()
    hm = _head_mean_matrix()
    rope_tabs = _rope_tables()
    lane_gain = lambda gvec: jnp.tile(gvec, LANES // HEAD_DIM).reshape(1, LANES)
    lat_row = lambda b: b
    ctx_row = lambda b: CTX_MOD_ROW
    tm = 512

    cs = jnp.concatenate([c, c_ctx[None, :], jnp.zeros((MOD_ROWS - BATCH - 1, D_MODEL), F32)], axis=0)
    mod = _modulation(cs, ada_w, ada_b).reshape(DEPTH, MOD_ROWS, 1, 6 * D_MODEL)

    w_in0 = ev_w_in[0].astype(BF16)
    w_out0 = ev_w_out[0].astype(BF16)
    g1 = norm1_g[0].reshape(1, D_MODEL)
    g2 = norm2_g[0].reshape(1, D_MODEL)
    ffn0 = (ffn_w_gate[0].astype(BF16), ffn_w_up[0].astype(BF16), ffn_w_down[0].astype(BF16))
    qg, kg = lane_gain(ev_q_norm[0]), lane_gain(ev_k_norm[0])
    y_l, q_l, k_l, v_l = _in_even(x, mod[0], lat_row, g1, w_in0, qg, kg, hm, wc, rope_tabs, tm)
    y_c, q_c, k_c, v_c = _in_even(ctx, mod[0], ctx_row, g1, w_in0, qg, kg, hm, wc, None, CTX_LEN)
    f_l = _fourier_latent(y_l, m1, m3, twc, tws)
    f_c = _fourier_ctx(y_c, mctx)
    a_l = _win_attention(q_l, k_l, v_l, k_c, v_c, ev_sink[0])
    a_c = _ctx_attention_even(q_c, k_c, v_c, ev_sink[0])
    x1 = _out_ffn("even_latent", (f_l, a_l), x, mod[0], lat_row, g2, w_out0, *ffn0, tm)
    y1 = _out_ffn("even_ctx", (f_c, a_c), ctx, mod[0], ctx_row, g2, w_out0, *ffn0, CTX_LEN)

    w_in1 = od_w_in[0].astype(BF16)
    w_out1 = od_w_out[0].astype(BF16)
    g1 = norm1_g[1].reshape(1, D_MODEL)
    g2 = norm2_g[1].reshape(1, D_MODEL)
    ffn1 = (ffn_w_gate[1].astype(BF16), ffn_w_up[1].astype(BF16), ffn_w_down[1].astype(BF16))
    qg, kg = lane_gain(od_q_norm[0]), lane_gain(od_k_norm[0])
    q_l, k_l, v_l = _in_odd(x1, mod[1], lat_row, g1, w_in1, qg, kg, hm, True, tm)
    k_c, v_c = _in_odd(y1, mod[1], ctx_row, g1, w_in1, qg, kg, hm, False, CTX_LEN)
    bias_tab = _na_bias_table(od_rel_bias[0], _na_onehot())
    a_l = _neighbourhood_attention(q_l, k_l, v_l, k_c, v_c, bias_tab)
    return _out_ffn("odd", (a_l,), x1, mod[1], lat_row, g2, w_out1, *ffn1, tm)
```

```python
import functools
import math

import numpy as np
import jax
import jax.numpy as jnp
from jax import lax
from jax.experimental import pallas as pl
from jax.experimental.pallas import tpu as pltpu

D_MODEL = 1024
BATCH = 4
SEQ = 4096
DEPTH = 2
GRID_W = 64
CTX_LEN = 256
HEAD_DIM = 64
EPS = 1e-6
NEG = -1e30
FOURIER_WIDTH = D_MODEL // 2
FOURIER_GROUPS = 4
FOURIER_GROUP_CH = FOURIER_WIDTH // FOURIER_GROUPS
WIN_Q_HEADS = (D_MODEL // 2) // HEAD_DIM
WIN_KV_HEADS = 2
WIN_GROUP = WIN_Q_HEADS // WIN_KV_HEADS
WIN_RADIUS = 128
WIN_BLOCK = 128
QW = WIN_Q_HEADS * HEAD_DIM
KW = WIN_KV_HEADS * HEAD_DIM
KVD = 2 * KW
EV_IN_WIDTH = FOURIER_WIDTH + QW + 2 * KW
NA_HEADS = D_MODEL // HEAD_DIM
NA_KH = 8
NA_KW = 16
NA_WIDTH = NA_HEADS * HEAD_DIM
ROPE_THETA = 10000.0
ROPE_FREQS = HEAD_DIM // 4
D_FF = ((8 * D_MODEL // 3 + 255) // 256) * 256
GRID_ROWS = SEQ // GRID_W

LANES = 128
MOD_ROWS = 8
CTX_MOD_ROW = BATCH
FFT_N = 64
FF_CHUNK = 256
SOFTMAX_ROWS = 32
VMEM_LIMIT = 56 * 1024 * 1024

BF16 = jnp.bfloat16
F32 = jnp.float32

assert DEPTH == 2 and SEQ == FFT_N * FFT_N and D_FF % FF_CHUNK == 0


def _params(*sem):
    return pltpu.CompilerParams(dimension_semantics=sem, vmem_limit_bytes=VMEM_LIMIT)


def _dot(a, b):
    return jnp.dot(a, b, preferred_element_type=F32)


def _dot_nt(a, b):
    return lax.dot_general(a, b, (((1,), (1,)), ((), ())), preferred_element_type=F32)


def _silu(x):
    return x / (1.0 + jnp.exp(-x))


def _const_spec(shape):
    nd = len(shape)
    return pl.BlockSpec(shape, lambda *_: (0,) * nd, pipeline_mode=pl.Buffered(1))


def _dft_cos_sin(n):
    idx = (np.arange(n)[:, None] * np.arange(n)[None, :]) % n
    ang = 2.0 * np.pi * idx / n
    return np.cos(ang), np.sin(ang)


def _fourier_tables():
    cc, sc = _dft_cos_sin(FOURIER_GROUP_CH)
    wc = np.concatenate([cc, -sc], axis=1) / math.sqrt(FOURIER_GROUP_CH)
    c64, s64 = _dft_cos_sin(FFT_N)
    m1 = np.concatenate([c64, -s64], axis=0) / math.sqrt(FFT_N)
    m3 = np.concatenate([c64, s64], axis=1) / math.sqrt(FFT_N)
    tw = (np.arange(FFT_N)[:, None] * np.arange(FFT_N)[None, :]) % SEQ
    tw = 2.0 * np.pi * tw / SEQ
    twc = np.repeat(np.cos(tw)[:, :, None], LANES, axis=2)
    tws = np.repeat(np.sin(tw)[:, :, None], LANES, axis=2)
    cx, sx = _dft_cos_sin(CTX_LEN)
    mctx = np.concatenate([cx, sx], axis=1) / math.sqrt(CTX_LEN)
    as32 = lambda a: jnp.asarray(a, F32)
    return (as32(wc).astype(BF16), as32(m1).astype(BF16), as32(m3).astype(BF16),
            as32(twc), as32(tws), as32(mctx).astype(BF16))


def _head_mean_matrix():
    blk = np.kron(np.eye(LANES // HEAD_DIM), np.ones((HEAD_DIM, HEAD_DIM))) / HEAD_DIM
    return jnp.asarray(blk, BF16)


def _rope_tables():
    t = jnp.arange(SEQ, dtype=jnp.int32)
    row = (t // GRID_W).astype(F32)
    col = (t % GRID_W).astype(F32)
    inv = ROPE_THETA ** (-jnp.arange(ROPE_FREQS, dtype=F32) / ROPE_FREQS)
    ang_row = row[:, None] * inv[None, :]
    ang_col = col[:, None] * inv[None, :]
    zero = jnp.zeros_like(ang_row)
    cos = jnp.concatenate([jnp.cos(ang_row)] * 2 + [jnp.cos(ang_col)] * 2, axis=1)
    sin_hi = jnp.concatenate([-jnp.sin(ang_row), zero, -jnp.sin(ang_col), zero], axis=1)
    sin_lo = jnp.concatenate([zero, jnp.sin(ang_row), zero, jnp.sin(ang_col)], axis=1)
    rep = LANES // HEAD_DIM
    return jnp.tile(cos, (1, rep)), jnp.tile(sin_hi, (1, rep)), jnp.tile(sin_lo, (1, rep))


def _na_onehot():
    cq = np.arange(GRID_W)
    dc = np.clip(cq[None, :] - cq[:, None] + NA_KW - 1, 0, 2 * NA_KW - 2)
    oh = np.zeros((LANES, GRID_W * GRID_W), np.float32)
    oh[dc.reshape(-1), np.arange(GRID_W * GRID_W)] = 1.0
    return jnp.asarray(oh, BF16)


def _mod_kernel(cs_ref, w_ref, b_ref, o_ref):
    s = _silu(cs_ref[...]).astype(BF16)
    o_ref[0] = _dot(s, w_ref[0].astype(BF16)) + b_ref[0]


def _modulation(cs, ada_w, ada_b):
    tn = 1536
    return pl.pallas_call(
        _mod_kernel,
        grid=(DEPTH, 6 * D_MODEL // tn),
        in_specs=[pl.BlockSpec((MOD_ROWS, D_MODEL), lambda i, j: (0, 0)),
                  pl.BlockSpec((1, D_MODEL, tn), lambda i, j: (i, 0, j)),
                  pl.BlockSpec((1, 1, tn), lambda i, j: (i, 0, j))],
        out_specs=pl.BlockSpec((1, MOD_ROWS, tn), lambda i, j: (i, 0, j)),
        out_shape=jax.ShapeDtypeStruct((DEPTH, MOD_ROWS, 6 * D_MODEL), F32),
        compiler_params=_params("arbitrary", "arbitrary"),
        name="ada_modulation",
    )(cs, ada_w, ada_b.reshape(DEPTH, 1, 6 * D_MODEL))


def _mod_slice(mod_ref, k):
    return mod_ref[0, :, k * D_MODEL:(k + 1) * D_MODEL]


def _rms_mod(x, g, scale, shift):
    y = x * lax.rsqrt(jnp.mean(x * x, axis=-1, keepdims=True) + EPS)
    return (y * g) * (1.0 + scale) + shift


def _head_rms(t, gain, hm):
    sq = t * t
    hi = sq.astype(BF16)
    lo = (sq - hi.astype(F32)).astype(BF16)
    ms = _dot(hi, hm) + _dot(lo, hm)
    return t * lax.rsqrt(ms + EPS) * gain


def _rope(t, cos, sin_hi, sin_lo):
    up = pltpu.roll(t, LANES - ROPE_FREQS, axis=1)
    dn = pltpu.roll(t, ROPE_FREQS, axis=1)
    return t * cos + up * sin_hi + dn * sin_lo


def _in_even_kernel(rope, x_ref, mod_ref, g_ref, w_ref, qg_ref, kg_ref, hm_ref, wc_ref, *rest):
    if rope:
        cos_ref, sh_ref, sl_ref, y_ref, q_ref, k_ref, v_ref = rest
    else:
        y_ref, q_ref, k_ref, v_ref = rest
    h = _rms_mod(x_ref[0], g_ref[...], _mod_slice(mod_ref, 1), _mod_slice(mod_ref, 0)).astype(BF16)
    hm = hm_ref[...]
    F = FOURIER_WIDTH
    f = _dot(h, w_ref[:, :F]).astype(BF16)
    for g in range(FOURIER_GROUPS):
        yg = _dot(f[:, g * LANES:(g + 1) * LANES], wc_ref[...])
        y_ref[0, :, g * LANES:(g + 1) * LANES] = yg[:, :LANES].astype(y_ref.dtype)
        y_ref[0, :, F + g * LANES:F + (g + 1) * LANES] = yg[:, LANES:].astype(y_ref.dtype)

    def qk(t, gain):
        t = _head_rms(t, gain, hm)
        if rope:
            t = _rope(t, cos_ref[...], sh_ref[...], sl_ref[...])
        return t

    q = _dot(h, w_ref[:, F:F + QW])
    for j in range(QW // LANES):
        qj = qk(q[:, j * LANES:(j + 1) * LANES], qg_ref[...]) * (1.0 / math.sqrt(HEAD_DIM))
        q_ref[0, :, j * LANES:(j + 1) * LANES] = qj.astype(BF16)
    kv = _dot(h, w_ref[:, F + QW:])
    lo_lanes = lax.broadcasted_iota(jnp.int32, (1, LANES), 1) < HEAD_DIM

    def dup_heads(t):
        sw = pltpu.roll(t, HEAD_DIM, axis=1)
        return jnp.concatenate([jnp.where(lo_lanes, t, sw), jnp.where(lo_lanes, sw, t)], axis=1)

    k_ref[0] = dup_heads(qk(kv[:, :KW], kg_ref[...])).astype(BF16)
    v_ref[0] = dup_heads(kv[:, KW:]).astype(BF16)


def _in_even(x, mod, mod_row, g, w_in, q_g, k_g, hm, wc, rope_tabs, tm):
    B, L, _ = x.shape
    rope = rope_tabs is not None
    in_specs = [pl.BlockSpec((1, tm, D_MODEL), lambda b, t: (b, t, 0)),
                pl.BlockSpec((1, 1, 6 * D_MODEL), lambda b, t: (mod_row(b), 0, 0)),
                _const_spec((1, D_MODEL)),
                _const_spec((D_MODEL, EV_IN_WIDTH)),
                _const_spec((1, LANES)), _const_spec((1, LANES)),
                _const_spec((LANES, LANES)), _const_spec((LANES, 2 * LANES))]
    args = [x, mod, g, w_in, q_g, k_g, hm, wc]
    if rope:
        in_specs += [pl.BlockSpec((tm, LANES), lambda b, t: (t, 0))] * 3
        args += list(rope_tabs)
    return pl.pallas_call(
        functools.partial(_in_even_kernel, rope),
        grid=(B, L // tm),
        in_specs=in_specs,
        out_specs=[pl.BlockSpec((1, tm, 2 * FOURIER_WIDTH), lambda b, t: (b, t, 0)),
                   pl.BlockSpec((1, tm, QW), lambda b, t: (b, t, 0)),
                   pl.BlockSpec((1, tm, KVD), lambda b, t: (b, t, 0)),
                   pl.BlockSpec((1, tm, KVD), lambda b, t: (b, t, 0))],
        out_shape=[jax.ShapeDtypeStruct((B, L, 2 * FOURIER_WIDTH), BF16),
                   jax.ShapeDtypeStruct((B, L, QW), BF16),
                   jax.ShapeDtypeStruct((B, L, KVD), BF16),
                   jax.ShapeDtypeStruct((B, L, KVD), BF16)],
        compiler_params=_params("arbitrary", "arbitrary"),
        name="in_even_rope" if rope else "in_even_ctx",
    )(*args)


def _fft_stage1_kernel(n2t, y_ref, m1_ref, twc_ref, tws_ref, o_ref):
    F = FOURIER_WIDTH
    for j in range(n2t):
        a = _dot(m1_ref[...], y_ref[0, :, j * 2 * F:(j + 1) * 2 * F])
        top, bot = a[:FFT_N], a[FFT_N:]
        ar = top[:, :F] - bot[:, F:]
        ai = top[:, F:] + bot[:, :F]
        tc = jnp.tile(twc_ref[j], (1, F // LANES))
        ts = jnp.tile(tws_ref[j], (1, F // LANES))
        o_ref[0, 0, :, j * F:(j + 1) * F] = (ar * tc + ai * ts).astype(BF16)
        o_ref[0, 1, :, j * F:(j + 1) * F] = (ai * tc - ar * ts).astype(BF16)


def _fft_stage2_kernel(k1t, b_ref, m3_ref, o_ref):
    for j in range(k1t):
        rhs = jnp.concatenate([b_ref[0, 0, j], b_ref[0, 1, j]], axis=0)
        o_ref[0, j] = _dot(m3_ref[...], rhs)


def _fourier_latent(y, m1, m3, twc, tws):
    B = y.shape[0]
    F = FOURIER_WIDTH
    n2t = 8
    yv = y.reshape(B, FFT_N, FFT_N * 2 * F)
    bh = pl.pallas_call(
        functools.partial(_fft_stage1_kernel, n2t),
        grid=(B, FFT_N // n2t),
        in_specs=[pl.BlockSpec((1, FFT_N, n2t * 2 * F), lambda b, t: (b, 0, t)),
                  _const_spec((2 * FFT_N, FFT_N)),
                  pl.BlockSpec((n2t, FFT_N, LANES), lambda b, t: (t, 0, 0)),
                  pl.BlockSpec((n2t, FFT_N, LANES), lambda b, t: (t, 0, 0))],
        out_specs=pl.BlockSpec((1, 2, FFT_N, n2t * F), lambda b, t: (b, 0, 0, t)),
        out_shape=jax.ShapeDtypeStruct((B, 2, FFT_N, FFT_N * F), BF16),
        compiler_params=_params("arbitrary", "arbitrary"),
        name="fft_stage1",
    )(yv, m1, twc, tws)
    k1t = 8
    bv = bh.reshape(B, 2, FFT_N, FFT_N, F)
    return pl.pallas_call(
        functools.partial(_fft_stage2_kernel, k1t),
        grid=(B, FFT_N // k1t),
        in_specs=[pl.BlockSpec((1, 2, k1t, FFT_N, F), lambda b, t: (b, 0, t, 0, 0)),
                  _const_spec((FFT_N, 2 * FFT_N))],
        out_specs=pl.BlockSpec((1, k1t, FFT_N, F), lambda b, t: (b, t, 0, 0)),
        out_shape=jax.ShapeDtypeStruct((B, FFT_N, FFT_N, F), F32),
        compiler_params=_params("arbitrary", "arbitrary"),
        name="fft_stage2",
    )(bv, m3)


def _fourier_ctx_kernel(y_ref, m_ref, o_ref):
    F = FOURIER_WIDTH
    rhs = jnp.concatenate([y_ref[0, :, :F], y_ref[0, :, F:]], axis=0)
    o_ref[0] = _dot(m_ref[...], rhs)


def _fourier_ctx(y, mctx):
    B = y.shape[0]
    return pl.pallas_call(
        _fourier_ctx_kernel,
        grid=(B,),
        in_specs=[pl.BlockSpec((1, CTX_LEN, 2 * FOURIER_WIDTH), lambda b: (b, 0, 0)),
                  _const_spec((CTX_LEN, 2 * CTX_LEN))],
        out_specs=pl.BlockSpec((1, CTX_LEN, FOURIER_WIDTH), lambda b: (b, 0, 0)),
        out_shape=jax.ShapeDtypeStruct((B, CTX_LEN, FOURIER_WIDTH), F32),
        compiler_params=_params("arbitrary"),
        name="fourier_ctx",
    )(y, mctx)


def _gqa_group(q_slabs, keys, vals, masks, sinks, rows, o_ref, first_slab, scratch):
    s_ref, p_ref, inv_ref = scratch
    lo_lanes = lax.broadcasted_iota(jnp.int32, (rows, LANES), 1) < HEAD_DIM
    zero = jnp.zeros((rows, LANES), BF16)
    qs = jnp.concatenate([jnp.where(lo_lanes if half == 0 else ~lo_lanes, qslab, zero)
                          for qslab in q_slabs for half in range(2)], axis=0)
    s_ref[...] = _dot_nt(qs, keys)
    for c in range(WIN_GROUP * rows // SOFTMAX_ROWS):
        rs = slice(c * SOFTMAX_ROWS, (c + 1) * SOFTMAX_ROWS)
        s = s_ref[rs, :]
        if masks is not None:
            mask, w = masks
            r0 = (c * SOFTMAX_ROWS) % rows
            s = jnp.concatenate([jnp.where(mask[r0:r0 + SOFTMAX_ROWS], s[:, :w], NEG), s[:, w:]], axis=1)
        sk = sinks[(c * SOFTMAX_ROWS) // rows]
        m = jnp.maximum(jnp.max(s, axis=-1, keepdims=True), sk)
        e = jnp.exp(s - m)
        inv_ref[rs, :] = 1.0 / (jnp.sum(e, axis=-1, keepdims=True) + jnp.exp(sk - m))
        p_ref[rs, :] = e.astype(BF16)
    o = _dot(p_ref[...], vals) * inv_ref[...]
    for i in range(WIN_GROUP // 2):
        a = o[(2 * i) * rows:(2 * i + 1) * rows]
        b = o[(2 * i + 1) * rows:(2 * i + 2) * rows]
        sl = slice((first_slab + i) * LANES, (first_slab + i + 1) * LANES)
        o_ref[0, :, sl] = jnp.where(lo_lanes, a, b).astype(BF16)


def _gqa_scratch(rows, n_keys):
    return [pltpu.VMEM((WIN_GROUP * rows, n_keys), F32), pltpu.VMEM((WIN_GROUP * rows, n_keys), BF16),
            pltpu.VMEM((WIN_GROUP * rows, 1), F32)]


def _gqa_all(q_ref, keys, vals, masks, sink_ref, rows, o_ref, scratch):
    slabs_per_group = WIN_GROUP // 2
    for kv in range(WIN_KV_HEADS):
        q_slabs = [q_ref[0, :, (kv * slabs_per_group + i) * LANES:(kv * slabs_per_group + i + 1) * LANES]
                   for i in range(slabs_per_group)]
        sinks = [sink_ref[kv * WIN_GROUP + g] for g in range(WIN_GROUP)]
        ksl = slice(kv * LANES, (kv + 1) * LANES)
        _gqa_group(q_slabs, keys[:, ksl], vals[:, ksl], masks, sinks, rows, o_ref, kv * slabs_per_group, scratch)


def _window_band():
    i = np.arange(WIN_BLOCK)[:, None]
    j = np.arange(3 * WIN_BLOCK)[None, :]
    return jnp.asarray((np.abs(j - WIN_BLOCK - i) <= WIN_RADIUS).astype(np.float32))


def _win_attn_kernel(sink_ref, band_ref, q_ref, kp_ref, kc_ref, kn_ref, vp_ref, vc_ref, vn_ref, kx_ref, vx_ref, o_ref,
                     *scratch):
    n = pl.program_id(1)
    nw = 3 * WIN_BLOCK
    keys = jnp.concatenate([kp_ref[0], kc_ref[0], kn_ref[0], kx_ref[0]], axis=0)
    vals = jnp.concatenate([vp_ref[0], vc_ref[0], vn_ref[0], vx_ref[0]], axis=0)
    has_prev = (n > 0).astype(F32)
    has_next = (n < pl.num_programs(1) - 1).astype(F32)
    band = band_ref[...]
    valid = jnp.concatenate([band[:, :WIN_BLOCK] * has_prev, band[:, WIN_BLOCK:2 * WIN_BLOCK],
                             band[:, 2 * WIN_BLOCK:] * has_next], axis=1) > 0.0
    _gqa_all(q_ref, keys, vals, (valid, nw), sink_ref, WIN_BLOCK, o_ref, scratch)


def _win_attention(q, k, v, kx, vx, sink):
    B, L, _ = q.shape
    nb = L // WIN_BLOCK
    blk = lambda f: pl.BlockSpec((1, WIN_BLOCK, KVD), f)
    prev = lambda b, n: (b, jnp.maximum(n - 1, 0), 0)
    cur = lambda b, n: (b, n, 0)
    nxt = lambda b, n: (b, jnp.minimum(n + 1, nb - 1), 0)
    ctx_spec = pl.BlockSpec((1, CTX_LEN, KVD), lambda b, n: (b, 0, 0))
    return pl.pallas_call(
        _win_attn_kernel,
        grid=(B, nb),
        in_specs=[pl.BlockSpec(memory_space=pltpu.SMEM),
                  _const_spec((WIN_BLOCK, 3 * WIN_BLOCK)),
                  pl.BlockSpec((1, WIN_BLOCK, QW), cur),
                  blk(prev), blk(cur), blk(nxt), blk(prev), blk(cur), blk(nxt), ctx_spec, ctx_spec],
        out_specs=pl.BlockSpec((1, WIN_BLOCK, QW), cur),
        out_shape=jax.ShapeDtypeStruct((B, L, QW), BF16),
        scratch_shapes=_gqa_scratch(WIN_BLOCK, 3 * WIN_BLOCK + CTX_LEN),
        compiler_params=_params("arbitrary", "arbitrary"),
        name="window_attention",
    )(sink, _window_band(), q, k, k, k, v, v, v, kx, vx)


def _ctx_attn_even_kernel(sink_ref, q_ref, k_ref, v_ref, o_ref, *scratch):
    _gqa_all(q_ref, k_ref[0], v_ref[0], None, sink_ref, CTX_LEN, o_ref, scratch)


def _ctx_attention_even(q, k, v, sink):
    B = q.shape[0]
    return pl.pallas_call(
        _ctx_attn_even_kernel,
        grid=(B,),
        in_specs=[pl.BlockSpec(memory_space=pltpu.SMEM),
                  pl.BlockSpec((1, CTX_LEN, QW), lambda b: (b, 0, 0)),
                  pl.BlockSpec((1, CTX_LEN, KVD), lambda b: (b, 0, 0)),
                  pl.BlockSpec((1, CTX_LEN, KVD), lambda b: (b, 0, 0))],
        out_specs=pl.BlockSpec((1, CTX_LEN, QW), lambda b: (b, 0, 0)),
        out_shape=jax.ShapeDtypeStruct((B, CTX_LEN, QW), BF16),
        scratch_shapes=_gqa_scratch(CTX_LEN, CTX_LEN),
        compiler_params=_params("arbitrary"),
        name="ctx_attention_even",
    )(sink, q, k, v)


def _out_ffn_kernel(mode, *refs):
    if mode == "even_latent":
        f_ref, a_ref, x_ref, mod_ref, g_ref, wo_ref, wg_ref, wu_ref, wd_ref, o_ref = refs
        fm = jnp.concatenate([f_ref[0, :, j, :] for j in range(f_ref.shape[2])], axis=0).astype(BF16)
        o = _dot(fm, wo_ref[:FOURIER_WIDTH]) + _dot(a_ref[0], wo_ref[FOURIER_WIDTH:])
    elif mode == "even_ctx":
        f_ref, a_ref, x_ref, mod_ref, g_ref, wo_ref, wg_ref, wu_ref, wd_ref, o_ref = refs
        o = _dot(f_ref[0].astype(BF16), wo_ref[:FOURIER_WIDTH]) + _dot(a_ref[0], wo_ref[FOURIER_WIDTH:])
    else:
        a_ref, x_ref, mod_ref, g_ref, wo_ref, wg_ref, wu_ref, wd_ref, o_ref = refs
        o = _dot(a_ref[0], wo_ref[...])
    x1 = x_ref[0] + _mod_slice(mod_ref, 2) * o
    h = _rms_mod(x1, g_ref[...], _mod_slice(mod_ref, 4), _mod_slice(mod_ref, 3)).astype(BF16)
    acc = jnp.zeros_like(x1)
    for c in range(D_FF // FF_CHUNK):
        cs = slice(c * FF_CHUNK, (c + 1) * FF_CHUNK)
        a = _silu(_dot(h, wg_ref[:, cs])) * _dot(h, wu_ref[:, cs])
        acc = acc + _dot(a.astype(BF16), wd_ref[cs, :])
    o_ref[0] = x1 + _mod_slice(mod_ref, 5) * acc


def _out_ffn(mode, mix, x, mod, mod_row, g, w_out, wg, wu, wd, tm):
    B, L, _ = x.shape
    tok = lambda w: pl.BlockSpec((1, tm, w), lambda b, t: (b, t, 0))
    if mode == "even_latent":
        assert tm % FFT_N == 0
        mix_specs = [pl.BlockSpec((1, FFT_N, tm // FFT_N, FOURIER_WIDTH), lambda b, t: (b, 0, t, 0)), tok(QW)]
    elif mode == "even_ctx":
        mix_specs = [tok(FOURIER_WIDTH), tok(QW)]
    else:
        mix_specs = [tok(NA_WIDTH)]
    return pl.pallas_call(
        functools.partial(_out_ffn_kernel, mode),
        grid=(B, L // tm),
        in_specs=mix_specs + [tok(D_MODEL),
                              pl.BlockSpec((1, 1, 6 * D_MODEL), lambda b, t: (mod_row(b), 0, 0)),
                              _const_spec((1, D_MODEL)),
                              _const_spec((D_MODEL, D_MODEL)),
                              _const_spec((D_MODEL, D_FF)), _const_spec((D_MODEL, D_FF)),
                              _const_spec((D_FF, D_MODEL))],
        out_specs=tok(D_MODEL),
        out_shape=jax.ShapeDtypeStruct((B, L, D_MODEL), F32),
        compiler_params=_params("arbitrary", "arbitrary"),
        name="out_ffn_" + mode,
    )(*mix, x, mod, g, w_out, wg, wu, wd)


def _in_odd_kernel(with_q, x_ref, mod_ref, g_ref, w_ref, qg_ref, kg_ref, hm_ref, *outs):
    h = _rms_mod(x_ref[0], g_ref[...], _mod_slice(mod_ref, 1), _mod_slice(mod_ref, 0)).astype(BF16)
    hm = hm_ref[...]
    W = NA_WIDTH
    if with_q:
        q_ref, k_ref, v_ref = outs
        q = _dot(h, w_ref[:, :W])
        for j in range(W // LANES):
            qj = _head_rms(q[:, j * LANES:(j + 1) * LANES], qg_ref[...], hm) * (1.0 / math.sqrt(HEAD_DIM))
            q_ref[0, :, j * LANES:(j + 1) * LANES] = qj.astype(BF16)
    else:
        k_ref, v_ref = outs
    k = _dot(h, w_ref[:, W:2 * W])
    for j in range(W // LANES):
        kj = _head_rms(k[:, j * LANES:(j + 1) * LANES], kg_ref[...], hm)
        k_ref[0, :, j * LANES:(j + 1) * LANES] = kj.astype(BF16)
    v_ref[0] = _dot(h, w_ref[:, 2 * W:]).astype(BF16)


def _in_odd(x, mod, mod_row, g, w_in, q_g, k_g, hm, with_q, tm):
    B, L, _ = x.shape
    tok = pl.BlockSpec((1, tm, NA_WIDTH), lambda b, t: (b, t, 0))
    n_out = 3 if with_q else 2
    return pl.pallas_call(
        functools.partial(_in_odd_kernel, with_q),
        grid=(B, L // tm),
        in_specs=[pl.BlockSpec((1, tm, D_MODEL), lambda b, t: (b, t, 0)),
                  pl.BlockSpec((1, 1, 6 * D_MODEL), lambda b, t: (mod_row(b), 0, 0)),
                  _const_spec((1, D_MODEL)),
                  _const_spec((D_MODEL, 3 * NA_WIDTH)),
                  _const_spec((1, LANES)), _const_spec((1, LANES)), _const_spec((LANES, LANES))],
        out_specs=[tok] * n_out,
        out_shape=[jax.ShapeDtypeStruct((B, L, NA_WIDTH), BF16)] * n_out,
        compiler_params=_params("arbitrary", "arbitrary"),
        name="in_odd" if with_q else "in_odd_ctx",
    )(x, mod, g, w_in, q_g, k_g, hm)


def _bias_table_kernel(rb_ref, oh_ref, o_ref):
    x = rb_ref[...]
    hi = x.astype(BF16)
    r1 = x - hi.astype(F32)
    mid = r1.astype(BF16)
    lo = (r1 - mid.astype(F32)).astype(BF16)
    oh = oh_ref[...]
    o_ref[...] = (_dot(hi, oh) + _dot(mid, oh)) + _dot(lo, oh)


def _na_bias_table(rel_bias, onehot):
    H = NA_HEADS
    ndr = 2 * NA_KH - 1
    rows = H * ndr
    rb = jnp.pad(rel_bias.reshape(rows, 2 * NA_KW - 1), ((0, 0), (0, LANES - (2 * NA_KW - 1))))
    flat = pl.pallas_call(
        _bias_table_kernel,
        out_shape=jax.ShapeDtypeStruct((rows, GRID_W * GRID_W), F32),
        compiler_params=pltpu.CompilerParams(vmem_limit_bytes=VMEM_LIMIT),
        name="na_bias_table",
    )(rb, onehot)
    t = flat.reshape(H, ndr, GRID_W, GRID_W)
    return jnp.concatenate([t[:, :-1], t[:, 1:]], axis=-1)


def _na_kernel(rows_per_step, q_ref, k_ref, v_ref, kx_ref, vx_ref, bias_ref, o_ref, s_ref, p_ref, inv_ref):
    t = pl.program_id(1)
    n = NA_KH * GRID_W
    cq = lax.broadcasted_iota(jnp.int32, (GRID_W, n), 0)
    ck = lax.broadcasted_iota(jnp.int32, (GRID_W, n), 1) % GRID_W
    c0 = jnp.clip(cq - NA_KW // 2, 0, GRID_W - NA_KW)
    mask = (ck >= c0) & (ck < c0 + NA_KW)
    lo_lanes = lax.broadcasted_iota(jnp.int32, (GRID_W, LANES), 1) < HEAD_DIM

    def one_row(i, carry):
        r = t * rows_per_step + i
        r0 = jnp.clip(r - NA_KH // 2, 0, GRID_ROWS - NA_KH)
        dr0 = r0 - r + NA_KH - 1
        start = pl.multiple_of(r0 * GRID_W, GRID_W)
        qoff = pl.multiple_of(i * GRID_W, GRID_W)
        for hp in range(NA_HEADS // 2):
            sl = slice(hp * LANES, (hp + 1) * LANES)
            qs = q_ref[0, pl.ds(qoff, GRID_W), sl]
            ks = k_ref[0, pl.ds(start, n), sl]
            kxs = kx_ref[0, :, sl]
            for half in range(2):
                h = 2 * hp + half
                qm = jnp.where(lo_lanes if half == 0 else ~lo_lanes, qs, jnp.zeros_like(qs))
                bias = jnp.concatenate([bias_ref[h, dr0 + 2 * p] for p in range(NA_KH // 2)], axis=1)
                rs = slice(h * GRID_W, (h + 1) * GRID_W)
                s_ref[rs, :n] = jnp.where(mask, _dot_nt(qm, ks) + bias, NEG)
                s_ref[rs, n:] = _dot_nt(qm, kxs)
        for c in range(NA_HEADS * GRID_W // SOFTMAX_ROWS):
            cs = slice(c * SOFTMAX_ROWS, (c + 1) * SOFTMAX_ROWS)
            s = s_ref[cs, :]
            e = jnp.exp(s - jnp.max(s, axis=-1, keepdims=True))
            inv_ref[cs, :] = 1.0 / jnp.sum(e, axis=-1, keepdims=True)
            p_ref[cs, :] = e.astype(BF16)
        for hp in range(NA_HEADS // 2):
            sl = slice(hp * LANES, (hp + 1) * LANES)
            vs = v_ref[0, pl.ds(start, n), sl]
            vxs = vx_ref[0, :, sl]
            res = []
            for half in range(2):
                rs = slice((2 * hp + half) * GRID_W, (2 * hp + half + 1) * GRID_W)
                res.append((_dot(p_ref[rs, :n], vs) + _dot(p_ref[rs, n:], vxs)) * inv_ref[rs, :])
            o_ref[0, pl.ds(qoff, GRID_W), sl] = jnp.where(lo_lanes, res[0], res[1]).astype(BF16)
        return carry

    lax.fori_loop(0, rows_per_step, one_row, 0)


def _neighbourhood_attention(q, k, v, kx, vx, bias_tab):
    B, L, W = q.shape
    rows_per_step = 8
    full = pl.BlockSpec((1, L, W), lambda b, t: (b, 0, 0))
    ctx_spec = pl.BlockSpec((1, CTX_LEN, W), lambda b, t: (b, 0, 0))
    rows = pl.BlockSpec((1, rows_per_step * GRID_W, W), lambda b, t: (b, t, 0))
    n_keys = NA_KH * GRID_W + CTX_LEN
    return pl.pallas_call(
        functools.partial(_na_kernel, rows_per_step),
        grid=(B, GRID_ROWS // rows_per_step),
        in_specs=[rows, full, full, ctx_spec, ctx_spec, _const_spec(bias_tab.shape)],
        out_specs=rows,
        out_shape=jax.ShapeDtypeStruct((B, L, W), BF16),
        scratch_shapes=[pltpu.VMEM((NA_HEADS * GRID_W, n_keys), F32),
                        pltpu.VMEM((NA_HEADS * GRID_W, n_keys), BF16),
                        pltpu.VMEM((NA_HEADS * GRID_W, 1), F32)],
        compiler_params=_params("arbitrary", "arbitrary"),
        name="neighbourhood_attention",
    )(q, k, v, kx, vx, bias_tab)


def kernel(x, c, ctx, c_ctx, ada_w, ada_b, norm1_g, norm2_g, ffn_w_gate, ffn_w_up, ffn_w_down,
           ev_w_in, ev_w_out, ev_q_norm, ev_k_norm, ev_sink,
           od_w_in, od_w_out, od_q_norm, od_k_norm, od_rel_bias):
    assert x.shape == (BATCH, SEQ, D_MODEL) and ctx.shape == (BATCH, CTX_LEN, D_MODEL)
    wc, m1, m3, twc, tws, mctx = _fourier_tables()
    hm = _head_mean_matrix()
    rope_tabs = _rope_tables()
    lane_gain = lambda gvec: jnp.tile(gvec, LANES // HEAD_DIM).reshape(1, LANES)
    lat_row = lambda b: b
    ctx_row = lambda b: CTX_MOD_ROW
    tm = 512

    cs = jnp.concatenate([c, c_ctx[None, :], jnp.zeros((MOD_ROWS - BATCH - 1, D_MODEL), F32)], axis=0)
    mod = _modulation(cs, ada_w, ada_b).reshape(DEPTH, MOD_ROWS, 1, 6 * D_MODEL)

    w_in0 = ev_w_in[0].astype(BF16)
    w_out0 = ev_w_out[0].astype(BF16)
    g1 = norm1_g[0].reshape(1, D_MODEL)
    g2 = norm2_g[0].reshape(1, D_MODEL)
    ffn0 = (ffn_w_gate[0].astype(BF16), ffn_w_up[0].astype(BF16), ffn_w_down[0].astype(BF16))
    qg, kg = lane_gain(ev_q_norm[0]), lane_gain(ev_k_norm[0])
    y_l, q_l, k_l, v_l = _in_even(x, mod[0], lat_row, g1, w_in0, qg, kg, hm, wc, rope_tabs, tm)
    y_c, q_c, k_c, v_c = _in_even(ctx, mod[0], ctx_row, g1, w_in0, qg, kg, hm, wc, None, CTX_LEN)
    f_l = _fourier_latent(y_l, m1, m3, twc, tws)
    f_c = _fourier_ctx(y_c, mctx)
    a_l = _win_attention(q_l, k_l, v_l, k_c, v_c, ev_sink[0])
    a_c = _ctx_attention_even(q_c, k_c, v_c, ev_sink[0])
    x1 = _out_ffn("even_latent", (f_l, a_l), x, mod[0], lat_row, g2, w_out0, *ffn0, tm)
    y1 = _out_ffn("even_ctx", (f_c, a_c), ctx, mod[0], ctx_row, g2, w_out0, *ffn0, CTX_LEN)

    w_in1 = od_w_in[0].astype(BF16)
    w_out1 = od_w_out[0].astype(BF16)
    g1 = norm1_g[1].reshape(1, D_MODEL)
    g2 = norm2_g[1].reshape(1, D_MODEL)
    ffn1 = (ffn_w_gate[1].astype(BF16), ffn_w_up[1].astype(BF16), ffn_w_down[1].astype(BF16))
    qg, kg = lane_gain(od_q_norm[0]), lane_gain(od_k_norm[0])
    q_l, k_l, v_l = _in_odd(x1, mod[1], lat_row, g1, w_in1, qg, kg, hm, True, tm)
    k_c, v_c = _in_odd(y1, mod[1], ctx_row, g1, w_in1, qg, kg, hm, False, CTX_LEN)
    bias_tab = _na_bias_table(od_rel_bias[0], _na_onehot())
    a_l = _neighbourhood_attention(q_l, k_l, v_l, k_c, v_c, bias_tab)
    return _out_ffn("odd", (a_l,), x1, mod[1], lat_row, g2, w_out1, *ffn1, tm)
```

```python
import functools
import math

import numpy as np
import jax
import jax.numpy as jnp
from jax import lax
from jax.experimental import pallas as pl
from jax.experimental.pallas import tpu as pltpu

D_MODEL = 1024
BATCH = 4
SEQ = 4096
DEPTH = 2
GRID_W = 64
CTX_LEN = 256
HEAD_DIM = 64
EPS = 1e-6
NEG = -1e30
FOURIER_WIDTH = D_MODEL // 2
FOURIER_GROUPS = 4
FOURIER_GROUP_CH = FOURIER_WIDTH // FOURIER_GROUPS
WIN_Q_HEADS = (D_MODEL // 2) // HEAD_DIM
WIN_KV_HEADS = 2
WIN_GROUP = WIN_Q_HEADS // WIN_KV_HEADS
WIN_RADIUS = 128
WIN_BLOCK = 128
QW = WIN_Q_HEADS * HEAD_DIM
KW = WIN_KV_HEADS * HEAD_DIM
KVD = 2 * KW
EV_IN_WIDTH = FOURIER_WIDTH + QW + 2 * KW
NA_HEADS = D_MODEL // HEAD_DIM
NA_KH = 8
NA_KW = 16
NA_WIDTH = NA_HEADS * HEAD_DIM
ROPE_THETA = 10000.0
ROPE_FREQS = HEAD_DIM // 4
D_FF = ((8 * D_MODEL // 3 + 255) // 256) * 256
GRID_ROWS = SEQ // GRID_W

LANES = 128
MOD_ROWS = 8
CTX_MOD_ROW = BATCH
FFT_N = 64
FF_CHUNK = 256
SOFTMAX_ROWS = 32
VMEM_LIMIT = 56 * 1024 * 1024

LOG2E = math.log2(math.e)
QK_SCALE = LOG2E / math.sqrt(HEAD_DIM)

BF16 = jnp.bfloat16
F32 = jnp.float32

assert DEPTH == 2 and SEQ == FFT_N * FFT_N and D_FF % FF_CHUNK == 0


def _params(*sem):
    return pltpu.CompilerParams(dimension_semantics=sem, vmem_limit_bytes=VMEM_LIMIT)


def _dot(a, b):
    return jnp.dot(a, b, preferred_element_type=F32)


def _dot_nt(a, b):
    return lax.dot_general(a, b, (((1,), (1,)), ((), ())), preferred_element_type=F32)


def _silu(x):
    return x / (1.0 + jnp.exp(-x))


def _const_spec(shape):
    nd = len(shape)
    return pl.BlockSpec(shape, lambda *_: (0,) * nd, pipeline_mode=pl.Buffered(1))


def _dft_cos_sin(n):
    idx = (np.arange(n)[:, None] * np.arange(n)[None, :]) % n
    ang = 2.0 * np.pi * idx / n
    return np.cos(ang), np.sin(ang)


def _fourier_tables():
    cc, sc = _dft_cos_sin(FOURIER_GROUP_CH)
    wc = np.concatenate([cc, -sc], axis=1) / math.sqrt(FOURIER_GROUP_CH)
    c64, s64 = _dft_cos_sin(FFT_N)
    m1 = np.concatenate([c64, -s64], axis=0) / math.sqrt(FFT_N)
    m3 = np.concatenate([c64, s64], axis=1) / math.sqrt(FFT_N)
    tw = (np.arange(FFT_N)[:, None] * np.arange(FFT_N)[None, :]) % SEQ
    tw = 2.0 * np.pi * tw / SEQ
    twc = np.repeat(np.cos(tw)[:, :, None], LANES, axis=2)
    tws = np.repeat(np.sin(tw)[:, :, None], LANES, axis=2)
    cx, sx = _dft_cos_sin(CTX_LEN)
    mctx = np.concatenate([cx, sx], axis=1) / math.sqrt(CTX_LEN)
    as32 = lambda a: jnp.asarray(a, F32)
    return (as32(wc).astype(BF16), as32(m1).astype(BF16), as32(m3).astype(BF16),
            as32(twc), as32(tws), as32(mctx).astype(BF16))


def _head_mean_matrix():
    blk = np.kron(np.eye(LANES // HEAD_DIM), np.ones((HEAD_DIM, HEAD_DIM))) / HEAD_DIM
    return jnp.asarray(blk, BF16)


def _rope_tables():
    t = jnp.arange(SEQ, dtype=jnp.int32)
    row = (t // GRID_W).astype(F32)
    col = (t % GRID_W).astype(F32)
    inv = ROPE_THETA ** (-jnp.arange(ROPE_FREQS, dtype=F32) / ROPE_FREQS)
    ang_row = row[:, None] * inv[None, :]
    ang_col = col[:, None] * inv[None, :]
    zero = jnp.zeros_like(ang_row)
    cos = jnp.concatenate([jnp.cos(ang_row)] * 2 + [jnp.cos(ang_col)] * 2, axis=1)
    sin_hi = jnp.concatenate([-jnp.sin(ang_row), zero, -jnp.sin(ang_col), zero], axis=1)
    sin_lo = jnp.concatenate([zero, jnp.sin(ang_row), zero, jnp.sin(ang_col)], axis=1)
    rep = LANES // HEAD_DIM
    return jnp.tile(cos, (1, rep)), jnp.tile(sin_hi, (1, rep)), jnp.tile(sin_lo, (1, rep))


def _na_onehot():
    cq = np.arange(GRID_W)
    dc = np.clip(cq[None, :] - cq[:, None] + NA_KW - 1, 0, 2 * NA_KW - 2)
    oh = np.zeros((LANES, GRID_W * GRID_W), np.float32)
    oh[dc.reshape(-1), np.arange(GRID_W * GRID_W)] = 1.0
    return jnp.asarray(oh, BF16)


def _mod_kernel(cs_ref, w_ref, b_ref, o_ref):
    s = _silu(cs_ref[...]).astype(BF16)
    o_ref[0] = _dot(s, w_ref[0].astype(BF16)) + b_ref[0]


def _modulation(cs, ada_w, ada_b):
    tn = 1536
    return pl.pallas_call(
        _mod_kernel,
        grid=(DEPTH, 6 * D_MODEL // tn),
        in_specs=[pl.BlockSpec((MOD_ROWS, D_MODEL), lambda i, j: (0, 0)),
                  pl.BlockSpec((1, D_MODEL, tn), lambda i, j: (i, 0, j)),
                  pl.BlockSpec((1, 1, tn), lambda i, j: (i, 0, j))],
        out_specs=pl.BlockSpec((1, MOD_ROWS, tn), lambda i, j: (i, 0, j)),
        out_shape=jax.ShapeDtypeStruct((DEPTH, MOD_ROWS, 6 * D_MODEL), F32),
        compiler_params=_params("arbitrary", "arbitrary"),
        name="ada_modulation",
    )(cs, ada_w, ada_b.reshape(DEPTH, 1, 6 * D_MODEL))


def _mod_slice(mod_ref, k):
    return mod_ref[0, :, k * D_MODEL:(k + 1) * D_MODEL]


def _rms_mod(x, g, scale, shift):
    y = x * lax.rsqrt(jnp.mean(x * x, axis=-1, keepdims=True) + EPS)
    return (y * g) * (1.0 + scale) + shift


def _head_rms(t, gain, hm):
    ms = _dot((t * t).astype(BF16), hm)
    return t * lax.rsqrt(ms + EPS) * gain


def _rope(t, cos, sin_hi, sin_lo):
    up = pltpu.roll(t, LANES - ROPE_FREQS, axis=1)
    dn = pltpu.roll(t, ROPE_FREQS, axis=1)
    return t * cos + up * sin_hi + dn * sin_lo


def _in_even_kernel(rope, x_ref, mod_ref, g_ref, w_ref, qg_ref, kg_ref, hm_ref, wc_ref, *rest):
    if rope:
        cos_ref, sh_ref, sl_ref, y_ref, q_ref, k_ref, v_ref = rest
    else:
        y_ref, q_ref, k_ref, v_ref = rest
    h = _rms_mod(x_ref[0], g_ref[...], _mod_slice(mod_ref, 1), _mod_slice(mod_ref, 0)).astype(BF16)
    hm = hm_ref[...]
    F = FOURIER_WIDTH
    f = _dot(h, w_ref[:, :F]).astype(BF16)
    for g in range(FOURIER_GROUPS):
        yg = _dot(f[:, g * LANES:(g + 1) * LANES], wc_ref[...])
        y_ref[0, :, g * LANES:(g + 1) * LANES] = yg[:, :LANES].astype(y_ref.dtype)
        y_ref[0, :, F + g * LANES:F + (g + 1) * LANES] = yg[:, LANES:].astype(y_ref.dtype)

    def qk(t, gain):
        t = _head_rms(t, gain, hm)
        if rope:
            t = _rope(t, cos_ref[...], sh_ref[...], sl_ref[...])
        return t

    q = _dot(h, w_ref[:, F:F + QW])
    for j in range(QW // LANES):
        qj = qk(q[:, j * LANES:(j + 1) * LANES], qg_ref[...]) * QK_SCALE
        q_ref[0, :, j * LANES:(j + 1) * LANES] = qj.astype(BF16)
    kv = _dot(h, w_ref[:, F + QW:])
    lo_lanes = lax.broadcasted_iota(jnp.int32, (1, LANES), 1) < HEAD_DIM

    def dup_heads(t):
        sw = pltpu.roll(t, HEAD_DIM, axis=1)
        return jnp.concatenate([jnp.where(lo_lanes, t, sw), jnp.where(lo_lanes, sw, t)], axis=1)

    k_ref[0] = dup_heads(qk(kv[:, :KW], kg_ref[...])).astype(BF16)
    v_ref[0] = dup_heads(kv[:, KW:]).astype(BF16)


def _in_even(x, mod, mod_row, g, w_in, q_g, k_g, hm, wc, rope_tabs, tm):
    B, L, _ = x.shape
    rope = rope_tabs is not None
    in_specs = [pl.BlockSpec((1, tm, D_MODEL), lambda b, t: (b, t, 0)),
                pl.BlockSpec((1, 1, 6 * D_MODEL), lambda b, t: (mod_row(b), 0, 0)),
                _const_spec((1, D_MODEL)),
                _const_spec((D_MODEL, EV_IN_WIDTH)),
                _const_spec((1, LANES)), _const_spec((1, LANES)),
                _const_spec((LANES, LANES)), _const_spec((LANES, 2 * LANES))]
    args = [x, mod, g, w_in, q_g, k_g, hm, wc]
    if rope:
        in_specs += [pl.BlockSpec((tm, LANES), lambda b, t: (t, 0))] * 3
        args += list(rope_tabs)
    return pl.pallas_call(
        functools.partial(_in_even_kernel, rope),
        grid=(B, L // tm),
        in_specs=in_specs,
        out_specs=[pl.BlockSpec((1, tm, 2 * FOURIER_WIDTH), lambda b, t: (b, t, 0)),
                   pl.BlockSpec((1, tm, QW), lambda b, t: (b, t, 0)),
                   pl.BlockSpec((1, tm, KVD), lambda b, t: (b, t, 0)),
                   pl.BlockSpec((1, tm, KVD), lambda b, t: (b, t, 0))],
        out_shape=[jax.ShapeDtypeStruct((B, L, 2 * FOURIER_WIDTH), BF16),
                   jax.ShapeDtypeStruct((B, L, QW), BF16),
                   jax.ShapeDtypeStruct((B, L, KVD), BF16),
                   jax.ShapeDtypeStruct((B, L, KVD), BF16)],
        compiler_params=_params("arbitrary", "arbitrary"),
        name="in_even_rope" if rope else "in_even_ctx",
    )(*args)


def _fft_stage1_kernel(n2t, y_ref, m1_ref, twc_ref, tws_ref, o_ref):
    F = FOURIER_WIDTH
    for j in range(n2t):
        a = _dot(m1_ref[...], y_ref[0, :, j * 2 * F:(j + 1) * 2 * F])
        top, bot = a[:FFT_N], a[FFT_N:]
        ar = top[:, :F] - bot[:, F:]
        ai = top[:, F:] + bot[:, :F]
        tc = jnp.tile(twc_ref[j], (1, F // LANES))
        ts = jnp.tile(tws_ref[j], (1, F // LANES))
        o_ref[0, 0, :, j * F:(j + 1) * F] = (ar * tc + ai * ts).astype(BF16)
        o_ref[0, 1, :, j * F:(j + 1) * F] = (ai * tc - ar * ts).astype(BF16)


def _fft_stage2_kernel(k1t, b_ref, m3_ref, o_ref):
    for j in range(k1t):
        rhs = jnp.concatenate([b_ref[0, 0, j], b_ref[0, 1, j]], axis=0)
        o_ref[0, j] = _dot(m3_ref[...], rhs)


def _fourier_latent(y, m1, m3, twc, tws):
    B = y.shape[0]
    F = FOURIER_WIDTH
    n2t = 8
    yv = y.reshape(B, FFT_N, FFT_N * 2 * F)
    bh = pl.pallas_call(
        functools.partial(_fft_stage1_kernel, n2t),
        grid=(B, FFT_N // n2t),
        in_specs=[pl.BlockSpec((1, FFT_N, n2t * 2 * F), lambda b, t: (b, 0, t)),
                  _const_spec((2 * FFT_N, FFT_N)),
                  pl.BlockSpec((n2t, FFT_N, LANES), lambda b, t: (t, 0, 0)),
                  pl.BlockSpec((n2t, FFT_N, LANES), lambda b, t: (t, 0, 0))],
        out_specs=pl.BlockSpec((1, 2, FFT_N, n2t * F), lambda b, t: (b, 0, 0, t)),
        out_shape=jax.ShapeDtypeStruct((B, 2, FFT_N, FFT_N * F), BF16),
        compiler_params=_params("arbitrary", "arbitrary"),
        name="fft_stage1",
    )(yv, m1, twc, tws)
    k1t = 8
    bv = bh.reshape(B, 2, FFT_N, FFT_N, F)
    return pl.pallas_call(
        functools.partial(_fft_stage2_kernel, k1t),
        grid=(B, FFT_N // k1t),
        in_specs=[pl.BlockSpec((1, 2, k1t, FFT_N, F), lambda b, t: (b, 0, t, 0, 0)),
                  _const_spec((FFT_N, 2 * FFT_N))],
        out_specs=pl.BlockSpec((1, k1t, FFT_N, F), lambda b, t: (b, t, 0, 0)),
        out_shape=jax.ShapeDtypeStruct((B, FFT_N, FFT_N, F), F32),
        compiler_params=_params("arbitrary", "arbitrary"),
        name="fft_stage2",
    )(bv, m3)


def _fourier_ctx_kernel(y_ref, m_ref, o_ref):
    F = FOURIER_WIDTH
    rhs = jnp.concatenate([y_ref[0, :, :F], y_ref[0, :, F:]], axis=0)
    o_ref[0] = _dot(m_ref[...], rhs)


def _fourier_ctx(y, mctx):
    B = y.shape[0]
    return pl.pallas_call(
        _fourier_ctx_kernel,
        grid=(B,),
        in_specs=[pl.BlockSpec((1, CTX_LEN, 2 * FOURIER_WIDTH), lambda b: (b, 0, 0)),
                  _const_spec((CTX_LEN, 2 * CTX_LEN))],
        out_specs=pl.BlockSpec((1, CTX_LEN, FOURIER_WIDTH), lambda b: (b, 0, 0)),
        out_shape=jax.ShapeDtypeStruct((B, CTX_LEN, FOURIER_WIDTH), F32),
        compiler_params=_params("arbitrary"),
        name="fourier_ctx",
    )(y, mctx)


def _gqa_group(q_slabs, keys, vals, masks, sinks, rows, o_ref, first_slab, scratch):
    s_ref, p_ref, inv_ref = scratch
    lo_lanes = lax.broadcasted_iota(jnp.int32, (rows, LANES), 1) < HEAD_DIM
    zero = jnp.zeros((rows, LANES), BF16)
    qs = jnp.concatenate([jnp.where(lo_lanes if half == 0 else ~lo_lanes, qslab, zero)
                          for qslab in q_slabs for half in range(2)], axis=0)
    s_ref[...] = _dot_nt(qs, keys)
    for c in range(WIN_GROUP * rows // SOFTMAX_ROWS):
        rs = slice(c * SOFTMAX_ROWS, (c + 1) * SOFTMAX_ROWS)
        s = s_ref[rs, :]
        if masks is not None:
            mask, w = masks
            r0 = (c * SOFTMAX_ROWS) % rows
            s = jnp.concatenate([jnp.where(mask[r0:r0 + SOFTMAX_ROWS], s[:, :w], NEG), s[:, w:]], axis=1)
        sk = sinks[(c * SOFTMAX_ROWS) // rows]
        m = jnp.maximum(jnp.max(s, axis=-1, keepdims=True), sk)
        e = jnp.exp2(s - m)
        inv_ref[rs, :] = 1.0 / (jnp.sum(e, axis=-1, keepdims=True) + jnp.exp2(sk - m))
        p_ref[rs, :] = e.astype(BF16)
    o = _dot(p_ref[...], vals) * inv_ref[...]
    for i in range(WIN_GROUP // 2):
        a = o[(2 * i) * rows:(2 * i + 1) * rows]
        b = o[(2 * i + 1) * rows:(2 * i + 2) * rows]
        sl = slice((first_slab + i) * LANES, (first_slab + i + 1) * LANES)
        o_ref[0, :, sl] = jnp.where(lo_lanes, a, b).astype(BF16)


def _gqa_scratch(rows, n_keys):
    return [pltpu.VMEM((WIN_GROUP * rows, n_keys), F32), pltpu.VMEM((WIN_GROUP * rows, n_keys), BF16),
            pltpu.VMEM((WIN_GROUP * rows, 1), F32)]


def _gqa_all(q_ref, keys, vals, masks, sink_ref, rows, o_ref, scratch):
    slabs_per_group = WIN_GROUP // 2
    for kv in range(WIN_KV_HEADS):
        q_slabs = [q_ref[0, :, (kv * slabs_per_group + i) * LANES:(kv * slabs_per_group + i + 1) * LANES]
                   for i in range(slabs_per_group)]
        sinks = [sink_ref[kv * WIN_GROUP + g] * LOG2E for g in range(WIN_GROUP)]
        ksl = slice(kv * LANES, (kv + 1) * LANES)
        _gqa_group(q_slabs, keys[:, ksl], vals[:, ksl], masks, sinks, rows, o_ref, kv * slabs_per_group, scratch)


def _window_band():
    i = np.arange(WIN_BLOCK)[:, None]
    j = np.arange(3 * WIN_BLOCK)[None, :]
    return jnp.asarray((np.abs(j - WIN_BLOCK - i) <= WIN_RADIUS).astype(np.float32))


def _win_attn_kernel(sink_ref, band_ref, q_ref, kp_ref, kc_ref, kn_ref, vp_ref, vc_ref, vn_ref, kx_ref, vx_ref, o_ref,
                     *scratch):
    n = pl.program_id(1)
    nw = 3 * WIN_BLOCK
    keys = jnp.concatenate([kp_ref[0], kc_ref[0], kn_ref[0], kx_ref[0]], axis=0)
    vals = jnp.concatenate([vp_ref[0], vc_ref[0], vn_ref[0], vx_ref[0]], axis=0)
    has_prev = (n > 0).astype(F32)
    has_next = (n < pl.num_programs(1) - 1).astype(F32)
    band = band_ref[...]
    valid = jnp.concatenate([band[:, :WIN_BLOCK] * has_prev, band[:, WIN_BLOCK:2 * WIN_BLOCK],
                             band[:, 2 * WIN_BLOCK:] * has_next], axis=1) > 0.0
    _gqa_all(q_ref, keys, vals, (valid, nw), sink_ref, WIN_BLOCK, o_ref, scratch)


def _win_attention(q, k, v, kx, vx, sink):
    B, L, _ = q.shape
    nb = L // WIN_BLOCK
    blk = lambda f: pl.BlockSpec((1, WIN_BLOCK, KVD), f)
    prev = lambda b, n: (b, jnp.maximum(n - 1, 0), 0)
    cur = lambda b, n: (b, n, 0)
    nxt = lambda b, n: (b, jnp.minimum(n + 1, nb - 1), 0)
    ctx_spec = pl.BlockSpec((1, CTX_LEN, KVD), lambda b, n: (b, 0, 0))
    return pl.pallas_call(
        _win_attn_kernel,
        grid=(B, nb),
        in_specs=[pl.BlockSpec(memory_space=pltpu.SMEM),
                  _const_spec((WIN_BLOCK, 3 * WIN_BLOCK)),
                  pl.BlockSpec((1, WIN_BLOCK, QW), cur),
                  blk(prev), blk(cur), blk(nxt), blk(prev), blk(cur), blk(nxt), ctx_spec, ctx_spec],
        out_specs=pl.BlockSpec((1, WIN_BLOCK, QW), cur),
        out_shape=jax.ShapeDtypeStruct((B, L, QW), BF16),
        scratch_shapes=_gqa_scratch(WIN_BLOCK, 3 * WIN_BLOCK + CTX_LEN),
        compiler_params=_params("arbitrary", "arbitrary"),
        name="window_attention",
    )(sink, _window_band(), q, k, k, k, v, v, v, kx, vx)


def _ctx_attn_even_kernel(sink_ref, q_ref, k_ref, v_ref, o_ref, *scratch):
    _gqa_all(q_ref, k_ref[0], v_ref[0], None, sink_ref, CTX_LEN, o_ref, scratch)


def _ctx_attention_even(q, k, v, sink):
    B = q.shape[0]
    return pl.pallas_call(
        _ctx_attn_even_kernel,
        grid=(B,),
        in_specs=[pl.BlockSpec(memory_space=pltpu.SMEM),
                  pl.BlockSpec((1, CTX_LEN, QW), lambda b: (b, 0, 0)),
                  pl.BlockSpec((1, CTX_LEN, KVD), lambda b: (b, 0, 0)),
                  pl.BlockSpec((1, CTX_LEN, KVD), lambda b: (b, 0, 0))],
        out_specs=pl.BlockSpec((1, CTX_LEN, QW), lambda b: (b, 0, 0)),
        out_shape=jax.ShapeDtypeStruct((B, CTX_LEN, QW), BF16),
        scratch_shapes=_gqa_scratch(CTX_LEN, CTX_LEN),
        compiler_params=_params("arbitrary"),
        name="ctx_attention_even",
    )(sink, q, k, v)


def _out_ffn_kernel(mode, *refs):
    if mode == "even_latent":
        f_ref, a_ref, x_ref, mod_ref, g_ref, wo_ref, wg_ref, wu_ref, wd_ref, o_ref = refs
        fm = jnp.concatenate([f_ref[0, :, j, :] for j in range(f_ref.shape[2])], axis=0).astype(BF16)
        o = _dot(fm, wo_ref[:FOURIER_WIDTH]) + _dot(a_ref[0], wo_ref[FOURIER_WIDTH:])
    elif mode == "even_ctx":
        f_ref, a_ref, x_ref, mod_ref, g_ref, wo_ref, wg_ref, wu_ref, wd_ref, o_ref = refs
        o = _dot(f_ref[0].astype(BF16), wo_ref[:FOURIER_WIDTH]) + _dot(a_ref[0], wo_ref[FOURIER_WIDTH:])
    else:
        a_ref, x_ref, mod_ref, g_ref, wo_ref, wg_ref, wu_ref, wd_ref, o_ref = refs
        o = _dot(a_ref[0], wo_ref[...])
    x1 = x_ref[0] + _mod_slice(mod_ref, 2) * o
    h = _rms_mod(x1, g_ref[...], _mod_slice(mod_ref, 4), _mod_slice(mod_ref, 3)).astype(BF16)
    acc = jnp.zeros_like(x1)
    for c in range(D_FF // FF_CHUNK):
        cs = slice(c * FF_CHUNK, (c + 1) * FF_CHUNK)
        a = _silu(_dot(h, wg_ref[0, :, cs])) * _dot(h, wu_ref[0, :, cs])
        acc = acc + _dot(a.astype(BF16), wd_ref[0, cs, :])
    o_ref[0] = x1 + _mod_slice(mod_ref, 5) * acc


def _out_ffn(mode, mix, x, mod, mod_row, g, w_out, layer, wg, wu, wd, tm):
    B, L, _ = x.shape
    tok = lambda w: pl.BlockSpec((1, tm, w), lambda b, t: (b, t, 0))
    ffn_spec = lambda r, c: pl.BlockSpec((1, r, c), lambda b, t: (layer, 0, 0), pipeline_mode=pl.Buffered(1))
    if mode == "even_latent":
        assert tm % FFT_N == 0
        mix_specs = [pl.BlockSpec((1, FFT_N, tm // FFT_N, FOURIER_WIDTH), lambda b, t: (b, 0, t, 0)), tok(QW)]
    elif mode == "even_ctx":
        mix_specs = [tok(FOURIER_WIDTH), tok(QW)]
    else:
        mix_specs = [tok(NA_WIDTH)]
    return pl.pallas_call(
        functools.partial(_out_ffn_kernel, mode),
        grid=(B, L // tm),
        in_specs=mix_specs + [tok(D_MODEL),
                              pl.BlockSpec((1, 1, 6 * D_MODEL), lambda b, t: (mod_row(b), 0, 0)),
                              _const_spec((1, D_MODEL)),
                              _const_spec((D_MODEL, D_MODEL)),
                              ffn_spec(D_MODEL, D_FF), ffn_spec(D_MODEL, D_FF), ffn_spec(D_FF, D_MODEL)],
        out_specs=tok(D_MODEL),
        out_shape=jax.ShapeDtypeStruct((B, L, D_MODEL), F32),
        compiler_params=_params("arbitrary", "arbitrary"),
        name="out_ffn_" + mode,
    )(*mix, x, mod, g, w_out, wg, wu, wd)


def _in_odd_kernel(with_q, x_ref, mod_ref, g_ref, w_ref, qg_ref, kg_ref, hm_ref, *outs):
    h = _rms_mod(x_ref[0], g_ref[...], _mod_slice(mod_ref, 1), _mod_slice(mod_ref, 0)).astype(BF16)
    hm = hm_ref[...]
    W = NA_WIDTH
    if with_q:
        q_ref, k_ref, v_ref = outs
        q = _dot(h, w_ref[:, :W])
        for j in range(W // LANES):
            qj = _head_rms(q[:, j * LANES:(j + 1) * LANES], qg_ref[...], hm) * QK_SCALE
            q_ref[0, j] = qj.astype(BF16)
    else:
        k_ref, v_ref = outs
    k = _dot(h, w_ref[:, W:2 * W])
    for j in range(W // LANES):
        kj = _head_rms(k[:, j * LANES:(j + 1) * LANES], kg_ref[...], hm)
        k_ref[0, j] = kj.astype(BF16)
    v = _dot(h, w_ref[:, 2 * W:]).astype(BF16)
    for j in range(W // LANES):
        v_ref[0, j] = v[:, j * LANES:(j + 1) * LANES]


def _in_odd(x, mod, mod_row, g, w_in, q_g, k_g, hm, with_q, tm):
    B, L, _ = x.shape
    pairs = NA_WIDTH // LANES
    tok = pl.BlockSpec((1, pairs, tm, LANES), lambda b, t: (b, 0, t, 0))
    n_out = 3 if with_q else 2
    return pl.pallas_call(
        functools.partial(_in_odd_kernel, with_q),
        grid=(B, L // tm),
        in_specs=[pl.BlockSpec((1, tm, D_MODEL), lambda b, t: (b, t, 0)),
                  pl.BlockSpec((1, 1, 6 * D_MODEL), lambda b, t: (mod_row(b), 0, 0)),
                  _const_spec((1, D_MODEL)),
                  _const_spec((D_MODEL, 3 * NA_WIDTH)),
                  _const_spec((1, LANES)), _const_spec((1, LANES)), _const_spec((LANES, LANES))],
        out_specs=[tok] * n_out,
        out_shape=[jax.ShapeDtypeStruct((B, pairs, L, LANES), BF16)] * n_out,
        compiler_params=_params("arbitrary", "arbitrary"),
        name="in_odd" if with_q else "in_odd_ctx",
    )(x, mod, g, w_in, q_g, k_g, hm)


def _bias_table_kernel(rb_ref, oh_ref, o_ref):
    x = rb_ref[...]
    hi = x.astype(BF16)
    r1 = x - hi.astype(F32)
    mid = r1.astype(BF16)
    lo = (r1 - mid.astype(F32)).astype(BF16)
    oh = oh_ref[...]
    o_ref[...] = ((_dot(hi, oh) + _dot(mid, oh)) + _dot(lo, oh)) * LOG2E


def _na_bias_table(rel_bias, onehot):
    H = NA_HEADS
    ndr = 2 * NA_KH - 1
    rows = H * ndr
    rb = jnp.pad(rel_bias.reshape(rows, 2 * NA_KW - 1), ((0, 0), (0, LANES - (2 * NA_KW - 1))))
    flat = pl.pallas_call(
        _bias_table_kernel,
        out_shape=jax.ShapeDtypeStruct((rows, GRID_W * GRID_W), F32),
        compiler_params=pltpu.CompilerParams(vmem_limit_bytes=VMEM_LIMIT),
        name="na_bias_table",
    )(rb, onehot)
    t = flat.reshape(H, ndr, GRID_W, GRID_W)
    return jnp.concatenate([t[:, :-1], t[:, 1:]], axis=-1)


def _na_kernel(rows_per_step, q_ref, k_ref, v_ref, kx_ref, vx_ref, bias_ref, o_ref, s_ref, p_ref, inv_ref):
    t = pl.program_id(1)
    n = NA_KH * GRID_W
    cq = lax.broadcasted_iota(jnp.int32, (GRID_W, n), 0)
    ck = lax.broadcasted_iota(jnp.int32, (GRID_W, n), 1) % GRID_W
    c0 = jnp.clip(cq - NA_KW // 2, 0, GRID_W - NA_KW)
    mask = (ck >= c0) & (ck < c0 + NA_KW)
    lo_lanes = lax.broadcasted_iota(jnp.int32, (GRID_W, LANES), 1) < HEAD_DIM

    def one_row(i, carry):
        r = t * rows_per_step + i
        r0 = jnp.clip(r - NA_KH // 2, 0, GRID_ROWS - NA_KH)
        dr0 = r0 - r + NA_KH - 1
        start = pl.multiple_of(r0 * GRID_W, GRID_W)
        qoff = pl.multiple_of(i * GRID_W, GRID_W)
        for hp in range(NA_HEADS // 2):
            qs = q_ref[0, hp, pl.ds(qoff, GRID_W), :]
            keys = jnp.concatenate([k_ref[0, hp, pl.ds(start, n), :], kx_ref[0, hp]], axis=0)
            qm = jnp.concatenate([jnp.where(lo_lanes, qs, jnp.zeros_like(qs)),
                                  jnp.where(lo_lanes, jnp.zeros_like(qs), qs)], axis=0)
            s = _dot_nt(qm, keys)
            for half in range(2):
                h = 2 * hp + half
                bias = jnp.concatenate([bias_ref[h, dr0 + 2 * p] for p in range(NA_KH // 2)], axis=1)
                rs = slice(h * GRID_W, (h + 1) * GRID_W)
                hr = slice(half * GRID_W, (half + 1) * GRID_W)
                s_ref[rs, :n] = jnp.where(mask, s[hr, :n] + bias, NEG)
                s_ref[rs, n:] = s[hr, n:]
        for c in range(NA_HEADS * GRID_W // SOFTMAX_ROWS):
            cs = slice(c * SOFTMAX_ROWS, (c + 1) * SOFTMAX_ROWS)
            s = s_ref[cs, :]
            e = jnp.exp2(s - jnp.max(s, axis=-1, keepdims=True))
            inv_ref[cs, :] = 1.0 / jnp.sum(e, axis=-1, keepdims=True)
            p_ref[cs, :] = e.astype(BF16)
        for hp in range(NA_HEADS // 2):
            vals = jnp.concatenate([v_ref[0, hp, pl.ds(start, n), :], vx_ref[0, hp]], axis=0)
            rs = slice(2 * hp * GRID_W, (2 * hp + 2) * GRID_W)
            res = _dot(p_ref[rs, :], vals) * inv_ref[rs, :]
            sl = slice(hp * LANES, (hp + 1) * LANES)
            o_ref[0, pl.ds(qoff, GRID_W), sl] = jnp.where(lo_lanes, res[:GRID_W], res[GRID_W:]).astype(BF16)
        return carry

    lax.fori_loop(0, rows_per_step, one_row, 0)


def _neighbourhood_attention(q, k, v, kx, vx, bias_tab):
    B, pairs, L, _ = q.shape
    rows_per_step = 8
    full = pl.BlockSpec((1, pairs, L, LANES), lambda b, t: (b, 0, 0, 0))
    ctx_spec = pl.BlockSpec((1, pairs, CTX_LEN, LANES), lambda b, t: (b, 0, 0, 0))
    q_rows = pl.BlockSpec((1, pairs, rows_per_step * GRID_W, LANES), lambda b, t: (b, 0, t, 0))
    n_keys = NA_KH * GRID_W + CTX_LEN
    return pl.pallas_call(
        functools.partial(_na_kernel, rows_per_step),
        grid=(B, GRID_ROWS // rows_per_step),
        in_specs=[q_rows, full, full, ctx_spec, ctx_spec, _const_spec(bias_tab.shape)],
        out_specs=pl.BlockSpec((1, rows_per_step * GRID_W, NA_WIDTH), lambda b, t: (b, t, 0)),
        out_shape=jax.ShapeDtypeStruct((B, L, NA_WIDTH), BF16),
        scratch_shapes=[pltpu.VMEM((NA_HEADS * GRID_W, n_keys), F32),
                        pltpu.VMEM((NA_HEADS * GRID_W, n_keys), BF16),
                        pltpu.VMEM((NA_HEADS * GRID_W, 1), F32)],
        compiler_params=_params("arbitrary", "arbitrary"),
        name="neighbourhood_attention",
    )(q, k, v, kx, vx, bias_tab)


def kernel(x, c, ctx, c_ctx, ada_w, ada_b, norm1_g, norm2_g, ffn_w_gate, ffn_w_up, ffn_w_down,
           ev_w_in, ev_w_out, ev_q_norm, ev_k_norm, ev_sink,
           od_w_in, od_w_out, od_q_norm, od_k_norm, od_rel_bias):
    assert x.shape == (BATCH, SEQ, D_MODEL) and ctx.shape == (BATCH, CTX_LEN, D_MODEL)
    wc, m1, m3, twc, tws, mctx = _fourier_tables()
    hm = _head_mean_matrix()
    rope_tabs = _rope_tables()
    lane_gain = lambda gvec: jnp.tile(gvec, LANES // HEAD_DIM).reshape(1, LANES)
    lat_row = lambda b: b
    ctx_row = lambda b: CTX_MOD_ROW
    tm = 512

    cs = jnp.concatenate([c, c_ctx[None, :], jnp.zeros((MOD_ROWS - BATCH - 1, D_MODEL), F32)], axis=0)
    mod = _modulation(cs, ada_w, ada_b).reshape(DEPTH, MOD_ROWS, 1, 6 * D_MODEL)

    w_in0 = ev_w_in[0].astype(BF16)
    w_out0 = ev_w_out[0].astype(BF16)
    g1 = norm1_g[0].reshape(1, D_MODEL)
    g2 = norm2_g[0].reshape(1, D_MODEL)
    ffn = (ffn_w_gate.astype(BF16), ffn_w_up.astype(BF16), ffn_w_down.astype(BF16))
    qg, kg = lane_gain(ev_q_norm[0]), lane_gain(ev_k_norm[0])
    y_l, q_l, k_l, v_l = _in_even(x, mod[0], lat_row, g1, w_in0, qg, kg, hm, wc, rope_tabs, tm)
    y_c, q_c, k_c, v_c = _in_even(ctx, mod[0], ctx_row, g1, w_in0, qg, kg, hm, wc, None, CTX_LEN)
    f_l = _fourier_latent(y_l, m1, m3, twc, tws)
    f_c = _fourier_ctx(y_c, mctx)
    a_l = _win_attention(q_l, k_l, v_l, k_c, v_c, ev_sink[0])
    a_c = _ctx_attention_even(q_c, k_c, v_c, ev_sink[0])
    x1 = _out_ffn("even_latent", (f_l, a_l), x, mod[0], lat_row, g2, w_out0, 0, *ffn, tm)
    y1 = _out_ffn("even_ctx", (f_c, a_c), ctx, mod[0], ctx_row, g2, w_out0, 0, *ffn, CTX_LEN)

    w_in1 = od_w_in[0].astype(BF16)
    w_out1 = od_w_out[0].astype(BF16)
    g1 = norm1_g[1].reshape(1, D_MODEL)
    g2 = norm2_g[1].reshape(1, D_MODEL)
    qg, kg = lane_gain(od_q_norm[0]), lane_gain(od_k_norm[0])
    q_l, k_l, v_l = _in_odd(x1, mod[1], lat_row, g1, w_in1, qg, kg, hm, True, tm)
    k_c, v_c = _in_odd(y1, mod[1], ctx_row, g1, w_in1, qg, kg, hm, False, CTX_LEN)
    bias_tab = _na_bias_table(od_rel_bias[0], _na_onehot())
    a_l = _neighbourhood_attention(q_l, k_l, v_l, k_c, v_c, bias_tab)
    return _out_ffn("odd", (a_l,), x1, mod[1], lat_row, g2, w_out1, 1, *ffn, tm)
```

```python
import functools
import math

import numpy as np
import jax
import jax.numpy as jnp
from jax import lax
from jax.experimental import pallas as pl
from jax.experimental.pallas import tpu as pltpu

D_MODEL = 1024
BATCH = 4
SEQ = 4096
DEPTH = 2
GRID_W = 64
CTX_LEN = 256
HEAD_DIM = 64
EPS = 1e-6
NEG = -1e30
FOURIER_WIDTH = D_MODEL // 2
FOURIER_GROUPS = 4
FOURIER_GROUP_CH = FOURIER_WIDTH // FOURIER_GROUPS
WIN_Q_HEADS = (D_MODEL // 2) // HEAD_DIM
WIN_KV_HEADS = 2
WIN_GROUP = WIN_Q_HEADS // WIN_KV_HEADS
WIN_RADIUS = 128
WIN_BLOCK = 128
QW = WIN_Q_HEADS * HEAD_DIM
KW = WIN_KV_HEADS * HEAD_DIM
KVD = 2 * KW
EV_IN_WIDTH = FOURIER_WIDTH + QW + 2 * KW
NA_HEADS = D_MODEL // HEAD_DIM
NA_KH = 8
NA_KW = 16
NA_WIDTH = NA_HEADS * HEAD_DIM
ROPE_THETA = 10000.0
ROPE_FREQS = HEAD_DIM // 4
D_FF = ((8 * D_MODEL // 3 + 255) // 256) * 256
GRID_ROWS = SEQ // GRID_W

LANES = 128
MOD_ROWS = 8
CTX_MOD_ROW = BATCH
FFT_N = 64
FF_CHUNK = 256
SOFTMAX_ROWS = 32
NA_UNROLL = 4
VMEM_LIMIT = 56 * 1024 * 1024

LOG2E = math.log2(math.e)
QK_SCALE = LOG2E / math.sqrt(HEAD_DIM)

BF16 = jnp.bfloat16
F32 = jnp.float32

assert DEPTH == 2 and SEQ == FFT_N * FFT_N and D_FF % FF_CHUNK == 0


def _params(*sem):
    return pltpu.CompilerParams(dimension_semantics=sem, vmem_limit_bytes=VMEM_LIMIT)


def _dot(a, b):
    return jnp.dot(a, b, preferred_element_type=F32)


def _dot_nt(a, b):
    return lax.dot_general(a, b, (((1,), (1,)), ((), ())), preferred_element_type=F32)


def _silu(x):
    return x / (1.0 + jnp.exp(-x))


def _const_spec(shape):
    nd = len(shape)
    return pl.BlockSpec(shape, lambda *_: (0,) * nd, pipeline_mode=pl.Buffered(1))


def _dft_cos_sin(n):
    idx = (np.arange(n)[:, None] * np.arange(n)[None, :]) % n
    ang = 2.0 * np.pi * idx / n
    return np.cos(ang), np.sin(ang)


def _fourier_tables():
    cc, sc = _dft_cos_sin(FOURIER_GROUP_CH)
    wc = np.concatenate([cc, -sc], axis=1) / math.sqrt(FOURIER_GROUP_CH)
    c64, s64 = _dft_cos_sin(FFT_N)
    m1 = np.concatenate([c64, -s64], axis=0) / math.sqrt(FFT_N)
    m3 = np.concatenate([c64, s64], axis=1) / math.sqrt(FFT_N)
    tw = (np.arange(FFT_N)[:, None] * np.arange(FFT_N)[None, :]) % SEQ
    tw = 2.0 * np.pi * tw / SEQ
    twc = np.repeat(np.cos(tw)[:, :, None], LANES, axis=2)
    tws = np.repeat(np.sin(tw)[:, :, None], LANES, axis=2)
    cx, sx = _dft_cos_sin(CTX_LEN)
    mctx = np.concatenate([cx, sx], axis=1) / math.sqrt(CTX_LEN)
    as32 = lambda a: jnp.asarray(a, F32)
    return (as32(wc).astype(BF16), as32(m1).astype(BF16), as32(m3).astype(BF16),
            as32(twc), as32(tws), as32(mctx).astype(BF16))


def _head_mean_matrix():
    blk = np.kron(np.eye(LANES // HEAD_DIM), np.ones((HEAD_DIM, HEAD_DIM))) / HEAD_DIM
    return jnp.asarray(blk, BF16)


def _rope_tables():
    t = jnp.arange(SEQ, dtype=jnp.int32)
    row = (t // GRID_W).astype(F32)
    col = (t % GRID_W).astype(F32)
    inv = ROPE_THETA ** (-jnp.arange(ROPE_FREQS, dtype=F32) / ROPE_FREQS)
    ang_row = row[:, None] * inv[None, :]
    ang_col = col[:, None] * inv[None, :]
    zero = jnp.zeros_like(ang_row)
    cos = jnp.concatenate([jnp.cos(ang_row)] * 2 + [jnp.cos(ang_col)] * 2, axis=1)
    sin_hi = jnp.concatenate([-jnp.sin(ang_row), zero, -jnp.sin(ang_col), zero], axis=1)
    sin_lo = jnp.concatenate([zero, jnp.sin(ang_row), zero, jnp.sin(ang_col)], axis=1)
    rep = LANES // HEAD_DIM
    return jnp.tile(cos, (1, rep)), jnp.tile(sin_hi, (1, rep)), jnp.tile(sin_lo, (1, rep))


def _na_onehot():
    cq = np.arange(GRID_W)
    dc = np.clip(cq[None, :] - cq[:, None] + NA_KW - 1, 0, 2 * NA_KW - 2)
    oh = np.zeros((LANES, GRID_W * GRID_W), np.float32)
    oh[dc.reshape(-1), np.arange(GRID_W * GRID_W)] = 1.0
    return jnp.asarray(oh, BF16)


def _mod_kernel(cs_ref, w_ref, b_ref, o_ref):
    s = _silu(cs_ref[...]).astype(BF16)
    o_ref[0] = _dot(s, w_ref[0].astype(BF16)) + b_ref[0]


def _modulation(cs, ada_w, ada_b):
    tn = 1536
    return pl.pallas_call(
        _mod_kernel,
        grid=(DEPTH, 6 * D_MODEL // tn),
        in_specs=[pl.BlockSpec((MOD_ROWS, D_MODEL), lambda i, j: (0, 0)),
                  pl.BlockSpec((1, D_MODEL, tn), lambda i, j: (i, 0, j)),
                  pl.BlockSpec((1, 1, tn), lambda i, j: (i, 0, j))],
        out_specs=pl.BlockSpec((1, MOD_ROWS, tn), lambda i, j: (i, 0, j)),
        out_shape=jax.ShapeDtypeStruct((DEPTH, MOD_ROWS, 6 * D_MODEL), F32),
        compiler_params=_params("arbitrary", "arbitrary"),
        name="ada_modulation",
    )(cs, ada_w, ada_b.reshape(DEPTH, 1, 6 * D_MODEL))


def _mod_slice(mod_ref, k):
    return mod_ref[0, :, k * D_MODEL:(k + 1) * D_MODEL]


def _rms_mod(x, g, scale, shift):
    y = x * lax.rsqrt(jnp.mean(x * x, axis=-1, keepdims=True) + EPS)
    return (y * g) * (1.0 + scale) + shift


def _head_rms(t, gain, hm):
    ms = _dot((t * t).astype(BF16), hm)
    return t * lax.rsqrt(ms + EPS) * gain


def _rope(t, cos, sin_hi, sin_lo):
    up = pltpu.roll(t, LANES - ROPE_FREQS, axis=1)
    dn = pltpu.roll(t, ROPE_FREQS, axis=1)
    return t * cos + up * sin_hi + dn * sin_lo


def _in_even_kernel(rope, x_ref, mod_ref, g_ref, w_ref, qg_ref, kg_ref, hm_ref, wc_ref, *rest):
    if rope:
        cos_ref, sh_ref, sl_ref, y_ref, q_ref, k_ref, v_ref = rest
    else:
        y_ref, q_ref, k_ref, v_ref = rest
    h = _rms_mod(x_ref[0], g_ref[...], _mod_slice(mod_ref, 1), _mod_slice(mod_ref, 0)).astype(BF16)
    hm = hm_ref[...]
    F = FOURIER_WIDTH
    f = _dot(h, w_ref[:, :F]).astype(BF16)
    for g in range(FOURIER_GROUPS):
        yg = _dot(f[:, g * LANES:(g + 1) * LANES], wc_ref[...])
        y_ref[0, :, g * LANES:(g + 1) * LANES] = yg[:, :LANES].astype(y_ref.dtype)
        y_ref[0, :, F + g * LANES:F + (g + 1) * LANES] = yg[:, LANES:].astype(y_ref.dtype)

    def qk(t, gain):
        t = _head_rms(t, gain, hm)
        if rope:
            t = _rope(t, cos_ref[...], sh_ref[...], sl_ref[...])
        return t

    q = _dot(h, w_ref[:, F:F + QW])
    for j in range(QW // LANES):
        qj = qk(q[:, j * LANES:(j + 1) * LANES], qg_ref[...]) * QK_SCALE
        q_ref[0, :, j * LANES:(j + 1) * LANES] = qj.astype(BF16)
    kv = _dot(h, w_ref[:, F + QW:])
    lo_lanes = lax.broadcasted_iota(jnp.int32, (1, LANES), 1) < HEAD_DIM

    def dup_heads(t):
        sw = pltpu.roll(t, HEAD_DIM, axis=1)
        return jnp.concatenate([jnp.where(lo_lanes, t, sw), jnp.where(lo_lanes, sw, t)], axis=1)

    k_ref[0] = dup_heads(qk(kv[:, :KW], kg_ref[...])).astype(BF16)
    v_ref[0] = dup_heads(kv[:, KW:]).astype(BF16)


def _in_even(x, mod, mod_row, g, w_in, q_g, k_g, hm, wc, rope_tabs, tm):
    B, L, _ = x.shape
    rope = rope_tabs is not None
    in_specs = [pl.BlockSpec((1, tm, D_MODEL), lambda b, t: (b, t, 0)),
                pl.BlockSpec((1, 1, 6 * D_MODEL), lambda b, t: (mod_row(b), 0, 0)),
                _const_spec((1, D_MODEL)),
                _const_spec((D_MODEL, EV_IN_WIDTH)),
                _const_spec((1, LANES)), _const_spec((1, LANES)),
                _const_spec((LANES, LANES)), _const_spec((LANES, 2 * LANES))]
    args = [x, mod, g, w_in, q_g, k_g, hm, wc]
    if rope:
        in_specs += [pl.BlockSpec((tm, LANES), lambda b, t: (t, 0))] * 3
        args += list(rope_tabs)
    return pl.pallas_call(
        functools.partial(_in_even_kernel, rope),
        grid=(B, L // tm),
        in_specs=in_specs,
        out_specs=[pl.BlockSpec((1, tm, 2 * FOURIER_WIDTH), lambda b, t: (b, t, 0)),
                   pl.BlockSpec((1, tm, QW), lambda b, t: (b, t, 0)),
                   pl.BlockSpec((1, tm, KVD), lambda b, t: (b, t, 0)),
                   pl.BlockSpec((1, tm, KVD), lambda b, t: (b, t, 0))],
        out_shape=[jax.ShapeDtypeStruct((B, L, 2 * FOURIER_WIDTH), BF16),
                   jax.ShapeDtypeStruct((B, L, QW), BF16),
                   jax.ShapeDtypeStruct((B, L, KVD), BF16),
                   jax.ShapeDtypeStruct((B, L, KVD), BF16)],
        compiler_params=_params("arbitrary", "arbitrary"),
        name="in_even_rope" if rope else "in_even_ctx",
    )(*args)


def _fft_stage1_kernel(n2t, y_ref, m1_ref, twc_ref, tws_ref, o_ref):
    F = FOURIER_WIDTH
    for j in range(n2t):
        a = _dot(m1_ref[...], y_ref[0, :, j * 2 * F:(j + 1) * 2 * F])
        top, bot = a[:FFT_N], a[FFT_N:]
        ar = top[:, :F] - bot[:, F:]
        ai = top[:, F:] + bot[:, :F]
        tc = jnp.tile(twc_ref[j], (1, F // LANES))
        ts = jnp.tile(tws_ref[j], (1, F // LANES))
        o_ref[0, 0, :, j * F:(j + 1) * F] = (ar * tc + ai * ts).astype(BF16)
        o_ref[0, 1, :, j * F:(j + 1) * F] = (ai * tc - ar * ts).astype(BF16)


def _fft_stage2_kernel(k1t, b_ref, m3_ref, o_ref):
    for j in range(k1t):
        rhs = jnp.concatenate([b_ref[0, 0, j], b_ref[0, 1, j]], axis=0)
        o_ref[0, j] = _dot(m3_ref[...], rhs)


def _fourier_latent(y, m1, m3, twc, tws):
    B = y.shape[0]
    F = FOURIER_WIDTH
    n2t = 8
    yv = y.reshape(B, FFT_N, FFT_N * 2 * F)
    bh = pl.pallas_call(
        functools.partial(_fft_stage1_kernel, n2t),
        grid=(B, FFT_N // n2t),
        in_specs=[pl.BlockSpec((1, FFT_N, n2t * 2 * F), lambda b, t: (b, 0, t)),
                  _const_spec((2 * FFT_N, FFT_N)),
                  pl.BlockSpec((n2t, FFT_N, LANES), lambda b, t: (t, 0, 0)),
                  pl.BlockSpec((n2t, FFT_N, LANES), lambda b, t: (t, 0, 0))],
        out_specs=pl.BlockSpec((1, 2, FFT_N, n2t * F), lambda b, t: (b, 0, 0, t)),
        out_shape=jax.ShapeDtypeStruct((B, 2, FFT_N, FFT_N * F), BF16),
        compiler_params=_params("arbitrary", "arbitrary"),
        name="fft_stage1",
    )(yv, m1, twc, tws)
    k1t = 8
    bv = bh.reshape(B, 2, FFT_N, FFT_N, F)
    return pl.pallas_call(
        functools.partial(_fft_stage2_kernel, k1t),
        grid=(B, FFT_N // k1t),
        in_specs=[pl.BlockSpec((1, 2, k1t, FFT_N, F), lambda b, t: (b, 0, t, 0, 0)),
                  _const_spec((FFT_N, 2 * FFT_N))],
        out_specs=pl.BlockSpec((1, k1t, FFT_N, F), lambda b, t: (b, t, 0, 0)),
        out_shape=jax.ShapeDtypeStruct((B, FFT_N, FFT_N, F), F32),
        compiler_params=_params("arbitrary", "arbitrary"),
        name="fft_stage2",
    )(bv, m3)


def _fourier_ctx_kernel(y_ref, m_ref, o_ref):
    F = FOURIER_WIDTH
    rhs = jnp.concatenate([y_ref[0, :, :F], y_ref[0, :, F:]], axis=0)
    o_ref[0] = _dot(m_ref[...], rhs)


def _fourier_ctx(y, mctx):
    B = y.shape[0]
    return pl.pallas_call(
        _fourier_ctx_kernel,
        grid=(B,),
        in_specs=[pl.BlockSpec((1, CTX_LEN, 2 * FOURIER_WIDTH), lambda b: (b, 0, 0)),
                  _const_spec((CTX_LEN, 2 * CTX_LEN))],
        out_specs=pl.BlockSpec((1, CTX_LEN, FOURIER_WIDTH), lambda b: (b, 0, 0)),
        out_shape=jax.ShapeDtypeStruct((B, CTX_LEN, FOURIER_WIDTH), F32),
        compiler_params=_params("arbitrary"),
        name="fourier_ctx",
    )(y, mctx)


def _gqa_group(q_slabs, keys, vals, masks, sinks, rows, o_ref, first_slab, scratch):
    s_ref, p_ref, inv_ref = scratch
    lo_lanes = lax.broadcasted_iota(jnp.int32, (rows, LANES), 1) < HEAD_DIM
    zero = jnp.zeros((rows, LANES), BF16)
    qs = jnp.concatenate([jnp.where(lo_lanes if half == 0 else ~lo_lanes, qslab, zero)
                          for qslab in q_slabs for half in range(2)], axis=0)
    s_ref[...] = _dot_nt(qs, keys)
    for c in range(WIN_GROUP * rows // SOFTMAX_ROWS):
        rs = slice(c * SOFTMAX_ROWS, (c + 1) * SOFTMAX_ROWS)
        s = s_ref[rs, :]
        if masks is not None:
            mask, w = masks
            r0 = (c * SOFTMAX_ROWS) % rows
            s = jnp.concatenate([jnp.where(mask[r0:r0 + SOFTMAX_ROWS], s[:, :w], NEG), s[:, w:]], axis=1)
        sk = sinks[(c * SOFTMAX_ROWS) // rows]
        m = jnp.maximum(jnp.max(s, axis=-1, keepdims=True), sk)
        e = jnp.exp2(s - m)
        inv_ref[rs, :] = 1.0 / (jnp.sum(e, axis=-1, keepdims=True) + jnp.exp2(sk - m))
        p_ref[rs, :] = e.astype(BF16)
    o = _dot(p_ref[...], vals) * inv_ref[...]
    for i in range(WIN_GROUP // 2):
        a = o[(2 * i) * rows:(2 * i + 1) * rows]
        b = o[(2 * i + 1) * rows:(2 * i + 2) * rows]
        sl = slice((first_slab + i) * LANES, (first_slab + i + 1) * LANES)
        o_ref[0, :, sl] = jnp.where(lo_lanes, a, b).astype(BF16)


def _gqa_scratch(rows, n_keys):
    one = [pltpu.VMEM((WIN_GROUP * rows, n_keys), F32), pltpu.VMEM((WIN_GROUP * rows, n_keys), BF16),
           pltpu.VMEM((WIN_GROUP * rows, 1), F32)]
    return one * WIN_KV_HEADS


def _gqa_all(q_ref, keys, vals, masks, sink_ref, rows, o_ref, scratch):
    slabs_per_group = WIN_GROUP // 2
    for kv in range(WIN_KV_HEADS):
        q_slabs = [q_ref[0, :, (kv * slabs_per_group + i) * LANES:(kv * slabs_per_group + i + 1) * LANES]
                   for i in range(slabs_per_group)]
        sinks = [sink_ref[kv * WIN_GROUP + g] * LOG2E for g in range(WIN_GROUP)]
        ksl = slice(kv * LANES, (kv + 1) * LANES)
        _gqa_group(q_slabs, keys[:, ksl], vals[:, ksl], masks, sinks, rows, o_ref, kv * slabs_per_group,
                   scratch[3 * kv:3 * kv + 3])


def _window_band():
    i = np.arange(WIN_BLOCK)[:, None]
    j = np.arange(3 * WIN_BLOCK)[None, :]
    return jnp.asarray((np.abs(j - WIN_BLOCK - i) <= WIN_RADIUS).astype(np.float32))


def _win_attn_kernel(sink_ref, band_ref, q_ref, kp_ref, kc_ref, kn_ref, vp_ref, vc_ref, vn_ref, kx_ref, vx_ref, o_ref,
                     *scratch):
    n = pl.program_id(1)
    nw = 3 * WIN_BLOCK
    keys = jnp.concatenate([kp_ref[0], kc_ref[0], kn_ref[0], kx_ref[0]], axis=0)
    vals = jnp.concatenate([vp_ref[0], vc_ref[0], vn_ref[0], vx_ref[0]], axis=0)
    has_prev = (n > 0).astype(F32)
    has_next = (n < pl.num_programs(1) - 1).astype(F32)
    band = band_ref[...]
    valid = jnp.concatenate([band[:, :WIN_BLOCK] * has_prev, band[:, WIN_BLOCK:2 * WIN_BLOCK],
                             band[:, 2 * WIN_BLOCK:] * has_next], axis=1) > 0.0
    _gqa_all(q_ref, keys, vals, (valid, nw), sink_ref, WIN_BLOCK, o_ref, scratch)


def _win_attention(q, k, v, kx, vx, sink):
    B, L, _ = q.shape
    nb = L // WIN_BLOCK
    blk = lambda f: pl.BlockSpec((1, WIN_BLOCK, KVD), f)
    prev = lambda b, n: (b, jnp.maximum(n - 1, 0), 0)
    cur = lambda b, n: (b, n, 0)
    nxt = lambda b, n: (b, jnp.minimum(n + 1, nb - 1), 0)
    ctx_spec = pl.BlockSpec((1, CTX_LEN, KVD), lambda b, n: (b, 0, 0))
    return pl.pallas_call(
        _win_attn_kernel,
        grid=(B, nb),
        in_specs=[pl.BlockSpec(memory_space=pltpu.SMEM),
                  _const_spec((WIN_BLOCK, 3 * WIN_BLOCK)),
                  pl.BlockSpec((1, WIN_BLOCK, QW), cur),
                  blk(prev), blk(cur), blk(nxt), blk(prev), blk(cur), blk(nxt), ctx_spec, ctx_spec],
        out_specs=pl.BlockSpec((1, WIN_BLOCK, QW), cur),
        out_shape=jax.ShapeDtypeStruct((B, L, QW), BF16),
        scratch_shapes=_gqa_scratch(WIN_BLOCK, 3 * WIN_BLOCK + CTX_LEN),
        compiler_params=_params("arbitrary", "arbitrary"),
        name="window_attention",
    )(sink, _window_band(), q, k, k, k, v, v, v, kx, vx)


def _ctx_attn_even_kernel(sink_ref, q_ref, k_ref, v_ref, o_ref, *scratch):
    _gqa_all(q_ref, k_ref[0], v_ref[0], None, sink_ref, CTX_LEN, o_ref, scratch)


def _ctx_attention_even(q, k, v, sink):
    B = q.shape[0]
    return pl.pallas_call(
        _ctx_attn_even_kernel,
        grid=(B,),
        in_specs=[pl.BlockSpec(memory_space=pltpu.SMEM),
                  pl.BlockSpec((1, CTX_LEN, QW), lambda b: (b, 0, 0)),
                  pl.BlockSpec((1, CTX_LEN, KVD), lambda b: (b, 0, 0)),
                  pl.BlockSpec((1, CTX_LEN, KVD), lambda b: (b, 0, 0))],
        out_specs=pl.BlockSpec((1, CTX_LEN, QW), lambda b: (b, 0, 0)),
        out_shape=jax.ShapeDtypeStruct((B, CTX_LEN, QW), BF16),
        scratch_shapes=_gqa_scratch(CTX_LEN, CTX_LEN),
        compiler_params=_params("arbitrary"),
        name="ctx_attention_even",
    )(sink, q, k, v)


def _out_ffn_kernel(mode, *refs):
    if mode == "even_latent":
        f_ref, a_ref, x_ref, mod_ref, g_ref, wo_ref, wg_ref, wu_ref, wd_ref, o_ref = refs
        fm = jnp.concatenate([f_ref[0, :, j, :] for j in range(f_ref.shape[2])], axis=0).astype(BF16)
        o = _dot(fm, wo_ref[:FOURIER_WIDTH]) + _dot(a_ref[0], wo_ref[FOURIER_WIDTH:])
    elif mode == "even_ctx":
        f_ref, a_ref, x_ref, mod_ref, g_ref, wo_ref, wg_ref, wu_ref, wd_ref, o_ref = refs
        o = _dot(f_ref[0].astype(BF16), wo_ref[:FOURIER_WIDTH]) + _dot(a_ref[0], wo_ref[FOURIER_WIDTH:])
    else:
        a_ref, x_ref, mod_ref, g_ref, wo_ref, wg_ref, wu_ref, wd_ref, o_ref = refs
        o = _dot(jnp.concatenate([a_ref[0, j] for j in range(a_ref.shape[1])], axis=1), wo_ref[...])
    x1 = x_ref[0] + _mod_slice(mod_ref, 2) * o
    h = _rms_mod(x1, g_ref[...], _mod_slice(mod_ref, 4), _mod_slice(mod_ref, 3)).astype(BF16)
    acc = jnp.zeros_like(x1)
    for c in range(D_FF // FF_CHUNK):
        cs = slice(c * FF_CHUNK, (c + 1) * FF_CHUNK)
        a = _silu(_dot(h, wg_ref[0, :, cs])) * _dot(h, wu_ref[0, :, cs])
        acc = acc + _dot(a.astype(BF16), wd_ref[0, cs, :])
    o_ref[0] = x1 + _mod_slice(mod_ref, 5) * acc


def _out_ffn(mode, mix, x, mod, mod_row, g, w_out, layer, wg, wu, wd, tm):
    B, L, _ = x.shape
    tok = lambda w: pl.BlockSpec((1, tm, w), lambda b, t: (b, t, 0))
    ffn_spec = lambda r, c: pl.BlockSpec((1, r, c), lambda b, t: (layer, 0, 0), pipeline_mode=pl.Buffered(1))
    if mode == "even_latent":
        assert tm % FFT_N == 0
        mix_specs = [pl.BlockSpec((1, FFT_N, tm // FFT_N, FOURIER_WIDTH), lambda b, t: (b, 0, t, 0)), tok(QW)]
    elif mode == "even_ctx":
        mix_specs = [tok(FOURIER_WIDTH), tok(QW)]
    else:
        mix_specs = [pl.BlockSpec((1, NA_WIDTH // LANES, tm, LANES), lambda b, t: (b, 0, t, 0))]
    return pl.pallas_call(
        functools.partial(_out_ffn_kernel, mode),
        grid=(B, L // tm),
        in_specs=mix_specs + [tok(D_MODEL),
                              pl.BlockSpec((1, 1, 6 * D_MODEL), lambda b, t: (mod_row(b), 0, 0)),
                              _const_spec((1, D_MODEL)),
                              _const_spec((D_MODEL, D_MODEL)),
                              ffn_spec(D_MODEL, D_FF), ffn_spec(D_MODEL, D_FF), ffn_spec(D_FF, D_MODEL)],
        out_specs=tok(D_MODEL),
        out_shape=jax.ShapeDtypeStruct((B, L, D_MODEL), F32),
        compiler_params=_params("arbitrary", "arbitrary"),
        name="out_ffn_" + mode,
    )(*mix, x, mod, g, w_out, wg, wu, wd)


def _in_odd_kernel(with_q, x_ref, mod_ref, g_ref, w_ref, qg_ref, kg_ref, hm_ref, *outs):
    h = _rms_mod(x_ref[0], g_ref[...], _mod_slice(mod_ref, 1), _mod_slice(mod_ref, 0)).astype(BF16)
    hm = hm_ref[...]
    W = NA_WIDTH
    if with_q:
        q_ref, k_ref, v_ref = outs
        q = _dot(h, w_ref[:, :W])
        for j in range(W // LANES):
            qj = _head_rms(q[:, j * LANES:(j + 1) * LANES], qg_ref[...], hm) * QK_SCALE
            q_ref[0, j] = qj.astype(BF16)
    else:
        k_ref, v_ref = outs
    k = _dot(h, w_ref[:, W:2 * W])
    for j in range(W // LANES):
        kj = _head_rms(k[:, j * LANES:(j + 1) * LANES], kg_ref[...], hm)
        k_ref[0, j] = kj.astype(BF16)
    v = _dot(h, w_ref[:, 2 * W:]).astype(BF16)
    for j in range(W // LANES):
        v_ref[0, j] = v[:, j * LANES:(j + 1) * LANES]


def _in_odd(x, mod, mod_row, g, w_in, q_g, k_g, hm, with_q, tm):
    B, L, _ = x.shape
    pairs = NA_WIDTH // LANES
    tok = pl.BlockSpec((1, pairs, tm, LANES), lambda b, t: (b, 0, t, 0))
    n_out = 3 if with_q else 2
    return pl.pallas_call(
        functools.partial(_in_odd_kernel, with_q),
        grid=(B, L // tm),
        in_specs=[pl.BlockSpec((1, tm, D_MODEL), lambda b, t: (b, t, 0)),
                  pl.BlockSpec((1, 1, 6 * D_MODEL), lambda b, t: (mod_row(b), 0, 0)),
                  _const_spec((1, D_MODEL)),
                  _const_spec((D_MODEL, 3 * NA_WIDTH)),
                  _const_spec((1, LANES)), _const_spec((1, LANES)), _const_spec((LANES, LANES))],
        out_specs=[tok] * n_out,
        out_shape=[jax.ShapeDtypeStruct((B, pairs, L, LANES), BF16)] * n_out,
        compiler_params=_params("arbitrary", "arbitrary"),
        name="in_odd" if with_q else "in_odd_ctx",
    )(x, mod, g, w_in, q_g, k_g, hm)


def _bias_table_kernel(rb_ref, oh_ref, o_ref):
    x = rb_ref[...]
    hi = x.astype(BF16)
    r1 = x - hi.astype(F32)
    mid = r1.astype(BF16)
    lo = (r1 - mid.astype(F32)).astype(BF16)
    oh = oh_ref[...]
    o_ref[...] = ((_dot(hi, oh) + _dot(mid, oh)) + _dot(lo, oh)) * LOG2E


def _na_bias_table(rel_bias, onehot):
    H = NA_HEADS
    ndr = 2 * NA_KH - 1
    rows = H * ndr
    rb = jnp.pad(rel_bias.reshape(rows, 2 * NA_KW - 1), ((0, 0), (0, LANES - (2 * NA_KW - 1))))
    flat = pl.pallas_call(
        _bias_table_kernel,
        out_shape=jax.ShapeDtypeStruct((rows, GRID_W * GRID_W), F32),
        compiler_params=pltpu.CompilerParams(vmem_limit_bytes=VMEM_LIMIT),
        name="na_bias_table",
    )(rb, onehot)
    t = flat.reshape(H, ndr, GRID_W, GRID_W)
    return jnp.concatenate([t[:, :-1], t[:, 1:]], axis=-1)


def _na_kernel(rows_per_step, q_ref, k_ref, v_ref, kx_ref, vx_ref, bias_ref, o_ref, *scratch):
    s_refs, p_refs, inv_refs = scratch[0:2], scratch[2:4], scratch[4:6]
    t = pl.program_id(1)
    n = NA_KH * GRID_W
    pairs = NA_HEADS // 2
    units = rows_per_step * pairs
    cq = lax.broadcasted_iota(jnp.int32, (GRID_W, n), 0)
    ck = lax.broadcasted_iota(jnp.int32, (GRID_W, n), 1) % GRID_W
    c0 = jnp.clip(cq - NA_KW // 2, 0, GRID_W - NA_KW)
    mask = (ck >= c0) & (ck < c0 + NA_KW)
    lo_lanes = lax.broadcasted_iota(jnp.int32, (GRID_W, LANES), 1) < HEAD_DIM

    def unit(u):
        i, hp = u // pairs, u % pairs
        r = t * rows_per_step + i
        r0 = jnp.clip(r - NA_KH // 2, 0, GRID_ROWS - NA_KH)
        return hp, r0 - r + NA_KH - 1, pl.multiple_of(r0 * GRID_W, GRID_W), pl.multiple_of(i * GRID_W, GRID_W)

    def scores(u, slot):
        hp, dr0, start, qoff = unit(u)
        qs = q_ref[0, hp, pl.ds(qoff, GRID_W), :]
        keys = jnp.concatenate([k_ref[0, hp, pl.ds(start, n), :], kx_ref[0, hp]], axis=0)
        qm = jnp.concatenate([jnp.where(lo_lanes, qs, jnp.zeros_like(qs)),
                              jnp.where(lo_lanes, jnp.zeros_like(qs), qs)], axis=0)
        s = _dot_nt(qm, keys)
        for half in range(2):
            bias = jnp.concatenate([bias_ref[2 * hp + half, dr0 + 2 * p] for p in range(NA_KH // 2)], axis=1)
            hr = slice(half * GRID_W, (half + 1) * GRID_W)
            s_refs[slot][hr, :n] = jnp.where(mask, s[hr, :n] + bias, NEG)
            s_refs[slot][hr, n:] = s[hr, n:]

    def softmax(slot):
        for c in range(2 * GRID_W // SOFTMAX_ROWS):
            cs = slice(c * SOFTMAX_ROWS, (c + 1) * SOFTMAX_ROWS)
            s = s_refs[slot][cs, :]
            e = jnp.exp2(s - jnp.max(s, axis=-1, keepdims=True))
            inv_refs[slot][cs, :] = 1.0 / jnp.sum(e, axis=-1, keepdims=True)
            p_refs[slot][cs, :] = e.astype(BF16)

    def values(u, slot):
        hp, _, start, qoff = unit(u)
        vals = jnp.concatenate([v_ref[0, hp, pl.ds(start, n), :], vx_ref[0, hp]], axis=0)
        res = _dot(p_refs[slot][...], vals) * inv_refs[slot][...]
        o_ref[0, hp, pl.ds(qoff, GRID_W), :] = jnp.where(lo_lanes, res[:GRID_W], res[GRID_W:]).astype(BF16)

    def stage(u, slot):
        values(u - 1, 1 - slot)
        softmax(slot)
        scores(jnp.minimum(u + 1, units - 1), 1 - slot)

    scores(0, 0)
    softmax(0)
    scores(1, 1)

    def two_units(j, carry):
        stage(2 * j + 1, 1)
        stage(2 * j + 2, 0)
        return carry

    lax.fori_loop(0, (units - 2) // 2, two_units, 0, unroll=NA_UNROLL)
    stage(units - 1, 1)
    values(units - 1, 1)


def _neighbourhood_attention(q, k, v, kx, vx, bias_tab):
    B, pairs, L, _ = q.shape
    rows_per_step = 8
    full = pl.BlockSpec((1, pairs, L, LANES), lambda b, t: (b, 0, 0, 0))
    ctx_spec = pl.BlockSpec((1, pairs, CTX_LEN, LANES), lambda b, t: (b, 0, 0, 0))
    q_rows = pl.BlockSpec((1, pairs, rows_per_step * GRID_W, LANES), lambda b, t: (b, 0, t, 0))
    n_keys = NA_KH * GRID_W + CTX_LEN
    unit_rows = 2 * GRID_W
    return pl.pallas_call(
        functools.partial(_na_kernel, rows_per_step),
        grid=(B, GRID_ROWS // rows_per_step),
        in_specs=[q_rows, full, full, ctx_spec, ctx_spec, _const_spec(bias_tab.shape)],
        out_specs=q_rows,
        out_shape=jax.ShapeDtypeStruct((B, pairs, L, LANES), BF16),
        scratch_shapes=([pltpu.VMEM((unit_rows, n_keys), F32)] * 2 + [pltpu.VMEM((unit_rows, n_keys), BF16)] * 2
                        + [pltpu.VMEM((unit_rows, 1), F32)] * 2),
        compiler_params=_params("arbitrary", "arbitrary"),
        name="neighbourhood_attention",
    )(q, k, v, kx, vx, bias_tab)


def kernel(x, c, ctx, c_ctx, ada_w, ada_b, norm1_g, norm2_g, ffn_w_gate, ffn_w_up, ffn_w_down,
           ev_w_in, ev_w_out, ev_q_norm, ev_k_norm, ev_sink,
           od_w_in, od_w_out, od_q_norm, od_k_norm, od_rel_bias):
    assert x.shape == (BATCH, SEQ, D_MODEL) and ctx.shape == (BATCH, CTX_LEN, D_MODEL)
    wc, m1, m3, twc, tws, mctx = _fourier_tables()
    hm = _head_mean_matrix()
    rope_tabs = _rope_tables()
    lane_gain = lambda gvec: jnp.tile(gvec, LANES // HEAD_DIM).reshape(1, LANES)
    lat_row = lambda b: b
    ctx_row = lambda b: CTX_MOD_ROW
    tm = 512

    cs = jnp.concatenate([c, c_ctx[None, :], jnp.zeros((MOD_ROWS - BATCH - 1, D_MODEL), F32)], axis=0)
    mod = _modulation(cs, ada_w, ada_b).reshape(DEPTH, MOD_ROWS, 1, 6 * D_MODEL)

    w_in0 = ev_w_in[0].astype(BF16)
    w_out0 = ev_w_out[0].astype(BF16)
    g1 = norm1_g[0].reshape(1, D_MODEL)
    g2 = norm2_g[0].reshape(1, D_MODEL)
    ffn = (ffn_w_gate.astype(BF16), ffn_w_up.astype(BF16), ffn_w_down.astype(BF16))
    qg, kg = lane_gain(ev_q_norm[0]), lane_gain(ev_k_norm[0])
    y_l, q_l, k_l, v_l = _in_even(x, mod[0], lat_row, g1, w_in0, qg, kg, hm, wc, rope_tabs, tm)
    y_c, q_c, k_c, v_c = _in_even(ctx, mod[0], ctx_row, g1, w_in0, qg, kg, hm, wc, None, CTX_LEN)
    f_l = _fourier_latent(y_l, m1, m3, twc, tws)
    f_c = _fourier_ctx(y_c, mctx)
    a_l = _win_attention(q_l, k_l, v_l, k_c, v_c, ev_sink[0])
    a_c = _ctx_attention_even(q_c, k_c, v_c, ev_sink[0])
    x1 = _out_ffn("even_latent", (f_l, a_l), x, mod[0], lat_row, g2, w_out0, 0, *ffn, tm)
    y1 = _out_ffn("even_ctx", (f_c, a_c), ctx, mod[0], ctx_row, g2, w_out0, 0, *ffn, CTX_LEN)

    w_in1 = od_w_in[0].astype(BF16)
    w_out1 = od_w_out[0].astype(BF16)
    g1 = norm1_g[1].reshape(1, D_MODEL)
    g2 = norm2_g[1].reshape(1, D_MODEL)
    qg, kg = lane_gain(od_q_norm[0]), lane_gain(od_k_norm[0])
    q_l, k_l, v_l = _in_odd(x1, mod[1], lat_row, g1, w_in1, qg, kg, hm, True, tm)
    k_c, v_c = _in_odd(y1, mod[1], ctx_row, g1, w_in1, qg, kg, hm, False, CTX_LEN)
    bias_tab = _na_bias_table(od_rel_bias[0], _na_onehot())
    a_l = _neighbourhood_attention(q_l, k_l, v_l, k_c, v_c, bias_tab)
    return _out_ffn("odd", (a_l,), x1, mod[1], lat_row, g2, w_out1, 1, *ffn, tm)
```

```python
import functools
import math

import numpy as np
import jax
import jax.numpy as jnp
from jax import lax
from jax.experimental import pallas as pl
from jax.experimental.pallas import tpu as pltpu

D_MODEL = 1024
BATCH = 4
SEQ = 4096
DEPTH = 2
GRID_W = 64
CTX_LEN = 256
HEAD_DIM = 64
EPS = 1e-6
NEG = -1e30
FOURIER_WIDTH = D_MODEL // 2
FOURIER_GROUPS = 4
FOURIER_GROUP_CH = FOURIER_WIDTH // FOURIER_GROUPS
WIN_Q_HEADS = (D_MODEL // 2) // HEAD_DIM
WIN_KV_HEADS = 2
WIN_GROUP = WIN_Q_HEADS // WIN_KV_HEADS
WIN_RADIUS = 128
WIN_BLOCK = 128
QW = WIN_Q_HEADS * HEAD_DIM
KW = WIN_KV_HEADS * HEAD_DIM
KVD = 2 * KW
EV_IN_WIDTH = FOURIER_WIDTH + QW + 2 * KW
NA_HEADS = D_MODEL // HEAD_DIM
NA_KH = 8
NA_KW = 16
NA_WIDTH = NA_HEADS * HEAD_DIM
ROPE_THETA = 10000.0
ROPE_FREQS = HEAD_DIM // 4
D_FF = ((8 * D_MODEL // 3 + 255) // 256) * 256
GRID_ROWS = SEQ // GRID_W

LANES = 128
MOD_ROWS = 8
CTX_MOD_ROW = BATCH
FFT_N = 64
FF_CHUNK = 256
SOFTMAX_ROWS = 32
RMS_W = 256
VMEM_LIMIT = 56 * 1024 * 1024

LOG2E = math.log2(math.e)
QK_SCALE = LOG2E / math.sqrt(HEAD_DIM)

BF16 = jnp.bfloat16
F32 = jnp.float32

assert DEPTH == 2 and SEQ == FFT_N * FFT_N and D_FF % FF_CHUNK == 0


def _params(*sem):
    return pltpu.CompilerParams(dimension_semantics=sem, vmem_limit_bytes=VMEM_LIMIT)


def _dot(a, b):
    return jnp.dot(a, b, preferred_element_type=F32)


def _dot_nt(a, b):
    return lax.dot_general(a, b, (((1,), (1,)), ((), ())), preferred_element_type=F32)


def _silu(x):
    return x / (1.0 + jnp.exp(-x))


def _const_spec(shape):
    nd = len(shape)
    return pl.BlockSpec(shape, lambda *_: (0,) * nd, pipeline_mode=pl.Buffered(1))


def _dft_cos_sin(n):
    idx = (np.arange(n)[:, None] * np.arange(n)[None, :]) % n
    ang = 2.0 * np.pi * idx / n
    return np.cos(ang), np.sin(ang)


def _fourier_tables():
    cc, sc = _dft_cos_sin(FOURIER_GROUP_CH)
    wc = np.concatenate([cc, -sc], axis=1) / math.sqrt(FOURIER_GROUP_CH)
    c64, s64 = _dft_cos_sin(FFT_N)
    m1 = np.concatenate([c64, -s64], axis=0) / math.sqrt(FFT_N)
    m3 = np.concatenate([c64, s64], axis=1) / math.sqrt(FFT_N)
    tw = (np.arange(FFT_N)[:, None] * np.arange(FFT_N)[None, :]) % SEQ
    tw = 2.0 * np.pi * tw / SEQ
    twc = np.repeat(np.cos(tw)[:, :, None], LANES, axis=2)
    tws = np.repeat(np.sin(tw)[:, :, None], LANES, axis=2)
    cx, sx = _dft_cos_sin(CTX_LEN)
    mctx = np.concatenate([cx, sx], axis=1) / math.sqrt(CTX_LEN)
    as32 = lambda a: jnp.asarray(a, F32)
    return (as32(wc).astype(BF16), as32(m1).astype(BF16), as32(m3).astype(BF16),
            as32(twc), as32(tws), as32(mctx).astype(BF16))


def _head_mean_matrix():
    blk = np.kron(np.eye(RMS_W // HEAD_DIM), np.ones((HEAD_DIM, HEAD_DIM))) / HEAD_DIM
    return jnp.asarray(blk, BF16)


def _rope_tables():
    t = jnp.arange(SEQ, dtype=jnp.int32)
    row = (t // GRID_W).astype(F32)
    col = (t % GRID_W).astype(F32)
    inv = ROPE_THETA ** (-jnp.arange(ROPE_FREQS, dtype=F32) / ROPE_FREQS)
    ang_row = row[:, None] * inv[None, :]
    ang_col = col[:, None] * inv[None, :]
    zero = jnp.zeros_like(ang_row)
    cos = jnp.concatenate([jnp.cos(ang_row)] * 2 + [jnp.cos(ang_col)] * 2, axis=1)
    sin_hi = jnp.concatenate([-jnp.sin(ang_row), zero, -jnp.sin(ang_col), zero], axis=1)
    sin_lo = jnp.concatenate([zero, jnp.sin(ang_row), zero, jnp.sin(ang_col)], axis=1)
    rep = LANES // HEAD_DIM
    return jnp.tile(cos, (1, rep)), jnp.tile(sin_hi, (1, rep)), jnp.tile(sin_lo, (1, rep))


def _na_onehot():
    cq = np.arange(GRID_W)
    dc = np.clip(cq[None, :] - cq[:, None] + NA_KW - 1, 0, 2 * NA_KW - 2)
    oh = np.zeros((LANES, GRID_W * GRID_W), np.float32)
    oh[dc.reshape(-1), np.arange(GRID_W * GRID_W)] = 1.0
    return jnp.asarray(oh, BF16)


def _mod_kernel(cs_ref, w_ref, b_ref, o_ref):
    s = _silu(cs_ref[...]).astype(BF16)
    o_ref[0] = _dot(s, w_ref[0].astype(BF16)) + b_ref[0]


def _modulation(cs, ada_w, ada_b):
    tn = 1536
    return pl.pallas_call(
        _mod_kernel,
        grid=(DEPTH, 6 * D_MODEL // tn),
        in_specs=[pl.BlockSpec((MOD_ROWS, D_MODEL), lambda i, j: (0, 0)),
                  pl.BlockSpec((1, D_MODEL, tn), lambda i, j: (i, 0, j)),
                  pl.BlockSpec((1, 1, tn), lambda i, j: (i, 0, j))],
        out_specs=pl.BlockSpec((1, MOD_ROWS, tn), lambda i, j: (i, 0, j)),
        out_shape=jax.ShapeDtypeStruct((DEPTH, MOD_ROWS, 6 * D_MODEL), F32),
        compiler_params=_params("arbitrary", "arbitrary"),
        name="ada_modulation",
    )(cs, ada_w, ada_b.reshape(DEPTH, 1, 6 * D_MODEL))


def _mod_slice(mod_ref, k):
    return mod_ref[0, :, k * D_MODEL:(k + 1) * D_MODEL]


def _rms_mod(x, g, scale, shift):
    y = x * lax.rsqrt(jnp.mean(x * x, axis=-1, keepdims=True) + EPS)
    return (y * g) * (1.0 + scale) + shift


def _head_rms(t, gain_ref, hm_ref):
    w = t.shape[1]
    ms = _dot((t * t).astype(BF16), hm_ref[:w, :w])
    return t * lax.rsqrt(ms + EPS) * gain_ref[:, :w]


def _rope(t, cos, sin_hi, sin_lo):
    up = pltpu.roll(t, LANES - ROPE_FREQS, axis=1)
    dn = pltpu.roll(t, ROPE_FREQS, axis=1)
    return t * cos + up * sin_hi + dn * sin_lo


def _in_even_kernel(latent, x_ref, mod_ref, g_ref, w_ref, qg_ref, kg_ref, hm_ref, wc_ref, *rest):
    if latent:
        cos_ref, sh_ref, sl_ref, y_ref, q_ref, k_ref, v_ref = rest
    else:
        y_ref, q_ref, k_ref, v_ref = rest
    h = _rms_mod(x_ref[0], g_ref[...], _mod_slice(mod_ref, 1), _mod_slice(mod_ref, 0)).astype(BF16)
    F = FOURIER_WIDTH
    f = _dot(h, w_ref[:, :F]).astype(BF16)
    for g in range(FOURIER_GROUPS):
        yg = _dot(f[:, g * LANES:(g + 1) * LANES], wc_ref[...])
        y_ref[0, :, g * LANES:(g + 1) * LANES] = yg[:, :LANES].astype(BF16)
        y_ref[0, :, F + g * LANES:F + (g + 1) * LANES] = yg[:, LANES:].astype(BF16)

    def qk(t, gain_ref):
        t = _head_rms(t, gain_ref, hm_ref)
        if latent:
            t = jnp.concatenate([_rope(t[:, i * LANES:(i + 1) * LANES], cos_ref[...], sh_ref[...], sl_ref[...])
                                 for i in range(t.shape[1] // LANES)], axis=1)
        return t

    q = _dot(h, w_ref[:, F:F + QW])
    for j in range(QW // RMS_W):
        qj = qk(q[:, j * RMS_W:(j + 1) * RMS_W], qg_ref) * QK_SCALE
        q_ref[0, :, j * RMS_W:(j + 1) * RMS_W] = qj.astype(BF16)
    kv = _dot(h, w_ref[:, F + QW:])
    lo_lanes = lax.broadcasted_iota(jnp.int32, (1, LANES), 1) < HEAD_DIM

    def dup_heads(t):
        sw = pltpu.roll(t, HEAD_DIM, axis=1)
        return jnp.concatenate([jnp.where(lo_lanes, t, sw), jnp.where(lo_lanes, sw, t)], axis=1)

    k_ref[0] = dup_heads(qk(kv[:, :KW], kg_ref)).astype(BF16)
    v_ref[0] = dup_heads(kv[:, KW:]).astype(BF16)


def _in_even(x, mod, mod_row, g, w_in, q_g, k_g, hm, wc, rope_tabs, tm):
    B, L, _ = x.shape
    latent = rope_tabs is not None
    in_specs = [pl.BlockSpec((1, tm, D_MODEL), lambda b, t: (b, t, 0)),
                pl.BlockSpec((1, 1, 6 * D_MODEL), lambda b, t: (mod_row(b), 0, 0)),
                _const_spec((1, D_MODEL)),
                _const_spec((D_MODEL, EV_IN_WIDTH)),
                _const_spec((1, RMS_W)), _const_spec((1, RMS_W)),
                _const_spec((RMS_W, RMS_W)), _const_spec((LANES, 2 * LANES))]
    args = [x, mod, g, w_in, q_g, k_g, hm, wc]
    tok = lambda w: pl.BlockSpec((1, tm, w), lambda b, t: (b, t, 0))
    if latent:
        in_specs += [pl.BlockSpec((tm, LANES), lambda b, t: (t, 0))] * 3
        args += list(rope_tabs)
    return pl.pallas_call(
        functools.partial(_in_even_kernel, latent),
        grid=(B, L // tm),
        in_specs=in_specs,
        out_specs=[tok(2 * FOURIER_WIDTH), tok(QW), tok(KVD), tok(KVD)],
        out_shape=[jax.ShapeDtypeStruct((B, L, 2 * FOURIER_WIDTH), BF16),
                   jax.ShapeDtypeStruct((B, L, QW), BF16),
                   jax.ShapeDtypeStruct((B, L, KVD), BF16),
                   jax.ShapeDtypeStruct((B, L, KVD), BF16)],
        compiler_params=_params("arbitrary", "arbitrary"),
        name="in_even_latent" if latent else "in_even_ctx",
    )(*args)


def _fft_stage1_kernel(n2t, y_ref, m1_ref, twc_ref, tws_ref, o_ref):
    F = FOURIER_WIDTH
    for j in range(n2t):
        a = _dot(m1_ref[...], y_ref[0, :, j * 2 * F:(j + 1) * 2 * F])
        top, bot = a[:FFT_N], a[FFT_N:]
        ar = top[:, :F] - bot[:, F:]
        ai = top[:, F:] + bot[:, :F]
        tc = jnp.tile(twc_ref[j], (1, F // LANES))
        ts = jnp.tile(tws_ref[j], (1, F // LANES))
        o_ref[0, 0, :, j * F:(j + 1) * F] = (ar * tc + ai * ts).astype(BF16)
        o_ref[0, 1, :, j * F:(j + 1) * F] = (ai * tc - ar * ts).astype(BF16)


def _fft_stage2_kernel(k1t, b_ref, m3_ref, o_ref):
    for j in range(k1t):
        rhs = jnp.concatenate([b_ref[0, 0, j], b_ref[0, 1, j]], axis=0)
        o_ref[0, j] = _dot(m3_ref[...], rhs)


def _fourier_latent(y, m1, m3, twc, tws):
    B = y.shape[0]
    F = FOURIER_WIDTH
    n2t = 8
    yv = y.reshape(B, FFT_N, FFT_N * 2 * F)
    bh = pl.pallas_call(
        functools.partial(_fft_stage1_kernel, n2t),
        grid=(B, FFT_N // n2t),
        in_specs=[pl.BlockSpec((1, FFT_N, n2t * 2 * F), lambda b, t: (b, 0, t)),
                  _const_spec((2 * FFT_N, FFT_N)),
                  pl.BlockSpec((n2t, FFT_N, LANES), lambda b, t: (t, 0, 0)),
                  pl.BlockSpec((n2t, FFT_N, LANES), lambda b, t: (t, 0, 0))],
        out_specs=pl.BlockSpec((1, 2, FFT_N, n2t * F), lambda b, t: (b, 0, 0, t)),
        out_shape=jax.ShapeDtypeStruct((B, 2, FFT_N, FFT_N * F), BF16),
        compiler_params=_params("arbitrary", "arbitrary"),
        name="fft_stage1",
    )(yv, m1, twc, tws)
    k1t = 8
    bv = bh.reshape(B, 2, FFT_N, FFT_N, F)
    return pl.pallas_call(
        functools.partial(_fft_stage2_kernel, k1t),
        grid=(B, FFT_N // k1t),
        in_specs=[pl.BlockSpec((1, 2, k1t, FFT_N, F), lambda b, t: (b, 0, t, 0, 0)),
                  _const_spec((FFT_N, 2 * FFT_N))],
        out_specs=pl.BlockSpec((1, k1t, FFT_N, F), lambda b, t: (b, t, 0, 0)),
        out_shape=jax.ShapeDtypeStruct((B, FFT_N, FFT_N, F), F32),
        compiler_params=_params("arbitrary", "arbitrary"),
        name="fft_stage2",
    )(bv, m3)


def _fourier_ctx_kernel(y_ref, m_ref, o_ref):
    F = FOURIER_WIDTH
    rhs = jnp.concatenate([y_ref[0, :, :F], y_ref[0, :, F:]], axis=0)
    o_ref[0] = _dot(m_ref[...], rhs)


def _fourier_ctx(y, mctx):
    B = y.shape[0]
    return pl.pallas_call(
        _fourier_ctx_kernel,
        grid=(B,),
        in_specs=[pl.BlockSpec((1, CTX_LEN, 2 * FOURIER_WIDTH), lambda b: (b, 0, 0)),
                  _const_spec((CTX_LEN, 2 * CTX_LEN))],
        out_specs=pl.BlockSpec((1, CTX_LEN, FOURIER_WIDTH), lambda b: (b, 0, 0)),
        out_shape=jax.ShapeDtypeStruct((B, CTX_LEN, FOURIER_WIDTH), F32),
        compiler_params=_params("arbitrary"),
        name="fourier_ctx",
    )(y, mctx)


def _gqa_group(q_slabs, keys, vals, masks, sinks, rows, o_ref, first_slab, scratch):
    s_ref, p_ref, inv_ref = scratch
    lo_lanes = lax.broadcasted_iota(jnp.int32, (rows, LANES), 1) < HEAD_DIM
    zero = jnp.zeros((rows, LANES), BF16)
    qs = jnp.concatenate([jnp.where(lo_lanes if half == 0 else ~lo_lanes, qslab, zero)
                          for qslab in q_slabs for half in range(2)], axis=0)
    s_ref[...] = _dot_nt(qs, keys)
    for c in range(WIN_GROUP * rows // SOFTMAX_ROWS):
        rs = slice(c * SOFTMAX_ROWS, (c + 1) * SOFTMAX_ROWS)
        s = s_ref[rs, :]
        if masks is not None:
            mask, w = masks
            r0 = (c * SOFTMAX_ROWS) % rows
            s = jnp.concatenate([jnp.where(mask[r0:r0 + SOFTMAX_ROWS], s[:, :w], NEG), s[:, w:]], axis=1)
        sk = sinks[(c * SOFTMAX_ROWS) // rows]
        m = jnp.maximum(jnp.max(s, axis=-1, keepdims=True), sk)
        e = jnp.exp2(s - m)
        inv_ref[rs, :] = 1.0 / (jnp.sum(e, axis=-1, keepdims=True) + jnp.exp2(sk - m))
        p_ref[rs, :] = e.astype(BF16)
    o = _dot(p_ref[...], vals) * inv_ref[...]
    for i in range(WIN_GROUP // 2):
        a = o[(2 * i) * rows:(2 * i + 1) * rows]
        b = o[(2 * i + 1) * rows:(2 * i + 2) * rows]
        sl = slice((first_slab + i) * LANES, (first_slab + i + 1) * LANES)
        o_ref[0, :, sl] = jnp.where(lo_lanes, a, b).astype(BF16)


def _gqa_scratch(rows, n_keys):
    one = [pltpu.VMEM((WIN_GROUP * rows, n_keys), F32), pltpu.VMEM((WIN_GROUP * rows, n_keys), BF16),
           pltpu.VMEM((WIN_GROUP * rows, 1), F32)]
    return one * WIN_KV_HEADS


def _gqa_all(q_ref, keys, vals, masks, sink_ref, rows, o_ref, scratch):
    slabs_per_group = WIN_GROUP // 2
    for kv in range(WIN_KV_HEADS):
        q_slabs = [q_ref[0, :, (kv * slabs_per_group + i) * LANES:(kv * slabs_per_group + i + 1) * LANES]
                   for i in range(slabs_per_group)]
        sinks = [sink_ref[kv * WIN_GROUP + g] * LOG2E for g in range(WIN_GROUP)]
        ksl = slice(kv * LANES, (kv + 1) * LANES)
        _gqa_group(q_slabs, keys[:, ksl], vals[:, ksl], masks, sinks, rows, o_ref, kv * slabs_per_group,
                   scratch[3 * kv:3 * kv + 3])


def _window_band():
    i = np.arange(WIN_BLOCK)[:, None]
    j = np.arange(3 * WIN_BLOCK)[None, :]
    return jnp.asarray((np.abs(j - WIN_BLOCK - i) <= WIN_RADIUS).astype(np.float32))


def _win_attn_kernel(sink_ref, band_ref, q_ref, kp_ref, kc_ref, kn_ref, vp_ref, vc_ref, vn_ref, kx_ref, vx_ref, o_ref,
                     *scratch):
    n = pl.program_id(1)
    nw = 3 * WIN_BLOCK
    keys = jnp.concatenate([kp_ref[0], kc_ref[0], kn_ref[0], kx_ref[0]], axis=0)
    vals = jnp.concatenate([vp_ref[0], vc_ref[0], vn_ref[0], vx_ref[0]], axis=0)
    has_prev = (n > 0).astype(F32)
    has_next = (n < pl.num_programs(1) - 1).astype(F32)
    band = band_ref[...]
    valid = jnp.concatenate([band[:, :WIN_BLOCK] * has_prev, band[:, WIN_BLOCK:2 * WIN_BLOCK],
                             band[:, 2 * WIN_BLOCK:] * has_next], axis=1) > 0.0
    _gqa_all(q_ref, keys, vals, (valid, nw), sink_ref, WIN_BLOCK, o_ref, scratch)


def _win_attention(q, k, v, kx, vx, sink):
    B, L, _ = q.shape
    nb = L // WIN_BLOCK
    blk = lambda f: pl.BlockSpec((1, WIN_BLOCK, KVD), f)
    prev = lambda b, n: (b, jnp.maximum(n - 1, 0), 0)
    cur = lambda b, n: (b, n, 0)
    nxt = lambda b, n: (b, jnp.minimum(n + 1, nb - 1), 0)
    ctx_spec = pl.BlockSpec((1, CTX_LEN, KVD), lambda b, n: (b, 0, 0))
    return pl.pallas_call(
        _win_attn_kernel,
        grid=(B, nb),
        in_specs=[pl.BlockSpec(memory_space=pltpu.SMEM),
                  _const_spec((WIN_BLOCK, 3 * WIN_BLOCK)),
                  pl.BlockSpec((1, WIN_BLOCK, QW), cur),
                  blk(prev), blk(cur), blk(nxt), blk(prev), blk(cur), blk(nxt), ctx_spec, ctx_spec],
        out_specs=pl.BlockSpec((1, WIN_BLOCK, QW), cur),
        out_shape=jax.ShapeDtypeStruct((B, L, QW), BF16),
        scratch_shapes=_gqa_scratch(WIN_BLOCK, 3 * WIN_BLOCK + CTX_LEN),
        compiler_params=_params("arbitrary", "arbitrary"),
        name="window_attention",
    )(sink, _window_band(), q, k, k, k, v, v, v, kx, vx)


def _ctx_attn_even_kernel(sink_ref, q_ref, k_ref, v_ref, o_ref, *scratch):
    _gqa_all(q_ref, k_ref[0], v_ref[0], None, sink_ref, CTX_LEN, o_ref, scratch)


def _ctx_attention_even(q, k, v, sink):
    B = q.shape[0]
    return pl.pallas_call(
        _ctx_attn_even_kernel,
        grid=(B,),
        in_specs=[pl.BlockSpec(memory_space=pltpu.SMEM),
                  pl.BlockSpec((1, CTX_LEN, QW), lambda b: (b, 0, 0)),
                  pl.BlockSpec((1, CTX_LEN, KVD), lambda b: (b, 0, 0)),
                  pl.BlockSpec((1, CTX_LEN, KVD), lambda b: (b, 0, 0))],
        out_specs=pl.BlockSpec((1, CTX_LEN, QW), lambda b: (b, 0, 0)),
        out_shape=jax.ShapeDtypeStruct((B, CTX_LEN, QW), BF16),
        scratch_shapes=_gqa_scratch(CTX_LEN, CTX_LEN),
        compiler_params=_params("arbitrary"),
        name="ctx_attention_even",
    )(sink, q, k, v)


def _out_ffn_kernel(mode, *refs):
    if mode == "even_latent":
        f_ref, a_ref, x_ref, mod_ref, g_ref, wo_ref, wg_ref, wu_ref, wd_ref, o_ref = refs
        fm = jnp.concatenate([f_ref[0, :, j, :] for j in range(f_ref.shape[2])], axis=0).astype(BF16)
        o = _dot(fm, wo_ref[:FOURIER_WIDTH]) + _dot(a_ref[0], wo_ref[FOURIER_WIDTH:])
    elif mode == "even_ctx":
        f_ref, a_ref, x_ref, mod_ref, g_ref, wo_ref, wg_ref, wu_ref, wd_ref, o_ref = refs
        o = _dot(f_ref[0].astype(BF16), wo_ref[:FOURIER_WIDTH]) + _dot(a_ref[0], wo_ref[FOURIER_WIDTH:])
    else:
        a_ref, x_ref, mod_ref, g_ref, wo_ref, wg_ref, wu_ref, wd_ref, o_ref = refs
        o = _dot(jnp.concatenate([a_ref[0, j] for j in range(a_ref.shape[1])], axis=1), wo_ref[...])
    x1 = x_ref[0] + _mod_slice(mod_ref, 2) * o
    h = _rms_mod(x1, g_ref[...], _mod_slice(mod_ref, 4), _mod_slice(mod_ref, 3)).astype(BF16)
    acc = jnp.zeros_like(x1)
    for c in range(D_FF // FF_CHUNK):
        cs = slice(c * FF_CHUNK, (c + 1) * FF_CHUNK)
        a = _silu(_dot(h, wg_ref[0, :, cs])) * _dot(h, wu_ref[0, :, cs])
        acc = acc + _dot(a.astype(BF16), wd_ref[0, cs, :])
    o_ref[0] = x1 + _mod_slice(mod_ref, 5) * acc


def _out_ffn(mode, mix, x, mod, mod_row, g, w_out, layer, wg, wu, wd, tm):
    B, L, _ = x.shape
    tok = lambda w: pl.BlockSpec((1, tm, w), lambda b, t: (b, t, 0))
    ffn_spec = lambda r, c: pl.BlockSpec((1, r, c), lambda b, t: (layer, 0, 0), pipeline_mode=pl.Buffered(1))
    if mode == "even_latent":
        assert tm % FFT_N == 0
        mix_specs = [pl.BlockSpec((1, FFT_N, tm // FFT_N, FOURIER_WIDTH), lambda b, t: (b, 0, t, 0)), tok(QW)]
    elif mode == "even_ctx":
        mix_specs = [tok(FOURIER_WIDTH), tok(QW)]
    else:
        mix_specs = [pl.BlockSpec((1, NA_WIDTH // LANES, tm, LANES), lambda b, t: (b, 0, t, 0))]
    return pl.pallas_call(
        functools.partial(_out_ffn_kernel, mode),
        grid=(B, L // tm),
        in_specs=mix_specs + [tok(D_MODEL),
                              pl.BlockSpec((1, 1, 6 * D_MODEL), lambda b, t: (mod_row(b), 0, 0)),
                              _const_spec((1, D_MODEL)),
                              _const_spec((D_MODEL, D_MODEL)),
                              ffn_spec(D_MODEL, D_FF), ffn_spec(D_MODEL, D_FF), ffn_spec(D_FF, D_MODEL)],
        out_specs=tok(D_MODEL),
        out_shape=jax.ShapeDtypeStruct((B, L, D_MODEL), F32),
        compiler_params=_params("arbitrary", "arbitrary"),
        name="out_ffn_" + mode,
    )(*mix, x, mod, g, w_out, wg, wu, wd)


def _in_odd_kernel(with_q, x_ref, mod_ref, g_ref, w_ref, qg_ref, kg_ref, hm_ref, *outs):
    h = _rms_mod(x_ref[0], g_ref[...], _mod_slice(mod_ref, 1), _mod_slice(mod_ref, 0)).astype(BF16)
    W = NA_WIDTH
    slabs = RMS_W // LANES

    def normed(t, gain_ref, scale, o_ref):
        for j in range(W // RMS_W):
            tj = (_head_rms(t[:, j * RMS_W:(j + 1) * RMS_W], gain_ref, hm_ref) * scale).astype(BF16)
            for i in range(slabs):
                o_ref[0, j * slabs + i] = tj[:, i * LANES:(i + 1) * LANES]

    if with_q:
        q_ref, k_ref, v_ref = outs
        normed(_dot(h, w_ref[:, :W]), qg_ref, QK_SCALE, q_ref)
    else:
        k_ref, v_ref = outs
    normed(_dot(h, w_ref[:, W:2 * W]), kg_ref, 1.0, k_ref)
    v = _dot(h, w_ref[:, 2 * W:]).astype(BF16)
    for j in range(W // LANES):
        v_ref[0, j] = v[:, j * LANES:(j + 1) * LANES]


def _in_odd(x, mod, mod_row, g, w_in, q_g, k_g, hm, with_q, tm):
    B, L, _ = x.shape
    pairs = NA_WIDTH // LANES
    tok = pl.BlockSpec((1, pairs, tm, LANES), lambda b, t: (b, 0, t, 0))
    n_out = 3 if with_q else 2
    return pl.pallas_call(
        functools.partial(_in_odd_kernel, with_q),
        grid=(B, L // tm),
        in_specs=[pl.BlockSpec((1, tm, D_MODEL), lambda b, t: (b, t, 0)),
                  pl.BlockSpec((1, 1, 6 * D_MODEL), lambda b, t: (mod_row(b), 0, 0)),
                  _const_spec((1, D_MODEL)),
                  _const_spec((D_MODEL, 3 * NA_WIDTH)),
                  _const_spec((1, RMS_W)), _const_spec((1, RMS_W)), _const_spec((RMS_W, RMS_W))],
        out_specs=[tok] * n_out,
        out_shape=[jax.ShapeDtypeStruct((B, pairs, L, LANES), BF16)] * n_out,
        compiler_params=_params("arbitrary", "arbitrary"),
        name="in_odd" if with_q else "in_odd_ctx",
    )(x, mod, g, w_in, q_g, k_g, hm)


def _bias_table_kernel(rb_ref, oh_ref, o_ref):
    x = rb_ref[...]
    hi = x.astype(BF16)
    r1 = x - hi.astype(F32)
    mid = r1.astype(BF16)
    lo = (r1 - mid.astype(F32)).astype(BF16)
    oh = oh_ref[...]
    o_ref[...] = ((_dot(hi, oh) + _dot(mid, oh)) + _dot(lo, oh)) * LOG2E


def _na_bias_table(rel_bias, onehot):
    H = NA_HEADS
    ndr = 2 * NA_KH - 1
    rows = H * ndr
    rb = jnp.pad(rel_bias.reshape(rows, 2 * NA_KW - 1), ((0, 0), (0, LANES - (2 * NA_KW - 1))))
    flat = pl.pallas_call(
        _bias_table_kernel,
        out_shape=jax.ShapeDtypeStruct((rows, GRID_W * GRID_W), F32),
        compiler_params=pltpu.CompilerParams(vmem_limit_bytes=VMEM_LIMIT),
        name="na_bias_table",
    )(rb, onehot)
    t = flat.reshape(H, ndr, GRID_W, GRID_W)
    return jnp.concatenate([t[:, :-1], t[:, 1:]], axis=-1)


def _na_kernel(rows_per_step, q_ref, k_ref, v_ref, kx_ref, vx_ref, bias_ref, o_ref, s0_ref, s1_ref, p_ref, inv_ref):
    t = pl.program_id(1)
    n = NA_KH * GRID_W
    cq = lax.broadcasted_iota(jnp.int32, (GRID_W, n), 0)
    ck = lax.broadcasted_iota(jnp.int32, (GRID_W, n), 1) % GRID_W
    c0 = jnp.clip(cq - NA_KW // 2, 0, GRID_W - NA_KW)
    mask = (ck >= c0) & (ck < c0 + NA_KW)
    lo_lanes = lax.broadcasted_iota(jnp.int32, (GRID_W, LANES), 1) < HEAD_DIM

    def window(i):
        r = t * rows_per_step + i
        r0 = jnp.clip(r - NA_KH // 2, 0, GRID_ROWS - NA_KH)
        return r0 - r + NA_KH - 1, pl.multiple_of(r0 * GRID_W, GRID_W), pl.multiple_of(i * GRID_W, GRID_W)

    def scores(i, s_ref):
        dr0, start, qoff = window(i)
        for hp in range(NA_HEADS // 2):
            qs = q_ref[0, hp, pl.ds(qoff, GRID_W), :]
            keys = jnp.concatenate([k_ref[0, hp, pl.ds(start, n), :], kx_ref[0, hp]], axis=0)
            qm = jnp.concatenate([jnp.where(lo_lanes, qs, jnp.zeros_like(qs)),
                                  jnp.where(lo_lanes, jnp.zeros_like(qs), qs)], axis=0)
            s = _dot_nt(qm, keys)
            for half in range(2):
                h = 2 * hp + half
                bias = jnp.concatenate([bias_ref[h, dr0 + 2 * p] for p in range(NA_KH // 2)], axis=1)
                rs = slice(h * GRID_W, (h + 1) * GRID_W)
                hr = slice(half * GRID_W, (half + 1) * GRID_W)
                s_ref[rs, :n] = jnp.where(mask, s[hr, :n] + bias, NEG)
                s_ref[rs, n:] = s[hr, n:]

    def attend(i, s_ref):
        _, start, qoff = window(i)
        for c in range(NA_HEADS * GRID_W // SOFTMAX_ROWS):
            cs = slice(c * SOFTMAX_ROWS, (c + 1) * SOFTMAX_ROWS)
            s = s_ref[cs, :]
            e = jnp.exp2(s - jnp.max(s, axis=-1, keepdims=True))
            inv_ref[cs, :] = 1.0 / jnp.sum(e, axis=-1, keepdims=True)
            p_ref[cs, :] = e.astype(BF16)
        for hp in range(NA_HEADS // 2):
            vals = jnp.concatenate([v_ref[0, hp, pl.ds(start, n), :], vx_ref[0, hp]], axis=0)
            rs = slice(2 * hp * GRID_W, (2 * hp + 2) * GRID_W)
            res = _dot(p_ref[rs, :], vals) * inv_ref[rs, :]
            o_ref[0, hp, pl.ds(qoff, GRID_W), :] = jnp.where(lo_lanes, res[:GRID_W], res[GRID_W:]).astype(BF16)

    last = rows_per_step - 1
    scores(0, s0_ref)

    def two_rows(j, carry):
        scores(2 * j + 1, s1_ref)
        attend(2 * j, s0_ref)
        scores(jnp.minimum(2 * j + 2, last), s0_ref)
        attend(2 * j + 1, s1_ref)
        return carry

    lax.fori_loop(0, rows_per_step // 2, two_rows, 0)


def _neighbourhood_attention(q, k, v, kx, vx, bias_tab):
    B, pairs, L, _ = q.shape
    rows_per_step = 8
    full = pl.BlockSpec((1, pairs, L, LANES), lambda b, t: (b, 0, 0, 0))
    ctx_spec = pl.BlockSpec((1, pairs, CTX_LEN, LANES), lambda b, t: (0, 0, b, 0))
    q_rows = pl.BlockSpec((1, pairs, rows_per_step * GRID_W, LANES), lambda b, t: (b, 0, t, 0))
    n_keys = NA_KH * GRID_W + CTX_LEN
    all_rows = NA_HEADS * GRID_W
    return pl.pallas_call(
        functools.partial(_na_kernel, rows_per_step),
        grid=(B, GRID_ROWS // rows_per_step),
        in_specs=[q_rows, full, full, ctx_spec, ctx_spec, _const_spec(bias_tab.shape)],
        out_specs=q_rows,
        out_shape=jax.ShapeDtypeStruct((B, pairs, L, LANES), BF16),
        scratch_shapes=[pltpu.VMEM((all_rows, n_keys), F32), pltpu.VMEM((all_rows, n_keys), F32),
                        pltpu.VMEM((all_rows, n_keys), BF16), pltpu.VMEM((all_rows, 1), F32)],
        compiler_params=_params("arbitrary", "arbitrary"),
        name="neighbourhood_attention",
    )(q, k, v, kx, vx, bias_tab)


def kernel(x, c, ctx, c_ctx, ada_w, ada_b, norm1_g, norm2_g, ffn_w_gate, ffn_w_up, ffn_w_down,
           ev_w_in, ev_w_out, ev_q_norm, ev_k_norm, ev_sink,
           od_w_in, od_w_out, od_q_norm, od_k_norm, od_rel_bias):
    assert x.shape == (BATCH, SEQ, D_MODEL) and ctx.shape == (BATCH, CTX_LEN, D_MODEL)
    wc, m1, m3, twc, tws, mctx = _fourier_tables()
    hm = _head_mean_matrix()
    rope_tabs = _rope_tables()
    lane_gain = lambda gvec: jnp.tile(gvec, RMS_W // HEAD_DIM).reshape(1, RMS_W)
    lat_row = lambda b: b
    ctx_row = lambda b: CTX_MOD_ROW
    tm = 512
    n_ctx = BATCH * CTX_LEN
    as_seq = lambda a: a.reshape(1, n_ctx, a.shape[-1])
    per_batch = lambda a: a.reshape(BATCH, CTX_LEN, a.shape[-1])

    cs = jnp.concatenate([c, c_ctx[None, :], jnp.zeros((MOD_ROWS - BATCH - 1, D_MODEL), F32)], axis=0)
    mod = _modulation(cs, ada_w, ada_b).reshape(DEPTH, MOD_ROWS, 1, 6 * D_MODEL)

    w_in0 = ev_w_in[0].astype(BF16)
    w_out0 = ev_w_out[0].astype(BF16)
    g1 = norm1_g[0].reshape(1, D_MODEL)
    g2 = norm2_g[0].reshape(1, D_MODEL)
    ffn = (ffn_w_gate.astype(BF16), ffn_w_up.astype(BF16), ffn_w_down.astype(BF16))
    qg, kg = lane_gain(ev_q_norm[0]), lane_gain(ev_k_norm[0])
    y_l, q_l, k_l, v_l = _in_even(x, mod[0], lat_row, g1, w_in0, qg, kg, hm, wc, rope_tabs, tm)
    y_c, q_c, k_c, v_c = map(per_batch, _in_even(as_seq(ctx), mod[0], ctx_row, g1, w_in0, qg, kg, hm, wc, None, tm))
    f_l = _fourier_latent(y_l, m1, m3, twc, tws)
    f_c = _fourier_ctx(y_c, mctx)
    a_l = _win_attention(q_l, k_l, v_l, k_c, v_c, ev_sink[0])
    a_c = _ctx_attention_even(q_c, k_c, v_c, ev_sink[0])
    x1 = _out_ffn("even_latent", (f_l, a_l), x, mod[0], lat_row, g2, w_out0, 0, *ffn, tm)
    y1 = _out_ffn("even_ctx", (as_seq(f_c), as_seq(a_c)), as_seq(ctx), mod[0], ctx_row, g2, w_out0, 0, *ffn, tm)

    w_in1 = od_w_in[0].astype(BF16)
    w_out1 = od_w_out[0].astype(BF16)
    g1 = norm1_g[1].reshape(1, D_MODEL)
    g2 = norm2_g[1].reshape(1, D_MODEL)
    qg, kg = lane_gain(od_q_norm[0]), lane_gain(od_k_norm[0])
    q_l, k_l, v_l = _in_odd(x1, mod[1], lat_row, g1, w_in1, qg, kg, hm, True, tm)
    k_c, v_c = _in_odd(y1, mod[1], ctx_row, g1, w_in1, qg, kg, hm, False, tm)
    bias_tab = _na_bias_table(od_rel_bias[0], _na_onehot())
    a_l = _neighbourhood_attention(q_l, k_l, v_l, k_c, v_c, bias_tab)
    return _out_ffn("odd", (a_l,), x1, mod[1], lat_row, g2, w_out1, 1, *ffn, tm)
```

```python
import functools
import math

import numpy as np
import jax
import jax.numpy as jnp
from jax import lax
from jax.experimental import pallas as pl
from jax.experimental.pallas import tpu as pltpu

D_MODEL = 1024
BATCH = 4
SEQ = 4096
DEPTH = 2
GRID_W = 64
CTX_LEN = 256
HEAD_DIM = 64
EPS = 1e-6
NEG = -1e30
FOURIER_WIDTH = D_MODEL // 2
FOURIER_GROUPS = 4
FOURIER_GROUP_CH = FOURIER_WIDTH // FOURIER_GROUPS
WIN_Q_HEADS = (D_MODEL // 2) // HEAD_DIM
WIN_KV_HEADS = 2
WIN_GROUP = WIN_Q_HEADS // WIN_KV_HEADS
WIN_RADIUS = 128
WIN_BLOCK = 128
QW = WIN_Q_HEADS * HEAD_DIM
KW = WIN_KV_HEADS * HEAD_DIM
EV_IN_WIDTH = FOURIER_WIDTH + QW + 2 * KW
NA_HEADS = D_MODEL // HEAD_DIM
NA_KH = 8
NA_KW = 16
NA_WIDTH = NA_HEADS * HEAD_DIM
ROPE_THETA = 10000.0
ROPE_FREQS = HEAD_DIM // 4
D_FF = ((8 * D_MODEL // 3 + 255) // 256) * 256
GRID_ROWS = SEQ // GRID_W

LANES = 128
MOD_ROWS = 8
CTX_MOD_ROW = BATCH
FFT_N = 64
FF_CHUNK = 256
SOFTMAX_ROWS = 32
RMS_W = 256
VMEM_LIMIT = 56 * 1024 * 1024

LOG2E = math.log2(math.e)
QK_SCALE = LOG2E / math.sqrt(HEAD_DIM)

BF16 = jnp.bfloat16
F32 = jnp.float32

assert DEPTH == 2 and SEQ == FFT_N * FFT_N and D_FF % FF_CHUNK == 0


def _params(*sem):
    return pltpu.CompilerParams(dimension_semantics=sem, vmem_limit_bytes=VMEM_LIMIT)


def _dot(a, b):
    return jnp.dot(a, b, preferred_element_type=F32)


def _dot_nt(a, b):
    return lax.dot_general(a, b, (((1,), (1,)), ((), ())), preferred_element_type=F32)


def _silu(x):
    return x / (1.0 + jnp.exp(-x))


def _const_spec(shape):
    nd = len(shape)
    return pl.BlockSpec(shape, lambda *_: (0,) * nd, pipeline_mode=pl.Buffered(1))


def _dft_cos_sin(n):
    idx = (np.arange(n)[:, None] * np.arange(n)[None, :]) % n
    ang = 2.0 * np.pi * idx / n
    return np.cos(ang), np.sin(ang)


def _fourier_tables():
    cc, sc = _dft_cos_sin(FOURIER_GROUP_CH)
    wc = np.concatenate([cc, -sc], axis=1) / math.sqrt(FOURIER_GROUP_CH)
    c64, s64 = _dft_cos_sin(FFT_N)
    m1 = np.concatenate([c64, -s64], axis=0) / math.sqrt(FFT_N)
    m3 = np.concatenate([c64, s64], axis=1) / math.sqrt(FFT_N)
    tw = (np.arange(FFT_N)[:, None] * np.arange(FFT_N)[None, :]) % SEQ
    tw = 2.0 * np.pi * tw / SEQ
    twc = np.repeat(np.cos(tw)[:, :, None], LANES, axis=2)
    tws = np.repeat(np.sin(tw)[:, :, None], LANES, axis=2)
    cx, sx = _dft_cos_sin(CTX_LEN)
    mctx = np.concatenate([cx, sx], axis=1) / math.sqrt(CTX_LEN)
    as32 = lambda a: jnp.asarray(a, F32)
    return (as32(wc).astype(BF16), as32(m1).astype(BF16), as32(m3).astype(BF16),
            as32(twc), as32(tws), as32(mctx).astype(BF16))


def _head_mean_matrix():
    blk = np.kron(np.eye(RMS_W // HEAD_DIM), np.ones((HEAD_DIM, HEAD_DIM))) / HEAD_DIM
    return jnp.asarray(blk, BF16)


def _rope_tables():
    t = jnp.arange(SEQ, dtype=jnp.int32)
    row = (t // GRID_W).astype(F32)
    col = (t % GRID_W).astype(F32)
    inv = ROPE_THETA ** (-jnp.arange(ROPE_FREQS, dtype=F32) / ROPE_FREQS)
    ang_row = row[:, None] * inv[None, :]
    ang_col = col[:, None] * inv[None, :]
    zero = jnp.zeros_like(ang_row)
    cos = jnp.concatenate([jnp.cos(ang_row)] * 2 + [jnp.cos(ang_col)] * 2, axis=1)
    sin_hi = jnp.concatenate([-jnp.sin(ang_row), zero, -jnp.sin(ang_col), zero], axis=1)
    sin_lo = jnp.concatenate([zero, jnp.sin(ang_row), zero, jnp.sin(ang_col)], axis=1)
    rep = LANES // HEAD_DIM
    return jnp.tile(cos, (1, rep)), jnp.tile(sin_hi, (1, rep)), jnp.tile(sin_lo, (1, rep))


def _na_onehot():
    cq = np.arange(GRID_W)
    dc = np.clip(cq[None, :] - cq[:, None] + NA_KW - 1, 0, 2 * NA_KW - 2)
    oh = np.zeros((LANES, GRID_W * GRID_W), np.float32)
    oh[dc.reshape(-1), np.arange(GRID_W * GRID_W)] = 1.0
    return jnp.asarray(oh, BF16)


def _mod_kernel(cs_ref, w_ref, b_ref, o_ref):
    s = _silu(cs_ref[...]).astype(BF16)
    o_ref[0] = _dot(s, w_ref[0].astype(BF16)) + b_ref[0]


def _modulation(cs, ada_w, ada_b):
    tn = 1536
    return pl.pallas_call(
        _mod_kernel,
        grid=(DEPTH, 6 * D_MODEL // tn),
        in_specs=[pl.BlockSpec((MOD_ROWS, D_MODEL), lambda i, j: (0, 0)),
                  pl.BlockSpec((1, D_MODEL, tn), lambda i, j: (i, 0, j)),
                  pl.BlockSpec((1, 1, tn), lambda i, j: (i, 0, j))],
        out_specs=pl.BlockSpec((1, MOD_ROWS, tn), lambda i, j: (i, 0, j)),
        out_shape=jax.ShapeDtypeStruct((DEPTH, MOD_ROWS, 6 * D_MODEL), F32),
        compiler_params=_params("arbitrary", "arbitrary"),
        name="ada_modulation",
    )(cs, ada_w, ada_b.reshape(DEPTH, 1, 6 * D_MODEL))


def _mod_slice(mod_ref, k):
    return mod_ref[0, :, k * D_MODEL:(k + 1) * D_MODEL]


def _rms_mod(x, g, scale, shift):
    y = x * lax.rsqrt(jnp.mean(x * x, axis=-1, keepdims=True) + EPS)
    return (y * g) * (1.0 + scale) + shift


def _head_rms(t, gain_ref, hm_ref):
    w = t.shape[1]
    ms = _dot((t * t).astype(BF16), hm_ref[:w, :w])
    return t * lax.rsqrt(ms + EPS) * gain_ref[:, :w]


def _rope(t, cos, sin_hi, sin_lo):
    up = pltpu.roll(t, LANES - ROPE_FREQS, axis=1)
    dn = pltpu.roll(t, ROPE_FREQS, axis=1)
    return t * cos + up * sin_hi + dn * sin_lo


def _in_even_kernel(latent, x_ref, mod_ref, g_ref, w_ref, qg_ref, kg_ref, hm_ref, wc_ref, *rest):
    if latent:
        cos_ref, sh_ref, sl_ref, y_ref, q_ref, k_ref, v_ref = rest
    else:
        y_ref, q_ref, k_ref, v_ref = rest
    h = _rms_mod(x_ref[0], g_ref[...], _mod_slice(mod_ref, 1), _mod_slice(mod_ref, 0)).astype(BF16)
    F = FOURIER_WIDTH
    f = _dot(h, w_ref[:, :F]).astype(BF16)
    for g in range(FOURIER_GROUPS):
        yg = _dot(f[:, g * LANES:(g + 1) * LANES], wc_ref[...])
        y_ref[0, :, g * LANES:(g + 1) * LANES] = yg[:, :LANES].astype(BF16)
        y_ref[0, :, F + g * LANES:F + (g + 1) * LANES] = yg[:, LANES:].astype(BF16)

    def qk(t, gain_ref):
        t = _head_rms(t, gain_ref, hm_ref)
        if latent:
            t = jnp.concatenate([_rope(t[:, i * LANES:(i + 1) * LANES], cos_ref[...], sh_ref[...], sl_ref[...])
                                 for i in range(t.shape[1] // LANES)], axis=1)
        return t

    q = _dot(h, w_ref[:, F:F + QW])
    for j in range(QW // RMS_W):
        qj = qk(q[:, j * RMS_W:(j + 1) * RMS_W], qg_ref) * QK_SCALE
        q_ref[0, :, j * RMS_W:(j + 1) * RMS_W] = qj.astype(BF16)
    kv = _dot(h, w_ref[:, F + QW:])
    lo_lanes = lax.broadcasted_iota(jnp.int32, (1, LANES), 1) < HEAD_DIM

    def store_dup(t, o_ref):
        sw = pltpu.roll(t, HEAD_DIM, axis=1)
        o_ref[0, 0] = jnp.where(lo_lanes, t, sw).astype(BF16)
        o_ref[0, 1] = jnp.where(lo_lanes, sw, t).astype(BF16)

    store_dup(qk(kv[:, :KW], kg_ref), k_ref)
    store_dup(kv[:, KW:], v_ref)


def _in_even(x, mod, mod_row, g, w_in, q_g, k_g, hm, wc, rope_tabs, tm):
    B, L, _ = x.shape
    latent = rope_tabs is not None
    in_specs = [pl.BlockSpec((1, tm, D_MODEL), lambda b, t: (b, t, 0)),
                pl.BlockSpec((1, 1, 6 * D_MODEL), lambda b, t: (mod_row(b), 0, 0)),
                _const_spec((1, D_MODEL)),
                _const_spec((D_MODEL, EV_IN_WIDTH)),
                _const_spec((1, RMS_W)), _const_spec((1, RMS_W)),
                _const_spec((RMS_W, RMS_W)), _const_spec((LANES, 2 * LANES))]
    args = [x, mod, g, w_in, q_g, k_g, hm, wc]
    tok = lambda w: pl.BlockSpec((1, tm, w), lambda b, t: (b, t, 0))
    kv_spec = pl.BlockSpec((1, WIN_KV_HEADS, tm, LANES), lambda b, t: (b, 0, t, 0))
    if latent:
        in_specs += [pl.BlockSpec((tm, LANES), lambda b, t: (t, 0))] * 3
        args += list(rope_tabs)
    return pl.pallas_call(
        functools.partial(_in_even_kernel, latent),
        grid=(B, L // tm),
        in_specs=in_specs,
        out_specs=[tok(2 * FOURIER_WIDTH), tok(QW), kv_spec, kv_spec],
        out_shape=[jax.ShapeDtypeStruct((B, L, 2 * FOURIER_WIDTH), BF16),
                   jax.ShapeDtypeStruct((B, L, QW), BF16),
                   jax.ShapeDtypeStruct((B, WIN_KV_HEADS, L, LANES), BF16),
                   jax.ShapeDtypeStruct((B, WIN_KV_HEADS, L, LANES), BF16)],
        compiler_params=_params("arbitrary", "arbitrary"),
        name="in_even_latent" if latent else "in_even_ctx",
    )(*args)


def _fft_stage1_kernel(n2t, y_ref, m1_ref, twc_ref, tws_ref, o_ref):
    F = FOURIER_WIDTH
    for j in range(n2t):
        a = _dot(m1_ref[...], y_ref[0, :, j * 2 * F:(j + 1) * 2 * F])
        top, bot = a[:FFT_N], a[FFT_N:]
        ar = top[:, :F] - bot[:, F:]
        ai = top[:, F:] + bot[:, :F]
        tc = jnp.tile(twc_ref[j], (1, F // LANES))
        ts = jnp.tile(tws_ref[j], (1, F // LANES))
        o_ref[0, 0, :, j * F:(j + 1) * F] = (ar * tc + ai * ts).astype(BF16)
        o_ref[0, 1, :, j * F:(j + 1) * F] = (ai * tc - ar * ts).astype(BF16)


def _fft_stage2_kernel(k1t, b_ref, m3_ref, o_ref):
    for j in range(k1t):
        rhs = jnp.concatenate([b_ref[0, 0, j], b_ref[0, 1, j]], axis=0)
        o_ref[0, j] = _dot(m3_ref[...], rhs)


def _fourier_latent(y, m1, m3, twc, tws):
    B = y.shape[0]
    F = FOURIER_WIDTH
    n2t = 8
    yv = y.reshape(B, FFT_N, FFT_N * 2 * F)
    bh = pl.pallas_call(
        functools.partial(_fft_stage1_kernel, n2t),
        grid=(B, FFT_N // n2t),
        in_specs=[pl.BlockSpec((1, FFT_N, n2t * 2 * F), lambda b, t: (b, 0, t)),
                  _const_spec((2 * FFT_N, FFT_N)),
                  pl.BlockSpec((n2t, FFT_N, LANES), lambda b, t: (t, 0, 0)),
                  pl.BlockSpec((n2t, FFT_N, LANES), lambda b, t: (t, 0, 0))],
        out_specs=pl.BlockSpec((1, 2, FFT_N, n2t * F), lambda b, t: (b, 0, 0, t)),
        out_shape=jax.ShapeDtypeStruct((B, 2, FFT_N, FFT_N * F), BF16),
        compiler_params=_params("arbitrary", "arbitrary"),
        name="fft_stage1",
    )(yv, m1, twc, tws)
    k1t = 8
    bv = bh.reshape(B, 2, FFT_N, FFT_N, F)
    return pl.pallas_call(
        functools.partial(_fft_stage2_kernel, k1t),
        grid=(B, FFT_N // k1t),
        in_specs=[pl.BlockSpec((1, 2, k1t, FFT_N, F), lambda b, t: (b, 0, t, 0, 0)),
                  _const_spec((FFT_N, 2 * FFT_N))],
        out_specs=pl.BlockSpec((1, k1t, FFT_N, F), lambda b, t: (b, t, 0, 0)),
        out_shape=jax.ShapeDtypeStruct((B, FFT_N, FFT_N, F), F32),
        compiler_params=_params("arbitrary", "arbitrary"),
        name="fft_stage2",
    )(bv, m3)


def _fourier_ctx_kernel(y_ref, m_ref, o_ref):
    F = FOURIER_WIDTH
    rhs = jnp.concatenate([y_ref[0, :, :F], y_ref[0, :, F:]], axis=0)
    o_ref[0] = _dot(m_ref[...], rhs)


def _fourier_ctx(y, mctx):
    B = y.shape[0]
    return pl.pallas_call(
        _fourier_ctx_kernel,
        grid=(B,),
        in_specs=[pl.BlockSpec((1, CTX_LEN, 2 * FOURIER_WIDTH), lambda b: (b, 0, 0)),
                  _const_spec((CTX_LEN, 2 * CTX_LEN))],
        out_specs=pl.BlockSpec((1, CTX_LEN, FOURIER_WIDTH), lambda b: (b, 0, 0)),
        out_shape=jax.ShapeDtypeStruct((B, CTX_LEN, FOURIER_WIDTH), F32),
        compiler_params=_params("arbitrary"),
        name="fourier_ctx",
    )(y, mctx)


def _gqa_group(q_slabs, keys, vals, masks, sinks, rows, o_ref, first_slab, scratch):
    s_ref, p_ref, inv_ref = scratch
    _gqa_scores(q_slabs, keys, rows, s_ref)
    _gqa_attend(vals, masks, sinks, rows, lambda sl: o_ref.at[0, :, sl], first_slab, scratch)


def _gqa_scores(q_slabs, keys, rows, s_ref):
    lo_lanes = lax.broadcasted_iota(jnp.int32, (rows, LANES), 1) < HEAD_DIM
    zero = jnp.zeros((rows, LANES), BF16)
    qs = jnp.concatenate([jnp.where(lo_lanes if half == 0 else ~lo_lanes, qslab, zero)
                          for qslab in q_slabs for half in range(2)], axis=0)
    s_ref[...] = _dot_nt(qs, keys)


def _gqa_attend(vals, masks, sinks, rows, out_slab, first_slab, scratch):
    s_ref, p_ref, inv_ref = scratch
    lo_lanes = lax.broadcasted_iota(jnp.int32, (rows, LANES), 1) < HEAD_DIM
    for c in range(WIN_GROUP * rows // SOFTMAX_ROWS):
        rs = slice(c * SOFTMAX_ROWS, (c + 1) * SOFTMAX_ROWS)
        s = s_ref[rs, :]
        if masks is not None:
            mask, w = masks
            r0 = (c * SOFTMAX_ROWS) % rows
            s = jnp.concatenate([jnp.where(mask[r0:r0 + SOFTMAX_ROWS], s[:, :w], NEG), s[:, w:]], axis=1)
        sk = sinks[(c * SOFTMAX_ROWS) // rows]
        m = jnp.maximum(jnp.max(s, axis=-1, keepdims=True), sk)
        e = jnp.exp2(s - m)
        inv_ref[rs, :] = 1.0 / (jnp.sum(e, axis=-1, keepdims=True) + jnp.exp2(sk - m))
        p_ref[rs, :] = e.astype(BF16)
    o = _dot(p_ref[...], vals) * inv_ref[...]
    for i in range(WIN_GROUP // 2):
        a = o[(2 * i) * rows:(2 * i + 1) * rows]
        b = o[(2 * i + 1) * rows:(2 * i + 2) * rows]
        sl = slice((first_slab + i) * LANES, (first_slab + i + 1) * LANES)
        out_slab(sl)[...] = jnp.where(lo_lanes, a, b).astype(BF16)


def _gqa_scratch(rows, n_keys):
    one = [pltpu.VMEM((WIN_GROUP * rows, n_keys), F32), pltpu.VMEM((WIN_GROUP * rows, n_keys), BF16),
           pltpu.VMEM((WIN_GROUP * rows, 1), F32)]
    return one * WIN_KV_HEADS


def _group_sinks(sink_ref, kv):
    return [sink_ref[kv * WIN_GROUP + g] * LOG2E for g in range(WIN_GROUP)]


def _group_slab(kv, i):
    return slice((kv * (WIN_GROUP // 2) + i) * LANES, (kv * (WIN_GROUP // 2) + i + 1) * LANES)


def _window_bands():
    i = np.arange(WIN_BLOCK)[:, None]
    j = np.arange(3 * WIN_BLOCK)[None, :]
    bands = [np.abs(j - off - i) <= WIN_RADIUS for off in (0, WIN_BLOCK, 2 * WIN_BLOCK)]
    return jnp.asarray(np.stack(bands).astype(np.float32))


def _win_attn_kernel(blocks_per_step, sink_ref, band_ref, q_ref, k_ref, v_ref, kx_ref, vx_ref, o_ref,
                     s0_ref, s1_ref, p_ref, inv_ref):
    t = pl.program_id(1)
    nb = SEQ // WIN_BLOCK
    nw = 3 * WIN_BLOCK

    def place(j):
        n = t * blocks_per_step + j
        start = pl.multiple_of(jnp.clip((n - 1) * WIN_BLOCK, 0, SEQ - nw), WIN_BLOCK)
        sel = jnp.where(n == 0, 0, jnp.where(n == nb - 1, 2, 1))
        return pl.multiple_of(j * WIN_BLOCK, WIN_BLOCK), start, sel

    def scores(j, kv, s_ref):
        qoff, start, _ = place(j)
        keys = jnp.concatenate([k_ref[0, kv, pl.ds(start, nw), :], kx_ref[0, kv]], axis=0)
        q_slabs = [q_ref[0, pl.ds(qoff, WIN_BLOCK), _group_slab(kv, i)] for i in range(WIN_GROUP // 2)]
        _gqa_scores(q_slabs, keys, WIN_BLOCK, s_ref)

    def attend(j, kv, s_ref):
        qoff, start, sel = place(j)
        vals = jnp.concatenate([v_ref[0, kv, pl.ds(start, nw), :], vx_ref[0, kv]], axis=0)
        _gqa_attend(vals, (band_ref[sel] > 0.0, nw), _group_sinks(sink_ref, kv), WIN_BLOCK,
                    lambda sl: o_ref.at[0, pl.ds(qoff, WIN_BLOCK), sl], kv * (WIN_GROUP // 2),
                    (s_ref, p_ref, inv_ref))

    scores(0, 0, s0_ref)

    def one_block(j, carry):
        scores(j, 1, s1_ref)
        attend(j, 0, s0_ref)
        scores(jnp.minimum(j + 1, blocks_per_step - 1), 0, s0_ref)
        attend(j, 1, s1_ref)
        return carry

    lax.fori_loop(0, blocks_per_step, one_block, 0)


def _win_attention(q, k, v, kx, vx, sink):
    B, L, _ = q.shape
    blocks_per_step = 8
    rows = blocks_per_step * WIN_BLOCK
    n_keys = 3 * WIN_BLOCK + CTX_LEN
    full = pl.BlockSpec((1, WIN_KV_HEADS, L, LANES), lambda b, t: (b, 0, 0, 0))
    ctx_spec = pl.BlockSpec((1, WIN_KV_HEADS, CTX_LEN, LANES), lambda b, t: (0, 0, b, 0))
    tok = pl.BlockSpec((1, rows, QW), lambda b, t: (b, t, 0))
    return pl.pallas_call(
        functools.partial(_win_attn_kernel, blocks_per_step),
        grid=(B, L // rows),
        in_specs=[pl.BlockSpec(memory_space=pltpu.SMEM),
                  _const_spec((3, WIN_BLOCK, 3 * WIN_BLOCK)),
                  tok, full, full, ctx_spec, ctx_spec],
        out_specs=tok,
        out_shape=jax.ShapeDtypeStruct((B, L, QW), BF16),
        scratch_shapes=[pltpu.VMEM((WIN_GROUP * WIN_BLOCK, n_keys), F32), pltpu.VMEM((WIN_GROUP * WIN_BLOCK, n_keys), F32),
                        pltpu.VMEM((WIN_GROUP * WIN_BLOCK, n_keys), BF16), pltpu.VMEM((WIN_GROUP * WIN_BLOCK, 1), F32)],
        compiler_params=_params("arbitrary", "arbitrary"),
        name="window_attention",
    )(sink, _window_bands(), q, k, v, kx, vx)


def _ctx_attn_even_kernel(sink_ref, q_ref, k_ref, v_ref, o_ref, *scratch):
    for kv in range(WIN_KV_HEADS):
        q_slabs = [q_ref[0, :, _group_slab(kv, i)] for i in range(WIN_GROUP // 2)]
        _gqa_group(q_slabs, k_ref[0, kv], v_ref[0, kv], None, _group_sinks(sink_ref, kv), CTX_LEN, o_ref,
                   kv * (WIN_GROUP // 2), scratch[3 * kv:3 * kv + 3])


def _ctx_attention_even(q, k, v, sink):
    B = q.shape[0]
    kv_spec = pl.BlockSpec((1, WIN_KV_HEADS, CTX_LEN, LANES), lambda b: (0, 0, b, 0))
    return pl.pallas_call(
        _ctx_attn_even_kernel,
        grid=(B,),
        in_specs=[pl.BlockSpec(memory_space=pltpu.SMEM),
                  pl.BlockSpec((1, CTX_LEN, QW), lambda b: (b, 0, 0)),
                  kv_spec, kv_spec],
        out_specs=pl.BlockSpec((1, CTX_LEN, QW), lambda b: (b, 0, 0)),
        out_shape=jax.ShapeDtypeStruct((B, CTX_LEN, QW), BF16),
        scratch_shapes=_gqa_scratch(CTX_LEN, CTX_LEN),
        compiler_params=_params("arbitrary"),
        name="ctx_attention_even",
    )(sink, q, k, v)


def _out_ffn_kernel(mode, *refs):
    if mode == "even_latent":
        f_ref, a_ref, x_ref, mod_ref, g_ref, wo_ref, wg_ref, wu_ref, wd_ref, o_ref = refs
        fm = jnp.concatenate([f_ref[0, :, j, :] for j in range(f_ref.shape[2])], axis=0).astype(BF16)
        o = _dot(fm, wo_ref[:FOURIER_WIDTH]) + _dot(a_ref[0], wo_ref[FOURIER_WIDTH:])
    elif mode == "even_ctx":
        f_ref, a_ref, x_ref, mod_ref, g_ref, wo_ref, wg_ref, wu_ref, wd_ref, o_ref = refs
        o = _dot(f_ref[0].astype(BF16), wo_ref[:FOURIER_WIDTH]) + _dot(a_ref[0], wo_ref[FOURIER_WIDTH:])
    else:
        a_ref, x_ref, mod_ref, g_ref, wo_ref, wg_ref, wu_ref, wd_ref, o_ref = refs
        o = _dot(jnp.concatenate([a_ref[0, j] for j in range(a_ref.shape[1])], axis=1), wo_ref[...])
    x1 = x_ref[0] + _mod_slice(mod_ref, 2) * o
    h = _rms_mod(x1, g_ref[...], _mod_slice(mod_ref, 4), _mod_slice(mod_ref, 3)).astype(BF16)
    acc = jnp.zeros_like(x1)
    for c in range(D_FF // FF_CHUNK):
        cs = slice(c * FF_CHUNK, (c + 1) * FF_CHUNK)
        a = _silu(_dot(h, wg_ref[0, :, cs])) * _dot(h, wu_ref[0, :, cs])
        acc = acc + _dot(a.astype(BF16), wd_ref[0, cs, :])
    o_ref[0] = x1 + _mod_slice(mod_ref, 5) * acc


def _out_ffn(mode, mix, x, mod, mod_row, g, w_out, layer, wg, wu, wd, tm):
    B, L, _ = x.shape
    tok = lambda w: pl.BlockSpec((1, tm, w), lambda b, t: (b, t, 0))
    ffn_spec = lambda r, c: pl.BlockSpec((1, r, c), lambda b, t: (layer, 0, 0), pipeline_mode=pl.Buffered(1))
    if mode == "even_latent":
        assert tm % FFT_N == 0
        mix_specs = [pl.BlockSpec((1, FFT_N, tm // FFT_N, FOURIER_WIDTH), lambda b, t: (b, 0, t, 0)), tok(QW)]
    elif mode == "even_ctx":
        mix_specs = [tok(FOURIER_WIDTH), tok(QW)]
    else:
        mix_specs = [pl.BlockSpec((1, NA_WIDTH // LANES, tm, LANES), lambda b, t: (b, 0, t, 0))]
    return pl.pallas_call(
        functools.partial(_out_ffn_kernel, mode),
        grid=(B, L // tm),
        in_specs=mix_specs + [tok(D_MODEL),
                              pl.BlockSpec((1, 1, 6 * D_MODEL), lambda b, t: (mod_row(b), 0, 0)),
                              _const_spec((1, D_MODEL)),
                              _const_spec((D_MODEL, D_MODEL)),
                              ffn_spec(D_MODEL, D_FF), ffn_spec(D_MODEL, D_FF), ffn_spec(D_FF, D_MODEL)],
        out_specs=tok(D_MODEL),
        out_shape=jax.ShapeDtypeStruct((B, L, D_MODEL), F32),
        compiler_params=_params("arbitrary", "arbitrary"),
        name="out_ffn_" + mode,
    )(*mix, x, mod, g, w_out, wg, wu, wd)


def _in_odd_kernel(with_q, x_ref, mod_ref, g_ref, w_ref, qg_ref, kg_ref, hm_ref, *outs):
    h = _rms_mod(x_ref[0], g_ref[...], _mod_slice(mod_ref, 1), _mod_slice(mod_ref, 0)).astype(BF16)
    W = NA_WIDTH
    slabs = RMS_W // LANES

    def normed(t, gain_ref, scale, o_ref):
        for j in range(W // RMS_W):
            tj = (_head_rms(t[:, j * RMS_W:(j + 1) * RMS_W], gain_ref, hm_ref) * scale).astype(BF16)
            for i in range(slabs):
                o_ref[0, j * slabs + i] = tj[:, i * LANES:(i + 1) * LANES]

    if with_q:
        q_ref, k_ref, v_ref = outs
        normed(_dot(h, w_ref[:, :W]), qg_ref, QK_SCALE, q_ref)
    else:
        k_ref, v_ref = outs
    normed(_dot(h, w_ref[:, W:2 * W]), kg_ref, 1.0, k_ref)
    v = _dot(h, w_ref[:, 2 * W:]).astype(BF16)
    for j in range(W // LANES):
        v_ref[0, j] = v[:, j * LANES:(j + 1) * LANES]


def _in_odd(x, mod, mod_row, g, w_in, q_g, k_g, hm, with_q, tm):
    B, L, _ = x.shape
    pairs = NA_WIDTH // LANES
    tok = pl.BlockSpec((1, pairs, tm, LANES), lambda b, t: (b, 0, t, 0))
    n_out = 3 if with_q else 2
    return pl.pallas_call(
        functools.partial(_in_odd_kernel, with_q),
        grid=(B, L // tm),
        in_specs=[pl.BlockSpec((1, tm, D_MODEL), lambda b, t: (b, t, 0)),
                  pl.BlockSpec((1, 1, 6 * D_MODEL), lambda b, t: (mod_row(b), 0, 0)),
                  _const_spec((1, D_MODEL)),
                  _const_spec((D_MODEL, 3 * NA_WIDTH)),
                  _const_spec((1, RMS_W)), _const_spec((1, RMS_W)), _const_spec((RMS_W, RMS_W))],
        out_specs=[tok] * n_out,
        out_shape=[jax.ShapeDtypeStruct((B, pairs, L, LANES), BF16)] * n_out,
        compiler_params=_params("arbitrary", "arbitrary"),
        name="in_odd" if with_q else "in_odd_ctx",
    )(x, mod, g, w_in, q_g, k_g, hm)


def _bias_table_kernel(rb_ref, oh_ref, o_ref):
    x = rb_ref[...]
    hi = x.astype(BF16)
    r1 = x - hi.astype(F32)
    mid = r1.astype(BF16)
    lo = (r1 - mid.astype(F32)).astype(BF16)
    oh = oh_ref[...]
    o_ref[...] = ((_dot(hi, oh) + _dot(mid, oh)) + _dot(lo, oh)) * LOG2E


def _na_bias_table(rel_bias, onehot):
    H = NA_HEADS
    ndr = 2 * NA_KH - 1
    rows = H * ndr
    rb = jnp.pad(rel_bias.reshape(rows, 2 * NA_KW - 1), ((0, 0), (0, LANES - (2 * NA_KW - 1))))
    flat = pl.pallas_call(
        _bias_table_kernel,
        out_shape=jax.ShapeDtypeStruct((rows, GRID_W * GRID_W), F32),
        compiler_params=pltpu.CompilerParams(vmem_limit_bytes=VMEM_LIMIT),
        name="na_bias_table",
    )(rb, onehot)
    t = flat.reshape(H, ndr, GRID_W, GRID_W)
    return jnp.concatenate([t[:, :-1], t[:, 1:]], axis=-1)


def _na_kernel(rows_per_step, q_ref, k_ref, v_ref, kx_ref, vx_ref, bias_ref, o_ref, s0_ref, s1_ref, p_ref, inv_ref):
    t = pl.program_id(1)
    n = NA_KH * GRID_W
    cq = lax.broadcasted_iota(jnp.int32, (GRID_W, n), 0)
    ck = lax.broadcasted_iota(jnp.int32, (GRID_W, n), 1) % GRID_W
    c0 = jnp.clip(cq - NA_KW // 2, 0, GRID_W - NA_KW)
    mask = (ck >= c0) & (ck < c0 + NA_KW)
    lo_lanes = lax.broadcasted_iota(jnp.int32, (GRID_W, LANES), 1) < HEAD_DIM

    def window(i):
        r = t * rows_per_step + i
        r0 = jnp.clip(r - NA_KH // 2, 0, GRID_ROWS - NA_KH)
        return r0 - r + NA_KH - 1, pl.multiple_of(r0 * GRID_W, GRID_W), pl.multiple_of(i * GRID_W, GRID_W)

    def scores(i, s_ref):
        dr0, start, qoff = window(i)
        for hp in range(NA_HEADS // 2):
            qs = q_ref[0, hp, pl.ds(qoff, GRID_W), :]
            keys = jnp.concatenate([k_ref[0, hp, pl.ds(start, n), :], kx_ref[0, hp]], axis=0)
            qm = jnp.concatenate([jnp.where(lo_lanes, qs, jnp.zeros_like(qs)),
                                  jnp.where(lo_lanes, jnp.zeros_like(qs), qs)], axis=0)
            s = _dot_nt(qm, keys)
            for half in range(2):
                h = 2 * hp + half
                bias = jnp.concatenate([bias_ref[h, dr0 + 2 * p] for p in range(NA_KH // 2)], axis=1)
                rs = slice(h * GRID_W, (h + 1) * GRID_W)
                hr = slice(half * GRID_W, (half + 1) * GRID_W)
                s_ref[rs, :n] = jnp.where(mask, s[hr, :n] + bias, NEG)
                s_ref[rs, n:] = s[hr, n:]

    def attend(i, s_ref):
        _, start, qoff = window(i)
        for c in range(NA_HEADS * GRID_W // SOFTMAX_ROWS):
            cs = slice(c * SOFTMAX_ROWS, (c + 1) * SOFTMAX_ROWS)
            s = s_ref[cs, :]
            e = jnp.exp2(s - jnp.max(s, axis=-1, keepdims=True))
            inv_ref[cs, :] = 1.0 / jnp.sum(e, axis=-1, keepdims=True)
            p_ref[cs, :] = e.astype(BF16)
        for hp in range(NA_HEADS // 2):
            vals = jnp.concatenate([v_ref[0, hp, pl.ds(start, n), :], vx_ref[0, hp]], axis=0)
            rs = slice(2 * hp * GRID_W, (2 * hp + 2) * GRID_W)
            res = _dot(p_ref[rs, :], vals) * inv_ref[rs, :]
            o_ref[0, hp, pl.ds(qoff, GRID_W), :] = jnp.where(lo_lanes, res[:GRID_W], res[GRID_W:]).astype(BF16)

    last = rows_per_step - 1
    scores(0, s0_ref)

    def two_rows(j, carry):
        scores(2 * j + 1, s1_ref)
        attend(2 * j, s0_ref)
        scores(jnp.minimum(2 * j + 2, last), s0_ref)
        attend(2 * j + 1, s1_ref)
        return carry

    lax.fori_loop(0, rows_per_step // 2, two_rows, 0)


def _neighbourhood_attention(q, k, v, kx, vx, bias_tab):
    B, pairs, L, _ = q.shape
    rows_per_step = 8
    full = pl.BlockSpec((1, pairs, L, LANES), lambda b, t: (b, 0, 0, 0))
    ctx_spec = pl.BlockSpec((1, pairs, CTX_LEN, LANES), lambda b, t: (0, 0, b, 0))
    q_rows = pl.BlockSpec((1, pairs, rows_per_step * GRID_W, LANES), lambda b, t: (b, 0, t, 0))
    n_keys = NA_KH * GRID_W + CTX_LEN
    all_rows = NA_HEADS * GRID_W
    return pl.pallas_call(
        functools.partial(_na_kernel, rows_per_step),
        grid=(B, GRID_ROWS // rows_per_step),
        in_specs=[q_rows, full, full, ctx_spec, ctx_spec, _const_spec(bias_tab.shape)],
        out_specs=q_rows,
        out_shape=jax.ShapeDtypeStruct((B, pairs, L, LANES), BF16),
        scratch_shapes=[pltpu.VMEM((all_rows, n_keys), F32), pltpu.VMEM((all_rows, n_keys), F32),
                        pltpu.VMEM((all_rows, n_keys), BF16), pltpu.VMEM((all_rows, 1), F32)],
        compiler_params=_params("arbitrary", "arbitrary"),
        name="neighbourhood_attention",
    )(q, k, v, kx, vx, bias_tab)


def kernel(x, c, ctx, c_ctx, ada_w, ada_b, norm1_g, norm2_g, ffn_w_gate, ffn_w_up, ffn_w_down,
           ev_w_in, ev_w_out, ev_q_norm, ev_k_norm, ev_sink,
           od_w_in, od_w_out, od_q_norm, od_k_norm, od_rel_bias):
    assert x.shape == (BATCH, SEQ, D_MODEL) and ctx.shape == (BATCH, CTX_LEN, D_MODEL)
    wc, m1, m3, twc, tws, mctx = _fourier_tables()
    hm = _head_mean_matrix()
    rope_tabs = _rope_tables()
    lane_gain = lambda gvec: jnp.tile(gvec, RMS_W // HEAD_DIM).reshape(1, RMS_W)
    lat_row = lambda b: b
    ctx_row = lambda b: CTX_MOD_ROW
    tm = 512
    n_ctx = BATCH * CTX_LEN
    as_seq = lambda a: a.reshape(1, n_ctx, a.shape[-1])
    per_batch = lambda a: a.reshape(BATCH, CTX_LEN, a.shape[-1])

    cs = jnp.concatenate([c, c_ctx[None, :], jnp.zeros((MOD_ROWS - BATCH - 1, D_MODEL), F32)], axis=0)
    mod = _modulation(cs, ada_w, ada_b).reshape(DEPTH, MOD_ROWS, 1, 6 * D_MODEL)

    w_in0 = ev_w_in[0].astype(BF16)
    w_out0 = ev_w_out[0].astype(BF16)
    g1 = norm1_g[0].reshape(1, D_MODEL)
    g2 = norm2_g[0].reshape(1, D_MODEL)
    ffn = (ffn_w_gate.astype(BF16), ffn_w_up.astype(BF16), ffn_w_down.astype(BF16))
    qg, kg = lane_gain(ev_q_norm[0]), lane_gain(ev_k_norm[0])
    y_l, q_l, k_l, v_l = _in_even(x, mod[0], lat_row, g1, w_in0, qg, kg, hm, wc, rope_tabs, tm)
    y_c, q_c, k_c, v_c = _in_even(as_seq(ctx), mod[0], ctx_row, g1, w_in0, qg, kg, hm, wc, None, tm)
    y_c, q_c = per_batch(y_c), per_batch(q_c)
    f_l = _fourier_latent(y_l, m1, m3, twc, tws)
    f_c = _fourier_ctx(y_c, mctx)
    a_l = _win_attention(q_l, k_l, v_l, k_c, v_c, ev_sink[0])
    a_c = _ctx_attention_even(q_c, k_c, v_c, ev_sink[0])
    x1 = _out_ffn("even_latent", (f_l, a_l), x, mod[0], lat_row, g2, w_out0, 0, *ffn, tm)
    y1 = _out_ffn("even_ctx", (as_seq(f_c), as_seq(a_c)), as_seq(ctx), mod[0], ctx_row, g2, w_out0, 0, *ffn, tm)

    w_in1 = od_w_in[0].astype(BF16)
    w_out1 = od_w_out[0].astype(BF16)
    g1 = norm1_g[1].reshape(1, D_MODEL)
    g2 = norm2_g[1].reshape(1, D_MODEL)
    qg, kg = lane_gain(od_q_norm[0]), lane_gain(od_k_norm[0])
    q_l, k_l, v_l = _in_odd(x1, mod[1], lat_row, g1, w_in1, qg, kg, hm, True, tm)
    k_c, v_c = _in_odd(y1, mod[1], ctx_row, g1, w_in1, qg, kg, hm, False, tm)
    bias_tab = _na_bias_table(od_rel_bias[0], _na_onehot())
    a_l = _neighbourhood_attention(q_l, k_l, v_l, k_c, v_c, bias_tab)
    return _out_ffn("odd", (a_l,), x1, mod[1], lat_row, g2, w_out1, 1, *ffn, tm)
```

```python
import functools
import math

import numpy as np
import jax
import jax.numpy as jnp
from jax import lax
from jax.experimental import pallas as pl
from jax.experimental.pallas import tpu as pltpu

D_MODEL = 1024
BATCH = 4
SEQ = 4096
DEPTH = 2
GRID_W = 64
CTX_LEN = 256
HEAD_DIM = 64
EPS = 1e-6
NEG = -1e30
FOURIER_WIDTH = D_MODEL // 2
FOURIER_GROUPS = 4
FOURIER_GROUP_CH = FOURIER_WIDTH // FOURIER_GROUPS
WIN_Q_HEADS = (D_MODEL // 2) // HEAD_DIM
WIN_KV_HEADS = 2
WIN_GROUP = WIN_Q_HEADS // WIN_KV_HEADS
WIN_RADIUS = 128
WIN_BLOCK = 128
QW = WIN_Q_HEADS * HEAD_DIM
KW = WIN_KV_HEADS * HEAD_DIM
EV_IN_WIDTH = FOURIER_WIDTH + QW + 2 * KW
NA_HEADS = D_MODEL // HEAD_DIM
NA_KH = 8
NA_KW = 16
NA_WIDTH = NA_HEADS * HEAD_DIM
ROPE_THETA = 10000.0
ROPE_FREQS = HEAD_DIM // 4
D_FF = ((8 * D_MODEL // 3 + 255) // 256) * 256
GRID_ROWS = SEQ // GRID_W

LANES = 128
MOD_ROWS = 8
CTX_MOD_ROW = BATCH
FFT_N = 64
FF_CHUNK = 256
SOFTMAX_ROWS = 32
RMS_W = 256
SUB_ROWS = 512
VMEM_LIMIT = 60 * 1024 * 1024

LOG2E = math.log2(math.e)
QK_SCALE = LOG2E / math.sqrt(HEAD_DIM)

BF16 = jnp.bfloat16
F32 = jnp.float32

assert DEPTH == 2 and SEQ == FFT_N * FFT_N and D_FF % FF_CHUNK == 0


def _params(*sem):
    return pltpu.CompilerParams(dimension_semantics=sem, vmem_limit_bytes=VMEM_LIMIT)


def _dot(a, b):
    return jnp.dot(a, b, preferred_element_type=F32)


def _dot_nt(a, b):
    return lax.dot_general(a, b, (((1,), (1,)), ((), ())), preferred_element_type=F32)


def _silu(x):
    return x / (1.0 + jnp.exp(-x))


def _const_spec(shape):
    nd = len(shape)
    return pl.BlockSpec(shape, lambda *_: (0,) * nd, pipeline_mode=pl.Buffered(1))


def _dft_cos_sin(n):
    idx = (np.arange(n)[:, None] * np.arange(n)[None, :]) % n
    ang = 2.0 * np.pi * idx / n
    return np.cos(ang), np.sin(ang)


def _fourier_tables():
    cc, sc = _dft_cos_sin(FOURIER_GROUP_CH)
    wc = np.concatenate([cc, -sc], axis=1) / math.sqrt(FOURIER_GROUP_CH)
    c64, s64 = _dft_cos_sin(FFT_N)
    m1 = np.concatenate([c64, -s64], axis=0) / math.sqrt(FFT_N)
    m3 = np.concatenate([c64, s64], axis=1) / math.sqrt(FFT_N)
    tw = (np.arange(FFT_N)[:, None] * np.arange(FFT_N)[None, :]) % SEQ
    tw = 2.0 * np.pi * tw / SEQ
    twc = np.repeat(np.cos(tw)[:, :, None], LANES, axis=2)
    tws = np.repeat(np.sin(tw)[:, :, None], LANES, axis=2)
    cx, sx = _dft_cos_sin(CTX_LEN)
    mctx = np.concatenate([cx, sx], axis=1) / math.sqrt(CTX_LEN)
    as32 = lambda a: jnp.asarray(a, F32)
    return (as32(wc).astype(BF16), as32(m1).astype(BF16), as32(m3).astype(BF16),
            as32(twc), as32(tws), as32(mctx).astype(BF16))


def _head_mean_matrix():
    blk = np.kron(np.eye(RMS_W // HEAD_DIM), np.ones((HEAD_DIM, HEAD_DIM))) / HEAD_DIM
    return jnp.asarray(blk, BF16)


def _rope_tables():
    t = jnp.arange(SEQ, dtype=jnp.int32)
    row = (t // GRID_W).astype(F32)
    col = (t % GRID_W).astype(F32)
    inv = ROPE_THETA ** (-jnp.arange(ROPE_FREQS, dtype=F32) / ROPE_FREQS)
    ang_row = row[:, None] * inv[None, :]
    ang_col = col[:, None] * inv[None, :]
    zero = jnp.zeros_like(ang_row)
    cos = jnp.concatenate([jnp.cos(ang_row)] * 2 + [jnp.cos(ang_col)] * 2, axis=1)
    sin_hi = jnp.concatenate([-jnp.sin(ang_row), zero, -jnp.sin(ang_col), zero], axis=1)
    sin_lo = jnp.concatenate([zero, jnp.sin(ang_row), zero, jnp.sin(ang_col)], axis=1)
    rep = LANES // HEAD_DIM
    return jnp.tile(cos, (1, rep)), jnp.tile(sin_hi, (1, rep)), jnp.tile(sin_lo, (1, rep))


def _na_onehot():
    cq = np.arange(GRID_W)
    dc = np.clip(cq[None, :] - cq[:, None] + NA_KW - 1, 0, 2 * NA_KW - 2)
    oh = np.zeros((LANES, GRID_W * GRID_W), np.float32)
    oh[dc.reshape(-1), np.arange(GRID_W * GRID_W)] = 1.0
    return jnp.asarray(oh, BF16)


def _mod_kernel(cs_ref, w_ref, b_ref, o_ref):
    s = _silu(cs_ref[...]).astype(BF16)
    o_ref[0] = _dot(s, w_ref[0].astype(BF16)) + b_ref[0]


def _modulation(cs, ada_w, ada_b):
    tn = 1536
    return pl.pallas_call(
        _mod_kernel,
        grid=(DEPTH, 6 * D_MODEL // tn),
        in_specs=[pl.BlockSpec((MOD_ROWS, D_MODEL), lambda i, j: (0, 0)),
                  pl.BlockSpec((1, D_MODEL, tn), lambda i, j: (i, 0, j)),
                  pl.BlockSpec((1, 1, tn), lambda i, j: (i, 0, j))],
        out_specs=pl.BlockSpec((1, MOD_ROWS, tn), lambda i, j: (i, 0, j)),
        out_shape=jax.ShapeDtypeStruct((DEPTH, MOD_ROWS, 6 * D_MODEL), F32),
        compiler_params=_params("arbitrary", "arbitrary"),
        name="ada_modulation",
    )(cs, ada_w, ada_b.reshape(DEPTH, 1, 6 * D_MODEL))


def _mod_slice(mod_ref, k):
    return mod_ref[0, :, k * D_MODEL:(k + 1) * D_MODEL]


def _rms_mod(x, g, scale, shift):
    y = x * lax.rsqrt(jnp.mean(x * x, axis=-1, keepdims=True) + EPS)
    return (y * g) * (1.0 + scale) + shift


def _head_rms(t, gain_ref, hm_ref):
    w = t.shape[1]
    ms = _dot((t * t).astype(BF16), hm_ref[:w, :w])
    return t * lax.rsqrt(ms + EPS) * gain_ref[:, :w]


def _rope(t, cos, sin_hi, sin_lo):
    up = pltpu.roll(t, LANES - ROPE_FREQS, axis=1)
    dn = pltpu.roll(t, ROPE_FREQS, axis=1)
    return t * cos + up * sin_hi + dn * sin_lo


def _in_even_kernel(latent, x_ref, mod_ref, g_ref, w_ref, qg_ref, kg_ref, hm_ref, wc_ref, *rest):
    if latent:
        cos_ref, sh_ref, sl_ref, y_ref, q_ref, k_ref, v_ref = rest
    else:
        y_ref, q_ref, k_ref, v_ref = rest
    F = FOURIER_WIDTH
    lo_lanes = lax.broadcasted_iota(jnp.int32, (1, LANES), 1) < HEAD_DIM

    for r in range(x_ref.shape[1] // SUB_ROWS):
        rows = slice(r * SUB_ROWS, (r + 1) * SUB_ROWS)
        h = _rms_mod(x_ref[0, rows], g_ref[...], _mod_slice(mod_ref, 1), _mod_slice(mod_ref, 0)).astype(BF16)
        f = _dot(h, w_ref[:, :F]).astype(BF16)
        for g in range(FOURIER_GROUPS):
            yg = _dot(f[:, g * LANES:(g + 1) * LANES], wc_ref[...])
            y_ref[0, rows, g * LANES:(g + 1) * LANES] = yg[:, :LANES].astype(BF16)
            y_ref[0, rows, F + g * LANES:F + (g + 1) * LANES] = yg[:, LANES:].astype(BF16)

        def qk(t, gain_ref):
            t = _head_rms(t, gain_ref, hm_ref)
            if latent:
                t = jnp.concatenate([_rope(t[:, i * LANES:(i + 1) * LANES], cos_ref[rows], sh_ref[rows], sl_ref[rows])
                                     for i in range(t.shape[1] // LANES)], axis=1)
            return t

        q = _dot(h, w_ref[:, F:F + QW])
        for j in range(QW // RMS_W):
            qj = qk(q[:, j * RMS_W:(j + 1) * RMS_W], qg_ref) * QK_SCALE
            q_ref[0, rows, j * RMS_W:(j + 1) * RMS_W] = qj.astype(BF16)
        kv = _dot(h, w_ref[:, F + QW:])

        def store_dup(t, o_ref):
            sw = pltpu.roll(t, HEAD_DIM, axis=1)
            o_ref[0, 0, rows] = jnp.where(lo_lanes, t, sw).astype(BF16)
            o_ref[0, 1, rows] = jnp.where(lo_lanes, sw, t).astype(BF16)

        store_dup(qk(kv[:, :KW], kg_ref), k_ref)
        store_dup(kv[:, KW:], v_ref)


def _in_even(x, mod, mod_row, g, w_in, q_g, k_g, hm, wc, rope_tabs, tm):
    B, L, _ = x.shape
    latent = rope_tabs is not None
    in_specs = [pl.BlockSpec((1, tm, D_MODEL), lambda b, t: (b, t, 0)),
                pl.BlockSpec((1, 1, 6 * D_MODEL), lambda b, t: (mod_row(b), 0, 0)),
                _const_spec((1, D_MODEL)),
                _const_spec((D_MODEL, EV_IN_WIDTH)),
                _const_spec((1, RMS_W)), _const_spec((1, RMS_W)),
                _const_spec((RMS_W, RMS_W)), _const_spec((LANES, 2 * LANES))]
    args = [x, mod, g, w_in, q_g, k_g, hm, wc]
    tok = lambda w: pl.BlockSpec((1, tm, w), lambda b, t: (b, t, 0))
    kv_spec = pl.BlockSpec((1, WIN_KV_HEADS, tm, LANES), lambda b, t: (b, 0, t, 0))
    if latent:
        in_specs += [pl.BlockSpec((tm, LANES), lambda b, t: (t, 0))] * 3
        args += list(rope_tabs)
    return pl.pallas_call(
        functools.partial(_in_even_kernel, latent),
        grid=(B, L // tm),
        in_specs=in_specs,
        out_specs=[tok(2 * FOURIER_WIDTH), tok(QW), kv_spec, kv_spec],
        out_shape=[jax.ShapeDtypeStruct((B, L, 2 * FOURIER_WIDTH), BF16),
                   jax.ShapeDtypeStruct((B, L, QW), BF16),
                   jax.ShapeDtypeStruct((B, WIN_KV_HEADS, L, LANES), BF16),
                   jax.ShapeDtypeStruct((B, WIN_KV_HEADS, L, LANES), BF16)],
        compiler_params=_params("arbitrary", "arbitrary"),
        name="in_even_latent" if latent else "in_even_ctx",
    )(*args)


def _fft_stage1_kernel(n2t, y_ref, m1_ref, twc_ref, tws_ref, o_ref):
    F = FOURIER_WIDTH
    for j in range(n2t):
        a = _dot(m1_ref[...], y_ref[0, :, j * 2 * F:(j + 1) * 2 * F])
        top, bot = a[:FFT_N], a[FFT_N:]
        ar = top[:, :F] - bot[:, F:]
        ai = top[:, F:] + bot[:, :F]
        tc = jnp.tile(twc_ref[j], (1, F // LANES))
        ts = jnp.tile(tws_ref[j], (1, F // LANES))
        o_ref[0, 0, :, j * F:(j + 1) * F] = (ar * tc + ai * ts).astype(BF16)
        o_ref[0, 1, :, j * F:(j + 1) * F] = (ai * tc - ar * ts).astype(BF16)


def _fft_stage2_kernel(k1t, b_ref, m3_ref, o_ref):
    for j in range(k1t):
        rhs = jnp.concatenate([b_ref[0, 0, j], b_ref[0, 1, j]], axis=0)
        o_ref[0, j] = _dot(m3_ref[...], rhs)


def _fourier_latent(y, m1, m3, twc, tws):
    B = y.shape[0]
    F = FOURIER_WIDTH
    n2t = 8
    yv = y.reshape(B, FFT_N, FFT_N * 2 * F)
    bh = pl.pallas_call(
        functools.partial(_fft_stage1_kernel, n2t),
        grid=(B, FFT_N // n2t),
        in_specs=[pl.BlockSpec((1, FFT_N, n2t * 2 * F), lambda b, t: (b, 0, t)),
                  _const_spec((2 * FFT_N, FFT_N)),
                  pl.BlockSpec((n2t, FFT_N, LANES), lambda b, t: (t, 0, 0)),
                  pl.BlockSpec((n2t, FFT_N, LANES), lambda b, t: (t, 0, 0))],
        out_specs=pl.BlockSpec((1, 2, FFT_N, n2t * F), lambda b, t: (b, 0, 0, t)),
        out_shape=jax.ShapeDtypeStruct((B, 2, FFT_N, FFT_N * F), BF16),
        compiler_params=_params("arbitrary", "arbitrary"),
        name="fft_stage1",
    )(yv, m1, twc, tws)
    k1t = 8
    bv = bh.reshape(B, 2, FFT_N, FFT_N, F)
    return pl.pallas_call(
        functools.partial(_fft_stage2_kernel, k1t),
        grid=(B, FFT_N // k1t),
        in_specs=[pl.BlockSpec((1, 2, k1t, FFT_N, F), lambda b, t: (b, 0, t, 0, 0)),
                  _const_spec((FFT_N, 2 * FFT_N))],
        out_specs=pl.BlockSpec((1, k1t, FFT_N, F), lambda b, t: (b, t, 0, 0)),
        out_shape=jax.ShapeDtypeStruct((B, FFT_N, FFT_N, F), F32),
        compiler_params=_params("arbitrary", "arbitrary"),
        name="fft_stage2",
    )(bv, m3)


def _fourier_ctx_kernel(y_ref, m_ref, o_ref):
    F = FOURIER_WIDTH
    rhs = jnp.concatenate([y_ref[0, :, :F], y_ref[0, :, F:]], axis=0)
    o_ref[0] = _dot(m_ref[...], rhs)


def _fourier_ctx(y, mctx):
    B = y.shape[0]
    return pl.pallas_call(
        _fourier_ctx_kernel,
        grid=(B,),
        in_specs=[pl.BlockSpec((1, CTX_LEN, 2 * FOURIER_WIDTH), lambda b: (b, 0, 0)),
                  _const_spec((CTX_LEN, 2 * CTX_LEN))],
        out_specs=pl.BlockSpec((1, CTX_LEN, FOURIER_WIDTH), lambda b: (b, 0, 0)),
        out_shape=jax.ShapeDtypeStruct((B, CTX_LEN, FOURIER_WIDTH), F32),
        compiler_params=_params("arbitrary"),
        name="fourier_ctx",
    )(y, mctx)


def _gqa_group(q_slabs, keys, vals, masks, sinks, rows, o_ref, first_slab, scratch):
    s_ref, p_ref, inv_ref = scratch
    _gqa_scores(q_slabs, keys, rows, s_ref)
    _gqa_attend(vals, masks, sinks, rows, lambda sl: o_ref.at[0, :, sl], first_slab, scratch)


def _gqa_scores(q_slabs, keys, rows, s_ref):
    lo_lanes = lax.broadcasted_iota(jnp.int32, (rows, LANES), 1) < HEAD_DIM
    zero = jnp.zeros((rows, LANES), BF16)
    qs = jnp.concatenate([jnp.where(lo_lanes if half == 0 else ~lo_lanes, qslab, zero)
                          for qslab in q_slabs for half in range(2)], axis=0)
    s_ref[...] = _dot_nt(qs, keys)


def _gqa_attend(vals, masks, sinks, rows, out_slab, first_slab, scratch):
    s_ref, p_ref, inv_ref = scratch
    lo_lanes = lax.broadcasted_iota(jnp.int32, (rows, LANES), 1) < HEAD_DIM
    for c in range(WIN_GROUP * rows // SOFTMAX_ROWS):
        rs = slice(c * SOFTMAX_ROWS, (c + 1) * SOFTMAX_ROWS)
        s = s_ref[rs, :]
        if masks is not None:
            mask, w = masks
            r0 = (c * SOFTMAX_ROWS) % rows
            s = jnp.concatenate([jnp.where(mask[r0:r0 + SOFTMAX_ROWS], s[:, :w], NEG), s[:, w:]], axis=1)
        sk = sinks[(c * SOFTMAX_ROWS) // rows]
        m = jnp.maximum(jnp.max(s, axis=-1, keepdims=True), sk)
        e = jnp.exp2(s - m)
        inv_ref[rs, :] = 1.0 / (jnp.sum(e, axis=-1, keepdims=True) + jnp.exp2(sk - m))
        p_ref[rs, :] = e.astype(BF16)
    o = _dot(p_ref[...], vals) * inv_ref[...]
    for i in range(WIN_GROUP // 2):
        a = o[(2 * i) * rows:(2 * i + 1) * rows]
        b = o[(2 * i + 1) * rows:(2 * i + 2) * rows]
        sl = slice((first_slab + i) * LANES, (first_slab + i + 1) * LANES)
        out_slab(sl)[...] = jnp.where(lo_lanes, a, b).astype(BF16)


def _gqa_scratch(rows, n_keys):
    one = [pltpu.VMEM((WIN_GROUP * rows, n_keys), F32), pltpu.VMEM((WIN_GROUP * rows, n_keys), BF16),
           pltpu.VMEM((WIN_GROUP * rows, 1), F32)]
    return one * WIN_KV_HEADS


def _group_sinks(sink_ref, kv):
    return [sink_ref[kv * WIN_GROUP + g] * LOG2E for g in range(WIN_GROUP)]


def _group_slab(kv, i):
    return slice((kv * (WIN_GROUP // 2) + i) * LANES, (kv * (WIN_GROUP // 2) + i + 1) * LANES)


def _window_bands():
    i = np.arange(WIN_BLOCK)[:, None]
    j = np.arange(3 * WIN_BLOCK)[None, :]
    bands = [np.abs(j - off - i) <= WIN_RADIUS for off in (0, WIN_BLOCK, 2 * WIN_BLOCK)]
    return jnp.asarray(np.stack(bands).astype(np.float32))


def _win_attn_kernel(blocks_per_step, sink_ref, band_ref, q_ref, k_ref, v_ref, kx_ref, vx_ref, o_ref,
                     s0_ref, s1_ref, p_ref, inv_ref):
    t = pl.program_id(1)
    nb = SEQ // WIN_BLOCK
    nw = 3 * WIN_BLOCK

    def place(j):
        n = t * blocks_per_step + j
        start = pl.multiple_of(jnp.clip((n - 1) * WIN_BLOCK, 0, SEQ - nw), WIN_BLOCK)
        sel = jnp.where(n == 0, 0, jnp.where(n == nb - 1, 2, 1))
        return pl.multiple_of(j * WIN_BLOCK, WIN_BLOCK), start, sel

    def scores(j, kv, s_ref):
        qoff, start, _ = place(j)
        keys = jnp.concatenate([k_ref[0, kv, pl.ds(start, nw), :], kx_ref[0, kv]], axis=0)
        q_slabs = [q_ref[0, pl.ds(qoff, WIN_BLOCK), _group_slab(kv, i)] for i in range(WIN_GROUP // 2)]
        _gqa_scores(q_slabs, keys, WIN_BLOCK, s_ref)

    def attend(j, kv, s_ref):
        qoff, start, sel = place(j)
        vals = jnp.concatenate([v_ref[0, kv, pl.ds(start, nw), :], vx_ref[0, kv]], axis=0)
        _gqa_attend(vals, (band_ref[sel] > 0.0, nw), _group_sinks(sink_ref, kv), WIN_BLOCK,
                    lambda sl: o_ref.at[0, pl.ds(qoff, WIN_BLOCK), sl], kv * (WIN_GROUP // 2),
                    (s_ref, p_ref, inv_ref))

    last = blocks_per_step - 1
    scores(0, 0, s0_ref)

    def one_block(j, carry):
        scores(j, 1, s1_ref)
        attend(j, 0, s0_ref)
        scores(j + 1, 0, s0_ref)
        attend(j, 1, s1_ref)
        return carry

    lax.fori_loop(0, last, one_block, 0)
    scores(last, 1, s1_ref)
    attend(last, 0, s0_ref)
    attend(last, 1, s1_ref)


def _win_attention(q, k, v, kx, vx, sink):
    B, L, _ = q.shape
    blocks_per_step = 8
    rows = blocks_per_step * WIN_BLOCK
    n_keys = 3 * WIN_BLOCK + CTX_LEN
    full = pl.BlockSpec((1, WIN_KV_HEADS, L, LANES), lambda b, t: (b, 0, 0, 0))
    ctx_spec = pl.BlockSpec((1, WIN_KV_HEADS, CTX_LEN, LANES), lambda b, t: (0, 0, b, 0))
    tok = pl.BlockSpec((1, rows, QW), lambda b, t: (b, t, 0))
    return pl.pallas_call(
        functools.partial(_win_attn_kernel, blocks_per_step),
        grid=(B, L // rows),
        in_specs=[pl.BlockSpec(memory_space=pltpu.SMEM),
                  _const_spec((3, WIN_BLOCK, 3 * WIN_BLOCK)),
                  tok, full, full, ctx_spec, ctx_spec],
        out_specs=tok,
        out_shape=jax.ShapeDtypeStruct((B, L, QW), BF16),
        scratch_shapes=[pltpu.VMEM((WIN_GROUP * WIN_BLOCK, n_keys), F32), pltpu.VMEM((WIN_GROUP * WIN_BLOCK, n_keys), F32),
                        pltpu.VMEM((WIN_GROUP * WIN_BLOCK, n_keys), BF16), pltpu.VMEM((WIN_GROUP * WIN_BLOCK, 1), F32)],
        compiler_params=_params("arbitrary", "arbitrary"),
        name="window_attention",
    )(sink, _window_bands(), q, k, v, kx, vx)


def _ctx_attn_even_kernel(sink_ref, q_ref, k_ref, v_ref, o_ref, *scratch):
    for kv in range(WIN_KV_HEADS):
        q_slabs = [q_ref[0, :, _group_slab(kv, i)] for i in range(WIN_GROUP // 2)]
        _gqa_group(q_slabs, k_ref[0, kv], v_ref[0, kv], None, _group_sinks(sink_ref, kv), CTX_LEN, o_ref,
                   kv * (WIN_GROUP // 2), scratch[3 * kv:3 * kv + 3])


def _ctx_attention_even(q, k, v, sink):
    B = q.shape[0]
    kv_spec = pl.BlockSpec((1, WIN_KV_HEADS, CTX_LEN, LANES), lambda b: (0, 0, b, 0))
    return pl.pallas_call(
        _ctx_attn_even_kernel,
        grid=(B,),
        in_specs=[pl.BlockSpec(memory_space=pltpu.SMEM),
                  pl.BlockSpec((1, CTX_LEN, QW), lambda b: (b, 0, 0)),
                  kv_spec, kv_spec],
        out_specs=pl.BlockSpec((1, CTX_LEN, QW), lambda b: (b, 0, 0)),
        out_shape=jax.ShapeDtypeStruct((B, CTX_LEN, QW), BF16),
        scratch_shapes=_gqa_scratch(CTX_LEN, CTX_LEN),
        compiler_params=_params("arbitrary"),
        name="ctx_attention_even",
    )(sink, q, k, v)


def _out_ffn_kernel(mode, *refs):
    if mode == "even_latent":
        f_ref, a_ref, x_ref, mod_ref, g_ref, wo_ref, wg_ref, wu_ref, wd_ref, o_ref = refs
        fm = jnp.concatenate([f_ref[0, :, j, :] for j in range(f_ref.shape[2])], axis=0).astype(BF16)
        o = _dot(fm, wo_ref[:FOURIER_WIDTH]) + _dot(a_ref[0], wo_ref[FOURIER_WIDTH:])
    elif mode == "even_ctx":
        f_ref, a_ref, x_ref, mod_ref, g_ref, wo_ref, wg_ref, wu_ref, wd_ref, o_ref = refs
        o = _dot(f_ref[0].astype(BF16), wo_ref[:FOURIER_WIDTH]) + _dot(a_ref[0], wo_ref[FOURIER_WIDTH:])
    else:
        a_ref, x_ref, mod_ref, g_ref, wo_ref, wg_ref, wu_ref, wd_ref, o_ref = refs
        o = _dot(jnp.concatenate([a_ref[0, j] for j in range(a_ref.shape[1])], axis=1), wo_ref[...])
    x1 = x_ref[0] + _mod_slice(mod_ref, 2) * o
    h = _rms_mod(x1, g_ref[...], _mod_slice(mod_ref, 4), _mod_slice(mod_ref, 3)).astype(BF16)
    acc = jnp.zeros_like(x1)
    for c in range(D_FF // FF_CHUNK):
        cs = slice(c * FF_CHUNK, (c + 1) * FF_CHUNK)
        a = _silu(_dot(h, wg_ref[0, :, cs])) * _dot(h, wu_ref[0, :, cs])
        acc = acc + _dot(a.astype(BF16), wd_ref[0, cs, :])
    o_ref[0] = x1 + _mod_slice(mod_ref, 5) * acc


def _out_ffn(mode, mix, x, mod, mod_row, g, w_out, layer, wg, wu, wd, tm):
    B, L, _ = x.shape
    tok = lambda w: pl.BlockSpec((1, tm, w), lambda b, t: (b, t, 0))
    ffn_spec = lambda r, c: pl.BlockSpec((1, r, c), lambda b, t: (layer, 0, 0), pipeline_mode=pl.Buffered(1))
    if mode == "even_latent":
        assert tm % FFT_N == 0
        mix_specs = [pl.BlockSpec((1, FFT_N, tm // FFT_N, FOURIER_WIDTH), lambda b, t: (b, 0, t, 0)), tok(QW)]
    elif mode == "even_ctx":
        mix_specs = [tok(FOURIER_WIDTH), tok(QW)]
    else:
        mix_specs = [pl.BlockSpec((1, NA_WIDTH // LANES, tm, LANES), lambda b, t: (b, 0, t, 0))]
    return pl.pallas_call(
        functools.partial(_out_ffn_kernel, mode),
        grid=(B, L // tm),
        in_specs=mix_specs + [tok(D_MODEL),
                              pl.BlockSpec((1, 1, 6 * D_MODEL), lambda b, t: (mod_row(b), 0, 0)),
                              _const_spec((1, D_MODEL)),
                              _const_spec((D_MODEL, D_MODEL)),
                              ffn_spec(D_MODEL, D_FF), ffn_spec(D_MODEL, D_FF), ffn_spec(D_FF, D_MODEL)],
        out_specs=tok(D_MODEL),
        out_shape=jax.ShapeDtypeStruct((B, L, D_MODEL), F32),
        compiler_params=_params("arbitrary", "arbitrary"),
        name="out_ffn_" + mode,
    )(*mix, x, mod, g, w_out, wg, wu, wd)


def _in_odd_kernel(with_q, x_ref, mod_ref, g_ref, w_ref, qg_ref, kg_ref, hm_ref, *outs):
    W = NA_WIDTH
    slabs = RMS_W // LANES
    if with_q:
        q_ref, k_ref, v_ref = outs
    else:
        k_ref, v_ref = outs

    for r in range(x_ref.shape[1] // SUB_ROWS):
        rows = slice(r * SUB_ROWS, (r + 1) * SUB_ROWS)
        h = _rms_mod(x_ref[0, rows], g_ref[...], _mod_slice(mod_ref, 1), _mod_slice(mod_ref, 0)).astype(BF16)

        def normed(t, gain_ref, scale, o_ref):
            for j in range(W // RMS_W):
                tj = (_head_rms(t[:, j * RMS_W:(j + 1) * RMS_W], gain_ref, hm_ref) * scale).astype(BF16)
                for i in range(slabs):
                    o_ref[0, j * slabs + i, rows] = tj[:, i * LANES:(i + 1) * LANES]

        if with_q:
            normed(_dot(h, w_ref[:, :W]), qg_ref, QK_SCALE, q_ref)
        normed(_dot(h, w_ref[:, W:2 * W]), kg_ref, 1.0, k_ref)
        v = _dot(h, w_ref[:, 2 * W:]).astype(BF16)
        for j in range(W // LANES):
            v_ref[0, j, rows] = v[:, j * LANES:(j + 1) * LANES]


def _in_odd(x, mod, mod_row, g, w_in, q_g, k_g, hm, with_q, tm):
    B, L, _ = x.shape
    pairs = NA_WIDTH // LANES
    tok = pl.BlockSpec((1, pairs, tm, LANES), lambda b, t: (b, 0, t, 0))
    n_out = 3 if with_q else 2
    return pl.pallas_call(
        functools.partial(_in_odd_kernel, with_q),
        grid=(B, L // tm),
        in_specs=[pl.BlockSpec((1, tm, D_MODEL), lambda b, t: (b, t, 0)),
                  pl.BlockSpec((1, 1, 6 * D_MODEL), lambda b, t: (mod_row(b), 0, 0)),
                  _const_spec((1, D_MODEL)),
                  _const_spec((D_MODEL, 3 * NA_WIDTH)),
                  _const_spec((1, RMS_W)), _const_spec((1, RMS_W)), _const_spec((RMS_W, RMS_W))],
        out_specs=[tok] * n_out,
        out_shape=[jax.ShapeDtypeStruct((B, pairs, L, LANES), BF16)] * n_out,
        compiler_params=_params("arbitrary", "arbitrary"),
        name="in_odd" if with_q else "in_odd_ctx",
    )(x, mod, g, w_in, q_g, k_g, hm)


def _bias_table_kernel(rb_ref, oh_ref, o_ref):
    x = rb_ref[...]
    hi = x.astype(BF16)
    r1 = x - hi.astype(F32)
    mid = r1.astype(BF16)
    lo = (r1 - mid.astype(F32)).astype(BF16)
    oh = oh_ref[...]
    o_ref[...] = ((_dot(hi, oh) + _dot(mid, oh)) + _dot(lo, oh)) * LOG2E


def _na_bias_table(rel_bias, onehot):
    H = NA_HEADS
    ndr = 2 * NA_KH - 1
    rows = H * ndr
    rb = jnp.pad(rel_bias.reshape(rows, 2 * NA_KW - 1), ((0, 0), (0, LANES - (2 * NA_KW - 1))))
    flat = pl.pallas_call(
        _bias_table_kernel,
        out_shape=jax.ShapeDtypeStruct((rows, GRID_W * GRID_W), F32),
        compiler_params=pltpu.CompilerParams(vmem_limit_bytes=VMEM_LIMIT),
        name="na_bias_table",
    )(rb, onehot)
    t = flat.reshape(H, ndr, GRID_W, GRID_W)
    return jnp.concatenate([t[:, :-1], t[:, 1:]], axis=-1)


def _na_kernel(rows_per_step, q_ref, k_ref, v_ref, kx_ref, vx_ref, bias_ref, o_ref, s0_ref, s1_ref, p_ref, inv_ref):
    t = pl.program_id(1)
    n = NA_KH * GRID_W
    cq = lax.broadcasted_iota(jnp.int32, (GRID_W, n), 0)
    ck = lax.broadcasted_iota(jnp.int32, (GRID_W, n), 1) % GRID_W
    c0 = jnp.clip(cq - NA_KW // 2, 0, GRID_W - NA_KW)
    mask = (ck >= c0) & (ck < c0 + NA_KW)
    lo_lanes = lax.broadcasted_iota(jnp.int32, (GRID_W, LANES), 1) < HEAD_DIM

    def window(i):
        r = t * rows_per_step + i
        r0 = jnp.clip(r - NA_KH // 2, 0, GRID_ROWS - NA_KH)
        return r0 - r + NA_KH - 1, pl.multiple_of(r0 * GRID_W, GRID_W), pl.multiple_of(i * GRID_W, GRID_W)

    def scores(i, s_ref):
        dr0, start, qoff = window(i)
        for hp in range(NA_HEADS // 2):
            qs = q_ref[0, hp, pl.ds(qoff, GRID_W), :]
            keys = jnp.concatenate([k_ref[0, hp, pl.ds(start, n), :], kx_ref[0, hp]], axis=0)
            qm = jnp.concatenate([jnp.where(lo_lanes, qs, jnp.zeros_like(qs)),
                                  jnp.where(lo_lanes, jnp.zeros_like(qs), qs)], axis=0)
            s = _dot_nt(qm, keys)
            for half in range(2):
                h = 2 * hp + half
                bias = jnp.concatenate([bias_ref[h, dr0 + 2 * p] for p in range(NA_KH // 2)], axis=1)
                rs = slice(h * GRID_W, (h + 1) * GRID_W)
                hr = slice(half * GRID_W, (half + 1) * GRID_W)
                s_ref[rs, :n] = jnp.where(mask, s[hr, :n] + bias, NEG)
                s_ref[rs, n:] = s[hr, n:]

    def attend(i, s_ref):
        _, start, qoff = window(i)
        for c in range(NA_HEADS * GRID_W // SOFTMAX_ROWS):
            cs = slice(c * SOFTMAX_ROWS, (c + 1) * SOFTMAX_ROWS)
            s = s_ref[cs, :]
            e = jnp.exp2(s - jnp.max(s, axis=-1, keepdims=True))
            inv_ref[cs, :] = 1.0 / jnp.sum(e, axis=-1, keepdims=True)
            p_ref[cs, :] = e.astype(BF16)
        for hp in range(NA_HEADS // 2):
            vals = jnp.concatenate([v_ref[0, hp, pl.ds(start, n), :], vx_ref[0, hp]], axis=0)
            rs = slice(2 * hp * GRID_W, (2 * hp + 2) * GRID_W)
            res = _dot(p_ref[rs, :], vals) * inv_ref[rs, :]
            o_ref[0, hp, pl.ds(qoff, GRID_W), :] = jnp.where(lo_lanes, res[:GRID_W], res[GRID_W:]).astype(BF16)

    last = rows_per_step - 1
    scores(0, s0_ref)

    def two_rows(j, carry):
        scores(2 * j + 1, s1_ref)
        attend(2 * j, s0_ref)
        scores(2 * j + 2, s0_ref)
        attend(2 * j + 1, s1_ref)
        return carry

    lax.fori_loop(0, rows_per_step // 2 - 1, two_rows, 0)
    scores(last, s1_ref)
    attend(last - 1, s0_ref)
    attend(last, s1_ref)


def _neighbourhood_attention(q, k, v, kx, vx, bias_tab):
    B, pairs, L, _ = q.shape
    rows_per_step = 8
    full = pl.BlockSpec((1, pairs, L, LANES), lambda b, t: (b, 0, 0, 0))
    ctx_spec = pl.BlockSpec((1, pairs, CTX_LEN, LANES), lambda b, t: (0, 0, b, 0))
    q_rows = pl.BlockSpec((1, pairs, rows_per_step * GRID_W, LANES), lambda b, t: (b, 0, t, 0))
    n_keys = NA_KH * GRID_W + CTX_LEN
    all_rows = NA_HEADS * GRID_W
    return pl.pallas_call(
        functools.partial(_na_kernel, rows_per_step),
        grid=(B, GRID_ROWS // rows_per_step),
        in_specs=[q_rows, full, full, ctx_spec, ctx_spec, _const_spec(bias_tab.shape)],
        out_specs=q_rows,
        out_shape=jax.ShapeDtypeStruct((B, pairs, L, LANES), BF16),
        scratch_shapes=[pltpu.VMEM((all_rows, n_keys), F32), pltpu.VMEM((all_rows, n_keys), F32),
                        pltpu.VMEM((all_rows, n_keys), BF16), pltpu.VMEM((all_rows, 1), F32)],
        compiler_params=_params("arbitrary", "arbitrary"),
        name="neighbourhood_attention",
    )(q, k, v, kx, vx, bias_tab)


def kernel(x, c, ctx, c_ctx, ada_w, ada_b, norm1_g, norm2_g, ffn_w_gate, ffn_w_up, ffn_w_down,
           ev_w_in, ev_w_out, ev_q_norm, ev_k_norm, ev_sink,
           od_w_in, od_w_out, od_q_norm, od_k_norm, od_rel_bias):
    assert x.shape == (BATCH, SEQ, D_MODEL) and ctx.shape == (BATCH, CTX_LEN, D_MODEL)
    wc, m1, m3, twc, tws, mctx = _fourier_tables()
    hm = _head_mean_matrix()
    rope_tabs = _rope_tables()
    lane_gain = lambda gvec: jnp.tile(gvec, RMS_W // HEAD_DIM).reshape(1, RMS_W)
    lat_row = lambda b: b
    ctx_row = lambda b: CTX_MOD_ROW
    tm = 512
    n_ctx = BATCH * CTX_LEN
    as_seq = lambda a: a.reshape(1, n_ctx, a.shape[-1])
    per_batch = lambda a: a.reshape(BATCH, CTX_LEN, a.shape[-1])

    cs = jnp.concatenate([c, c_ctx[None, :], jnp.zeros((MOD_ROWS - BATCH - 1, D_MODEL), F32)], axis=0)
    mod = _modulation(cs, ada_w, ada_b).reshape(DEPTH, MOD_ROWS, 1, 6 * D_MODEL)

    w_in0 = ev_w_in[0].astype(BF16)
    w_out0 = ev_w_out[0].astype(BF16)
    g1 = norm1_g[0].reshape(1, D_MODEL)
    g2 = norm2_g[0].reshape(1, D_MODEL)
    ffn = (ffn_w_gate.astype(BF16), ffn_w_up.astype(BF16), ffn_w_down.astype(BF16))
    qg, kg = lane_gain(ev_q_norm[0]), lane_gain(ev_k_norm[0])
    y_l, q_l, k_l, v_l = _in_even(x, mod[0], lat_row, g1, w_in0, qg, kg, hm, wc, rope_tabs, 2 * tm)
    y_c, q_c, k_c, v_c = _in_even(as_seq(ctx), mod[0], ctx_row, g1, w_in0, qg, kg, hm, wc, None, tm)
    y_c, q_c = per_batch(y_c), per_batch(q_c)
    f_l = _fourier_latent(y_l, m1, m3, twc, tws)
    f_c = _fourier_ctx(y_c, mctx)
    a_l = _win_attention(q_l, k_l, v_l, k_c, v_c, ev_sink[0])
    a_c = _ctx_attention_even(q_c, k_c, v_c, ev_sink[0])
    x1 = _out_ffn("even_latent", (f_l, a_l), x, mod[0], lat_row, g2, w_out0, 0, *ffn, tm)
    y1 = _out_ffn("even_ctx", (as_seq(f_c), as_seq(a_c)), as_seq(ctx), mod[0], ctx_row, g2, w_out0, 0, *ffn, tm)

    w_in1 = od_w_in[0].astype(BF16)
    w_out1 = od_w_out[0].astype(BF16)
    g1 = norm1_g[1].reshape(1, D_MODEL)
    g2 = norm2_g[1].reshape(1, D_MODEL)
    qg, kg = lane_gain(od_q_norm[0]), lane_gain(od_k_norm[0])
    q_l, k_l, v_l = _in_odd(x1, mod[1], lat_row, g1, w_in1, qg, kg, hm, True, 2 * tm)
    k_c, v_c = _in_odd(y1, mod[1], ctx_row, g1, w_in1, qg, kg, hm, False, tm)
    bias_tab = _na_bias_table(od_rel_bias[0], _na_onehot())
    a_l = _neighbourhood_attention(q_l, k_l, v_l, k_c, v_c, bias_tab)
    return _out_ffn("odd", (a_l,), x1, mod[1], lat_row, g2, w_out1, 1, *ffn, tm)
```

```python
import functools
import math

import numpy as np
import jax
import jax.numpy as jnp
from jax import lax
from jax.experimental import pallas as pl
from jax.experimental.pallas import tpu as pltpu

D_MODEL = 1024
BATCH = 4
SEQ = 4096
DEPTH = 2
GRID_W = 64
CTX_LEN = 256
HEAD_DIM = 64
EPS = 1e-6
NEG = -1e30
FOURIER_WIDTH = D_MODEL // 2
FOURIER_GROUPS = 4
FOURIER_GROUP_CH = FOURIER_WIDTH // FOURIER_GROUPS
WIN_Q_HEADS = (D_MODEL // 2) // HEAD_DIM
WIN_KV_HEADS = 2
WIN_GROUP = WIN_Q_HEADS // WIN_KV_HEADS
WIN_RADIUS = 128
WIN_BLOCK = 128
QW = WIN_Q_HEADS * HEAD_DIM
KW = WIN_KV_HEADS * HEAD_DIM
EV_IN_WIDTH = FOURIER_WIDTH + QW + 2 * KW
NA_HEADS = D_MODEL // HEAD_DIM
NA_KH = 8
NA_KW = 16
NA_WIDTH = NA_HEADS * HEAD_DIM
ROPE_THETA = 10000.0
ROPE_FREQS = HEAD_DIM // 4
D_FF = ((8 * D_MODEL // 3 + 255) // 256) * 256
GRID_ROWS = SEQ // GRID_W

LANES = 128
MOD_ROWS = 8
CTX_MOD_ROW = BATCH
FFT_N = 64
FF_CHUNK = 256
SOFTMAX_ROWS = 32
RMS_W = 256
SUB_ROWS = 512
VMEM_LIMIT = 60 * 1024 * 1024

LOG2E = math.log2(math.e)
QK_SCALE = LOG2E / math.sqrt(HEAD_DIM)

BF16 = jnp.bfloat16
F32 = jnp.float32

assert DEPTH == 2 and SEQ == FFT_N * FFT_N and D_FF % FF_CHUNK == 0


def _params(*sem):
    return pltpu.CompilerParams(dimension_semantics=sem, vmem_limit_bytes=VMEM_LIMIT)


def _dot(a, b):
    return jnp.dot(a, b, preferred_element_type=F32)


def _dot_nt(a, b):
    return lax.dot_general(a, b, (((1,), (1,)), ((), ())), preferred_element_type=F32)


def _silu(x):
    return x / (1.0 + jnp.exp(-x))


def _const_spec(shape):
    nd = len(shape)
    return pl.BlockSpec(shape, lambda *_: (0,) * nd, pipeline_mode=pl.Buffered(1))


def _dft_cos_sin(n):
    idx = (np.arange(n)[:, None] * np.arange(n)[None, :]) % n
    ang = 2.0 * np.pi * idx / n
    return np.cos(ang), np.sin(ang)


def _fourier_tables():
    cc, sc = _dft_cos_sin(FOURIER_GROUP_CH)
    wc = np.concatenate([cc, -sc], axis=1) / math.sqrt(FOURIER_GROUP_CH)
    c64, s64 = _dft_cos_sin(FFT_N)
    m1 = np.concatenate([c64, -s64], axis=0) / math.sqrt(FFT_N)
    m3 = np.concatenate([c64, s64], axis=1) / math.sqrt(FFT_N)
    tw = (np.arange(FFT_N)[:, None] * np.arange(FFT_N)[None, :]) % SEQ
    tw = 2.0 * np.pi * tw / SEQ
    twc = np.repeat(np.cos(tw)[:, :, None], LANES, axis=2)
    tws = np.repeat(np.sin(tw)[:, :, None], LANES, axis=2)
    cx, sx = _dft_cos_sin(CTX_LEN)
    mctx = np.concatenate([cx, sx], axis=1) / math.sqrt(CTX_LEN)
    as32 = lambda a: jnp.asarray(a, F32)
    return (as32(wc).astype(BF16), as32(m1).astype(BF16), as32(m3).astype(BF16),
            as32(twc), as32(tws), as32(mctx).astype(BF16))


def _head_mean_matrix():
    blk = np.kron(np.eye(RMS_W // HEAD_DIM), np.ones((HEAD_DIM, HEAD_DIM))) / HEAD_DIM
    return jnp.asarray(blk, BF16)


def _rope_tables():
    t = jnp.arange(SEQ, dtype=jnp.int32)
    row = (t // GRID_W).astype(F32)
    col = (t % GRID_W).astype(F32)
    inv = ROPE_THETA ** (-jnp.arange(ROPE_FREQS, dtype=F32) / ROPE_FREQS)
    ang_row = row[:, None] * inv[None, :]
    ang_col = col[:, None] * inv[None, :]
    zero = jnp.zeros_like(ang_row)
    cos = jnp.concatenate([jnp.cos(ang_row)] * 2 + [jnp.cos(ang_col)] * 2, axis=1)
    sin_hi = jnp.concatenate([-jnp.sin(ang_row), zero, -jnp.sin(ang_col), zero], axis=1)
    sin_lo = jnp.concatenate([zero, jnp.sin(ang_row), zero, jnp.sin(ang_col)], axis=1)
    rep = LANES // HEAD_DIM
    return jnp.tile(cos, (1, rep)), jnp.tile(sin_hi, (1, rep)), jnp.tile(sin_lo, (1, rep))


def _na_onehot():
    cq = np.arange(GRID_W)
    dc = np.clip(cq[None, :] - cq[:, None] + NA_KW - 1, 0, 2 * NA_KW - 2)
    oh = np.zeros((LANES, GRID_W * GRID_W), np.float32)
    oh[dc.reshape(-1), np.arange(GRID_W * GRID_W)] = 1.0
    return jnp.asarray(oh, BF16)


def _mod_kernel(cs_ref, w_ref, b_ref, o_ref):
    s = _silu(cs_ref[...]).astype(BF16)
    o_ref[0] = _dot(s, w_ref[0].astype(BF16)) + b_ref[0]


def _modulation(cs, ada_w, ada_b):
    tn = 1536
    return pl.pallas_call(
        _mod_kernel,
        grid=(DEPTH, 6 * D_MODEL // tn),
        in_specs=[pl.BlockSpec((MOD_ROWS, D_MODEL), lambda i, j: (0, 0)),
                  pl.BlockSpec((1, D_MODEL, tn), lambda i, j: (i, 0, j)),
                  pl.BlockSpec((1, 1, tn), lambda i, j: (i, 0, j))],
        out_specs=pl.BlockSpec((1, MOD_ROWS, tn), lambda i, j: (i, 0, j)),
        out_shape=jax.ShapeDtypeStruct((DEPTH, MOD_ROWS, 6 * D_MODEL), F32),
        compiler_params=_params("arbitrary", "arbitrary"),
        name="ada_modulation",
    )(cs, ada_w, ada_b.reshape(DEPTH, 1, 6 * D_MODEL))


def _mod_slice(mod_ref, k):
    return mod_ref[0, :, k * D_MODEL:(k + 1) * D_MODEL]


def _rms_mod(x, g, scale, shift):
    y = x * lax.rsqrt(jnp.mean(x * x, axis=-1, keepdims=True) + EPS)
    return (y * g) * (1.0 + scale) + shift


def _head_rms(t, gain_ref, hm_ref):
    w = t.shape[1]
    ms = _dot((t * t).astype(BF16), hm_ref[:w, :w])
    return t * lax.rsqrt(ms + EPS) * gain_ref[:, :w]


def _rope(t, cos, sin_hi, sin_lo):
    up = pltpu.roll(t, LANES - ROPE_FREQS, axis=1)
    dn = pltpu.roll(t, ROPE_FREQS, axis=1)
    return t * cos + up * sin_hi + dn * sin_lo


def _in_even_kernel(latent, x_ref, mod_ref, g_ref, w_ref, qg_ref, kg_ref, hm_ref, wc_ref, *rest):
    if latent:
        cos_ref, sh_ref, sl_ref, y_ref, q_ref, k_ref, v_ref = rest
    else:
        y_ref, q_ref, k_ref, v_ref = rest
    F = FOURIER_WIDTH
    lo_lanes = lax.broadcasted_iota(jnp.int32, (1, LANES), 1) < HEAD_DIM

    for r in range(x_ref.shape[1] // SUB_ROWS):
        rows = slice(r * SUB_ROWS, (r + 1) * SUB_ROWS)
        h = _rms_mod(x_ref[0, rows], g_ref[...], _mod_slice(mod_ref, 1), _mod_slice(mod_ref, 0)).astype(BF16)
        f = _dot(h, w_ref[:, :F]).astype(BF16)
        for g in range(FOURIER_GROUPS):
            yg = _dot(f[:, g * LANES:(g + 1) * LANES], wc_ref[...])
            y_ref[0, rows, g * LANES:(g + 1) * LANES] = yg[:, :LANES].astype(BF16)
            y_ref[0, rows, F + g * LANES:F + (g + 1) * LANES] = yg[:, LANES:].astype(BF16)

        def qk(t, gain_ref):
            t = _head_rms(t, gain_ref, hm_ref)
            if latent:
                t = jnp.concatenate([_rope(t[:, i * LANES:(i + 1) * LANES], cos_ref[rows], sh_ref[rows], sl_ref[rows])
                                     for i in range(t.shape[1] // LANES)], axis=1)
            return t

        q = _dot(h, w_ref[:, F:F + QW])
        for j in range(QW // RMS_W):
            qj = qk(q[:, j * RMS_W:(j + 1) * RMS_W], qg_ref) * QK_SCALE
            q_ref[0, rows, j * RMS_W:(j + 1) * RMS_W] = qj.astype(BF16)
        kv = _dot(h, w_ref[:, F + QW:])

        def store_dup(t, o_ref):
            sw = pltpu.roll(t, HEAD_DIM, axis=1)
            o_ref[0, 0, rows] = jnp.where(lo_lanes, t, sw).astype(BF16)
            o_ref[0, 1, rows] = jnp.where(lo_lanes, sw, t).astype(BF16)

        store_dup(qk(kv[:, :KW], kg_ref), k_ref)
        store_dup(kv[:, KW:], v_ref)


def _in_even(x, mod, mod_row, g, w_in, q_g, k_g, hm, wc, rope_tabs, tm):
    B, L, _ = x.shape
    latent = rope_tabs is not None
    in_specs = [pl.BlockSpec((1, tm, D_MODEL), lambda b, t: (b, t, 0)),
                pl.BlockSpec((1, 1, 6 * D_MODEL), lambda b, t: (mod_row(b), 0, 0)),
                _const_spec((1, D_MODEL)),
                _const_spec((D_MODEL, EV_IN_WIDTH)),
                _const_spec((1, RMS_W)), _const_spec((1, RMS_W)),
                _const_spec((RMS_W, RMS_W)), _const_spec((LANES, 2 * LANES))]
    args = [x, mod, g, w_in, q_g, k_g, hm, wc]
    tok = lambda w: pl.BlockSpec((1, tm, w), lambda b, t: (b, t, 0))
    kv_spec = pl.BlockSpec((1, WIN_KV_HEADS, tm, LANES), lambda b, t: (b, 0, t, 0))
    if latent:
        in_specs += [pl.BlockSpec((tm, LANES), lambda b, t: (t, 0))] * 3
        args += list(rope_tabs)
    return pl.pallas_call(
        functools.partial(_in_even_kernel, latent),
        grid=(B, L // tm),
        in_specs=in_specs,
        out_specs=[tok(2 * FOURIER_WIDTH), tok(QW), kv_spec, kv_spec],
        out_shape=[jax.ShapeDtypeStruct((B, L, 2 * FOURIER_WIDTH), BF16),
                   jax.ShapeDtypeStruct((B, L, QW), BF16),
                   jax.ShapeDtypeStruct((B, WIN_KV_HEADS, L, LANES), BF16),
                   jax.ShapeDtypeStruct((B, WIN_KV_HEADS, L, LANES), BF16)],
        compiler_params=_params("arbitrary", "arbitrary"),
        name="in_even_latent" if latent else "in_even_ctx",
    )(*args)


def _fft_stage1_kernel(n2t, y_ref, m1_ref, twc_ref, tws_ref, o_ref):
    F = FOURIER_WIDTH
    for j in range(n2t):
        a = _dot(m1_ref[...], y_ref[0, :, j * 2 * F:(j + 1) * 2 * F])
        top, bot = a[:FFT_N], a[FFT_N:]
        ar = top[:, :F] - bot[:, F:]
        ai = top[:, F:] + bot[:, :F]
        tc = jnp.tile(twc_ref[j], (1, F // LANES))
        ts = jnp.tile(tws_ref[j], (1, F // LANES))
        o_ref[0, 0, :, j * F:(j + 1) * F] = (ar * tc + ai * ts).astype(BF16)
        o_ref[0, 1, :, j * F:(j + 1) * F] = (ai * tc - ar * ts).astype(BF16)


def _fft_stage2_kernel(k1t, b_ref, m3_ref, o_ref):
    for j in range(k1t):
        rhs = jnp.concatenate([b_ref[0, 0, j], b_ref[0, 1, j]], axis=0)
        o_ref[0, j] = _dot(m3_ref[...], rhs)


def _fourier_latent(y, m1, m3, twc, tws):
    B = y.shape[0]
    F = FOURIER_WIDTH
    n2t = 16
    yv = y.reshape(B, FFT_N, FFT_N * 2 * F)
    bh = pl.pallas_call(
        functools.partial(_fft_stage1_kernel, n2t),
        grid=(B, FFT_N // n2t),
        in_specs=[pl.BlockSpec((1, FFT_N, n2t * 2 * F), lambda b, t: (b, 0, t)),
                  _const_spec((2 * FFT_N, FFT_N)),
                  pl.BlockSpec((n2t, FFT_N, LANES), lambda b, t: (t, 0, 0)),
                  pl.BlockSpec((n2t, FFT_N, LANES), lambda b, t: (t, 0, 0))],
        out_specs=pl.BlockSpec((1, 2, FFT_N, n2t * F), lambda b, t: (b, 0, 0, t)),
        out_shape=jax.ShapeDtypeStruct((B, 2, FFT_N, FFT_N * F), BF16),
        compiler_params=_params("arbitrary", "arbitrary"),
        name="fft_stage1",
    )(yv, m1, twc, tws)
    k1t = 16
    bv = bh.reshape(B, 2, FFT_N, FFT_N, F)
    return pl.pallas_call(
        functools.partial(_fft_stage2_kernel, k1t),
        grid=(B, FFT_N // k1t),
        in_specs=[pl.BlockSpec((1, 2, k1t, FFT_N, F), lambda b, t: (b, 0, t, 0, 0)),
                  _const_spec((FFT_N, 2 * FFT_N))],
        out_specs=pl.BlockSpec((1, k1t, FFT_N, F), lambda b, t: (b, t, 0, 0)),
        out_shape=jax.ShapeDtypeStruct((B, FFT_N, FFT_N, F), F32),
        compiler_params=_params("arbitrary", "arbitrary"),
        name="fft_stage2",
    )(bv, m3)


def _fourier_ctx_kernel(y_ref, m_ref, o_ref):
    F = FOURIER_WIDTH
    rhs = jnp.concatenate([y_ref[0, :, :F], y_ref[0, :, F:]], axis=0)
    o_ref[0] = _dot(m_ref[...], rhs)


def _fourier_ctx(y, mctx):
    B = y.shape[0]
    return pl.pallas_call(
        _fourier_ctx_kernel,
        grid=(B,),
        in_specs=[pl.BlockSpec((1, CTX_LEN, 2 * FOURIER_WIDTH), lambda b: (b, 0, 0)),
                  _const_spec((CTX_LEN, 2 * CTX_LEN))],
        out_specs=pl.BlockSpec((1, CTX_LEN, FOURIER_WIDTH), lambda b: (b, 0, 0)),
        out_shape=jax.ShapeDtypeStruct((B, CTX_LEN, FOURIER_WIDTH), F32),
        compiler_params=_params("arbitrary"),
        name="fourier_ctx",
    )(y, mctx)


def _gqa_group(q_slabs, keys, vals, masks, sinks, rows, o_ref, first_slab, scratch):
    s_ref, p_ref, inv_ref = scratch
    _gqa_scores(q_slabs, keys, rows, s_ref)
    _gqa_attend(vals, masks, sinks, rows, lambda sl: o_ref.at[0, :, sl], first_slab, scratch)


def _gqa_scores(q_slabs, keys, rows, s_ref):
    lo_lanes = lax.broadcasted_iota(jnp.int32, (rows, LANES), 1) < HEAD_DIM
    zero = jnp.zeros((rows, LANES), BF16)
    qs = jnp.concatenate([jnp.where(lo_lanes if half == 0 else ~lo_lanes, qslab, zero)
                          for qslab in q_slabs for half in range(2)], axis=0)
    s_ref[...] = _dot_nt(qs, keys)


def _gqa_attend(vals, masks, sinks, rows, out_slab, first_slab, scratch):
    s_ref, p_ref, inv_ref = scratch
    lo_lanes = lax.broadcasted_iota(jnp.int32, (rows, LANES), 1) < HEAD_DIM
    for c in range(WIN_GROUP * rows // SOFTMAX_ROWS):
        rs = slice(c * SOFTMAX_ROWS, (c + 1) * SOFTMAX_ROWS)
        s = s_ref[rs, :]
        if masks is not None:
            mask, w = masks
            r0 = (c * SOFTMAX_ROWS) % rows
            s = jnp.concatenate([jnp.where(mask[r0:r0 + SOFTMAX_ROWS], s[:, :w], NEG), s[:, w:]], axis=1)
        sk = sinks[(c * SOFTMAX_ROWS) // rows]
        m = jnp.maximum(jnp.max(s, axis=-1, keepdims=True), sk)
        e = jnp.exp2(s - m)
        inv_ref[rs, :] = 1.0 / (jnp.sum(e, axis=-1, keepdims=True) + jnp.exp2(sk - m))
        p_ref[rs, :] = e.astype(BF16)
    o = _dot(p_ref[...], vals) * inv_ref[...]
    for i in range(WIN_GROUP // 2):
        a = o[(2 * i) * rows:(2 * i + 1) * rows]
        b = o[(2 * i + 1) * rows:(2 * i + 2) * rows]
        sl = slice((first_slab + i) * LANES, (first_slab + i + 1) * LANES)
        out_slab(sl)[...] = jnp.where(lo_lanes, a, b).astype(BF16)


def _gqa_scratch(rows, n_keys):
    one = [pltpu.VMEM((WIN_GROUP * rows, n_keys), F32), pltpu.VMEM((WIN_GROUP * rows, n_keys), BF16),
           pltpu.VMEM((WIN_GROUP * rows, 1), F32)]
    return one * WIN_KV_HEADS


def _group_sinks(sink_ref, kv):
    return [sink_ref[kv * WIN_GROUP + g] * LOG2E for g in range(WIN_GROUP)]


def _group_slab(kv, i):
    return slice((kv * (WIN_GROUP // 2) + i) * LANES, (kv * (WIN_GROUP // 2) + i + 1) * LANES)


def _window_bands():
    i = np.arange(WIN_BLOCK)[:, None]
    j = np.arange(3 * WIN_BLOCK)[None, :]
    bands = [np.abs(j - off - i) <= WIN_RADIUS for off in (0, WIN_BLOCK, 2 * WIN_BLOCK)]
    return jnp.asarray(np.stack(bands).astype(np.float32))


def _win_attn_kernel(blocks_per_step, sink_ref, band_ref, q_ref, k_ref, v_ref, kx_ref, vx_ref, o_ref,
                     s0_ref, s1_ref, p_ref, inv_ref):
    t = pl.program_id(1)
    nb = SEQ // WIN_BLOCK
    nw = 3 * WIN_BLOCK

    def place(j):
        n = t * blocks_per_step + j
        start = pl.multiple_of(jnp.clip((n - 1) * WIN_BLOCK, 0, SEQ - nw), WIN_BLOCK)
        sel = jnp.where(n == 0, 0, jnp.where(n == nb - 1, 2, 1))
        return pl.multiple_of(j * WIN_BLOCK, WIN_BLOCK), start, sel

    def scores(j, kv, s_ref):
        qoff, start, _ = place(j)
        keys = jnp.concatenate([k_ref[0, kv, pl.ds(start, nw), :], kx_ref[0, kv]], axis=0)
        q_slabs = [q_ref[0, pl.ds(qoff, WIN_BLOCK), _group_slab(kv, i)] for i in range(WIN_GROUP // 2)]
        _gqa_scores(q_slabs, keys, WIN_BLOCK, s_ref)

    def attend(j, kv, s_ref):
        qoff, start, sel = place(j)
        vals = jnp.concatenate([v_ref[0, kv, pl.ds(start, nw), :], vx_ref[0, kv]], axis=0)
        _gqa_attend(vals, (band_ref[sel] > 0.0, nw), _group_sinks(sink_ref, kv), WIN_BLOCK,
                    lambda sl: o_ref.at[0, pl.ds(qoff, WIN_BLOCK), sl], kv * (WIN_GROUP // 2),
                    (s_ref, p_ref, inv_ref))

    scores(0, 0, s0_ref)

    def one_block(j, carry):
        scores(j, 1, s1_ref)
        attend(j, 0, s0_ref)
        scores(jnp.minimum(j + 1, blocks_per_step - 1), 0, s0_ref)
        attend(j, 1, s1_ref)
        return carry

    lax.fori_loop(0, blocks_per_step, one_block, 0)


def _win_attention(q, k, v, kx, vx, sink):
    B, L, _ = q.shape
    blocks_per_step = 8
    rows = blocks_per_step * WIN_BLOCK
    n_keys = 3 * WIN_BLOCK + CTX_LEN
    full = pl.BlockSpec((1, WIN_KV_HEADS, L, LANES), lambda b, t: (b, 0, 0, 0))
    ctx_spec = pl.BlockSpec((1, WIN_KV_HEADS, CTX_LEN, LANES), lambda b, t: (0, 0, b, 0))
    tok = pl.BlockSpec((1, rows, QW), lambda b, t: (b, t, 0))
    return pl.pallas_call(
        functools.partial(_win_attn_kernel, blocks_per_step),
        grid=(B, L // rows),
        in_specs=[pl.BlockSpec(memory_space=pltpu.SMEM),
                  _const_spec((3, WIN_BLOCK, 3 * WIN_BLOCK)),
                  tok, full, full, ctx_spec, ctx_spec],
        out_specs=tok,
        out_shape=jax.ShapeDtypeStruct((B, L, QW), BF16),
        scratch_shapes=[pltpu.VMEM((WIN_GROUP * WIN_BLOCK, n_keys), F32), pltpu.VMEM((WIN_GROUP * WIN_BLOCK, n_keys), F32),
                        pltpu.VMEM((WIN_GROUP * WIN_BLOCK, n_keys), BF16), pltpu.VMEM((WIN_GROUP * WIN_BLOCK, 1), F32)],
        compiler_params=_params("arbitrary", "arbitrary"),
        name="window_attention",
    )(sink, _window_bands(), q, k, v, kx, vx)


def _ctx_attn_even_kernel(sink_ref, q_ref, k_ref, v_ref, o_ref, *scratch):
    for kv in range(WIN_KV_HEADS):
        q_slabs = [q_ref[0, :, _group_slab(kv, i)] for i in range(WIN_GROUP // 2)]
        _gqa_group(q_slabs, k_ref[0, kv], v_ref[0, kv], None, _group_sinks(sink_ref, kv), CTX_LEN, o_ref,
                   kv * (WIN_GROUP // 2), scratch[3 * kv:3 * kv + 3])


def _ctx_attention_even(q, k, v, sink):
    B = q.shape[0]
    kv_spec = pl.BlockSpec((1, WIN_KV_HEADS, CTX_LEN, LANES), lambda b: (0, 0, b, 0))
    return pl.pallas_call(
        _ctx_attn_even_kernel,
        grid=(B,),
        in_specs=[pl.BlockSpec(memory_space=pltpu.SMEM),
                  pl.BlockSpec((1, CTX_LEN, QW), lambda b: (b, 0, 0)),
                  kv_spec, kv_spec],
        out_specs=pl.BlockSpec((1, CTX_LEN, QW), lambda b: (b, 0, 0)),
        out_shape=jax.ShapeDtypeStruct((B, CTX_LEN, QW), BF16),
        scratch_shapes=_gqa_scratch(CTX_LEN, CTX_LEN),
        compiler_params=_params("arbitrary"),
        name="ctx_attention_even",
    )(sink, q, k, v)


def _out_ffn_kernel(mode, *refs):
    if mode == "even_latent":
        f_ref, a_ref, x_ref, mod_ref, g_ref, wo_ref, wg_ref, wu_ref, wd_ref, o_ref = refs
        fm = jnp.concatenate([f_ref[0, :, j, :] for j in range(f_ref.shape[2])], axis=0).astype(BF16)
        o = _dot(fm, wo_ref[:FOURIER_WIDTH]) + _dot(a_ref[0], wo_ref[FOURIER_WIDTH:])
    elif mode == "even_ctx":
        f_ref, a_ref, x_ref, mod_ref, g_ref, wo_ref, wg_ref, wu_ref, wd_ref, o_ref = refs
        o = _dot(f_ref[0].astype(BF16), wo_ref[:FOURIER_WIDTH]) + _dot(a_ref[0], wo_ref[FOURIER_WIDTH:])
    else:
        a_ref, x_ref, mod_ref, g_ref, wo_ref, wg_ref, wu_ref, wd_ref, o_ref = refs
        o = _dot(jnp.concatenate([a_ref[0, j] for j in range(a_ref.shape[1])], axis=1), wo_ref[...])
    x1 = x_ref[0] + _mod_slice(mod_ref, 2) * o
    h = _rms_mod(x1, g_ref[...], _mod_slice(mod_ref, 4), _mod_slice(mod_ref, 3)).astype(BF16)
    acc = jnp.zeros_like(x1)
    for c in range(D_FF // FF_CHUNK):
        cs = slice(c * FF_CHUNK, (c + 1) * FF_CHUNK)
        a = _silu(_dot(h, wg_ref[0, :, cs])) * _dot(h, wu_ref[0, :, cs])
        acc = acc + _dot(a.astype(BF16), wd_ref[0, cs, :])
    o_ref[0] = x1 + _mod_slice(mod_ref, 5) * acc


def _out_ffn(mode, mix, x, mod, mod_row, g, w_out, layer, wg, wu, wd, tm):
    B, L, _ = x.shape
    tok = lambda w: pl.BlockSpec((1, tm, w), lambda b, t: (b, t, 0))
    ffn_spec = lambda r, c: pl.BlockSpec((1, r, c), lambda b, t: (layer, 0, 0), pipeline_mode=pl.Buffered(1))
    if mode == "even_latent":
        assert tm % FFT_N == 0
        mix_specs = [pl.BlockSpec((1, FFT_N, tm // FFT_N, FOURIER_WIDTH), lambda b, t: (b, 0, t, 0)), tok(QW)]
    elif mode == "even_ctx":
        mix_specs = [tok(FOURIER_WIDTH), tok(QW)]
    else:
        mix_specs = [pl.BlockSpec((1, NA_WIDTH // LANES, tm, LANES), lambda b, t: (b, 0, t, 0))]
    return pl.pallas_call(
        functools.partial(_out_ffn_kernel, mode),
        grid=(B, L // tm),
        in_specs=mix_specs + [tok(D_MODEL),
                              pl.BlockSpec((1, 1, 6 * D_MODEL), lambda b, t: (mod_row(b), 0, 0)),
                              _const_spec((1, D_MODEL)),
                              _const_spec((D_MODEL, D_MODEL)),
                              ffn_spec(D_MODEL, D_FF), ffn_spec(D_MODEL, D_FF), ffn_spec(D_FF, D_MODEL)],
        out_specs=tok(D_MODEL),
        out_shape=jax.ShapeDtypeStruct((B, L, D_MODEL), F32),
        compiler_params=_params("arbitrary", "arbitrary"),
        name="out_ffn_" + mode,
    )(*mix, x, mod, g, w_out, wg, wu, wd)


def _in_odd_kernel(with_q, x_ref, mod_ref, g_ref, w_ref, qg_ref, kg_ref, hm_ref, *outs):
    W = NA_WIDTH
    slabs = RMS_W // LANES
    if with_q:
        q_ref, k_ref, v_ref = outs
    else:
        k_ref, v_ref = outs

    for r in range(x_ref.shape[1] // SUB_ROWS):
        rows = slice(r * SUB_ROWS, (r + 1) * SUB_ROWS)
        h = _rms_mod(x_ref[0, rows], g_ref[...], _mod_slice(mod_ref, 1), _mod_slice(mod_ref, 0)).astype(BF16)

        def normed(t, gain_ref, scale, o_ref):
            for j in range(W // RMS_W):
                tj = (_head_rms(t[:, j * RMS_W:(j + 1) * RMS_W], gain_ref, hm_ref) * scale).astype(BF16)
                for i in range(slabs):
                    o_ref[0, j * slabs + i, rows] = tj[:, i * LANES:(i + 1) * LANES]

        if with_q:
            normed(_dot(h, w_ref[:, :W]), qg_ref, QK_SCALE, q_ref)
        normed(_dot(h, w_ref[:, W:2 * W]), kg_ref, 1.0, k_ref)
        v = _dot(h, w_ref[:, 2 * W:]).astype(BF16)
        for j in range(W // LANES):
            v_ref[0, j, rows] = v[:, j * LANES:(j + 1) * LANES]


def _in_odd(x, mod, mod_row, g, w_in, q_g, k_g, hm, with_q, tm):
    B, L, _ = x.shape
    pairs = NA_WIDTH // LANES
    tok = pl.BlockSpec((1, pairs, tm, LANES), lambda b, t: (b, 0, t, 0))
    n_out = 3 if with_q else 2
    return pl.pallas_call(
        functools.partial(_in_odd_kernel, with_q),
        grid=(B, L // tm),
        in_specs=[pl.BlockSpec((1, tm, D_MODEL), lambda b, t: (b, t, 0)),
                  pl.BlockSpec((1, 1, 6 * D_MODEL), lambda b, t: (mod_row(b), 0, 0)),
                  _const_spec((1, D_MODEL)),
                  _const_spec((D_MODEL, 3 * NA_WIDTH)),
                  _const_spec((1, RMS_W)), _const_spec((1, RMS_W)), _const_spec((RMS_W, RMS_W))],
        out_specs=[tok] * n_out,
        out_shape=[jax.ShapeDtypeStruct((B, pairs, L, LANES), BF16)] * n_out,
        compiler_params=_params("arbitrary", "arbitrary"),
        name="in_odd" if with_q else "in_odd_ctx",
    )(x, mod, g, w_in, q_g, k_g, hm)


def _bias_table_kernel(rb_ref, oh_ref, o_ref):
    x = rb_ref[...]
    hi = x.astype(BF16)
    r1 = x - hi.astype(F32)
    mid = r1.astype(BF16)
    lo = (r1 - mid.astype(F32)).astype(BF16)
    oh = oh_ref[...]
    o_ref[...] = ((_dot(hi, oh) + _dot(mid, oh)) + _dot(lo, oh)) * LOG2E


def _na_bias_table(rel_bias, onehot):
    H = NA_HEADS
    ndr = 2 * NA_KH - 1
    rows = H * ndr
    rb = jnp.pad(rel_bias.reshape(rows, 2 * NA_KW - 1), ((0, 0), (0, LANES - (2 * NA_KW - 1))))
    flat = pl.pallas_call(
        _bias_table_kernel,
        out_shape=jax.ShapeDtypeStruct((rows, GRID_W * GRID_W), F32),
        compiler_params=pltpu.CompilerParams(vmem_limit_bytes=VMEM_LIMIT),
        name="na_bias_table",
    )(rb, onehot)
    t = flat.reshape(H, ndr, GRID_W, GRID_W)
    return jnp.concatenate([t[:, :-1], t[:, 1:]], axis=-1)


def _na_kernel(rows_per_step, q_ref, k_ref, v_ref, kx_ref, vx_ref, bias_ref, o_ref, s0_ref, s1_ref, p_ref, inv_ref):
    t = pl.program_id(1)
    n = NA_KH * GRID_W
    cq = lax.broadcasted_iota(jnp.int32, (GRID_W, n), 0)
    ck = lax.broadcasted_iota(jnp.int32, (GRID_W, n), 1) % GRID_W
    c0 = jnp.clip(cq - NA_KW // 2, 0, GRID_W - NA_KW)
    mask = (ck >= c0) & (ck < c0 + NA_KW)
    lo_lanes = lax.broadcasted_iota(jnp.int32, (GRID_W, LANES), 1) < HEAD_DIM

    def window(i):
        r = t * rows_per_step + i
        r0 = jnp.clip(r - NA_KH // 2, 0, GRID_ROWS - NA_KH)
        return r0 - r + NA_KH - 1, pl.multiple_of(r0 * GRID_W, GRID_W), pl.multiple_of(i * GRID_W, GRID_W)

    def scores(i, s_ref):
        dr0, start, qoff = window(i)
        for hp in range(NA_HEADS // 2):
            qs = q_ref[0, hp, pl.ds(qoff, GRID_W), :]
            keys = jnp.concatenate([k_ref[0, hp, pl.ds(start, n), :], kx_ref[0, hp]], axis=0)
            qm = jnp.concatenate([jnp.where(lo_lanes, qs, jnp.zeros_like(qs)),
                                  jnp.where(lo_lanes, jnp.zeros_like(qs), qs)], axis=0)
            s = _dot_nt(qm, keys)
            for half in range(2):
                h = 2 * hp + half
                bias = jnp.concatenate([bias_ref[h, dr0 + 2 * p] for p in range(NA_KH // 2)], axis=1)
                rs = slice(h * GRID_W, (h + 1) * GRID_W)
                hr = slice(half * GRID_W, (half + 1) * GRID_W)
                s_ref[rs, :n] = jnp.where(mask, s[hr, :n] + bias, NEG)
                s_ref[rs, n:] = s[hr, n:]

    def attend(i, s_ref):
        _, start, qoff = window(i)
        for c in range(NA_HEADS * GRID_W // SOFTMAX_ROWS):
            cs = slice(c * SOFTMAX_ROWS, (c + 1) * SOFTMAX_ROWS)
            s = s_ref[cs, :]
            e = jnp.exp2(s - jnp.max(s, axis=-1, keepdims=True))
            inv_ref[cs, :] = 1.0 / jnp.sum(e, axis=-1, keepdims=True)
            p_ref[cs, :] = e.astype(BF16)
        for hp in range(NA_HEADS // 2):
            vals = jnp.concatenate([v_ref[0, hp, pl.ds(start, n), :], vx_ref[0, hp]], axis=0)
            rs = slice(2 * hp * GRID_W, (2 * hp + 2) * GRID_W)
            res = _dot(p_ref[rs, :], vals) * inv_ref[rs, :]
            o_ref[0, hp, pl.ds(qoff, GRID_W), :] = jnp.where(lo_lanes, res[:GRID_W], res[GRID_W:]).astype(BF16)

    last = rows_per_step - 1
    scores(0, s0_ref)

    def two_rows(j, carry):
        scores(2 * j + 1, s1_ref)
        attend(2 * j, s0_ref)
        scores(jnp.minimum(2 * j + 2, last), s0_ref)
        attend(2 * j + 1, s1_ref)
        return carry

    lax.fori_loop(0, rows_per_step // 2, two_rows, 0)


def _neighbourhood_attention(q, k, v, kx, vx, bias_tab):
    B, pairs, L, _ = q.shape
    rows_per_step = 16
    full = pl.BlockSpec((1, pairs, L, LANES), lambda b, t: (b, 0, 0, 0))
    ctx_spec = pl.BlockSpec((1, pairs, CTX_LEN, LANES), lambda b, t: (0, 0, b, 0))
    q_rows = pl.BlockSpec((1, pairs, rows_per_step * GRID_W, LANES), lambda b, t: (b, 0, t, 0))
    n_keys = NA_KH * GRID_W + CTX_LEN
    all_rows = NA_HEADS * GRID_W
    return pl.pallas_call(
        functools.partial(_na_kernel, rows_per_step),
        grid=(B, GRID_ROWS // rows_per_step),
        in_specs=[q_rows, full, full, ctx_spec, ctx_spec, _const_spec(bias_tab.shape)],
        out_specs=q_rows,
        out_shape=jax.ShapeDtypeStruct((B, pairs, L, LANES), BF16),
        scratch_shapes=[pltpu.VMEM((all_rows, n_keys), F32), pltpu.VMEM((all_rows, n_keys), F32),
                        pltpu.VMEM((all_rows, n_keys), BF16), pltpu.VMEM((all_rows, 1), F32)],
        compiler_params=_params("arbitrary", "arbitrary"),
        name="neighbourhood_attention",
    )(q, k, v, kx, vx, bias_tab)


def kernel(x, c, ctx, c_ctx, ada_w, ada_b, norm1_g, norm2_g, ffn_w_gate, ffn_w_up, ffn_w_down,
           ev_w_in, ev_w_out, ev_q_norm, ev_k_norm, ev_sink,
           od_w_in, od_w_out, od_q_norm, od_k_norm, od_rel_bias):
    assert x.shape == (BATCH, SEQ, D_MODEL) and ctx.shape == (BATCH, CTX_LEN, D_MODEL)
    wc, m1, m3, twc, tws, mctx = _fourier_tables()
    hm = _head_mean_matrix()
    rope_tabs = _rope_tables()
    lane_gain = lambda gvec: jnp.tile(gvec, RMS_W // HEAD_DIM).reshape(1, RMS_W)
    lat_row = lambda b: b
    ctx_row = lambda b: CTX_MOD_ROW
    tm = 512
    n_ctx = BATCH * CTX_LEN
    as_seq = lambda a: a.reshape(1, n_ctx, a.shape[-1])
    per_batch = lambda a: a.reshape(BATCH, CTX_LEN, a.shape[-1])

    cs = jnp.concatenate([c, c_ctx[None, :], jnp.zeros((MOD_ROWS - BATCH - 1, D_MODEL), F32)], axis=0)
    mod = _modulation(cs, ada_w, ada_b).reshape(DEPTH, MOD_ROWS, 1, 6 * D_MODEL)

    w_in0 = ev_w_in[0].astype(BF16)
    w_out0 = ev_w_out[0].astype(BF16)
    g1 = norm1_g[0].reshape(1, D_MODEL)
    g2 = norm2_g[0].reshape(1, D_MODEL)
    ffn = (ffn_w_gate.astype(BF16), ffn_w_up.astype(BF16), ffn_w_down.astype(BF16))
    qg, kg = lane_gain(ev_q_norm[0]), lane_gain(ev_k_norm[0])
    y_l, q_l, k_l, v_l = _in_even(x, mod[0], lat_row, g1, w_in0, qg, kg, hm, wc, rope_tabs, 2 * tm)
    y_c, q_c, k_c, v_c = _in_even(as_seq(ctx), mod[0], ctx_row, g1, w_in0, qg, kg, hm, wc, None, tm)
    y_c, q_c = per_batch(y_c), per_batch(q_c)
    f_l = _fourier_latent(y_l, m1, m3, twc, tws)
    f_c = _fourier_ctx(y_c, mctx)
    a_l = _win_attention(q_l, k_l, v_l, k_c, v_c, ev_sink[0])
    a_c = _ctx_attention_even(q_c, k_c, v_c, ev_sink[0])
    x1 = _out_ffn("even_latent", (f_l, a_l), x, mod[0], lat_row, g2, w_out0, 0, *ffn, tm)
    y1 = _out_ffn("even_ctx", (as_seq(f_c), as_seq(a_c)), as_seq(ctx), mod[0], ctx_row, g2, w_out0, 0, *ffn, tm)

    w_in1 = od_w_in[0].astype(BF16)
    w_out1 = od_w_out[0].astype(BF16)
    g1 = norm1_g[1].reshape(1, D_MODEL)
    g2 = norm2_g[1].reshape(1, D_MODEL)
    qg, kg = lane_gain(od_q_norm[0]), lane_gain(od_k_norm[0])
    q_l, k_l, v_l = _in_odd(x1, mod[1], lat_row, g1, w_in1, qg, kg, hm, True, 2 * tm)
    k_c, v_c = _in_odd(y1, mod[1], ctx_row, g1, w_in1, qg, kg, hm, False, tm)
    bias_tab = _na_bias_table(od_rel_bias[0], _na_onehot())
    a_l = _neighbourhood_attention(q_l, k_l, v_l, k_c, v_c, bias_tab)
    return _out_ffn("odd", (a_l,), x1, mod[1], lat_row, g2, w_out1, 1, *ffn, tm)
```

```python
import functools
import math

import numpy as np
import jax
import jax.numpy as jnp
from jax import lax
from jax.experimental import pallas as pl
from jax.experimental.pallas import tpu as pltpu

D_MODEL = 1024
BATCH = 4
SEQ = 4096
DEPTH = 2
GRID_W = 64
CTX_LEN = 256
HEAD_DIM = 64
EPS = 1e-6
NEG = -1e30
FOURIER_WIDTH = D_MODEL // 2
FOURIER_GROUPS = 4
FOURIER_GROUP_CH = FOURIER_WIDTH // FOURIER_GROUPS
WIN_Q_HEADS = (D_MODEL // 2) // HEAD_DIM
WIN_KV_HEADS = 2
WIN_GROUP = WIN_Q_HEADS // WIN_KV_HEADS
WIN_RADIUS = 128
WIN_BLOCK = 128
QW = WIN_Q_HEADS * HEAD_DIM
KW = WIN_KV_HEADS * HEAD_DIM
EV_IN_WIDTH = FOURIER_WIDTH + QW + 2 * KW
NA_HEADS = D_MODEL // HEAD_DIM
NA_KH = 8
NA_KW = 16
NA_WIDTH = NA_HEADS * HEAD_DIM
ROPE_THETA = 10000.0
ROPE_FREQS = HEAD_DIM // 4
D_FF = ((8 * D_MODEL // 3 + 255) // 256) * 256
GRID_ROWS = SEQ // GRID_W

LANES = 128
MOD_ROWS = 8
CTX_MOD_ROW = BATCH
FFT_N = 64
FF_CHUNK = 256
SOFTMAX_ROWS = 64
RMS_W = 256
SUB_ROWS = 512
VMEM_LIMIT = 60 * 1024 * 1024

LOG2E = math.log2(math.e)
QK_SCALE = LOG2E / math.sqrt(HEAD_DIM)

BF16 = jnp.bfloat16
F32 = jnp.float32

assert DEPTH == 2 and SEQ == FFT_N * FFT_N and D_FF % FF_CHUNK == 0


def _params(*sem):
    return pltpu.CompilerParams(dimension_semantics=sem, vmem_limit_bytes=VMEM_LIMIT)


def _dot(a, b):
    return jnp.dot(a, b, preferred_element_type=F32)


def _dot_nt(a, b):
    return lax.dot_general(a, b, (((1,), (1,)), ((), ())), preferred_element_type=F32)


def _silu(x):
    return x / (1.0 + jnp.exp(-x))


def _const_spec(shape):
    nd = len(shape)
    return pl.BlockSpec(shape, lambda *_: (0,) * nd, pipeline_mode=pl.Buffered(1))


def _dft_cos_sin(n):
    idx = (np.arange(n)[:, None] * np.arange(n)[None, :]) % n
    ang = 2.0 * np.pi * idx / n
    return np.cos(ang), np.sin(ang)


def _fourier_tables():
    cc, sc = _dft_cos_sin(FOURIER_GROUP_CH)
    wc = np.concatenate([cc, -sc], axis=1) / math.sqrt(FOURIER_GROUP_CH)
    c64, s64 = _dft_cos_sin(FFT_N)
    m1 = np.concatenate([c64, -s64], axis=0) / math.sqrt(FFT_N)
    m3 = np.concatenate([c64, s64], axis=1) / math.sqrt(FFT_N)
    tw = (np.arange(FFT_N)[:, None] * np.arange(FFT_N)[None, :]) % SEQ
    tw = 2.0 * np.pi * tw / SEQ
    twc = np.repeat(np.cos(tw)[:, :, None], LANES, axis=2)
    tws = np.repeat(np.sin(tw)[:, :, None], LANES, axis=2)
    cx, sx = _dft_cos_sin(CTX_LEN)
    mctx = np.concatenate([cx, sx], axis=1) / math.sqrt(CTX_LEN)
    as32 = lambda a: jnp.asarray(a, F32)
    return (as32(wc).astype(BF16), as32(m1).astype(BF16), as32(m3).astype(BF16),
            as32(twc), as32(tws), as32(mctx).astype(BF16))


def _head_mean_matrix():
    blk = np.kron(np.eye(RMS_W // HEAD_DIM), np.ones((HEAD_DIM, HEAD_DIM))) / HEAD_DIM
    return jnp.asarray(blk, BF16)


def _rope_tables():
    t = jnp.arange(SEQ, dtype=jnp.int32)
    row = (t // GRID_W).astype(F32)
    col = (t % GRID_W).astype(F32)
    inv = ROPE_THETA ** (-jnp.arange(ROPE_FREQS, dtype=F32) / ROPE_FREQS)
    ang_row = row[:, None] * inv[None, :]
    ang_col = col[:, None] * inv[None, :]
    zero = jnp.zeros_like(ang_row)
    cos = jnp.concatenate([jnp.cos(ang_row)] * 2 + [jnp.cos(ang_col)] * 2, axis=1)
    sin_hi = jnp.concatenate([-jnp.sin(ang_row), zero, -jnp.sin(ang_col), zero], axis=1)
    sin_lo = jnp.concatenate([zero, jnp.sin(ang_row), zero, jnp.sin(ang_col)], axis=1)
    rep = LANES // HEAD_DIM
    return jnp.tile(cos, (1, rep)), jnp.tile(sin_hi, (1, rep)), jnp.tile(sin_lo, (1, rep))


def _na_onehot():
    cq = np.arange(GRID_W)
    dc = np.clip(cq[None, :] - cq[:, None] + NA_KW - 1, 0, 2 * NA_KW - 2)
    oh = np.zeros((LANES, GRID_W * GRID_W), np.float32)
    oh[dc.reshape(-1), np.arange(GRID_W * GRID_W)] = 1.0
    return jnp.asarray(oh, BF16)


def _mod_kernel(cs_ref, w_ref, b_ref, o_ref):
    s = _silu(cs_ref[...]).astype(BF16)
    o_ref[0] = _dot(s, w_ref[0].astype(BF16)) + b_ref[0]


def _modulation(cs, ada_w, ada_b):
    tn = 1536
    return pl.pallas_call(
        _mod_kernel,
        grid=(DEPTH, 6 * D_MODEL // tn),
        in_specs=[pl.BlockSpec((MOD_ROWS, D_MODEL), lambda i, j: (0, 0)),
                  pl.BlockSpec((1, D_MODEL, tn), lambda i, j: (i, 0, j)),
                  pl.BlockSpec((1, 1, tn), lambda i, j: (i, 0, j))],
        out_specs=pl.BlockSpec((1, MOD_ROWS, tn), lambda i, j: (i, 0, j)),
        out_shape=jax.ShapeDtypeStruct((DEPTH, MOD_ROWS, 6 * D_MODEL), F32),
        compiler_params=_params("arbitrary", "arbitrary"),
        name="ada_modulation",
    )(cs, ada_w, ada_b.reshape(DEPTH, 1, 6 * D_MODEL))


def _mod_slice(mod_ref, k):
    return mod_ref[0, :, k * D_MODEL:(k + 1) * D_MODEL]


def _rms_mod(x, g, scale, shift):
    y = x * lax.rsqrt(jnp.mean(x * x, axis=-1, keepdims=True) + EPS)
    return (y * g) * (1.0 + scale) + shift


def _head_rms(t, gain_ref, hm_ref):
    w = t.shape[1]
    ms = _dot((t * t).astype(BF16), hm_ref[:w, :w])
    return t * lax.rsqrt(ms + EPS) * gain_ref[:, :w]


def _rope(t, cos, sin_hi, sin_lo):
    up = pltpu.roll(t, LANES - ROPE_FREQS, axis=1)
    dn = pltpu.roll(t, ROPE_FREQS, axis=1)
    return t * cos + up * sin_hi + dn * sin_lo


def _in_even_kernel(latent, x_ref, mod_ref, g_ref, w_ref, qg_ref, kg_ref, hm_ref, wc_ref, *rest):
    if latent:
        cos_ref, sh_ref, sl_ref, y_ref, q_ref, k_ref, v_ref = rest
    else:
        y_ref, q_ref, k_ref, v_ref = rest
    F = FOURIER_WIDTH
    lo_lanes = lax.broadcasted_iota(jnp.int32, (1, LANES), 1) < HEAD_DIM

    for r in range(x_ref.shape[1] // SUB_ROWS):
        rows = slice(r * SUB_ROWS, (r + 1) * SUB_ROWS)
        h = _rms_mod(x_ref[0, rows], g_ref[...], _mod_slice(mod_ref, 1), _mod_slice(mod_ref, 0)).astype(BF16)
        f = _dot(h, w_ref[:, :F]).astype(BF16)
        for g in range(FOURIER_GROUPS):
            yg = _dot(f[:, g * LANES:(g + 1) * LANES], wc_ref[...])
            y_ref[0, rows, g * LANES:(g + 1) * LANES] = yg[:, :LANES].astype(BF16)
            y_ref[0, rows, F + g * LANES:F + (g + 1) * LANES] = yg[:, LANES:].astype(BF16)

        def qk(t, gain_ref):
            t = _head_rms(t, gain_ref, hm_ref)
            if latent:
                t = jnp.concatenate([_rope(t[:, i * LANES:(i + 1) * LANES], cos_ref[rows], sh_ref[rows], sl_ref[rows])
                                     for i in range(t.shape[1] // LANES)], axis=1)
            return t

        q = _dot(h, w_ref[:, F:F + QW])
        for j in range(QW // RMS_W):
            qj = qk(q[:, j * RMS_W:(j + 1) * RMS_W], qg_ref) * QK_SCALE
            q_ref[0, rows, j * RMS_W:(j + 1) * RMS_W] = qj.astype(BF16)
        kv = _dot(h, w_ref[:, F + QW:])

        def store_dup(t, o_ref):
            sw = pltpu.roll(t, HEAD_DIM, axis=1)
            o_ref[0, 0, rows] = jnp.where(lo_lanes, t, sw).astype(BF16)
            o_ref[0, 1, rows] = jnp.where(lo_lanes, sw, t).astype(BF16)

        store_dup(qk(kv[:, :KW], kg_ref), k_ref)
        store_dup(kv[:, KW:], v_ref)


def _in_even(x, mod, mod_row, g, w_in, q_g, k_g, hm, wc, rope_tabs, tm):
    B, L, _ = x.shape
    latent = rope_tabs is not None
    in_specs = [pl.BlockSpec((1, tm, D_MODEL), lambda b, t: (b, t, 0)),
                pl.BlockSpec((1, 1, 6 * D_MODEL), lambda b, t: (mod_row(b), 0, 0)),
                _const_spec((1, D_MODEL)),
                _const_spec((D_MODEL, EV_IN_WIDTH)),
                _const_spec((1, RMS_W)), _const_spec((1, RMS_W)),
                _const_spec((RMS_W, RMS_W)), _const_spec((LANES, 2 * LANES))]
    args = [x, mod, g, w_in, q_g, k_g, hm, wc]
    tok = lambda w: pl.BlockSpec((1, tm, w), lambda b, t: (b, t, 0))
    kv_spec = pl.BlockSpec((1, WIN_KV_HEADS, tm, LANES), lambda b, t: (b, 0, t, 0))
    if latent:
        in_specs += [pl.BlockSpec((tm, LANES), lambda b, t: (t, 0))] * 3
        args += list(rope_tabs)
    return pl.pallas_call(
        functools.partial(_in_even_kernel, latent),
        grid=(B, L // tm),
        in_specs=in_specs,
        out_specs=[tok(2 * FOURIER_WIDTH), tok(QW), kv_spec, kv_spec],
        out_shape=[jax.ShapeDtypeStruct((B, L, 2 * FOURIER_WIDTH), BF16),
                   jax.ShapeDtypeStruct((B, L, QW), BF16),
                   jax.ShapeDtypeStruct((B, WIN_KV_HEADS, L, LANES), BF16),
                   jax.ShapeDtypeStruct((B, WIN_KV_HEADS, L, LANES), BF16)],
        compiler_params=_params("arbitrary", "arbitrary"),
        name="in_even_latent" if latent else "in_even_ctx",
    )(*args)


def _fft_stage1_kernel(n2t, y_ref, m1_ref, twc_ref, tws_ref, o_ref):
    F = FOURIER_WIDTH
    for j in range(n2t):
        a = _dot(m1_ref[...], y_ref[0, :, j * 2 * F:(j + 1) * 2 * F])
        top, bot = a[:FFT_N], a[FFT_N:]
        ar = top[:, :F] - bot[:, F:]
        ai = top[:, F:] + bot[:, :F]
        tc = jnp.tile(twc_ref[j], (1, F // LANES))
        ts = jnp.tile(tws_ref[j], (1, F // LANES))
        o_ref[0, 0, :, j * F:(j + 1) * F] = (ar * tc + ai * ts).astype(BF16)
        o_ref[0, 1, :, j * F:(j + 1) * F] = (ai * tc - ar * ts).astype(BF16)


def _fft_stage2_kernel(k1t, b_ref, m3_ref, o_ref):
    for j in range(k1t):
        rhs = jnp.concatenate([b_ref[0, 0, j], b_ref[0, 1, j]], axis=0)
        o_ref[0, j] = _dot(m3_ref[...], rhs)


def _fourier_latent(y, m1, m3, twc, tws):
    B = y.shape[0]
    F = FOURIER_WIDTH
    n2t = 16
    yv = y.reshape(B, FFT_N, FFT_N * 2 * F)
    bh = pl.pallas_call(
        functools.partial(_fft_stage1_kernel, n2t),
        grid=(B, FFT_N // n2t),
        in_specs=[pl.BlockSpec((1, FFT_N, n2t * 2 * F), lambda b, t: (b, 0, t)),
                  _const_spec((2 * FFT_N, FFT_N)),
                  pl.BlockSpec((n2t, FFT_N, LANES), lambda b, t: (t, 0, 0)),
                  pl.BlockSpec((n2t, FFT_N, LANES), lambda b, t: (t, 0, 0))],
        out_specs=pl.BlockSpec((1, 2, FFT_N, n2t * F), lambda b, t: (b, 0, 0, t)),
        out_shape=jax.ShapeDtypeStruct((B, 2, FFT_N, FFT_N * F), BF16),
        compiler_params=_params("arbitrary", "arbitrary"),
        name="fft_stage1",
    )(yv, m1, twc, tws)
    k1t = 16
    bv = bh.reshape(B, 2, FFT_N, FFT_N, F)
    return pl.pallas_call(
        functools.partial(_fft_stage2_kernel, k1t),
        grid=(B, FFT_N // k1t),
        in_specs=[pl.BlockSpec((1, 2, k1t, FFT_N, F), lambda b, t: (b, 0, t, 0, 0)),
                  _const_spec((FFT_N, 2 * FFT_N))],
        out_specs=pl.BlockSpec((1, k1t, FFT_N, F), lambda b, t: (b, t, 0, 0)),
        out_shape=jax.ShapeDtypeStruct((B, FFT_N, FFT_N, F), F32),
        compiler_params=_params("arbitrary", "arbitrary"),
        name="fft_stage2",
    )(bv, m3)


def _fourier_ctx_kernel(y_ref, m_ref, o_ref):
    F = FOURIER_WIDTH
    rhs = jnp.concatenate([y_ref[0, :, :F], y_ref[0, :, F:]], axis=0)
    o_ref[0] = _dot(m_ref[...], rhs)


def _fourier_ctx(y, mctx):
    B = y.shape[0]
    return pl.pallas_call(
        _fourier_ctx_kernel,
        grid=(B,),
        in_specs=[pl.BlockSpec((1, CTX_LEN, 2 * FOURIER_WIDTH), lambda b: (b, 0, 0)),
                  _const_spec((CTX_LEN, 2 * CTX_LEN))],
        out_specs=pl.BlockSpec((1, CTX_LEN, FOURIER_WIDTH), lambda b: (b, 0, 0)),
        out_shape=jax.ShapeDtypeStruct((B, CTX_LEN, FOURIER_WIDTH), F32),
        compiler_params=_params("arbitrary"),
        name="fourier_ctx",
    )(y, mctx)


def _gqa_group(q_slabs, keys, vals, masks, sinks, rows, o_ref, first_slab, scratch):
    s_ref, p_ref, inv_ref = scratch
    _gqa_scores(q_slabs, keys, rows, s_ref)
    _gqa_attend(vals, masks, sinks, rows, lambda sl: o_ref.at[0, :, sl], first_slab, scratch)


def _gqa_scores(q_slabs, keys, rows, s_ref):
    lo_lanes = lax.broadcasted_iota(jnp.int32, (rows, LANES), 1) < HEAD_DIM
    zero = jnp.zeros((rows, LANES), BF16)
    qs = jnp.concatenate([jnp.where(lo_lanes if half == 0 else ~lo_lanes, qslab, zero)
                          for qslab in q_slabs for half in range(2)], axis=0)
    s_ref[...] = _dot_nt(qs, keys)


def _gqa_attend(vals, masks, sinks, rows, out_slab, first_slab, scratch):
    s_ref, p_ref, inv_ref = scratch
    lo_lanes = lax.broadcasted_iota(jnp.int32, (rows, LANES), 1) < HEAD_DIM
    for c in range(WIN_GROUP * rows // SOFTMAX_ROWS):
        rs = slice(c * SOFTMAX_ROWS, (c + 1) * SOFTMAX_ROWS)
        s = s_ref[rs, :]
        if masks is not None:
            mask, w = masks
            r0 = (c * SOFTMAX_ROWS) % rows
            s = jnp.concatenate([jnp.where(mask[r0:r0 + SOFTMAX_ROWS], s[:, :w], NEG), s[:, w:]], axis=1)
        sk = sinks[(c * SOFTMAX_ROWS) // rows]
        m = jnp.maximum(jnp.max(s, axis=-1, keepdims=True), sk)
        e = jnp.exp2(s - m)
        inv_ref[rs, :] = 1.0 / (jnp.sum(e, axis=-1, keepdims=True) + jnp.exp2(sk - m))
        p_ref[rs, :] = e.astype(BF16)
    o = _dot(p_ref[...], vals) * inv_ref[...]
    for i in range(WIN_GROUP // 2):
        a = o[(2 * i) * rows:(2 * i + 1) * rows]
        b = o[(2 * i + 1) * rows:(2 * i + 2) * rows]
        sl = slice((first_slab + i) * LANES, (first_slab + i + 1) * LANES)
        out_slab(sl)[...] = jnp.where(lo_lanes, a, b).astype(BF16)


def _gqa_scratch(rows, n_keys):
    one = [pltpu.VMEM((WIN_GROUP * rows, n_keys), F32), pltpu.VMEM((WIN_GROUP * rows, n_keys), BF16),
           pltpu.VMEM((WIN_GROUP * rows, 1), F32)]
    return one * WIN_KV_HEADS


def _group_sinks(sink_ref, kv):
    return [sink_ref[kv * WIN_GROUP + g] * LOG2E for g in range(WIN_GROUP)]


def _group_slab(kv, i):
    return slice((kv * (WIN_GROUP // 2) + i) * LANES, (kv * (WIN_GROUP // 2) + i + 1) * LANES)


def _window_bands():
    i = np.arange(WIN_BLOCK)[:, None]
    j = np.arange(3 * WIN_BLOCK)[None, :]
    bands = [np.abs(j - off - i) <= WIN_RADIUS for off in (0, WIN_BLOCK, 2 * WIN_BLOCK)]
    return jnp.asarray(np.stack(bands).astype(np.float32))


def _win_attn_kernel(blocks_per_step, sink_ref, band_ref, q_ref, k_ref, v_ref, kx_ref, vx_ref, o_ref,
                     s0_ref, s1_ref, p_ref, inv_ref):
    t = pl.program_id(1)
    nb = SEQ // WIN_BLOCK
    nw = 3 * WIN_BLOCK

    def place(j):
        n = t * blocks_per_step + j
        start = pl.multiple_of(jnp.clip((n - 1) * WIN_BLOCK, 0, SEQ - nw), WIN_BLOCK)
        sel = jnp.where(n == 0, 0, jnp.where(n == nb - 1, 2, 1))
        return pl.multiple_of(j * WIN_BLOCK, WIN_BLOCK), start, sel

    def scores(j, kv, s_ref):
        qoff, start, _ = place(j)
        keys = jnp.concatenate([k_ref[0, kv, pl.ds(start, nw), :], kx_ref[0, kv]], axis=0)
        q_slabs = [q_ref[0, pl.ds(qoff, WIN_BLOCK), _group_slab(kv, i)] for i in range(WIN_GROUP // 2)]
        _gqa_scores(q_slabs, keys, WIN_BLOCK, s_ref)

    def attend(j, kv, s_ref):
        qoff, start, sel = place(j)
        vals = jnp.concatenate([v_ref[0, kv, pl.ds(start, nw), :], vx_ref[0, kv]], axis=0)
        _gqa_attend(vals, (band_ref[sel] > 0.0, nw), _group_sinks(sink_ref, kv), WIN_BLOCK,
                    lambda sl: o_ref.at[0, pl.ds(qoff, WIN_BLOCK), sl], kv * (WIN_GROUP // 2),
                    (s_ref, p_ref, inv_ref))

    scores(0, 0, s0_ref)

    def one_block(j, carry):
        scores(j, 1, s1_ref)
        attend(j, 0, s0_ref)
        scores(jnp.minimum(j + 1, blocks_per_step - 1), 0, s0_ref)
        attend(j, 1, s1_ref)
        return carry

    lax.fori_loop(0, blocks_per_step, one_block, 0)


def _win_attention(q, k, v, kx, vx, sink):
    B, L, _ = q.shape
    blocks_per_step = 8
    rows = blocks_per_step * WIN_BLOCK
    n_keys = 3 * WIN_BLOCK + CTX_LEN
    full = pl.BlockSpec((1, WIN_KV_HEADS, L, LANES), lambda b, t: (b, 0, 0, 0))
    ctx_spec = pl.BlockSpec((1, WIN_KV_HEADS, CTX_LEN, LANES), lambda b, t: (0, 0, b, 0))
    tok = pl.BlockSpec((1, rows, QW), lambda b, t: (b, t, 0))
    return pl.pallas_call(
        functools.partial(_win_attn_kernel, blocks_per_step),
        grid=(B, L // rows),
        in_specs=[pl.BlockSpec(memory_space=pltpu.SMEM),
                  _const_spec((3, WIN_BLOCK, 3 * WIN_BLOCK)),
                  tok, full, full, ctx_spec, ctx_spec],
        out_specs=tok,
        out_shape=jax.ShapeDtypeStruct((B, L, QW), BF16),
        scratch_shapes=[pltpu.VMEM((WIN_GROUP * WIN_BLOCK, n_keys), F32), pltpu.VMEM((WIN_GROUP * WIN_BLOCK, n_keys), F32),
                        pltpu.VMEM((WIN_GROUP * WIN_BLOCK, n_keys), BF16), pltpu.VMEM((WIN_GROUP * WIN_BLOCK, 1), F32)],
        compiler_params=_params("arbitrary", "arbitrary"),
        name="window_attention",
    )(sink, _window_bands(), q, k, v, kx, vx)


def _ctx_attn_even_kernel(sink_ref, q_ref, k_ref, v_ref, o_ref, *scratch):
    for kv in range(WIN_KV_HEADS):
        q_slabs = [q_ref[0, :, _group_slab(kv, i)] for i in range(WIN_GROUP // 2)]
        _gqa_group(q_slabs, k_ref[0, kv], v_ref[0, kv], None, _group_sinks(sink_ref, kv), CTX_LEN, o_ref,
                   kv * (WIN_GROUP // 2), scratch[3 * kv:3 * kv + 3])


def _ctx_attention_even(q, k, v, sink):
    B = q.shape[0]
    kv_spec = pl.BlockSpec((1, WIN_KV_HEADS, CTX_LEN, LANES), lambda b: (0, 0, b, 0))
    return pl.pallas_call(
        _ctx_attn_even_kernel,
        grid=(B,),
        in_specs=[pl.BlockSpec(memory_space=pltpu.SMEM),
                  pl.BlockSpec((1, CTX_LEN, QW), lambda b: (b, 0, 0)),
                  kv_spec, kv_spec],
        out_specs=pl.BlockSpec((1, CTX_LEN, QW), lambda b: (b, 0, 0)),
        out_shape=jax.ShapeDtypeStruct((B, CTX_LEN, QW), BF16),
        scratch_shapes=_gqa_scratch(CTX_LEN, CTX_LEN),
        compiler_params=_params("arbitrary"),
        name="ctx_attention_even",
    )(sink, q, k, v)


def _out_ffn_kernel(mode, *refs):
    if mode == "odd":
        a_ref, x_ref, mod_ref, g_ref, wo_ref, wg_ref, wu_ref, wd_ref, o_ref = refs
    else:
        f_ref, a_ref, x_ref, mod_ref, g_ref, wo_ref, wg_ref, wu_ref, wd_ref, o_ref = refs
    for r in range(x_ref.shape[1] // SUB_ROWS):
        rows = slice(r * SUB_ROWS, (r + 1) * SUB_ROWS)
        if mode == "even_latent":
            planes = range(r * SUB_ROWS // FFT_N, (r + 1) * SUB_ROWS // FFT_N)
            fm = jnp.concatenate([f_ref[0, :, j, :] for j in planes], axis=0).astype(BF16)
            o = _dot(fm, wo_ref[:FOURIER_WIDTH]) + _dot(a_ref[0, rows], wo_ref[FOURIER_WIDTH:])
        elif mode == "even_ctx":
            o = _dot(f_ref[0, rows].astype(BF16), wo_ref[:FOURIER_WIDTH]) + _dot(a_ref[0, rows], wo_ref[FOURIER_WIDTH:])
        else:
            o = _dot(jnp.concatenate([a_ref[0, j, rows] for j in range(a_ref.shape[1])], axis=1), wo_ref[...])
        x1 = x_ref[0, rows] + _mod_slice(mod_ref, 2) * o
        h = _rms_mod(x1, g_ref[...], _mod_slice(mod_ref, 4), _mod_slice(mod_ref, 3)).astype(BF16)
        acc = jnp.zeros_like(x1)
        for c in range(D_FF // FF_CHUNK):
            cs = slice(c * FF_CHUNK, (c + 1) * FF_CHUNK)
            a = _silu(_dot(h, wg_ref[0, :, cs])) * _dot(h, wu_ref[0, :, cs])
            acc = acc + _dot(a.astype(BF16), wd_ref[0, cs, :])
        o_ref[0, rows] = x1 + _mod_slice(mod_ref, 5) * acc


def _out_ffn(mode, mix, x, mod, mod_row, g, w_out, layer, wg, wu, wd, tm):
    B, L, _ = x.shape
    tok = lambda w: pl.BlockSpec((1, tm, w), lambda b, t: (b, t, 0))
    ffn_spec = lambda r, c: pl.BlockSpec((1, r, c), lambda b, t: (layer, 0, 0), pipeline_mode=pl.Buffered(1))
    if mode == "even_latent":
        assert tm % FFT_N == 0
        mix_specs = [pl.BlockSpec((1, FFT_N, tm // FFT_N, FOURIER_WIDTH), lambda b, t: (b, 0, t, 0)), tok(QW)]
    elif mode == "even_ctx":
        mix_specs = [tok(FOURIER_WIDTH), tok(QW)]
    else:
        mix_specs = [pl.BlockSpec((1, NA_WIDTH // LANES, tm, LANES), lambda b, t: (b, 0, t, 0))]
    return pl.pallas_call(
        functools.partial(_out_ffn_kernel, mode),
        grid=(B, L // tm),
        in_specs=mix_specs + [tok(D_MODEL),
                              pl.BlockSpec((1, 1, 6 * D_MODEL), lambda b, t: (mod_row(b), 0, 0)),
                              _const_spec((1, D_MODEL)),
                              _const_spec((D_MODEL, D_MODEL)),
                              ffn_spec(D_MODEL, D_FF), ffn_spec(D_MODEL, D_FF), ffn_spec(D_FF, D_MODEL)],
        out_specs=tok(D_MODEL),
        out_shape=jax.ShapeDtypeStruct((B, L, D_MODEL), F32),
        compiler_params=_params("arbitrary", "arbitrary"),
        name="out_ffn_" + mode,
    )(*mix, x, mod, g, w_out, wg, wu, wd)


def _in_odd_kernel(with_q, x_ref, mod_ref, g_ref, w_ref, qg_ref, kg_ref, hm_ref, *outs):
    W = NA_WIDTH
    slabs = RMS_W // LANES
    if with_q:
        q_ref, k_ref, v_ref = outs
    else:
        k_ref, v_ref = outs

    for r in range(x_ref.shape[1] // SUB_ROWS):
        rows = slice(r * SUB_ROWS, (r + 1) * SUB_ROWS)
        h = _rms_mod(x_ref[0, rows], g_ref[...], _mod_slice(mod_ref, 1), _mod_slice(mod_ref, 0)).astype(BF16)

        def normed(t, gain_ref, scale, o_ref):
            for j in range(W // RMS_W):
                tj = (_head_rms(t[:, j * RMS_W:(j + 1) * RMS_W], gain_ref, hm_ref) * scale).astype(BF16)
                for i in range(slabs):
                    o_ref[0, j * slabs + i, rows] = tj[:, i * LANES:(i + 1) * LANES]

        if with_q:
            normed(_dot(h, w_ref[:, :W]), qg_ref, QK_SCALE, q_ref)
        normed(_dot(h, w_ref[:, W:2 * W]), kg_ref, 1.0, k_ref)
        v = _dot(h, w_ref[:, 2 * W:]).astype(BF16)
        for j in range(W // LANES):
            v_ref[0, j, rows] = v[:, j * LANES:(j + 1) * LANES]


def _in_odd(x, mod, mod_row, g, w_in, q_g, k_g, hm, with_q, tm):
    B, L, _ = x.shape
    pairs = NA_WIDTH // LANES
    tok = pl.BlockSpec((1, pairs, tm, LANES), lambda b, t: (b, 0, t, 0))
    n_out = 3 if with_q else 2
    return pl.pallas_call(
        functools.partial(_in_odd_kernel, with_q),
        grid=(B, L // tm),
        in_specs=[pl.BlockSpec((1, tm, D_MODEL), lambda b, t: (b, t, 0)),
                  pl.BlockSpec((1, 1, 6 * D_MODEL), lambda b, t: (mod_row(b), 0, 0)),
                  _const_spec((1, D_MODEL)),
                  _const_spec((D_MODEL, 3 * NA_WIDTH)),
                  _const_spec((1, RMS_W)), _const_spec((1, RMS_W)), _const_spec((RMS_W, RMS_W))],
        out_specs=[tok] * n_out,
        out_shape=[jax.ShapeDtypeStruct((B, pairs, L, LANES), BF16)] * n_out,
        compiler_params=_params("arbitrary", "arbitrary"),
        name="in_odd" if with_q else "in_odd_ctx",
    )(x, mod, g, w_in, q_g, k_g, hm)


def _bias_table_kernel(rb_ref, oh_ref, o_ref):
    x = rb_ref[...]
    hi = x.astype(BF16)
    r1 = x - hi.astype(F32)
    mid = r1.astype(BF16)
    lo = (r1 - mid.astype(F32)).astype(BF16)
    oh = oh_ref[...]
    o_ref[...] = ((_dot(hi, oh) + _dot(mid, oh)) + _dot(lo, oh)) * LOG2E


def _na_bias_table(rel_bias, onehot):
    H = NA_HEADS
    ndr = 2 * NA_KH - 1
    rows = H * ndr
    rb = jnp.pad(rel_bias.reshape(rows, 2 * NA_KW - 1), ((0, 0), (0, LANES - (2 * NA_KW - 1))))
    flat = pl.pallas_call(
        _bias_table_kernel,
        out_shape=jax.ShapeDtypeStruct((rows, GRID_W * GRID_W), F32),
        compiler_params=pltpu.CompilerParams(vmem_limit_bytes=VMEM_LIMIT),
        name="na_bias_table",
    )(rb, onehot)
    t = flat.reshape(H, ndr, GRID_W, GRID_W)
    return jnp.concatenate([t[:, :-1], t[:, 1:]], axis=-1)


def _na_kernel(rows_per_step, q_ref, k_ref, v_ref, kx_ref, vx_ref, bias_ref, o_ref, s0_ref, s1_ref, p_ref, inv_ref):
    t = pl.program_id(1)
    n = NA_KH * GRID_W
    cq = lax.broadcasted_iota(jnp.int32, (GRID_W, n), 0)
    ck = lax.broadcasted_iota(jnp.int32, (GRID_W, n), 1) % GRID_W
    c0 = jnp.clip(cq - NA_KW // 2, 0, GRID_W - NA_KW)
    mask = (ck >= c0) & (ck < c0 + NA_KW)
    lo_lanes = lax.broadcasted_iota(jnp.int32, (GRID_W, LANES), 1) < HEAD_DIM

    def window(i):
        r = t * rows_per_step + i
        r0 = jnp.clip(r - NA_KH // 2, 0, GRID_ROWS - NA_KH)
        return r0 - r + NA_KH - 1, pl.multiple_of(r0 * GRID_W, GRID_W), pl.multiple_of(i * GRID_W, GRID_W)

    def scores(i, s_ref):
        dr0, start, qoff = window(i)
        for hp in range(NA_HEADS // 2):
            qs = q_ref[0, hp, pl.ds(qoff, GRID_W), :]
            keys = jnp.concatenate([k_ref[0, hp, pl.ds(start, n), :], kx_ref[0, hp]], axis=0)
            qm = jnp.concatenate([jnp.where(lo_lanes, qs, jnp.zeros_like(qs)),
                                  jnp.where(lo_lanes, jnp.zeros_like(qs), qs)], axis=0)
            s = _dot_nt(qm, keys)
            for half in range(2):
                h = 2 * hp + half
                bias = jnp.concatenate([bias_ref[h, dr0 + 2 * p] for p in range(NA_KH // 2)], axis=1)
                rs = slice(h * GRID_W, (h + 1) * GRID_W)
                hr = slice(half * GRID_W, (half + 1) * GRID_W)
                s_ref[rs, :n] = jnp.where(mask, s[hr, :n] + bias, NEG)
                s_ref[rs, n:] = s[hr, n:]

    def attend(i, s_ref):
        _, start, qoff = window(i)
        for c in range(NA_HEADS * GRID_W // SOFTMAX_ROWS):
            cs = slice(c * SOFTMAX_ROWS, (c + 1) * SOFTMAX_ROWS)
            s = s_ref[cs, :]
            e = jnp.exp2(s - jnp.max(s, axis=-1, keepdims=True))
            inv_ref[cs, :] = 1.0 / jnp.sum(e, axis=-1, keepdims=True)
            p_ref[cs, :] = e.astype(BF16)
        for hp in range(NA_HEADS // 2):
            vals = jnp.concatenate([v_ref[0, hp, pl.ds(start, n), :], vx_ref[0, hp]], axis=0)
            rs = slice(2 * hp * GRID_W, (2 * hp + 2) * GRID_W)
            res = _dot(p_ref[rs, :], vals) * inv_ref[rs, :]
            o_ref[0, hp, pl.ds(qoff, GRID_W), :] = jnp.where(lo_lanes, res[:GRID_W], res[GRID_W:]).astype(BF16)

    last = rows_per_step - 1
    scores(0, s0_ref)

    def two_rows(j, carry):
        scores(2 * j + 1, s1_ref)
        attend(2 * j, s0_ref)
        scores(jnp.minimum(2 * j + 2, last), s0_ref)
        attend(2 * j + 1, s1_ref)
        return carry

    lax.fori_loop(0, rows_per_step // 2, two_rows, 0)


def _neighbourhood_attention(q, k, v, kx, vx, bias_tab):
    B, pairs, L, _ = q.shape
    rows_per_step = 16
    full = pl.BlockSpec((1, pairs, L, LANES), lambda b, t: (b, 0, 0, 0))
    ctx_spec = pl.BlockSpec((1, pairs, CTX_LEN, LANES), lambda b, t: (0, 0, b, 0))
    q_rows = pl.BlockSpec((1, pairs, rows_per_step * GRID_W, LANES), lambda b, t: (b, 0, t, 0))
    n_keys = NA_KH * GRID_W + CTX_LEN
    all_rows = NA_HEADS * GRID_W
    return pl.pallas_call(
        functools.partial(_na_kernel, rows_per_step),
        grid=(B, GRID_ROWS // rows_per_step),
        in_specs=[q_rows, full, full, ctx_spec, ctx_spec, _const_spec(bias_tab.shape)],
        out_specs=q_rows,
        out_shape=jax.ShapeDtypeStruct((B, pairs, L, LANES), BF16),
        scratch_shapes=[pltpu.VMEM((all_rows, n_keys), F32), pltpu.VMEM((all_rows, n_keys), F32),
                        pltpu.VMEM((all_rows, n_keys), BF16), pltpu.VMEM((all_rows, 1), F32)],
        compiler_params=_params("arbitrary", "arbitrary"),
        name="neighbourhood_attention",
    )(q, k, v, kx, vx, bias_tab)


def kernel(x, c, ctx, c_ctx, ada_w, ada_b, norm1_g, norm2_g, ffn_w_gate, ffn_w_up, ffn_w_down,
           ev_w_in, ev_w_out, ev_q_norm, ev_k_norm, ev_sink,
           od_w_in, od_w_out, od_q_norm, od_k_norm, od_rel_bias):
    assert x.shape == (BATCH, SEQ, D_MODEL) and ctx.shape == (BATCH, CTX_LEN, D_MODEL)
    wc, m1, m3, twc, tws, mctx = _fourier_tables()
    hm = _head_mean_matrix()
    rope_tabs = _rope_tables()
    lane_gain = lambda gvec: jnp.tile(gvec, RMS_W // HEAD_DIM).reshape(1, RMS_W)
    lat_row = lambda b: b
    ctx_row = lambda b: CTX_MOD_ROW
    tm = 512
    n_ctx = BATCH * CTX_LEN
    as_seq = lambda a: a.reshape(1, n_ctx, a.shape[-1])
    per_batch = lambda a: a.reshape(BATCH, CTX_LEN, a.shape[-1])

    cs = jnp.concatenate([c, c_ctx[None, :], jnp.zeros((MOD_ROWS - BATCH - 1, D_MODEL), F32)], axis=0)
    mod = _modulation(cs, ada_w, ada_b).reshape(DEPTH, MOD_ROWS, 1, 6 * D_MODEL)

    w_in0 = ev_w_in[0].astype(BF16)
    w_out0 = ev_w_out[0].astype(BF16)
    g1 = norm1_g[0].reshape(1, D_MODEL)
    g2 = norm2_g[0].reshape(1, D_MODEL)
    ffn = (ffn_w_gate.astype(BF16), ffn_w_up.astype(BF16), ffn_w_down.astype(BF16))
    qg, kg = lane_gain(ev_q_norm[0]), lane_gain(ev_k_norm[0])
    y_l, q_l, k_l, v_l = _in_even(x, mod[0], lat_row, g1, w_in0, qg, kg, hm, wc, rope_tabs, 2 * tm)
    y_c, q_c, k_c, v_c = _in_even(as_seq(ctx), mod[0], ctx_row, g1, w_in0, qg, kg, hm, wc, None, tm)
    y_c, q_c = per_batch(y_c), per_batch(q_c)
    f_l = _fourier_latent(y_l, m1, m3, twc, tws)
    f_c = _fourier_ctx(y_c, mctx)
    a_l = _win_attention(q_l, k_l, v_l, k_c, v_c, ev_sink[0])
    a_c = _ctx_attention_even(q_c, k_c, v_c, ev_sink[0])
    x1 = _out_ffn("even_latent", (f_l, a_l), x, mod[0], lat_row, g2, w_out0, 0, *ffn, 2 * tm)
    y1 = _out_ffn("even_ctx", (as_seq(f_c), as_seq(a_c)), as_seq(ctx), mod[0], ctx_row, g2, w_out0, 0, *ffn, tm)

    w_in1 = od_w_in[0].astype(BF16)
    w_out1 = od_w_out[0].astype(BF16)
    g1 = norm1_g[1].reshape(1, D_MODEL)
    g2 = norm2_g[1].reshape(1, D_MODEL)
    qg, kg = lane_gain(od_q_norm[0]), lane_gain(od_k_norm[0])
    q_l, k_l, v_l = _in_odd(x1, mod[1], lat_row, g1, w_in1, qg, kg, hm, True, 2 * tm)
    k_c, v_c = _in_odd(y1, mod[1], ctx_row, g1, w_in1, qg, kg, hm, False, tm)
    bias_tab = _na_bias_table(od_rel_bias[0], _na_onehot())
    a_l = _neighbourhood_attention(q_l, k_l, v_l, k_c, v_c, bias_tab)
    return _out_ffn("odd", (a_l,), x1, mod[1], lat_row, g2, w_out1, 1, *ffn, 2 * tm)
```

```python
import functools
import math

import numpy as np
import jax
import jax.numpy as jnp
from jax import lax
from jax.experimental import pallas as pl
from jax.experimental.pallas import tpu as pltpu

D_MODEL = 1024
BATCH = 4
SEQ = 4096
DEPTH = 2
GRID_W = 64
CTX_LEN = 256
HEAD_DIM = 64
EPS = 1e-6
NEG = -1e30
FOURIER_WIDTH = D_MODEL // 2
FOURIER_GROUPS = 4
FOURIER_GROUP_CH = FOURIER_WIDTH // FOURIER_GROUPS
WIN_Q_HEADS = (D_MODEL // 2) // HEAD_DIM
WIN_KV_HEADS = 2
WIN_GROUP = WIN_Q_HEADS // WIN_KV_HEADS
WIN_RADIUS = 128
WIN_BLOCK = 128
QW = WIN_Q_HEADS * HEAD_DIM
KW = WIN_KV_HEADS * HEAD_DIM
EV_IN_WIDTH = FOURIER_WIDTH + QW + 2 * KW
NA_HEADS = D_MODEL // HEAD_DIM
NA_KH = 8
NA_KW = 16
NA_WIDTH = NA_HEADS * HEAD_DIM
ROPE_THETA = 10000.0
ROPE_FREQS = HEAD_DIM // 4
D_FF = ((8 * D_MODEL // 3 + 255) // 256) * 256
GRID_ROWS = SEQ // GRID_W

LANES = 128
MOD_ROWS = 8
CTX_MOD_ROW = BATCH
FFT_N = 64
FF_CHUNK = 256
SOFTMAX_ROWS = 64
RMS_W = 256
SUB_ROWS = 512
VMEM_LIMIT = 60 * 1024 * 1024

LOG2E = math.log2(math.e)
QK_SCALE = LOG2E / math.sqrt(HEAD_DIM)

BF16 = jnp.bfloat16
F32 = jnp.float32

assert DEPTH == 2 and SEQ == FFT_N * FFT_N and D_FF % FF_CHUNK == 0


def _params(*sem):
    return pltpu.CompilerParams(dimension_semantics=sem, vmem_limit_bytes=VMEM_LIMIT)


def _dot(a, b):
    return jnp.dot(a, b, preferred_element_type=F32)


def _dot_nt(a, b):
    return lax.dot_general(a, b, (((1,), (1,)), ((), ())), preferred_element_type=F32)


def _silu(x):
    return x / (1.0 + jnp.exp(-x))


def _const_spec(shape):
    nd = len(shape)
    return pl.BlockSpec(shape, lambda *_: (0,) * nd, pipeline_mode=pl.Buffered(1))


def _dft_cos_sin(n):
    idx = (np.arange(n)[:, None] * np.arange(n)[None, :]) % n
    ang = 2.0 * np.pi * idx / n
    return np.cos(ang), np.sin(ang)


def _fourier_tables():
    cc, sc = _dft_cos_sin(FOURIER_GROUP_CH)
    wc = np.concatenate([cc, -sc], axis=1) / math.sqrt(FOURIER_GROUP_CH)
    c64, s64 = _dft_cos_sin(FFT_N)
    m1 = np.concatenate([c64, -s64], axis=0) / math.sqrt(FFT_N)
    m3 = np.concatenate([c64, s64], axis=1) / math.sqrt(FFT_N)
    tw = (np.arange(FFT_N)[:, None] * np.arange(FFT_N)[None, :]) % SEQ
    tw = 2.0 * np.pi * tw / SEQ
    twc = np.repeat(np.cos(tw)[:, :, None], LANES, axis=2)
    tws = np.repeat(np.sin(tw)[:, :, None], LANES, axis=2)
    cx, sx = _dft_cos_sin(CTX_LEN)
    mctx = np.concatenate([cx, sx], axis=1) / math.sqrt(CTX_LEN)
    as32 = lambda a: jnp.asarray(a, F32)
    return (as32(wc).astype(BF16), as32(m1).astype(BF16), as32(m3).astype(BF16),
            as32(twc), as32(tws), as32(mctx).astype(BF16))


def _head_mean_matrix():
    blk = np.kron(np.eye(RMS_W // HEAD_DIM), np.ones((HEAD_DIM, HEAD_DIM))) / HEAD_DIM
    return jnp.asarray(blk, BF16)


def _rope_tables():
    t = jnp.arange(SEQ, dtype=jnp.int32)
    row = (t // GRID_W).astype(F32)
    col = (t % GRID_W).astype(F32)
    inv = ROPE_THETA ** (-jnp.arange(ROPE_FREQS, dtype=F32) / ROPE_FREQS)
    ang_row = row[:, None] * inv[None, :]
    ang_col = col[:, None] * inv[None, :]
    zero = jnp.zeros_like(ang_row)
    cos = jnp.concatenate([jnp.cos(ang_row)] * 2 + [jnp.cos(ang_col)] * 2, axis=1)
    sin_hi = jnp.concatenate([-jnp.sin(ang_row), zero, -jnp.sin(ang_col), zero], axis=1)
    sin_lo = jnp.concatenate([zero, jnp.sin(ang_row), zero, jnp.sin(ang_col)], axis=1)
    rep = LANES // HEAD_DIM
    return jnp.tile(cos, (1, rep)), jnp.tile(sin_hi, (1, rep)), jnp.tile(sin_lo, (1, rep))


def _na_bias_index():
    cq = np.arange(GRID_W)
    dc = np.clip(cq[None, :] - cq[:, None] + NA_KW - 1, 0, 2 * NA_KW - 2)
    return jnp.asarray(np.concatenate([dc, dc + LANES // 2], axis=1), jnp.int32)


def _mod_kernel(cs_ref, w_ref, b_ref, o_ref):
    s = _silu(cs_ref[...]).astype(BF16)
    o_ref[0] = _dot(s, w_ref[0].astype(BF16)) + b_ref[0]


def _modulation(cs, ada_w, ada_b):
    tn = 1536
    return pl.pallas_call(
        _mod_kernel,
        grid=(DEPTH, 6 * D_MODEL // tn),
        in_specs=[pl.BlockSpec((MOD_ROWS, D_MODEL), lambda i, j: (0, 0)),
                  pl.BlockSpec((1, D_MODEL, tn), lambda i, j: (i, 0, j)),
                  pl.BlockSpec((1, 1, tn), lambda i, j: (i, 0, j))],
        out_specs=pl.BlockSpec((1, MOD_ROWS, tn), lambda i, j: (i, 0, j)),
        out_shape=jax.ShapeDtypeStruct((DEPTH, MOD_ROWS, 6 * D_MODEL), F32),
        compiler_params=_params("arbitrary", "arbitrary"),
        name="ada_modulation",
    )(cs, ada_w, ada_b.reshape(DEPTH, 1, 6 * D_MODEL))


def _mod_slice(mod_ref, k):
    return mod_ref[0, :, k * D_MODEL:(k + 1) * D_MODEL]


def _rms_mod(x, g, scale, shift):
    y = x * lax.rsqrt(jnp.mean(x * x, axis=-1, keepdims=True) + EPS)
    return (y * g) * (1.0 + scale) + shift


def _head_rms(t, gain_ref, hm_ref):
    w = t.shape[1]
    ms = _dot((t * t).astype(BF16), hm_ref[:w, :w])
    return t * lax.rsqrt(ms + EPS) * gain_ref[:, :w]


def _rope(t, cos, sin_hi, sin_lo):
    up = pltpu.roll(t, LANES - ROPE_FREQS, axis=1)
    dn = pltpu.roll(t, ROPE_FREQS, axis=1)
    return t * cos + up * sin_hi + dn * sin_lo


def _in_even_kernel(latent, x_ref, mod_ref, g_ref, w_ref, qg_ref, kg_ref, hm_ref, wc_ref, *rest):
    if latent:
        cos_ref, sh_ref, sl_ref, y_ref, q_ref, k_ref, v_ref = rest
    else:
        y_ref, q_ref, k_ref, v_ref = rest
    F = FOURIER_WIDTH
    lo_lanes = lax.broadcasted_iota(jnp.int32, (1, LANES), 1) < HEAD_DIM

    for r in range(x_ref.shape[1] // SUB_ROWS):
        rows = slice(r * SUB_ROWS, (r + 1) * SUB_ROWS)
        h = _rms_mod(x_ref[0, rows], g_ref[...], _mod_slice(mod_ref, 1), _mod_slice(mod_ref, 0)).astype(BF16)
        f = _dot(h, w_ref[:, :F]).astype(BF16)
        for g in range(FOURIER_GROUPS):
            yg = _dot(f[:, g * LANES:(g + 1) * LANES], wc_ref[...])
            y_ref[0, rows, g * LANES:(g + 1) * LANES] = yg[:, :LANES].astype(BF16)
            y_ref[0, rows, F + g * LANES:F + (g + 1) * LANES] = yg[:, LANES:].astype(BF16)

        def qk(t, gain_ref):
            t = _head_rms(t, gain_ref, hm_ref)
            if latent:
                t = jnp.concatenate([_rope(t[:, i * LANES:(i + 1) * LANES], cos_ref[rows], sh_ref[rows], sl_ref[rows])
                                     for i in range(t.shape[1] // LANES)], axis=1)
            return t

        q = _dot(h, w_ref[:, F:F + QW])
        for j in range(QW // RMS_W):
            qj = qk(q[:, j * RMS_W:(j + 1) * RMS_W], qg_ref) * QK_SCALE
            q_ref[0, rows, j * RMS_W:(j + 1) * RMS_W] = qj.astype(BF16)
        kv = _dot(h, w_ref[:, F + QW:])

        def store_dup(t, o_ref):
            sw = pltpu.roll(t, HEAD_DIM, axis=1)
            o_ref[0, 0, rows] = jnp.where(lo_lanes, t, sw).astype(BF16)
            o_ref[0, 1, rows] = jnp.where(lo_lanes, sw, t).astype(BF16)

        store_dup(qk(kv[:, :KW], kg_ref), k_ref)
        store_dup(kv[:, KW:], v_ref)


def _in_even(x, mod, mod_row, g, w_in, q_g, k_g, hm, wc, rope_tabs, tm):
    B, L, _ = x.shape
    latent = rope_tabs is not None
    in_specs = [pl.BlockSpec((1, tm, D_MODEL), lambda b, t: (b, t, 0)),
                pl.BlockSpec((1, 1, 6 * D_MODEL), lambda b, t: (mod_row(b), 0, 0)),
                _const_spec((1, D_MODEL)),
                _const_spec((D_MODEL, EV_IN_WIDTH)),
                _const_spec((1, RMS_W)), _const_spec((1, RMS_W)),
                _const_spec((RMS_W, RMS_W)), _const_spec((LANES, 2 * LANES))]
    args = [x, mod, g, w_in, q_g, k_g, hm, wc]
    tok = lambda w: pl.BlockSpec((1, tm, w), lambda b, t: (b, t, 0))
    kv_spec = pl.BlockSpec((1, WIN_KV_HEADS, tm, LANES), lambda b, t: (b, 0, t, 0))
    if latent:
        in_specs += [pl.BlockSpec((tm, LANES), lambda b, t: (t, 0))] * 3
        args += list(rope_tabs)
    return pl.pallas_call(
        functools.partial(_in_even_kernel, latent),
        grid=(B, L // tm),
        in_specs=in_specs,
        out_specs=[tok(2 * FOURIER_WIDTH), tok(QW), kv_spec, kv_spec],
        out_shape=[jax.ShapeDtypeStruct((B, L, 2 * FOURIER_WIDTH), BF16),
                   jax.ShapeDtypeStruct((B, L, QW), BF16),
                   jax.ShapeDtypeStruct((B, WIN_KV_HEADS, L, LANES), BF16),
                   jax.ShapeDtypeStruct((B, WIN_KV_HEADS, L, LANES), BF16)],
        compiler_params=_params("arbitrary", "arbitrary"),
        name="in_even_latent" if latent else "in_even_ctx",
    )(*args)


def _fft_stage1_kernel(n2t, y_ref, m1_ref, twc_ref, tws_ref, o_ref):
    F = FOURIER_WIDTH
    for j in range(n2t):
        a = _dot(m1_ref[...], y_ref[0, :, j * 2 * F:(j + 1) * 2 * F])
        top, bot = a[:FFT_N], a[FFT_N:]
        ar = top[:, :F] - bot[:, F:]
        ai = top[:, F:] + bot[:, :F]
        tc = jnp.tile(twc_ref[j], (1, F // LANES))
        ts = jnp.tile(tws_ref[j], (1, F // LANES))
        o_ref[0, 0, :, j * F:(j + 1) * F] = (ar * tc + ai * ts).astype(BF16)
        o_ref[0, 1, :, j * F:(j + 1) * F] = (ai * tc - ar * ts).astype(BF16)


def _fft_stage2_kernel(k1t, b_ref, m3_ref, o_ref):
    for j in range(k1t):
        rhs = jnp.concatenate([b_ref[0, 0, j], b_ref[0, 1, j]], axis=0)
        o_ref[0, j] = _dot(m3_ref[...], rhs)


def _fourier_latent(y, m1, m3, twc, tws):
    B = y.shape[0]
    F = FOURIER_WIDTH
    n2t = 16
    yv = y.reshape(B, FFT_N, FFT_N * 2 * F)
    bh = pl.pallas_call(
        functools.partial(_fft_stage1_kernel, n2t),
        grid=(B, FFT_N // n2t),
        in_specs=[pl.BlockSpec((1, FFT_N, n2t * 2 * F), lambda b, t: (b, 0, t)),
                  _const_spec((2 * FFT_N, FFT_N)),
                  pl.BlockSpec((n2t, FFT_N, LANES), lambda b, t: (t, 0, 0)),
                  pl.BlockSpec((n2t, FFT_N, LANES), lambda b, t: (t, 0, 0))],
        out_specs=pl.BlockSpec((1, 2, FFT_N, n2t * F), lambda b, t: (b, 0, 0, t)),
        out_shape=jax.ShapeDtypeStruct((B, 2, FFT_N, FFT_N * F), BF16),
        compiler_params=_params("arbitrary", "arbitrary"),
        name="fft_stage1",
    )(yv, m1, twc, tws)
    k1t = 16
    bv = bh.reshape(B, 2, FFT_N, FFT_N, F)
    return pl.pallas_call(
        functools.partial(_fft_stage2_kernel, k1t),
        grid=(B, FFT_N // k1t),
        in_specs=[pl.BlockSpec((1, 2, k1t, FFT_N, F), lambda b, t: (b, 0, t, 0, 0)),
                  _const_spec((FFT_N, 2 * FFT_N))],
        out_specs=pl.BlockSpec((1, k1t, FFT_N, F), lambda b, t: (b, t, 0, 0)),
        out_shape=jax.ShapeDtypeStruct((B, FFT_N, FFT_N, F), F32),
        compiler_params=_params("arbitrary", "arbitrary"),
        name="fft_stage2",
    )(bv, m3)


def _fourier_ctx_kernel(y_ref, m_ref, o_ref):
    F = FOURIER_WIDTH
    rhs = jnp.concatenate([y_ref[0, :, :F], y_ref[0, :, F:]], axis=0)
    o_ref[0] = _dot(m_ref[...], rhs)


def _fourier_ctx(y, mctx):
    B = y.shape[0]
    return pl.pallas_call(
        _fourier_ctx_kernel,
        grid=(B,),
        in_specs=[pl.BlockSpec((1, CTX_LEN, 2 * FOURIER_WIDTH), lambda b: (b, 0, 0)),
                  _const_spec((CTX_LEN, 2 * CTX_LEN))],
        out_specs=pl.BlockSpec((1, CTX_LEN, FOURIER_WIDTH), lambda b: (b, 0, 0)),
        out_shape=jax.ShapeDtypeStruct((B, CTX_LEN, FOURIER_WIDTH), F32),
        compiler_params=_params("arbitrary"),
        name="fourier_ctx",
    )(y, mctx)


def _gqa_group(q_slabs, keys, vals, masks, sinks, rows, o_ref, first_slab, scratch):
    s_ref, p_ref, inv_ref = scratch
    _gqa_scores(q_slabs, keys, rows, s_ref)
    _gqa_attend(vals, masks, sinks, rows, lambda sl: o_ref.at[0, :, sl], first_slab, scratch)


def _gqa_scores(q_slabs, keys, rows, s_ref):
    lo_lanes = lax.broadcasted_iota(jnp.int32, (rows, LANES), 1) < HEAD_DIM
    zero = jnp.zeros((rows, LANES), BF16)
    qs = jnp.concatenate([jnp.where(lo_lanes if half == 0 else ~lo_lanes, qslab, zero)
                          for qslab in q_slabs for half in range(2)], axis=0)
    s_ref[...] = _dot_nt(qs, keys)


def _gqa_attend(vals, masks, sinks, rows, out_slab, first_slab, scratch):
    s_ref, p_ref, inv_ref = scratch
    lo_lanes = lax.broadcasted_iota(jnp.int32, (rows, LANES), 1) < HEAD_DIM
    for c in range(WIN_GROUP * rows // SOFTMAX_ROWS):
        rs = slice(c * SOFTMAX_ROWS, (c + 1) * SOFTMAX_ROWS)
        s = s_ref[rs, :]
        if masks is not None:
            mask, w = masks
            r0 = (c * SOFTMAX_ROWS) % rows
            s = jnp.concatenate([jnp.where(mask[r0:r0 + SOFTMAX_ROWS], s[:, :w], NEG), s[:, w:]], axis=1)
        sk = sinks[(c * SOFTMAX_ROWS) // rows]
        m = jnp.maximum(jnp.max(s, axis=-1, keepdims=True), sk)
        e = jnp.exp2(s - m)
        inv_ref[rs, :] = 1.0 / (jnp.sum(e, axis=-1, keepdims=True) + jnp.exp2(sk - m))
        p_ref[rs, :] = e.astype(BF16)
    o = _dot(p_ref[...], vals) * inv_ref[...]
    for i in range(WIN_GROUP // 2):
        a = o[(2 * i) * rows:(2 * i + 1) * rows]
        b = o[(2 * i + 1) * rows:(2 * i + 2) * rows]
        sl = slice((first_slab + i) * LANES, (first_slab + i + 1) * LANES)
        out_slab(sl)[...] = jnp.where(lo_lanes, a, b).astype(BF16)


def _gqa_scratch(rows, n_keys):
    one = [pltpu.VMEM((WIN_GROUP * rows, n_keys), F32), pltpu.VMEM((WIN_GROUP * rows, n_keys), BF16),
           pltpu.VMEM((WIN_GROUP * rows, 1), F32)]
    return one * WIN_KV_HEADS


def _group_sinks(sink_ref, kv):
    return [sink_ref[kv * WIN_GROUP + g] * LOG2E for g in range(WIN_GROUP)]


def _group_slab(kv, i):
    return slice((kv * (WIN_GROUP // 2) + i) * LANES, (kv * (WIN_GROUP // 2) + i + 1) * LANES)


def _window_bands():
    i = np.arange(WIN_BLOCK)[:, None]
    j = np.arange(3 * WIN_BLOCK)[None, :]
    bands = [np.abs(j - off - i) <= WIN_RADIUS for off in (0, WIN_BLOCK, 2 * WIN_BLOCK)]
    return jnp.asarray(np.stack(bands).astype(np.float32))


def _win_attn_kernel(blocks_per_step, sink_ref, band_ref, q_ref, k_ref, v_ref, kx_ref, vx_ref, o_ref,
                     s0_ref, s1_ref, p_ref, inv_ref):
    t = pl.program_id(1)
    nb = SEQ // WIN_BLOCK
    nw = 3 * WIN_BLOCK

    def place(j):
        n = t * blocks_per_step + j
        start = pl.multiple_of(jnp.clip((n - 1) * WIN_BLOCK, 0, SEQ - nw), WIN_BLOCK)
        sel = jnp.where(n == 0, 0, jnp.where(n == nb - 1, 2, 1))
        return pl.multiple_of(j * WIN_BLOCK, WIN_BLOCK), start, sel

    def scores(j, kv, s_ref):
        qoff, start, _ = place(j)
        keys = jnp.concatenate([k_ref[0, kv, pl.ds(start, nw), :], kx_ref[0, kv]], axis=0)
        q_slabs = [q_ref[0, pl.ds(qoff, WIN_BLOCK), _group_slab(kv, i)] for i in range(WIN_GROUP // 2)]
        _gqa_scores(q_slabs, keys, WIN_BLOCK, s_ref)

    def attend(j, kv, s_ref):
        qoff, start, sel = place(j)
        vals = jnp.concatenate([v_ref[0, kv, pl.ds(start, nw), :], vx_ref[0, kv]], axis=0)
        _gqa_attend(vals, (band_ref[sel] > 0.0, nw), _group_sinks(sink_ref, kv), WIN_BLOCK,
                    lambda sl: o_ref.at[0, pl.ds(qoff, WIN_BLOCK), sl], kv * (WIN_GROUP // 2),
                    (s_ref, p_ref, inv_ref))

    scores(0, 0, s0_ref)

    def one_block(j, carry):
        scores(j, 1, s1_ref)
        attend(j, 0, s0_ref)
        scores(jnp.minimum(j + 1, blocks_per_step - 1), 0, s0_ref)
        attend(j, 1, s1_ref)
        return carry

    lax.fori_loop(0, blocks_per_step, one_block, 0)


def _win_attention(q, k, v, kx, vx, sink):
    B, L, _ = q.shape
    blocks_per_step = 8
    rows = blocks_per_step * WIN_BLOCK
    n_keys = 3 * WIN_BLOCK + CTX_LEN
    full = pl.BlockSpec((1, WIN_KV_HEADS, L, LANES), lambda b, t: (b, 0, 0, 0))
    ctx_spec = pl.BlockSpec((1, WIN_KV_HEADS, CTX_LEN, LANES), lambda b, t: (0, 0, b, 0))
    tok = pl.BlockSpec((1, rows, QW), lambda b, t: (b, t, 0))
    return pl.pallas_call(
        functools.partial(_win_attn_kernel, blocks_per_step),
        grid=(B, L // rows),
        in_specs=[pl.BlockSpec(memory_space=pltpu.SMEM),
                  _const_spec((3, WIN_BLOCK, 3 * WIN_BLOCK)),
                  tok, full, full, ctx_spec, ctx_spec],
        out_specs=tok,
        out_shape=jax.ShapeDtypeStruct((B, L, QW), BF16),
        scratch_shapes=[pltpu.VMEM((WIN_GROUP * WIN_BLOCK, n_keys), F32), pltpu.VMEM((WIN_GROUP * WIN_BLOCK, n_keys), F32),
                        pltpu.VMEM((WIN_GROUP * WIN_BLOCK, n_keys), BF16), pltpu.VMEM((WIN_GROUP * WIN_BLOCK, 1), F32)],
        compiler_params=_params("arbitrary", "arbitrary"),
        name="window_attention",
    )(sink, _window_bands(), q, k, v, kx, vx)


def _ctx_attn_even_kernel(sink_ref, q_ref, k_ref, v_ref, o_ref, *scratch):
    for kv in range(WIN_KV_HEADS):
        q_slabs = [q_ref[0, :, _group_slab(kv, i)] for i in range(WIN_GROUP // 2)]
        _gqa_group(q_slabs, k_ref[0, kv], v_ref[0, kv], None, _group_sinks(sink_ref, kv), CTX_LEN, o_ref,
                   kv * (WIN_GROUP // 2), scratch[3 * kv:3 * kv + 3])


def _ctx_attention_even(q, k, v, sink):
    B = q.shape[0]
    kv_spec = pl.BlockSpec((1, WIN_KV_HEADS, CTX_LEN, LANES), lambda b: (0, 0, b, 0))
    return pl.pallas_call(
        _ctx_attn_even_kernel,
        grid=(B,),
        in_specs=[pl.BlockSpec(memory_space=pltpu.SMEM),
                  pl.BlockSpec((1, CTX_LEN, QW), lambda b: (b, 0, 0)),
                  kv_spec, kv_spec],
        out_specs=pl.BlockSpec((1, CTX_LEN, QW), lambda b: (b, 0, 0)),
        out_shape=jax.ShapeDtypeStruct((B, CTX_LEN, QW), BF16),
        scratch_shapes=_gqa_scratch(CTX_LEN, CTX_LEN),
        compiler_params=_params("arbitrary"),
        name="ctx_attention_even",
    )(sink, q, k, v)


def _out_ffn_kernel(mode, *refs):
    if mode == "odd":
        a_ref, x_ref, mod_ref, g_ref, wo_ref, wg_ref, wu_ref, wd_ref, o_ref = refs
    else:
        f_ref, a_ref, x_ref, mod_ref, g_ref, wo_ref, wg_ref, wu_ref, wd_ref, o_ref = refs
    for r in range(x_ref.shape[1] // SUB_ROWS):
        rows = slice(r * SUB_ROWS, (r + 1) * SUB_ROWS)
        if mode == "even_latent":
            planes = range(r * SUB_ROWS // FFT_N, (r + 1) * SUB_ROWS // FFT_N)
            fm = jnp.concatenate([f_ref[0, :, j, :] for j in planes], axis=0).astype(BF16)
            o = _dot(fm, wo_ref[:FOURIER_WIDTH]) + _dot(a_ref[0, rows], wo_ref[FOURIER_WIDTH:])
        elif mode == "even_ctx":
            o = _dot(f_ref[0, rows].astype(BF16), wo_ref[:FOURIER_WIDTH]) + _dot(a_ref[0, rows], wo_ref[FOURIER_WIDTH:])
        else:
            o = _dot(jnp.concatenate([a_ref[0, j, rows] for j in range(a_ref.shape[1])], axis=1), wo_ref[...])
        x1 = x_ref[0, rows] + _mod_slice(mod_ref, 2) * o
        h = _rms_mod(x1, g_ref[...], _mod_slice(mod_ref, 4), _mod_slice(mod_ref, 3)).astype(BF16)
        acc = jnp.zeros_like(x1)
        for c in range(D_FF // FF_CHUNK):
            cs = slice(c * FF_CHUNK, (c + 1) * FF_CHUNK)
            a = _silu(_dot(h, wg_ref[0, :, cs])) * _dot(h, wu_ref[0, :, cs])
            acc = acc + _dot(a.astype(BF16), wd_ref[0, cs, :])
        o_ref[0, rows] = x1 + _mod_slice(mod_ref, 5) * acc


def _out_ffn(mode, mix, x, mod, mod_row, g, w_out, layer, wg, wu, wd, tm):
    B, L, _ = x.shape
    tok = lambda w: pl.BlockSpec((1, tm, w), lambda b, t: (b, t, 0))
    ffn_spec = lambda r, c: pl.BlockSpec((1, r, c), lambda b, t: (layer, 0, 0), pipeline_mode=pl.Buffered(1))
    if mode == "even_latent":
        assert tm % FFT_N == 0
        mix_specs = [pl.BlockSpec((1, FFT_N, tm // FFT_N, FOURIER_WIDTH), lambda b, t: (b, 0, t, 0)), tok(QW)]
    elif mode == "even_ctx":
        mix_specs = [tok(FOURIER_WIDTH), tok(QW)]
    else:
        mix_specs = [pl.BlockSpec((1, NA_WIDTH // LANES, tm, LANES), lambda b, t: (b, 0, t, 0))]
    return pl.pallas_call(
        functools.partial(_out_ffn_kernel, mode),
        grid=(B, L // tm),
        in_specs=mix_specs + [tok(D_MODEL),
                              pl.BlockSpec((1, 1, 6 * D_MODEL), lambda b, t: (mod_row(b), 0, 0)),
                              _const_spec((1, D_MODEL)),
                              _const_spec((D_MODEL, D_MODEL)),
                              ffn_spec(D_MODEL, D_FF), ffn_spec(D_MODEL, D_FF), ffn_spec(D_FF, D_MODEL)],
        out_specs=tok(D_MODEL),
        out_shape=jax.ShapeDtypeStruct((B, L, D_MODEL), F32),
        compiler_params=_params("arbitrary", "arbitrary"),
        name="out_ffn_" + mode,
    )(*mix, x, mod, g, w_out, wg, wu, wd)


def _in_odd_kernel(with_q, x_ref, mod_ref, g_ref, w_ref, qg_ref, kg_ref, hm_ref, *outs):
    W = NA_WIDTH
    slabs = RMS_W // LANES
    if with_q:
        q_ref, k_ref, v_ref = outs
    else:
        k_ref, v_ref = outs

    for r in range(x_ref.shape[1] // SUB_ROWS):
        rows = slice(r * SUB_ROWS, (r + 1) * SUB_ROWS)
        h = _rms_mod(x_ref[0, rows], g_ref[...], _mod_slice(mod_ref, 1), _mod_slice(mod_ref, 0)).astype(BF16)

        def normed(t, gain_ref, scale, o_ref):
            for j in range(W // RMS_W):
                tj = (_head_rms(t[:, j * RMS_W:(j + 1) * RMS_W], gain_ref, hm_ref) * scale).astype(BF16)
                for i in range(slabs):
                    o_ref[0, j * slabs + i, rows] = tj[:, i * LANES:(i + 1) * LANES]

        if with_q:
            normed(_dot(h, w_ref[:, :W]), qg_ref, QK_SCALE, q_ref)
        normed(_dot(h, w_ref[:, W:2 * W]), kg_ref, 1.0, k_ref)
        v = _dot(h, w_ref[:, 2 * W:]).astype(BF16)
        for j in range(W // LANES):
            v_ref[0, j, rows] = v[:, j * LANES:(j + 1) * LANES]


def _in_odd(x, mod, mod_row, g, w_in, q_g, k_g, hm, with_q, tm):
    B, L, _ = x.shape
    pairs = NA_WIDTH // LANES
    tok = pl.BlockSpec((1, pairs, tm, LANES), lambda b, t: (b, 0, t, 0))
    n_out = 3 if with_q else 2
    return pl.pallas_call(
        functools.partial(_in_odd_kernel, with_q),
        grid=(B, L // tm),
        in_specs=[pl.BlockSpec((1, tm, D_MODEL), lambda b, t: (b, t, 0)),
                  pl.BlockSpec((1, 1, 6 * D_MODEL), lambda b, t: (mod_row(b), 0, 0)),
                  _const_spec((1, D_MODEL)),
                  _const_spec((D_MODEL, 3 * NA_WIDTH)),
                  _const_spec((1, RMS_W)), _const_spec((1, RMS_W)), _const_spec((RMS_W, RMS_W))],
        out_specs=[tok] * n_out,
        out_shape=[jax.ShapeDtypeStruct((B, pairs, L, LANES), BF16)] * n_out,
        compiler_params=_params("arbitrary", "arbitrary"),
        name="in_odd" if with_q else "in_odd_ctx",
    )(x, mod, g, w_in, q_g, k_g, hm)


def _bias_table_kernel(rb_ref, idx_ref, o_ref):
    idx = idx_ref[...]
    for dr in range(o_ref.shape[1]):
        row = jnp.broadcast_to(rb_ref[0, dr:dr + 1, :], idx.shape)
        o_ref[0, dr] = jnp.take_along_axis(row, idx, axis=1) * LOG2E


def _na_bias_table(rel_bias, idx):
    H = NA_HEADS
    npair = 2 * NA_KH - 2
    half = LANES // 2
    pad = lambda a: jnp.pad(a, ((0, 0), (0, 0), (0, half - a.shape[-1])))
    rb2 = jnp.concatenate([pad(rel_bias[:, :-1]), pad(rel_bias[:, 1:])], axis=-1)
    return pl.pallas_call(
        _bias_table_kernel,
        grid=(H,),
        in_specs=[pl.BlockSpec((1, npair, LANES), lambda h: (h, 0, 0)), _const_spec((GRID_W, LANES))],
        out_specs=pl.BlockSpec((1, npair, GRID_W, LANES), lambda h: (h, 0, 0, 0)),
        out_shape=jax.ShapeDtypeStruct((H, npair, GRID_W, LANES), F32),
        compiler_params=_params("arbitrary"),
        name="na_bias_table",
    )(rb2, idx)


def _na_kernel(rows_per_step, q_ref, k_ref, v_ref, kx_ref, vx_ref, bias_ref, o_ref, s0_ref, s1_ref, p_ref, inv_ref):
    t = pl.program_id(1)
    n = NA_KH * GRID_W
    cq = lax.broadcasted_iota(jnp.int32, (GRID_W, n), 0)
    ck = lax.broadcasted_iota(jnp.int32, (GRID_W, n), 1) % GRID_W
    c0 = jnp.clip(cq - NA_KW // 2, 0, GRID_W - NA_KW)
    mask = (ck >= c0) & (ck < c0 + NA_KW)
    lo_lanes = lax.broadcasted_iota(jnp.int32, (GRID_W, LANES), 1) < HEAD_DIM

    def window(i):
        r = t * rows_per_step + i
        r0 = jnp.clip(r - NA_KH // 2, 0, GRID_ROWS - NA_KH)
        return r0 - r + NA_KH - 1, pl.multiple_of(r0 * GRID_W, GRID_W), pl.multiple_of(i * GRID_W, GRID_W)

    def scores(i, s_ref):
        dr0, start, qoff = window(i)
        for hp in range(NA_HEADS // 2):
            qs = q_ref[0, hp, pl.ds(qoff, GRID_W), :]
            keys = jnp.concatenate([k_ref[0, hp, pl.ds(start, n), :], kx_ref[0, hp]], axis=0)
            qm = jnp.concatenate([jnp.where(lo_lanes, qs, jnp.zeros_like(qs)),
                                  jnp.where(lo_lanes, jnp.zeros_like(qs), qs)], axis=0)
            s = _dot_nt(qm, keys)
            for half in range(2):
                h = 2 * hp + half
                bias = jnp.concatenate([bias_ref[h, dr0 + 2 * p] for p in range(NA_KH // 2)], axis=1)
                rs = slice(h * GRID_W, (h + 1) * GRID_W)
                hr = slice(half * GRID_W, (half + 1) * GRID_W)
                s_ref[rs, :n] = jnp.where(mask, s[hr, :n] + bias, NEG)
                s_ref[rs, n:] = s[hr, n:]

    def attend(i, s_ref):
        _, start, qoff = window(i)
        for c in range(NA_HEADS * GRID_W // SOFTMAX_ROWS):
            cs = slice(c * SOFTMAX_ROWS, (c + 1) * SOFTMAX_ROWS)
            s = s_ref[cs, :]
            e = jnp.exp2(s - jnp.max(s, axis=-1, keepdims=True))
            inv_ref[cs, :] = 1.0 / jnp.sum(e, axis=-1, keepdims=True)
            p_ref[cs, :] = e.astype(BF16)
        for hp in range(NA_HEADS // 2):
            vals = jnp.concatenate([v_ref[0, hp, pl.ds(start, n), :], vx_ref[0, hp]], axis=0)
            rs = slice(2 * hp * GRID_W, (2 * hp + 2) * GRID_W)
            res = _dot(p_ref[rs, :], vals) * inv_ref[rs, :]
            o_ref[0, hp, pl.ds(qoff, GRID_W), :] = jnp.where(lo_lanes, res[:GRID_W], res[GRID_W:]).astype(BF16)

    last = rows_per_step - 1
    scores(0, s0_ref)

    def two_rows(j, carry):
        scores(2 * j + 1, s1_ref)
        attend(2 * j, s0_ref)
        scores(jnp.minimum(2 * j + 2, last), s0_ref)
        attend(2 * j + 1, s1_ref)
        return carry

    lax.fori_loop(0, rows_per_step // 2, two_rows, 0)


def _neighbourhood_attention(q, k, v, kx, vx, bias_tab):
    B, pairs, L, _ = q.shape
    rows_per_step = 16
    full = pl.BlockSpec((1, pairs, L, LANES), lambda b, t: (b, 0, 0, 0))
    ctx_spec = pl.BlockSpec((1, pairs, CTX_LEN, LANES), lambda b, t: (0, 0, b, 0))
    q_rows = pl.BlockSpec((1, pairs, rows_per_step * GRID_W, LANES), lambda b, t: (b, 0, t, 0))
    n_keys = NA_KH * GRID_W + CTX_LEN
    all_rows = NA_HEADS * GRID_W
    return pl.pallas_call(
        functools.partial(_na_kernel, rows_per_step),
        grid=(B, GRID_ROWS // rows_per_step),
        in_specs=[q_rows, full, full, ctx_spec, ctx_spec, _const_spec(bias_tab.shape)],
        out_specs=q_rows,
        out_shape=jax.ShapeDtypeStruct((B, pairs, L, LANES), BF16),
        scratch_shapes=[pltpu.VMEM((all_rows, n_keys), F32), pltpu.VMEM((all_rows, n_keys), F32),
                        pltpu.VMEM((all_rows, n_keys), BF16), pltpu.VMEM((all_rows, 1), F32)],
        compiler_params=_params("arbitrary", "arbitrary"),
        name="neighbourhood_attention",
    )(q, k, v, kx, vx, bias_tab)


def kernel(x, c, ctx, c_ctx, ada_w, ada_b, norm1_g, norm2_g, ffn_w_gate, ffn_w_up, ffn_w_down,
           ev_w_in, ev_w_out, ev_q_norm, ev_k_norm, ev_sink,
           od_w_in, od_w_out, od_q_norm, od_k_norm, od_rel_bias):
    assert x.shape == (BATCH, SEQ, D_MODEL) and ctx.shape == (BATCH, CTX_LEN, D_MODEL)
    wc, m1, m3, twc, tws, mctx = _fourier_tables()
    hm = _head_mean_matrix()
    rope_tabs = _rope_tables()
    lane_gain = lambda gvec: jnp.tile(gvec, RMS_W // HEAD_DIM).reshape(1, RMS_W)
    lat_row = lambda b: b
    ctx_row = lambda b: CTX_MOD_ROW
    tm = 512
    n_ctx = BATCH * CTX_LEN
    as_seq = lambda a: a.reshape(1, n_ctx, a.shape[-1])
    per_batch = lambda a: a.reshape(BATCH, CTX_LEN, a.shape[-1])

    cs = jnp.concatenate([c, c_ctx[None, :], jnp.zeros((MOD_ROWS - BATCH - 1, D_MODEL), F32)], axis=0)
    mod = _modulation(cs, ada_w, ada_b).reshape(DEPTH, MOD_ROWS, 1, 6 * D_MODEL)

    w_in0 = ev_w_in[0].astype(BF16)
    w_out0 = ev_w_out[0].astype(BF16)
    g1 = norm1_g[0].reshape(1, D_MODEL)
    g2 = norm2_g[0].reshape(1, D_MODEL)
    ffn = (ffn_w_gate.astype(BF16), ffn_w_up.astype(BF16), ffn_w_down.astype(BF16))
    qg, kg = lane_gain(ev_q_norm[0]), lane_gain(ev_k_norm[0])
    y_l, q_l, k_l, v_l = _in_even(x, mod[0], lat_row, g1, w_in0, qg, kg, hm, wc, rope_tabs, 2 * tm)
    y_c, q_c, k_c, v_c = _in_even(as_seq(ctx), mod[0], ctx_row, g1, w_in0, qg, kg, hm, wc, None, tm)
    y_c, q_c = per_batch(y_c), per_batch(q_c)
    f_l = _fourier_latent(y_l, m1, m3, twc, tws)
    f_c = _fourier_ctx(y_c, mctx)
    a_l = _win_attention(q_l, k_l, v_l, k_c, v_c, ev_sink[0])
    a_c = _ctx_attention_even(q_c, k_c, v_c, ev_sink[0])
    x1 = _out_ffn("even_latent", (f_l, a_l), x, mod[0], lat_row, g2, w_out0, 0, *ffn, 2 * tm)
    y1 = _out_ffn("even_ctx", (as_seq(f_c), as_seq(a_c)), as_seq(ctx), mod[0], ctx_row, g2, w_out0, 0, *ffn, tm)

    w_in1 = od_w_in[0].astype(BF16)
    w_out1 = od_w_out[0].astype(BF16)
    g1 = norm1_g[1].reshape(1, D_MODEL)
    g2 = norm2_g[1].reshape(1, D_MODEL)
    qg, kg = lane_gain(od_q_norm[0]), lane_gain(od_k_norm[0])
    q_l, k_l, v_l = _in_odd(x1, mod[1], lat_row, g1, w_in1, qg, kg, hm, True, 2 * tm)
    k_c, v_c = _in_odd(y1, mod[1], ctx_row, g1, w_in1, qg, kg, hm, False, tm)
    bias_tab = _na_bias_table(od_rel_bias[0], _na_bias_index())
    a_l = _neighbourhood_attention(q_l, k_l, v_l, k_c, v_c, bias_tab)
    return _out_ffn("odd", (a_l,), x1, mod[1], lat_row, g2, w_out1, 1, *ffn, 2 * tm)
```

```python
import functools
import math

import numpy as np
import jax
import jax.numpy as jnp
from jax import lax
from jax.experimental import pallas as pl
from jax.experimental.pallas import tpu as pltpu

D_MODEL = 1024
BATCH = 4
SEQ = 4096
DEPTH = 2
GRID_W = 64
CTX_LEN = 256
HEAD_DIM = 64
EPS = 1e-6
NEG = -1e30
FOURIER_WIDTH = D_MODEL // 2
FOURIER_GROUPS = 4
FOURIER_GROUP_CH = FOURIER_WIDTH // FOURIER_GROUPS
WIN_Q_HEADS = (D_MODEL // 2) // HEAD_DIM
WIN_KV_HEADS = 2
WIN_GROUP = WIN_Q_HEADS // WIN_KV_HEADS
WIN_RADIUS = 128
WIN_BLOCK = 128
QW = WIN_Q_HEADS * HEAD_DIM
KW = WIN_KV_HEADS * HEAD_DIM
EV_IN_WIDTH = FOURIER_WIDTH + QW + 2 * KW
NA_HEADS = D_MODEL // HEAD_DIM
NA_KH = 8
NA_KW = 16
NA_WIDTH = NA_HEADS * HEAD_DIM
ROPE_THETA = 10000.0
ROPE_FREQS = HEAD_DIM // 4
D_FF = ((8 * D_MODEL // 3 + 255) // 256) * 256
GRID_ROWS = SEQ // GRID_W

LANES = 128
MOD_ROWS = 8
CTX_MOD_ROW = BATCH
FFT_N = 64
FF_CHUNK = 256
SOFTMAX_ROWS = 64
RMS_W = 256
SUB_ROWS = 512
VMEM_LIMIT = 60 * 1024 * 1024

LOG2E = math.log2(math.e)
QK_SCALE = LOG2E / math.sqrt(HEAD_DIM)

BF16 = jnp.bfloat16
F32 = jnp.float32

assert DEPTH == 2 and SEQ == FFT_N * FFT_N and D_FF % FF_CHUNK == 0


def _params(*sem):
    return pltpu.CompilerParams(dimension_semantics=sem, vmem_limit_bytes=VMEM_LIMIT)


def _dot(a, b):
    return jnp.dot(a, b, preferred_element_type=F32)


def _dot_nt(a, b):
    return lax.dot_general(a, b, (((1,), (1,)), ((), ())), preferred_element_type=F32)


def _silu(x):
    return x / (1.0 + jnp.exp(-x))


def _const_spec(shape):
    nd = len(shape)
    return pl.BlockSpec(shape, lambda *_: (0,) * nd, pipeline_mode=pl.Buffered(1))


def _dft_cos_sin(n):
    idx = (np.arange(n)[:, None] * np.arange(n)[None, :]) % n
    ang = 2.0 * np.pi * idx / n
    return np.cos(ang), np.sin(ang)


def _fourier_tables():
    cc, sc = _dft_cos_sin(FOURIER_GROUP_CH)
    wc = np.concatenate([cc, -sc], axis=1) / math.sqrt(FOURIER_GROUP_CH)
    c64, s64 = _dft_cos_sin(FFT_N)
    m1 = np.concatenate([c64, -s64], axis=0) / math.sqrt(FFT_N)
    m3 = np.concatenate([c64, s64], axis=1) / math.sqrt(FFT_N)
    tw = (np.arange(FFT_N)[:, None] * np.arange(FFT_N)[None, :]) % SEQ
    tw = 2.0 * np.pi * tw / SEQ
    twc = np.repeat(np.cos(tw)[:, :, None], LANES, axis=2)
    tws = np.repeat(np.sin(tw)[:, :, None], LANES, axis=2)
    cx, sx = _dft_cos_sin(CTX_LEN)
    mctx = np.concatenate([cx, sx], axis=1) / math.sqrt(CTX_LEN)
    as32 = lambda a: jnp.asarray(a, F32)
    return (as32(wc).astype(BF16), as32(m1).astype(BF16), as32(m3).astype(BF16),
            as32(twc), as32(tws), as32(mctx).astype(BF16))


def _head_mean_matrix():
    blk = np.kron(np.eye(RMS_W // HEAD_DIM), np.ones((HEAD_DIM, HEAD_DIM))) / HEAD_DIM
    return jnp.asarray(blk, BF16)


def _rope_tables():
    t = jnp.arange(SEQ, dtype=jnp.int32)
    row = (t // GRID_W).astype(F32)
    col = (t % GRID_W).astype(F32)
    inv = ROPE_THETA ** (-jnp.arange(ROPE_FREQS, dtype=F32) / ROPE_FREQS)
    ang_row = row[:, None] * inv[None, :]
    ang_col = col[:, None] * inv[None, :]
    zero = jnp.zeros_like(ang_row)
    cos = jnp.concatenate([jnp.cos(ang_row)] * 2 + [jnp.cos(ang_col)] * 2, axis=1)
    sin_hi = jnp.concatenate([-jnp.sin(ang_row), zero, -jnp.sin(ang_col), zero], axis=1)
    sin_lo = jnp.concatenate([zero, jnp.sin(ang_row), zero, jnp.sin(ang_col)], axis=1)
    rep = LANES // HEAD_DIM
    return jnp.tile(cos, (1, rep)), jnp.tile(sin_hi, (1, rep)), jnp.tile(sin_lo, (1, rep))


def _na_bias_index():
    cq = np.arange(GRID_W)
    dc = np.clip(cq[None, :] - cq[:, None] + NA_KW - 1, 0, 2 * NA_KW - 2)
    return jnp.asarray(np.concatenate([dc, dc + LANES // 2], axis=1), jnp.int32)


def _mod_kernel(cs_ref, w_ref, b_ref, o_ref):
    s = _silu(cs_ref[...]).astype(BF16)
    o_ref[0] = _dot(s, w_ref[0].astype(BF16)) + b_ref[0]


def _modulation(cs, ada_w, ada_b):
    tn = 1536
    return pl.pallas_call(
        _mod_kernel,
        grid=(DEPTH, 6 * D_MODEL // tn),
        in_specs=[pl.BlockSpec((MOD_ROWS, D_MODEL), lambda i, j: (0, 0)),
                  pl.BlockSpec((1, D_MODEL, tn), lambda i, j: (i, 0, j)),
                  pl.BlockSpec((1, 1, tn), lambda i, j: (i, 0, j))],
        out_specs=pl.BlockSpec((1, MOD_ROWS, tn), lambda i, j: (i, 0, j)),
        out_shape=jax.ShapeDtypeStruct((DEPTH, MOD_ROWS, 6 * D_MODEL), F32),
        compiler_params=_params("arbitrary", "arbitrary"),
        name="ada_modulation",
    )(cs, ada_w, ada_b.reshape(DEPTH, 1, 6 * D_MODEL))


def _mod_slice(mod_ref, k):
    return mod_ref[0, :, k * D_MODEL:(k + 1) * D_MODEL]


def _rms_mod(x, g, scale, shift):
    y = x * lax.rsqrt(jnp.mean(x * x, axis=-1, keepdims=True) + EPS)
    return (y * g) * (1.0 + scale) + shift


def _head_rms(t, gain_ref, hm_ref):
    w = t.shape[1]
    ms = _dot((t * t).astype(BF16), hm_ref[:w, :w])
    return t * lax.rsqrt(ms + EPS) * gain_ref[:, :w]


def _rope(t, cos, sin_hi, sin_lo):
    up = pltpu.roll(t, LANES - ROPE_FREQS, axis=1)
    dn = pltpu.roll(t, ROPE_FREQS, axis=1)
    return t * cos + up * sin_hi + dn * sin_lo


def _in_even_kernel(latent, x_ref, mod_ref, g_ref, w_ref, qg_ref, kg_ref, hm_ref, wc_ref, *rest):
    if latent:
        cos_ref, sh_ref, sl_ref, y_ref, q_ref, k_ref, v_ref = rest
    else:
        y_ref, q_ref, k_ref, v_ref = rest
    F = FOURIER_WIDTH
    lo_lanes = lax.broadcasted_iota(jnp.int32, (1, LANES), 1) < HEAD_DIM

    for r in range(x_ref.shape[1] // SUB_ROWS):
        rows = slice(r * SUB_ROWS, (r + 1) * SUB_ROWS)
        h = _rms_mod(x_ref[0, rows], g_ref[...], _mod_slice(mod_ref, 1), _mod_slice(mod_ref, 0)).astype(BF16)
        f = _dot(h, w_ref[:, :F]).astype(BF16)
        for g in range(FOURIER_GROUPS):
            yg = _dot(f[:, g * LANES:(g + 1) * LANES], wc_ref[...])
            y_ref[0, rows, g * LANES:(g + 1) * LANES] = yg[:, :LANES].astype(BF16)
            y_ref[0, rows, F + g * LANES:F + (g + 1) * LANES] = yg[:, LANES:].astype(BF16)

        def qk(t, gain_ref):
            t = _head_rms(t, gain_ref, hm_ref)
            if latent:
                t = jnp.concatenate([_rope(t[:, i * LANES:(i + 1) * LANES], cos_ref[rows], sh_ref[rows], sl_ref[rows])
                                     for i in range(t.shape[1] // LANES)], axis=1)
            return t

        q = _dot(h, w_ref[:, F:F + QW])
        for j in range(QW // RMS_W):
            qj = qk(q[:, j * RMS_W:(j + 1) * RMS_W], qg_ref) * QK_SCALE
            q_ref[0, rows, j * RMS_W:(j + 1) * RMS_W] = qj.astype(BF16)
        kv = _dot(h, w_ref[:, F + QW:])

        def store_dup(t, o_ref):
            sw = pltpu.roll(t, HEAD_DIM, axis=1)
            o_ref[0, 0, rows] = jnp.where(lo_lanes, t, sw).astype(BF16)
            o_ref[0, 1, rows] = jnp.where(lo_lanes, sw, t).astype(BF16)

        store_dup(qk(kv[:, :KW], kg_ref), k_ref)
        store_dup(kv[:, KW:], v_ref)


def _in_even(x, mod, mod_row, g, w_in, q_g, k_g, hm, wc, rope_tabs, tm):
    B, L, _ = x.shape
    latent = rope_tabs is not None
    in_specs = [pl.BlockSpec((1, tm, D_MODEL), lambda b, t: (b, t, 0)),
                pl.BlockSpec((1, 1, 6 * D_MODEL), lambda b, t: (mod_row(b), 0, 0)),
                _const_spec((1, D_MODEL)),
                _const_spec((D_MODEL, EV_IN_WIDTH)),
                _const_spec((1, RMS_W)), _const_spec((1, RMS_W)),
                _const_spec((RMS_W, RMS_W)), _const_spec((LANES, 2 * LANES))]
    args = [x, mod, g, w_in, q_g, k_g, hm, wc]
    tok = lambda w: pl.BlockSpec((1, tm, w), lambda b, t: (b, t, 0))
    kv_spec = pl.BlockSpec((1, WIN_KV_HEADS, tm, LANES), lambda b, t: (b, 0, t, 0))
    if latent:
        in_specs += [pl.BlockSpec((tm, LANES), lambda b, t: (t, 0))] * 3
        args += list(rope_tabs)
    return pl.pallas_call(
        functools.partial(_in_even_kernel, latent),
        grid=(B, L // tm),
        in_specs=in_specs,
        out_specs=[tok(2 * FOURIER_WIDTH), tok(QW), kv_spec, kv_spec],
        out_shape=[jax.ShapeDtypeStruct((B, L, 2 * FOURIER_WIDTH), BF16),
                   jax.ShapeDtypeStruct((B, L, QW), BF16),
                   jax.ShapeDtypeStruct((B, WIN_KV_HEADS, L, LANES), BF16),
                   jax.ShapeDtypeStruct((B, WIN_KV_HEADS, L, LANES), BF16)],
        compiler_params=_params("arbitrary", "arbitrary"),
        name="in_even_latent" if latent else "in_even_ctx",
    )(*args)


def _fft_stage1_kernel(n2t, y_ref, m1_ref, twc_ref, tws_ref, o_ref):
    F = FOURIER_WIDTH
    for j in range(n2t):
        a = _dot(m1_ref[...], y_ref[0, :, j * 2 * F:(j + 1) * 2 * F])
        top, bot = a[:FFT_N], a[FFT_N:]
        ar = top[:, :F] - bot[:, F:]
        ai = top[:, F:] + bot[:, :F]
        tc = jnp.tile(twc_ref[j], (1, F // LANES))
        ts = jnp.tile(tws_ref[j], (1, F // LANES))
        o_ref[0, 0, :, j * F:(j + 1) * F] = (ar * tc + ai * ts).astype(BF16)
        o_ref[0, 1, :, j * F:(j + 1) * F] = (ai * tc - ar * ts).astype(BF16)


def _fft_stage2_kernel(k1t, b_ref, m3_ref, o_ref):
    for j in range(k1t):
        rhs = jnp.concatenate([b_ref[0, 0, j], b_ref[0, 1, j]], axis=0)
        o_ref[0, j] = _dot(m3_ref[...], rhs)


def _fourier_latent(y, m1, m3, twc, tws):
    B = y.shape[0]
    F = FOURIER_WIDTH
    n2t = 16
    yv = y.reshape(B, FFT_N, FFT_N * 2 * F)
    bh = pl.pallas_call(
        functools.partial(_fft_stage1_kernel, n2t),
        grid=(B, FFT_N // n2t),
        in_specs=[pl.BlockSpec((1, FFT_N, n2t * 2 * F), lambda b, t: (b, 0, t)),
                  _const_spec((2 * FFT_N, FFT_N)),
                  pl.BlockSpec((n2t, FFT_N, LANES), lambda b, t: (t, 0, 0)),
                  pl.BlockSpec((n2t, FFT_N, LANES), lambda b, t: (t, 0, 0))],
        out_specs=pl.BlockSpec((1, 2, FFT_N, n2t * F), lambda b, t: (b, 0, 0, t)),
        out_shape=jax.ShapeDtypeStruct((B, 2, FFT_N, FFT_N * F), BF16),
        compiler_params=_params("arbitrary", "arbitrary"),
        name="fft_stage1",
    )(yv, m1, twc, tws)
    k1t = 16
    bv = bh.reshape(B, 2, FFT_N, FFT_N, F)
    return pl.pallas_call(
        functools.partial(_fft_stage2_kernel, k1t),
        grid=(B, FFT_N // k1t),
        in_specs=[pl.BlockSpec((1, 2, k1t, FFT_N, F), lambda b, t: (b, 0, t, 0, 0)),
                  _const_spec((FFT_N, 2 * FFT_N))],
        out_specs=pl.BlockSpec((1, k1t, FFT_N, F), lambda b, t: (b, t, 0, 0)),
        out_shape=jax.ShapeDtypeStruct((B, FFT_N, FFT_N, F), F32),
        compiler_params=_params("arbitrary", "arbitrary"),
        name="fft_stage2",
    )(bv, m3)


def _fourier_ctx_kernel(y_ref, m_ref, o_ref):
    F = FOURIER_WIDTH
    rhs = jnp.concatenate([y_ref[0, :, :F], y_ref[0, :, F:]], axis=0)
    o_ref[0] = _dot(m_ref[...], rhs)


def _fourier_ctx(y, mctx):
    B = y.shape[0]
    return pl.pallas_call(
        _fourier_ctx_kernel,
        grid=(B,),
        in_specs=[pl.BlockSpec((1, CTX_LEN, 2 * FOURIER_WIDTH), lambda b: (b, 0, 0)),
                  _const_spec((CTX_LEN, 2 * CTX_LEN))],
        out_specs=pl.BlockSpec((1, CTX_LEN, FOURIER_WIDTH), lambda b: (b, 0, 0)),
        out_shape=jax.ShapeDtypeStruct((B, CTX_LEN, FOURIER_WIDTH), F32),
        compiler_params=_params("arbitrary"),
        name="fourier_ctx",
    )(y, mctx)


def _gqa_group(q_slabs, keys, vals, masks, sinks, rows, o_ref, first_slab, scratch):
    s_ref, p_ref, inv_ref = scratch
    _gqa_scores(q_slabs, keys, rows, s_ref)
    _gqa_attend(vals, masks, sinks, rows, lambda sl: o_ref.at[0, :, sl], first_slab, scratch)


def _gqa_scores(q_slabs, keys, rows, s_ref):
    lo_lanes = lax.broadcasted_iota(jnp.int32, (rows, LANES), 1) < HEAD_DIM
    zero = jnp.zeros((rows, LANES), BF16)
    qs = jnp.concatenate([jnp.where(lo_lanes if half == 0 else ~lo_lanes, qslab, zero)
                          for qslab in q_slabs for half in range(2)], axis=0)
    s_ref[...] = _dot_nt(qs, keys)


def _gqa_attend(vals, masks, sinks, rows, out_slab, first_slab, scratch):
    s_ref, p_ref, inv_ref = scratch
    lo_lanes = lax.broadcasted_iota(jnp.int32, (rows, LANES), 1) < HEAD_DIM
    for c in range(WIN_GROUP * rows // SOFTMAX_ROWS):
        rs = slice(c * SOFTMAX_ROWS, (c + 1) * SOFTMAX_ROWS)
        s = s_ref[rs, :]
        if masks is not None:
            mask, w = masks
            r0 = (c * SOFTMAX_ROWS) % rows
            s = jnp.concatenate([jnp.where(mask[r0:r0 + SOFTMAX_ROWS], s[:, :w], NEG), s[:, w:]], axis=1)
        sk = sinks[(c * SOFTMAX_ROWS) // rows]
        m = jnp.maximum(jnp.max(s, axis=-1, keepdims=True), sk)
        e = jnp.exp2(s - m)
        inv_ref[rs, :] = 1.0 / (jnp.sum(e, axis=-1, keepdims=True) + jnp.exp2(sk - m))
        p_ref[rs, :] = e.astype(BF16)
    o = _dot(p_ref[...], vals) * inv_ref[...]
    for i in range(WIN_GROUP // 2):
        a = o[(2 * i) * rows:(2 * i + 1) * rows]
        b = o[(2 * i + 1) * rows:(2 * i + 2) * rows]
        sl = slice((first_slab + i) * LANES, (first_slab + i + 1) * LANES)
        out_slab(sl)[...] = jnp.where(lo_lanes, a, b).astype(BF16)


def _gqa_scratch(rows, n_keys):
    one = [pltpu.VMEM((WIN_GROUP * rows, n_keys), F32), pltpu.VMEM((WIN_GROUP * rows, n_keys), BF16),
           pltpu.VMEM((WIN_GROUP * rows, 1), F32)]
    return one * WIN_KV_HEADS


def _group_sinks(sink_ref, kv):
    return [sink_ref[kv * WIN_GROUP + g] * LOG2E for g in range(WIN_GROUP)]


def _group_slab(kv, i):
    return slice((kv * (WIN_GROUP // 2) + i) * LANES, (kv * (WIN_GROUP // 2) + i + 1) * LANES)


def _window_bands():
    i = np.arange(WIN_BLOCK)[:, None]
    j = np.arange(3 * WIN_BLOCK)[None, :]
    bands = [np.abs(j - off - i) <= WIN_RADIUS for off in (0, WIN_BLOCK, 2 * WIN_BLOCK)]
    return jnp.asarray(np.stack(bands).astype(np.float32))


def _win_attn_kernel(blocks_per_step, sink_ref, band_ref, q_ref, k_ref, v_ref, kx_ref, vx_ref, o_ref,
                     s0_ref, s1_ref, p_ref, inv_ref):
    t = pl.program_id(1)
    nb = SEQ // WIN_BLOCK
    nw = 3 * WIN_BLOCK

    def place(j):
        n = t * blocks_per_step + j
        start = pl.multiple_of(jnp.clip((n - 1) * WIN_BLOCK, 0, SEQ - nw), WIN_BLOCK)
        sel = jnp.where(n == 0, 0, jnp.where(n == nb - 1, 2, 1))
        return pl.multiple_of(j * WIN_BLOCK, WIN_BLOCK), start, sel

    def scores(j, kv, s_ref):
        qoff, start, _ = place(j)
        keys = jnp.concatenate([k_ref[0, kv, pl.ds(start, nw), :], kx_ref[0, kv]], axis=0)
        q_slabs = [q_ref[0, pl.ds(qoff, WIN_BLOCK), _group_slab(kv, i)] for i in range(WIN_GROUP // 2)]
        _gqa_scores(q_slabs, keys, WIN_BLOCK, s_ref)

    def attend(j, kv, s_ref):
        qoff, start, sel = place(j)
        vals = jnp.concatenate([v_ref[0, kv, pl.ds(start, nw), :], vx_ref[0, kv]], axis=0)
        _gqa_attend(vals, (band_ref[sel] > 0.0, nw), _group_sinks(sink_ref, kv), WIN_BLOCK,
                    lambda sl: o_ref.at[0, pl.ds(qoff, WIN_BLOCK), sl], kv * (WIN_GROUP // 2),
                    (s_ref, p_ref, inv_ref))

    scores(0, 0, s0_ref)

    def one_block(j, carry):
        scores(j, 1, s1_ref)
        attend(j, 0, s0_ref)
        scores(jnp.minimum(j + 1, blocks_per_step - 1), 0, s0_ref)
        attend(j, 1, s1_ref)
        return carry

    lax.fori_loop(0, blocks_per_step, one_block, 0)


def _win_attention(q, k, v, kx, vx, sink):
    B, L, _ = q.shape
    blocks_per_step = 8
    rows = blocks_per_step * WIN_BLOCK
    n_keys = 3 * WIN_BLOCK + CTX_LEN
    full = pl.BlockSpec((1, WIN_KV_HEADS, L, LANES), lambda b, t: (b, 0, 0, 0))
    ctx_spec = pl.BlockSpec((1, WIN_KV_HEADS, CTX_LEN, LANES), lambda b, t: (0, 0, b, 0))
    tok = pl.BlockSpec((1, rows, QW), lambda b, t: (b, t, 0))
    return pl.pallas_call(
        functools.partial(_win_attn_kernel, blocks_per_step),
        grid=(B, L // rows),
        in_specs=[pl.BlockSpec(memory_space=pltpu.SMEM),
                  _const_spec((3, WIN_BLOCK, 3 * WIN_BLOCK)),
                  tok, full, full, ctx_spec, ctx_spec],
        out_specs=tok,
        out_shape=jax.ShapeDtypeStruct((B, L, QW), BF16),
        scratch_shapes=[pltpu.VMEM((WIN_GROUP * WIN_BLOCK, n_keys), F32), pltpu.VMEM((WIN_GROUP * WIN_BLOCK, n_keys), F32),
                        pltpu.VMEM((WIN_GROUP * WIN_BLOCK, n_keys), BF16), pltpu.VMEM((WIN_GROUP * WIN_BLOCK, 1), F32)],
        compiler_params=_params("arbitrary", "arbitrary"),
        name="window_attention",
    )(sink, _window_bands(), q, k, v, kx, vx)


def _ctx_attn_even_kernel(sink_ref, q_ref, k_ref, v_ref, o_ref, *scratch):
    for kv in range(WIN_KV_HEADS):
        q_slabs = [q_ref[0, :, _group_slab(kv, i)] for i in range(WIN_GROUP // 2)]
        _gqa_group(q_slabs, k_ref[0, kv], v_ref[0, kv], None, _group_sinks(sink_ref, kv), CTX_LEN, o_ref,
                   kv * (WIN_GROUP // 2), scratch[3 * kv:3 * kv + 3])


def _ctx_attention_even(q, k, v, sink):
    B = q.shape[0]
    kv_spec = pl.BlockSpec((1, WIN_KV_HEADS, CTX_LEN, LANES), lambda b: (0, 0, b, 0))
    return pl.pallas_call(
        _ctx_attn_even_kernel,
        grid=(B,),
        in_specs=[pl.BlockSpec(memory_space=pltpu.SMEM),
                  pl.BlockSpec((1, CTX_LEN, QW), lambda b: (b, 0, 0)),
                  kv_spec, kv_spec],
        out_specs=pl.BlockSpec((1, CTX_LEN, QW), lambda b: (b, 0, 0)),
        out_shape=jax.ShapeDtypeStruct((B, CTX_LEN, QW), BF16),
        scratch_shapes=_gqa_scratch(CTX_LEN, CTX_LEN),
        compiler_params=_params("arbitrary"),
        name="ctx_attention_even",
    )(sink, q, k, v)


def _out_ffn_kernel(mode, *refs):
    if mode == "odd":
        a_ref, x_ref, mod_ref, g_ref, wo_ref, wg_ref, wu_ref, wd_ref, o_ref = refs
    else:
        f_ref, a_ref, x_ref, mod_ref, g_ref, wo_ref, wg_ref, wu_ref, wd_ref, o_ref = refs
    for r in range(x_ref.shape[1] // SUB_ROWS):
        rows = slice(r * SUB_ROWS, (r + 1) * SUB_ROWS)
        if mode == "even_latent":
            planes = range(r * SUB_ROWS // FFT_N, (r + 1) * SUB_ROWS // FFT_N)
            fm = jnp.concatenate([f_ref[0, :, j, :] for j in planes], axis=0).astype(BF16)
            o = _dot(fm, wo_ref[:FOURIER_WIDTH]) + _dot(a_ref[0, rows], wo_ref[FOURIER_WIDTH:])
        elif mode == "even_ctx":
            o = _dot(f_ref[0, rows].astype(BF16), wo_ref[:FOURIER_WIDTH]) + _dot(a_ref[0, rows], wo_ref[FOURIER_WIDTH:])
        else:
            o = _dot(jnp.concatenate([a_ref[0, j, rows] for j in range(a_ref.shape[1])], axis=1), wo_ref[...])
        x1 = x_ref[0, rows] + _mod_slice(mod_ref, 2) * o
        h = _rms_mod(x1, g_ref[...], _mod_slice(mod_ref, 4), _mod_slice(mod_ref, 3)).astype(BF16)
        acc = jnp.zeros_like(x1)
        for c in range(D_FF // FF_CHUNK):
            cs = slice(c * FF_CHUNK, (c + 1) * FF_CHUNK)
            a = _silu(_dot(h, wg_ref[0, :, cs])) * _dot(h, wu_ref[0, :, cs])
            acc = acc + _dot(a.astype(BF16), wd_ref[0, cs, :])
        o_ref[0, rows] = x1 + _mod_slice(mod_ref, 5) * acc


def _out_ffn(mode, mix, x, mod, mod_row, g, w_out, layer, wg, wu, wd, tm):
    B, L, _ = x.shape
    tok = lambda w: pl.BlockSpec((1, tm, w), lambda b, t: (b, t, 0))
    ffn_spec = lambda r, c: pl.BlockSpec((1, r, c), lambda b, t: (layer, 0, 0), pipeline_mode=pl.Buffered(1))
    if mode == "even_latent":
        assert tm % FFT_N == 0
        mix_specs = [pl.BlockSpec((1, FFT_N, tm // FFT_N, FOURIER_WIDTH), lambda b, t: (b, 0, t, 0)), tok(QW)]
    elif mode == "even_ctx":
        mix_specs = [tok(FOURIER_WIDTH), tok(QW)]
    else:
        mix_specs = [pl.BlockSpec((1, NA_WIDTH // LANES, tm, LANES), lambda b, t: (b, 0, t, 0))]
    return pl.pallas_call(
        functools.partial(_out_ffn_kernel, mode),
        grid=(B, L // tm),
        in_specs=mix_specs + [tok(D_MODEL),
                              pl.BlockSpec((1, 1, 6 * D_MODEL), lambda b, t: (mod_row(b), 0, 0)),
                              _const_spec((1, D_MODEL)),
                              _const_spec((D_MODEL, D_MODEL)),
                              ffn_spec(D_MODEL, D_FF), ffn_spec(D_MODEL, D_FF), ffn_spec(D_FF, D_MODEL)],
        out_specs=tok(D_MODEL),
        out_shape=jax.ShapeDtypeStruct((B, L, D_MODEL), F32),
        compiler_params=_params("arbitrary", "arbitrary"),
        name="out_ffn_" + mode,
    )(*mix, x, mod, g, w_out, wg, wu, wd)


def _in_odd_kernel(with_q, x_ref, mod_ref, g_ref, w_ref, qg_ref, kg_ref, hm_ref, *outs):
    W = NA_WIDTH
    slabs = RMS_W // LANES
    if with_q:
        q_ref, k_ref, v_ref = outs
    else:
        k_ref, v_ref = outs

    for r in range(x_ref.shape[1] // SUB_ROWS):
        rows = slice(r * SUB_ROWS, (r + 1) * SUB_ROWS)
        h = _rms_mod(x_ref[0, rows], g_ref[...], _mod_slice(mod_ref, 1), _mod_slice(mod_ref, 0)).astype(BF16)

        def normed(t, gain_ref, scale, o_ref):
            for j in range(W // RMS_W):
                tj = (_head_rms(t[:, j * RMS_W:(j + 1) * RMS_W], gain_ref, hm_ref) * scale).astype(BF16)
                for i in range(slabs):
                    o_ref[0, j * slabs + i, rows] = tj[:, i * LANES:(i + 1) * LANES]

        if with_q:
            normed(_dot(h, w_ref[:, :W]), qg_ref, QK_SCALE, q_ref)
        normed(_dot(h, w_ref[:, W:2 * W]), kg_ref, 1.0, k_ref)
        v = _dot(h, w_ref[:, 2 * W:]).astype(BF16)
        for j in range(W // LANES):
            v_ref[0, j, rows] = v[:, j * LANES:(j + 1) * LANES]


def _in_odd(x, mod, mod_row, g, w_in, q_g, k_g, hm, with_q, tm):
    B, L, _ = x.shape
    pairs = NA_WIDTH // LANES
    tok = pl.BlockSpec((1, pairs, tm, LANES), lambda b, t: (b, 0, t, 0))
    n_out = 3 if with_q else 2
    return pl.pallas_call(
        functools.partial(_in_odd_kernel, with_q),
        grid=(B, L // tm),
        in_specs=[pl.BlockSpec((1, tm, D_MODEL), lambda b, t: (b, t, 0)),
                  pl.BlockSpec((1, 1, 6 * D_MODEL), lambda b, t: (mod_row(b), 0, 0)),
                  _const_spec((1, D_MODEL)),
                  _const_spec((D_MODEL, 3 * NA_WIDTH)),
                  _const_spec((1, RMS_W)), _const_spec((1, RMS_W)), _const_spec((RMS_W, RMS_W))],
        out_specs=[tok] * n_out,
        out_shape=[jax.ShapeDtypeStruct((B, pairs, L, LANES), BF16)] * n_out,
        compiler_params=_params("arbitrary", "arbitrary"),
        name="in_odd" if with_q else "in_odd_ctx",
    )(x, mod, g, w_in, q_g, k_g, hm)


def _bias_table_kernel(rb_ref, idx_ref, o_ref):
    idx = idx_ref[...]
    cq = lax.broadcasted_iota(jnp.int32, idx.shape, 0)
    ck = lax.broadcasted_iota(jnp.int32, idx.shape, 1) % GRID_W
    c0 = jnp.clip(cq - NA_KW // 2, 0, GRID_W - NA_KW)
    inside = (ck >= c0) & (ck < c0 + NA_KW)
    for h in range(o_ref.shape[0]):
        for dr in range(o_ref.shape[1]):
            row = jnp.broadcast_to(rb_ref[h, dr:dr + 1, :], idx.shape)
            o_ref[h, dr] = jnp.where(inside, jnp.take_along_axis(row, idx, axis=1) * LOG2E, -jnp.inf)


def _na_bias_table(rel_bias, idx):
    H = NA_HEADS
    npair = 2 * NA_KH - 2
    half = LANES // 2
    hb = 4
    pad = lambda a: jnp.pad(a, ((0, 0), (0, 0), (0, half - a.shape[-1])))
    rb2 = jnp.concatenate([pad(rel_bias[:, :-1]), pad(rel_bias[:, 1:])], axis=-1)
    return pl.pallas_call(
        _bias_table_kernel,
        grid=(H // hb,),
        in_specs=[pl.BlockSpec((hb, npair, LANES), lambda h: (h, 0, 0)), _const_spec((GRID_W, LANES))],
        out_specs=pl.BlockSpec((hb, npair, GRID_W, LANES), lambda h: (h, 0, 0, 0)),
        out_shape=jax.ShapeDtypeStruct((H, npair, GRID_W, LANES), F32),
        compiler_params=_params("arbitrary"),
        name="na_bias_table",
    )(rb2, idx)


def _na_kernel(rows_per_step, q_ref, k_ref, v_ref, kx_ref, vx_ref, bias_ref, o_ref, s0_ref, s1_ref, p_ref, inv_ref):
    t = pl.program_id(1)
    n = NA_KH * GRID_W
    lo_lanes = lax.broadcasted_iota(jnp.int32, (GRID_W, LANES), 1) < HEAD_DIM

    def window(i):
        r = t * rows_per_step + i
        r0 = jnp.clip(r - NA_KH // 2, 0, GRID_ROWS - NA_KH)
        return r0 - r + NA_KH - 1, pl.multiple_of(r0 * GRID_W, GRID_W), pl.multiple_of(i * GRID_W, GRID_W)

    def scores(i, s_ref):
        dr0, start, qoff = window(i)
        for hp in range(NA_HEADS // 2):
            qs = q_ref[0, hp, pl.ds(qoff, GRID_W), :]
            keys = jnp.concatenate([k_ref[0, hp, pl.ds(start, n), :], kx_ref[0, hp]], axis=0)
            qm = jnp.concatenate([jnp.where(lo_lanes, qs, jnp.zeros_like(qs)),
                                  jnp.where(lo_lanes, jnp.zeros_like(qs), qs)], axis=0)
            s = _dot_nt(qm, keys)
            for half in range(2):
                h = 2 * hp + half
                bias = jnp.concatenate([bias_ref[h, dr0 + 2 * p] for p in range(NA_KH // 2)], axis=1)
                rs = slice(h * GRID_W, (h + 1) * GRID_W)
                hr = slice(half * GRID_W, (half + 1) * GRID_W)
                s_ref[rs, :n] = s[hr, :n] + bias
                s_ref[rs, n:] = s[hr, n:]

    def attend(i, s_ref):
        _, start, qoff = window(i)
        for c in range(NA_HEADS * GRID_W // SOFTMAX_ROWS):
            cs = slice(c * SOFTMAX_ROWS, (c + 1) * SOFTMAX_ROWS)
            s = s_ref[cs, :]
            e = jnp.exp2(s - jnp.max(s, axis=-1, keepdims=True))
            inv_ref[cs, :] = 1.0 / jnp.sum(e, axis=-1, keepdims=True)
            p_ref[cs, :] = e.astype(BF16)
        for hp in range(NA_HEADS // 2):
            vals = jnp.concatenate([v_ref[0, hp, pl.ds(start, n), :], vx_ref[0, hp]], axis=0)
            rs = slice(2 * hp * GRID_W, (2 * hp + 2) * GRID_W)
            res = _dot(p_ref[rs, :], vals) * inv_ref[rs, :]
            o_ref[0, hp, pl.ds(qoff, GRID_W), :] = jnp.where(lo_lanes, res[:GRID_W], res[GRID_W:]).astype(BF16)

    last = rows_per_step - 1
    scores(0, s0_ref)

    def two_rows(j, carry):
        scores(2 * j + 1, s1_ref)
        attend(2 * j, s0_ref)
        scores(jnp.minimum(2 * j + 2, last), s0_ref)
        attend(2 * j + 1, s1_ref)
        return carry

    lax.fori_loop(0, rows_per_step // 2, two_rows, 0)


def _neighbourhood_attention(q, k, v, kx, vx, bias_tab):
    B, pairs, L, _ = q.shape
    rows_per_step = 16
    full = pl.BlockSpec((1, pairs, L, LANES), lambda b, t: (b, 0, 0, 0))
    ctx_spec = pl.BlockSpec((1, pairs, CTX_LEN, LANES), lambda b, t: (0, 0, b, 0))
    q_rows = pl.BlockSpec((1, pairs, rows_per_step * GRID_W, LANES), lambda b, t: (b, 0, t, 0))
    n_keys = NA_KH * GRID_W + CTX_LEN
    all_rows = NA_HEADS * GRID_W
    return pl.pallas_call(
        functools.partial(_na_kernel, rows_per_step),
        grid=(B, GRID_ROWS // rows_per_step),
        in_specs=[q_rows, full, full, ctx_spec, ctx_spec, _const_spec(bias_tab.shape)],
        out_specs=q_rows,
        out_shape=jax.ShapeDtypeStruct((B, pairs, L, LANES), BF16),
        scratch_shapes=[pltpu.VMEM((all_rows, n_keys), F32), pltpu.VMEM((all_rows, n_keys), F32),
                        pltpu.VMEM((all_rows, n_keys), BF16), pltpu.VMEM((all_rows, 1), F32)],
        compiler_params=_params("arbitrary", "arbitrary"),
        name="neighbourhood_attention",
    )(q, k, v, kx, vx, bias_tab)


def kernel(x, c, ctx, c_ctx, ada_w, ada_b, norm1_g, norm2_g, ffn_w_gate, ffn_w_up, ffn_w_down,
           ev_w_in, ev_w_out, ev_q_norm, ev_k_norm, ev_sink,
           od_w_in, od_w_out, od_q_norm, od_k_norm, od_rel_bias):
    assert x.shape == (BATCH, SEQ, D_MODEL) and ctx.shape == (BATCH, CTX_LEN, D_MODEL)
    wc, m1, m3, twc, tws, mctx = _fourier_tables()
    hm = _head_mean_matrix()
    rope_tabs = _rope_tables()
    lane_gain = lambda gvec: jnp.tile(gvec, RMS_W // HEAD_DIM).reshape(1, RMS_W)
    lat_row = lambda b: b
    ctx_row = lambda b: CTX_MOD_ROW
    tm = 512
    n_ctx = BATCH * CTX_LEN
    as_seq = lambda a: a.reshape(1, n_ctx, a.shape[-1])
    per_batch = lambda a: a.reshape(BATCH, CTX_LEN, a.shape[-1])

    cs = jnp.concatenate([c, c_ctx[None, :], jnp.zeros((MOD_ROWS - BATCH - 1, D_MODEL), F32)], axis=0)
    mod = _modulation(cs, ada_w, ada_b).reshape(DEPTH, MOD_ROWS, 1, 6 * D_MODEL)

    w_in0 = ev_w_in[0].astype(BF16)
    w_out0 = ev_w_out[0].astype(BF16)
    g1 = norm1_g[0].reshape(1, D_MODEL)
    g2 = norm2_g[0].reshape(1, D_MODEL)
    ffn = (ffn_w_gate.astype(BF16), ffn_w_up.astype(BF16), ffn_w_down.astype(BF16))
    qg, kg = lane_gain(ev_q_norm[0]), lane_gain(ev_k_norm[0])
    y_l, q_l, k_l, v_l = _in_even(x, mod[0], lat_row, g1, w_in0, qg, kg, hm, wc, rope_tabs, 2 * tm)
    y_c, q_c, k_c, v_c = _in_even(as_seq(ctx), mod[0], ctx_row, g1, w_in0, qg, kg, hm, wc, None, tm)
    y_c, q_c = per_batch(y_c), per_batch(q_c)
    f_l = _fourier_latent(y_l, m1, m3, twc, tws)
    f_c = _fourier_ctx(y_c, mctx)
    a_l = _win_attention(q_l, k_l, v_l, k_c, v_c, ev_sink[0])
    a_c = _ctx_attention_even(q_c, k_c, v_c, ev_sink[0])
    x1 = _out_ffn("even_latent", (f_l, a_l), x, mod[0], lat_row, g2, w_out0, 0, *ffn, 2 * tm)
    y1 = _out_ffn("even_ctx", (as_seq(f_c), as_seq(a_c)), as_seq(ctx), mod[0], ctx_row, g2, w_out0, 0, *ffn, tm)

    w_in1 = od_w_in[0].astype(BF16)
    w_out1 = od_w_out[0].astype(BF16)
    g1 = norm1_g[1].reshape(1, D_MODEL)
    g2 = norm2_g[1].reshape(1, D_MODEL)
    qg, kg = lane_gain(od_q_norm[0]), lane_gain(od_k_norm[0])
    q_l, k_l, v_l = _in_odd(x1, mod[1], lat_row, g1, w_in1, qg, kg, hm, True, 2 * tm)
    k_c, v_c = _in_odd(y1, mod[1], ctx_row, g1, w_in1, qg, kg, hm, False, tm)
    bias_tab = _na_bias_table(od_rel_bias[0], _na_bias_index())
    a_l = _neighbourhood_attention(q_l, k_l, v_l, k_c, v_c, bias_tab)
    return _out_ffn("odd", (a_l,), x1, mod[1], lat_row, g2, w_out1, 1, *ffn, 2 * tm)
```

```python
import functools
import math

import numpy as np
import jax
import jax.numpy as jnp
from jax import lax
from jax.experimental import pallas as pl
from jax.experimental.pallas import tpu as pltpu

D_MODEL = 1024
BATCH = 4
SEQ = 4096
DEPTH = 2
GRID_W = 64
CTX_LEN = 256
HEAD_DIM = 64
EPS = 1e-6
FOURIER_WIDTH = D_MODEL // 2
FOURIER_GROUPS = 4
FOURIER_GROUP_CH = FOURIER_WIDTH // FOURIER_GROUPS
WIN_Q_HEADS = (D_MODEL // 2) // HEAD_DIM
WIN_KV_HEADS = 2
WIN_GROUP = WIN_Q_HEADS // WIN_KV_HEADS
WIN_RADIUS = 128
WIN_BLOCK = 128
QW = WIN_Q_HEADS * HEAD_DIM
KW = WIN_KV_HEADS * HEAD_DIM
EV_IN_WIDTH = FOURIER_WIDTH + QW + 2 * KW
NA_HEADS = D_MODEL // HEAD_DIM
NA_KH = 8
NA_KW = 16
NA_WIDTH = NA_HEADS * HEAD_DIM
ROPE_THETA = 10000.0
ROPE_FREQS = HEAD_DIM // 4
D_FF = ((8 * D_MODEL // 3 + 255) // 256) * 256
GRID_ROWS = SEQ // GRID_W

LANES = 128
MOD_ROWS = 8
CTX_MOD_ROW = BATCH
FFT_N = 64
FF_CHUNK = 256
SOFTMAX_ROWS = 64
RMS_W = 256
SUB_ROWS = 512
VMEM_LIMIT = 60 * 1024 * 1024

LOG2E = math.log2(math.e)
QK_SCALE = LOG2E / math.sqrt(HEAD_DIM)

BF16 = jnp.bfloat16
F32 = jnp.float32

assert DEPTH == 2 and SEQ == FFT_N * FFT_N and D_FF % FF_CHUNK == 0


def _params(*sem):
    return pltpu.CompilerParams(dimension_semantics=sem, vmem_limit_bytes=VMEM_LIMIT)


def _dot(a, b):
    return jnp.dot(a, b, preferred_element_type=F32)


def _dot_nt(a, b):
    return lax.dot_general(a, b, (((1,), (1,)), ((), ())), preferred_element_type=F32)


def _silu(x):
    return x / (1.0 + jnp.exp(-x))


def _const_spec(shape):
    nd = len(shape)
    return pl.BlockSpec(shape, lambda *_: (0,) * nd, pipeline_mode=pl.Buffered(1))


def _dft_cos_sin(n):
    idx = (np.arange(n)[:, None] * np.arange(n)[None, :]) % n
    ang = 2.0 * np.pi * idx / n
    return np.cos(ang), np.sin(ang)


def _fourier_tables():
    cc, sc = _dft_cos_sin(FOURIER_GROUP_CH)
    wc = np.concatenate([cc, -sc], axis=1) / math.sqrt(FOURIER_GROUP_CH)
    c64, s64 = _dft_cos_sin(FFT_N)
    m1 = np.concatenate([c64, -s64], axis=0) / math.sqrt(FFT_N)
    m3 = np.concatenate([c64, s64], axis=1) / math.sqrt(FFT_N)
    tw = (np.arange(FFT_N)[:, None] * np.arange(FFT_N)[None, :]) % SEQ
    tw = 2.0 * np.pi * tw / SEQ
    twc = np.repeat(np.cos(tw)[:, :, None], LANES, axis=2)
    tws = np.repeat(np.sin(tw)[:, :, None], LANES, axis=2)
    cx, sx = _dft_cos_sin(CTX_LEN)
    mctx = np.concatenate([cx, sx], axis=1) / math.sqrt(CTX_LEN)
    as32 = lambda a: jnp.asarray(a, F32)
    return (as32(wc).astype(BF16), as32(m1).astype(BF16), as32(m3).astype(BF16),
            as32(twc), as32(tws), as32(mctx).astype(BF16))


def _head_mean_matrix():
    blk = np.kron(np.eye(RMS_W // HEAD_DIM), np.ones((HEAD_DIM, HEAD_DIM))) / HEAD_DIM
    return jnp.asarray(blk, BF16)


def _rope_tables():
    t = jnp.arange(SEQ, dtype=jnp.int32)
    row = (t // GRID_W).astype(F32)
    col = (t % GRID_W).astype(F32)
    inv = ROPE_THETA ** (-jnp.arange(ROPE_FREQS, dtype=F32) / ROPE_FREQS)
    ang_row = row[:, None] * inv[None, :]
    ang_col = col[:, None] * inv[None, :]
    zero = jnp.zeros_like(ang_row)
    cos = jnp.concatenate([jnp.cos(ang_row)] * 2 + [jnp.cos(ang_col)] * 2, axis=1)
    sin_hi = jnp.concatenate([-jnp.sin(ang_row), zero, -jnp.sin(ang_col), zero], axis=1)
    sin_lo = jnp.concatenate([zero, jnp.sin(ang_row), zero, jnp.sin(ang_col)], axis=1)
    rep = LANES // HEAD_DIM
    return jnp.tile(cos, (1, rep)), jnp.tile(sin_hi, (1, rep)), jnp.tile(sin_lo, (1, rep))


def _na_bias_index():
    cq = np.arange(GRID_W)
    dc = np.clip(cq[None, :] - cq[:, None] + NA_KW - 1, 0, 2 * NA_KW - 2)
    return jnp.asarray(np.concatenate([dc, dc + LANES // 2], axis=1), jnp.int32)


def _mod_kernel(cs_ref, w_ref, b_ref, o_ref):
    s = _silu(cs_ref[...]).astype(BF16)
    o_ref[0] = _dot(s, w_ref[0].astype(BF16)) + b_ref[0]


def _modulation(cs, ada_w, ada_b):
    tn = 1536
    return pl.pallas_call(
        _mod_kernel,
        grid=(DEPTH, 6 * D_MODEL // tn),
        in_specs=[pl.BlockSpec((MOD_ROWS, D_MODEL), lambda i, j: (0, 0)),
                  pl.BlockSpec((1, D_MODEL, tn), lambda i, j: (i, 0, j)),
                  pl.BlockSpec((1, 1, tn), lambda i, j: (i, 0, j))],
        out_specs=pl.BlockSpec((1, MOD_ROWS, tn), lambda i, j: (i, 0, j)),
        out_shape=jax.ShapeDtypeStruct((DEPTH, MOD_ROWS, 6 * D_MODEL), F32),
        compiler_params=_params("arbitrary", "arbitrary"),
        name="ada_modulation",
    )(cs, ada_w, ada_b.reshape(DEPTH, 1, 6 * D_MODEL))


def _mod_slice(mod_ref, k):
    return mod_ref[0, :, k * D_MODEL:(k + 1) * D_MODEL]


def _rms_mod(x, g, scale, shift):
    y = x * lax.rsqrt(jnp.mean(x * x, axis=-1, keepdims=True) + EPS)
    return (y * g) * (1.0 + scale) + shift


def _head_rms(t, gain_ref, hm_ref):
    w = t.shape[1]
    ms = _dot((t * t).astype(BF16), hm_ref[:w, :w])
    return t * lax.rsqrt(ms + EPS) * gain_ref[:, :w]


def _rope(t, cos, sin_hi, sin_lo):
    up = pltpu.roll(t, LANES - ROPE_FREQS, axis=1)
    dn = pltpu.roll(t, ROPE_FREQS, axis=1)
    return t * cos + up * sin_hi + dn * sin_lo


def _cast_once(w32_ref, w_ref):
    @pl.when((pl.program_id(0) == 0) & (pl.program_id(1) == 0))
    def _():
        w_ref[...] = w32_ref[...].astype(BF16)


def _in_even_kernel(latent, x_ref, mod_ref, g_ref, w32_ref, qg_ref, kg_ref, hm_ref, wc_ref, *rest):
    if latent:
        cos_ref, sh_ref, sl_ref, y_ref, q_ref, k_ref, v_ref, w_ref = rest
    else:
        y_ref, q_ref, k_ref, v_ref, w_ref = rest
    _cast_once(w32_ref, w_ref)
    F = FOURIER_WIDTH
    lo_lanes = lax.broadcasted_iota(jnp.int32, (1, LANES), 1) < HEAD_DIM

    for r in range(x_ref.shape[1] // SUB_ROWS):
        rows = slice(r * SUB_ROWS, (r + 1) * SUB_ROWS)
        h = _rms_mod(x_ref[0, rows], g_ref[...], _mod_slice(mod_ref, 1), _mod_slice(mod_ref, 0)).astype(BF16)
        f = _dot(h, w_ref[:, :F]).astype(BF16)
        for g in range(FOURIER_GROUPS):
            yg = _dot(f[:, g * LANES:(g + 1) * LANES], wc_ref[...])
            y_ref[0, rows, g * LANES:(g + 1) * LANES] = yg[:, :LANES].astype(BF16)
            y_ref[0, rows, F + g * LANES:F + (g + 1) * LANES] = yg[:, LANES:].astype(BF16)

        def qk(t, gain_ref):
            t = _head_rms(t, gain_ref, hm_ref)
            if latent:
                t = jnp.concatenate([_rope(t[:, i * LANES:(i + 1) * LANES], cos_ref[rows], sh_ref[rows], sl_ref[rows])
                                     for i in range(t.shape[1] // LANES)], axis=1)
            return t

        q = _dot(h, w_ref[:, F:F + QW])
        for j in range(QW // RMS_W):
            qj = qk(q[:, j * RMS_W:(j + 1) * RMS_W], qg_ref) * QK_SCALE
            q_ref[0, rows, j * RMS_W:(j + 1) * RMS_W] = qj.astype(BF16)
        kv = _dot(h, w_ref[:, F + QW:])

        def store_dup(t, o_ref):
            sw = pltpu.roll(t, HEAD_DIM, axis=1)
            o_ref[0, 0, rows] = jnp.where(lo_lanes, t, sw).astype(BF16)
            o_ref[0, 1, rows] = jnp.where(lo_lanes, sw, t).astype(BF16)

        store_dup(qk(kv[:, :KW], kg_ref), k_ref)
        store_dup(kv[:, KW:], v_ref)


def _in_even(x, mod, mod_row, g, w_in, q_g, k_g, hm, wc, rope_tabs, tm):
    B, L, _ = x.shape
    latent = rope_tabs is not None
    in_specs = [pl.BlockSpec((1, tm, D_MODEL), lambda b, t: (b, t, 0)),
                pl.BlockSpec((1, 1, 6 * D_MODEL), lambda b, t: (mod_row(b), 0, 0)),
                _const_spec((1, D_MODEL)),
                _const_spec((D_MODEL, EV_IN_WIDTH)),
                _const_spec((1, RMS_W)), _const_spec((1, RMS_W)),
                _const_spec((RMS_W, RMS_W)), _const_spec((LANES, 2 * LANES))]
    args = [x, mod, g, w_in, q_g, k_g, hm, wc]
    tok = lambda w: pl.BlockSpec((1, tm, w), lambda b, t: (b, t, 0))
    kv_spec = pl.BlockSpec((1, WIN_KV_HEADS, tm, LANES), lambda b, t: (b, 0, t, 0))
    if latent:
        in_specs += [pl.BlockSpec((tm, LANES), lambda b, t: (t, 0))] * 3
        args += list(rope_tabs)
    return pl.pallas_call(
        functools.partial(_in_even_kernel, latent),
        grid=(B, L // tm),
        in_specs=in_specs,
        out_specs=[tok(2 * FOURIER_WIDTH), tok(QW), kv_spec, kv_spec],
        out_shape=[jax.ShapeDtypeStruct((B, L, 2 * FOURIER_WIDTH), BF16),
                   jax.ShapeDtypeStruct((B, L, QW), BF16),
                   jax.ShapeDtypeStruct((B, WIN_KV_HEADS, L, LANES), BF16),
                   jax.ShapeDtypeStruct((B, WIN_KV_HEADS, L, LANES), BF16)],
        compiler_params=_params("arbitrary", "arbitrary"),
        scratch_shapes=[pltpu.VMEM((D_MODEL, EV_IN_WIDTH), BF16)],
        name="in_even_latent" if latent else "in_even_ctx",
    )(*args)


def _fft_stage1_kernel(n2t, y_ref, m1_ref, twc_ref, tws_ref, o_ref):
    F = FOURIER_WIDTH
    for j in range(n2t):
        a = _dot(m1_ref[...], y_ref[0, :, j * 2 * F:(j + 1) * 2 * F])
        top, bot = a[:FFT_N], a[FFT_N:]
        ar = top[:, :F] - bot[:, F:]
        ai = top[:, F:] + bot[:, :F]
        tc = jnp.tile(twc_ref[j], (1, F // LANES))
        ts = jnp.tile(tws_ref[j], (1, F // LANES))
        o_ref[0, 0, :, j * F:(j + 1) * F] = (ar * tc + ai * ts).astype(BF16)
        o_ref[0, 1, :, j * F:(j + 1) * F] = (ai * tc - ar * ts).astype(BF16)


def _fft_stage2_kernel(k1t, b_ref, m3_ref, o_ref):
    for j in range(k1t):
        rhs = jnp.concatenate([b_ref[0, 0, j], b_ref[0, 1, j]], axis=0)
        o_ref[0, j] = _dot(m3_ref[...], rhs)


def _fourier_latent(y, m1, m3, twc, tws):
    B = y.shape[0]
    F = FOURIER_WIDTH
    n2t = 16
    yv = y.reshape(B, FFT_N, FFT_N * 2 * F)
    bh = pl.pallas_call(
        functools.partial(_fft_stage1_kernel, n2t),
        grid=(B, FFT_N // n2t),
        in_specs=[pl.BlockSpec((1, FFT_N, n2t * 2 * F), lambda b, t: (b, 0, t)),
                  _const_spec((2 * FFT_N, FFT_N)),
                  pl.BlockSpec((n2t, FFT_N, LANES), lambda b, t: (t, 0, 0)),
                  pl.BlockSpec((n2t, FFT_N, LANES), lambda b, t: (t, 0, 0))],
        out_specs=pl.BlockSpec((1, 2, FFT_N, n2t * F), lambda b, t: (b, 0, 0, t)),
        out_shape=jax.ShapeDtypeStruct((B, 2, FFT_N, FFT_N * F), BF16),
        compiler_params=_params("arbitrary", "arbitrary"),
        name="fft_stage1",
    )(yv, m1, twc, tws)
    k1t = 16
    bv = bh.reshape(B, 2, FFT_N, FFT_N, F)
    return pl.pallas_call(
        functools.partial(_fft_stage2_kernel, k1t),
        grid=(B, FFT_N // k1t),
        in_specs=[pl.BlockSpec((1, 2, k1t, FFT_N, F), lambda b, t: (b, 0, t, 0, 0)),
                  _const_spec((FFT_N, 2 * FFT_N))],
        out_specs=pl.BlockSpec((1, k1t, FFT_N, F), lambda b, t: (b, t, 0, 0)),
        out_shape=jax.ShapeDtypeStruct((B, FFT_N, FFT_N, F), F32),
        compiler_params=_params("arbitrary", "arbitrary"),
        name="fft_stage2",
    )(bv, m3)


def _fourier_ctx_kernel(y_ref, m_ref, o_ref):
    F = FOURIER_WIDTH
    rhs = jnp.concatenate([y_ref[0, :, :F], y_ref[0, :, F:]], axis=0)
    o_ref[0] = _dot(m_ref[...], rhs)


def _fourier_ctx(y, mctx):
    B = y.shape[0]
    return pl.pallas_call(
        _fourier_ctx_kernel,
        grid=(B,),
        in_specs=[pl.BlockSpec((1, CTX_LEN, 2 * FOURIER_WIDTH), lambda b: (b, 0, 0)),
                  _const_spec((CTX_LEN, 2 * CTX_LEN))],
        out_specs=pl.BlockSpec((1, CTX_LEN, FOURIER_WIDTH), lambda b: (b, 0, 0)),
        out_shape=jax.ShapeDtypeStruct((B, CTX_LEN, FOURIER_WIDTH), F32),
        compiler_params=_params("arbitrary"),
        name="fourier_ctx",
    )(y, mctx)


def _gqa_group(q_slabs, keys, vals, masks, sinks, rows, o_ref, first_slab, scratch):
    s_ref, p_ref, inv_ref = scratch
    _gqa_scores(q_slabs, keys, rows, s_ref)
    _gqa_attend(vals, masks, sinks, rows, lambda sl: o_ref.at[0, :, sl], first_slab, scratch)


def _gqa_scores(q_slabs, keys, rows, s_ref):
    lo_lanes = lax.broadcasted_iota(jnp.int32, (rows, LANES), 1) < HEAD_DIM
    zero = jnp.zeros((rows, LANES), BF16)
    qs = jnp.concatenate([jnp.where(lo_lanes if half == 0 else ~lo_lanes, qslab, zero)
                          for qslab in q_slabs for half in range(2)], axis=0)
    s_ref[...] = _dot_nt(qs, keys)


def _gqa_attend(vals, masks, sinks, rows, out_slab, first_slab, scratch):
    s_ref, p_ref, inv_ref = scratch
    lo_lanes = lax.broadcasted_iota(jnp.int32, (rows, LANES), 1) < HEAD_DIM
    for c in range(WIN_GROUP * rows // SOFTMAX_ROWS):
        rs = slice(c * SOFTMAX_ROWS, (c + 1) * SOFTMAX_ROWS)
        s = s_ref[rs, :]
        if masks is not None:
            band, w = masks
            r0 = (c * SOFTMAX_ROWS) % rows
            s = jnp.concatenate([s[:, :w] + band[r0:r0 + SOFTMAX_ROWS], s[:, w:]], axis=1)
        sk = sinks[(c * SOFTMAX_ROWS) // rows]
        m = jnp.maximum(jnp.max(s, axis=-1, keepdims=True), sk)
        e = jnp.exp2(s - m)
        inv_ref[rs, :] = 1.0 / (jnp.sum(e, axis=-1, keepdims=True) + jnp.exp2(sk - m))
        p_ref[rs, :] = e.astype(BF16)
    o = _dot(p_ref[...], vals) * inv_ref[...]
    for i in range(WIN_GROUP // 2):
        a = o[(2 * i) * rows:(2 * i + 1) * rows]
        b = o[(2 * i + 1) * rows:(2 * i + 2) * rows]
        sl = slice((first_slab + i) * LANES, (first_slab + i + 1) * LANES)
        out_slab(sl)[...] = jnp.where(lo_lanes, a, b).astype(BF16)


def _gqa_scratch(rows, n_keys):
    one = [pltpu.VMEM((WIN_GROUP * rows, n_keys), F32), pltpu.VMEM((WIN_GROUP * rows, n_keys), BF16),
           pltpu.VMEM((WIN_GROUP * rows, 1), F32)]
    return one * WIN_KV_HEADS


def _group_sinks(sink_ref, kv):
    return [sink_ref[kv * WIN_GROUP + g] * LOG2E for g in range(WIN_GROUP)]


def _group_slab(kv, i):
    return slice((kv * (WIN_GROUP // 2) + i) * LANES, (kv * (WIN_GROUP // 2) + i + 1) * LANES)


def _window_bands():
    i = np.arange(WIN_BLOCK)[:, None]
    j = np.arange(3 * WIN_BLOCK)[None, :]
    bands = [np.where(np.abs(j - off - i) <= WIN_RADIUS, 0.0, -np.inf) for off in (0, WIN_BLOCK, 2 * WIN_BLOCK)]
    return jnp.asarray(np.stack(bands).astype(np.float32))


def _win_attn_kernel(blocks_per_step, sink_ref, band_ref, q_ref, k_ref, v_ref, kx_ref, vx_ref, o_ref,
                     s0_ref, s1_ref, p_ref, inv_ref):
    t = pl.program_id(1)
    nb = SEQ // WIN_BLOCK
    nw = 3 * WIN_BLOCK

    def place(j):
        n = t * blocks_per_step + j
        start = pl.multiple_of(jnp.clip((n - 1) * WIN_BLOCK, 0, SEQ - nw), WIN_BLOCK)
        sel = jnp.where(n == 0, 0, jnp.where(n == nb - 1, 2, 1))
        return pl.multiple_of(j * WIN_BLOCK, WIN_BLOCK), start, sel

    def scores(j, kv, s_ref):
        qoff, start, _ = place(j)
        keys = jnp.concatenate([k_ref[0, kv, pl.ds(start, nw), :], kx_ref[0, kv]], axis=0)
        q_slabs = [q_ref[0, pl.ds(qoff, WIN_BLOCK), _group_slab(kv, i)] for i in range(WIN_GROUP // 2)]
        _gqa_scores(q_slabs, keys, WIN_BLOCK, s_ref)

    def attend(j, kv, s_ref):
        qoff, start, sel = place(j)
        vals = jnp.concatenate([v_ref[0, kv, pl.ds(start, nw), :], vx_ref[0, kv]], axis=0)
        _gqa_attend(vals, (band_ref[sel], nw), _group_sinks(sink_ref, kv), WIN_BLOCK,
                    lambda sl: o_ref.at[0, pl.ds(qoff, WIN_BLOCK), sl], kv * (WIN_GROUP // 2),
                    (s_ref, p_ref, inv_ref))

    scores(0, 0, s0_ref)

    def one_block(j, carry):
        scores(j, 1, s1_ref)
        attend(j, 0, s0_ref)
        scores(jnp.minimum(j + 1, blocks_per_step - 1), 0, s0_ref)
        attend(j, 1, s1_ref)
        return carry

    lax.fori_loop(0, blocks_per_step, one_block, 0)


def _win_attention(q, k, v, kx, vx, sink):
    B, L, _ = q.shape
    blocks_per_step = 8
    rows = blocks_per_step * WIN_BLOCK
    n_keys = 3 * WIN_BLOCK + CTX_LEN
    full = pl.BlockSpec((1, WIN_KV_HEADS, L, LANES), lambda b, t: (b, 0, 0, 0))
    ctx_spec = pl.BlockSpec((1, WIN_KV_HEADS, CTX_LEN, LANES), lambda b, t: (0, 0, b, 0))
    tok = pl.BlockSpec((1, rows, QW), lambda b, t: (b, t, 0))
    return pl.pallas_call(
        functools.partial(_win_attn_kernel, blocks_per_step),
        grid=(B, L // rows),
        in_specs=[pl.BlockSpec(memory_space=pltpu.SMEM),
                  _const_spec((3, WIN_BLOCK, 3 * WIN_BLOCK)),
                  tok, full, full, ctx_spec, ctx_spec],
        out_specs=tok,
        out_shape=jax.ShapeDtypeStruct((B, L, QW), BF16),
        scratch_shapes=[pltpu.VMEM((WIN_GROUP * WIN_BLOCK, n_keys), F32), pltpu.VMEM((WIN_GROUP * WIN_BLOCK, n_keys), F32),
                        pltpu.VMEM((WIN_GROUP * WIN_BLOCK, n_keys), BF16), pltpu.VMEM((WIN_GROUP * WIN_BLOCK, 1), F32)],
        compiler_params=_params("arbitrary", "arbitrary"),
        name="window_attention",
    )(sink, _window_bands(), q, k, v, kx, vx)


def _ctx_attn_even_kernel(sink_ref, q_ref, k_ref, v_ref, o_ref, *scratch):
    for kv in range(WIN_KV_HEADS):
        q_slabs = [q_ref[0, :, _group_slab(kv, i)] for i in range(WIN_GROUP // 2)]
        _gqa_group(q_slabs, k_ref[0, kv], v_ref[0, kv], None, _group_sinks(sink_ref, kv), CTX_LEN, o_ref,
                   kv * (WIN_GROUP // 2), scratch[3 * kv:3 * kv + 3])


def _ctx_attention_even(q, k, v, sink):
    B = q.shape[0]
    kv_spec = pl.BlockSpec((1, WIN_KV_HEADS, CTX_LEN, LANES), lambda b: (0, 0, b, 0))
    return pl.pallas_call(
        _ctx_attn_even_kernel,
        grid=(B,),
        in_specs=[pl.BlockSpec(memory_space=pltpu.SMEM),
                  pl.BlockSpec((1, CTX_LEN, QW), lambda b: (b, 0, 0)),
                  kv_spec, kv_spec],
        out_specs=pl.BlockSpec((1, CTX_LEN, QW), lambda b: (b, 0, 0)),
        out_shape=jax.ShapeDtypeStruct((B, CTX_LEN, QW), BF16),
        scratch_shapes=_gqa_scratch(CTX_LEN, CTX_LEN),
        compiler_params=_params("arbitrary"),
        name="ctx_attention_even",
    )(sink, q, k, v)


def _out_ffn_kernel(mode, *refs):
    if mode == "odd":
        a_ref, x_ref, mod_ref, g_ref, wo_ref, wg_ref, wu_ref, wd_ref, o_ref = refs
    else:
        f_ref, a_ref, x_ref, mod_ref, g_ref, wo_ref, wg_ref, wu_ref, wd_ref, o_ref = refs
    for r in range(x_ref.shape[1] // SUB_ROWS):
        rows = slice(r * SUB_ROWS, (r + 1) * SUB_ROWS)
        if mode == "even_latent":
            planes = range(r * SUB_ROWS // FFT_N, (r + 1) * SUB_ROWS // FFT_N)
            fm = jnp.concatenate([f_ref[0, :, j, :] for j in planes], axis=0).astype(BF16)
            o = _dot(fm, wo_ref[:FOURIER_WIDTH]) + _dot(a_ref[0, rows], wo_ref[FOURIER_WIDTH:])
        elif mode == "even_ctx":
            o = _dot(f_ref[0, rows].astype(BF16), wo_ref[:FOURIER_WIDTH]) + _dot(a_ref[0, rows], wo_ref[FOURIER_WIDTH:])
        else:
            o = _dot(jnp.concatenate([a_ref[0, j, rows] for j in range(a_ref.shape[1])], axis=1), wo_ref[...])
        x1 = x_ref[0, rows] + _mod_slice(mod_ref, 2) * o
        h = _rms_mod(x1, g_ref[...], _mod_slice(mod_ref, 4), _mod_slice(mod_ref, 3)).astype(BF16)
        acc = jnp.zeros_like(x1)
        for c in range(D_FF // FF_CHUNK):
            cs = slice(c * FF_CHUNK, (c + 1) * FF_CHUNK)
            a = _silu(_dot(h, wg_ref[0, :, cs])) * _dot(h, wu_ref[0, :, cs])
            acc = acc + _dot(a.astype(BF16), wd_ref[0, cs, :])
        o_ref[0, rows] = x1 + _mod_slice(mod_ref, 5) * acc


def _out_ffn(mode, mix, x, mod, mod_row, g, w_out, layer, wg, wu, wd, tm):
    B, L, _ = x.shape
    tok = lambda w: pl.BlockSpec((1, tm, w), lambda b, t: (b, t, 0))
    ffn_spec = lambda r, c: pl.BlockSpec((1, r, c), lambda b, t: (layer, 0, 0), pipeline_mode=pl.Buffered(1))
    if mode == "even_latent":
        assert tm % FFT_N == 0
        mix_specs = [pl.BlockSpec((1, FFT_N, tm // FFT_N, FOURIER_WIDTH), lambda b, t: (b, 0, t, 0)), tok(QW)]
    elif mode == "even_ctx":
        mix_specs = [tok(FOURIER_WIDTH), tok(QW)]
    else:
        mix_specs = [pl.BlockSpec((1, NA_WIDTH // LANES, tm, LANES), lambda b, t: (b, 0, t, 0))]
    return pl.pallas_call(
        functools.partial(_out_ffn_kernel, mode),
        grid=(B, L // tm),
        in_specs=mix_specs + [tok(D_MODEL),
                              pl.BlockSpec((1, 1, 6 * D_MODEL), lambda b, t: (mod_row(b), 0, 0)),
                              _const_spec((1, D_MODEL)),
                              _const_spec((D_MODEL, D_MODEL)),
                              ffn_spec(D_MODEL, D_FF), ffn_spec(D_MODEL, D_FF), ffn_spec(D_FF, D_MODEL)],
        out_specs=tok(D_MODEL),
        out_shape=jax.ShapeDtypeStruct((B, L, D_MODEL), F32),
        compiler_params=_params("arbitrary", "arbitrary"),
        name="out_ffn_" + mode,
    )(*mix, x, mod, g, w_out, wg, wu, wd)


def _in_odd_kernel(with_q, x_ref, mod_ref, g_ref, w32_ref, qg_ref, kg_ref, hm_ref, *outs):
    W = NA_WIDTH
    slabs = RMS_W // LANES
    if with_q:
        q_ref, k_ref, v_ref, w_ref = outs
    else:
        k_ref, v_ref, w_ref = outs
    _cast_once(w32_ref, w_ref)

    for r in range(x_ref.shape[1] // SUB_ROWS):
        rows = slice(r * SUB_ROWS, (r + 1) * SUB_ROWS)
        h = _rms_mod(x_ref[0, rows], g_ref[...], _mod_slice(mod_ref, 1), _mod_slice(mod_ref, 0)).astype(BF16)

        def normed(t, gain_ref, scale, o_ref):
            for j in range(W // RMS_W):
                tj = (_head_rms(t[:, j * RMS_W:(j + 1) * RMS_W], gain_ref, hm_ref) * scale).astype(BF16)
                for i in range(slabs):
                    o_ref[0, j * slabs + i, rows] = tj[:, i * LANES:(i + 1) * LANES]

        if with_q:
            normed(_dot(h, w_ref[:, :W]), qg_ref, QK_SCALE, q_ref)
        normed(_dot(h, w_ref[:, W:2 * W]), kg_ref, 1.0, k_ref)
        v = _dot(h, w_ref[:, 2 * W:]).astype(BF16)
        for j in range(W // LANES):
            v_ref[0, j, rows] = v[:, j * LANES:(j + 1) * LANES]


def _in_odd(x, mod, mod_row, g, w_in, q_g, k_g, hm, with_q, tm):
    B, L, _ = x.shape
    pairs = NA_WIDTH // LANES
    tok = pl.BlockSpec((1, pairs, tm, LANES), lambda b, t: (b, 0, t, 0))
    n_out = 3 if with_q else 2
    return pl.pallas_call(
        functools.partial(_in_odd_kernel, with_q),
        grid=(B, L // tm),
        in_specs=[pl.BlockSpec((1, tm, D_MODEL), lambda b, t: (b, t, 0)),
                  pl.BlockSpec((1, 1, 6 * D_MODEL), lambda b, t: (mod_row(b), 0, 0)),
                  _const_spec((1, D_MODEL)),
                  _const_spec((D_MODEL, 3 * NA_WIDTH)),
                  _const_spec((1, RMS_W)), _const_spec((1, RMS_W)), _const_spec((RMS_W, RMS_W))],
        out_specs=[tok] * n_out,
        out_shape=[jax.ShapeDtypeStruct((B, pairs, L, LANES), BF16)] * n_out,
        compiler_params=_params("arbitrary", "arbitrary"),
        scratch_shapes=[pltpu.VMEM((D_MODEL, 3 * NA_WIDTH), BF16)],
        name="in_odd" if with_q else "in_odd_ctx",
    )(x, mod, g, w_in, q_g, k_g, hm)


def _bias_table_kernel(rb_ref, idx_ref, o_ref):
    idx = idx_ref[...]
    cq = lax.broadcasted_iota(jnp.int32, idx.shape, 0)
    ck = lax.broadcasted_iota(jnp.int32, idx.shape, 1) % GRID_W
    c0 = jnp.clip(cq - NA_KW // 2, 0, GRID_W - NA_KW)
    inside = (ck >= c0) & (ck < c0 + NA_KW)
    for h in range(o_ref.shape[0]):
        for dr in range(o_ref.shape[1]):
            row = jnp.broadcast_to(rb_ref[h, dr:dr + 1, :], idx.shape)
            o_ref[h, dr] = jnp.where(inside, jnp.take_along_axis(row, idx, axis=1) * LOG2E, -jnp.inf)


def _na_bias_table(rel_bias, idx):
    H = NA_HEADS
    npair = 2 * NA_KH - 2
    half = LANES // 2
    hb = 4
    pad = lambda a: jnp.pad(a, ((0, 0), (0, 0), (0, half - a.shape[-1])))
    rb2 = jnp.concatenate([pad(rel_bias[:, :-1]), pad(rel_bias[:, 1:])], axis=-1)
    return pl.pallas_call(
        _bias_table_kernel,
        grid=(H // hb,),
        in_specs=[pl.BlockSpec((hb, npair, LANES), lambda h: (h, 0, 0)), _const_spec((GRID_W, LANES))],
        out_specs=pl.BlockSpec((hb, npair, GRID_W, LANES), lambda h: (h, 0, 0, 0)),
        out_shape=jax.ShapeDtypeStruct((H, npair, GRID_W, LANES), F32),
        compiler_params=_params("arbitrary"),
        name="na_bias_table",
    )(rb2, idx)


def _na_kernel(rows_per_step, q_ref, k_ref, v_ref, kx_ref, vx_ref, bias_ref, o_ref, s0_ref, s1_ref, p_ref, inv_ref):
    t = pl.program_id(1)
    n = NA_KH * GRID_W
    lo_lanes = lax.broadcasted_iota(jnp.int32, (GRID_W, LANES), 1) < HEAD_DIM

    def window(i):
        r = t * rows_per_step + i
        r0 = jnp.clip(r - NA_KH // 2, 0, GRID_ROWS - NA_KH)
        return r0 - r + NA_KH - 1, pl.multiple_of(r0 * GRID_W, GRID_W), pl.multiple_of(i * GRID_W, GRID_W)

    def scores(i, s_ref):
        dr0, start, qoff = window(i)
        for hp in range(NA_HEADS // 2):
            qs = q_ref[0, hp, pl.ds(qoff, GRID_W), :]
            keys = jnp.concatenate([k_ref[0, hp, pl.ds(start, n), :], kx_ref[0, hp]], axis=0)
            qm = jnp.concatenate([jnp.where(lo_lanes, qs, jnp.zeros_like(qs)),
                                  jnp.where(lo_lanes, jnp.zeros_like(qs), qs)], axis=0)
            s = _dot_nt(qm, keys)
            for half in range(2):
                h = 2 * hp + half
                bias = jnp.concatenate([bias_ref[h, dr0 + 2 * p] for p in range(NA_KH // 2)], axis=1)
                rs = slice(h * GRID_W, (h + 1) * GRID_W)
                hr = slice(half * GRID_W, (half + 1) * GRID_W)
                s_ref[rs, :n] = s[hr, :n] + bias
                s_ref[rs, n:] = s[hr, n:]

    def attend(i, s_ref):
        _, start, qoff = window(i)
        for c in range(NA_HEADS * GRID_W // SOFTMAX_ROWS):
            cs = slice(c * SOFTMAX_ROWS, (c + 1) * SOFTMAX_ROWS)
            s = s_ref[cs, :]
            e = jnp.exp2(s - jnp.max(s, axis=-1, keepdims=True))
            inv_ref[cs, :] = 1.0 / jnp.sum(e, axis=-1, keepdims=True)
            p_ref[cs, :] = e.astype(BF16)
        for hp in range(NA_HEADS // 2):
            vals = jnp.concatenate([v_ref[0, hp, pl.ds(start, n), :], vx_ref[0, hp]], axis=0)
            rs = slice(2 * hp * GRID_W, (2 * hp + 2) * GRID_W)
            res = _dot(p_ref[rs, :], vals) * inv_ref[rs, :]
            o_ref[0, hp, pl.ds(qoff, GRID_W), :] = jnp.where(lo_lanes, res[:GRID_W], res[GRID_W:]).astype(BF16)

    last = rows_per_step - 1
    scores(0, s0_ref)

    def two_rows(j, carry):
        scores(2 * j + 1, s1_ref)
        attend(2 * j, s0_ref)
        scores(jnp.minimum(2 * j + 2, last), s0_ref)
        attend(2 * j + 1, s1_ref)
        return carry

    lax.fori_loop(0, rows_per_step // 2, two_rows, 0)


def _neighbourhood_attention(q, k, v, kx, vx, bias_tab):
    B, pairs, L, _ = q.shape
    rows_per_step = 16
    full = pl.BlockSpec((1, pairs, L, LANES), lambda b, t: (b, 0, 0, 0))
    ctx_spec = pl.BlockSpec((1, pairs, CTX_LEN, LANES), lambda b, t: (0, 0, b, 0))
    q_rows = pl.BlockSpec((1, pairs, rows_per_step * GRID_W, LANES), lambda b, t: (b, 0, t, 0))
    n_keys = NA_KH * GRID_W + CTX_LEN
    all_rows = NA_HEADS * GRID_W
    return pl.pallas_call(
        functools.partial(_na_kernel, rows_per_step),
        grid=(B, GRID_ROWS // rows_per_step),
        in_specs=[q_rows, full, full, ctx_spec, ctx_spec, _const_spec(bias_tab.shape)],
        out_specs=q_rows,
        out_shape=jax.ShapeDtypeStruct((B, pairs, L, LANES), BF16),
        scratch_shapes=[pltpu.VMEM((all_rows, n_keys), F32), pltpu.VMEM((all_rows, n_keys), F32),
                        pltpu.VMEM((all_rows, n_keys), BF16), pltpu.VMEM((all_rows, 1), F32)],
        compiler_params=_params("arbitrary", "arbitrary"),
        name="neighbourhood_attention",
    )(q, k, v, kx, vx, bias_tab)


def kernel(x, c, ctx, c_ctx, ada_w, ada_b, norm1_g, norm2_g, ffn_w_gate, ffn_w_up, ffn_w_down,
           ev_w_in, ev_w_out, ev_q_norm, ev_k_norm, ev_sink,
           od_w_in, od_w_out, od_q_norm, od_k_norm, od_rel_bias):
    assert x.shape == (BATCH, SEQ, D_MODEL) and ctx.shape == (BATCH, CTX_LEN, D_MODEL)
    wc, m1, m3, twc, tws, mctx = _fourier_tables()
    hm = _head_mean_matrix()
    rope_tabs = _rope_tables()
    lane_gain = lambda gvec: jnp.tile(gvec, RMS_W // HEAD_DIM).reshape(1, RMS_W)
    lat_row = lambda b: b
    ctx_row = lambda b: CTX_MOD_ROW
    tm = 512
    n_ctx = BATCH * CTX_LEN
    as_seq = lambda a: a.reshape(1, n_ctx, a.shape[-1])
    per_batch = lambda a: a.reshape(BATCH, CTX_LEN, a.shape[-1])

    cs = jnp.concatenate([c, c_ctx[None, :], jnp.zeros((MOD_ROWS - BATCH - 1, D_MODEL), F32)], axis=0)
    mod = _modulation(cs, ada_w, ada_b).reshape(DEPTH, MOD_ROWS, 1, 6 * D_MODEL)

    w_in0 = ev_w_in[0]
    w_out0 = ev_w_out[0].astype(BF16)
    g1 = norm1_g[0].reshape(1, D_MODEL)
    g2 = norm2_g[0].reshape(1, D_MODEL)
    ffn = (ffn_w_gate.astype(BF16), ffn_w_up.astype(BF16), ffn_w_down.astype(BF16))
    qg, kg = lane_gain(ev_q_norm[0]), lane_gain(ev_k_norm[0])
    y_l, q_l, k_l, v_l = _in_even(x, mod[0], lat_row, g1, w_in0, qg, kg, hm, wc, rope_tabs, 2 * tm)
    y_c, q_c, k_c, v_c = _in_even(as_seq(ctx), mod[0], ctx_row, g1, w_in0, qg, kg, hm, wc, None, tm)
    y_c, q_c = per_batch(y_c), per_batch(q_c)
    f_l = _fourier_latent(y_l, m1, m3, twc, tws)
    f_c = _fourier_ctx(y_c, mctx)
    a_l = _win_attention(q_l, k_l, v_l, k_c, v_c, ev_sink[0])
    a_c = _ctx_attention_even(q_c, k_c, v_c, ev_sink[0])
    x1 = _out_ffn("even_latent", (f_l, a_l), x, mod[0], lat_row, g2, w_out0, 0, *ffn, 2 * tm)
    y1 = _out_ffn("even_ctx", (as_seq(f_c), as_seq(a_c)), as_seq(ctx), mod[0], ctx_row, g2, w_out0, 0, *ffn, tm)

    w_in1 = od_w_in[0]
    w_out1 = od_w_out[0].astype(BF16)
    g1 = norm1_g[1].reshape(1, D_MODEL)
    g2 = norm2_g[1].reshape(1, D_MODEL)
    qg, kg = lane_gain(od_q_norm[0]), lane_gain(od_k_norm[0])
    q_l, k_l, v_l = _in_odd(x1, mod[1], lat_row, g1, w_in1, qg, kg, hm, True, 2 * tm)
    k_c, v_c = _in_odd(y1, mod[1], ctx_row, g1, w_in1, qg, kg, hm, False, tm)
    bias_tab = _na_bias_table(od_rel_bias[0], _na_bias_index())
    a_l = _neighbourhood_attention(q_l, k_l, v_l, k_c, v_c, bias_tab)
    return _out_ffn("odd", (a_l,), x1, mod[1], lat_row, g2, w_out1, 1, *ffn, 2 * tm)
```

```python
import functools
import math

import numpy as np
import jax
import jax.numpy as jnp
from jax import lax
from jax.experimental import pallas as pl
from jax.experimental.pallas import tpu as pltpu

D_MODEL = 1024
BATCH = 4
SEQ = 4096
DEPTH = 2
GRID_W = 64
CTX_LEN = 256
HEAD_DIM = 64
EPS = 1e-6
FOURIER_WIDTH = D_MODEL // 2
FOURIER_GROUPS = 4
FOURIER_GROUP_CH = FOURIER_WIDTH // FOURIER_GROUPS
WIN_Q_HEADS = (D_MODEL // 2) // HEAD_DIM
WIN_KV_HEADS = 2
WIN_GROUP = WIN_Q_HEADS // WIN_KV_HEADS
WIN_RADIUS = 128
WIN_BLOCK = 128
QW = WIN_Q_HEADS * HEAD_DIM
KW = WIN_KV_HEADS * HEAD_DIM
EV_IN_WIDTH = FOURIER_WIDTH + QW + 2 * KW
NA_HEADS = D_MODEL // HEAD_DIM
NA_KH = 8
NA_KW = 16
NA_WIDTH = NA_HEADS * HEAD_DIM
ROPE_THETA = 10000.0
ROPE_FREQS = HEAD_DIM // 4
D_FF = ((8 * D_MODEL // 3 + 255) // 256) * 256
GRID_ROWS = SEQ // GRID_W

LANES = 128
MOD_ROWS = 8
CTX_MOD_ROW = BATCH
FFT_N = 64
FF_CHUNK = 256
SOFTMAX_ROWS = 64
RMS_W = 256
SUB_ROWS = 512
VMEM_LIMIT = 60 * 1024 * 1024

LOG2E = math.log2(math.e)
QK_SCALE = LOG2E / math.sqrt(HEAD_DIM)

BF16 = jnp.bfloat16
F32 = jnp.float32

assert DEPTH == 2 and SEQ == FFT_N * FFT_N and D_FF % FF_CHUNK == 0


def _params(*sem):
    return pltpu.CompilerParams(dimension_semantics=sem, vmem_limit_bytes=VMEM_LIMIT)


def _dot(a, b):
    return jnp.dot(a, b, preferred_element_type=F32)


def _dot_nt(a, b):
    return lax.dot_general(a, b, (((1,), (1,)), ((), ())), preferred_element_type=F32)


def _silu(x):
    return x / (1.0 + jnp.exp(-x))


def _const_spec(shape):
    nd = len(shape)
    return pl.BlockSpec(shape, lambda *_: (0,) * nd, pipeline_mode=pl.Buffered(1))


def _dft_cos_sin(n):
    idx = (np.arange(n)[:, None] * np.arange(n)[None, :]) % n
    ang = 2.0 * np.pi * idx / n
    return np.cos(ang), np.sin(ang)


def _fourier_tables():
    cc, sc = _dft_cos_sin(FOURIER_GROUP_CH)
    wc = np.concatenate([cc, -sc], axis=1) / math.sqrt(FOURIER_GROUP_CH)
    c64, s64 = _dft_cos_sin(FFT_N)
    m1 = np.concatenate([c64, -s64], axis=0) / math.sqrt(FFT_N)
    m3 = np.concatenate([c64, s64], axis=1) / math.sqrt(FFT_N)
    tw = (np.arange(FFT_N)[:, None] * np.arange(FFT_N)[None, :]) % SEQ
    tw = 2.0 * np.pi * tw / SEQ
    twc = np.repeat(np.cos(tw)[:, :, None], LANES, axis=2)
    tws = np.repeat(np.sin(tw)[:, :, None], LANES, axis=2)
    cx, sx = _dft_cos_sin(CTX_LEN)
    mctx = np.concatenate([cx, sx], axis=1) / math.sqrt(CTX_LEN)
    as32 = lambda a: jnp.asarray(a, F32)
    return (as32(wc).astype(BF16), as32(m1).astype(BF16), as32(m3).astype(BF16),
            as32(twc), as32(tws), as32(mctx).astype(BF16))


def _head_mean_matrix():
    blk = np.kron(np.eye(RMS_W // HEAD_DIM), np.ones((HEAD_DIM, HEAD_DIM))) / HEAD_DIM
    return jnp.asarray(blk, BF16)


def _rope_tables():
    t = jnp.arange(SEQ, dtype=jnp.int32)
    row = (t // GRID_W).astype(F32)
    col = (t % GRID_W).astype(F32)
    inv = ROPE_THETA ** (-jnp.arange(ROPE_FREQS, dtype=F32) / ROPE_FREQS)
    ang_row = row[:, None] * inv[None, :]
    ang_col = col[:, None] * inv[None, :]
    zero = jnp.zeros_like(ang_row)
    cos = jnp.concatenate([jnp.cos(ang_row)] * 2 + [jnp.cos(ang_col)] * 2, axis=1)
    sin_hi = jnp.concatenate([-jnp.sin(ang_row), zero, -jnp.sin(ang_col), zero], axis=1)
    sin_lo = jnp.concatenate([zero, jnp.sin(ang_row), zero, jnp.sin(ang_col)], axis=1)
    rep = LANES // HEAD_DIM
    return jnp.tile(cos, (1, rep)), jnp.tile(sin_hi, (1, rep)), jnp.tile(sin_lo, (1, rep))


def _na_bias_index():
    cq = np.arange(GRID_W)
    dc = np.clip(cq[None, :] - cq[:, None] + NA_KW - 1, 0, 2 * NA_KW - 2)
    return jnp.asarray(np.concatenate([dc, dc + LANES // 2], axis=1), jnp.int32)


def _mod_kernel(cs_ref, w_ref, b_ref, o_ref):
    s = _silu(cs_ref[...]).astype(BF16)
    o_ref[0] = _dot(s, w_ref[0].astype(BF16)) + b_ref[0]


def _modulation(cs, ada_w, ada_b):
    tn = 1536
    return pl.pallas_call(
        _mod_kernel,
        grid=(DEPTH, 6 * D_MODEL // tn),
        in_specs=[pl.BlockSpec((MOD_ROWS, D_MODEL), lambda i, j: (0, 0)),
                  pl.BlockSpec((1, D_MODEL, tn), lambda i, j: (i, 0, j)),
                  pl.BlockSpec((1, 1, tn), lambda i, j: (i, 0, j))],
        out_specs=pl.BlockSpec((1, MOD_ROWS, tn), lambda i, j: (i, 0, j)),
        out_shape=jax.ShapeDtypeStruct((DEPTH, MOD_ROWS, 6 * D_MODEL), F32),
        compiler_params=_params("arbitrary", "arbitrary"),
        name="ada_modulation",
    )(cs, ada_w, ada_b.reshape(DEPTH, 1, 6 * D_MODEL))


def _mod_slice(mod_ref, k):
    return mod_ref[0, :, k * D_MODEL:(k + 1) * D_MODEL]


def _rms_mod(x, g, scale, shift):
    y = x * lax.rsqrt(jnp.mean(x * x, axis=-1, keepdims=True) + EPS)
    return (y * g) * (1.0 + scale) + shift


def _head_rms(t, gain_ref, hm_ref):
    w = t.shape[1]
    ms = _dot((t * t).astype(BF16), hm_ref[:w, :w])
    return t * lax.rsqrt(ms + EPS) * gain_ref[:, :w]


def _rope(t, cos, sin_hi, sin_lo):
    up = pltpu.roll(t, LANES - ROPE_FREQS, axis=1)
    dn = pltpu.roll(t, ROPE_FREQS, axis=1)
    return t * cos + up * sin_hi + dn * sin_lo


def _cast_once(w32_ref, w_ref):
    @pl.when((pl.program_id(0) == 0) & (pl.program_id(1) == 0))
    def _():
        w_ref[...] = w32_ref[...].astype(BF16)


def _in_even_kernel(latent, x_ref, mod_ref, g_ref, w32_ref, qg_ref, kg_ref, hm_ref, wc_ref, *rest):
    if latent:
        cos_ref, sh_ref, sl_ref, y_ref, q_ref, k_ref, v_ref, w_ref = rest
    else:
        y_ref, q_ref, k_ref, v_ref, w_ref = rest
    _cast_once(w32_ref, w_ref)
    F = FOURIER_WIDTH
    lo_lanes = lax.broadcasted_iota(jnp.int32, (1, LANES), 1) < HEAD_DIM

    for r in range(x_ref.shape[1] // SUB_ROWS):
        rows = slice(r * SUB_ROWS, (r + 1) * SUB_ROWS)
        h = _rms_mod(x_ref[0, rows], g_ref[...], _mod_slice(mod_ref, 1), _mod_slice(mod_ref, 0)).astype(BF16)
        f = _dot(h, w_ref[:, :F]).astype(BF16)
        for g in range(FOURIER_GROUPS):
            yg = _dot(f[:, g * LANES:(g + 1) * LANES], wc_ref[...])
            y_ref[0, rows, g * LANES:(g + 1) * LANES] = yg[:, :LANES].astype(y_ref.dtype)
            y_ref[0, rows, F + g * LANES:F + (g + 1) * LANES] = yg[:, LANES:].astype(y_ref.dtype)

        def qk(t, gain_ref):
            t = _head_rms(t, gain_ref, hm_ref)
            if latent:
                t = jnp.concatenate([_rope(t[:, i * LANES:(i + 1) * LANES], cos_ref[rows], sh_ref[rows], sl_ref[rows])
                                     for i in range(t.shape[1] // LANES)], axis=1)
            return t

        q = _dot(h, w_ref[:, F:F + QW])
        for j in range(QW // RMS_W):
            qj = qk(q[:, j * RMS_W:(j + 1) * RMS_W], qg_ref) * QK_SCALE
            q_ref[0, rows, j * RMS_W:(j + 1) * RMS_W] = qj.astype(BF16)
        kv = _dot(h, w_ref[:, F + QW:])

        def store_dup(t, o_ref):
            sw = pltpu.roll(t, HEAD_DIM, axis=1)
            o_ref[0, 0, rows] = jnp.where(lo_lanes, t, sw).astype(BF16)
            o_ref[0, 1, rows] = jnp.where(lo_lanes, sw, t).astype(BF16)

        store_dup(qk(kv[:, :KW], kg_ref), k_ref)
        store_dup(kv[:, KW:], v_ref)


def _in_even(x, mod, mod_row, g, w_in, q_g, k_g, hm, wc, rope_tabs, tm):
    B, L, _ = x.shape
    latent = rope_tabs is not None
    in_specs = [pl.BlockSpec((1, tm, D_MODEL), lambda b, t: (b, t, 0)),
                pl.BlockSpec((1, 1, 6 * D_MODEL), lambda b, t: (mod_row(b), 0, 0)),
                _const_spec((1, D_MODEL)),
                _const_spec((D_MODEL, EV_IN_WIDTH)),
                _const_spec((1, RMS_W)), _const_spec((1, RMS_W)),
                _const_spec((RMS_W, RMS_W)), _const_spec((LANES, 2 * LANES))]
    args = [x, mod, g, w_in, q_g, k_g, hm, wc]
    tok = lambda w: pl.BlockSpec((1, tm, w), lambda b, t: (b, t, 0))
    kv_spec = pl.BlockSpec((1, WIN_KV_HEADS, tm, LANES), lambda b, t: (b, 0, t, 0))
    if latent:
        in_specs += [pl.BlockSpec((tm, LANES), lambda b, t: (t, 0))] * 3
        args += list(rope_tabs)
    return pl.pallas_call(
        functools.partial(_in_even_kernel, latent),
        grid=(B, L // tm),
        in_specs=in_specs,
        out_specs=[tok(2 * FOURIER_WIDTH), tok(QW), kv_spec, kv_spec],
        out_shape=[jax.ShapeDtypeStruct((B, L, 2 * FOURIER_WIDTH), F32 if latent else BF16),
                   jax.ShapeDtypeStruct((B, L, QW), BF16),
                   jax.ShapeDtypeStruct((B, WIN_KV_HEADS, L, LANES), BF16),
                   jax.ShapeDtypeStruct((B, WIN_KV_HEADS, L, LANES), BF16)],
        compiler_params=_params("arbitrary", "arbitrary"),
        scratch_shapes=[pltpu.VMEM((D_MODEL, EV_IN_WIDTH), BF16)],
        name="in_even_latent" if latent else "in_even_ctx",
    )(*args)


def _fft_stage1_kernel(n2t, y_ref, m1_ref, twc_ref, tws_ref, o_ref, row_ref):
    F = FOURIER_WIDTH
    for j in range(n2t):
        row_ref[j % 2] = y_ref[0, :, j, :]
        a = _dot(m1_ref[...], row_ref[j % 2].astype(BF16))
        top, bot = a[:FFT_N], a[FFT_N:]
        ar = top[:, :F] - bot[:, F:]
        ai = top[:, F:] + bot[:, :F]
        tc = jnp.tile(twc_ref[j], (1, F // LANES))
        ts = jnp.tile(tws_ref[j], (1, F // LANES))
        o_ref[0, 0, :, j * F:(j + 1) * F] = (ar * tc + ai * ts).astype(BF16)
        o_ref[0, 1, :, j * F:(j + 1) * F] = (ai * tc - ar * ts).astype(BF16)


def _fft_stage2_kernel(k1t, b_ref, m3_ref, o_ref):
    for j in range(k1t):
        rhs = jnp.concatenate([b_ref[0, 0, j], b_ref[0, 1, j]], axis=0)
        o_ref[0, j] = _dot(m3_ref[...], rhs)


def _fourier_latent(y, m1, m3, twc, tws):
    B = y.shape[0]
    F = FOURIER_WIDTH
    n2t = 16
    yv = y.reshape(B, FFT_N, FFT_N, 2 * F)
    bh = pl.pallas_call(
        functools.partial(_fft_stage1_kernel, n2t),
        grid=(B, FFT_N // n2t),
        in_specs=[pl.BlockSpec((1, FFT_N, n2t, 2 * F), lambda b, t: (b, 0, t, 0)),
                  _const_spec((2 * FFT_N, FFT_N)),
                  pl.BlockSpec((n2t, FFT_N, LANES), lambda b, t: (t, 0, 0)),
                  pl.BlockSpec((n2t, FFT_N, LANES), lambda b, t: (t, 0, 0))],
        out_specs=pl.BlockSpec((1, 2, FFT_N, n2t * F), lambda b, t: (b, 0, 0, t)),
        out_shape=jax.ShapeDtypeStruct((B, 2, FFT_N, FFT_N * F), BF16),
        scratch_shapes=[pltpu.VMEM((2, FFT_N, 2 * F), F32)],
        compiler_params=_params("arbitrary", "arbitrary"),
        name="fft_stage1",
    )(yv, m1, twc, tws)
    k1t = 16
    bv = bh.reshape(B, 2, FFT_N, FFT_N, F)
    return pl.pallas_call(
        functools.partial(_fft_stage2_kernel, k1t),
        grid=(B, FFT_N // k1t),
        in_specs=[pl.BlockSpec((1, 2, k1t, FFT_N, F), lambda b, t: (b, 0, t, 0, 0)),
                  _const_spec((FFT_N, 2 * FFT_N))],
        out_specs=pl.BlockSpec((1, k1t, FFT_N, F), lambda b, t: (b, t, 0, 0)),
        out_shape=jax.ShapeDtypeStruct((B, FFT_N, FFT_N, F), F32),
        compiler_params=_params("arbitrary", "arbitrary"),
        name="fft_stage2",
    )(bv, m3)


def _fourier_ctx_kernel(y_ref, m_ref, o_ref):
    F = FOURIER_WIDTH
    rhs = jnp.concatenate([y_ref[0, :, :F], y_ref[0, :, F:]], axis=0)
    o_ref[0] = _dot(m_ref[...], rhs)


def _fourier_ctx(y, mctx):
    B = y.shape[0]
    return pl.pallas_call(
        _fourier_ctx_kernel,
        grid=(B,),
        in_specs=[pl.BlockSpec((1, CTX_LEN, 2 * FOURIER_WIDTH), lambda b: (b, 0, 0)),
                  _const_spec((CTX_LEN, 2 * CTX_LEN))],
        out_specs=pl.BlockSpec((1, CTX_LEN, FOURIER_WIDTH), lambda b: (b, 0, 0)),
        out_shape=jax.ShapeDtypeStruct((B, CTX_LEN, FOURIER_WIDTH), F32),
        compiler_params=_params("arbitrary"),
        name="fourier_ctx",
    )(y, mctx)


def _gqa_group(q_slabs, keys, vals, masks, sinks, rows, o_ref, first_slab, scratch):
    s_ref, p_ref, inv_ref = scratch
    _gqa_scores(q_slabs, keys, rows, s_ref)
    _gqa_attend(vals, masks, sinks, rows, lambda sl: o_ref.at[0, :, sl], first_slab, scratch)


def _gqa_scores(q_slabs, keys, rows, s_ref):
    lo_lanes = lax.broadcasted_iota(jnp.int32, (rows, LANES), 1) < HEAD_DIM
    zero = jnp.zeros((rows, LANES), BF16)
    qs = jnp.concatenate([jnp.where(lo_lanes if half == 0 else ~lo_lanes, qslab, zero)
                          for qslab in q_slabs for half in range(2)], axis=0)
    s_ref[...] = _dot_nt(qs, keys)


def _gqa_attend(vals, masks, sinks, rows, out_slab, first_slab, scratch):
    s_ref, p_ref, inv_ref = scratch
    lo_lanes = lax.broadcasted_iota(jnp.int32, (rows, LANES), 1) < HEAD_DIM
    for c in range(WIN_GROUP * rows // SOFTMAX_ROWS):
        rs = slice(c * SOFTMAX_ROWS, (c + 1) * SOFTMAX_ROWS)
        s = s_ref[rs, :]
        if masks is not None:
            band, w = masks
            r0 = (c * SOFTMAX_ROWS) % rows
            s = jnp.concatenate([s[:, :w] + band[r0:r0 + SOFTMAX_ROWS], s[:, w:]], axis=1)
        sk = sinks[(c * SOFTMAX_ROWS) // rows]
        m = jnp.maximum(jnp.max(s, axis=-1, keepdims=True), sk)
        e = jnp.exp2(s - m)
        inv_ref[rs, :] = 1.0 / (jnp.sum(e, axis=-1, keepdims=True) + jnp.exp2(sk - m))
        p_ref[rs, :] = e.astype(BF16)
    o = _dot(p_ref[...], vals) * inv_ref[...]
    for i in range(WIN_GROUP // 2):
        a = o[(2 * i) * rows:(2 * i + 1) * rows]
        b = o[(2 * i + 1) * rows:(2 * i + 2) * rows]
        sl = slice((first_slab + i) * LANES, (first_slab + i + 1) * LANES)
        out_slab(sl)[...] = jnp.where(lo_lanes, a, b).astype(BF16)


def _gqa_scratch(rows, n_keys):
    one = [pltpu.VMEM((WIN_GROUP * rows, n_keys), F32), pltpu.VMEM((WIN_GROUP * rows, n_keys), BF16),
           pltpu.VMEM((WIN_GROUP * rows, 1), F32)]
    return one * WIN_KV_HEADS


def _group_sinks(sink_ref, kv):
    return [sink_ref[kv * WIN_GROUP + g] * LOG2E for g in range(WIN_GROUP)]


def _group_slab(kv, i):
    return slice((kv * (WIN_GROUP // 2) + i) * LANES, (kv * (WIN_GROUP // 2) + i + 1) * LANES)


def _window_bands():
    i = np.arange(WIN_BLOCK)[:, None]
    j = np.arange(3 * WIN_BLOCK)[None, :]
    bands = [np.where(np.abs(j - off - i) <= WIN_RADIUS, 0.0, -np.inf) for off in (0, WIN_BLOCK, 2 * WIN_BLOCK)]
    return jnp.asarray(np.stack(bands).astype(np.float32))


def _win_attn_kernel(blocks_per_step, sink_ref, band_ref, q_ref, k_ref, v_ref, kx_ref, vx_ref, o_ref,
                     s0_ref, s1_ref, p_ref, inv_ref):
    t = pl.program_id(1)
    nb = SEQ // WIN_BLOCK
    nw = 3 * WIN_BLOCK

    def place(j):
        n = t * blocks_per_step + j
        start = pl.multiple_of(jnp.clip((n - 1) * WIN_BLOCK, 0, SEQ - nw), WIN_BLOCK)
        sel = jnp.where(n == 0, 0, jnp.where(n == nb - 1, 2, 1))
        return pl.multiple_of(j * WIN_BLOCK, WIN_BLOCK), start, sel

    def scores(j, kv, s_ref):
        qoff, start, _ = place(j)
        keys = jnp.concatenate([k_ref[0, kv, pl.ds(start, nw), :], kx_ref[0, kv]], axis=0)
        q_slabs = [q_ref[0, pl.ds(qoff, WIN_BLOCK), _group_slab(kv, i)] for i in range(WIN_GROUP // 2)]
        _gqa_scores(q_slabs, keys, WIN_BLOCK, s_ref)

    def attend(j, kv, s_ref):
        qoff, start, sel = place(j)
        vals = jnp.concatenate([v_ref[0, kv, pl.ds(start, nw), :], vx_ref[0, kv]], axis=0)
        _gqa_attend(vals, (band_ref[sel], nw), _group_sinks(sink_ref, kv), WIN_BLOCK,
                    lambda sl: o_ref.at[0, pl.ds(qoff, WIN_BLOCK), sl], kv * (WIN_GROUP // 2),
                    (s_ref, p_ref, inv_ref))

    scores(0, 0, s0_ref)

    def one_block(j, carry):
        scores(j, 1, s1_ref)
        attend(j, 0, s0_ref)
        scores(jnp.minimum(j + 1, blocks_per_step - 1), 0, s0_ref)
        attend(j, 1, s1_ref)
        return carry

    lax.fori_loop(0, blocks_per_step, one_block, 0)


def _win_attention(q, k, v, kx, vx, sink):
    B, L, _ = q.shape
    blocks_per_step = 8
    rows = blocks_per_step * WIN_BLOCK
    n_keys = 3 * WIN_BLOCK + CTX_LEN
    full = pl.BlockSpec((1, WIN_KV_HEADS, L, LANES), lambda b, t: (b, 0, 0, 0))
    ctx_spec = pl.BlockSpec((1, WIN_KV_HEADS, CTX_LEN, LANES), lambda b, t: (0, 0, b, 0))
    tok = pl.BlockSpec((1, rows, QW), lambda b, t: (b, t, 0))
    return pl.pallas_call(
        functools.partial(_win_attn_kernel, blocks_per_step),
        grid=(B, L // rows),
        in_specs=[pl.BlockSpec(memory_space=pltpu.SMEM),
                  _const_spec((3, WIN_BLOCK, 3 * WIN_BLOCK)),
                  tok, full, full, ctx_spec, ctx_spec],
        out_specs=tok,
        out_shape=jax.ShapeDtypeStruct((B, L, QW), BF16),
        scratch_shapes=[pltpu.VMEM((WIN_GROUP * WIN_BLOCK, n_keys), F32), pltpu.VMEM((WIN_GROUP * WIN_BLOCK, n_keys), F32),
                        pltpu.VMEM((WIN_GROUP * WIN_BLOCK, n_keys), BF16), pltpu.VMEM((WIN_GROUP * WIN_BLOCK, 1), F32)],
        compiler_params=_params("arbitrary", "arbitrary"),
        name="window_attention",
    )(sink, _window_bands(), q, k, v, kx, vx)


def _ctx_attn_even_kernel(sink_ref, q_ref, k_ref, v_ref, o_ref, *scratch):
    for kv in range(WIN_KV_HEADS):
        q_slabs = [q_ref[0, :, _group_slab(kv, i)] for i in range(WIN_GROUP // 2)]
        _gqa_group(q_slabs, k_ref[0, kv], v_ref[0, kv], None, _group_sinks(sink_ref, kv), CTX_LEN, o_ref,
                   kv * (WIN_GROUP // 2), scratch[3 * kv:3 * kv + 3])


def _ctx_attention_even(q, k, v, sink):
    B = q.shape[0]
    kv_spec = pl.BlockSpec((1, WIN_KV_HEADS, CTX_LEN, LANES), lambda b: (0, 0, b, 0))
    return pl.pallas_call(
        _ctx_attn_even_kernel,
        grid=(B,),
        in_specs=[pl.BlockSpec(memory_space=pltpu.SMEM),
                  pl.BlockSpec((1, CTX_LEN, QW), lambda b: (b, 0, 0)),
                  kv_spec, kv_spec],
        out_specs=pl.BlockSpec((1, CTX_LEN, QW), lambda b: (b, 0, 0)),
        out_shape=jax.ShapeDtypeStruct((B, CTX_LEN, QW), BF16),
        scratch_shapes=_gqa_scratch(CTX_LEN, CTX_LEN),
        compiler_params=_params("arbitrary"),
        name="ctx_attention_even",
    )(sink, q, k, v)


def _out_ffn_kernel(mode, *refs):
    if mode == "odd":
        a_ref, x_ref, mod_ref, g_ref, wo_ref, wg_ref, wu_ref, wd_ref, o_ref = refs
    else:
        f_ref, a_ref, x_ref, mod_ref, g_ref, wo_ref, wg_ref, wu_ref, wd_ref, o_ref = refs
    for r in range(x_ref.shape[1] // SUB_ROWS):
        rows = slice(r * SUB_ROWS, (r + 1) * SUB_ROWS)
        if mode == "even_latent":
            planes = range(r * SUB_ROWS // FFT_N, (r + 1) * SUB_ROWS // FFT_N)
            fm = jnp.concatenate([f_ref[0, :, j, :] for j in planes], axis=0).astype(BF16)
            o = _dot(fm, wo_ref[:FOURIER_WIDTH]) + _dot(a_ref[0, rows], wo_ref[FOURIER_WIDTH:])
        elif mode == "even_ctx":
            o = _dot(f_ref[0, rows].astype(BF16), wo_ref[:FOURIER_WIDTH]) + _dot(a_ref[0, rows], wo_ref[FOURIER_WIDTH:])
        else:
            o = _dot(jnp.concatenate([a_ref[0, j, rows] for j in range(a_ref.shape[1])], axis=1), wo_ref[...])
        x1 = x_ref[0, rows] + _mod_slice(mod_ref, 2) * o
        h = _rms_mod(x1, g_ref[...], _mod_slice(mod_ref, 4), _mod_slice(mod_ref, 3)).astype(BF16)
        acc = jnp.zeros_like(x1)
        for c in range(D_FF // FF_CHUNK):
            cs = slice(c * FF_CHUNK, (c + 1) * FF_CHUNK)
            a = _silu(_dot(h, wg_ref[0, :, cs])) * _dot(h, wu_ref[0, :, cs])
            acc = acc + _dot(a.astype(BF16), wd_ref[0, cs, :])
        o_ref[0, rows] = x1 + _mod_slice(mod_ref, 5) * acc


def _out_ffn(mode, mix, x, mod, mod_row, g, w_out, layer, wg, wu, wd, tm):
    B, L, _ = x.shape
    tok = lambda w: pl.BlockSpec((1, tm, w), lambda b, t: (b, t, 0))
    ffn_spec = lambda r, c: pl.BlockSpec((1, r, c), lambda b, t: (layer, 0, 0), pipeline_mode=pl.Buffered(1))
    if mode == "even_latent":
        assert tm % FFT_N == 0
        mix_specs = [pl.BlockSpec((1, FFT_N, tm // FFT_N, FOURIER_WIDTH), lambda b, t: (b, 0, t, 0)), tok(QW)]
    elif mode == "even_ctx":
        mix_specs = [tok(FOURIER_WIDTH), tok(QW)]
    else:
        mix_specs = [pl.BlockSpec((1, NA_WIDTH // LANES, tm, LANES), lambda b, t: (b, 0, t, 0))]
    return pl.pallas_call(
        functools.partial(_out_ffn_kernel, mode),
        grid=(B, L // tm),
        in_specs=mix_specs + [tok(D_MODEL),
                              pl.BlockSpec((1, 1, 6 * D_MODEL), lambda b, t: (mod_row(b), 0, 0)),
                              _const_spec((1, D_MODEL)),
                              _const_spec((D_MODEL, D_MODEL)),
                              ffn_spec(D_MODEL, D_FF), ffn_spec(D_MODEL, D_FF), ffn_spec(D_FF, D_MODEL)],
        out_specs=tok(D_MODEL),
        out_shape=jax.ShapeDtypeStruct((B, L, D_MODEL), F32),
        compiler_params=_params("arbitrary", "arbitrary"),
        name="out_ffn_" + mode,
    )(*mix, x, mod, g, w_out, wg, wu, wd)


def _in_odd_kernel(with_q, x_ref, mod_ref, g_ref, w32_ref, qg_ref, kg_ref, hm_ref, *outs):
    W = NA_WIDTH
    slabs = RMS_W // LANES
    if with_q:
        q_ref, k_ref, v_ref, w_ref = outs
    else:
        k_ref, v_ref, w_ref = outs
    _cast_once(w32_ref, w_ref)

    for r in range(x_ref.shape[1] // SUB_ROWS):
        rows = slice(r * SUB_ROWS, (r + 1) * SUB_ROWS)
        h = _rms_mod(x_ref[0, rows], g_ref[...], _mod_slice(mod_ref, 1), _mod_slice(mod_ref, 0)).astype(BF16)

        def normed(t, gain_ref, scale, o_ref):
            for j in range(W // RMS_W):
                tj = (_head_rms(t[:, j * RMS_W:(j + 1) * RMS_W], gain_ref, hm_ref) * scale).astype(BF16)
                for i in range(slabs):
                    o_ref[0, j * slabs + i, rows] = tj[:, i * LANES:(i + 1) * LANES]

        if with_q:
            normed(_dot(h, w_ref[:, :W]), qg_ref, QK_SCALE, q_ref)
        normed(_dot(h, w_ref[:, W:2 * W]), kg_ref, 1.0, k_ref)
        v = _dot(h, w_ref[:, 2 * W:]).astype(BF16)
        for j in range(W // LANES):
            v_ref[0, j, rows] = v[:, j * LANES:(j + 1) * LANES]


def _in_odd(x, mod, mod_row, g, w_in, q_g, k_g, hm, with_q, tm):
    B, L, _ = x.shape
    pairs = NA_WIDTH // LANES
    tok = pl.BlockSpec((1, pairs, tm, LANES), lambda b, t: (b, 0, t, 0))
    n_out = 3 if with_q else 2
    return pl.pallas_call(
        functools.partial(_in_odd_kernel, with_q),
        grid=(B, L // tm),
        in_specs=[pl.BlockSpec((1, tm, D_MODEL), lambda b, t: (b, t, 0)),
                  pl.BlockSpec((1, 1, 6 * D_MODEL), lambda b, t: (mod_row(b), 0, 0)),
                  _const_spec((1, D_MODEL)),
                  _const_spec((D_MODEL, 3 * NA_WIDTH)),
                  _const_spec((1, RMS_W)), _const_spec((1, RMS_W)), _const_spec((RMS_W, RMS_W))],
        out_specs=[tok] * n_out,
        out_shape=[jax.ShapeDtypeStruct((B, pairs, L, LANES), BF16)] * n_out,
        compiler_params=_params("arbitrary", "arbitrary"),
        scratch_shapes=[pltpu.VMEM((D_MODEL, 3 * NA_WIDTH), BF16)],
        name="in_odd" if with_q else "in_odd_ctx",
    )(x, mod, g, w_in, q_g, k_g, hm)


def _bias_table_kernel(rb_ref, idx_ref, o_ref):
    idx = idx_ref[...]
    cq = lax.broadcasted_iota(jnp.int32, idx.shape, 0)
    ck = lax.broadcasted_iota(jnp.int32, idx.shape, 1) % GRID_W
    c0 = jnp.clip(cq - NA_KW // 2, 0, GRID_W - NA_KW)
    inside = (ck >= c0) & (ck < c0 + NA_KW)
    for h in range(o_ref.shape[0]):
        for dr in range(o_ref.shape[1]):
            row = jnp.broadcast_to(rb_ref[h, dr:dr + 1, :], idx.shape)
            o_ref[h, dr] = jnp.where(inside, jnp.take_along_axis(row, idx, axis=1) * LOG2E, -jnp.inf)


def _na_bias_table(rel_bias, idx):
    H = NA_HEADS
    npair = 2 * NA_KH - 2
    half = LANES // 2
    hb = 4
    pad = lambda a: jnp.pad(a, ((0, 0), (0, 0), (0, half - a.shape[-1])))
    rb2 = jnp.concatenate([pad(rel_bias[:, :-1]), pad(rel_bias[:, 1:])], axis=-1)
    return pl.pallas_call(
        _bias_table_kernel,
        grid=(H // hb,),
        in_specs=[pl.BlockSpec((hb, npair, LANES), lambda h: (h, 0, 0)), _const_spec((GRID_W, LANES))],
        out_specs=pl.BlockSpec((hb, npair, GRID_W, LANES), lambda h: (h, 0, 0, 0)),
        out_shape=jax.ShapeDtypeStruct((H, npair, GRID_W, LANES), F32),
        compiler_params=_params("arbitrary"),
        name="na_bias_table",
    )(rb2, idx)


def _na_kernel(rows_per_step, q_ref, k_ref, v_ref, kx_ref, vx_ref, bias_ref, o_ref, s0_ref, s1_ref, p_ref, inv_ref):
    t = pl.program_id(1)
    n = NA_KH * GRID_W
    lo_lanes = lax.broadcasted_iota(jnp.int32, (GRID_W, LANES), 1) < HEAD_DIM

    def window(i):
        r = t * rows_per_step + i
        r0 = jnp.clip(r - NA_KH // 2, 0, GRID_ROWS - NA_KH)
        return r0 - r + NA_KH - 1, pl.multiple_of(r0 * GRID_W, GRID_W), pl.multiple_of(i * GRID_W, GRID_W)

    def scores(i, s_ref):
        dr0, start, qoff = window(i)
        for hp in range(NA_HEADS // 2):
            qs = q_ref[0, hp, pl.ds(qoff, GRID_W), :]
            keys = jnp.concatenate([k_ref[0, hp, pl.ds(start, n), :], kx_ref[0, hp]], axis=0)
            qm = jnp.concatenate([jnp.where(lo_lanes, qs, jnp.zeros_like(qs)),
                                  jnp.where(lo_lanes, jnp.zeros_like(qs), qs)], axis=0)
            s = _dot_nt(qm, keys)
            for half in range(2):
                h = 2 * hp + half
                bias = jnp.concatenate([bias_ref[h, dr0 + 2 * p] for p in range(NA_KH // 2)], axis=1)
                rs = slice(h * GRID_W, (h + 1) * GRID_W)
                hr = slice(half * GRID_W, (half + 1) * GRID_W)
                s_ref[rs, :n] = s[hr, :n] + bias
                s_ref[rs, n:] = s[hr, n:]

    def attend(i, s_ref):
        _, start, qoff = window(i)
        for c in range(NA_HEADS * GRID_W // SOFTMAX_ROWS):
            cs = slice(c * SOFTMAX_ROWS, (c + 1) * SOFTMAX_ROWS)
            s = s_ref[cs, :]
            e = jnp.exp2(s - jnp.max(s, axis=-1, keepdims=True))
            inv_ref[cs, :] = 1.0 / jnp.sum(e, axis=-1, keepdims=True)
            p_ref[cs, :] = e.astype(BF16)
        for hp in range(NA_HEADS // 2):
            vals = jnp.concatenate([v_ref[0, hp, pl.ds(start, n), :], vx_ref[0, hp]], axis=0)
            rs = slice(2 * hp * GRID_W, (2 * hp + 2) * GRID_W)
            res = _dot(p_ref[rs, :], vals) * inv_ref[rs, :]
            o_ref[0, hp, pl.ds(qoff, GRID_W), :] = jnp.where(lo_lanes, res[:GRID_W], res[GRID_W:]).astype(BF16)

    last = rows_per_step - 1
    scores(0, s0_ref)

    def two_rows(j, carry):
        scores(2 * j + 1, s1_ref)
        attend(2 * j, s0_ref)
        scores(jnp.minimum(2 * j + 2, last), s0_ref)
        attend(2 * j + 1, s1_ref)
        return carry

    lax.fori_loop(0, rows_per_step // 2, two_rows, 0)


def _neighbourhood_attention(q, k, v, kx, vx, bias_tab):
    B, pairs, L, _ = q.shape
    rows_per_step = 16
    full = pl.BlockSpec((1, pairs, L, LANES), lambda b, t: (b, 0, 0, 0))
    ctx_spec = pl.BlockSpec((1, pairs, CTX_LEN, LANES), lambda b, t: (0, 0, b, 0))
    q_rows = pl.BlockSpec((1, pairs, rows_per_step * GRID_W, LANES), lambda b, t: (b, 0, t, 0))
    n_keys = NA_KH * GRID_W + CTX_LEN
    all_rows = NA_HEADS * GRID_W
    return pl.pallas_call(
        functools.partial(_na_kernel, rows_per_step),
        grid=(B, GRID_ROWS // rows_per_step),
        in_specs=[q_rows, full, full, ctx_spec, ctx_spec, _const_spec(bias_tab.shape)],
        out_specs=q_rows,
        out_shape=jax.ShapeDtypeStruct((B, pairs, L, LANES), BF16),
        scratch_shapes=[pltpu.VMEM((all_rows, n_keys), F32), pltpu.VMEM((all_rows, n_keys), F32),
                        pltpu.VMEM((all_rows, n_keys), BF16), pltpu.VMEM((all_rows, 1), F32)],
        compiler_params=_params("arbitrary", "arbitrary"),
        name="neighbourhood_attention",
    )(q, k, v, kx, vx, bias_tab)


def kernel(x, c, ctx, c_ctx, ada_w, ada_b, norm1_g, norm2_g, ffn_w_gate, ffn_w_up, ffn_w_down,
           ev_w_in, ev_w_out, ev_q_norm, ev_k_norm, ev_sink,
           od_w_in, od_w_out, od_q_norm, od_k_norm, od_rel_bias):
    assert x.shape == (BATCH, SEQ, D_MODEL) and ctx.shape == (BATCH, CTX_LEN, D_MODEL)
    wc, m1, m3, twc, tws, mctx = _fourier_tables()
    hm = _head_mean_matrix()
    rope_tabs = _rope_tables()
    lane_gain = lambda gvec: jnp.tile(gvec, RMS_W // HEAD_DIM).reshape(1, RMS_W)
    lat_row = lambda b: b
    ctx_row = lambda b: CTX_MOD_ROW
    tm = 512
    n_ctx = BATCH * CTX_LEN
    as_seq = lambda a: a.reshape(1, n_ctx, a.shape[-1])
    per_batch = lambda a: a.reshape(BATCH, CTX_LEN, a.shape[-1])

    cs = jnp.concatenate([c, c_ctx[None, :], jnp.zeros((MOD_ROWS - BATCH - 1, D_MODEL), F32)], axis=0)
    mod = _modulation(cs, ada_w, ada_b).reshape(DEPTH, MOD_ROWS, 1, 6 * D_MODEL)

    w_in0 = ev_w_in[0]
    w_out0 = ev_w_out[0].astype(BF16)
    g1 = norm1_g[0].reshape(1, D_MODEL)
    g2 = norm2_g[0].reshape(1, D_MODEL)
    ffn = (ffn_w_gate.astype(BF16), ffn_w_up.astype(BF16), ffn_w_down.astype(BF16))
    qg, kg = lane_gain(ev_q_norm[0]), lane_gain(ev_k_norm[0])
    y_l, q_l, k_l, v_l = _in_even(x, mod[0], lat_row, g1, w_in0, qg, kg, hm, wc, rope_tabs, 2 * tm)
    y_c, q_c, k_c, v_c = _in_even(as_seq(ctx), mod[0], ctx_row, g1, w_in0, qg, kg, hm, wc, None, tm)
    y_c, q_c = per_batch(y_c), per_batch(q_c)
    f_l = _fourier_latent(y_l, m1, m3, twc, tws)
    f_c = _fourier_ctx(y_c, mctx)
    a_l = _win_attention(q_l, k_l, v_l, k_c, v_c, ev_sink[0])
    a_c = _ctx_attention_even(q_c, k_c, v_c, ev_sink[0])
    x1 = _out_ffn("even_latent", (f_l, a_l), x, mod[0], lat_row, g2, w_out0, 0, *ffn, 2 * tm)
    y1 = _out_ffn("even_ctx", (as_seq(f_c), as_seq(a_c)), as_seq(ctx), mod[0], ctx_row, g2, w_out0, 0, *ffn, tm)

    w_in1 = od_w_in[0]
    w_out1 = od_w_out[0].astype(BF16)
    g1 = norm1_g[1].reshape(1, D_MODEL)
    g2 = norm2_g[1].reshape(1, D_MODEL)
    qg, kg = lane_gain(od_q_norm[0]), lane_gain(od_k_norm[0])
    q_l, k_l, v_l = _in_odd(x1, mod[1], lat_row, g1, w_in1, qg, kg, hm, True, 2 * tm)
    k_c, v_c = _in_odd(y1, mod[1], ctx_row, g1, w_in1, qg, kg, hm, False, tm)
    bias_tab = _na_bias_table(od_rel_bias[0], _na_bias_index())
    a_l = _neighbourhood_attention(q_l, k_l, v_l, k_c, v_c, bias_tab)
    return _out_ffn("odd", (a_l,), x1, mod[1], lat_row, g2, w_out1, 1, *ffn, 2 * tm)
```

```python
import functools
import math

import numpy as np
import jax
import jax.numpy as jnp
from jax import lax
from jax.experimental import pallas as pl
from jax.experimental.pallas import tpu as pltpu

D_MODEL = 1024
BATCH = 4
SEQ = 4096
DEPTH = 2
GRID_W = 64
CTX_LEN = 256
HEAD_DIM = 64
EPS = 1e-6
FOURIER_WIDTH = D_MODEL // 2
FOURIER_GROUPS = 4
FOURIER_GROUP_CH = FOURIER_WIDTH // FOURIER_GROUPS
WIN_Q_HEADS = (D_MODEL // 2) // HEAD_DIM
WIN_KV_HEADS = 2
WIN_GROUP = WIN_Q_HEADS // WIN_KV_HEADS
WIN_RADIUS = 128
WIN_BLOCK = 128
QW = WIN_Q_HEADS * HEAD_DIM
KW = WIN_KV_HEADS * HEAD_DIM
EV_IN_WIDTH = FOURIER_WIDTH + QW + 2 * KW
NA_HEADS = D_MODEL // HEAD_DIM
NA_KH = 8
NA_KW = 16
NA_WIDTH = NA_HEADS * HEAD_DIM
ROPE_THETA = 10000.0
ROPE_FREQS = HEAD_DIM // 4
D_FF = ((8 * D_MODEL // 3 + 255) // 256) * 256
GRID_ROWS = SEQ // GRID_W

LANES = 128
MOD_ROWS = 8
CTX_MOD_ROW = BATCH
FFT_N = 64
FF_CHUNK = 256
SOFTMAX_ROWS = 64
RMS_W = 256
SUB_ROWS = 512
VMEM_LIMIT = 60 * 1024 * 1024

LOG2E = math.log2(math.e)
QK_SCALE = LOG2E / math.sqrt(HEAD_DIM)

BF16 = jnp.bfloat16
F32 = jnp.float32

assert DEPTH == 2 and SEQ == FFT_N * FFT_N and D_FF % FF_CHUNK == 0


def _params(*sem):
    return pltpu.CompilerParams(dimension_semantics=sem, vmem_limit_bytes=VMEM_LIMIT)


def _dot(a, b):
    return jnp.dot(a, b, preferred_element_type=F32)


def _dot_nt(a, b):
    return lax.dot_general(a, b, (((1,), (1,)), ((), ())), preferred_element_type=F32)


def _silu(x):
    return x / (1.0 + jnp.exp(-x))


def _const_spec(shape):
    nd = len(shape)
    return pl.BlockSpec(shape, lambda *_: (0,) * nd, pipeline_mode=pl.Buffered(1))


def _dft_cos_sin(n):
    idx = (np.arange(n)[:, None] * np.arange(n)[None, :]) % n
    ang = 2.0 * np.pi * idx / n
    return np.cos(ang), np.sin(ang)


def _fourier_tables():
    cc, sc = _dft_cos_sin(FOURIER_GROUP_CH)
    wc = np.concatenate([cc, -sc], axis=1) / math.sqrt(FOURIER_GROUP_CH)
    c64, s64 = _dft_cos_sin(FFT_N)
    m1 = np.concatenate([c64, -s64], axis=0) / math.sqrt(FFT_N)
    m3 = np.concatenate([c64, s64], axis=1) / math.sqrt(FFT_N)
    tw = (np.arange(FFT_N)[:, None] * np.arange(FFT_N)[None, :]) % SEQ
    tw = 2.0 * np.pi * tw / SEQ
    twc = np.repeat(np.cos(tw)[:, :, None], LANES, axis=2)
    tws = np.repeat(np.sin(tw)[:, :, None], LANES, axis=2)
    cx, sx = _dft_cos_sin(CTX_LEN)
    mctx = np.concatenate([cx, sx], axis=1) / math.sqrt(CTX_LEN)
    as32 = lambda a: jnp.asarray(a, F32)
    return (as32(wc).astype(BF16), as32(m1).astype(BF16), as32(m3).astype(BF16),
            as32(twc), as32(tws), as32(mctx).astype(BF16))


def _head_mean_matrix():
    blk = np.kron(np.eye(RMS_W // HEAD_DIM), np.ones((HEAD_DIM, HEAD_DIM))) / HEAD_DIM
    return jnp.asarray(blk, BF16)


def _rope_tables():
    t = jnp.arange(SEQ, dtype=jnp.int32)
    row = (t // GRID_W).astype(F32)
    col = (t % GRID_W).astype(F32)
    inv = ROPE_THETA ** (-jnp.arange(ROPE_FREQS, dtype=F32) / ROPE_FREQS)
    ang_row = row[:, None] * inv[None, :]
    ang_col = col[:, None] * inv[None, :]
    zero = jnp.zeros_like(ang_row)
    cos = jnp.concatenate([jnp.cos(ang_row)] * 2 + [jnp.cos(ang_col)] * 2, axis=1)
    sin_hi = jnp.concatenate([-jnp.sin(ang_row), zero, -jnp.sin(ang_col), zero], axis=1)
    sin_lo = jnp.concatenate([zero, jnp.sin(ang_row), zero, jnp.sin(ang_col)], axis=1)
    rep = LANES // HEAD_DIM
    return jnp.tile(cos, (1, rep)), jnp.tile(sin_hi, (1, rep)), jnp.tile(sin_lo, (1, rep))


def _na_bias_index():
    cq = np.arange(GRID_W)
    dc = np.clip(cq[None, :] - cq[:, None] + NA_KW - 1, 0, 2 * NA_KW - 2)
    return jnp.asarray(np.concatenate([dc, dc + LANES // 2], axis=1), jnp.int32)


def _mod_kernel(cs_ref, w_ref, b_ref, o_ref):
    s = _silu(cs_ref[...]).astype(BF16)
    o_ref[0] = _dot(s, w_ref[0].astype(BF16)) + b_ref[0]


def _modulation(cs, ada_w, ada_b):
    tn = 1536
    return pl.pallas_call(
        _mod_kernel,
        grid=(DEPTH, 6 * D_MODEL // tn),
        in_specs=[pl.BlockSpec((MOD_ROWS, D_MODEL), lambda i, j: (0, 0)),
                  pl.BlockSpec((1, D_MODEL, tn), lambda i, j: (i, 0, j)),
                  pl.BlockSpec((1, 1, tn), lambda i, j: (i, 0, j))],
        out_specs=pl.BlockSpec((1, MOD_ROWS, tn), lambda i, j: (i, 0, j)),
        out_shape=jax.ShapeDtypeStruct((DEPTH, MOD_ROWS, 6 * D_MODEL), F32),
        compiler_params=_params("arbitrary", "arbitrary"),
        name="ada_modulation",
    )(cs, ada_w, ada_b.reshape(DEPTH, 1, 6 * D_MODEL))


def _mod_slice(mod_ref, k):
    return mod_ref[0, :, k * D_MODEL:(k + 1) * D_MODEL]


def _rms_mod(x, g, scale, shift):
    y = x * lax.rsqrt(jnp.mean(x * x, axis=-1, keepdims=True) + EPS)
    return (y * g) * (1.0 + scale) + shift


def _head_rms(t, gain_ref, hm_ref):
    w = t.shape[1]
    ms = _dot((t * t).astype(BF16), hm_ref[:w, :w])
    return t * lax.rsqrt(ms + EPS) * gain_ref[:, :w]


def _rope(t, cos, sin_hi, sin_lo):
    up = pltpu.roll(t, LANES - ROPE_FREQS, axis=1)
    dn = pltpu.roll(t, ROPE_FREQS, axis=1)
    return t * cos + up * sin_hi + dn * sin_lo


def _cast_once(w32_ref, w_ref):
    @pl.when((pl.program_id(0) == 0) & (pl.program_id(1) == 0))
    def _():
        w_ref[...] = w32_ref[...].astype(BF16)


def _in_even_kernel(latent, x_ref, mod_ref, g_ref, w32_ref, qg_ref, kg_ref, hm_ref, wc_ref, *rest):
    if latent:
        cos_ref, sh_ref, sl_ref, y_ref, q_ref, k_ref, v_ref, w_ref = rest
    else:
        y_ref, q_ref, k_ref, v_ref, w_ref = rest
    _cast_once(w32_ref, w_ref)
    F = FOURIER_WIDTH
    lo_lanes = lax.broadcasted_iota(jnp.int32, (1, LANES), 1) < HEAD_DIM

    for r in range(x_ref.shape[1] // SUB_ROWS):
        rows = slice(r * SUB_ROWS, (r + 1) * SUB_ROWS)
        h = _rms_mod(x_ref[0, rows], g_ref[...], _mod_slice(mod_ref, 1), _mod_slice(mod_ref, 0)).astype(BF16)
        f = _dot(h, w_ref[:, :F]).astype(BF16)
        for g in range(FOURIER_GROUPS):
            yg = _dot(f[:, g * LANES:(g + 1) * LANES], wc_ref[...])
            y_ref[0, rows, g * LANES:(g + 1) * LANES] = yg[:, :LANES].astype(y_ref.dtype)
            y_ref[0, rows, F + g * LANES:F + (g + 1) * LANES] = yg[:, LANES:].astype(y_ref.dtype)

        def qk(t, gain_ref):
            t = _head_rms(t, gain_ref, hm_ref)
            if latent:
                t = jnp.concatenate([_rope(t[:, i * LANES:(i + 1) * LANES], cos_ref[rows], sh_ref[rows], sl_ref[rows])
                                     for i in range(t.shape[1] // LANES)], axis=1)
            return t

        q = _dot(h, w_ref[:, F:F + QW])
        for j in range(QW // RMS_W):
            qj = qk(q[:, j * RMS_W:(j + 1) * RMS_W], qg_ref) * QK_SCALE
            q_ref[0, rows, j * RMS_W:(j + 1) * RMS_W] = qj.astype(BF16)
        kv = _dot(h, w_ref[:, F + QW:])

        def store_dup(t, o_ref):
            sw = pltpu.roll(t, HEAD_DIM, axis=1)
            o_ref[0, 0, rows] = jnp.where(lo_lanes, t, sw).astype(BF16)
            o_ref[0, 1, rows] = jnp.where(lo_lanes, sw, t).astype(BF16)

        store_dup(qk(kv[:, :KW], kg_ref), k_ref)
        store_dup(kv[:, KW:], v_ref)


def _in_even(x, mod, mod_row, g, w_in, q_g, k_g, hm, wc, rope_tabs, tm):
    B, L, _ = x.shape
    latent = rope_tabs is not None
    in_specs = [pl.BlockSpec((1, tm, D_MODEL), lambda b, t: (b, t, 0)),
                pl.BlockSpec((1, 1, 6 * D_MODEL), lambda b, t: (mod_row(b), 0, 0)),
                _const_spec((1, D_MODEL)),
                _const_spec((D_MODEL, EV_IN_WIDTH)),
                _const_spec((1, RMS_W)), _const_spec((1, RMS_W)),
                _const_spec((RMS_W, RMS_W)), _const_spec((LANES, 2 * LANES))]
    args = [x, mod, g, w_in, q_g, k_g, hm, wc]
    tok = lambda w: pl.BlockSpec((1, tm, w), lambda b, t: (b, t, 0))
    kv_spec = pl.BlockSpec((1, WIN_KV_HEADS, tm, LANES), lambda b, t: (b, 0, t, 0))
    if latent:
        in_specs += [pl.BlockSpec((tm, LANES), lambda b, t: (t, 0))] * 3
        args += list(rope_tabs)
    return pl.pallas_call(
        functools.partial(_in_even_kernel, latent),
        grid=(B, L // tm),
        in_specs=in_specs,
        out_specs=[tok(2 * FOURIER_WIDTH), tok(QW), kv_spec, kv_spec],
        out_shape=[jax.ShapeDtypeStruct((B, L, 2 * FOURIER_WIDTH), F32 if latent else BF16),
                   jax.ShapeDtypeStruct((B, L, QW), BF16),
                   jax.ShapeDtypeStruct((B, WIN_KV_HEADS, L, LANES), BF16),
                   jax.ShapeDtypeStruct((B, WIN_KV_HEADS, L, LANES), BF16)],
        compiler_params=_params("arbitrary", "arbitrary"),
        scratch_shapes=[pltpu.VMEM((D_MODEL, EV_IN_WIDTH), BF16)],
        name="in_even_latent" if latent else "in_even_ctx",
    )(*args)


def _fft_stage1_kernel(n2t, y_ref, m1_ref, twc_ref, tws_ref, o_ref, row_ref):
    F = FOURIER_WIDTH
    for j in range(n2t):
        row_ref[j % 2] = y_ref[0, :, j, :]
        a = _dot(m1_ref[...], row_ref[j % 2].astype(BF16))
        top, bot = a[:FFT_N], a[FFT_N:]
        ar = top[:, :F] - bot[:, F:]
        ai = top[:, F:] + bot[:, :F]
        tc = jnp.tile(twc_ref[j], (1, F // LANES))
        ts = jnp.tile(tws_ref[j], (1, F // LANES))
        o_ref[0, 0, :, j * F:(j + 1) * F] = (ar * tc + ai * ts).astype(BF16)
        o_ref[0, 1, :, j * F:(j + 1) * F] = (ai * tc - ar * ts).astype(BF16)


def _fft_stage2_kernel(k1t, b_ref, m3_ref, o_ref):
    for j in range(k1t):
        rhs = jnp.concatenate([b_ref[0, 0, j], b_ref[0, 1, j]], axis=0)
        o_ref[0, j] = _dot(m3_ref[...], rhs)


def _fourier_latent(y, m1, m3, twc, tws):
    B = y.shape[0]
    F = FOURIER_WIDTH
    n2t = 16
    yv = y.reshape(B, FFT_N, FFT_N, 2 * F)
    bh = pl.pallas_call(
        functools.partial(_fft_stage1_kernel, n2t),
        grid=(B, FFT_N // n2t),
        in_specs=[pl.BlockSpec((1, FFT_N, n2t, 2 * F), lambda b, t: (b, 0, t, 0)),
                  _const_spec((2 * FFT_N, FFT_N)),
                  pl.BlockSpec((n2t, FFT_N, LANES), lambda b, t: (t, 0, 0)),
                  pl.BlockSpec((n2t, FFT_N, LANES), lambda b, t: (t, 0, 0))],
        out_specs=pl.BlockSpec((1, 2, FFT_N, n2t * F), lambda b, t: (b, 0, 0, t)),
        out_shape=jax.ShapeDtypeStruct((B, 2, FFT_N, FFT_N * F), BF16),
        scratch_shapes=[pltpu.VMEM((2, FFT_N, 2 * F), F32)],
        compiler_params=_params("arbitrary", "arbitrary"),
        name="fft_stage1",
    )(yv, m1, twc, tws)
    k1t = 16
    bv = bh.reshape(B, 2, FFT_N, FFT_N, F)
    return pl.pallas_call(
        functools.partial(_fft_stage2_kernel, k1t),
        grid=(B, FFT_N // k1t),
        in_specs=[pl.BlockSpec((1, 2, k1t, FFT_N, F), lambda b, t: (b, 0, t, 0, 0)),
                  _const_spec((FFT_N, 2 * FFT_N))],
        out_specs=pl.BlockSpec((1, k1t, FFT_N, F), lambda b, t: (b, t, 0, 0)),
        out_shape=jax.ShapeDtypeStruct((B, FFT_N, FFT_N, F), F32),
        compiler_params=_params("arbitrary", "arbitrary"),
        name="fft_stage2",
    )(bv, m3)


def _fourier_ctx_kernel(y_ref, m_ref, o_ref):
    F = FOURIER_WIDTH
    rhs = jnp.concatenate([y_ref[0, :, :F], y_ref[0, :, F:]], axis=0)
    o_ref[0] = _dot(m_ref[...], rhs)


def _fourier_ctx(y, mctx):
    B = y.shape[0]
    return pl.pallas_call(
        _fourier_ctx_kernel,
        grid=(B,),
        in_specs=[pl.BlockSpec((1, CTX_LEN, 2 * FOURIER_WIDTH), lambda b: (b, 0, 0)),
                  _const_spec((CTX_LEN, 2 * CTX_LEN))],
        out_specs=pl.BlockSpec((1, CTX_LEN, FOURIER_WIDTH), lambda b: (b, 0, 0)),
        out_shape=jax.ShapeDtypeStruct((B, CTX_LEN, FOURIER_WIDTH), F32),
        compiler_params=_params("arbitrary"),
        name="fourier_ctx",
    )(y, mctx)


def _gqa_group(q_slabs, keys, vals, masks, sinks, rows, o_ref, first_slab, scratch):
    s_ref, p_ref, inv_ref = scratch
    _gqa_scores(q_slabs, keys, rows, s_ref)
    _gqa_attend(vals, masks, sinks, rows, lambda sl: o_ref.at[0, :, sl], first_slab, scratch)


def _gqa_scores(q_slabs, keys, rows, s_ref):
    lo_lanes = lax.broadcasted_iota(jnp.int32, (rows, LANES), 1) < HEAD_DIM
    zero = jnp.zeros((rows, LANES), BF16)
    qs = jnp.concatenate([jnp.where(lo_lanes if half == 0 else ~lo_lanes, qslab, zero)
                          for qslab in q_slabs for half in range(2)], axis=0)
    s_ref[...] = _dot_nt(qs, keys)


def _gqa_attend(vals, masks, sinks, rows, out_slab, first_slab, scratch):
    s_ref, p_ref, inv_ref = scratch
    lo_lanes = lax.broadcasted_iota(jnp.int32, (rows, LANES), 1) < HEAD_DIM
    for c in range(WIN_GROUP * rows // SOFTMAX_ROWS):
        rs = slice(c * SOFTMAX_ROWS, (c + 1) * SOFTMAX_ROWS)
        s = s_ref[rs, :]
        if masks is not None:
            band, w = masks
            r0 = (c * SOFTMAX_ROWS) % rows
            s = jnp.concatenate([s[:, :w] + band[r0:r0 + SOFTMAX_ROWS], s[:, w:]], axis=1)
        sk = sinks[(c * SOFTMAX_ROWS) // rows]
        m = jnp.maximum(jnp.max(s, axis=-1, keepdims=True), sk)
        e = jnp.exp2(s - m)
        inv_ref[rs, :] = 1.0 / (jnp.sum(e, axis=-1, keepdims=True) + jnp.exp2(sk - m))
        p_ref[rs, :] = e.astype(BF16)
    o = _dot(p_ref[...], vals) * inv_ref[...]
    for i in range(WIN_GROUP // 2):
        a = o[(2 * i) * rows:(2 * i + 1) * rows]
        b = o[(2 * i + 1) * rows:(2 * i + 2) * rows]
        sl = slice((first_slab + i) * LANES, (first_slab + i + 1) * LANES)
        out_slab(sl)[...] = jnp.where(lo_lanes, a, b).astype(BF16)


def _gqa_scratch(rows, n_keys):
    one = [pltpu.VMEM((WIN_GROUP * rows, n_keys), F32), pltpu.VMEM((WIN_GROUP * rows, n_keys), BF16),
           pltpu.VMEM((WIN_GROUP * rows, 1), F32)]
    return one * WIN_KV_HEADS


def _group_sinks(sink_ref, kv):
    return [sink_ref[kv * WIN_GROUP + g] * LOG2E for g in range(WIN_GROUP)]


def _group_slab(kv, i):
    return slice((kv * (WIN_GROUP // 2) + i) * LANES, (kv * (WIN_GROUP // 2) + i + 1) * LANES)


def _window_bands():
    i = np.arange(WIN_BLOCK)[:, None]
    j = np.arange(3 * WIN_BLOCK)[None, :]
    bands = [np.where(np.abs(j - off - i) <= WIN_RADIUS, 0.0, -np.inf) for off in (0, WIN_BLOCK, 2 * WIN_BLOCK)]
    return jnp.asarray(np.stack(bands).astype(np.float32))


def _win_attn_kernel(blocks_per_step, sink_ref, band_ref, q_ref, k_ref, v_ref, kx_ref, vx_ref, o_ref,
                     s0_ref, s1_ref, p_ref, inv_ref):
    t = pl.program_id(1)
    nb = SEQ // WIN_BLOCK
    nw = 3 * WIN_BLOCK

    def place(j):
        n = t * blocks_per_step + j
        start = pl.multiple_of(jnp.clip((n - 1) * WIN_BLOCK, 0, SEQ - nw), WIN_BLOCK)
        sel = jnp.where(n == 0, 0, jnp.where(n == nb - 1, 2, 1))
        return pl.multiple_of(j * WIN_BLOCK, WIN_BLOCK), start, sel

    def scores(j, kv, s_ref):
        qoff, start, _ = place(j)
        keys = jnp.concatenate([k_ref[0, kv, pl.ds(start, nw), :], kx_ref[0, kv]], axis=0)
        q_slabs = [q_ref[0, pl.ds(qoff, WIN_BLOCK), _group_slab(kv, i)] for i in range(WIN_GROUP // 2)]
        _gqa_scores(q_slabs, keys, WIN_BLOCK, s_ref)

    def attend(j, kv, s_ref):
        qoff, start, sel = place(j)
        vals = jnp.concatenate([v_ref[0, kv, pl.ds(start, nw), :], vx_ref[0, kv]], axis=0)
        _gqa_attend(vals, (band_ref[sel], nw), _group_sinks(sink_ref, kv), WIN_BLOCK,
                    lambda sl: o_ref.at[0, pl.ds(qoff, WIN_BLOCK), sl], kv * (WIN_GROUP // 2),
                    (s_ref, p_ref, inv_ref))

    scores(0, 0, s0_ref)

    def one_block(j, carry):
        scores(j, 1, s1_ref)
        attend(j, 0, s0_ref)
        scores(jnp.minimum(j + 1, blocks_per_step - 1), 0, s0_ref)
        attend(j, 1, s1_ref)
        return carry

    lax.fori_loop(0, blocks_per_step, one_block, 0)


def _win_attention(q, k, v, kx, vx, sink):
    B, L, _ = q.shape
    blocks_per_step = 8
    rows = blocks_per_step * WIN_BLOCK
    n_keys = 3 * WIN_BLOCK + CTX_LEN
    full = pl.BlockSpec((1, WIN_KV_HEADS, L, LANES), lambda b, t: (b, 0, 0, 0))
    ctx_spec = pl.BlockSpec((1, WIN_KV_HEADS, CTX_LEN, LANES), lambda b, t: (0, 0, b, 0))
    tok = pl.BlockSpec((1, rows, QW), lambda b, t: (b, t, 0))
    return pl.pallas_call(
        functools.partial(_win_attn_kernel, blocks_per_step),
        grid=(B, L // rows),
        in_specs=[pl.BlockSpec(memory_space=pltpu.SMEM),
                  _const_spec((3, WIN_BLOCK, 3 * WIN_BLOCK)),
                  tok, full, full, ctx_spec, ctx_spec],
        out_specs=tok,
        out_shape=jax.ShapeDtypeStruct((B, L, QW), BF16),
        scratch_shapes=[pltpu.VMEM((WIN_GROUP * WIN_BLOCK, n_keys), F32), pltpu.VMEM((WIN_GROUP * WIN_BLOCK, n_keys), F32),
                        pltpu.VMEM((WIN_GROUP * WIN_BLOCK, n_keys), BF16), pltpu.VMEM((WIN_GROUP * WIN_BLOCK, 1), F32)],
        compiler_params=_params("arbitrary", "arbitrary"),
        name="window_attention",
    )(sink, _window_bands(), q, k, v, kx, vx)


def _ctx_attn_even_kernel(sink_ref, q_ref, k_ref, v_ref, o_ref, *scratch):
    for kv in range(WIN_KV_HEADS):
        q_slabs = [q_ref[0, :, _group_slab(kv, i)] for i in range(WIN_GROUP // 2)]
        _gqa_group(q_slabs, k_ref[0, kv], v_ref[0, kv], None, _group_sinks(sink_ref, kv), CTX_LEN, o_ref,
                   kv * (WIN_GROUP // 2), scratch[3 * kv:3 * kv + 3])


def _ctx_attention_even(q, k, v, sink):
    B = q.shape[0]
    kv_spec = pl.BlockSpec((1, WIN_KV_HEADS, CTX_LEN, LANES), lambda b: (0, 0, b, 0))
    return pl.pallas_call(
        _ctx_attn_even_kernel,
        grid=(B,),
        in_specs=[pl.BlockSpec(memory_space=pltpu.SMEM),
                  pl.BlockSpec((1, CTX_LEN, QW), lambda b: (b, 0, 0)),
                  kv_spec, kv_spec],
        out_specs=pl.BlockSpec((1, CTX_LEN, QW), lambda b: (b, 0, 0)),
        out_shape=jax.ShapeDtypeStruct((B, CTX_LEN, QW), BF16),
        scratch_shapes=_gqa_scratch(CTX_LEN, CTX_LEN),
        compiler_params=_params("arbitrary"),
        name="ctx_attention_even",
    )(sink, q, k, v)


def _out_ffn_kernel(mode, *refs):
    if mode == "odd":
        a_ref, x_ref, mod_ref, g_ref, wo_ref, wg_ref, wu_ref, wd_ref, o_ref = refs
    else:
        f_ref, a_ref, x_ref, mod_ref, g_ref, wo_ref, wg_ref, wu_ref, wd_ref, o_ref = refs
    for r in range(x_ref.shape[1] // SUB_ROWS):
        rows = slice(r * SUB_ROWS, (r + 1) * SUB_ROWS)
        if mode == "even_latent":
            planes = range(r * SUB_ROWS // FFT_N, (r + 1) * SUB_ROWS // FFT_N)
            fm = jnp.concatenate([f_ref[0, :, j, :] for j in planes], axis=0).astype(BF16)
            o = _dot(fm, wo_ref[:FOURIER_WIDTH]) + _dot(a_ref[0, rows], wo_ref[FOURIER_WIDTH:])
        elif mode == "even_ctx":
            o = _dot(f_ref[0, rows].astype(BF16), wo_ref[:FOURIER_WIDTH]) + _dot(a_ref[0, rows], wo_ref[FOURIER_WIDTH:])
        else:
            o = _dot(jnp.concatenate([a_ref[0, j, rows] for j in range(a_ref.shape[1])], axis=1), wo_ref[...])
        x1 = x_ref[0, rows] + _mod_slice(mod_ref, 2) * o
        h = _rms_mod(x1, g_ref[...], _mod_slice(mod_ref, 4), _mod_slice(mod_ref, 3)).astype(BF16)
        acc = jnp.zeros_like(x1)
        for c in range(D_FF // FF_CHUNK):
            cs = slice(c * FF_CHUNK, (c + 1) * FF_CHUNK)
            a = _silu(_dot(h, wg_ref[0, :, cs])) * _dot(h, wu_ref[0, :, cs])
            acc = acc + _dot(a.astype(BF16), wd_ref[0, cs, :])
        o_ref[0, rows] = x1 + _mod_slice(mod_ref, 5) * acc


def _out_ffn(mode, mix, x, mod, mod_row, g, w_out, layer, wg, wu, wd, tm):
    B, L, _ = x.shape
    tok = lambda w: pl.BlockSpec((1, tm, w), lambda b, t: (b, t, 0))
    ffn_spec = lambda r, c: pl.BlockSpec((1, r, c), lambda b, t: (layer, 0, 0), pipeline_mode=pl.Buffered(1))
    if mode == "even_latent":
        assert tm % FFT_N == 0
        mix_specs = [pl.BlockSpec((1, FFT_N, tm // FFT_N, FOURIER_WIDTH), lambda b, t: (b, 0, t, 0)), tok(QW)]
    elif mode == "even_ctx":
        mix_specs = [tok(FOURIER_WIDTH), tok(QW)]
    else:
        mix_specs = [pl.BlockSpec((1, NA_WIDTH // LANES, tm, LANES), lambda b, t: (b, 0, t, 0))]
    return pl.pallas_call(
        functools.partial(_out_ffn_kernel, mode),
        grid=(B, L // tm),
        in_specs=mix_specs + [tok(D_MODEL),
                              pl.BlockSpec((1, 1, 6 * D_MODEL), lambda b, t: (mod_row(b), 0, 0)),
                              _const_spec((1, D_MODEL)),
                              _const_spec((D_MODEL, D_MODEL)),
                              ffn_spec(D_MODEL, D_FF), ffn_spec(D_MODEL, D_FF), ffn_spec(D_FF, D_MODEL)],
        out_specs=tok(D_MODEL),
        out_shape=jax.ShapeDtypeStruct((B, L, D_MODEL), F32),
        compiler_params=_params("arbitrary", "arbitrary"),
        name="out_ffn_" + mode,
    )(*mix, x, mod, g, w_out, wg, wu, wd)


def _ctx_ffn_kernel(f_ref, a_ref, x_ref, mod_ref, g_ref, wo_ref, wg32_ref, wu32_ref, wd32_ref,
                    o_ref, wg_ref, wu_ref, wd_ref, x1_ref, h_ref, acc_ref):
    c = pl.program_id(0)
    wg_ref[...] = wg32_ref[...].astype(BF16)
    wu_ref[...] = wu32_ref[...].astype(BF16)
    wd_ref[...] = wd32_ref[...].astype(BF16)

    @pl.when(c == 0)
    def _():
        o = _dot(f_ref[0].astype(BF16), wo_ref[:FOURIER_WIDTH]) + _dot(a_ref[0], wo_ref[FOURIER_WIDTH:])
        x1 = x_ref[0] + _mod_slice(mod_ref, 2) * o
        x1_ref[...] = x1
        h_ref[...] = _rms_mod(x1, g_ref[...], _mod_slice(mod_ref, 4), _mod_slice(mod_ref, 3)).astype(BF16)
        acc_ref[...] = jnp.zeros_like(acc_ref)

    h = h_ref[...]
    a = _silu(_dot(h, wg_ref[0])) * _dot(h, wu_ref[0])
    acc_ref[...] += _dot(a.astype(BF16), wd_ref[0])

    @pl.when(c == pl.num_programs(0) - 1)
    def _():
        o_ref[0] = x1_ref[...] + _mod_slice(mod_ref, 5) * acc_ref[...]


def _ctx_ffn_and_weights(f, a, x, mod, mod_row, g, w_out, wg32, wu32, wd32):
    _, n, _ = x.shape
    nc = D_FF // FF_CHUNK
    whole = lambda w: pl.BlockSpec((1, n, w), lambda c: (0, 0, 0))
    col_chunk = pl.BlockSpec((DEPTH, D_MODEL, FF_CHUNK), lambda c: (0, 0, c))
    row_chunk = pl.BlockSpec((DEPTH, FF_CHUNK, D_MODEL), lambda c: (0, c, 0))
    return pl.pallas_call(
        _ctx_ffn_kernel,
        grid=(nc,),
        in_specs=[whole(FOURIER_WIDTH), whole(QW), whole(D_MODEL),
                  pl.BlockSpec((1, 1, 6 * D_MODEL), lambda c: (mod_row(0), 0, 0)),
                  pl.BlockSpec((1, D_MODEL), lambda c: (0, 0)),
                  pl.BlockSpec((D_MODEL, D_MODEL), lambda c: (0, 0)),
                  col_chunk, col_chunk, row_chunk],
        out_specs=[whole(D_MODEL), col_chunk, col_chunk, row_chunk],
        out_shape=[jax.ShapeDtypeStruct((1, n, D_MODEL), F32),
                   jax.ShapeDtypeStruct((DEPTH, D_MODEL, D_FF), BF16),
                   jax.ShapeDtypeStruct((DEPTH, D_MODEL, D_FF), BF16),
                   jax.ShapeDtypeStruct((DEPTH, D_FF, D_MODEL), BF16)],
        scratch_shapes=[pltpu.VMEM((n, D_MODEL), F32), pltpu.VMEM((n, D_MODEL), BF16), pltpu.VMEM((n, D_MODEL), F32)],
        compiler_params=_params("arbitrary"),
        name="ctx_ffn_and_weights",
    )(f, a, x, mod, g, w_out, wg32, wu32, wd32)


def _in_odd_kernel(with_q, x_ref, mod_ref, g_ref, w32_ref, qg_ref, kg_ref, hm_ref, *outs):
    W = NA_WIDTH
    slabs = RMS_W // LANES
    if with_q:
        q_ref, k_ref, v_ref, w_ref = outs
    else:
        k_ref, v_ref, w_ref = outs
    _cast_once(w32_ref, w_ref)

    for r in range(x_ref.shape[1] // SUB_ROWS):
        rows = slice(r * SUB_ROWS, (r + 1) * SUB_ROWS)
        h = _rms_mod(x_ref[0, rows], g_ref[...], _mod_slice(mod_ref, 1), _mod_slice(mod_ref, 0)).astype(BF16)

        def normed(t, gain_ref, scale, o_ref):
            for j in range(W // RMS_W):
                tj = (_head_rms(t[:, j * RMS_W:(j + 1) * RMS_W], gain_ref, hm_ref) * scale).astype(BF16)
                for i in range(slabs):
                    o_ref[0, j * slabs + i, rows] = tj[:, i * LANES:(i + 1) * LANES]

        if with_q:
            normed(_dot(h, w_ref[:, :W]), qg_ref, QK_SCALE, q_ref)
        normed(_dot(h, w_ref[:, W:2 * W]), kg_ref, 1.0, k_ref)
        v = _dot(h, w_ref[:, 2 * W:]).astype(BF16)
        for j in range(W // LANES):
            v_ref[0, j, rows] = v[:, j * LANES:(j + 1) * LANES]


def _in_odd(x, mod, mod_row, g, w_in, q_g, k_g, hm, with_q, tm):
    B, L, _ = x.shape
    pairs = NA_WIDTH // LANES
    tok = pl.BlockSpec((1, pairs, tm, LANES), lambda b, t: (b, 0, t, 0))
    n_out = 3 if with_q else 2
    return pl.pallas_call(
        functools.partial(_in_odd_kernel, with_q),
        grid=(B, L // tm),
        in_specs=[pl.BlockSpec((1, tm, D_MODEL), lambda b, t: (b, t, 0)),
                  pl.BlockSpec((1, 1, 6 * D_MODEL), lambda b, t: (mod_row(b), 0, 0)),
                  _const_spec((1, D_MODEL)),
                  _const_spec((D_MODEL, 3 * NA_WIDTH)),
                  _const_spec((1, RMS_W)), _const_spec((1, RMS_W)), _const_spec((RMS_W, RMS_W))],
        out_specs=[tok] * n_out,
        out_shape=[jax.ShapeDtypeStruct((B, pairs, L, LANES), BF16)] * n_out,
        compiler_params=_params("arbitrary", "arbitrary"),
        scratch_shapes=[pltpu.VMEM((D_MODEL, 3 * NA_WIDTH), BF16)],
        name="in_odd" if with_q else "in_odd_ctx",
    )(x, mod, g, w_in, q_g, k_g, hm)


def _bias_table_kernel(rb_ref, idx_ref, o_ref):
    idx = idx_ref[...]
    cq = lax.broadcasted_iota(jnp.int32, idx.shape, 0)
    ck = lax.broadcasted_iota(jnp.int32, idx.shape, 1) % GRID_W
    c0 = jnp.clip(cq - NA_KW // 2, 0, GRID_W - NA_KW)
    inside = (ck >= c0) & (ck < c0 + NA_KW)
    for h in range(o_ref.shape[0]):
        for dr in range(o_ref.shape[1]):
            row = jnp.broadcast_to(rb_ref[h, dr:dr + 1, :], idx.shape)
            o_ref[h, dr] = jnp.where(inside, jnp.take_along_axis(row, idx, axis=1) * LOG2E, -jnp.inf)


def _na_bias_table(rel_bias, idx):
    H = NA_HEADS
    npair = 2 * NA_KH - 2
    half = LANES // 2
    hb = 4
    pad = lambda a: jnp.pad(a, ((0, 0), (0, 0), (0, half - a.shape[-1])))
    rb2 = jnp.concatenate([pad(rel_bias[:, :-1]), pad(rel_bias[:, 1:])], axis=-1)
    return pl.pallas_call(
        _bias_table_kernel,
        grid=(H // hb,),
        in_specs=[pl.BlockSpec((hb, npair, LANES), lambda h: (h, 0, 0)), _const_spec((GRID_W, LANES))],
        out_specs=pl.BlockSpec((hb, npair, GRID_W, LANES), lambda h: (h, 0, 0, 0)),
        out_shape=jax.ShapeDtypeStruct((H, npair, GRID_W, LANES), F32),
        compiler_params=_params("arbitrary"),
        name="na_bias_table",
    )(rb2, idx)


def _na_kernel(rows_per_step, q_ref, k_ref, v_ref, kx_ref, vx_ref, bias_ref, o_ref, s0_ref, s1_ref, p_ref, inv_ref):
    t = pl.program_id(1)
    n = NA_KH * GRID_W
    lo_lanes = lax.broadcasted_iota(jnp.int32, (GRID_W, LANES), 1) < HEAD_DIM

    def window(i):
        r = t * rows_per_step + i
        r0 = jnp.clip(r - NA_KH // 2, 0, GRID_ROWS - NA_KH)
        return r0 - r + NA_KH - 1, pl.multiple_of(r0 * GRID_W, GRID_W), pl.multiple_of(i * GRID_W, GRID_W)

    def scores(i, s_ref):
        dr0, start, qoff = window(i)
        for hp in range(NA_HEADS // 2):
            qs = q_ref[0, hp, pl.ds(qoff, GRID_W), :]
            keys = jnp.concatenate([k_ref[0, hp, pl.ds(start, n), :], kx_ref[0, hp]], axis=0)
            qm = jnp.concatenate([jnp.where(lo_lanes, qs, jnp.zeros_like(qs)),
                                  jnp.where(lo_lanes, jnp.zeros_like(qs), qs)], axis=0)
            s = _dot_nt(qm, keys)
            for half in range(2):
                h = 2 * hp + half
                bias = jnp.concatenate([bias_ref[h, dr0 + 2 * p] for p in range(NA_KH // 2)], axis=1)
                rs = slice(h * GRID_W, (h + 1) * GRID_W)
                hr = slice(half * GRID_W, (half + 1) * GRID_W)
                s_ref[rs, :n] = s[hr, :n] + bias
                s_ref[rs, n:] = s[hr, n:]

    def attend(i, s_ref):
        _, start, qoff = window(i)
        for c in range(NA_HEADS * GRID_W // SOFTMAX_ROWS):
            cs = slice(c * SOFTMAX_ROWS, (c + 1) * SOFTMAX_ROWS)
            s = s_ref[cs, :]
            e = jnp.exp2(s - jnp.max(s, axis=-1, keepdims=True))
            inv_ref[cs, :] = 1.0 / jnp.sum(e, axis=-1, keepdims=True)
            p_ref[cs, :] = e.astype(BF16)
        for hp in range(NA_HEADS // 2):
            vals = jnp.concatenate([v_ref[0, hp, pl.ds(start, n), :], vx_ref[0, hp]], axis=0)
            rs = slice(2 * hp * GRID_W, (2 * hp + 2) * GRID_W)
            res = _dot(p_ref[rs, :], vals) * inv_ref[rs, :]
            o_ref[0, hp, pl.ds(qoff, GRID_W), :] = jnp.where(lo_lanes, res[:GRID_W], res[GRID_W:]).astype(BF16)

    last = rows_per_step - 1
    scores(0, s0_ref)

    def two_rows(j, carry):
        scores(2 * j + 1, s1_ref)
        attend(2 * j, s0_ref)
        scores(jnp.minimum(2 * j + 2, last), s0_ref)
        attend(2 * j + 1, s1_ref)
        return carry

    lax.fori_loop(0, rows_per_step // 2, two_rows, 0)


def _neighbourhood_attention(q, k, v, kx, vx, bias_tab):
    B, pairs, L, _ = q.shape
    rows_per_step = 16
    full = pl.BlockSpec((1, pairs, L, LANES), lambda b, t: (b, 0, 0, 0))
    ctx_spec = pl.BlockSpec((1, pairs, CTX_LEN, LANES), lambda b, t: (0, 0, b, 0))
    q_rows = pl.BlockSpec((1, pairs, rows_per_step * GRID_W, LANES), lambda b, t: (b, 0, t, 0))
    n_keys = NA_KH * GRID_W + CTX_LEN
    all_rows = NA_HEADS * GRID_W
    return pl.pallas_call(
        functools.partial(_na_kernel, rows_per_step),
        grid=(B, GRID_ROWS // rows_per_step),
        in_specs=[q_rows, full, full, ctx_spec, ctx_spec, _const_spec(bias_tab.shape)],
        out_specs=q_rows,
        out_shape=jax.ShapeDtypeStruct((B, pairs, L, LANES), BF16),
        scratch_shapes=[pltpu.VMEM((all_rows, n_keys), F32), pltpu.VMEM((all_rows, n_keys), F32),
                        pltpu.VMEM((all_rows, n_keys), BF16), pltpu.VMEM((all_rows, 1), F32)],
        compiler_params=_params("arbitrary", "arbitrary"),
        name="neighbourhood_attention",
    )(q, k, v, kx, vx, bias_tab)


def kernel(x, c, ctx, c_ctx, ada_w, ada_b, norm1_g, norm2_g, ffn_w_gate, ffn_w_up, ffn_w_down,
           ev_w_in, ev_w_out, ev_q_norm, ev_k_norm, ev_sink,
           od_w_in, od_w_out, od_q_norm, od_k_norm, od_rel_bias):
    assert x.shape == (BATCH, SEQ, D_MODEL) and ctx.shape == (BATCH, CTX_LEN, D_MODEL)
    wc, m1, m3, twc, tws, mctx = _fourier_tables()
    hm = _head_mean_matrix()
    rope_tabs = _rope_tables()
    lane_gain = lambda gvec: jnp.tile(gvec, RMS_W // HEAD_DIM).reshape(1, RMS_W)
    lat_row = lambda b: b
    ctx_row = lambda b: CTX_MOD_ROW
    tm = 512
    n_ctx = BATCH * CTX_LEN
    as_seq = lambda a: a.reshape(1, n_ctx, a.shape[-1])
    per_batch = lambda a: a.reshape(BATCH, CTX_LEN, a.shape[-1])

    cs = jnp.concatenate([c, c_ctx[None, :], jnp.zeros((MOD_ROWS - BATCH - 1, D_MODEL), F32)], axis=0)
    mod = _modulation(cs, ada_w, ada_b).reshape(DEPTH, MOD_ROWS, 1, 6 * D_MODEL)

    w_in0 = ev_w_in[0]
    w_out0 = ev_w_out[0].astype(BF16)
    g1 = norm1_g[0].reshape(1, D_MODEL)
    g2 = norm2_g[0].reshape(1, D_MODEL)
    qg, kg = lane_gain(ev_q_norm[0]), lane_gain(ev_k_norm[0])
    y_l, q_l, k_l, v_l = _in_even(x, mod[0], lat_row, g1, w_in0, qg, kg, hm, wc, rope_tabs, 2 * tm)
    y_c, q_c, k_c, v_c = _in_even(as_seq(ctx), mod[0], ctx_row, g1, w_in0, qg, kg, hm, wc, None, tm)
    y_c, q_c = per_batch(y_c), per_batch(q_c)
    f_l = _fourier_latent(y_l, m1, m3, twc, tws)
    f_c = _fourier_ctx(y_c, mctx)
    a_l = _win_attention(q_l, k_l, v_l, k_c, v_c, ev_sink[0])
    a_c = _ctx_attention_even(q_c, k_c, v_c, ev_sink[0])
    y1, *ffn = _ctx_ffn_and_weights(as_seq(f_c), as_seq(a_c), as_seq(ctx), mod[0], ctx_row, g2, w_out0,
                                    ffn_w_gate, ffn_w_up, ffn_w_down)
    x1 = _out_ffn("even_latent", (f_l, a_l), x, mod[0], lat_row, g2, w_out0, 0, *ffn, 2 * tm)

    w_in1 = od_w_in[0]
    w_out1 = od_w_out[0].astype(BF16)
    g1 = norm1_g[1].reshape(1, D_MODEL)
    g2 = norm2_g[1].reshape(1, D_MODEL)
    qg, kg = lane_gain(od_q_norm[0]), lane_gain(od_k_norm[0])
    q_l, k_l, v_l = _in_odd(x1, mod[1], lat_row, g1, w_in1, qg, kg, hm, True, 2 * tm)
    k_c, v_c = _in_odd(y1, mod[1], ctx_row, g1, w_in1, qg, kg, hm, False, tm)
    bias_tab = _na_bias_table(od_rel_bias[0], _na_bias_index())
    a_l = _neighbourhood_attention(q_l, k_l, v_l, k_c, v_c, bias_tab)
    return _out_ffn("odd", (a_l,), x1, mod[1], lat_row, g2, w_out1, 1, *ffn, 2 * tm)
```

```python
import functools
import math

import numpy as np
import jax
import jax.numpy as jnp
from jax import lax
from jax.experimental import pallas as pl
from jax.experimental.pallas import tpu as pltpu

D_MODEL = 1024
BATCH = 4
SEQ = 4096
DEPTH = 2
GRID_W = 64
CTX_LEN = 256
HEAD_DIM = 64
EPS = 1e-6
FOURIER_WIDTH = D_MODEL // 2
FOURIER_GROUPS = 4
FOURIER_GROUP_CH = FOURIER_WIDTH // FOURIER_GROUPS
WIN_Q_HEADS = (D_MODEL // 2) // HEAD_DIM
WIN_KV_HEADS = 2
WIN_GROUP = WIN_Q_HEADS // WIN_KV_HEADS
WIN_RADIUS = 128
WIN_BLOCK = 128
QW = WIN_Q_HEADS * HEAD_DIM
KW = WIN_KV_HEADS * HEAD_DIM
EV_IN_WIDTH = FOURIER_WIDTH + QW + 2 * KW
NA_HEADS = D_MODEL // HEAD_DIM
NA_KH = 8
NA_KW = 16
NA_WIDTH = NA_HEADS * HEAD_DIM
ROPE_THETA = 10000.0
ROPE_FREQS = HEAD_DIM // 4
D_FF = ((8 * D_MODEL // 3 + 255) // 256) * 256
GRID_ROWS = SEQ // GRID_W

LANES = 128
MOD_ROWS = 8
CTX_MOD_ROW = BATCH
FFT_N = 64
FF_CHUNK = 256
SOFTMAX_ROWS = 64
RMS_W = 256
SUB_ROWS = 512
VMEM_LIMIT = 60 * 1024 * 1024

LOG2E = math.log2(math.e)
QK_SCALE = LOG2E / math.sqrt(HEAD_DIM)

BF16 = jnp.bfloat16
F32 = jnp.float32

assert DEPTH == 2 and SEQ == FFT_N * FFT_N and D_FF % FF_CHUNK == 0


def _params(*sem):
    return pltpu.CompilerParams(dimension_semantics=sem, vmem_limit_bytes=VMEM_LIMIT)


def _dot(a, b):
    return jnp.dot(a, b, preferred_element_type=F32)


def _dot_nt(a, b):
    return lax.dot_general(a, b, (((1,), (1,)), ((), ())), preferred_element_type=F32)


def _silu(x):
    return x / (1.0 + jnp.exp(-x))


def _const_spec(shape):
    nd = len(shape)
    return pl.BlockSpec(shape, lambda *_: (0,) * nd, pipeline_mode=pl.Buffered(1))


def _dft_cos_sin(n):
    idx = (np.arange(n)[:, None] * np.arange(n)[None, :]) % n
    ang = 2.0 * np.pi * idx / n
    return np.cos(ang), np.sin(ang)


def _fourier_tables():
    cc, sc = _dft_cos_sin(FOURIER_GROUP_CH)
    wc = np.concatenate([cc, -sc], axis=1) / math.sqrt(FOURIER_GROUP_CH)
    c64, s64 = _dft_cos_sin(FFT_N)
    m1 = np.concatenate([c64, -s64], axis=0) / math.sqrt(FFT_N)
    m3 = np.concatenate([c64, s64], axis=1) / math.sqrt(FFT_N)
    tw = (np.arange(FFT_N)[:, None] * np.arange(FFT_N)[None, :]) % SEQ
    tw = 2.0 * np.pi * tw / SEQ
    twc = np.repeat(np.cos(tw)[:, :, None], LANES, axis=2)
    tws = np.repeat(np.sin(tw)[:, :, None], LANES, axis=2)
    cx, sx = _dft_cos_sin(CTX_LEN)
    mctx = np.concatenate([cx, sx], axis=1) / math.sqrt(CTX_LEN)
    as32 = lambda a: jnp.asarray(a, F32)
    return (as32(wc).astype(BF16), as32(m1).astype(BF16), as32(m3).astype(BF16),
            as32(twc), as32(tws), as32(mctx).astype(BF16))


def _head_mean_matrix():
    blk = np.kron(np.eye(RMS_W // HEAD_DIM), np.ones((HEAD_DIM, HEAD_DIM))) / HEAD_DIM
    return jnp.asarray(blk, BF16)


def _rope_tables():
    t = np.arange(SEQ)
    row = (t // GRID_W).astype(np.float32)
    col = (t % GRID_W).astype(np.float32)
    inv = np.float32(ROPE_THETA) ** (-np.arange(ROPE_FREQS, dtype=np.float32) / np.float32(ROPE_FREQS))
    ang_row = (row[:, None] * inv[None, :]).astype(np.float32).astype(np.float64)
    ang_col = (col[:, None] * inv[None, :]).astype(np.float32).astype(np.float64)
    zero = np.zeros_like(ang_row)
    cos = np.concatenate([np.cos(ang_row)] * 2 + [np.cos(ang_col)] * 2, axis=1)
    sin_hi = np.concatenate([-np.sin(ang_row), zero, -np.sin(ang_col), zero], axis=1)
    sin_lo = np.concatenate([zero, np.sin(ang_row), zero, np.sin(ang_col)], axis=1)
    rep = LANES // HEAD_DIM
    return tuple(jnp.asarray(np.tile(a, (1, rep)), F32) for a in (cos, sin_hi, sin_lo))


def _na_bias_index():
    cq = np.arange(GRID_W)
    dc = np.clip(cq[None, :] - cq[:, None] + NA_KW - 1, 0, 2 * NA_KW - 2)
    return jnp.asarray(np.concatenate([dc, dc + LANES // 2], axis=1), jnp.int32)


def _mod_kernel(cs_ref, w_ref, b_ref, o_ref):
    s = _silu(cs_ref[...]).astype(BF16)
    o_ref[0] = _dot(s, w_ref[0].astype(BF16)) + b_ref[0]


def _modulation(cs, ada_w, ada_b):
    tn = 1536
    return pl.pallas_call(
        _mod_kernel,
        grid=(DEPTH, 6 * D_MODEL // tn),
        in_specs=[pl.BlockSpec((MOD_ROWS, D_MODEL), lambda i, j: (0, 0)),
                  pl.BlockSpec((1, D_MODEL, tn), lambda i, j: (i, 0, j)),
                  pl.BlockSpec((1, 1, tn), lambda i, j: (i, 0, j))],
        out_specs=pl.BlockSpec((1, MOD_ROWS, tn), lambda i, j: (i, 0, j)),
        out_shape=jax.ShapeDtypeStruct((DEPTH, MOD_ROWS, 6 * D_MODEL), F32),
        compiler_params=_params("arbitrary", "arbitrary"),
        name="ada_modulation",
    )(cs, ada_w, ada_b.reshape(DEPTH, 1, 6 * D_MODEL))


def _mod_slice(mod_ref, k):
    return mod_ref[0, :, k * D_MODEL:(k + 1) * D_MODEL]


def _rms_mod(x, g, scale, shift):
    y = x * lax.rsqrt(jnp.mean(x * x, axis=-1, keepdims=True) + EPS)
    return (y * g) * (1.0 + scale) + shift


def _head_rms(t, gain_ref, hm_ref):
    w = t.shape[1]
    ms = _dot((t * t).astype(BF16), hm_ref[:w, :w])
    return t * lax.rsqrt(ms + EPS) * gain_ref[:, :w]


def _rope(t, cos, sin_hi, sin_lo):
    up = pltpu.roll(t, LANES - ROPE_FREQS, axis=1)
    dn = pltpu.roll(t, ROPE_FREQS, axis=1)
    return t * cos + up * sin_hi + dn * sin_lo


def _cast_once(w32_ref, w_ref):
    @pl.when((pl.program_id(0) == 0) & (pl.program_id(1) == 0))
    def _():
        w_ref[...] = w32_ref[...].astype(BF16)


def _in_even_kernel(latent, x_ref, mod_ref, g_ref, w32_ref, qg_ref, kg_ref, hm_ref, wc_ref, *rest):
    if latent:
        cos_ref, sh_ref, sl_ref, y_ref, q_ref, k_ref, v_ref, w_ref = rest
    else:
        y_ref, q_ref, k_ref, v_ref, w_ref = rest
    _cast_once(w32_ref, w_ref)
    F = FOURIER_WIDTH
    lo_lanes = lax.broadcasted_iota(jnp.int32, (1, LANES), 1) < HEAD_DIM

    for r in range(x_ref.shape[1] // SUB_ROWS):
        rows = slice(r * SUB_ROWS, (r + 1) * SUB_ROWS)
        h = _rms_mod(x_ref[0, rows], g_ref[...], _mod_slice(mod_ref, 1), _mod_slice(mod_ref, 0)).astype(BF16)
        f = _dot(h, w_ref[:, :F]).astype(BF16)
        for g in range(FOURIER_GROUPS):
            yg = _dot(f[:, g * LANES:(g + 1) * LANES], wc_ref[...])
            y_ref[0, rows, g * LANES:(g + 1) * LANES] = yg[:, :LANES].astype(y_ref.dtype)
            y_ref[0, rows, F + g * LANES:F + (g + 1) * LANES] = yg[:, LANES:].astype(y_ref.dtype)

        def qk(t, gain_ref):
            t = _head_rms(t, gain_ref, hm_ref)
            if latent:
                t = jnp.concatenate([_rope(t[:, i * LANES:(i + 1) * LANES], cos_ref[rows], sh_ref[rows], sl_ref[rows])
                                     for i in range(t.shape[1] // LANES)], axis=1)
            return t

        q = _dot(h, w_ref[:, F:F + QW])
        for j in range(QW // RMS_W):
            qj = qk(q[:, j * RMS_W:(j + 1) * RMS_W], qg_ref) * QK_SCALE
            q_ref[0, rows, j * RMS_W:(j + 1) * RMS_W] = qj.astype(BF16)
        kv = _dot(h, w_ref[:, F + QW:])

        def store_dup(t, o_ref):
            sw = pltpu.roll(t, HEAD_DIM, axis=1)
            o_ref[0, 0, rows] = jnp.where(lo_lanes, t, sw).astype(BF16)
            o_ref[0, 1, rows] = jnp.where(lo_lanes, sw, t).astype(BF16)

        store_dup(qk(kv[:, :KW], kg_ref), k_ref)
        store_dup(kv[:, KW:], v_ref)


def _in_even(x, mod, mod_row, g, w_in, q_g, k_g, hm, wc, rope_tabs, tm):
    B, L, _ = x.shape
    latent = rope_tabs is not None
    in_specs = [pl.BlockSpec((1, tm, D_MODEL), lambda b, t: (b, t, 0)),
                pl.BlockSpec((1, 1, 6 * D_MODEL), lambda b, t: (mod_row(b), 0, 0)),
                _const_spec((1, D_MODEL)),
                _const_spec((D_MODEL, EV_IN_WIDTH)),
                _const_spec((1, RMS_W)), _const_spec((1, RMS_W)),
                _const_spec((RMS_W, RMS_W)), _const_spec((LANES, 2 * LANES))]
    args = [x, mod, g, w_in, q_g, k_g, hm, wc]
    tok = lambda w: pl.BlockSpec((1, tm, w), lambda b, t: (b, t, 0))
    kv_spec = pl.BlockSpec((1, WIN_KV_HEADS, tm, LANES), lambda b, t: (b, 0, t, 0))
    if latent:
        in_specs += [pl.BlockSpec((tm, LANES), lambda b, t: (t, 0))] * 3
        args += list(rope_tabs)
    return pl.pallas_call(
        functools.partial(_in_even_kernel, latent),
        grid=(B, L // tm),
        in_specs=in_specs,
        out_specs=[tok(2 * FOURIER_WIDTH), tok(QW), kv_spec, kv_spec],
        out_shape=[jax.ShapeDtypeStruct((B, L, 2 * FOURIER_WIDTH), F32 if latent else BF16),
                   jax.ShapeDtypeStruct((B, L, QW), BF16),
                   jax.ShapeDtypeStruct((B, WIN_KV_HEADS, L, LANES), BF16),
                   jax.ShapeDtypeStruct((B, WIN_KV_HEADS, L, LANES), BF16)],
        compiler_params=_params("arbitrary", "arbitrary"),
        scratch_shapes=[pltpu.VMEM((D_MODEL, EV_IN_WIDTH), BF16)],
        name="in_even_latent" if latent else "in_even_ctx",
    )(*args)


def _fft_stage1_kernel(n2t, y_ref, m1_ref, twc_ref, tws_ref, o_ref, row_ref):
    F = FOURIER_WIDTH
    for j in range(n2t):
        row_ref[j % 2] = y_ref[0, :, j, :]
        a = _dot(m1_ref[...], row_ref[j % 2].astype(BF16))
        top, bot = a[:FFT_N], a[FFT_N:]
        ar = top[:, :F] - bot[:, F:]
        ai = top[:, F:] + bot[:, :F]
        tc = jnp.tile(twc_ref[j], (1, F // LANES))
        ts = jnp.tile(tws_ref[j], (1, F // LANES))
        o_ref[0, 0, :, j * F:(j + 1) * F] = (ar * tc + ai * ts).astype(BF16)
        o_ref[0, 1, :, j * F:(j + 1) * F] = (ai * tc - ar * ts).astype(BF16)


def _fft_stage2_kernel(k1t, b_ref, m3_ref, o_ref):
    for j in range(k1t):
        rhs = jnp.concatenate([b_ref[0, 0, j], b_ref[0, 1, j]], axis=0)
        o_ref[0, j] = _dot(m3_ref[...], rhs)


def _fourier_latent(y, m1, m3, twc, tws):
    B = y.shape[0]
    F = FOURIER_WIDTH
    n2t = 16
    yv = y.reshape(B, FFT_N, FFT_N, 2 * F)
    bh = pl.pallas_call(
        functools.partial(_fft_stage1_kernel, n2t),
        grid=(B, FFT_N // n2t),
        in_specs=[pl.BlockSpec((1, FFT_N, n2t, 2 * F), lambda b, t: (b, 0, t, 0)),
                  _const_spec((2 * FFT_N, FFT_N)),
                  pl.BlockSpec((n2t, FFT_N, LANES), lambda b, t: (t, 0, 0)),
                  pl.BlockSpec((n2t, FFT_N, LANES), lambda b, t: (t, 0, 0))],
        out_specs=pl.BlockSpec((1, 2, FFT_N, n2t * F), lambda b, t: (b, 0, 0, t)),
        out_shape=jax.ShapeDtypeStruct((B, 2, FFT_N, FFT_N * F), BF16),
        scratch_shapes=[pltpu.VMEM((2, FFT_N, 2 * F), F32)],
        compiler_params=_params("arbitrary", "arbitrary"),
        name="fft_stage1",
    )(yv, m1, twc, tws)
    k1t = 16
    bv = bh.reshape(B, 2, FFT_N, FFT_N, F)
    return pl.pallas_call(
        functools.partial(_fft_stage2_kernel, k1t),
        grid=(B, FFT_N // k1t),
        in_specs=[pl.BlockSpec((1, 2, k1t, FFT_N, F), lambda b, t: (b, 0, t, 0, 0)),
                  _const_spec((FFT_N, 2 * FFT_N))],
        out_specs=pl.BlockSpec((1, k1t, FFT_N, F), lambda b, t: (b, t, 0, 0)),
        out_shape=jax.ShapeDtypeStruct((B, FFT_N, FFT_N, F), F32),
        compiler_params=_params("arbitrary", "arbitrary"),
        name="fft_stage2",
    )(bv, m3)


def _fourier_ctx_kernel(y_ref, m_ref, o_ref):
    F = FOURIER_WIDTH
    rhs = jnp.concatenate([y_ref[0, :, :F], y_ref[0, :, F:]], axis=0)
    o_ref[0] = _dot(m_ref[...], rhs)


def _fourier_ctx(y, mctx):
    B = y.shape[0]
    return pl.pallas_call(
        _fourier_ctx_kernel,
        grid=(B,),
        in_specs=[pl.BlockSpec((1, CTX_LEN, 2 * FOURIER_WIDTH), lambda b: (b, 0, 0)),
                  _const_spec((CTX_LEN, 2 * CTX_LEN))],
        out_specs=pl.BlockSpec((1, CTX_LEN, FOURIER_WIDTH), lambda b: (b, 0, 0)),
        out_shape=jax.ShapeDtypeStruct((B, CTX_LEN, FOURIER_WIDTH), F32),
        compiler_params=_params("arbitrary"),
        name="fourier_ctx",
    )(y, mctx)


def _gqa_group(q_slabs, keys, vals, masks, sinks, rows, o_ref, first_slab, scratch):
    s_ref, p_ref, inv_ref = scratch
    _gqa_scores(q_slabs, keys, rows, s_ref)
    _gqa_attend(vals, masks, sinks, rows, lambda sl: o_ref.at[0, :, sl], first_slab, scratch)


def _gqa_scores(q_slabs, keys, rows, s_ref):
    lo_lanes = lax.broadcasted_iota(jnp.int32, (rows, LANES), 1) < HEAD_DIM
    zero = jnp.zeros((rows, LANES), BF16)
    qs = jnp.concatenate([jnp.where(lo_lanes if half == 0 else ~lo_lanes, qslab, zero)
                          for qslab in q_slabs for half in range(2)], axis=0)
    s_ref[...] = _dot_nt(qs, keys)


def _gqa_attend(vals, masks, sinks, rows, out_slab, first_slab, scratch):
    s_ref, p_ref, inv_ref = scratch
    lo_lanes = lax.broadcasted_iota(jnp.int32, (rows, LANES), 1) < HEAD_DIM
    for c in range(WIN_GROUP * rows // SOFTMAX_ROWS):
        rs = slice(c * SOFTMAX_ROWS, (c + 1) * SOFTMAX_ROWS)
        s = s_ref[rs, :]
        if masks is not None:
            band, w = masks
            r0 = (c * SOFTMAX_ROWS) % rows
            s = jnp.concatenate([s[:, :w] + band[r0:r0 + SOFTMAX_ROWS], s[:, w:]], axis=1)
        sk = sinks[(c * SOFTMAX_ROWS) // rows]
        m = jnp.maximum(jnp.max(s, axis=-1, keepdims=True), sk)
        e = jnp.exp2(s - m)
        inv_ref[rs, :] = 1.0 / (jnp.sum(e, axis=-1, keepdims=True) + jnp.exp2(sk - m))
        p_ref[rs, :] = e.astype(BF16)
    o = _dot(p_ref[...], vals) * inv_ref[...]
    for i in range(WIN_GROUP // 2):
        a = o[(2 * i) * rows:(2 * i + 1) * rows]
        b = o[(2 * i + 1) * rows:(2 * i + 2) * rows]
        sl = slice((first_slab + i) * LANES, (first_slab + i + 1) * LANES)
        out_slab(sl)[...] = jnp.where(lo_lanes, a, b).astype(BF16)


def _gqa_scratch(rows, n_keys):
    one = [pltpu.VMEM((WIN_GROUP * rows, n_keys), F32), pltpu.VMEM((WIN_GROUP * rows, n_keys), BF16),
           pltpu.VMEM((WIN_GROUP * rows, 1), F32)]
    return one * WIN_KV_HEADS


def _group_sinks(sink_ref, kv):
    return [sink_ref[kv * WIN_GROUP + g] * LOG2E for g in range(WIN_GROUP)]


def _group_slab(kv, i):
    return slice((kv * (WIN_GROUP // 2) + i) * LANES, (kv * (WIN_GROUP // 2) + i + 1) * LANES)


def _window_bands():
    i = np.arange(WIN_BLOCK)[:, None]
    j = np.arange(3 * WIN_BLOCK)[None, :]
    bands = [np.where(np.abs(j - off - i) <= WIN_RADIUS, 0.0, -np.inf) for off in (0, WIN_BLOCK, 2 * WIN_BLOCK)]
    return jnp.asarray(np.stack(bands).astype(np.float32))


def _win_attn_kernel(blocks_per_step, sink_ref, band_ref, q_ref, k_ref, v_ref, kx_ref, vx_ref, o_ref,
                     s0_ref, s1_ref, p_ref, inv_ref):
    t = pl.program_id(1)
    nb = SEQ // WIN_BLOCK
    nw = 3 * WIN_BLOCK

    def place(j):
        n = t * blocks_per_step + j
        start = pl.multiple_of(jnp.clip((n - 1) * WIN_BLOCK, 0, SEQ - nw), WIN_BLOCK)
        sel = jnp.where(n == 0, 0, jnp.where(n == nb - 1, 2, 1))
        return pl.multiple_of(j * WIN_BLOCK, WIN_BLOCK), start, sel

    def scores(j, kv, s_ref):
        qoff, start, _ = place(j)
        keys = jnp.concatenate([k_ref[0, kv, pl.ds(start, nw), :], kx_ref[0, kv]], axis=0)
        q_slabs = [q_ref[0, pl.ds(qoff, WIN_BLOCK), _group_slab(kv, i)] for i in range(WIN_GROUP // 2)]
        _gqa_scores(q_slabs, keys, WIN_BLOCK, s_ref)

    def attend(j, kv, s_ref):
        qoff, start, sel = place(j)
        vals = jnp.concatenate([v_ref[0, kv, pl.ds(start, nw), :], vx_ref[0, kv]], axis=0)
        _gqa_attend(vals, (band_ref[sel], nw), _group_sinks(sink_ref, kv), WIN_BLOCK,
                    lambda sl: o_ref.at[0, pl.ds(qoff, WIN_BLOCK), sl], kv * (WIN_GROUP // 2),
                    (s_ref, p_ref, inv_ref))

    scores(0, 0, s0_ref)

    def one_block(j, carry):
        scores(j, 1, s1_ref)
        attend(j, 0, s0_ref)
        scores(jnp.minimum(j + 1, blocks_per_step - 1), 0, s0_ref)
        attend(j, 1, s1_ref)
        return carry

    lax.fori_loop(0, blocks_per_step, one_block, 0)


def _win_attention(q, k, v, kx, vx, sink):
    B, L, _ = q.shape
    blocks_per_step = 8
    rows = blocks_per_step * WIN_BLOCK
    n_keys = 3 * WIN_BLOCK + CTX_LEN
    full = pl.BlockSpec((1, WIN_KV_HEADS, L, LANES), lambda b, t: (b, 0, 0, 0))
    ctx_spec = pl.BlockSpec((1, WIN_KV_HEADS, CTX_LEN, LANES), lambda b, t: (0, 0, b, 0))
    tok = pl.BlockSpec((1, rows, QW), lambda b, t: (b, t, 0))
    return pl.pallas_call(
        functools.partial(_win_attn_kernel, blocks_per_step),
        grid=(B, L // rows),
        in_specs=[pl.BlockSpec(memory_space=pltpu.SMEM),
                  _const_spec((3, WIN_BLOCK, 3 * WIN_BLOCK)),
                  tok, full, full, ctx_spec, ctx_spec],
        out_specs=tok,
        out_shape=jax.ShapeDtypeStruct((B, L, QW), BF16),
        scratch_shapes=[pltpu.VMEM((WIN_GROUP * WIN_BLOCK, n_keys), F32), pltpu.VMEM((WIN_GROUP * WIN_BLOCK, n_keys), F32),
                        pltpu.VMEM((WIN_GROUP * WIN_BLOCK, n_keys), BF16), pltpu.VMEM((WIN_GROUP * WIN_BLOCK, 1), F32)],
        compiler_params=_params("arbitrary", "arbitrary"),
        name="window_attention",
    )(sink, _window_bands(), q, k, v, kx, vx)


def _ctx_attn_even_kernel(sink_ref, q_ref, k_ref, v_ref, o_ref, *scratch):
    for kv in range(WIN_KV_HEADS):
        q_slabs = [q_ref[0, :, _group_slab(kv, i)] for i in range(WIN_GROUP // 2)]
        _gqa_group(q_slabs, k_ref[0, kv], v_ref[0, kv], None, _group_sinks(sink_ref, kv), CTX_LEN, o_ref,
                   kv * (WIN_GROUP // 2), scratch[3 * kv:3 * kv + 3])


def _ctx_attention_even(q, k, v, sink):
    B = q.shape[0]
    kv_spec = pl.BlockSpec((1, WIN_KV_HEADS, CTX_LEN, LANES), lambda b: (0, 0, b, 0))
    return pl.pallas_call(
        _ctx_attn_even_kernel,
        grid=(B,),
        in_specs=[pl.BlockSpec(memory_space=pltpu.SMEM),
                  pl.BlockSpec((1, CTX_LEN, QW), lambda b: (b, 0, 0)),
                  kv_spec, kv_spec],
        out_specs=pl.BlockSpec((1, CTX_LEN, QW), lambda b: (b, 0, 0)),
        out_shape=jax.ShapeDtypeStruct((B, CTX_LEN, QW), BF16),
        scratch_shapes=_gqa_scratch(CTX_LEN, CTX_LEN),
        compiler_params=_params("arbitrary"),
        name="ctx_attention_even",
    )(sink, q, k, v)


def _out_ffn_kernel(mode, *refs):
    if mode == "odd":
        a_ref, x_ref, mod_ref, g_ref, wo_ref, wg_ref, wu_ref, wd_ref, o_ref = refs
    else:
        f_ref, a_ref, x_ref, mod_ref, g_ref, wo_ref, wg_ref, wu_ref, wd_ref, o_ref = refs
    for r in range(x_ref.shape[1] // SUB_ROWS):
        rows = slice(r * SUB_ROWS, (r + 1) * SUB_ROWS)
        if mode == "even_latent":
            planes = range(r * SUB_ROWS // FFT_N, (r + 1) * SUB_ROWS // FFT_N)
            fm = jnp.concatenate([f_ref[0, :, j, :] for j in planes], axis=0).astype(BF16)
            o = _dot(fm, wo_ref[:FOURIER_WIDTH]) + _dot(a_ref[0, rows], wo_ref[FOURIER_WIDTH:])
        elif mode == "even_ctx":
            o = _dot(f_ref[0, rows].astype(BF16), wo_ref[:FOURIER_WIDTH]) + _dot(a_ref[0, rows], wo_ref[FOURIER_WIDTH:])
        else:
            o = _dot(jnp.concatenate([a_ref[0, j, rows] for j in range(a_ref.shape[1])], axis=1), wo_ref[...])
        x1 = x_ref[0, rows] + _mod_slice(mod_ref, 2) * o
        h = _rms_mod(x1, g_ref[...], _mod_slice(mod_ref, 4), _mod_slice(mod_ref, 3)).astype(BF16)
        acc = jnp.zeros_like(x1)
        for c in range(D_FF // FF_CHUNK):
            cs = slice(c * FF_CHUNK, (c + 1) * FF_CHUNK)
            a = _silu(_dot(h, wg_ref[0, :, cs])) * _dot(h, wu_ref[0, :, cs])
            acc = acc + _dot(a.astype(BF16), wd_ref[0, cs, :])
        o_ref[0, rows] = x1 + _mod_slice(mod_ref, 5) * acc


def _out_ffn(mode, mix, x, mod, mod_row, g, w_out, layer, wg, wu, wd, tm):
    B, L, _ = x.shape
    tok = lambda w: pl.BlockSpec((1, tm, w), lambda b, t: (b, t, 0))
    ffn_spec = lambda r, c: pl.BlockSpec((1, r, c), lambda b, t: (layer, 0, 0), pipeline_mode=pl.Buffered(1))
    if mode == "even_latent":
        assert tm % FFT_N == 0
        mix_specs = [pl.BlockSpec((1, FFT_N, tm // FFT_N, FOURIER_WIDTH), lambda b, t: (b, 0, t, 0)), tok(QW)]
    elif mode == "even_ctx":
        mix_specs = [tok(FOURIER_WIDTH), tok(QW)]
    else:
        mix_specs = [pl.BlockSpec((1, NA_WIDTH // LANES, tm, LANES), lambda b, t: (b, 0, t, 0))]
    return pl.pallas_call(
        functools.partial(_out_ffn_kernel, mode),
        grid=(B, L // tm),
        in_specs=mix_specs + [tok(D_MODEL),
                              pl.BlockSpec((1, 1, 6 * D_MODEL), lambda b, t: (mod_row(b), 0, 0)),
                              _const_spec((1, D_MODEL)),
                              _const_spec((D_MODEL, D_MODEL)),
                              ffn_spec(D_MODEL, D_FF), ffn_spec(D_MODEL, D_FF), ffn_spec(D_FF, D_MODEL)],
        out_specs=tok(D_MODEL),
        out_shape=jax.ShapeDtypeStruct((B, L, D_MODEL), F32),
        compiler_params=_params("arbitrary", "arbitrary"),
        name="out_ffn_" + mode,
    )(*mix, x, mod, g, w_out, wg, wu, wd)


def _ctx_ffn_kernel(f_ref, a_ref, x_ref, mod_ref, g_ref, wo_ref, wg32_ref, wu32_ref, wd32_ref,
                    o_ref, wg_ref, wu_ref, wd_ref, x1_ref, h_ref, acc_ref):
    c = pl.program_id(0)
    wg_ref[...] = wg32_ref[...].astype(BF16)
    wu_ref[...] = wu32_ref[...].astype(BF16)
    wd_ref[...] = wd32_ref[...].astype(BF16)

    @pl.when(c == 0)
    def _():
        o = _dot(f_ref[0].astype(BF16), wo_ref[:FOURIER_WIDTH]) + _dot(a_ref[0], wo_ref[FOURIER_WIDTH:])
        x1 = x_ref[0] + _mod_slice(mod_ref, 2) * o
        x1_ref[...] = x1
        h_ref[...] = _rms_mod(x1, g_ref[...], _mod_slice(mod_ref, 4), _mod_slice(mod_ref, 3)).astype(BF16)
        acc_ref[...] = jnp.zeros_like(acc_ref)

    h = h_ref[...]
    a = _silu(_dot(h, wg_ref[0])) * _dot(h, wu_ref[0])
    acc_ref[...] += _dot(a.astype(BF16), wd_ref[0])

    @pl.when(c == pl.num_programs(0) - 1)
    def _():
        o_ref[0] = x1_ref[...] + _mod_slice(mod_ref, 5) * acc_ref[...]


def _ctx_ffn_and_weights(f, a, x, mod, mod_row, g, w_out, wg32, wu32, wd32):
    _, n, _ = x.shape
    nc = D_FF // FF_CHUNK
    whole = lambda w: pl.BlockSpec((1, n, w), lambda c: (0, 0, 0))
    col_chunk = pl.BlockSpec((DEPTH, D_MODEL, FF_CHUNK), lambda c: (0, 0, c))
    row_chunk = pl.BlockSpec((DEPTH, FF_CHUNK, D_MODEL), lambda c: (0, c, 0))
    return pl.pallas_call(
        _ctx_ffn_kernel,
        grid=(nc,),
        in_specs=[whole(FOURIER_WIDTH), whole(QW), whole(D_MODEL),
                  pl.BlockSpec((1, 1, 6 * D_MODEL), lambda c: (mod_row(0), 0, 0)),
                  pl.BlockSpec((1, D_MODEL), lambda c: (0, 0)),
                  pl.BlockSpec((D_MODEL, D_MODEL), lambda c: (0, 0)),
                  col_chunk, col_chunk, row_chunk],
        out_specs=[whole(D_MODEL), col_chunk, col_chunk, row_chunk],
        out_shape=[jax.ShapeDtypeStruct((1, n, D_MODEL), F32),
                   jax.ShapeDtypeStruct((DEPTH, D_MODEL, D_FF), BF16),
                   jax.ShapeDtypeStruct((DEPTH, D_MODEL, D_FF), BF16),
                   jax.ShapeDtypeStruct((DEPTH, D_FF, D_MODEL), BF16)],
        scratch_shapes=[pltpu.VMEM((n, D_MODEL), F32), pltpu.VMEM((n, D_MODEL), BF16), pltpu.VMEM((n, D_MODEL), F32)],
        compiler_params=_params("arbitrary"),
        name="ctx_ffn_and_weights",
    )(f, a, x, mod, g, w_out, wg32, wu32, wd32)


def _in_odd_kernel(with_q, x_ref, mod_ref, g_ref, w32_ref, qg_ref, kg_ref, hm_ref, *outs):
    W = NA_WIDTH
    slabs = RMS_W // LANES
    if with_q:
        q_ref, k_ref, v_ref, w_ref = outs
    else:
        k_ref, v_ref, w_ref = outs
    _cast_once(w32_ref, w_ref)

    for r in range(x_ref.shape[1] // SUB_ROWS):
        rows = slice(r * SUB_ROWS, (r + 1) * SUB_ROWS)
        h = _rms_mod(x_ref[0, rows], g_ref[...], _mod_slice(mod_ref, 1), _mod_slice(mod_ref, 0)).astype(BF16)

        def normed(t, gain_ref, scale, o_ref):
            for j in range(W // RMS_W):
                tj = (_head_rms(t[:, j * RMS_W:(j + 1) * RMS_W], gain_ref, hm_ref) * scale).astype(BF16)
                for i in range(slabs):
                    o_ref[0, j * slabs + i, rows] = tj[:, i * LANES:(i + 1) * LANES]

        if with_q:
            normed(_dot(h, w_ref[:, :W]), qg_ref, QK_SCALE, q_ref)
        normed(_dot(h, w_ref[:, W:2 * W]), kg_ref, 1.0, k_ref)
        v = _dot(h, w_ref[:, 2 * W:]).astype(BF16)
        for j in range(W // LANES):
            v_ref[0, j, rows] = v[:, j * LANES:(j + 1) * LANES]


def _in_odd(x, mod, mod_row, g, w_in, q_g, k_g, hm, with_q, tm):
    B, L, _ = x.shape
    pairs = NA_WIDTH // LANES
    tok = pl.BlockSpec((1, pairs, tm, LANES), lambda b, t: (b, 0, t, 0))
    n_out = 3 if with_q else 2
    return pl.pallas_call(
        functools.partial(_in_odd_kernel, with_q),
        grid=(B, L // tm),
        in_specs=[pl.BlockSpec((1, tm, D_MODEL), lambda b, t: (b, t, 0)),
                  pl.BlockSpec((1, 1, 6 * D_MODEL), lambda b, t: (mod_row(b), 0, 0)),
                  _const_spec((1, D_MODEL)),
                  _const_spec((D_MODEL, 3 * NA_WIDTH)),
                  _const_spec((1, RMS_W)), _const_spec((1, RMS_W)), _const_spec((RMS_W, RMS_W))],
        out_specs=[tok] * n_out,
        out_shape=[jax.ShapeDtypeStruct((B, pairs, L, LANES), BF16)] * n_out,
        compiler_params=_params("arbitrary", "arbitrary"),
        scratch_shapes=[pltpu.VMEM((D_MODEL, 3 * NA_WIDTH), BF16)],
        name="in_odd" if with_q else "in_odd_ctx",
    )(x, mod, g, w_in, q_g, k_g, hm)


def _bias_table_kernel(rb_ref, idx_ref, o_ref):
    idx = idx_ref[...]
    cq = lax.broadcasted_iota(jnp.int32, idx.shape, 0)
    ck = lax.broadcasted_iota(jnp.int32, idx.shape, 1) % GRID_W
    c0 = jnp.clip(cq - NA_KW // 2, 0, GRID_W - NA_KW)
    inside = (ck >= c0) & (ck < c0 + NA_KW)
    for h in range(o_ref.shape[0]):
        for dr in range(o_ref.shape[1]):
            row = jnp.broadcast_to(rb_ref[h, dr:dr + 1, :], idx.shape)
            o_ref[h, dr] = jnp.where(inside, jnp.take_along_axis(row, idx, axis=1) * LOG2E, -jnp.inf)


def _na_bias_table(rel_bias, idx):
    H = NA_HEADS
    npair = 2 * NA_KH - 2
    half = LANES // 2
    hb = 4
    pad = lambda a: jnp.pad(a, ((0, 0), (0, 0), (0, half - a.shape[-1])))
    rb2 = jnp.concatenate([pad(rel_bias[:, :-1]), pad(rel_bias[:, 1:])], axis=-1)
    return pl.pallas_call(
        _bias_table_kernel,
        grid=(H // hb,),
        in_specs=[pl.BlockSpec((hb, npair, LANES), lambda h: (h, 0, 0)), _const_spec((GRID_W, LANES))],
        out_specs=pl.BlockSpec((hb, npair, GRID_W, LANES), lambda h: (h, 0, 0, 0)),
        out_shape=jax.ShapeDtypeStruct((H, npair, GRID_W, LANES), F32),
        compiler_params=_params("arbitrary"),
        name="na_bias_table",
    )(rb2, idx)


def _na_kernel(rows_per_step, q_ref, k_ref, v_ref, kx_ref, vx_ref, bias_ref, o_ref, s0_ref, s1_ref, p_ref, inv_ref):
    t = pl.program_id(1)
    n = NA_KH * GRID_W
    lo_lanes = lax.broadcasted_iota(jnp.int32, (GRID_W, LANES), 1) < HEAD_DIM

    def window(i):
        r = t * rows_per_step + i
        r0 = jnp.clip(r - NA_KH // 2, 0, GRID_ROWS - NA_KH)
        return r0 - r + NA_KH - 1, pl.multiple_of(r0 * GRID_W, GRID_W), pl.multiple_of(i * GRID_W, GRID_W)

    def scores(i, s_ref):
        dr0, start, qoff = window(i)
        for hp in range(NA_HEADS // 2):
            qs = q_ref[0, hp, pl.ds(qoff, GRID_W), :]
            keys = jnp.concatenate([k_ref[0, hp, pl.ds(start, n), :], kx_ref[0, hp]], axis=0)
            qm = jnp.concatenate([jnp.where(lo_lanes, qs, jnp.zeros_like(qs)),
                                  jnp.where(lo_lanes, jnp.zeros_like(qs), qs)], axis=0)
            bias = jnp.concatenate(
                [jnp.concatenate([bias_ref[2 * hp + half, dr0 + 2 * p] for p in range(NA_KH // 2)], axis=1)
                 for half in range(2)], axis=0)
            rs = slice(2 * hp * GRID_W, (2 * hp + 2) * GRID_W)
            s_ref[rs, :n] = _dot_nt(qm, keys[:n]) + bias
            s_ref[rs, n:] = _dot_nt(qm, keys[n:])

    def attend(i, s_ref):
        _, start, qoff = window(i)
        for c in range(NA_HEADS * GRID_W // SOFTMAX_ROWS):
            cs = slice(c * SOFTMAX_ROWS, (c + 1) * SOFTMAX_ROWS)
            s = s_ref[cs, :]
            e = jnp.exp2(s - jnp.max(s, axis=-1, keepdims=True))
            inv_ref[cs, :] = 1.0 / jnp.sum(e, axis=-1, keepdims=True)
            p_ref[cs, :] = e.astype(BF16)
        for hp in range(NA_HEADS // 2):
            vals = jnp.concatenate([v_ref[0, hp, pl.ds(start, n), :], vx_ref[0, hp]], axis=0)
            rs = slice(2 * hp * GRID_W, (2 * hp + 2) * GRID_W)
            res = _dot(p_ref[rs, :], vals) * inv_ref[rs, :]
            o_ref[0, hp, pl.ds(qoff, GRID_W), :] = jnp.where(lo_lanes, res[:GRID_W], res[GRID_W:]).astype(BF16)

    last = rows_per_step - 1
    scores(0, s0_ref)

    def two_rows(j, carry):
        scores(2 * j + 1, s1_ref)
        attend(2 * j, s0_ref)
        scores(jnp.minimum(2 * j + 2, last), s0_ref)
        attend(2 * j + 1, s1_ref)
        return carry

    lax.fori_loop(0, rows_per_step // 2, two_rows, 0)


def _neighbourhood_attention(q, k, v, kx, vx, bias_tab):
    B, pairs, L, _ = q.shape
    rows_per_step = 16
    full = pl.BlockSpec((1, pairs, L, LANES), lambda b, t: (b, 0, 0, 0))
    ctx_spec = pl.BlockSpec((1, pairs, CTX_LEN, LANES), lambda b, t: (0, 0, b, 0))
    q_rows = pl.BlockSpec((1, pairs, rows_per_step * GRID_W, LANES), lambda b, t: (b, 0, t, 0))
    n_keys = NA_KH * GRID_W + CTX_LEN
    all_rows = NA_HEADS * GRID_W
    return pl.pallas_call(
        functools.partial(_na_kernel, rows_per_step),
        grid=(B, GRID_ROWS // rows_per_step),
        in_specs=[q_rows, full, full, ctx_spec, ctx_spec, _const_spec(bias_tab.shape)],
        out_specs=q_rows,
        out_shape=jax.ShapeDtypeStruct((B, pairs, L, LANES), BF16),
        scratch_shapes=[pltpu.VMEM((all_rows, n_keys), F32), pltpu.VMEM((all_rows, n_keys), F32),
                        pltpu.VMEM((all_rows, n_keys), BF16), pltpu.VMEM((all_rows, 1), F32)],
        compiler_params=_params("arbitrary", "arbitrary"),
        name="neighbourhood_attention",
    )(q, k, v, kx, vx, bias_tab)


def kernel(x, c, ctx, c_ctx, ada_w, ada_b, norm1_g, norm2_g, ffn_w_gate, ffn_w_up, ffn_w_down,
           ev_w_in, ev_w_out, ev_q_norm, ev_k_norm, ev_sink,
           od_w_in, od_w_out, od_q_norm, od_k_norm, od_rel_bias):
    assert x.shape == (BATCH, SEQ, D_MODEL) and ctx.shape == (BATCH, CTX_LEN, D_MODEL)
    wc, m1, m3, twc, tws, mctx = _fourier_tables()
    hm = _head_mean_matrix()
    rope_tabs = _rope_tables()
    lane_gain = lambda gvec: jnp.tile(gvec, RMS_W // HEAD_DIM).reshape(1, RMS_W)
    lat_row = lambda b: b
    ctx_row = lambda b: CTX_MOD_ROW
    tm = 512
    n_ctx = BATCH * CTX_LEN
    as_seq = lambda a: a.reshape(1, n_ctx, a.shape[-1])
    per_batch = lambda a: a.reshape(BATCH, CTX_LEN, a.shape[-1])

    cs = jnp.concatenate([c, c_ctx[None, :], jnp.zeros((MOD_ROWS - BATCH - 1, D_MODEL), F32)], axis=0)
    mod = _modulation(cs, ada_w, ada_b).reshape(DEPTH, MOD_ROWS, 1, 6 * D_MODEL)

    w_in0 = ev_w_in[0]
    w_out0 = ev_w_out[0].astype(BF16)
    g1 = norm1_g[0].reshape(1, D_MODEL)
    g2 = norm2_g[0].reshape(1, D_MODEL)
    qg, kg = lane_gain(ev_q_norm[0]), lane_gain(ev_k_norm[0])
    y_l, q_l, k_l, v_l = _in_even(x, mod[0], lat_row, g1, w_in0, qg, kg, hm, wc, rope_tabs, 2 * tm)
    y_c, q_c, k_c, v_c = _in_even(as_seq(ctx), mod[0], ctx_row, g1, w_in0, qg, kg, hm, wc, None, tm)
    y_c, q_c = per_batch(y_c), per_batch(q_c)
    f_l = _fourier_latent(y_l, m1, m3, twc, tws)
    f_c = _fourier_ctx(y_c, mctx)
    a_l = _win_attention(q_l, k_l, v_l, k_c, v_c, ev_sink[0])
    a_c = _ctx_attention_even(q_c, k_c, v_c, ev_sink[0])
    y1, *ffn = _ctx_ffn_and_weights(as_seq(f_c), as_seq(a_c), as_seq(ctx), mod[0], ctx_row, g2, w_out0,
                                    ffn_w_gate, ffn_w_up, ffn_w_down)
    x1 = _out_ffn("even_latent", (f_l, a_l), x, mod[0], lat_row, g2, w_out0, 0, *ffn, 2 * tm)

    w_in1 = od_w_in[0]
    w_out1 = od_w_out[0].astype(BF16)
    g1 = norm1_g[1].reshape(1, D_MODEL)
    g2 = norm2_g[1].reshape(1, D_MODEL)
    qg, kg = lane_gain(od_q_norm[0]), lane_gain(od_k_norm[0])
    q_l, k_l, v_l = _in_odd(x1, mod[1], lat_row, g1, w_in1, qg, kg, hm, True, 2 * tm)
    k_c, v_c = _in_odd(y1, mod[1], ctx_row, g1, w_in1, qg, kg, hm, False, tm)
    bias_tab = _na_bias_table(od_rel_bias[0], _na_bias_index())
    a_l = _neighbourhood_attention(q_l, k_l, v_l, k_c, v_c, bias_tab)
    return _out_ffn("odd", (a_l,), x1, mod[1], lat_row, g2, w_out1, 1, *ffn, 2 * tm)
```

```python
import functools
import math

import numpy as np
import jax
import jax.numpy as jnp
from jax import lax
from jax.experimental import pallas as pl
from jax.experimental.pallas import tpu as pltpu

D_MODEL = 1024
BATCH = 4
SEQ = 4096
DEPTH = 2
GRID_W = 64
CTX_LEN = 256
HEAD_DIM = 64
EPS = 1e-6
FOURIER_WIDTH = D_MODEL // 2
FOURIER_GROUPS = 4
FOURIER_GROUP_CH = FOURIER_WIDTH // FOURIER_GROUPS
WIN_Q_HEADS = (D_MODEL // 2) // HEAD_DIM
WIN_KV_HEADS = 2
WIN_GROUP = WIN_Q_HEADS // WIN_KV_HEADS
WIN_RADIUS = 128
WIN_BLOCK = 128
QW = WIN_Q_HEADS * HEAD_DIM
KW = WIN_KV_HEADS * HEAD_DIM
EV_IN_WIDTH = FOURIER_WIDTH + QW + 2 * KW
NA_HEADS = D_MODEL // HEAD_DIM
NA_KH = 8
NA_KW = 16
NA_WIDTH = NA_HEADS * HEAD_DIM
ROPE_THETA = 10000.0
ROPE_FREQS = HEAD_DIM // 4
D_FF = ((8 * D_MODEL // 3 + 255) // 256) * 256
GRID_ROWS = SEQ // GRID_W

LANES = 128
MOD_ROWS = 8
CTX_MOD_ROW = BATCH
FFT_N = 64
FF_CHUNK = 256
SOFTMAX_ROWS = 64
RMS_W = 256
SUB_ROWS = 512
FFN_SUB_ROWS = 1024
VMEM_LIMIT = 60 * 1024 * 1024

LOG2E = math.log2(math.e)
QK_SCALE = LOG2E / math.sqrt(HEAD_DIM)

BF16 = jnp.bfloat16
F32 = jnp.float32

assert DEPTH == 2 and SEQ == FFT_N * FFT_N and D_FF % FF_CHUNK == 0


def _params(*sem):
    return pltpu.CompilerParams(dimension_semantics=sem, vmem_limit_bytes=VMEM_LIMIT)


def _dot(a, b):
    return jnp.dot(a, b, preferred_element_type=F32)


def _dot_nt(a, b):
    return lax.dot_general(a, b, (((1,), (1,)), ((), ())), preferred_element_type=F32)


def _silu(x):
    return x / (1.0 + jnp.exp(-x))


def _const_spec(shape):
    nd = len(shape)
    return pl.BlockSpec(shape, lambda *_: (0,) * nd, pipeline_mode=pl.Buffered(1))


def _dft_cos_sin(n):
    idx = (np.arange(n)[:, None] * np.arange(n)[None, :]) % n
    ang = 2.0 * np.pi * idx / n
    return np.cos(ang), np.sin(ang)


def _fourier_tables():
    cc, sc = _dft_cos_sin(FOURIER_GROUP_CH)
    wc = np.concatenate([cc, -sc], axis=1) / math.sqrt(FOURIER_GROUP_CH)
    c64, s64 = _dft_cos_sin(FFT_N)
    m1 = np.concatenate([c64, -s64], axis=0) / math.sqrt(FFT_N)
    m3 = np.concatenate([c64, s64], axis=1) / math.sqrt(FFT_N)
    tw = (np.arange(FFT_N)[:, None] * np.arange(FFT_N)[None, :]) % SEQ
    tw = 2.0 * np.pi * tw / SEQ
    twc = np.repeat(np.cos(tw)[:, :, None], LANES, axis=2)
    tws = np.repeat(np.sin(tw)[:, :, None], LANES, axis=2)
    cx, sx = _dft_cos_sin(CTX_LEN)
    mctx = np.concatenate([cx, sx], axis=1) / math.sqrt(CTX_LEN)
    as32 = lambda a: jnp.asarray(a, F32)
    return (as32(wc).astype(BF16), as32(m1).astype(BF16), as32(m3).astype(BF16),
            as32(twc), as32(tws), as32(mctx).astype(BF16))


def _head_mean_matrix():
    blk = np.kron(np.eye(RMS_W // HEAD_DIM), np.ones((HEAD_DIM, HEAD_DIM))) / HEAD_DIM
    return jnp.asarray(blk, BF16)


def _rope_tables():
    t = np.arange(SEQ)
    row = (t // GRID_W).astype(np.float32)
    col = (t % GRID_W).astype(np.float32)
    inv = np.float32(ROPE_THETA) ** (-np.arange(ROPE_FREQS, dtype=np.float32) / np.float32(ROPE_FREQS))
    ang_row = (row[:, None] * inv[None, :]).astype(np.float32).astype(np.float64)
    ang_col = (col[:, None] * inv[None, :]).astype(np.float32).astype(np.float64)
    zero = np.zeros_like(ang_row)
    cos = np.concatenate([np.cos(ang_row)] * 2 + [np.cos(ang_col)] * 2, axis=1)
    sin_hi = np.concatenate([-np.sin(ang_row), zero, -np.sin(ang_col), zero], axis=1)
    sin_lo = np.concatenate([zero, np.sin(ang_row), zero, np.sin(ang_col)], axis=1)
    rep = LANES // HEAD_DIM
    return tuple(jnp.asarray(np.tile(a, (1, rep)), F32) for a in (cos, sin_hi, sin_lo))


def _na_bias_index():
    cq = np.arange(GRID_W)
    dc = np.clip(cq[None, :] - cq[:, None] + NA_KW - 1, 0, 2 * NA_KW - 2)
    return jnp.asarray(np.concatenate([dc, dc + LANES // 2], axis=1), jnp.int32)


def _mod_kernel(cs_ref, w_ref, b_ref, o_ref):
    s = _silu(cs_ref[...]).astype(BF16)
    o_ref[0] = _dot(s, w_ref[0].astype(BF16)) + b_ref[0]


def _modulation(cs, ada_w, ada_b):
    tn = 1536
    return pl.pallas_call(
        _mod_kernel,
        grid=(DEPTH, 6 * D_MODEL // tn),
        in_specs=[pl.BlockSpec((MOD_ROWS, D_MODEL), lambda i, j: (0, 0)),
                  pl.BlockSpec((1, D_MODEL, tn), lambda i, j: (i, 0, j)),
                  pl.BlockSpec((1, 1, tn), lambda i, j: (i, 0, j))],
        out_specs=pl.BlockSpec((1, MOD_ROWS, tn), lambda i, j: (i, 0, j)),
        out_shape=jax.ShapeDtypeStruct((DEPTH, MOD_ROWS, 6 * D_MODEL), F32),
        compiler_params=_params("arbitrary", "arbitrary"),
        name="ada_modulation",
    )(cs, ada_w, ada_b.reshape(DEPTH, 1, 6 * D_MODEL))


def _mod_slice(mod_ref, k):
    return mod_ref[0, :, k * D_MODEL:(k + 1) * D_MODEL]


def _rms_mod(x, g, scale, shift):
    y = x * lax.rsqrt(jnp.mean(x * x, axis=-1, keepdims=True) + EPS)
    return (y * g) * (1.0 + scale) + shift


def _head_rms(t, gain_ref, hm_ref):
    w = t.shape[1]
    ms = _dot((t * t).astype(BF16), hm_ref[:w, :w])
    return t * lax.rsqrt(ms + EPS) * gain_ref[:, :w]


def _rope(t, cos, sin_hi, sin_lo):
    up = pltpu.roll(t, LANES - ROPE_FREQS, axis=1)
    dn = pltpu.roll(t, ROPE_FREQS, axis=1)
    return t * cos + up * sin_hi + dn * sin_lo


def _cast_once(w32_ref, w_ref):
    @pl.when((pl.program_id(0) == 0) & (pl.program_id(1) == 0))
    def _():
        w_ref[...] = w32_ref[...].astype(BF16)


def _in_even_kernel(latent, x_ref, mod_ref, g_ref, w32_ref, qg_ref, kg_ref, hm_ref, wc_ref, *rest):
    if latent:
        cos_ref, sh_ref, sl_ref, y_ref, q_ref, k_ref, v_ref, w_ref = rest
    else:
        y_ref, q_ref, k_ref, v_ref, w_ref = rest
    _cast_once(w32_ref, w_ref)
    F = FOURIER_WIDTH
    lo_lanes = lax.broadcasted_iota(jnp.int32, (1, LANES), 1) < HEAD_DIM

    for r in range(x_ref.shape[1] // SUB_ROWS):
        rows = slice(r * SUB_ROWS, (r + 1) * SUB_ROWS)
        h = _rms_mod(x_ref[0, rows], g_ref[...], _mod_slice(mod_ref, 1), _mod_slice(mod_ref, 0)).astype(BF16)
        f = _dot(h, w_ref[:, :F]).astype(BF16)
        for g in range(FOURIER_GROUPS):
            yg = _dot(f[:, g * LANES:(g + 1) * LANES], wc_ref[...])
            y_ref[0, rows, g * LANES:(g + 1) * LANES] = yg[:, :LANES].astype(y_ref.dtype)
            y_ref[0, rows, F + g * LANES:F + (g + 1) * LANES] = yg[:, LANES:].astype(y_ref.dtype)

        def qk(t, gain_ref):
            t = _head_rms(t, gain_ref, hm_ref)
            if latent:
                t = jnp.concatenate([_rope(t[:, i * LANES:(i + 1) * LANES], cos_ref[rows], sh_ref[rows], sl_ref[rows])
                                     for i in range(t.shape[1] // LANES)], axis=1)
            return t

        q = _dot(h, w_ref[:, F:F + QW])
        for j in range(QW // RMS_W):
            qj = qk(q[:, j * RMS_W:(j + 1) * RMS_W], qg_ref) * QK_SCALE
            q_ref[0, rows, j * RMS_W:(j + 1) * RMS_W] = qj.astype(BF16)
        kv = _dot(h, w_ref[:, F + QW:])

        def store_dup(t, o_ref):
            sw = pltpu.roll(t, HEAD_DIM, axis=1)
            o_ref[0, 0, rows] = jnp.where(lo_lanes, t, sw).astype(BF16)
            o_ref[0, 1, rows] = jnp.where(lo_lanes, sw, t).astype(BF16)

        store_dup(qk(kv[:, :KW], kg_ref), k_ref)
        store_dup(kv[:, KW:], v_ref)


def _in_even(x, mod, mod_row, g, w_in, q_g, k_g, hm, wc, rope_tabs, tm):
    B, L, _ = x.shape
    latent = rope_tabs is not None
    in_specs = [pl.BlockSpec((1, tm, D_MODEL), lambda b, t: (b, t, 0)),
                pl.BlockSpec((1, 1, 6 * D_MODEL), lambda b, t: (mod_row(b), 0, 0)),
                _const_spec((1, D_MODEL)),
                _const_spec((D_MODEL, EV_IN_WIDTH)),
                _const_spec((1, RMS_W)), _const_spec((1, RMS_W)),
                _const_spec((RMS_W, RMS_W)), _const_spec((LANES, 2 * LANES))]
    args = [x, mod, g, w_in, q_g, k_g, hm, wc]
    tok = lambda w: pl.BlockSpec((1, tm, w), lambda b, t: (b, t, 0))
    kv_spec = pl.BlockSpec((1, WIN_KV_HEADS, tm, LANES), lambda b, t: (b, 0, t, 0))
    if latent:
        in_specs += [pl.BlockSpec((tm, LANES), lambda b, t: (t, 0))] * 3
        args += list(rope_tabs)
    return pl.pallas_call(
        functools.partial(_in_even_kernel, latent),
        grid=(B, L // tm),
        in_specs=in_specs,
        out_specs=[tok(2 * FOURIER_WIDTH), tok(QW), kv_spec, kv_spec],
        out_shape=[jax.ShapeDtypeStruct((B, L, 2 * FOURIER_WIDTH), F32 if latent else BF16),
                   jax.ShapeDtypeStruct((B, L, QW), BF16),
                   jax.ShapeDtypeStruct((B, WIN_KV_HEADS, L, LANES), BF16),
                   jax.ShapeDtypeStruct((B, WIN_KV_HEADS, L, LANES), BF16)],
        compiler_params=_params("arbitrary", "arbitrary"),
        scratch_shapes=[pltpu.VMEM((D_MODEL, EV_IN_WIDTH), BF16)],
        name="in_even_latent" if latent else "in_even_ctx",
    )(*args)


def _fft_stage1_kernel(n2t, y_ref, m1_ref, twc_ref, tws_ref, o_ref, row_ref):
    F = FOURIER_WIDTH
    for j in range(n2t):
        row_ref[j % 2] = y_ref[0, :, j, :]
        a = _dot(m1_ref[...], row_ref[j % 2].astype(BF16))
        top, bot = a[:FFT_N], a[FFT_N:]
        ar = top[:, :F] - bot[:, F:]
        ai = top[:, F:] + bot[:, :F]
        tc = jnp.tile(twc_ref[j], (1, F // LANES))
        ts = jnp.tile(tws_ref[j], (1, F // LANES))
        o_ref[0, 0, :, j * F:(j + 1) * F] = (ar * tc + ai * ts).astype(BF16)
        o_ref[0, 1, :, j * F:(j + 1) * F] = (ai * tc - ar * ts).astype(BF16)


def _fft_stage2_kernel(k1t, b_ref, m3_ref, o_ref):
    for j in range(k1t):
        rhs = jnp.concatenate([b_ref[0, 0, j], b_ref[0, 1, j]], axis=0)
        o_ref[0, j] = _dot(m3_ref[...], rhs)


def _fourier_latent(y, m1, m3, twc, tws):
    B = y.shape[0]
    F = FOURIER_WIDTH
    n2t = 16
    yv = y.reshape(B, FFT_N, FFT_N, 2 * F)
    bh = pl.pallas_call(
        functools.partial(_fft_stage1_kernel, n2t),
        grid=(B, FFT_N // n2t),
        in_specs=[pl.BlockSpec((1, FFT_N, n2t, 2 * F), lambda b, t: (b, 0, t, 0)),
                  _const_spec((2 * FFT_N, FFT_N)),
                  pl.BlockSpec((n2t, FFT_N, LANES), lambda b, t: (t, 0, 0)),
                  pl.BlockSpec((n2t, FFT_N, LANES), lambda b, t: (t, 0, 0))],
        out_specs=pl.BlockSpec((1, 2, FFT_N, n2t * F), lambda b, t: (b, 0, 0, t)),
        out_shape=jax.ShapeDtypeStruct((B, 2, FFT_N, FFT_N * F), BF16),
        scratch_shapes=[pltpu.VMEM((2, FFT_N, 2 * F), F32)],
        compiler_params=_params("arbitrary", "arbitrary"),
        name="fft_stage1",
    )(yv, m1, twc, tws)
    k1t = 16
    bv = bh.reshape(B, 2, FFT_N, FFT_N, F)
    return pl.pallas_call(
        functools.partial(_fft_stage2_kernel, k1t),
        grid=(B, FFT_N // k1t),
        in_specs=[pl.BlockSpec((1, 2, k1t, FFT_N, F), lambda b, t: (b, 0, t, 0, 0)),
                  _const_spec((FFT_N, 2 * FFT_N))],
        out_specs=pl.BlockSpec((1, k1t, FFT_N, F), lambda b, t: (b, t, 0, 0)),
        out_shape=jax.ShapeDtypeStruct((B, FFT_N, FFT_N, F), F32),
        compiler_params=_params("arbitrary", "arbitrary"),
        name="fft_stage2",
    )(bv, m3)


def _fourier_ctx_kernel(y_ref, m_ref, o_ref):
    F = FOURIER_WIDTH
    rhs = jnp.concatenate([y_ref[0, :, :F], y_ref[0, :, F:]], axis=0)
    o_ref[0] = _dot(m_ref[...], rhs)


def _fourier_ctx(y, mctx):
    B = y.shape[0]
    return pl.pallas_call(
        _fourier_ctx_kernel,
        grid=(B,),
        in_specs=[pl.BlockSpec((1, CTX_LEN, 2 * FOURIER_WIDTH), lambda b: (b, 0, 0)),
                  _const_spec((CTX_LEN, 2 * CTX_LEN))],
        out_specs=pl.BlockSpec((1, CTX_LEN, FOURIER_WIDTH), lambda b: (b, 0, 0)),
        out_shape=jax.ShapeDtypeStruct((B, CTX_LEN, FOURIER_WIDTH), F32),
        compiler_params=_params("arbitrary"),
        name="fourier_ctx",
    )(y, mctx)


def _gqa_scores(q_slabs, keys, rows, s_ref):
    lo_lanes = lax.broadcasted_iota(jnp.int32, (rows, LANES), 1) < HEAD_DIM
    zero = jnp.zeros((rows, LANES), BF16)
    qs = jnp.concatenate([jnp.where(lo_lanes if half == 0 else ~lo_lanes, qslab, zero)
                          for qslab in q_slabs for half in range(2)], axis=0)
    s_ref[...] = _dot_nt(qs, keys)


def _gqa_attend(vals, masks, sinks, rows, out_slab, first_slab, scratch):
    s_ref, p_ref, inv_ref = scratch
    lo_lanes = lax.broadcasted_iota(jnp.int32, (rows, LANES), 1) < HEAD_DIM
    for c in range(WIN_GROUP * rows // SOFTMAX_ROWS):
        rs = slice(c * SOFTMAX_ROWS, (c + 1) * SOFTMAX_ROWS)
        s = s_ref[rs, :]
        if masks is not None:
            band, w = masks
            r0 = (c * SOFTMAX_ROWS) % rows
            s = jnp.concatenate([s[:, :w] + band[r0:r0 + SOFTMAX_ROWS], s[:, w:]], axis=1)
        sk = sinks[(c * SOFTMAX_ROWS) // rows]
        m = jnp.maximum(jnp.max(s, axis=-1, keepdims=True), sk)
        e = jnp.exp2(s - m)
        inv_ref[rs, :] = 1.0 / (jnp.sum(e, axis=-1, keepdims=True) + jnp.exp2(sk - m))
        p_ref[rs, :] = e.astype(BF16)
    o = _dot(p_ref[...], vals) * inv_ref[...]
    for i in range(WIN_GROUP // 2):
        a = o[(2 * i) * rows:(2 * i + 1) * rows]
        b = o[(2 * i + 1) * rows:(2 * i + 2) * rows]
        sl = slice((first_slab + i) * LANES, (first_slab + i + 1) * LANES)
        out_slab(sl)[...] = jnp.where(lo_lanes, a, b).astype(BF16)


def _gqa_scratch(rows, n_keys):
    one = [pltpu.VMEM((WIN_GROUP * rows, n_keys), F32), pltpu.VMEM((WIN_GROUP * rows, n_keys), BF16),
           pltpu.VMEM((WIN_GROUP * rows, 1), F32)]
    return one * WIN_KV_HEADS


def _group_sinks(sink_ref, kv):
    return [sink_ref[kv * WIN_GROUP + g] * LOG2E for g in range(WIN_GROUP)]


def _group_slab(kv, i):
    return slice((kv * (WIN_GROUP // 2) + i) * LANES, (kv * (WIN_GROUP // 2) + i + 1) * LANES)


def _window_bands():
    i = np.arange(WIN_BLOCK)[:, None]
    j = np.arange(3 * WIN_BLOCK)[None, :]
    bands = [np.where(np.abs(j - off - i) <= WIN_RADIUS, 0.0, -np.inf) for off in (0, WIN_BLOCK, 2 * WIN_BLOCK)]
    return jnp.asarray(np.stack(bands).astype(np.float32))


def _win_attn_kernel(blocks_per_step, sink_ref, band_ref, q_ref, k_ref, v_ref, kx_ref, vx_ref, o_ref,
                     s0_ref, s1_ref, p_ref, inv_ref):
    t = pl.program_id(1)
    nb = SEQ // WIN_BLOCK
    nw = 3 * WIN_BLOCK

    def place(j):
        n = t * blocks_per_step + j
        start = pl.multiple_of(jnp.clip((n - 1) * WIN_BLOCK, 0, SEQ - nw), WIN_BLOCK)
        sel = jnp.where(n == 0, 0, jnp.where(n == nb - 1, 2, 1))
        return pl.multiple_of(j * WIN_BLOCK, WIN_BLOCK), start, sel

    def scores(j, kv, s_ref):
        qoff, start, _ = place(j)
        keys = jnp.concatenate([k_ref[0, kv, pl.ds(start, nw), :], kx_ref[0, kv]], axis=0)
        q_slabs = [q_ref[0, pl.ds(qoff, WIN_BLOCK), _group_slab(kv, i)] for i in range(WIN_GROUP // 2)]
        _gqa_scores(q_slabs, keys, WIN_BLOCK, s_ref)

    def attend(j, kv, s_ref):
        qoff, start, sel = place(j)
        vals = jnp.concatenate([v_ref[0, kv, pl.ds(start, nw), :], vx_ref[0, kv]], axis=0)
        _gqa_attend(vals, (band_ref[sel], nw), _group_sinks(sink_ref, kv), WIN_BLOCK,
                    lambda sl: o_ref.at[0, pl.ds(qoff, WIN_BLOCK), sl], kv * (WIN_GROUP // 2),
                    (s_ref, p_ref, inv_ref))

    scores(0, 0, s0_ref)

    def one_block(j, carry):
        scores(j, 1, s1_ref)
        attend(j, 0, s0_ref)
        scores(jnp.minimum(j + 1, blocks_per_step - 1), 0, s0_ref)
        attend(j, 1, s1_ref)
        return carry

    lax.fori_loop(0, blocks_per_step, one_block, 0)


def _win_attention(q, k, v, kx, vx, sink):
    B, L, _ = q.shape
    blocks_per_step = 16
    rows = blocks_per_step * WIN_BLOCK
    n_keys = 3 * WIN_BLOCK + CTX_LEN
    full = pl.BlockSpec((1, WIN_KV_HEADS, L, LANES), lambda b, t: (b, 0, 0, 0))
    ctx_spec = pl.BlockSpec((1, WIN_KV_HEADS, CTX_LEN, LANES), lambda b, t: (0, 0, b, 0))
    tok = pl.BlockSpec((1, rows, QW), lambda b, t: (b, t, 0))
    return pl.pallas_call(
        functools.partial(_win_attn_kernel, blocks_per_step),
        grid=(B, L // rows),
        in_specs=[pl.BlockSpec(memory_space=pltpu.SMEM),
                  _const_spec((3, WIN_BLOCK, 3 * WIN_BLOCK)),
                  tok, full, full, ctx_spec, ctx_spec],
        out_specs=tok,
        out_shape=jax.ShapeDtypeStruct((B, L, QW), BF16),
        scratch_shapes=[pltpu.VMEM((WIN_GROUP * WIN_BLOCK, n_keys), F32), pltpu.VMEM((WIN_GROUP * WIN_BLOCK, n_keys), F32),
                        pltpu.VMEM((WIN_GROUP * WIN_BLOCK, n_keys), BF16), pltpu.VMEM((WIN_GROUP * WIN_BLOCK, 1), F32)],
        compiler_params=_params("arbitrary", "arbitrary"),
        name="window_attention",
    )(sink, _window_bands(), q, k, v, kx, vx)


def _ctx_attn_even_kernel(sink_ref, q_ref, k_ref, v_ref, o_ref, *scratch):
    for b in range(q_ref.shape[0]):
        tokens = slice(b * CTX_LEN, (b + 1) * CTX_LEN)
        for kv in range(WIN_KV_HEADS):
            unit = scratch[3 * (b * WIN_KV_HEADS + kv):3 * (b * WIN_KV_HEADS + kv) + 3]
            q_slabs = [q_ref[b, :, _group_slab(kv, i)] for i in range(WIN_GROUP // 2)]
            _gqa_scores(q_slabs, k_ref[0, kv, tokens, :], CTX_LEN, unit[0])
            _gqa_attend(v_ref[0, kv, tokens, :], None, _group_sinks(sink_ref, kv), CTX_LEN,
                        lambda sl, b=b: o_ref.at[b, :, sl], kv * (WIN_GROUP // 2), unit)


def _ctx_attention_even(q, k, v, sink):
    B = q.shape[0]
    whole = lambda a: pl.BlockSpec(a.shape, lambda i: (0,) * a.ndim)
    return pl.pallas_call(
        _ctx_attn_even_kernel,
        grid=(1,),
        in_specs=[pl.BlockSpec(memory_space=pltpu.SMEM), whole(q), whole(k), whole(v)],
        out_specs=whole(q),
        out_shape=jax.ShapeDtypeStruct((B, CTX_LEN, QW), BF16),
        scratch_shapes=_gqa_scratch(CTX_LEN, CTX_LEN) * B,
        compiler_params=_params("arbitrary"),
        name="ctx_attention_even",
    )(sink, q, k, v)


def _out_ffn_kernel(mode, *refs):
    if mode == "odd":
        a_ref, x_ref, mod_ref, g_ref, wo_ref, wg_ref, wu_ref, wd_ref, o_ref = refs
    else:
        f_ref, a_ref, x_ref, mod_ref, g_ref, wo_ref, wg_ref, wu_ref, wd_ref, o_ref = refs
    for r in range(x_ref.shape[1] // FFN_SUB_ROWS):
        rows = slice(r * FFN_SUB_ROWS, (r + 1) * FFN_SUB_ROWS)
        if mode == "even_latent":
            planes = range(r * FFN_SUB_ROWS // FFT_N, (r + 1) * FFN_SUB_ROWS // FFT_N)
            fm = jnp.concatenate([f_ref[0, :, j, :] for j in planes], axis=0).astype(BF16)
            o = _dot(fm, wo_ref[:FOURIER_WIDTH]) + _dot(a_ref[0, rows], wo_ref[FOURIER_WIDTH:])
        else:
            o = _dot(jnp.concatenate([a_ref[0, j, rows] for j in range(a_ref.shape[1])], axis=1), wo_ref[...])
        x1 = x_ref[0, rows] + _mod_slice(mod_ref, 2) * o
        h = _rms_mod(x1, g_ref[...], _mod_slice(mod_ref, 4), _mod_slice(mod_ref, 3)).astype(BF16)
        acc = jnp.zeros_like(x1)
        for c in range(D_FF // FF_CHUNK):
            cs = slice(c * FF_CHUNK, (c + 1) * FF_CHUNK)
            a = _silu(_dot(h, wg_ref[0, :, cs])) * _dot(h, wu_ref[0, :, cs])
            acc = acc + _dot(a.astype(BF16), wd_ref[0, cs, :])
        o_ref[0, rows] = x1 + _mod_slice(mod_ref, 5) * acc


def _out_ffn(mode, mix, x, mod, mod_row, g, w_out, layer, wg, wu, wd, tm):
    B, L, _ = x.shape
    tok = lambda w: pl.BlockSpec((1, tm, w), lambda b, t: (b, t, 0))
    ffn_spec = lambda r, c: pl.BlockSpec((1, r, c), lambda b, t: (layer, 0, 0), pipeline_mode=pl.Buffered(1))
    assert mode in ("even_latent", "odd") and tm % FFN_SUB_ROWS == 0
    if mode == "even_latent":
        mix_specs = [pl.BlockSpec((1, FFT_N, tm // FFT_N, FOURIER_WIDTH), lambda b, t: (b, 0, t, 0)), tok(QW)]
    else:
        mix_specs = [pl.BlockSpec((1, NA_WIDTH // LANES, tm, LANES), lambda b, t: (b, 0, t, 0))]
    return pl.pallas_call(
        functools.partial(_out_ffn_kernel, mode),
        grid=(B, L // tm),
        in_specs=mix_specs + [tok(D_MODEL),
                              pl.BlockSpec((1, 1, 6 * D_MODEL), lambda b, t: (mod_row(b), 0, 0)),
                              _const_spec((1, D_MODEL)),
                              _const_spec((D_MODEL, D_MODEL)),
                              ffn_spec(D_MODEL, D_FF), ffn_spec(D_MODEL, D_FF), ffn_spec(D_FF, D_MODEL)],
        out_specs=tok(D_MODEL),
        out_shape=jax.ShapeDtypeStruct((B, L, D_MODEL), F32),
        compiler_params=_params("arbitrary", "arbitrary"),
        name="out_ffn_" + mode,
    )(*mix, x, mod, g, w_out, wg, wu, wd)


def _ctx_ffn_kernel(f_ref, a_ref, x_ref, mod_ref, g_ref, wo_ref, wg32_ref, wu32_ref, wd32_ref,
                    o_ref, wg_ref, wu_ref, wd_ref, x1_ref, h_ref, acc_ref):
    c = pl.program_id(0)
    wg_ref[...] = wg32_ref[...].astype(BF16)
    wu_ref[...] = wu32_ref[...].astype(BF16)
    wd_ref[...] = wd32_ref[...].astype(BF16)

    @pl.when(c == 0)
    def _():
        o = _dot(f_ref[0].astype(BF16), wo_ref[:FOURIER_WIDTH]) + _dot(a_ref[0], wo_ref[FOURIER_WIDTH:])
        x1 = x_ref[0] + _mod_slice(mod_ref, 2) * o
        x1_ref[...] = x1
        h_ref[...] = _rms_mod(x1, g_ref[...], _mod_slice(mod_ref, 4), _mod_slice(mod_ref, 3)).astype(BF16)
        acc_ref[...] = jnp.zeros_like(acc_ref)

    h = h_ref[...]
    a = _silu(_dot(h, wg_ref[0])) * _dot(h, wu_ref[0])
    acc_ref[...] += _dot(a.astype(BF16), wd_ref[0])

    @pl.when(c == pl.num_programs(0) - 1)
    def _():
        o_ref[0] = x1_ref[...] + _mod_slice(mod_ref, 5) * acc_ref[...]


def _ctx_ffn_and_weights(f, a, x, mod, mod_row, g, w_out, wg32, wu32, wd32):
    _, n, _ = x.shape
    nc = D_FF // FF_CHUNK
    whole = lambda w: pl.BlockSpec((1, n, w), lambda c: (0, 0, 0))
    col_chunk = pl.BlockSpec((DEPTH, D_MODEL, FF_CHUNK), lambda c: (0, 0, c))
    row_chunk = pl.BlockSpec((DEPTH, FF_CHUNK, D_MODEL), lambda c: (0, c, 0))
    return pl.pallas_call(
        _ctx_ffn_kernel,
        grid=(nc,),
        in_specs=[whole(FOURIER_WIDTH), whole(QW), whole(D_MODEL),
                  pl.BlockSpec((1, 1, 6 * D_MODEL), lambda c: (mod_row(0), 0, 0)),
                  pl.BlockSpec((1, D_MODEL), lambda c: (0, 0)),
                  pl.BlockSpec((D_MODEL, D_MODEL), lambda c: (0, 0)),
                  col_chunk, col_chunk, row_chunk],
        out_specs=[whole(D_MODEL), col_chunk, col_chunk, row_chunk],
        out_shape=[jax.ShapeDtypeStruct((1, n, D_MODEL), F32),
                   jax.ShapeDtypeStruct((DEPTH, D_MODEL, D_FF), BF16),
                   jax.ShapeDtypeStruct((DEPTH, D_MODEL, D_FF), BF16),
                   jax.ShapeDtypeStruct((DEPTH, D_FF, D_MODEL), BF16)],
        scratch_shapes=[pltpu.VMEM((n, D_MODEL), F32), pltpu.VMEM((n, D_MODEL), BF16), pltpu.VMEM((n, D_MODEL), F32)],
        compiler_params=_params("arbitrary"),
        name="ctx_ffn_and_weights",
    )(f, a, x, mod, g, w_out, wg32, wu32, wd32)


def _in_odd_kernel(with_q, x_ref, mod_ref, g_ref, w32_ref, qg_ref, kg_ref, hm_ref, *outs):
    W = NA_WIDTH
    slabs = RMS_W // LANES
    if with_q:
        q_ref, k_ref, v_ref, w_ref = outs
    else:
        k_ref, v_ref, w_ref = outs
    _cast_once(w32_ref, w_ref)

    for r in range(x_ref.shape[1] // SUB_ROWS):
        rows = slice(r * SUB_ROWS, (r + 1) * SUB_ROWS)
        h = _rms_mod(x_ref[0, rows], g_ref[...], _mod_slice(mod_ref, 1), _mod_slice(mod_ref, 0)).astype(BF16)

        def normed(t, gain_ref, scale, o_ref):
            for j in range(W // RMS_W):
                tj = (_head_rms(t[:, j * RMS_W:(j + 1) * RMS_W], gain_ref, hm_ref) * scale).astype(BF16)
                for i in range(slabs):
                    o_ref[0, j * slabs + i, rows] = tj[:, i * LANES:(i + 1) * LANES]

        if with_q:
            normed(_dot(h, w_ref[:, :W]), qg_ref, QK_SCALE, q_ref)
        normed(_dot(h, w_ref[:, W:2 * W]), kg_ref, 1.0, k_ref)
        v = _dot(h, w_ref[:, 2 * W:]).astype(BF16)
        for j in range(W // LANES):
            v_ref[0, j, rows] = v[:, j * LANES:(j + 1) * LANES]


def _in_odd(x, mod, mod_row, g, w_in, q_g, k_g, hm, with_q, tm):
    B, L, _ = x.shape
    pairs = NA_WIDTH // LANES
    tok = pl.BlockSpec((1, pairs, tm, LANES), lambda b, t: (b, 0, t, 0))
    n_out = 3 if with_q else 2
    return pl.pallas_call(
        functools.partial(_in_odd_kernel, with_q),
        grid=(B, L // tm),
        in_specs=[pl.BlockSpec((1, tm, D_MODEL), lambda b, t: (b, t, 0)),
                  pl.BlockSpec((1, 1, 6 * D_MODEL), lambda b, t: (mod_row(b), 0, 0)),
                  _const_spec((1, D_MODEL)),
                  _const_spec((D_MODEL, 3 * NA_WIDTH)),
                  _const_spec((1, RMS_W)), _const_spec((1, RMS_W)), _const_spec((RMS_W, RMS_W))],
        out_specs=[tok] * n_out,
        out_shape=[jax.ShapeDtypeStruct((B, pairs, L, LANES), BF16)] * n_out,
        compiler_params=_params("arbitrary", "arbitrary"),
        scratch_shapes=[pltpu.VMEM((D_MODEL, 3 * NA_WIDTH), BF16)],
        name="in_odd" if with_q else "in_odd_ctx",
    )(x, mod, g, w_in, q_g, k_g, hm)


def _bias_table_kernel(rb_ref, idx_ref, o_ref):
    idx = idx_ref[...]
    cq = lax.broadcasted_iota(jnp.int32, idx.shape, 0)
    ck = lax.broadcasted_iota(jnp.int32, idx.shape, 1) % GRID_W
    c0 = jnp.clip(cq - NA_KW // 2, 0, GRID_W - NA_KW)
    inside = (ck >= c0) & (ck < c0 + NA_KW)
    for h in range(o_ref.shape[0]):
        for dr in range(o_ref.shape[1]):
            row = jnp.broadcast_to(rb_ref[h, dr:dr + 1, :], idx.shape)
            o_ref[h, dr] = jnp.where(inside, jnp.take_along_axis(row, idx, axis=1) * LOG2E, -jnp.inf)


def _na_bias_table(rel_bias, idx):
    H = NA_HEADS
    npair = 2 * NA_KH - 2
    half = LANES // 2
    hb = 8
    pad = lambda a: jnp.pad(a, ((0, 0), (0, 0), (0, half - a.shape[-1])))
    rb2 = jnp.concatenate([pad(rel_bias[:, :-1]), pad(rel_bias[:, 1:])], axis=-1)
    return pl.pallas_call(
        _bias_table_kernel,
        grid=(H // hb,),
        in_specs=[pl.BlockSpec((hb, npair, LANES), lambda h: (h, 0, 0)), _const_spec((GRID_W, LANES))],
        out_specs=pl.BlockSpec((hb, npair, GRID_W, LANES), lambda h: (h, 0, 0, 0)),
        out_shape=jax.ShapeDtypeStruct((H, npair, GRID_W, LANES), F32),
        compiler_params=_params("arbitrary"),
        name="na_bias_table",
    )(rb2, idx)


def _na_kernel(rows_per_step, q_ref, k_ref, v_ref, kx_ref, vx_ref, bias_ref, o_ref, s0_ref, s1_ref, p_ref, inv_ref):
    t = pl.program_id(1)
    n = NA_KH * GRID_W
    lo_lanes = lax.broadcasted_iota(jnp.int32, (GRID_W, LANES), 1) < HEAD_DIM

    def window(i):
        r = t * rows_per_step + i
        r0 = jnp.clip(r - NA_KH // 2, 0, GRID_ROWS - NA_KH)
        return r0 - r + NA_KH - 1, pl.multiple_of(r0 * GRID_W, GRID_W), pl.multiple_of(i * GRID_W, GRID_W)

    def scores(i, s_ref):
        dr0, start, qoff = window(i)
        for hp in range(NA_HEADS // 2):
            qs = q_ref[0, hp, pl.ds(qoff, GRID_W), :]
            keys = jnp.concatenate([k_ref[0, hp, pl.ds(start, n), :], kx_ref[0, hp]], axis=0)
            qm = jnp.concatenate([jnp.where(lo_lanes, qs, jnp.zeros_like(qs)),
                                  jnp.where(lo_lanes, jnp.zeros_like(qs), qs)], axis=0)
            bias = jnp.concatenate(
                [jnp.concatenate([bias_ref[2 * hp + half, dr0 + 2 * p] for p in range(NA_KH // 2)], axis=1)
                 for half in range(2)], axis=0)
            rs = slice(2 * hp * GRID_W, (2 * hp + 2) * GRID_W)
            s_ref[rs, :n] = _dot_nt(qm, keys[:n]) + bias
            s_ref[rs, n:] = _dot_nt(qm, keys[n:])

    def attend(i, s_ref):
        _, start, qoff = window(i)
        for c in range(NA_HEADS * GRID_W // SOFTMAX_ROWS):
            cs = slice(c * SOFTMAX_ROWS, (c + 1) * SOFTMAX_ROWS)
            s = s_ref[cs, :]
            e = jnp.exp2(s - jnp.max(s, axis=-1, keepdims=True))
            inv_ref[cs, :] = 1.0 / jnp.sum(e, axis=-1, keepdims=True)
            p_ref[cs, :] = e.astype(BF16)
        for hp in range(NA_HEADS // 2):
            vals = jnp.concatenate([v_ref[0, hp, pl.ds(start, n), :], vx_ref[0, hp]], axis=0)
            rs = slice(2 * hp * GRID_W, (2 * hp + 2) * GRID_W)
            res = _dot(p_ref[rs, :], vals) * inv_ref[rs, :]
            o_ref[0, hp, pl.ds(qoff, GRID_W), :] = jnp.where(lo_lanes, res[:GRID_W], res[GRID_W:]).astype(BF16)

    last = rows_per_step - 1
    scores(0, s0_ref)

    def two_rows(j, carry):
        scores(2 * j + 1, s1_ref)
        attend(2 * j, s0_ref)
        scores(jnp.minimum(2 * j + 2, last), s0_ref)
        attend(2 * j + 1, s1_ref)
        return carry

    lax.fori_loop(0, rows_per_step // 2, two_rows, 0)


def _neighbourhood_attention(q, k, v, kx, vx, bias_tab):
    B, pairs, L, _ = q.shape
    rows_per_step = 16
    full = pl.BlockSpec((1, pairs, L, LANES), lambda b, t: (b, 0, 0, 0))
    ctx_spec = pl.BlockSpec((1, pairs, CTX_LEN, LANES), lambda b, t: (0, 0, b, 0))
    q_rows = pl.BlockSpec((1, pairs, rows_per_step * GRID_W, LANES), lambda b, t: (b, 0, t, 0))
    n_keys = NA_KH * GRID_W + CTX_LEN
    all_rows = NA_HEADS * GRID_W
    return pl.pallas_call(
        functools.partial(_na_kernel, rows_per_step),
        grid=(B, GRID_ROWS // rows_per_step),
        in_specs=[q_rows, full, full, ctx_spec, ctx_spec, _const_spec(bias_tab.shape)],
        out_specs=q_rows,
        out_shape=jax.ShapeDtypeStruct((B, pairs, L, LANES), BF16),
        scratch_shapes=[pltpu.VMEM((all_rows, n_keys), F32), pltpu.VMEM((all_rows, n_keys), F32),
                        pltpu.VMEM((all_rows, n_keys), BF16), pltpu.VMEM((all_rows, 1), F32)],
        compiler_params=_params("arbitrary", "arbitrary"),
        name="neighbourhood_attention",
    )(q, k, v, kx, vx, bias_tab)


def kernel(x, c, ctx, c_ctx, ada_w, ada_b, norm1_g, norm2_g, ffn_w_gate, ffn_w_up, ffn_w_down,
           ev_w_in, ev_w_out, ev_q_norm, ev_k_norm, ev_sink,
           od_w_in, od_w_out, od_q_norm, od_k_norm, od_rel_bias):
    assert x.shape == (BATCH, SEQ, D_MODEL) and ctx.shape == (BATCH, CTX_LEN, D_MODEL)
    wc, m1, m3, twc, tws, mctx = _fourier_tables()
    hm = _head_mean_matrix()
    rope_tabs = _rope_tables()
    lane_gain = lambda gvec: jnp.tile(gvec, RMS_W // HEAD_DIM).reshape(1, RMS_W)
    lat_row = lambda b: b
    ctx_row = lambda b: CTX_MOD_ROW
    tm = 512
    n_ctx = BATCH * CTX_LEN
    as_seq = lambda a: a.reshape(1, n_ctx, a.shape[-1])
    per_batch = lambda a: a.reshape(BATCH, CTX_LEN, a.shape[-1])

    cs = jnp.concatenate([c, c_ctx[None, :], jnp.zeros((MOD_ROWS - BATCH - 1, D_MODEL), F32)], axis=0)
    mod = _modulation(cs, ada_w, ada_b).reshape(DEPTH, MOD_ROWS, 1, 6 * D_MODEL)

    w_in0 = ev_w_in[0]
    w_out0 = ev_w_out[0].astype(BF16)
    g1 = norm1_g[0].reshape(1, D_MODEL)
    g2 = norm2_g[0].reshape(1, D_MODEL)
    qg, kg = lane_gain(ev_q_norm[0]), lane_gain(ev_k_norm[0])
    y_l, q_l, k_l, v_l = _in_even(x, mod[0], lat_row, g1, w_in0, qg, kg, hm, wc, rope_tabs, 2 * tm)
    y_c, q_c, k_c, v_c = _in_even(as_seq(ctx), mod[0], ctx_row, g1, w_in0, qg, kg, hm, wc, None, tm)
    y_c, q_c = per_batch(y_c), per_batch(q_c)
    f_l = _fourier_latent(y_l, m1, m3, twc, tws)
    f_c = _fourier_ctx(y_c, mctx)
    a_l = _win_attention(q_l, k_l, v_l, k_c, v_c, ev_sink[0])
    a_c = _ctx_attention_even(q_c, k_c, v_c, ev_sink[0])
    y1, *ffn = _ctx_ffn_and_weights(as_seq(f_c), as_seq(a_c), as_seq(ctx), mod[0], ctx_row, g2, w_out0,
                                    ffn_w_gate, ffn_w_up, ffn_w_down)
    x1 = _out_ffn("even_latent", (f_l, a_l), x, mod[0], lat_row, g2, w_out0, 0, *ffn, 2 * tm)

    w_in1 = od_w_in[0]
    w_out1 = od_w_out[0].astype(BF16)
    g1 = norm1_g[1].reshape(1, D_MODEL)
    g2 = norm2_g[1].reshape(1, D_MODEL)
    qg, kg = lane_gain(od_q_norm[0]), lane_gain(od_k_norm[0])
    q_l, k_l, v_l = _in_odd(x1, mod[1], lat_row, g1, w_in1, qg, kg, hm, True, 2 * tm)
    k_c, v_c = _in_odd(y1, mod[1], ctx_row, g1, w_in1, qg, kg, hm, False, n_ctx)
    bias_tab = _na_bias_table(od_rel_bias[0], _na_bias_index())
    a_l = _neighbourhood_attention(q_l, k_l, v_l, k_c, v_c, bias_tab)
    return _out_ffn("odd", (a_l,), x1, mod[1], lat_row, g2, w_out1, 1, *ffn, 2 * tm)
```

```python
import functools
import math

import numpy as np
import jax
import jax.numpy as jnp
from jax import lax
from jax.experimental import pallas as pl
from jax.experimental.pallas import tpu as pltpu

D_MODEL = 1024
BATCH = 4
SEQ = 4096
DEPTH = 2
GRID_W = 64
CTX_LEN = 256
HEAD_DIM = 64
EPS = 1e-6
FOURIER_WIDTH = D_MODEL // 2
FOURIER_GROUPS = 4
FOURIER_GROUP_CH = FOURIER_WIDTH // FOURIER_GROUPS
WIN_Q_HEADS = (D_MODEL // 2) // HEAD_DIM
WIN_KV_HEADS = 2
WIN_GROUP = WIN_Q_HEADS // WIN_KV_HEADS
WIN_RADIUS = 128
WIN_BLOCK = 128
QW = WIN_Q_HEADS * HEAD_DIM
KW = WIN_KV_HEADS * HEAD_DIM
EV_IN_WIDTH = FOURIER_WIDTH + QW + 2 * KW
NA_HEADS = D_MODEL // HEAD_DIM
NA_KH = 8
NA_KW = 16
NA_WIDTH = NA_HEADS * HEAD_DIM
ROPE_THETA = 10000.0
ROPE_FREQS = HEAD_DIM // 4
D_FF = ((8 * D_MODEL // 3 + 255) // 256) * 256
GRID_ROWS = SEQ // GRID_W

LANES = 128
MOD_ROWS = 8
CTX_MOD_ROW = BATCH
FFT_N = 64
FF_CHUNK = 256
SOFTMAX_ROWS = 64
RMS_W = 256
SUB_ROWS = 512
FFN_SUB_ROWS = 1024
VMEM_LIMIT = 60 * 1024 * 1024

LOG2E = math.log2(math.e)
QK_SCALE = LOG2E / math.sqrt(HEAD_DIM)

BF16 = jnp.bfloat16
F32 = jnp.float32

assert DEPTH == 2 and SEQ == FFT_N * FFT_N and D_FF % FF_CHUNK == 0


def _params(*sem):
    return pltpu.CompilerParams(dimension_semantics=sem, vmem_limit_bytes=VMEM_LIMIT)


def _dot(a, b):
    return jnp.dot(a, b, preferred_element_type=F32)


def _dot_nt(a, b):
    return lax.dot_general(a, b, (((1,), (1,)), ((), ())), preferred_element_type=F32)


def _silu(x):
    return x / (1.0 + jnp.exp(-x))


def _const_spec(shape):
    nd = len(shape)
    return pl.BlockSpec(shape, lambda *_: (0,) * nd, pipeline_mode=pl.Buffered(1))


def _dft_cos_sin(n):
    idx = (np.arange(n)[:, None] * np.arange(n)[None, :]) % n
    ang = 2.0 * np.pi * idx / n
    return np.cos(ang), np.sin(ang)


def _fourier_tables():
    cc, sc = _dft_cos_sin(FOURIER_GROUP_CH)
    wc = np.concatenate([cc, -sc], axis=1) / math.sqrt(FOURIER_GROUP_CH)
    c64, s64 = _dft_cos_sin(FFT_N)
    m1 = np.concatenate([c64, -s64], axis=0) / math.sqrt(FFT_N)
    m3 = np.concatenate([c64, s64], axis=1) / math.sqrt(FFT_N)
    tw = (np.arange(FFT_N)[:, None] * np.arange(FFT_N)[None, :]) % SEQ
    tw = 2.0 * np.pi * tw / SEQ
    twc = np.repeat(np.cos(tw)[:, :, None], LANES, axis=2)
    tws = np.repeat(np.sin(tw)[:, :, None], LANES, axis=2)
    cx, sx = _dft_cos_sin(CTX_LEN)
    mctx = np.concatenate([cx, sx], axis=1) / math.sqrt(CTX_LEN)
    as32 = lambda a: jnp.asarray(a, F32)
    return (as32(wc).astype(BF16), as32(m1).astype(BF16), as32(m3).astype(BF16),
            as32(twc), as32(tws), as32(mctx).astype(BF16))


def _head_mean_matrix():
    blk = np.kron(np.eye(RMS_W // HEAD_DIM), np.ones((HEAD_DIM, HEAD_DIM))) / HEAD_DIM
    return jnp.asarray(blk, BF16)


def _rope_tables():
    t = np.arange(SEQ)
    row = (t // GRID_W).astype(np.float32)
    col = (t % GRID_W).astype(np.float32)
    inv = np.float32(ROPE_THETA) ** (-np.arange(ROPE_FREQS, dtype=np.float32) / np.float32(ROPE_FREQS))
    ang_row = (row[:, None] * inv[None, :]).astype(np.float32).astype(np.float64)
    ang_col = (col[:, None] * inv[None, :]).astype(np.float32).astype(np.float64)
    zero = np.zeros_like(ang_row)
    cos = np.concatenate([np.cos(ang_row)] * 2 + [np.cos(ang_col)] * 2, axis=1)
    sin_hi = np.concatenate([-np.sin(ang_row), zero, -np.sin(ang_col), zero], axis=1)
    sin_lo = np.concatenate([zero, np.sin(ang_row), zero, np.sin(ang_col)], axis=1)
    rep = LANES // HEAD_DIM
    return tuple(jnp.asarray(np.tile(a, (1, rep)), F32) for a in (cos, sin_hi, sin_lo))


def _na_bias_index():
    cq = np.arange(GRID_W)
    dc = np.clip(cq[None, :] - cq[:, None] + NA_KW - 1, 0, 2 * NA_KW - 2)
    return jnp.asarray(np.concatenate([dc, dc + LANES // 2], axis=1), jnp.int32)


def _mod_kernel(cs_ref, w_ref, b_ref, o_ref):
    s = _silu(cs_ref[...]).astype(BF16)
    o_ref[0] = _dot(s, w_ref[0].astype(BF16)) + b_ref[0]


def _modulation(cs, ada_w, ada_b):
    tn = 1536
    return pl.pallas_call(
        _mod_kernel,
        grid=(DEPTH, 6 * D_MODEL // tn),
        in_specs=[pl.BlockSpec((MOD_ROWS, D_MODEL), lambda i, j: (0, 0)),
                  pl.BlockSpec((1, D_MODEL, tn), lambda i, j: (i, 0, j)),
                  pl.BlockSpec((1, 1, tn), lambda i, j: (i, 0, j))],
        out_specs=pl.BlockSpec((1, MOD_ROWS, tn), lambda i, j: (i, 0, j)),
        out_shape=jax.ShapeDtypeStruct((DEPTH, MOD_ROWS, 6 * D_MODEL), F32),
        compiler_params=_params("arbitrary", "arbitrary"),
        name="ada_modulation",
    )(cs, ada_w, ada_b.reshape(DEPTH, 1, 6 * D_MODEL))


def _mod_slice(mod_ref, k):
    return mod_ref[0, :, k * D_MODEL:(k + 1) * D_MODEL]


def _rms_mod(x, g, scale, shift):
    y = x * lax.rsqrt(jnp.mean(x * x, axis=-1, keepdims=True) + EPS)
    return (y * g) * (1.0 + scale) + shift


def _head_rms(t, gain_ref, hm_ref):
    w = t.shape[1]
    ms = _dot((t * t).astype(BF16), hm_ref[:w, :w])
    return t * lax.rsqrt(ms + EPS) * gain_ref[:, :w]


def _rope(t, cos, sin_hi, sin_lo):
    up = pltpu.roll(t, LANES - ROPE_FREQS, axis=1)
    dn = pltpu.roll(t, ROPE_FREQS, axis=1)
    return t * cos + up * sin_hi + dn * sin_lo


def _cast_once(w32_ref, w_ref):
    @pl.when((pl.program_id(0) == 0) & (pl.program_id(1) == 0))
    def _():
        w_ref[...] = w32_ref[...].astype(BF16)


def _in_even_kernel(latent, x_ref, mod_ref, g_ref, w32_ref, qg_ref, kg_ref, hm_ref, wc_ref, *rest):
    if latent:
        cos_ref, sh_ref, sl_ref, y_ref, q_ref, k_ref, v_ref, w_ref = rest
    else:
        y_ref, q_ref, k_ref, v_ref, w_ref = rest
    _cast_once(w32_ref, w_ref)
    F = FOURIER_WIDTH
    lo_lanes = lax.broadcasted_iota(jnp.int32, (1, LANES), 1) < HEAD_DIM

    for r in range(x_ref.shape[1] // SUB_ROWS):
        rows = slice(r * SUB_ROWS, (r + 1) * SUB_ROWS)
        h = _rms_mod(x_ref[0, rows], g_ref[...], _mod_slice(mod_ref, 1), _mod_slice(mod_ref, 0)).astype(BF16)
        f = _dot(h, w_ref[:, :F]).astype(BF16)
        for g in range(FOURIER_GROUPS):
            yg = _dot(f[:, g * LANES:(g + 1) * LANES], wc_ref[...])
            y_ref[0, rows, g * LANES:(g + 1) * LANES] = yg[:, :LANES].astype(y_ref.dtype)
            y_ref[0, rows, F + g * LANES:F + (g + 1) * LANES] = yg[:, LANES:].astype(y_ref.dtype)

        def qk(t, gain_ref):
            t = _head_rms(t, gain_ref, hm_ref)
            if latent:
                t = jnp.concatenate([_rope(t[:, i * LANES:(i + 1) * LANES], cos_ref[rows], sh_ref[rows], sl_ref[rows])
                                     for i in range(t.shape[1] // LANES)], axis=1)
            return t

        q = _dot(h, w_ref[:, F:F + QW])
        for j in range(QW // RMS_W):
            qj = qk(q[:, j * RMS_W:(j + 1) * RMS_W], qg_ref) * QK_SCALE
            q_ref[0, rows, j * RMS_W:(j + 1) * RMS_W] = qj.astype(BF16)
        kv = _dot(h, w_ref[:, F + QW:])

        def store_dup(t, o_ref):
            sw = pltpu.roll(t, HEAD_DIM, axis=1)
            o_ref[0, 0, rows] = jnp.where(lo_lanes, t, sw).astype(BF16)
            o_ref[0, 1, rows] = jnp.where(lo_lanes, sw, t).astype(BF16)

        store_dup(qk(kv[:, :KW], kg_ref), k_ref)
        store_dup(kv[:, KW:], v_ref)


def _in_even(x, mod, mod_row, g, w_in, q_g, k_g, hm, wc, rope_tabs, tm):
    B, L, _ = x.shape
    latent = rope_tabs is not None
    in_specs = [pl.BlockSpec((1, tm, D_MODEL), lambda b, t: (b, t, 0)),
                pl.BlockSpec((1, 1, 6 * D_MODEL), lambda b, t: (mod_row(b), 0, 0)),
                _const_spec((1, D_MODEL)),
                _const_spec((D_MODEL, EV_IN_WIDTH)),
                _const_spec((1, RMS_W)), _const_spec((1, RMS_W)),
                _const_spec((RMS_W, RMS_W)), _const_spec((LANES, 2 * LANES))]
    args = [x, mod, g, w_in, q_g, k_g, hm, wc]
    tok = lambda w: pl.BlockSpec((1, tm, w), lambda b, t: (b, t, 0))
    kv_spec = pl.BlockSpec((1, WIN_KV_HEADS, tm, LANES), lambda b, t: (b, 0, t, 0))
    if latent:
        in_specs += [pl.BlockSpec((tm, LANES), lambda b, t: (t, 0))] * 3
        args += list(rope_tabs)
    return pl.pallas_call(
        functools.partial(_in_even_kernel, latent),
        grid=(B, L // tm),
        in_specs=in_specs,
        out_specs=[tok(2 * FOURIER_WIDTH), tok(QW), kv_spec, kv_spec],
        out_shape=[jax.ShapeDtypeStruct((B, L, 2 * FOURIER_WIDTH), F32 if latent else BF16),
                   jax.ShapeDtypeStruct((B, L, QW), BF16),
                   jax.ShapeDtypeStruct((B, WIN_KV_HEADS, L, LANES), BF16),
                   jax.ShapeDtypeStruct((B, WIN_KV_HEADS, L, LANES), BF16)],
        compiler_params=_params("arbitrary", "arbitrary"),
        scratch_shapes=[pltpu.VMEM((D_MODEL, EV_IN_WIDTH), BF16)],
        name="in_even_latent" if latent else "in_even_ctx",
    )(*args)


def _fft_stage1_kernel(n2t, y_ref, m1_ref, twc_ref, tws_ref, o_ref, row_ref):
    F = FOURIER_WIDTH
    for j in range(n2t):
        row_ref[j % 2] = y_ref[0, :, j, :]
        a = _dot(m1_ref[...], row_ref[j % 2].astype(BF16))
        top, bot = a[:FFT_N], a[FFT_N:]
        ar = top[:, :F] - bot[:, F:]
        ai = top[:, F:] + bot[:, :F]
        tc = jnp.tile(twc_ref[j], (1, F // LANES))
        ts = jnp.tile(tws_ref[j], (1, F // LANES))
        o_ref[0, 0, :, j * F:(j + 1) * F] = (ar * tc + ai * ts).astype(BF16)
        o_ref[0, 1, :, j * F:(j + 1) * F] = (ai * tc - ar * ts).astype(BF16)


def _fft_stage2_kernel(k1t, b_ref, m3_ref, o_ref):
    for j in range(k1t):
        rhs = jnp.concatenate([b_ref[0, 0, j], b_ref[0, 1, j]], axis=0)
        o_ref[0, j] = _dot(m3_ref[...], rhs)


def _fourier_latent(y, m1, m3, twc, tws):
    B = y.shape[0]
    F = FOURIER_WIDTH
    n2t = 16
    yv = y.reshape(B, FFT_N, FFT_N, 2 * F)
    bh = pl.pallas_call(
        functools.partial(_fft_stage1_kernel, n2t),
        grid=(B, FFT_N // n2t),
        in_specs=[pl.BlockSpec((1, FFT_N, n2t, 2 * F), lambda b, t: (b, 0, t, 0)),
                  _const_spec((2 * FFT_N, FFT_N)),
                  pl.BlockSpec((n2t, FFT_N, LANES), lambda b, t: (t, 0, 0)),
                  pl.BlockSpec((n2t, FFT_N, LANES), lambda b, t: (t, 0, 0))],
        out_specs=pl.BlockSpec((1, 2, FFT_N, n2t * F), lambda b, t: (b, 0, 0, t)),
        out_shape=jax.ShapeDtypeStruct((B, 2, FFT_N, FFT_N * F), BF16),
        scratch_shapes=[pltpu.VMEM((2, FFT_N, 2 * F), F32)],
        compiler_params=_params("arbitrary", "arbitrary"),
        name="fft_stage1",
    )(yv, m1, twc, tws)
    k1t = 16
    bv = bh.reshape(B, 2, FFT_N, FFT_N, F)
    return pl.pallas_call(
        functools.partial(_fft_stage2_kernel, k1t),
        grid=(B, FFT_N // k1t),
        in_specs=[pl.BlockSpec((1, 2, k1t, FFT_N, F), lambda b, t: (b, 0, t, 0, 0)),
                  _const_spec((FFT_N, 2 * FFT_N))],
        out_specs=pl.BlockSpec((1, k1t, FFT_N, F), lambda b, t: (b, t, 0, 0)),
        out_shape=jax.ShapeDtypeStruct((B, FFT_N, FFT_N, F), F32),
        compiler_params=_params("arbitrary", "arbitrary"),
        name="fft_stage2",
    )(bv, m3)


def _fourier_ctx_kernel(y_ref, m_ref, o_ref):
    F = FOURIER_WIDTH
    rhs = jnp.concatenate([y_ref[0, :, :F], y_ref[0, :, F:]], axis=0)
    o_ref[0] = _dot(m_ref[...], rhs)


def _fourier_ctx(y, mctx):
    B = y.shape[0]
    return pl.pallas_call(
        _fourier_ctx_kernel,
        grid=(B,),
        in_specs=[pl.BlockSpec((1, CTX_LEN, 2 * FOURIER_WIDTH), lambda b: (b, 0, 0)),
                  _const_spec((CTX_LEN, 2 * CTX_LEN))],
        out_specs=pl.BlockSpec((1, CTX_LEN, FOURIER_WIDTH), lambda b: (b, 0, 0)),
        out_shape=jax.ShapeDtypeStruct((B, CTX_LEN, FOURIER_WIDTH), F32),
        compiler_params=_params("arbitrary"),
        name="fourier_ctx",
    )(y, mctx)


def _gqa_scores(q_slabs, keys, rows, s_ref):
    lo_lanes = lax.broadcasted_iota(jnp.int32, (rows, LANES), 1) < HEAD_DIM
    zero = jnp.zeros((rows, LANES), BF16)
    qs = jnp.concatenate([jnp.where(lo_lanes if half == 0 else ~lo_lanes, qslab, zero)
                          for qslab in q_slabs for half in range(2)], axis=0)
    s_ref[...] = _dot_nt(qs, keys)


def _gqa_attend(vals, masks, sinks, rows, out_slab, first_slab, scratch):
    s_ref, p_ref, inv_ref = scratch
    lo_lanes = lax.broadcasted_iota(jnp.int32, (rows, LANES), 1) < HEAD_DIM
    for c in range(WIN_GROUP * rows // SOFTMAX_ROWS):
        rs = slice(c * SOFTMAX_ROWS, (c + 1) * SOFTMAX_ROWS)
        s = s_ref[rs, :]
        if masks is not None:
            band, w = masks
            r0 = (c * SOFTMAX_ROWS) % rows
            s = jnp.concatenate([s[:, :w] + band[r0:r0 + SOFTMAX_ROWS], s[:, w:]], axis=1)
        sk = sinks[(c * SOFTMAX_ROWS) // rows]
        m = jnp.maximum(jnp.max(s, axis=-1, keepdims=True), sk)
        e = jnp.exp2(s - m)
        inv_ref[rs, :] = 1.0 / (jnp.sum(e, axis=-1, keepdims=True) + jnp.exp2(sk - m))
        p_ref[rs, :] = e.astype(BF16)
    o = _dot(p_ref[...], vals) * inv_ref[...]
    for i in range(WIN_GROUP // 2):
        a = o[(2 * i) * rows:(2 * i + 1) * rows]
        b = o[(2 * i + 1) * rows:(2 * i + 2) * rows]
        sl = slice((first_slab + i) * LANES, (first_slab + i + 1) * LANES)
        out_slab(sl)[...] = jnp.where(lo_lanes, a, b).astype(BF16)


def _gqa_scratch(rows, n_keys):
    one = [pltpu.VMEM((WIN_GROUP * rows, n_keys), F32), pltpu.VMEM((WIN_GROUP * rows, n_keys), BF16),
           pltpu.VMEM((WIN_GROUP * rows, 1), F32)]
    return one * WIN_KV_HEADS


def _group_sinks(sink_ref, kv):
    return [sink_ref[kv * WIN_GROUP + g] * LOG2E for g in range(WIN_GROUP)]


def _group_slab(kv, i):
    return slice((kv * (WIN_GROUP // 2) + i) * LANES, (kv * (WIN_GROUP // 2) + i + 1) * LANES)


def _window_bands():
    i = np.arange(WIN_BLOCK)[:, None]
    j = np.arange(3 * WIN_BLOCK)[None, :]
    bands = [np.where(np.abs(j - off - i) <= WIN_RADIUS, 0.0, -np.inf) for off in (0, WIN_BLOCK, 2 * WIN_BLOCK)]
    return jnp.asarray(np.stack(bands).astype(np.float32))


def _win_attn_kernel(blocks_per_step, sink_ref, band_ref, q_ref, k_ref, v_ref, kx_ref, vx_ref, o_ref,
                     s0_ref, s1_ref, p_ref, inv_ref):
    t = pl.program_id(1)
    nb = SEQ // WIN_BLOCK
    nw = 3 * WIN_BLOCK

    def place(j):
        n = t * blocks_per_step + j
        start = pl.multiple_of(jnp.clip((n - 1) * WIN_BLOCK, 0, SEQ - nw), WIN_BLOCK)
        sel = jnp.where(n == 0, 0, jnp.where(n == nb - 1, 2, 1))
        return pl.multiple_of(j * WIN_BLOCK, WIN_BLOCK), start, sel

    def scores(j, kv, s_ref):
        qoff, start, _ = place(j)
        keys = jnp.concatenate([k_ref[0, kv, pl.ds(start, nw), :], kx_ref[0, kv]], axis=0)
        q_slabs = [q_ref[0, pl.ds(qoff, WIN_BLOCK), _group_slab(kv, i)] for i in range(WIN_GROUP // 2)]
        _gqa_scores(q_slabs, keys, WIN_BLOCK, s_ref)

    def attend(j, kv, s_ref):
        qoff, start, sel = place(j)
        vals = jnp.concatenate([v_ref[0, kv, pl.ds(start, nw), :], vx_ref[0, kv]], axis=0)
        _gqa_attend(vals, (band_ref[sel], nw), _group_sinks(sink_ref, kv), WIN_BLOCK,
                    lambda sl: o_ref.at[0, pl.ds(qoff, WIN_BLOCK), sl], kv * (WIN_GROUP // 2),
                    (s_ref, p_ref, inv_ref))

    scores(0, 0, s0_ref)

    def one_block(j, carry):
        scores(j, 1, s1_ref)
        attend(j, 0, s0_ref)
        scores(jnp.minimum(j + 1, blocks_per_step - 1), 0, s0_ref)
        attend(j, 1, s1_ref)
        return carry

    lax.fori_loop(0, blocks_per_step, one_block, 0)


def _win_attention(q, k, v, kx, vx, sink):
    B, L, _ = q.shape
    blocks_per_step = 16
    rows = blocks_per_step * WIN_BLOCK
    n_keys = 3 * WIN_BLOCK + CTX_LEN
    full = pl.BlockSpec((1, WIN_KV_HEADS, L, LANES), lambda b, t: (b, 0, 0, 0))
    ctx_spec = pl.BlockSpec((1, WIN_KV_HEADS, CTX_LEN, LANES), lambda b, t: (0, 0, b, 0))
    tok = pl.BlockSpec((1, rows, QW), lambda b, t: (b, t, 0))
    return pl.pallas_call(
        functools.partial(_win_attn_kernel, blocks_per_step),
        grid=(B, L // rows),
        in_specs=[pl.BlockSpec(memory_space=pltpu.SMEM),
                  _const_spec((3, WIN_BLOCK, 3 * WIN_BLOCK)),
                  tok, full, full, ctx_spec, ctx_spec],
        out_specs=tok,
        out_shape=jax.ShapeDtypeStruct((B, L, QW), BF16),
        scratch_shapes=[pltpu.VMEM((WIN_GROUP * WIN_BLOCK, n_keys), F32), pltpu.VMEM((WIN_GROUP * WIN_BLOCK, n_keys), F32),
                        pltpu.VMEM((WIN_GROUP * WIN_BLOCK, n_keys), BF16), pltpu.VMEM((WIN_GROUP * WIN_BLOCK, 1), F32)],
        compiler_params=_params("arbitrary", "arbitrary"),
        name="window_attention",
    )(sink, _window_bands(), q, k, v, kx, vx)


def _ctx_attn_even_kernel(sink_ref, q_ref, k_ref, v_ref, o_ref, *scratch):
    for b in range(q_ref.shape[0]):
        tokens = slice(b * CTX_LEN, (b + 1) * CTX_LEN)
        for kv in range(WIN_KV_HEADS):
            unit = scratch[3 * (b * WIN_KV_HEADS + kv):3 * (b * WIN_KV_HEADS + kv) + 3]
            q_slabs = [q_ref[b, :, _group_slab(kv, i)] for i in range(WIN_GROUP // 2)]
            _gqa_scores(q_slabs, k_ref[0, kv, tokens, :], CTX_LEN, unit[0])
            _gqa_attend(v_ref[0, kv, tokens, :], None, _group_sinks(sink_ref, kv), CTX_LEN,
                        lambda sl, b=b: o_ref.at[b, :, sl], kv * (WIN_GROUP // 2), unit)


def _ctx_attention_even(q, k, v, sink):
    B = q.shape[0]
    whole = lambda a: pl.BlockSpec(a.shape, lambda i: (0,) * a.ndim)
    return pl.pallas_call(
        _ctx_attn_even_kernel,
        grid=(1,),
        in_specs=[pl.BlockSpec(memory_space=pltpu.SMEM), whole(q), whole(k), whole(v)],
        out_specs=whole(q),
        out_shape=jax.ShapeDtypeStruct((B, CTX_LEN, QW), BF16),
        scratch_shapes=_gqa_scratch(CTX_LEN, CTX_LEN) * B,
        compiler_params=_params("arbitrary"),
        name="ctx_attention_even",
    )(sink, q, k, v)


def _out_ffn_kernel(mode, *refs):
    if mode == "odd":
        a_ref, x_ref, mod_ref, g_ref, wo_ref, wg_ref, wu_ref, wd_ref, o_ref = refs
    else:
        f_ref, a_ref, x_ref, mod_ref, g_ref, wo_ref, wg_ref, wu_ref, wd_ref, o_ref = refs
    for r in range(x_ref.shape[1] // FFN_SUB_ROWS):
        rows = slice(r * FFN_SUB_ROWS, (r + 1) * FFN_SUB_ROWS)
        if mode == "even_latent":
            planes = range(r * FFN_SUB_ROWS // FFT_N, (r + 1) * FFN_SUB_ROWS // FFT_N)
            fm = jnp.concatenate([f_ref[0, :, j, :] for j in planes], axis=0).astype(BF16)
            o = _dot(fm, wo_ref[:FOURIER_WIDTH]) + _dot(a_ref[0, rows], wo_ref[FOURIER_WIDTH:])
        else:
            o = _dot(jnp.concatenate([a_ref[0, j, rows] for j in range(a_ref.shape[1])], axis=1), wo_ref[...])
        x1 = x_ref[0, rows] + _mod_slice(mod_ref, 2) * o
        h = _rms_mod(x1, g_ref[...], _mod_slice(mod_ref, 4), _mod_slice(mod_ref, 3)).astype(BF16)
        acc = jnp.zeros_like(x1)
        for c in range(D_FF // FF_CHUNK):
            cs = slice(c * FF_CHUNK, (c + 1) * FF_CHUNK)
            a = _silu(_dot(h, wg_ref[0, :, cs])) * _dot(h, wu_ref[0, :, cs])
            acc = acc + _dot(a.astype(BF16), wd_ref[0, cs, :])
        o_ref[0, rows] = x1 + _mod_slice(mod_ref, 5) * acc


def _out_ffn(mode, mix, x, mod, mod_row, g, w_out, layer, wg, wu, wd, tm):
    B, L, _ = x.shape
    tok = lambda w: pl.BlockSpec((1, tm, w), lambda b, t: (b, t, 0))
    ffn_spec = lambda r, c: pl.BlockSpec((1, r, c), lambda b, t: (layer, 0, 0), pipeline_mode=pl.Buffered(1))
    assert mode in ("even_latent", "odd") and tm % FFN_SUB_ROWS == 0
    if mode == "even_latent":
        mix_specs = [pl.BlockSpec((1, FFT_N, tm // FFT_N, FOURIER_WIDTH), lambda b, t: (b, 0, t, 0)), tok(QW)]
    else:
        mix_specs = [pl.BlockSpec((1, NA_WIDTH // LANES, tm, LANES), lambda b, t: (b, 0, t, 0))]
    return pl.pallas_call(
        functools.partial(_out_ffn_kernel, mode),
        grid=(B, L // tm),
        in_specs=mix_specs + [tok(D_MODEL),
                              pl.BlockSpec((1, 1, 6 * D_MODEL), lambda b, t: (mod_row(b), 0, 0)),
                              _const_spec((1, D_MODEL)),
                              _const_spec((D_MODEL, D_MODEL)),
                              ffn_spec(D_MODEL, D_FF), ffn_spec(D_MODEL, D_FF), ffn_spec(D_FF, D_MODEL)],
        out_specs=tok(D_MODEL),
        out_shape=jax.ShapeDtypeStruct((B, L, D_MODEL), F32),
        compiler_params=_params("arbitrary", "arbitrary"),
        name="out_ffn_" + mode,
    )(*mix, x, mod, g, w_out, wg, wu, wd)


def _ctx_ffn_kernel(f_ref, a_ref, x_ref, mod_ref, g_ref, wo_ref, wg32_ref, wu32_ref, wd32_ref,
                    o_ref, wg_ref, wu_ref, wd_ref, x1_ref, h_ref, acc_ref):
    c = pl.program_id(0)
    wg_ref[...] = wg32_ref[...].astype(BF16)
    wu_ref[...] = wu32_ref[...].astype(BF16)
    wd_ref[...] = wd32_ref[...].astype(BF16)

    @pl.when(c == 0)
    def _():
        o = _dot(f_ref[0].astype(BF16), wo_ref[:FOURIER_WIDTH]) + _dot(a_ref[0], wo_ref[FOURIER_WIDTH:])
        x1 = x_ref[0] + _mod_slice(mod_ref, 2) * o
        x1_ref[...] = x1
        h_ref[...] = _rms_mod(x1, g_ref[...], _mod_slice(mod_ref, 4), _mod_slice(mod_ref, 3)).astype(BF16)
        acc_ref[...] = jnp.zeros_like(acc_ref)

    h = h_ref[...]
    a = _silu(_dot(h, wg_ref[0])) * _dot(h, wu_ref[0])
    acc_ref[...] += _dot(a.astype(BF16), wd_ref[0])

    @pl.when(c == pl.num_programs(0) - 1)
    def _():
        o_ref[0] = x1_ref[...] + _mod_slice(mod_ref, 5) * acc_ref[...]


def _ctx_ffn_and_weights(f, a, x, mod, mod_row, g, w_out, wg32, wu32, wd32):
    _, n, _ = x.shape
    nc = D_FF // FF_CHUNK
    whole = lambda w: pl.BlockSpec((1, n, w), lambda c: (0, 0, 0))
    col_chunk = pl.BlockSpec((DEPTH, D_MODEL, FF_CHUNK), lambda c: (0, 0, c))
    row_chunk = pl.BlockSpec((DEPTH, FF_CHUNK, D_MODEL), lambda c: (0, c, 0))
    return pl.pallas_call(
        _ctx_ffn_kernel,
        grid=(nc,),
        in_specs=[whole(FOURIER_WIDTH), whole(QW), whole(D_MODEL),
                  pl.BlockSpec((1, 1, 6 * D_MODEL), lambda c: (mod_row(0), 0, 0)),
                  pl.BlockSpec((1, D_MODEL), lambda c: (0, 0)),
                  pl.BlockSpec((D_MODEL, D_MODEL), lambda c: (0, 0)),
                  col_chunk, col_chunk, row_chunk],
        out_specs=[whole(D_MODEL), col_chunk, col_chunk, row_chunk],
        out_shape=[jax.ShapeDtypeStruct((1, n, D_MODEL), F32),
                   jax.ShapeDtypeStruct((DEPTH, D_MODEL, D_FF), BF16),
                   jax.ShapeDtypeStruct((DEPTH, D_MODEL, D_FF), BF16),
                   jax.ShapeDtypeStruct((DEPTH, D_FF, D_MODEL), BF16)],
        scratch_shapes=[pltpu.VMEM((n, D_MODEL), F32), pltpu.VMEM((n, D_MODEL), BF16), pltpu.VMEM((n, D_MODEL), F32)],
        compiler_params=_params("arbitrary"),
        name="ctx_ffn_and_weights",
    )(f, a, x, mod, g, w_out, wg32, wu32, wd32)


def _in_odd_kernel(with_q, x_ref, mod_ref, g_ref, w32_ref, qg_ref, kg_ref, hm_ref, *outs):
    W = NA_WIDTH
    slabs = RMS_W // LANES
    if with_q:
        q_ref, k_ref, v_ref, w_ref = outs
    else:
        k_ref, v_ref, w_ref = outs
    _cast_once(w32_ref, w_ref)

    for r in range(x_ref.shape[1] // SUB_ROWS):
        rows = slice(r * SUB_ROWS, (r + 1) * SUB_ROWS)
        h = _rms_mod(x_ref[0, rows], g_ref[...], _mod_slice(mod_ref, 1), _mod_slice(mod_ref, 0)).astype(BF16)

        def normed(t, gain_ref, scale, o_ref):
            for j in range(W // RMS_W):
                tj = (_head_rms(t[:, j * RMS_W:(j + 1) * RMS_W], gain_ref, hm_ref) * scale).astype(BF16)
                for i in range(slabs):
                    o_ref[0, j * slabs + i, rows] = tj[:, i * LANES:(i + 1) * LANES]

        if with_q:
            normed(_dot(h, w_ref[:, :W]), qg_ref, QK_SCALE, q_ref)
        normed(_dot(h, w_ref[:, W:2 * W]), kg_ref, 1.0, k_ref)
        v = _dot(h, w_ref[:, 2 * W:]).astype(BF16)
        for j in range(W // LANES):
            v_ref[0, j, rows] = v[:, j * LANES:(j + 1) * LANES]


def _in_odd(x, mod, mod_row, g, w_in, q_g, k_g, hm, with_q, tm):
    B, L, _ = x.shape
    pairs = NA_WIDTH // LANES
    tok = pl.BlockSpec((1, pairs, tm, LANES), lambda b, t: (b, 0, t, 0))
    n_out = 3 if with_q else 2
    return pl.pallas_call(
        functools.partial(_in_odd_kernel, with_q),
        grid=(B, L // tm),
        in_specs=[pl.BlockSpec((1, tm, D_MODEL), lambda b, t: (b, t, 0)),
                  pl.BlockSpec((1, 1, 6 * D_MODEL), lambda b, t: (mod_row(b), 0, 0)),
                  _const_spec((1, D_MODEL)),
                  _const_spec((D_MODEL, 3 * NA_WIDTH)),
                  _const_spec((1, RMS_W)), _const_spec((1, RMS_W)), _const_spec((RMS_W, RMS_W))],
        out_specs=[tok] * n_out,
        out_shape=[jax.ShapeDtypeStruct((B, pairs, L, LANES), BF16)] * n_out,
        compiler_params=_params("arbitrary", "arbitrary"),
        scratch_shapes=[pltpu.VMEM((D_MODEL, 3 * NA_WIDTH), BF16)],
        name="in_odd" if with_q else "in_odd_ctx",
    )(x, mod, g, w_in, q_g, k_g, hm)


def _bias_table_kernel(rb_ref, idx_ref, o_ref):
    idx = idx_ref[...]
    cq = lax.broadcasted_iota(jnp.int32, idx.shape, 0)
    ck = lax.broadcasted_iota(jnp.int32, idx.shape, 1) % GRID_W
    c0 = jnp.clip(cq - NA_KW // 2, 0, GRID_W - NA_KW)
    inside = (ck >= c0) & (ck < c0 + NA_KW)
    for h in range(o_ref.shape[0]):
        for dr in range(o_ref.shape[1]):
            row = jnp.broadcast_to(rb_ref[h, dr:dr + 1, :], idx.shape)
            o_ref[h, dr] = jnp.where(inside, jnp.take_along_axis(row, idx, axis=1) * LOG2E, -jnp.inf)


def _na_bias_table(rel_bias, idx):
    H = NA_HEADS
    npair = 2 * NA_KH - 2
    half = LANES // 2
    hb = 8
    pad = lambda a: jnp.pad(a, ((0, 0), (0, 0), (0, half - a.shape[-1])))
    rb2 = jnp.concatenate([pad(rel_bias[:, :-1]), pad(rel_bias[:, 1:])], axis=-1)
    return pl.pallas_call(
        _bias_table_kernel,
        grid=(H // hb,),
        in_specs=[pl.BlockSpec((hb, npair, LANES), lambda h: (h, 0, 0)), _const_spec((GRID_W, LANES))],
        out_specs=pl.BlockSpec((hb, npair, GRID_W, LANES), lambda h: (h, 0, 0, 0)),
        out_shape=jax.ShapeDtypeStruct((H, npair, GRID_W, LANES), F32),
        compiler_params=_params("arbitrary"),
        name="na_bias_table",
    )(rb2, idx)


def _na_kernel(rows_per_step, q_ref, qn_ref, k_ref, v_ref, kx_ref, vx_ref, bias_ref, o_ref,
               s0_ref, s1_ref, p_ref, inv_ref):
    t = pl.program_id(1)
    n = NA_KH * GRID_W
    lo_lanes = lax.broadcasted_iota(jnp.int32, (GRID_W, LANES), 1) < HEAD_DIM

    def window(i):
        r = jnp.minimum(t * rows_per_step + i, GRID_ROWS - 1)
        r0 = jnp.clip(r - NA_KH // 2, 0, GRID_ROWS - NA_KH)
        qoff = jnp.minimum(i, rows_per_step - 1) * GRID_W
        return r0 - r + NA_KH - 1, pl.multiple_of(r0 * GRID_W, GRID_W), pl.multiple_of(qoff, GRID_W)

    def scores(i, s_ref):
        dr0, start, qoff = window(i)
        for hp in range(NA_HEADS // 2):
            qs = jnp.where(i < rows_per_step, q_ref[0, hp, pl.ds(qoff, GRID_W), :], qn_ref[0, hp])
            keys = jnp.concatenate([k_ref[0, hp, pl.ds(start, n), :], kx_ref[0, hp]], axis=0)
            qm = jnp.concatenate([jnp.where(lo_lanes, qs, jnp.zeros_like(qs)),
                                  jnp.where(lo_lanes, jnp.zeros_like(qs), qs)], axis=0)
            bias = jnp.concatenate(
                [jnp.concatenate([bias_ref[2 * hp + half, dr0 + 2 * p] for p in range(NA_KH // 2)], axis=1)
                 for half in range(2)], axis=0)
            rs = slice(2 * hp * GRID_W, (2 * hp + 2) * GRID_W)
            s_ref[rs, :n] = _dot_nt(qm, keys[:n]) + bias
            s_ref[rs, n:] = _dot_nt(qm, keys[n:])

    def attend(i, s_ref):
        _, start, qoff = window(i)
        for c in range(NA_HEADS * GRID_W // SOFTMAX_ROWS):
            cs = slice(c * SOFTMAX_ROWS, (c + 1) * SOFTMAX_ROWS)
            s = s_ref[cs, :]
            e = jnp.exp2(s - jnp.max(s, axis=-1, keepdims=True))
            inv_ref[cs, :] = 1.0 / jnp.sum(e, axis=-1, keepdims=True)
            p_ref[cs, :] = e.astype(BF16)
        for hp in range(NA_HEADS // 2):
            vals = jnp.concatenate([v_ref[0, hp, pl.ds(start, n), :], vx_ref[0, hp]], axis=0)
            rs = slice(2 * hp * GRID_W, (2 * hp + 2) * GRID_W)
            res = _dot(p_ref[rs, :], vals) * inv_ref[rs, :]
            o_ref[0, hp, pl.ds(qoff, GRID_W), :] = jnp.where(lo_lanes, res[:GRID_W], res[GRID_W:]).astype(BF16)

    @pl.when(t == 0)
    def _():
        scores(0, s0_ref)

    def two_rows(j, carry):
        scores(2 * j + 1, s1_ref)
        attend(2 * j, s0_ref)
        scores(2 * j + 2, s0_ref)
        attend(2 * j + 1, s1_ref)
        return carry

    lax.fori_loop(0, rows_per_step // 2, two_rows, 0)


def _neighbourhood_attention(q, k, v, kx, vx, bias_tab):
    B, pairs, L, _ = q.shape
    rows_per_step = 16
    full = pl.BlockSpec((1, pairs, L, LANES), lambda b, t: (b, 0, 0, 0))
    ctx_spec = pl.BlockSpec((1, pairs, CTX_LEN, LANES), lambda b, t: (0, 0, b, 0))
    q_rows = pl.BlockSpec((1, pairs, rows_per_step * GRID_W, LANES), lambda b, t: (b, 0, t, 0))
    q_next = pl.BlockSpec((1, pairs, GRID_W, LANES),
                          lambda b, t: (b, 0, jnp.minimum((t + 1) * rows_per_step, GRID_ROWS - 1), 0))
    n_keys = NA_KH * GRID_W + CTX_LEN
    all_rows = NA_HEADS * GRID_W
    return pl.pallas_call(
        functools.partial(_na_kernel, rows_per_step),
        grid=(B, GRID_ROWS // rows_per_step),
        in_specs=[q_rows, q_next, full, full, ctx_spec, ctx_spec, _const_spec(bias_tab.shape)],
        out_specs=q_rows,
        out_shape=jax.ShapeDtypeStruct((B, pairs, L, LANES), BF16),
        scratch_shapes=[pltpu.VMEM((all_rows, n_keys), F32), pltpu.VMEM((all_rows, n_keys), F32),
                        pltpu.VMEM((all_rows, n_keys), BF16), pltpu.VMEM((all_rows, 1), F32)],
        compiler_params=_params("arbitrary", "arbitrary"),
        name="neighbourhood_attention",
    )(q, q, k, v, kx, vx, bias_tab)


def kernel(x, c, ctx, c_ctx, ada_w, ada_b, norm1_g, norm2_g, ffn_w_gate, ffn_w_up, ffn_w_down,
           ev_w_in, ev_w_out, ev_q_norm, ev_k_norm, ev_sink,
           od_w_in, od_w_out, od_q_norm, od_k_norm, od_rel_bias):
    assert x.shape == (BATCH, SEQ, D_MODEL) and ctx.shape == (BATCH, CTX_LEN, D_MODEL)
    wc, m1, m3, twc, tws, mctx = _fourier_tables()
    hm = _head_mean_matrix()
    rope_tabs = _rope_tables()
    lane_gain = lambda gvec: jnp.tile(gvec, RMS_W // HEAD_DIM).reshape(1, RMS_W)
    lat_row = lambda b: b
    ctx_row = lambda b: CTX_MOD_ROW
    tm = 512
    n_ctx = BATCH * CTX_LEN
    as_seq = lambda a: a.reshape(1, n_ctx, a.shape[-1])
    per_batch = lambda a: a.reshape(BATCH, CTX_LEN, a.shape[-1])

    cs = jnp.concatenate([c, c_ctx[None, :], jnp.zeros((MOD_ROWS - BATCH - 1, D_MODEL), F32)], axis=0)
    mod = _modulation(cs, ada_w, ada_b).reshape(DEPTH, MOD_ROWS, 1, 6 * D_MODEL)

    w_in0 = ev_w_in[0]
    w_out0 = ev_w_out[0].astype(BF16)
    g1 = norm1_g[0].reshape(1, D_MODEL)
    g2 = norm2_g[0].reshape(1, D_MODEL)
    qg, kg = lane_gain(ev_q_norm[0]), lane_gain(ev_k_norm[0])
    y_l, q_l, k_l, v_l = _in_even(x, mod[0], lat_row, g1, w_in0, qg, kg, hm, wc, rope_tabs, 2 * tm)
    y_c, q_c, k_c, v_c = _in_even(as_seq(ctx), mod[0], ctx_row, g1, w_in0, qg, kg, hm, wc, None, tm)
    y_c, q_c = per_batch(y_c), per_batch(q_c)
    f_l = _fourier_latent(y_l, m1, m3, twc, tws)
    f_c = _fourier_ctx(y_c, mctx)
    a_l = _win_attention(q_l, k_l, v_l, k_c, v_c, ev_sink[0])
    a_c = _ctx_attention_even(q_c, k_c, v_c, ev_sink[0])
    y1, *ffn = _ctx_ffn_and_weights(as_seq(f_c), as_seq(a_c), as_seq(ctx), mod[0], ctx_row, g2, w_out0,
                                    ffn_w_gate, ffn_w_up, ffn_w_down)
    x1 = _out_ffn("even_latent", (f_l, a_l), x, mod[0], lat_row, g2, w_out0, 0, *ffn, 2 * tm)

    w_in1 = od_w_in[0]
    w_out1 = od_w_out[0].astype(BF16)
    g1 = norm1_g[1].reshape(1, D_MODEL)
    g2 = norm2_g[1].reshape(1, D_MODEL)
    qg, kg = lane_gain(od_q_norm[0]), lane_gain(od_k_norm[0])
    q_l, k_l, v_l = _in_odd(x1, mod[1], lat_row, g1, w_in1, qg, kg, hm, True, 2 * tm)
    k_c, v_c = _in_odd(y1, mod[1], ctx_row, g1, w_in1, qg, kg, hm, False, n_ctx)
    bias_tab = _na_bias_table(od_rel_bias[0], _na_bias_index())
    a_l = _neighbourhood_attention(q_l, k_l, v_l, k_c, v_c, bias_tab)
    return _out_ffn("odd", (a_l,), x1, mod[1], lat_row, g2, w_out1, 1, *ffn, 2 * tm)
```

```python
import functools
import math

import numpy as np
import jax
import jax.numpy as jnp
from jax import lax
from jax.experimental import pallas as pl
from jax.experimental.pallas import tpu as pltpu

D_MODEL = 1024
BATCH = 4
SEQ = 4096
DEPTH = 2
GRID_W = 64
CTX_LEN = 256
HEAD_DIM = 64
EPS = 1e-6
FOURIER_WIDTH = D_MODEL // 2
FOURIER_GROUPS = 4
FOURIER_GROUP_CH = FOURIER_WIDTH // FOURIER_GROUPS
WIN_Q_HEADS = (D_MODEL // 2) // HEAD_DIM
WIN_KV_HEADS = 2
WIN_GROUP = WIN_Q_HEADS // WIN_KV_HEADS
WIN_RADIUS = 128
WIN_BLOCK = 128
QW = WIN_Q_HEADS * HEAD_DIM
KW = WIN_KV_HEADS * HEAD_DIM
EV_IN_WIDTH = FOURIER_WIDTH + QW + 2 * KW
NA_HEADS = D_MODEL // HEAD_DIM
NA_KH = 8
NA_KW = 16
NA_WIDTH = NA_HEADS * HEAD_DIM
ROPE_THETA = 10000.0
ROPE_FREQS = HEAD_DIM // 4
D_FF = ((8 * D_MODEL // 3 + 255) // 256) * 256
GRID_ROWS = SEQ // GRID_W

LANES = 128
MOD_ROWS = 8
CTX_MOD_ROW = BATCH
FFT_N = 64
FF_CHUNK = 256
SOFTMAX_ROWS = 64
RMS_W = 256
SUB_ROWS = 512
FFN_SUB_ROWS = 1024
VMEM_LIMIT = 60 * 1024 * 1024

LOG2E = math.log2(math.e)
QK_SCALE = LOG2E / math.sqrt(HEAD_DIM)

BF16 = jnp.bfloat16
F32 = jnp.float32

assert DEPTH == 2 and SEQ == FFT_N * FFT_N and D_FF % FF_CHUNK == 0


def _params(*sem):
    return pltpu.CompilerParams(dimension_semantics=sem, vmem_limit_bytes=VMEM_LIMIT)


def _dot(a, b):
    return jnp.dot(a, b, preferred_element_type=F32)


def _dot_nt(a, b):
    return lax.dot_general(a, b, (((1,), (1,)), ((), ())), preferred_element_type=F32)


def _silu(x):
    return x / (1.0 + jnp.exp(-x))


def _const_spec(shape):
    nd = len(shape)
    return pl.BlockSpec(shape, lambda *_: (0,) * nd, pipeline_mode=pl.Buffered(1))


def _dft_cos_sin(n):
    idx = (np.arange(n)[:, None] * np.arange(n)[None, :]) % n
    ang = 2.0 * np.pi * idx / n
    return np.cos(ang), np.sin(ang)


def _fourier_tables():
    cc, sc = _dft_cos_sin(FOURIER_GROUP_CH)
    wc = np.concatenate([cc, -sc], axis=1) / math.sqrt(FOURIER_GROUP_CH)
    c64, s64 = _dft_cos_sin(FFT_N)
    m1 = np.concatenate([c64, -s64], axis=0) / math.sqrt(FFT_N)
    m3 = np.concatenate([c64, s64], axis=1) / math.sqrt(FFT_N)
    tw = (np.arange(FFT_N)[:, None] * np.arange(FFT_N)[None, :]) % SEQ
    tw = 2.0 * np.pi * tw / SEQ
    twc = np.repeat(np.cos(tw)[:, :, None], LANES, axis=2)
    tws = np.repeat(np.sin(tw)[:, :, None], LANES, axis=2)
    cx, sx = _dft_cos_sin(CTX_LEN)
    mctx = np.concatenate([cx, sx], axis=1) / math.sqrt(CTX_LEN)
    as32 = lambda a: jnp.asarray(a, F32)
    return (as32(wc).astype(BF16), as32(m1).astype(BF16), as32(m3).astype(BF16),
            as32(twc), as32(tws), as32(mctx).astype(BF16))


def _head_mean_matrix():
    blk = np.kron(np.eye(RMS_W // HEAD_DIM), np.ones((HEAD_DIM, HEAD_DIM))) / HEAD_DIM
    return jnp.asarray(blk, BF16)


def _rope_tables():
    t = np.arange(SEQ)
    row = (t // GRID_W).astype(np.float32)
    col = (t % GRID_W).astype(np.float32)
    inv = np.float32(ROPE_THETA) ** (-np.arange(ROPE_FREQS, dtype=np.float32) / np.float32(ROPE_FREQS))
    ang_row = (row[:, None] * inv[None, :]).astype(np.float32).astype(np.float64)
    ang_col = (col[:, None] * inv[None, :]).astype(np.float32).astype(np.float64)
    zero = np.zeros_like(ang_row)
    cos = np.concatenate([np.cos(ang_row)] * 2 + [np.cos(ang_col)] * 2, axis=1)
    sin_hi = np.concatenate([-np.sin(ang_row), zero, -np.sin(ang_col), zero], axis=1)
    sin_lo = np.concatenate([zero, np.sin(ang_row), zero, np.sin(ang_col)], axis=1)
    rep = LANES // HEAD_DIM
    return tuple(jnp.asarray(np.tile(a, (1, rep)), F32) for a in (cos, sin_hi, sin_lo))


def _na_bias_index():
    cq = np.arange(GRID_W)
    dc = np.clip(cq[None, :] - cq[:, None] + NA_KW - 1, 0, 2 * NA_KW - 2)
    return jnp.asarray(np.concatenate([dc, dc + LANES // 2], axis=1), jnp.int32)


def _mod_kernel(cs_ref, w_ref, b_ref, o_ref):
    s = _silu(cs_ref[...]).astype(BF16)
    o_ref[0] = _dot(s, w_ref[0].astype(BF16)) + b_ref[0]


def _modulation(cs, ada_w, ada_b):
    tn = 1536
    return pl.pallas_call(
        _mod_kernel,
        grid=(DEPTH, 6 * D_MODEL // tn),
        in_specs=[pl.BlockSpec((MOD_ROWS, D_MODEL), lambda i, j: (0, 0)),
                  pl.BlockSpec((1, D_MODEL, tn), lambda i, j: (i, 0, j)),
                  pl.BlockSpec((1, 1, tn), lambda i, j: (i, 0, j))],
        out_specs=pl.BlockSpec((1, MOD_ROWS, tn), lambda i, j: (i, 0, j)),
        out_shape=jax.ShapeDtypeStruct((DEPTH, MOD_ROWS, 6 * D_MODEL), F32),
        compiler_params=_params("arbitrary", "arbitrary"),
        name="ada_modulation",
    )(cs, ada_w, ada_b.reshape(DEPTH, 1, 6 * D_MODEL))


def _mod_slice(mod_ref, k):
    return mod_ref[0, :, k * D_MODEL:(k + 1) * D_MODEL]


def _rms_mod(x, g, scale, shift):
    y = x * lax.rsqrt(jnp.mean(x * x, axis=-1, keepdims=True) + EPS)
    return (y * g) * (1.0 + scale) + shift


def _head_rms(t, gain_ref, hm_ref):
    w = t.shape[1]
    ms = _dot((t * t).astype(BF16), hm_ref[:w, :w])
    return t * lax.rsqrt(ms + EPS) * gain_ref[:, :w]


def _rope(t, cos, sin_hi, sin_lo):
    up = pltpu.roll(t, LANES - ROPE_FREQS, axis=1)
    dn = pltpu.roll(t, ROPE_FREQS, axis=1)
    return t * cos + up * sin_hi + dn * sin_lo


def _cast_once(w32_ref, w_ref):
    @pl.when((pl.program_id(0) == 0) & (pl.program_id(1) == 0))
    def _():
        w_ref[...] = w32_ref[...].astype(BF16)


def _in_even_kernel(latent, x_ref, mod_ref, g_ref, w32_ref, qg_ref, kg_ref, hm_ref, wc_ref, *rest):
    if latent:
        cos_ref, sh_ref, sl_ref, y_ref, q_ref, k_ref, v_ref, w_ref = rest
    else:
        y_ref, q_ref, k_ref, v_ref, w_ref = rest
    _cast_once(w32_ref, w_ref)
    F = FOURIER_WIDTH
    lo_lanes = lax.broadcasted_iota(jnp.int32, (1, LANES), 1) < HEAD_DIM

    for r in range(x_ref.shape[1] // SUB_ROWS):
        rows = slice(r * SUB_ROWS, (r + 1) * SUB_ROWS)
        h = _rms_mod(x_ref[0, rows], g_ref[...], _mod_slice(mod_ref, 1), _mod_slice(mod_ref, 0)).astype(BF16)
        f = _dot(h, w_ref[:, :F]).astype(BF16)
        for g in range(FOURIER_GROUPS):
            yg = _dot(f[:, g * LANES:(g + 1) * LANES], wc_ref[...])
            y_ref[0, rows, g * LANES:(g + 1) * LANES] = yg[:, :LANES].astype(y_ref.dtype)
            y_ref[0, rows, F + g * LANES:F + (g + 1) * LANES] = yg[:, LANES:].astype(y_ref.dtype)

        def qk(t, gain_ref):
            t = _head_rms(t, gain_ref, hm_ref)
            if latent:
                t = jnp.concatenate([_rope(t[:, i * LANES:(i + 1) * LANES], cos_ref[rows], sh_ref[rows], sl_ref[rows])
                                     for i in range(t.shape[1] // LANES)], axis=1)
            return t

        q = _dot(h, w_ref[:, F:F + QW])
        for j in range(QW // RMS_W):
            qj = qk(q[:, j * RMS_W:(j + 1) * RMS_W], qg_ref) * QK_SCALE
            q_ref[0, rows, j * RMS_W:(j + 1) * RMS_W] = qj.astype(BF16)
        kv = _dot(h, w_ref[:, F + QW:])

        def store_dup(t, o_ref):
            sw = pltpu.roll(t, HEAD_DIM, axis=1)
            o_ref[0, 0, rows] = jnp.where(lo_lanes, t, sw).astype(BF16)
            o_ref[0, 1, rows] = jnp.where(lo_lanes, sw, t).astype(BF16)

        store_dup(qk(kv[:, :KW], kg_ref), k_ref)
        store_dup(kv[:, KW:], v_ref)


def _in_even(x, mod, mod_row, g, w_in, q_g, k_g, hm, wc, rope_tabs, tm):
    B, L, _ = x.shape
    latent = rope_tabs is not None
    in_specs = [pl.BlockSpec((1, tm, D_MODEL), lambda b, t: (b, t, 0)),
                pl.BlockSpec((1, 1, 6 * D_MODEL), lambda b, t: (mod_row(b), 0, 0)),
                _const_spec((1, D_MODEL)),
                _const_spec((D_MODEL, EV_IN_WIDTH)),
                _const_spec((1, RMS_W)), _const_spec((1, RMS_W)),
                _const_spec((RMS_W, RMS_W)), _const_spec((LANES, 2 * LANES))]
    args = [x, mod, g, w_in, q_g, k_g, hm, wc]
    tok = lambda w: pl.BlockSpec((1, tm, w), lambda b, t: (b, t, 0))
    kv_spec = pl.BlockSpec((1, WIN_KV_HEADS, tm, LANES), lambda b, t: (b, 0, t, 0))
    if latent:
        in_specs += [pl.BlockSpec((tm, LANES), lambda b, t: (t, 0))] * 3
        args += list(rope_tabs)
    return pl.pallas_call(
        functools.partial(_in_even_kernel, latent),
        grid=(B, L // tm),
        in_specs=in_specs,
        out_specs=[tok(2 * FOURIER_WIDTH), tok(QW), kv_spec, kv_spec],
        out_shape=[jax.ShapeDtypeStruct((B, L, 2 * FOURIER_WIDTH), F32 if latent else BF16),
                   jax.ShapeDtypeStruct((B, L, QW), BF16),
                   jax.ShapeDtypeStruct((B, WIN_KV_HEADS, L, LANES), BF16),
                   jax.ShapeDtypeStruct((B, WIN_KV_HEADS, L, LANES), BF16)],
        compiler_params=_params("arbitrary", "arbitrary"),
        scratch_shapes=[pltpu.VMEM((D_MODEL, EV_IN_WIDTH), BF16)],
        name="in_even_latent" if latent else "in_even_ctx",
    )(*args)


def _fft_stage1_kernel(n2t, y_ref, m1_ref, twc_ref, tws_ref, o_ref, row_ref):
    F = FOURIER_WIDTH
    for j in range(n2t):
        row_ref[j % 2] = y_ref[0, :, j, :]
        a = _dot(m1_ref[...], row_ref[j % 2].astype(BF16))
        top, bot = a[:FFT_N], a[FFT_N:]
        ar = top[:, :F] - bot[:, F:]
        ai = top[:, F:] + bot[:, :F]
        tc = jnp.tile(twc_ref[j], (1, F // LANES))
        ts = jnp.tile(tws_ref[j], (1, F // LANES))
        o_ref[0, 0, :, j * F:(j + 1) * F] = (ar * tc + ai * ts).astype(BF16)
        o_ref[0, 1, :, j * F:(j + 1) * F] = (ai * tc - ar * ts).astype(BF16)


def _fft_stage2_kernel(k1t, b_ref, m3_ref, o_ref):
    for j in range(k1t):
        rhs = jnp.concatenate([b_ref[0, 0, j], b_ref[0, 1, j]], axis=0)
        o_ref[0, j] = _dot(m3_ref[...], rhs)


def _fourier_latent(y, m1, m3, twc, tws):
    B = y.shape[0]
    F = FOURIER_WIDTH
    n2t = 16
    yv = y.reshape(B, FFT_N, FFT_N, 2 * F)
    bh = pl.pallas_call(
        functools.partial(_fft_stage1_kernel, n2t),
        grid=(B, FFT_N // n2t),
        in_specs=[pl.BlockSpec((1, FFT_N, n2t, 2 * F), lambda b, t: (b, 0, t, 0)),
                  _const_spec((2 * FFT_N, FFT_N)),
                  pl.BlockSpec((n2t, FFT_N, LANES), lambda b, t: (t, 0, 0)),
                  pl.BlockSpec((n2t, FFT_N, LANES), lambda b, t: (t, 0, 0))],
        out_specs=pl.BlockSpec((1, 2, FFT_N, n2t * F), lambda b, t: (b, 0, 0, t)),
        out_shape=jax.ShapeDtypeStruct((B, 2, FFT_N, FFT_N * F), BF16),
        scratch_shapes=[pltpu.VMEM((2, FFT_N, 2 * F), F32)],
        compiler_params=_params("arbitrary", "arbitrary"),
        name="fft_stage1",
    )(yv, m1, twc, tws)
    k1t = 16
    bv = bh.reshape(B, 2, FFT_N, FFT_N, F)
    return pl.pallas_call(
        functools.partial(_fft_stage2_kernel, k1t),
        grid=(B, FFT_N // k1t),
        in_specs=[pl.BlockSpec((1, 2, k1t, FFT_N, F), lambda b, t: (b, 0, t, 0, 0)),
                  _const_spec((FFT_N, 2 * FFT_N))],
        out_specs=pl.BlockSpec((1, k1t, FFT_N, F), lambda b, t: (b, t, 0, 0)),
        out_shape=jax.ShapeDtypeStruct((B, FFT_N, FFT_N, F), F32),
        compiler_params=_params("arbitrary", "arbitrary"),
        name="fft_stage2",
    )(bv, m3)


def _fourier_ctx_kernel(y_ref, m_ref, o_ref):
    F = FOURIER_WIDTH
    rhs = jnp.concatenate([y_ref[0, :, :F], y_ref[0, :, F:]], axis=0)
    o_ref[0] = _dot(m_ref[...], rhs)


def _fourier_ctx(y, mctx):
    B = y.shape[0]
    return pl.pallas_call(
        _fourier_ctx_kernel,
        grid=(B,),
        in_specs=[pl.BlockSpec((1, CTX_LEN, 2 * FOURIER_WIDTH), lambda b: (b, 0, 0)),
                  _const_spec((CTX_LEN, 2 * CTX_LEN))],
        out_specs=pl.BlockSpec((1, CTX_LEN, FOURIER_WIDTH), lambda b: (b, 0, 0)),
        out_shape=jax.ShapeDtypeStruct((B, CTX_LEN, FOURIER_WIDTH), F32),
        compiler_params=_params("arbitrary"),
        name="fourier_ctx",
    )(y, mctx)


def _gqa_scores(q_slabs, keys, rows, s_ref):
    lo_lanes = lax.broadcasted_iota(jnp.int32, (rows, LANES), 1) < HEAD_DIM
    zero = jnp.zeros((rows, LANES), BF16)
    qs = jnp.concatenate([jnp.where(lo_lanes if half == 0 else ~lo_lanes, qslab, zero)
                          for qslab in q_slabs for half in range(2)], axis=0)
    s_ref[...] = _dot_nt(qs, keys)


def _gqa_attend(vals, masks, sinks, rows, out_slab, first_slab, scratch):
    s_ref, p_ref, inv_ref = scratch
    lo_lanes = lax.broadcasted_iota(jnp.int32, (rows, LANES), 1) < HEAD_DIM
    for c in range(WIN_GROUP * rows // SOFTMAX_ROWS):
        rs = slice(c * SOFTMAX_ROWS, (c + 1) * SOFTMAX_ROWS)
        s = s_ref[rs, :]
        if masks is not None:
            band, w = masks
            r0 = (c * SOFTMAX_ROWS) % rows
            s = jnp.concatenate([s[:, :w] + band[r0:r0 + SOFTMAX_ROWS], s[:, w:]], axis=1)
        sk = sinks[(c * SOFTMAX_ROWS) // rows]
        m = jnp.maximum(jnp.max(s, axis=-1, keepdims=True), sk)
        e = jnp.exp2(s - m)
        inv_ref[rs, :] = 1.0 / (jnp.sum(e, axis=-1, keepdims=True) + jnp.exp2(sk - m))
        p_ref[rs, :] = e.astype(BF16)
    o = _dot(p_ref[...], vals) * inv_ref[...]
    for i in range(WIN_GROUP // 2):
        a = o[(2 * i) * rows:(2 * i + 1) * rows]
        b = o[(2 * i + 1) * rows:(2 * i + 2) * rows]
        sl = slice((first_slab + i) * LANES, (first_slab + i + 1) * LANES)
        out_slab(sl)[...] = jnp.where(lo_lanes, a, b).astype(BF16)


def _gqa_scratch(rows, n_keys):
    one = [pltpu.VMEM((WIN_GROUP * rows, n_keys), F32), pltpu.VMEM((WIN_GROUP * rows, n_keys), BF16),
           pltpu.VMEM((WIN_GROUP * rows, 1), F32)]
    return one * WIN_KV_HEADS


def _group_sinks(sink_ref, kv):
    return [sink_ref[kv * WIN_GROUP + g] * LOG2E for g in range(WIN_GROUP)]


def _group_slab(kv, i):
    return slice((kv * (WIN_GROUP // 2) + i) * LANES, (kv * (WIN_GROUP // 2) + i + 1) * LANES)


def _window_bands():
    i = np.arange(WIN_BLOCK)[:, None]
    j = np.arange(3 * WIN_BLOCK)[None, :]
    bands = [np.where(np.abs(j - off - i) <= WIN_RADIUS, 0.0, -np.inf) for off in (0, WIN_BLOCK, 2 * WIN_BLOCK)]
    return jnp.asarray(np.stack(bands).astype(np.float32))


def _win_attn_kernel(blocks_per_step, sink_ref, band_ref, q_ref, qn_ref, k_ref, v_ref, kx_ref, vx_ref, o_ref,
                     s0_ref, s1_ref, p_ref, inv_ref):
    t = pl.program_id(1)
    nb = SEQ // WIN_BLOCK
    nw = 3 * WIN_BLOCK

    def place(j):
        n = jnp.minimum(t * blocks_per_step + j, nb - 1)
        start = pl.multiple_of(jnp.clip((n - 1) * WIN_BLOCK, 0, SEQ - nw), WIN_BLOCK)
        sel = jnp.where(n == 0, 0, jnp.where(n == nb - 1, 2, 1))
        return pl.multiple_of(jnp.minimum(j, blocks_per_step - 1) * WIN_BLOCK, WIN_BLOCK), start, sel

    def scores(j, kv, s_ref):
        qoff, start, _ = place(j)
        keys = jnp.concatenate([k_ref[0, kv, pl.ds(start, nw), :], kx_ref[0, kv]], axis=0)
        q_slabs = [jnp.where(j < blocks_per_step, q_ref[0, pl.ds(qoff, WIN_BLOCK), _group_slab(kv, i)],
                             qn_ref[0, :, _group_slab(kv, i)]) for i in range(WIN_GROUP // 2)]
        _gqa_scores(q_slabs, keys, WIN_BLOCK, s_ref)

    def attend(j, kv, s_ref):
        qoff, start, sel = place(j)
        vals = jnp.concatenate([v_ref[0, kv, pl.ds(start, nw), :], vx_ref[0, kv]], axis=0)
        _gqa_attend(vals, (band_ref[sel], nw), _group_sinks(sink_ref, kv), WIN_BLOCK,
                    lambda sl: o_ref.at[0, pl.ds(qoff, WIN_BLOCK), sl], kv * (WIN_GROUP // 2),
                    (s_ref, p_ref, inv_ref))

    @pl.when(t == 0)
    def _():
        scores(0, 0, s0_ref)

    def one_block(j, carry):
        scores(j, 1, s1_ref)
        attend(j, 0, s0_ref)
        scores(j + 1, 0, s0_ref)
        attend(j, 1, s1_ref)
        return carry

    lax.fori_loop(0, blocks_per_step, one_block, 0)


def _win_attention(q, k, v, kx, vx, sink):
    B, L, _ = q.shape
    blocks_per_step = 16
    rows = blocks_per_step * WIN_BLOCK
    n_keys = 3 * WIN_BLOCK + CTX_LEN
    full = pl.BlockSpec((1, WIN_KV_HEADS, L, LANES), lambda b, t: (b, 0, 0, 0))
    ctx_spec = pl.BlockSpec((1, WIN_KV_HEADS, CTX_LEN, LANES), lambda b, t: (0, 0, b, 0))
    tok = pl.BlockSpec((1, rows, QW), lambda b, t: (b, t, 0))
    q_next = pl.BlockSpec((1, WIN_BLOCK, QW),
                          lambda b, t: (b, jnp.minimum((t + 1) * blocks_per_step, L // WIN_BLOCK - 1), 0))
    return pl.pallas_call(
        functools.partial(_win_attn_kernel, blocks_per_step),
        grid=(B, L // rows),
        in_specs=[pl.BlockSpec(memory_space=pltpu.SMEM),
                  _const_spec((3, WIN_BLOCK, 3 * WIN_BLOCK)),
                  tok, q_next, full, full, ctx_spec, ctx_spec],
        out_specs=tok,
        out_shape=jax.ShapeDtypeStruct((B, L, QW), BF16),
        scratch_shapes=[pltpu.VMEM((WIN_GROUP * WIN_BLOCK, n_keys), F32), pltpu.VMEM((WIN_GROUP * WIN_BLOCK, n_keys), F32),
                        pltpu.VMEM((WIN_GROUP * WIN_BLOCK, n_keys), BF16), pltpu.VMEM((WIN_GROUP * WIN_BLOCK, 1), F32)],
        compiler_params=_params("arbitrary", "arbitrary"),
        name="window_attention",
    )(sink, _window_bands(), q, q, k, v, kx, vx)


def _ctx_attn_even_kernel(sink_ref, q_ref, k_ref, v_ref, o_ref, *scratch):
    for b in range(q_ref.shape[0]):
        tokens = slice(b * CTX_LEN, (b + 1) * CTX_LEN)
        for kv in range(WIN_KV_HEADS):
            unit = scratch[3 * (b * WIN_KV_HEADS + kv):3 * (b * WIN_KV_HEADS + kv) + 3]
            q_slabs = [q_ref[b, :, _group_slab(kv, i)] for i in range(WIN_GROUP // 2)]
            _gqa_scores(q_slabs, k_ref[0, kv, tokens, :], CTX_LEN, unit[0])
            _gqa_attend(v_ref[0, kv, tokens, :], None, _group_sinks(sink_ref, kv), CTX_LEN,
                        lambda sl, b=b: o_ref.at[b, :, sl], kv * (WIN_GROUP // 2), unit)


def _ctx_attention_even(q, k, v, sink):
    B = q.shape[0]
    whole = lambda a: pl.BlockSpec(a.shape, lambda i: (0,) * a.ndim)
    return pl.pallas_call(
        _ctx_attn_even_kernel,
        grid=(1,),
        in_specs=[pl.BlockSpec(memory_space=pltpu.SMEM), whole(q), whole(k), whole(v)],
        out_specs=whole(q),
        out_shape=jax.ShapeDtypeStruct((B, CTX_LEN, QW), BF16),
        scratch_shapes=_gqa_scratch(CTX_LEN, CTX_LEN) * B,
        compiler_params=_params("arbitrary"),
        name="ctx_attention_even",
    )(sink, q, k, v)


def _out_ffn_kernel(mode, *refs):
    if mode == "odd":
        a_ref, x_ref, mod_ref, g_ref, wo_ref, wg_ref, wu_ref, wd_ref, o_ref = refs
    else:
        f_ref, a_ref, x_ref, mod_ref, g_ref, wo_ref, wg_ref, wu_ref, wd_ref, o_ref = refs
    for r in range(x_ref.shape[1] // FFN_SUB_ROWS):
        rows = slice(r * FFN_SUB_ROWS, (r + 1) * FFN_SUB_ROWS)
        if mode == "even_latent":
            planes = range(r * FFN_SUB_ROWS // FFT_N, (r + 1) * FFN_SUB_ROWS // FFT_N)
            fm = jnp.concatenate([f_ref[0, :, j, :] for j in planes], axis=0).astype(BF16)
            o = _dot(fm, wo_ref[:FOURIER_WIDTH]) + _dot(a_ref[0, rows], wo_ref[FOURIER_WIDTH:])
        else:
            o = _dot(jnp.concatenate([a_ref[0, j, rows] for j in range(a_ref.shape[1])], axis=1), wo_ref[...])
        x1 = x_ref[0, rows] + _mod_slice(mod_ref, 2) * o
        h = _rms_mod(x1, g_ref[...], _mod_slice(mod_ref, 4), _mod_slice(mod_ref, 3)).astype(BF16)
        acc = jnp.zeros_like(x1)
        for c in range(D_FF // FF_CHUNK):
            cs = slice(c * FF_CHUNK, (c + 1) * FF_CHUNK)
            a = _silu(_dot(h, wg_ref[0, :, cs])) * _dot(h, wu_ref[0, :, cs])
            acc = acc + _dot(a.astype(BF16), wd_ref[0, cs, :])
        o_ref[0, rows] = x1 + _mod_slice(mod_ref, 5) * acc


def _out_ffn(mode, mix, x, mod, mod_row, g, w_out, layer, wg, wu, wd, tm):
    B, L, _ = x.shape
    tok = lambda w: pl.BlockSpec((1, tm, w), lambda b, t: (b, t, 0))
    ffn_spec = lambda r, c: pl.BlockSpec((1, r, c), lambda b, t: (layer, 0, 0), pipeline_mode=pl.Buffered(1))
    assert mode in ("even_latent", "odd") and tm % FFN_SUB_ROWS == 0
    if mode == "even_latent":
        mix_specs = [pl.BlockSpec((1, FFT_N, tm // FFT_N, FOURIER_WIDTH), lambda b, t: (b, 0, t, 0)), tok(QW)]
    else:
        mix_specs = [pl.BlockSpec((1, NA_WIDTH // LANES, tm, LANES), lambda b, t: (b, 0, t, 0))]
    return pl.pallas_call(
        functools.partial(_out_ffn_kernel, mode),
        grid=(B, L // tm),
        in_specs=mix_specs + [tok(D_MODEL),
                              pl.BlockSpec((1, 1, 6 * D_MODEL), lambda b, t: (mod_row(b), 0, 0)),
                              _const_spec((1, D_MODEL)),
                              _const_spec((D_MODEL, D_MODEL)),
                              ffn_spec(D_MODEL, D_FF), ffn_spec(D_MODEL, D_FF), ffn_spec(D_FF, D_MODEL)],
        out_specs=tok(D_MODEL),
        out_shape=jax.ShapeDtypeStruct((B, L, D_MODEL), F32),
        compiler_params=_params("arbitrary", "arbitrary"),
        name="out_ffn_" + mode,
    )(*mix, x, mod, g, w_out, wg, wu, wd)


def _ctx_ffn_kernel(f_ref, a_ref, x_ref, mod_ref, g_ref, wo_ref, wg32_ref, wu32_ref, wd32_ref,
                    o_ref, wg_ref, wu_ref, wd_ref, x1_ref, h_ref, acc_ref):
    c = pl.program_id(0)
    wg_ref[...] = wg32_ref[...].astype(BF16)
    wu_ref[...] = wu32_ref[...].astype(BF16)
    wd_ref[...] = wd32_ref[...].astype(BF16)

    @pl.when(c == 0)
    def _():
        o = _dot(f_ref[0].astype(BF16), wo_ref[:FOURIER_WIDTH]) + _dot(a_ref[0], wo_ref[FOURIER_WIDTH:])
        x1 = x_ref[0] + _mod_slice(mod_ref, 2) * o
        x1_ref[...] = x1
        h_ref[...] = _rms_mod(x1, g_ref[...], _mod_slice(mod_ref, 4), _mod_slice(mod_ref, 3)).astype(BF16)
        acc_ref[...] = jnp.zeros_like(acc_ref)

    h = h_ref[...]
    a = _silu(_dot(h, wg_ref[0])) * _dot(h, wu_ref[0])
    acc_ref[...] += _dot(a.astype(BF16), wd_ref[0])

    @pl.when(c == pl.num_programs(0) - 1)
    def _():
        o_ref[0] = x1_ref[...] + _mod_slice(mod_ref, 5) * acc_ref[...]


def _ctx_ffn_and_weights(f, a, x, mod, mod_row, g, w_out, wg32, wu32, wd32):
    _, n, _ = x.shape
    nc = D_FF // FF_CHUNK
    whole = lambda w: pl.BlockSpec((1, n, w), lambda c: (0, 0, 0))
    col_chunk = pl.BlockSpec((DEPTH, D_MODEL, FF_CHUNK), lambda c: (0, 0, c))
    row_chunk = pl.BlockSpec((DEPTH, FF_CHUNK, D_MODEL), lambda c: (0, c, 0))
    return pl.pallas_call(
        _ctx_ffn_kernel,
        grid=(nc,),
        in_specs=[whole(FOURIER_WIDTH), whole(QW), whole(D_MODEL),
                  pl.BlockSpec((1, 1, 6 * D_MODEL), lambda c: (mod_row(0), 0, 0)),
                  pl.BlockSpec((1, D_MODEL), lambda c: (0, 0)),
                  pl.BlockSpec((D_MODEL, D_MODEL), lambda c: (0, 0)),
                  col_chunk, col_chunk, row_chunk],
        out_specs=[whole(D_MODEL), col_chunk, col_chunk, row_chunk],
        out_shape=[jax.ShapeDtypeStruct((1, n, D_MODEL), F32),
                   jax.ShapeDtypeStruct((DEPTH, D_MODEL, D_FF), BF16),
                   jax.ShapeDtypeStruct((DEPTH, D_MODEL, D_FF), BF16),
                   jax.ShapeDtypeStruct((DEPTH, D_FF, D_MODEL), BF16)],
        scratch_shapes=[pltpu.VMEM((n, D_MODEL), F32), pltpu.VMEM((n, D_MODEL), BF16), pltpu.VMEM((n, D_MODEL), F32)],
        compiler_params=_params("arbitrary"),
        name="ctx_ffn_and_weights",
    )(f, a, x, mod, g, w_out, wg32, wu32, wd32)


def _in_odd_kernel(with_q, x_ref, mod_ref, g_ref, w32_ref, qg_ref, kg_ref, hm_ref, *outs):
    W = NA_WIDTH
    slabs = RMS_W // LANES
    if with_q:
        q_ref, k_ref, v_ref, w_ref = outs
    else:
        k_ref, v_ref, w_ref = outs
    _cast_once(w32_ref, w_ref)

    for r in range(x_ref.shape[1] // SUB_ROWS):
        rows = slice(r * SUB_ROWS, (r + 1) * SUB_ROWS)
        h = _rms_mod(x_ref[0, rows], g_ref[...], _mod_slice(mod_ref, 1), _mod_slice(mod_ref, 0)).astype(BF16)

        def normed(t, gain_ref, scale, o_ref):
            for j in range(W // RMS_W):
                tj = (_head_rms(t[:, j * RMS_W:(j + 1) * RMS_W], gain_ref, hm_ref) * scale).astype(BF16)
                for i in range(slabs):
                    o_ref[0, j * slabs + i, rows] = tj[:, i * LANES:(i + 1) * LANES]

        if with_q:
            normed(_dot(h, w_ref[:, :W]), qg_ref, QK_SCALE, q_ref)
        normed(_dot(h, w_ref[:, W:2 * W]), kg_ref, 1.0, k_ref)
        v = _dot(h, w_ref[:, 2 * W:]).astype(BF16)
        for j in range(W // LANES):
            v_ref[0, j, rows] = v[:, j * LANES:(j + 1) * LANES]


def _in_odd(x, mod, mod_row, g, w_in, q_g, k_g, hm, with_q, tm):
    B, L, _ = x.shape
    pairs = NA_WIDTH // LANES
    tok = pl.BlockSpec((1, pairs, tm, LANES), lambda b, t: (b, 0, t, 0))
    n_out = 3 if with_q else 2
    return pl.pallas_call(
        functools.partial(_in_odd_kernel, with_q),
        grid=(B, L // tm),
        in_specs=[pl.BlockSpec((1, tm, D_MODEL), lambda b, t: (b, t, 0)),
                  pl.BlockSpec((1, 1, 6 * D_MODEL), lambda b, t: (mod_row(b), 0, 0)),
                  _const_spec((1, D_MODEL)),
                  _const_spec((D_MODEL, 3 * NA_WIDTH)),
                  _const_spec((1, RMS_W)), _const_spec((1, RMS_W)), _const_spec((RMS_W, RMS_W))],
        out_specs=[tok] * n_out,
        out_shape=[jax.ShapeDtypeStruct((B, pairs, L, LANES), BF16)] * n_out,
        compiler_params=_params("arbitrary", "arbitrary"),
        scratch_shapes=[pltpu.VMEM((D_MODEL, 3 * NA_WIDTH), BF16)],
        name="in_odd" if with_q else "in_odd_ctx",
    )(x, mod, g, w_in, q_g, k_g, hm)


def _bias_table_kernel(rb_ref, idx_ref, o_ref):
    idx = idx_ref[...]
    cq = lax.broadcasted_iota(jnp.int32, idx.shape, 0)
    ck = lax.broadcasted_iota(jnp.int32, idx.shape, 1) % GRID_W
    c0 = jnp.clip(cq - NA_KW // 2, 0, GRID_W - NA_KW)
    inside = (ck >= c0) & (ck < c0 + NA_KW)
    for h in range(o_ref.shape[0]):
        for dr in range(o_ref.shape[1]):
            row = jnp.broadcast_to(rb_ref[h, dr:dr + 1, :], idx.shape)
            o_ref[h, dr] = jnp.where(inside, jnp.take_along_axis(row, idx, axis=1) * LOG2E, -jnp.inf)


def _na_bias_table(rel_bias, idx):
    H = NA_HEADS
    npair = 2 * NA_KH - 2
    half = LANES // 2
    hb = 8
    pad = lambda a: jnp.pad(a, ((0, 0), (0, 0), (0, half - a.shape[-1])))
    rb2 = jnp.concatenate([pad(rel_bias[:, :-1]), pad(rel_bias[:, 1:])], axis=-1)
    return pl.pallas_call(
        _bias_table_kernel,
        grid=(H // hb,),
        in_specs=[pl.BlockSpec((hb, npair, LANES), lambda h: (h, 0, 0)), _const_spec((GRID_W, LANES))],
        out_specs=pl.BlockSpec((hb, npair, GRID_W, LANES), lambda h: (h, 0, 0, 0)),
        out_shape=jax.ShapeDtypeStruct((H, npair, GRID_W, LANES), F32),
        compiler_params=_params("arbitrary"),
        name="na_bias_table",
    )(rb2, idx)


def _na_kernel(rows_per_step, q_ref, qn_ref, k_ref, v_ref, kx_ref, vx_ref, bias_ref, o_ref,
               s0_ref, s1_ref, p_ref, inv_ref):
    t = pl.program_id(1)
    n = NA_KH * GRID_W
    lo_lanes = lax.broadcasted_iota(jnp.int32, (GRID_W, LANES), 1) < HEAD_DIM

    def window(i):
        r = jnp.minimum(t * rows_per_step + i, GRID_ROWS - 1)
        r0 = jnp.clip(r - NA_KH // 2, 0, GRID_ROWS - NA_KH)
        qoff = jnp.minimum(i, rows_per_step - 1) * GRID_W
        return r0 - r + NA_KH - 1, pl.multiple_of(r0 * GRID_W, GRID_W), pl.multiple_of(qoff, GRID_W)

    def scores(i, s_ref):
        dr0, start, qoff = window(i)
        for hp in range(NA_HEADS // 2):
            qs = jnp.where(i < rows_per_step, q_ref[0, hp, pl.ds(qoff, GRID_W), :], qn_ref[0, hp])
            keys = jnp.concatenate([k_ref[0, hp, pl.ds(start, n), :], kx_ref[0, hp]], axis=0)
            qm = jnp.concatenate([jnp.where(lo_lanes, qs, jnp.zeros_like(qs)),
                                  jnp.where(lo_lanes, jnp.zeros_like(qs), qs)], axis=0)
            bias = jnp.concatenate(
                [jnp.concatenate([bias_ref[2 * hp + half, dr0 + 2 * p] for p in range(NA_KH // 2)], axis=1)
                 for half in range(2)], axis=0)
            rs = slice(2 * hp * GRID_W, (2 * hp + 2) * GRID_W)
            s_ref[rs, :n] = _dot_nt(qm, keys[:n]) + bias
            s_ref[rs, n:] = _dot_nt(qm, keys[n:])

    def attend(i, s_ref):
        _, start, qoff = window(i)
        for c in range(NA_HEADS * GRID_W // SOFTMAX_ROWS):
            cs = slice(c * SOFTMAX_ROWS, (c + 1) * SOFTMAX_ROWS)
            s = s_ref[cs, :]
            e = jnp.exp2(s - jnp.max(s, axis=-1, keepdims=True))
            inv_ref[cs, :] = 1.0 / jnp.sum(e, axis=-1, keepdims=True)
            p_ref[cs, :] = e.astype(BF16)
        for hp in range(NA_HEADS // 2):
            vals = jnp.concatenate([v_ref[0, hp, pl.ds(start, n), :], vx_ref[0, hp]], axis=0)
            rs = slice(2 * hp * GRID_W, (2 * hp + 2) * GRID_W)
            res = _dot(p_ref[rs, :], vals) * inv_ref[rs, :]
            o_ref[0, hp, pl.ds(qoff, GRID_W), :] = jnp.where(lo_lanes, res[:GRID_W], res[GRID_W:]).astype(BF16)

    @pl.when(t == 0)
    def _():
        scores(0, s0_ref)

    def two_rows(j, carry):
        scores(2 * j + 1, s1_ref)
        attend(2 * j, s0_ref)
        scores(2 * j + 2, s0_ref)
        attend(2 * j + 1, s1_ref)
        return carry

    lax.fori_loop(0, rows_per_step // 2, two_rows, 0)


def _neighbourhood_attention(q, k, v, kx, vx, bias_tab):
    B, pairs, L, _ = q.shape
    rows_per_step = 16
    full = pl.BlockSpec((1, pairs, L, LANES), lambda b, t: (b, 0, 0, 0))
    ctx_spec = pl.BlockSpec((1, pairs, CTX_LEN, LANES), lambda b, t: (0, 0, b, 0))
    q_rows = pl.BlockSpec((1, pairs, rows_per_step * GRID_W, LANES), lambda b, t: (b, 0, t, 0))
    q_next = pl.BlockSpec((1, pairs, GRID_W, LANES),
                          lambda b, t: (b, 0, jnp.minimum((t + 1) * rows_per_step, GRID_ROWS - 1), 0))
    n_keys = NA_KH * GRID_W + CTX_LEN
    all_rows = NA_HEADS * GRID_W
    return pl.pallas_call(
        functools.partial(_na_kernel, rows_per_step),
        grid=(B, GRID_ROWS // rows_per_step),
        in_specs=[q_rows, q_next, full, full, ctx_spec, ctx_spec, _const_spec(bias_tab.shape)],
        out_specs=q_rows,
        out_shape=jax.ShapeDtypeStruct((B, pairs, L, LANES), BF16),
        scratch_shapes=[pltpu.VMEM((all_rows, n_keys), F32), pltpu.VMEM((all_rows, n_keys), F32),
                        pltpu.VMEM((all_rows, n_keys), BF16), pltpu.VMEM((all_rows, 1), F32)],
        compiler_params=_params("arbitrary", "arbitrary"),
        name="neighbourhood_attention",
    )(q, q, k, v, kx, vx, bias_tab)


def kernel(x, c, ctx, c_ctx, ada_w, ada_b, norm1_g, norm2_g, ffn_w_gate, ffn_w_up, ffn_w_down,
           ev_w_in, ev_w_out, ev_q_norm, ev_k_norm, ev_sink,
           od_w_in, od_w_out, od_q_norm, od_k_norm, od_rel_bias):
    assert x.shape == (BATCH, SEQ, D_MODEL) and ctx.shape == (BATCH, CTX_LEN, D_MODEL)
    wc, m1, m3, twc, tws, mctx = _fourier_tables()
    hm = _head_mean_matrix()
    rope_tabs = _rope_tables()
    lane_gain = lambda gvec: jnp.tile(gvec, RMS_W // HEAD_DIM).reshape(1, RMS_W)
    lat_row = lambda b: b
    ctx_row = lambda b: CTX_MOD_ROW
    tm = 512
    n_ctx = BATCH * CTX_LEN
    as_seq = lambda a: a.reshape(1, n_ctx, a.shape[-1])
    per_batch = lambda a: a.reshape(BATCH, CTX_LEN, a.shape[-1])

    cs = jnp.concatenate([c, c_ctx[None, :], jnp.zeros((MOD_ROWS - BATCH - 1, D_MODEL), F32)], axis=0)
    mod = _modulation(cs, ada_w, ada_b).reshape(DEPTH, MOD_ROWS, 1, 6 * D_MODEL)

    w_in0 = ev_w_in[0]
    w_out0 = ev_w_out[0].astype(BF16)
    g1 = norm1_g[0].reshape(1, D_MODEL)
    g2 = norm2_g[0].reshape(1, D_MODEL)
    qg, kg = lane_gain(ev_q_norm[0]), lane_gain(ev_k_norm[0])
    y_l, q_l, k_l, v_l = _in_even(x, mod[0], lat_row, g1, w_in0, qg, kg, hm, wc, rope_tabs, 2 * tm)
    y_c, q_c, k_c, v_c = _in_even(as_seq(ctx), mod[0], ctx_row, g1, w_in0, qg, kg, hm, wc, None, tm)
    y_c, q_c = per_batch(y_c), per_batch(q_c)
    f_l = _fourier_latent(y_l, m1, m3, twc, tws)
    f_c = _fourier_ctx(y_c, mctx)
    a_l = _win_attention(q_l, k_l, v_l, k_c, v_c, ev_sink[0])
    a_c = _ctx_attention_even(q_c, k_c, v_c, ev_sink[0])
    y1, *ffn = _ctx_ffn_and_weights(as_seq(f_c), as_seq(a_c), as_seq(ctx), mod[0], ctx_row, g2, w_out0,
                                    ffn_w_gate, ffn_w_up, ffn_w_down)
    x1 = _out_ffn("even_latent", (f_l, a_l), x, mod[0], lat_row, g2, w_out0, 0, *ffn, 2 * tm)

    w_in1 = od_w_in[0]
    w_out1 = od_w_out[0].astype(BF16)
    g1 = norm1_g[1].reshape(1, D_MODEL)
    g2 = norm2_g[1].reshape(1, D_MODEL)
    qg, kg = lane_gain(od_q_norm[0]), lane_gain(od_k_norm[0])
    q_l, k_l, v_l = _in_odd(x1, mod[1], lat_row, g1, w_in1, qg, kg, hm, True, 2 * tm)
    k_c, v_c = _in_odd(y1, mod[1], ctx_row, g1, w_in1, qg, kg, hm, False, n_ctx)
    bias_tab = _na_bias_table(od_rel_bias[0], _na_bias_index())
    a_l = _neighbourhood_attention(q_l, k_l, v_l, k_c, v_c, bias_tab)
    return _out_ffn("odd", (a_l,), x1, mod[1], lat_row, g2, w_out1, 1, *ffn, 2 * tm)
```

```python
import functools
import math

import numpy as np
import jax
import jax.numpy as jnp
from jax import lax
from jax.experimental import pallas as pl
from jax.experimental.pallas import tpu as pltpu

D_MODEL = 1024
BATCH = 4
SEQ = 4096
DEPTH = 2
GRID_W = 64
CTX_LEN = 256
HEAD_DIM = 64
EPS = 1e-6
FOURIER_WIDTH = D_MODEL // 2
FOURIER_GROUPS = 4
FOURIER_GROUP_CH = FOURIER_WIDTH // FOURIER_GROUPS
WIN_Q_HEADS = (D_MODEL // 2) // HEAD_DIM
WIN_KV_HEADS = 2
WIN_GROUP = WIN_Q_HEADS // WIN_KV_HEADS
WIN_RADIUS = 128
WIN_BLOCK = 128
QW = WIN_Q_HEADS * HEAD_DIM
KW = WIN_KV_HEADS * HEAD_DIM
EV_IN_WIDTH = FOURIER_WIDTH + QW + 2 * KW
NA_HEADS = D_MODEL // HEAD_DIM
NA_KH = 8
NA_KW = 16
NA_WIDTH = NA_HEADS * HEAD_DIM
ROPE_THETA = 10000.0
ROPE_FREQS = HEAD_DIM // 4
D_FF = ((8 * D_MODEL // 3 + 255) // 256) * 256
GRID_ROWS = SEQ // GRID_W

LANES = 128
MOD_ROWS = 8
CTX_MOD_ROW = BATCH
FFT_N = 64
FF_CHUNK = 256
SOFTMAX_ROWS = 64
RMS_W = 256
SUB_ROWS = 512
FFN_SUB_ROWS = 1024
VMEM_LIMIT = 60 * 1024 * 1024

LOG2E = math.log2(math.e)
QK_SCALE = LOG2E / math.sqrt(HEAD_DIM)

BF16 = jnp.bfloat16
F32 = jnp.float32

assert DEPTH == 2 and SEQ == FFT_N * FFT_N and D_FF % FF_CHUNK == 0


def _params(*sem):
    return pltpu.CompilerParams(dimension_semantics=sem, vmem_limit_bytes=VMEM_LIMIT)


def _dot(a, b):
    return jnp.dot(a, b, preferred_element_type=F32)


def _dot_nt(a, b):
    return lax.dot_general(a, b, (((1,), (1,)), ((), ())), preferred_element_type=F32)


def _silu(x):
    return x / (1.0 + jnp.exp(-x))


def _const_spec(shape):
    nd = len(shape)
    return pl.BlockSpec(shape, lambda *_: (0,) * nd, pipeline_mode=pl.Buffered(1))


def _dft_cos_sin(n):
    idx = (np.arange(n)[:, None] * np.arange(n)[None, :]) % n
    ang = 2.0 * np.pi * idx / n
    return np.cos(ang), np.sin(ang)


def _fourier_tables():
    cc, sc = _dft_cos_sin(FOURIER_GROUP_CH)
    wc = np.concatenate([cc, -sc], axis=1) / math.sqrt(FOURIER_GROUP_CH)
    c64, s64 = _dft_cos_sin(FFT_N)
    m1 = np.concatenate([c64, -s64], axis=0) / math.sqrt(FFT_N)
    m3 = np.concatenate([c64, s64], axis=1) / math.sqrt(FFT_N)
    tw = (np.arange(FFT_N)[:, None] * np.arange(FFT_N)[None, :]) % SEQ
    tw = 2.0 * np.pi * tw / SEQ
    twc = np.repeat(np.cos(tw)[:, :, None], LANES, axis=2)
    tws = np.repeat(np.sin(tw)[:, :, None], LANES, axis=2)
    cx, sx = _dft_cos_sin(CTX_LEN)
    mctx = np.concatenate([cx, sx], axis=1) / math.sqrt(CTX_LEN)
    as32 = lambda a: jnp.asarray(a, F32)
    return (as32(wc).astype(BF16), as32(m1).astype(BF16), as32(m3).astype(BF16),
            as32(twc), as32(tws), as32(mctx).astype(BF16))


def _head_mean_matrix():
    blk = np.kron(np.eye(RMS_W // HEAD_DIM), np.ones((HEAD_DIM, HEAD_DIM))) / HEAD_DIM
    return jnp.asarray(blk, BF16)


def _rope_tables():
    t = np.arange(SEQ)
    row = (t // GRID_W).astype(np.float32)
    col = (t % GRID_W).astype(np.float32)
    inv = np.float32(ROPE_THETA) ** (-np.arange(ROPE_FREQS, dtype=np.float32) / np.float32(ROPE_FREQS))
    ang_row = (row[:, None] * inv[None, :]).astype(np.float32).astype(np.float64)
    ang_col = (col[:, None] * inv[None, :]).astype(np.float32).astype(np.float64)
    zero = np.zeros_like(ang_row)
    cos = np.concatenate([np.cos(ang_row)] * 2 + [np.cos(ang_col)] * 2, axis=1)
    sin_hi = np.concatenate([-np.sin(ang_row), zero, -np.sin(ang_col), zero], axis=1)
    sin_lo = np.concatenate([zero, np.sin(ang_row), zero, np.sin(ang_col)], axis=1)
    rep = LANES // HEAD_DIM
    return tuple(jnp.asarray(np.tile(a, (1, rep)), F32) for a in (cos, sin_hi, sin_lo))


def _na_bias_index():
    cq = np.arange(GRID_W)
    dc = np.clip(cq[None, :] - cq[:, None] + NA_KW - 1, 0, 2 * NA_KW - 2)
    return jnp.asarray(np.concatenate([dc, dc + LANES // 2], axis=1), jnp.int32)


def _mod_kernel(cs_ref, w_ref, b_ref, o_ref):
    s = _silu(cs_ref[...]).astype(BF16)
    o_ref[0] = _dot(s, w_ref[0].astype(BF16)) + b_ref[0]


def _modulation(cs, ada_w, ada_b):
    tn = 1536
    return pl.pallas_call(
        _mod_kernel,
        grid=(DEPTH, 6 * D_MODEL // tn),
        in_specs=[pl.BlockSpec((MOD_ROWS, D_MODEL), lambda i, j: (0, 0)),
                  pl.BlockSpec((1, D_MODEL, tn), lambda i, j: (i, 0, j)),
                  pl.BlockSpec((1, 1, tn), lambda i, j: (i, 0, j))],
        out_specs=pl.BlockSpec((1, MOD_ROWS, tn), lambda i, j: (i, 0, j)),
        out_shape=jax.ShapeDtypeStruct((DEPTH, MOD_ROWS, 6 * D_MODEL), F32),
        compiler_params=_params("arbitrary", "arbitrary"),
        name="ada_modulation",
    )(cs, ada_w, ada_b.reshape(DEPTH, 1, 6 * D_MODEL))


def _mod_slice(mod_ref, k):
    return mod_ref[0, :, k * D_MODEL:(k + 1) * D_MODEL]


def _rms_mod(x, g, scale, shift):
    y = x * lax.rsqrt(jnp.mean(x * x, axis=-1, keepdims=True) + EPS)
    return (y * g) * (1.0 + scale) + shift


def _head_rms(t, gain_ref, hm_ref):
    w = t.shape[1]
    ms = _dot((t * t).astype(BF16), hm_ref[:w, :w])
    return t * lax.rsqrt(ms + EPS) * gain_ref[:, :w]


def _rope(t, cos, sin_hi, sin_lo):
    up = pltpu.roll(t, LANES - ROPE_FREQS, axis=1)
    dn = pltpu.roll(t, ROPE_FREQS, axis=1)
    return t * cos + up * sin_hi + dn * sin_lo


def _cast_once(w32_ref, w_ref):
    @pl.when((pl.program_id(0) == 0) & (pl.program_id(1) == 0))
    def _():
        w_ref[...] = w32_ref[...].astype(BF16)


def _in_even_kernel(latent, x_ref, mod_ref, g_ref, w32_ref, qg_ref, kg_ref, hm_ref, wc_ref, *rest):
    if latent:
        cos_ref, sh_ref, sl_ref, y_ref, q_ref, k_ref, v_ref, w_ref = rest
    else:
        y_ref, q_ref, k_ref, v_ref, w_ref = rest
    _cast_once(w32_ref, w_ref)
    F = FOURIER_WIDTH
    lo_lanes = lax.broadcasted_iota(jnp.int32, (1, LANES), 1) < HEAD_DIM

    for r in range(x_ref.shape[1] // SUB_ROWS):
        rows = slice(r * SUB_ROWS, (r + 1) * SUB_ROWS)
        h = _rms_mod(x_ref[0, rows], g_ref[...], _mod_slice(mod_ref, 1), _mod_slice(mod_ref, 0)).astype(BF16)
        f = _dot(h, w_ref[:, :F]).astype(BF16)
        for g in range(FOURIER_GROUPS):
            yg = _dot(f[:, g * LANES:(g + 1) * LANES], wc_ref[...])
            y_ref[0, rows, g * LANES:(g + 1) * LANES] = yg[:, :LANES].astype(y_ref.dtype)
            y_ref[0, rows, F + g * LANES:F + (g + 1) * LANES] = yg[:, LANES:].astype(y_ref.dtype)

        def qk(t, gain_ref):
            t = _head_rms(t, gain_ref, hm_ref)
            if latent:
                t = jnp.concatenate([_rope(t[:, i * LANES:(i + 1) * LANES], cos_ref[rows], sh_ref[rows], sl_ref[rows])
                                     for i in range(t.shape[1] // LANES)], axis=1)
            return t

        q = _dot(h, w_ref[:, F:F + QW])
        for j in range(QW // RMS_W):
            qj = qk(q[:, j * RMS_W:(j + 1) * RMS_W], qg_ref) * QK_SCALE
            q_ref[0, rows, j * RMS_W:(j + 1) * RMS_W] = qj.astype(BF16)
        kv = _dot(h, w_ref[:, F + QW:])

        def store_dup(t, o_ref):
            sw = pltpu.roll(t, HEAD_DIM, axis=1)
            o_ref[0, 0, rows] = jnp.where(lo_lanes, t, sw).astype(BF16)
            o_ref[0, 1, rows] = jnp.where(lo_lanes, sw, t).astype(BF16)

        store_dup(qk(kv[:, :KW], kg_ref), k_ref)
        store_dup(kv[:, KW:], v_ref)


def _in_even(x, mod, mod_row, g, w_in, q_g, k_g, hm, wc, rope_tabs, tm):
    B, L, _ = x.shape
    latent = rope_tabs is not None
    in_specs = [pl.BlockSpec((1, tm, D_MODEL), lambda b, t: (b, t, 0)),
                pl.BlockSpec((1, 1, 6 * D_MODEL), lambda b, t: (mod_row(b), 0, 0)),
                _const_spec((1, D_MODEL)),
                _const_spec((D_MODEL, EV_IN_WIDTH)),
                _const_spec((1, RMS_W)), _const_spec((1, RMS_W)),
                _const_spec((RMS_W, RMS_W)), _const_spec((LANES, 2 * LANES))]
    args = [x, mod, g, w_in, q_g, k_g, hm, wc]
    tok = lambda w: pl.BlockSpec((1, tm, w), lambda b, t: (b, t, 0))
    kv_spec = pl.BlockSpec((1, WIN_KV_HEADS, tm, LANES), lambda b, t: (b, 0, t, 0))
    if latent:
        in_specs += [pl.BlockSpec((tm, LANES), lambda b, t: (t, 0))] * 3
        args += list(rope_tabs)
    return pl.pallas_call(
        functools.partial(_in_even_kernel, latent),
        grid=(B, L // tm),
        in_specs=in_specs,
        out_specs=[tok(2 * FOURIER_WIDTH), tok(QW), kv_spec, kv_spec],
        out_shape=[jax.ShapeDtypeStruct((B, L, 2 * FOURIER_WIDTH), F32 if latent else BF16),
                   jax.ShapeDtypeStruct((B, L, QW), BF16),
                   jax.ShapeDtypeStruct((B, WIN_KV_HEADS, L, LANES), BF16),
                   jax.ShapeDtypeStruct((B, WIN_KV_HEADS, L, LANES), BF16)],
        compiler_params=_params("arbitrary", "arbitrary"),
        scratch_shapes=[pltpu.VMEM((D_MODEL, EV_IN_WIDTH), BF16)],
        name="in_even_latent" if latent else "in_even_ctx",
    )(*args)


def _fft_stage1_kernel(n2t, y_ref, m1_ref, twc_ref, tws_ref, o_ref, row_ref):
    F = FOURIER_WIDTH
    for j in range(n2t):
        row_ref[j % 2] = y_ref[0, :, j, :]
        a = _dot(m1_ref[...], row_ref[j % 2].astype(BF16))
        top, bot = a[:FFT_N], a[FFT_N:]
        ar = top[:, :F] - bot[:, F:]
        ai = top[:, F:] + bot[:, :F]
        tc = jnp.tile(twc_ref[j], (1, F // LANES))
        ts = jnp.tile(tws_ref[j], (1, F // LANES))
        o_ref[0, 0, :, j * F:(j + 1) * F] = (ar * tc + ai * ts).astype(BF16)
        o_ref[0, 1, :, j * F:(j + 1) * F] = (ai * tc - ar * ts).astype(BF16)


def _fft_stage2_kernel(k1t, b_ref, m3_ref, o_ref):
    for j in range(k1t):
        rhs = jnp.concatenate([b_ref[0, 0, j], b_ref[0, 1, j]], axis=0)
        o_ref[0, j] = _dot(m3_ref[...], rhs)


def _fourier_latent(y, m1, m3, twc, tws):
    B = y.shape[0]
    F = FOURIER_WIDTH
    n2t = 16
    yv = y.reshape(B, FFT_N, FFT_N, 2 * F)
    bh = pl.pallas_call(
        functools.partial(_fft_stage1_kernel, n2t),
        grid=(B, FFT_N // n2t),
        in_specs=[pl.BlockSpec((1, FFT_N, n2t, 2 * F), lambda b, t: (b, 0, t, 0)),
                  _const_spec((2 * FFT_N, FFT_N)),
                  pl.BlockSpec((n2t, FFT_N, LANES), lambda b, t: (t, 0, 0)),
                  pl.BlockSpec((n2t, FFT_N, LANES), lambda b, t: (t, 0, 0))],
        out_specs=pl.BlockSpec((1, 2, FFT_N, n2t * F), lambda b, t: (b, 0, 0, t)),
        out_shape=jax.ShapeDtypeStruct((B, 2, FFT_N, FFT_N * F), BF16),
        scratch_shapes=[pltpu.VMEM((2, FFT_N, 2 * F), F32)],
        compiler_params=_params("arbitrary", "arbitrary"),
        name="fft_stage1",
    )(yv, m1, twc, tws)
    k1t = 16
    bv = bh.reshape(B, 2, FFT_N, FFT_N, F)
    return pl.pallas_call(
        functools.partial(_fft_stage2_kernel, k1t),
        grid=(B, FFT_N // k1t),
        in_specs=[pl.BlockSpec((1, 2, k1t, FFT_N, F), lambda b, t: (b, 0, t, 0, 0)),
                  _const_spec((FFT_N, 2 * FFT_N))],
        out_specs=pl.BlockSpec((1, k1t, FFT_N, F), lambda b, t: (b, t, 0, 0)),
        out_shape=jax.ShapeDtypeStruct((B, FFT_N, FFT_N, F), F32),
        compiler_params=_params("arbitrary", "arbitrary"),
        name="fft_stage2",
    )(bv, m3)


def _fourier_ctx_kernel(y_ref, m_ref, o_ref):
    F = FOURIER_WIDTH
    rhs = jnp.concatenate([y_ref[0, :, :F], y_ref[0, :, F:]], axis=0)
    o_ref[0] = _dot(m_ref[...], rhs)


def _fourier_ctx(y, mctx):
    B = y.shape[0]
    return pl.pallas_call(
        _fourier_ctx_kernel,
        grid=(B,),
        in_specs=[pl.BlockSpec((1, CTX_LEN, 2 * FOURIER_WIDTH), lambda b: (b, 0, 0)),
                  _const_spec((CTX_LEN, 2 * CTX_LEN))],
        out_specs=pl.BlockSpec((1, CTX_LEN, FOURIER_WIDTH), lambda b: (b, 0, 0)),
        out_shape=jax.ShapeDtypeStruct((B, CTX_LEN, FOURIER_WIDTH), F32),
        compiler_params=_params("arbitrary"),
        name="fourier_ctx",
    )(y, mctx)


def _gqa_scores(q_slabs, keys, rows, s_ref):
    lo_lanes = lax.broadcasted_iota(jnp.int32, (rows, LANES), 1) < HEAD_DIM
    zero = jnp.zeros((rows, LANES), BF16)
    qs = jnp.concatenate([jnp.where(lo_lanes if half == 0 else ~lo_lanes, qslab, zero)
                          for qslab in q_slabs for half in range(2)], axis=0)
    s_ref[...] = _dot_nt(qs, keys)


def _gqa_attend(vals, masks, sinks, rows, out_slab, first_slab, scratch):
    s_ref, p_ref, inv_ref = scratch
    lo_lanes = lax.broadcasted_iota(jnp.int32, (rows, LANES), 1) < HEAD_DIM
    for c in range(WIN_GROUP * rows // SOFTMAX_ROWS):
        rs = slice(c * SOFTMAX_ROWS, (c + 1) * SOFTMAX_ROWS)
        s = s_ref[rs, :]
        if masks is not None:
            band, w = masks
            r0 = (c * SOFTMAX_ROWS) % rows
            s = jnp.concatenate([s[:, :w] + band[r0:r0 + SOFTMAX_ROWS], s[:, w:]], axis=1)
        sk = sinks[(c * SOFTMAX_ROWS) // rows]
        m = jnp.maximum(jnp.max(s, axis=-1, keepdims=True), sk)
        e = jnp.exp2(s - m)
        inv_ref[rs, :] = 1.0 / (jnp.sum(e, axis=-1, keepdims=True) + jnp.exp2(sk - m))
        p_ref[rs, :] = e.astype(BF16)
    o = _dot(p_ref[...], vals) * inv_ref[...]
    for i in range(WIN_GROUP // 2):
        a = o[(2 * i) * rows:(2 * i + 1) * rows]
        b = o[(2 * i + 1) * rows:(2 * i + 2) * rows]
        sl = slice((first_slab + i) * LANES, (first_slab + i + 1) * LANES)
        out_slab(sl)[...] = jnp.where(lo_lanes, a, b).astype(BF16)


def _gqa_scratch(rows, n_keys):
    one = [pltpu.VMEM((WIN_GROUP * rows, n_keys), F32), pltpu.VMEM((WIN_GROUP * rows, n_keys), BF16),
           pltpu.VMEM((WIN_GROUP * rows, 1), F32)]
    return one * WIN_KV_HEADS


def _group_sinks(sink_ref, kv):
    return [sink_ref[kv * WIN_GROUP + g] * LOG2E for g in range(WIN_GROUP)]


def _group_slab(kv, i):
    return slice((kv * (WIN_GROUP // 2) + i) * LANES, (kv * (WIN_GROUP // 2) + i + 1) * LANES)


def _window_bands():
    i = np.arange(WIN_BLOCK)[:, None]
    j = np.arange(3 * WIN_BLOCK)[None, :]
    bands = [np.where(np.abs(j - off - i) <= WIN_RADIUS, 0.0, -np.inf) for off in (0, WIN_BLOCK, 2 * WIN_BLOCK)]
    return jnp.asarray(np.stack(bands).astype(np.float32))


def _win_attn_kernel(blocks_per_step, sink_ref, band_ref, q_ref, k_ref, v_ref, kx_ref, vx_ref, o_ref,
                     s0_ref, s1_ref, p_ref, inv_ref):
    t = pl.program_id(1)
    nb = SEQ // WIN_BLOCK
    nw = 3 * WIN_BLOCK

    def place(j):
        n = t * blocks_per_step + j
        start = pl.multiple_of(jnp.clip((n - 1) * WIN_BLOCK, 0, SEQ - nw), WIN_BLOCK)
        sel = jnp.where(n == 0, 0, jnp.where(n == nb - 1, 2, 1))
        return pl.multiple_of(j * WIN_BLOCK, WIN_BLOCK), start, sel

    def scores(j, kv, s_ref):
        qoff, start, _ = place(j)
        keys = jnp.concatenate([k_ref[0, kv, pl.ds(start, nw), :], kx_ref[0, kv]], axis=0)
        q_slabs = [q_ref[0, pl.ds(qoff, WIN_BLOCK), _group_slab(kv, i)] for i in range(WIN_GROUP // 2)]
        _gqa_scores(q_slabs, keys, WIN_BLOCK, s_ref)

    def attend(j, kv, s_ref):
        qoff, start, sel = place(j)
        vals = jnp.concatenate([v_ref[0, kv, pl.ds(start, nw), :], vx_ref[0, kv]], axis=0)
        _gqa_attend(vals, (band_ref[sel], nw), _group_sinks(sink_ref, kv), WIN_BLOCK,
                    lambda sl: o_ref.at[0, pl.ds(qoff, WIN_BLOCK), sl], kv * (WIN_GROUP // 2),
                    (s_ref, p_ref, inv_ref))

    scores(0, 0, s0_ref)

    def one_block(j, carry):
        scores(j, 1, s1_ref)
        attend(j, 0, s0_ref)
        scores(jnp.minimum(j + 1, blocks_per_step - 1), 0, s0_ref)
        attend(j, 1, s1_ref)
        return carry

    lax.fori_loop(0, blocks_per_step, one_block, 0)


def _win_attention(q, k, v, kx, vx, sink):
    B, L, _ = q.shape
    blocks_per_step = 16
    rows = blocks_per_step * WIN_BLOCK
    n_keys = 3 * WIN_BLOCK + CTX_LEN
    full = pl.BlockSpec((1, WIN_KV_HEADS, L, LANES), lambda b, t: (b, 0, 0, 0))
    ctx_spec = pl.BlockSpec((1, WIN_KV_HEADS, CTX_LEN, LANES), lambda b, t: (0, 0, b, 0))
    tok = pl.BlockSpec((1, rows, QW), lambda b, t: (b, t, 0))
    return pl.pallas_call(
        functools.partial(_win_attn_kernel, blocks_per_step),
        grid=(B, L // rows),
        in_specs=[pl.BlockSpec(memory_space=pltpu.SMEM),
                  _const_spec((3, WIN_BLOCK, 3 * WIN_BLOCK)),
                  tok, full, full, ctx_spec, ctx_spec],
        out_specs=tok,
        out_shape=jax.ShapeDtypeStruct((B, L, QW), BF16),
        scratch_shapes=[pltpu.VMEM((WIN_GROUP * WIN_BLOCK, n_keys), F32), pltpu.VMEM((WIN_GROUP * WIN_BLOCK, n_keys), F32),
                        pltpu.VMEM((WIN_GROUP * WIN_BLOCK, n_keys), BF16), pltpu.VMEM((WIN_GROUP * WIN_BLOCK, 1), F32)],
        compiler_params=_params("arbitrary", "arbitrary"),
        name="window_attention",
    )(sink, _window_bands(), q, k, v, kx, vx)


def _ctx_attn_even_kernel(sink_ref, q_ref, k_ref, v_ref, o_ref, *scratch):
    for b in range(q_ref.shape[0]):
        tokens = slice(b * CTX_LEN, (b + 1) * CTX_LEN)
        for kv in range(WIN_KV_HEADS):
            unit = scratch[3 * (b * WIN_KV_HEADS + kv):3 * (b * WIN_KV_HEADS + kv) + 3]
            q_slabs = [q_ref[b, :, _group_slab(kv, i)] for i in range(WIN_GROUP // 2)]
            _gqa_scores(q_slabs, k_ref[0, kv, tokens, :], CTX_LEN, unit[0])
            _gqa_attend(v_ref[0, kv, tokens, :], None, _group_sinks(sink_ref, kv), CTX_LEN,
                        lambda sl, b=b: o_ref.at[b, :, sl], kv * (WIN_GROUP // 2), unit)


def _ctx_attention_even(q, k, v, sink):
    B = q.shape[0]
    whole = lambda a: pl.BlockSpec(a.shape, lambda i: (0,) * a.ndim)
    return pl.pallas_call(
        _ctx_attn_even_kernel,
        grid=(1,),
        in_specs=[pl.BlockSpec(memory_space=pltpu.SMEM), whole(q), whole(k), whole(v)],
        out_specs=whole(q),
        out_shape=jax.ShapeDtypeStruct((B, CTX_LEN, QW), BF16),
        scratch_shapes=_gqa_scratch(CTX_LEN, CTX_LEN) * B,
        compiler_params=_params("arbitrary"),
        name="ctx_attention_even",
    )(sink, q, k, v)


def _out_ffn_kernel(mode, *refs):
    if mode == "odd":
        a_ref, x_ref, mod_ref, g_ref, wo_ref, wg_ref, wu_ref, wd_ref, o_ref = refs
    else:
        f_ref, a_ref, x_ref, mod_ref, g_ref, wo_ref, wg_ref, wu_ref, wd_ref, o_ref = refs
    for r in range(x_ref.shape[1] // FFN_SUB_ROWS):
        rows = slice(r * FFN_SUB_ROWS, (r + 1) * FFN_SUB_ROWS)
        if mode == "even_latent":
            planes = range(r * FFN_SUB_ROWS // FFT_N, (r + 1) * FFN_SUB_ROWS // FFT_N)
            fm = jnp.concatenate([f_ref[0, :, j, :] for j in planes], axis=0).astype(BF16)
            o = _dot(fm, wo_ref[:FOURIER_WIDTH]) + _dot(a_ref[0, rows], wo_ref[FOURIER_WIDTH:])
        else:
            o = _dot(jnp.concatenate([a_ref[0, j, rows] for j in range(a_ref.shape[1])], axis=1), wo_ref[...])
        x1 = x_ref[0, rows] + _mod_slice(mod_ref, 2) * o
        h = _rms_mod(x1, g_ref[...], _mod_slice(mod_ref, 4), _mod_slice(mod_ref, 3)).astype(BF16)
        acc = jnp.zeros_like(x1)
        for c in range(D_FF // FF_CHUNK):
            cs = slice(c * FF_CHUNK, (c + 1) * FF_CHUNK)
            a = _silu(_dot(h, wg_ref[0, :, cs])) * _dot(h, wu_ref[0, :, cs])
            acc = acc + _dot(a.astype(BF16), wd_ref[0, cs, :])
        o_ref[0, rows] = x1 + _mod_slice(mod_ref, 5) * acc


def _out_ffn(mode, mix, x, mod, mod_row, g, w_out, layer, wg, wu, wd, tm):
    B, L, _ = x.shape
    tok = lambda w: pl.BlockSpec((1, tm, w), lambda b, t: (b, t, 0))
    ffn_spec = lambda r, c: pl.BlockSpec((1, r, c), lambda b, t: (layer, 0, 0), pipeline_mode=pl.Buffered(1))
    assert mode in ("even_latent", "odd") and tm % FFN_SUB_ROWS == 0
    if mode == "even_latent":
        mix_specs = [pl.BlockSpec((1, FFT_N, tm // FFT_N, FOURIER_WIDTH), lambda b, t: (b, 0, t, 0)), tok(QW)]
    else:
        mix_specs = [pl.BlockSpec((1, NA_WIDTH // LANES, tm, LANES), lambda b, t: (b, 0, t, 0))]
    return pl.pallas_call(
        functools.partial(_out_ffn_kernel, mode),
        grid=(B, L // tm),
        in_specs=mix_specs + [tok(D_MODEL),
                              pl.BlockSpec((1, 1, 6 * D_MODEL), lambda b, t: (mod_row(b), 0, 0)),
                              _const_spec((1, D_MODEL)),
                              _const_spec((D_MODEL, D_MODEL)),
                              ffn_spec(D_MODEL, D_FF), ffn_spec(D_MODEL, D_FF), ffn_spec(D_FF, D_MODEL)],
        out_specs=tok(D_MODEL),
        out_shape=jax.ShapeDtypeStruct((B, L, D_MODEL), F32),
        compiler_params=_params("arbitrary", "arbitrary"),
        name="out_ffn_" + mode,
    )(*mix, x, mod, g, w_out, wg, wu, wd)


def _ctx_ffn_kernel(f_ref, a_ref, x_ref, mod_ref, g_ref, wo_ref, wg32_ref, wu32_ref, wd32_ref,
                    o_ref, wg_ref, wu_ref, wd_ref, x1_ref, h_ref, acc_ref):
    c = pl.program_id(0)
    wg_ref[...] = wg32_ref[...].astype(BF16)
    wu_ref[...] = wu32_ref[...].astype(BF16)
    wd_ref[...] = wd32_ref[...].astype(BF16)

    @pl.when(c == 0)
    def _():
        o = _dot(f_ref[0].astype(BF16), wo_ref[:FOURIER_WIDTH]) + _dot(a_ref[0], wo_ref[FOURIER_WIDTH:])
        x1 = x_ref[0] + _mod_slice(mod_ref, 2) * o
        x1_ref[...] = x1
        h_ref[...] = _rms_mod(x1, g_ref[...], _mod_slice(mod_ref, 4), _mod_slice(mod_ref, 3)).astype(BF16)
        acc_ref[...] = jnp.zeros_like(acc_ref)

    h = h_ref[...]
    a = _silu(_dot(h, wg_ref[0])) * _dot(h, wu_ref[0])
    acc_ref[...] += _dot(a.astype(BF16), wd_ref[0])

    @pl.when(c == pl.num_programs(0) - 1)
    def _():
        o_ref[0] = x1_ref[...] + _mod_slice(mod_ref, 5) * acc_ref[...]


def _ctx_ffn_and_weights(f, a, x, mod, mod_row, g, w_out, wg32, wu32, wd32):
    _, n, _ = x.shape
    nc = D_FF // FF_CHUNK
    whole = lambda w: pl.BlockSpec((1, n, w), lambda c: (0, 0, 0))
    col_chunk = pl.BlockSpec((DEPTH, D_MODEL, FF_CHUNK), lambda c: (0, 0, c))
    row_chunk = pl.BlockSpec((DEPTH, FF_CHUNK, D_MODEL), lambda c: (0, c, 0))
    return pl.pallas_call(
        _ctx_ffn_kernel,
        grid=(nc,),
        in_specs=[whole(FOURIER_WIDTH), whole(QW), whole(D_MODEL),
                  pl.BlockSpec((1, 1, 6 * D_MODEL), lambda c: (mod_row(0), 0, 0)),
                  pl.BlockSpec((1, D_MODEL), lambda c: (0, 0)),
                  pl.BlockSpec((D_MODEL, D_MODEL), lambda c: (0, 0)),
                  col_chunk, col_chunk, row_chunk],
        out_specs=[whole(D_MODEL), col_chunk, col_chunk, row_chunk],
        out_shape=[jax.ShapeDtypeStruct((1, n, D_MODEL), F32),
                   jax.ShapeDtypeStruct((DEPTH, D_MODEL, D_FF), BF16),
                   jax.ShapeDtypeStruct((DEPTH, D_MODEL, D_FF), BF16),
                   jax.ShapeDtypeStruct((DEPTH, D_FF, D_MODEL), BF16)],
        scratch_shapes=[pltpu.VMEM((n, D_MODEL), F32), pltpu.VMEM((n, D_MODEL), BF16), pltpu.VMEM((n, D_MODEL), F32)],
        compiler_params=_params("arbitrary"),
        name="ctx_ffn_and_weights",
    )(f, a, x, mod, g, w_out, wg32, wu32, wd32)


def _in_odd_kernel(with_q, x_ref, xn_ref, mod_ref, g_ref, w32_ref, qg_ref, kg_ref, hm_ref, *outs):
    W = NA_WIDTH
    slabs = RMS_W // LANES
    if with_q:
        q_ref, k_ref, v_ref, w_ref, hn_ref = outs
    else:
        k_ref, v_ref, w_ref, hn_ref = outs
    _cast_once(w32_ref, w_ref)
    norm = lambda xs: _rms_mod(xs, g_ref[...], _mod_slice(mod_ref, 1), _mod_slice(mod_ref, 0)).astype(BF16)

    @pl.when(pl.program_id(1) == 0)
    def _():
        hn_ref[...] = norm(x_ref[0, :SUB_ROWS])

    n_sub = x_ref.shape[1] // SUB_ROWS
    for r in range(n_sub):
        rows = slice(r * SUB_ROWS, (r + 1) * SUB_ROWS)
        h = hn_ref[...] if r == 0 else norm(x_ref[0, rows])
        if r == n_sub - 1:
            hn_ref[...] = norm(xn_ref[0])

        def normed(t, gain_ref, scale, o_ref):
            for j in range(W // RMS_W):
                tj = (_head_rms(t[:, j * RMS_W:(j + 1) * RMS_W], gain_ref, hm_ref) * scale).astype(BF16)
                for i in range(slabs):
                    o_ref[0, j * slabs + i, rows] = tj[:, i * LANES:(i + 1) * LANES]

        if with_q:
            normed(_dot(h, w_ref[:, :W]), qg_ref, QK_SCALE, q_ref)
        normed(_dot(h, w_ref[:, W:2 * W]), kg_ref, 1.0, k_ref)
        v = _dot(h, w_ref[:, 2 * W:]).astype(BF16)
        for j in range(W // LANES):
            v_ref[0, j, rows] = v[:, j * LANES:(j + 1) * LANES]


def _in_odd(x, mod, mod_row, g, w_in, q_g, k_g, hm, with_q, tm):
    B, L, _ = x.shape
    pairs = NA_WIDTH // LANES
    tok = pl.BlockSpec((1, pairs, tm, LANES), lambda b, t: (b, 0, t, 0))
    n_out = 3 if with_q else 2
    x_next = pl.BlockSpec((1, SUB_ROWS, D_MODEL),
                          lambda b, t: (b, jnp.minimum((t + 1) * (tm // SUB_ROWS), L // SUB_ROWS - 1), 0))
    return pl.pallas_call(
        functools.partial(_in_odd_kernel, with_q),
        grid=(B, L // tm),
        in_specs=[pl.BlockSpec((1, tm, D_MODEL), lambda b, t: (b, t, 0)), x_next,
                  pl.BlockSpec((1, 1, 6 * D_MODEL), lambda b, t: (mod_row(b), 0, 0)),
                  _const_spec((1, D_MODEL)),
                  _const_spec((D_MODEL, 3 * NA_WIDTH)),
                  _const_spec((1, RMS_W)), _const_spec((1, RMS_W)), _const_spec((RMS_W, RMS_W))],
        out_specs=[tok] * n_out,
        out_shape=[jax.ShapeDtypeStruct((B, pairs, L, LANES), BF16)] * n_out,
        compiler_params=_params("arbitrary", "arbitrary"),
        scratch_shapes=[pltpu.VMEM((D_MODEL, 3 * NA_WIDTH), BF16), pltpu.VMEM((SUB_ROWS, D_MODEL), BF16)],
        name="in_odd" if with_q else "in_odd_ctx",
    )(x, x, mod, g, w_in, q_g, k_g, hm)


def _bias_table_kernel(rb_ref, idx_ref, o_ref):
    idx = idx_ref[...]
    cq = lax.broadcasted_iota(jnp.int32, idx.shape, 0)
    ck = lax.broadcasted_iota(jnp.int32, idx.shape, 1) % GRID_W
    c0 = jnp.clip(cq - NA_KW // 2, 0, GRID_W - NA_KW)
    inside = (ck >= c0) & (ck < c0 + NA_KW)
    for h in range(o_ref.shape[0]):
        for dr in range(o_ref.shape[1]):
            row = jnp.broadcast_to(rb_ref[h, dr:dr + 1, :], idx.shape)
            o_ref[h, dr] = jnp.where(inside, jnp.take_along_axis(row, idx, axis=1) * LOG2E, -jnp.inf)


def _na_bias_table(rel_bias, idx):
    H = NA_HEADS
    npair = 2 * NA_KH - 2
    half = LANES // 2
    hb = 8
    pad = lambda a: jnp.pad(a, ((0, 0), (0, 0), (0, half - a.shape[-1])))
    rb2 = jnp.concatenate([pad(rel_bias[:, :-1]), pad(rel_bias[:, 1:])], axis=-1)
    return pl.pallas_call(
        _bias_table_kernel,
        grid=(H // hb,),
        in_specs=[pl.BlockSpec((hb, npair, LANES), lambda h: (h, 0, 0)), _const_spec((GRID_W, LANES))],
        out_specs=pl.BlockSpec((hb, npair, GRID_W, LANES), lambda h: (h, 0, 0, 0)),
        out_shape=jax.ShapeDtypeStruct((H, npair, GRID_W, LANES), F32),
        compiler_params=_params("arbitrary"),
        name="na_bias_table",
    )(rb2, idx)


def _na_kernel(rows_per_step, q_ref, qn_ref, k_ref, v_ref, kx_ref, vx_ref, bias_ref, o_ref,
               s0_ref, s1_ref, p_ref, inv_ref):
    t = pl.program_id(1)
    n = NA_KH * GRID_W
    lo_lanes = lax.broadcasted_iota(jnp.int32, (GRID_W, LANES), 1) < HEAD_DIM

    def window(i):
        r = jnp.minimum(t * rows_per_step + i, GRID_ROWS - 1)
        r0 = jnp.clip(r - NA_KH // 2, 0, GRID_ROWS - NA_KH)
        qoff = jnp.minimum(i, rows_per_step - 1) * GRID_W
        return r0 - r + NA_KH - 1, pl.multiple_of(r0 * GRID_W, GRID_W), pl.multiple_of(qoff, GRID_W)

    def scores(i, s_ref):
        dr0, start, qoff = window(i)
        for hp in range(NA_HEADS // 2):
            qs = jnp.where(i < rows_per_step, q_ref[0, hp, pl.ds(qoff, GRID_W), :], qn_ref[0, hp])
            keys = jnp.concatenate([k_ref[0, hp, pl.ds(start, n), :], kx_ref[0, hp]], axis=0)
            qm = jnp.concatenate([jnp.where(lo_lanes, qs, jnp.zeros_like(qs)),
                                  jnp.where(lo_lanes, jnp.zeros_like(qs), qs)], axis=0)
            bias = jnp.concatenate(
                [jnp.concatenate([bias_ref[2 * hp + half, dr0 + 2 * p] for p in range(NA_KH // 2)], axis=1)
                 for half in range(2)], axis=0)
            rs = slice(2 * hp * GRID_W, (2 * hp + 2) * GRID_W)
            s_ref[rs, :n] = _dot_nt(qm, keys[:n]) + bias
            s_ref[rs, n:] = _dot_nt(qm, keys[n:])

    def attend(i, s_ref):
        _, start, qoff = window(i)
        for c in range(NA_HEADS * GRID_W // SOFTMAX_ROWS):
            cs = slice(c * SOFTMAX_ROWS, (c + 1) * SOFTMAX_ROWS)
            s = s_ref[cs, :]
            e = jnp.exp2(s - jnp.max(s, axis=-1, keepdims=True))
            inv_ref[cs, :] = 1.0 / jnp.sum(e, axis=-1, keepdims=True)
            p_ref[cs, :] = e.astype(BF16)
        for hp in range(NA_HEADS // 2):
            vals = jnp.concatenate([v_ref[0, hp, pl.ds(start, n), :], vx_ref[0, hp]], axis=0)
            rs = slice(2 * hp * GRID_W, (2 * hp + 2) * GRID_W)
            res = _dot(p_ref[rs, :], vals) * inv_ref[rs, :]
            o_ref[0, hp, pl.ds(qoff, GRID_W), :] = jnp.where(lo_lanes, res[:GRID_W], res[GRID_W:]).astype(BF16)

    @pl.when(t == 0)
    def _():
        scores(0, s0_ref)

    def two_rows(j, carry):
        scores(2 * j + 1, s1_ref)
        attend(2 * j, s0_ref)
        scores(2 * j + 2, s0_ref)
        attend(2 * j + 1, s1_ref)
        return carry

    lax.fori_loop(0, rows_per_step // 2, two_rows, 0)


def _neighbourhood_attention(q, k, v, kx, vx, bias_tab):
    B, pairs, L, _ = q.shape
    rows_per_step = 16
    full = pl.BlockSpec((1, pairs, L, LANES), lambda b, t: (b, 0, 0, 0))
    ctx_spec = pl.BlockSpec((1, pairs, CTX_LEN, LANES), lambda b, t: (0, 0, b, 0))
    q_rows = pl.BlockSpec((1, pairs, rows_per_step * GRID_W, LANES), lambda b, t: (b, 0, t, 0))
    q_next = pl.BlockSpec((1, pairs, GRID_W, LANES),
                          lambda b, t: (b, 0, jnp.minimum((t + 1) * rows_per_step, GRID_ROWS - 1), 0))
    n_keys = NA_KH * GRID_W + CTX_LEN
    all_rows = NA_HEADS * GRID_W
    return pl.pallas_call(
        functools.partial(_na_kernel, rows_per_step),
        grid=(B, GRID_ROWS // rows_per_step),
        in_specs=[q_rows, q_next, full, full, ctx_spec, ctx_spec, _const_spec(bias_tab.shape)],
        out_specs=q_rows,
        out_shape=jax.ShapeDtypeStruct((B, pairs, L, LANES), BF16),
        scratch_shapes=[pltpu.VMEM((all_rows, n_keys), F32), pltpu.VMEM((all_rows, n_keys), F32),
                        pltpu.VMEM((all_rows, n_keys), BF16), pltpu.VMEM((all_rows, 1), F32)],
        compiler_params=_params("arbitrary", "arbitrary"),
        name="neighbourhood_attention",
    )(q, q, k, v, kx, vx, bias_tab)


def kernel(x, c, ctx, c_ctx, ada_w, ada_b, norm1_g, norm2_g, ffn_w_gate, ffn_w_up, ffn_w_down,
           ev_w_in, ev_w_out, ev_q_norm, ev_k_norm, ev_sink,
           od_w_in, od_w_out, od_q_norm, od_k_norm, od_rel_bias):
    assert x.shape == (BATCH, SEQ, D_MODEL) and ctx.shape == (BATCH, CTX_LEN, D_MODEL)
    wc, m1, m3, twc, tws, mctx = _fourier_tables()
    hm = _head_mean_matrix()
    rope_tabs = _rope_tables()
    lane_gain = lambda gvec: jnp.tile(gvec, RMS_W // HEAD_DIM).reshape(1, RMS_W)
    lat_row = lambda b: b
    ctx_row = lambda b: CTX_MOD_ROW
    tm = 512
    n_ctx = BATCH * CTX_LEN
    as_seq = lambda a: a.reshape(1, n_ctx, a.shape[-1])
    per_batch = lambda a: a.reshape(BATCH, CTX_LEN, a.shape[-1])

    cs = jnp.concatenate([c, c_ctx[None, :], jnp.zeros((MOD_ROWS - BATCH - 1, D_MODEL), F32)], axis=0)
    mod = _modulation(cs, ada_w, ada_b).reshape(DEPTH, MOD_ROWS, 1, 6 * D_MODEL)

    w_in0 = ev_w_in[0]
    w_out0 = ev_w_out[0].astype(BF16)
    g1 = norm1_g[0].reshape(1, D_MODEL)
    g2 = norm2_g[0].reshape(1, D_MODEL)
    qg, kg = lane_gain(ev_q_norm[0]), lane_gain(ev_k_norm[0])
    y_l, q_l, k_l, v_l = _in_even(x, mod[0], lat_row, g1, w_in0, qg, kg, hm, wc, rope_tabs, 2 * tm)
    y_c, q_c, k_c, v_c = _in_even(as_seq(ctx), mod[0], ctx_row, g1, w_in0, qg, kg, hm, wc, None, tm)
    y_c, q_c = per_batch(y_c), per_batch(q_c)
    f_l = _fourier_latent(y_l, m1, m3, twc, tws)
    f_c = _fourier_ctx(y_c, mctx)
    a_l = _win_attention(q_l, k_l, v_l, k_c, v_c, ev_sink[0])
    a_c = _ctx_attention_even(q_c, k_c, v_c, ev_sink[0])
    y1, *ffn = _ctx_ffn_and_weights(as_seq(f_c), as_seq(a_c), as_seq(ctx), mod[0], ctx_row, g2, w_out0,
                                    ffn_w_gate, ffn_w_up, ffn_w_down)
    x1 = _out_ffn("even_latent", (f_l, a_l), x, mod[0], lat_row, g2, w_out0, 0, *ffn, 2 * tm)

    w_in1 = od_w_in[0]
    w_out1 = od_w_out[0].astype(BF16)
    g1 = norm1_g[1].reshape(1, D_MODEL)
    g2 = norm2_g[1].reshape(1, D_MODEL)
    qg, kg = lane_gain(od_q_norm[0]), lane_gain(od_k_norm[0])
    q_l, k_l, v_l = _in_odd(x1, mod[1], lat_row, g1, w_in1, qg, kg, hm, True, 2 * tm)
    k_c, v_c = _in_odd(y1, mod[1], ctx_row, g1, w_in1, qg, kg, hm, False, n_ctx)
    bias_tab = _na_bias_table(od_rel_bias[0], _na_bias_index())
    a_l = _neighbourhood_attention(q_l, k_l, v_l, k_c, v_c, bias_tab)
    return _out_ffn("odd", (a_l,), x1, mod[1], lat_row, g2, w_out1, 1, *ffn, 2 * tm)
```

```python
import functools
import math

import numpy as np
import jax
import jax.numpy as jnp
from jax import lax
from jax.experimental import pallas as pl
from jax.experimental.pallas import tpu as pltpu

D_MODEL = 1024
BATCH = 4
SEQ = 4096
DEPTH = 2
GRID_W = 64
CTX_LEN = 256
HEAD_DIM = 64
EPS = 1e-6
FOURIER_WIDTH = D_MODEL // 2
FOURIER_GROUPS = 4
FOURIER_GROUP_CH = FOURIER_WIDTH // FOURIER_GROUPS
WIN_Q_HEADS = (D_MODEL // 2) // HEAD_DIM
WIN_KV_HEADS = 2
WIN_GROUP = WIN_Q_HEADS // WIN_KV_HEADS
WIN_RADIUS = 128
WIN_BLOCK = 128
QW = WIN_Q_HEADS * HEAD_DIM
KW = WIN_KV_HEADS * HEAD_DIM
EV_IN_WIDTH = FOURIER_WIDTH + QW + 2 * KW
NA_HEADS = D_MODEL // HEAD_DIM
NA_KH = 8
NA_KW = 16
NA_WIDTH = NA_HEADS * HEAD_DIM
ROPE_THETA = 10000.0
ROPE_FREQS = HEAD_DIM // 4
D_FF = ((8 * D_MODEL // 3 + 255) // 256) * 256
GRID_ROWS = SEQ // GRID_W

LANES = 128
MOD_ROWS = 8
CTX_MOD_ROW = BATCH
FFT_N = 64
FF_CHUNK = 256
SOFTMAX_ROWS = 64
RMS_W = 256
SUB_ROWS = 512
FFN_SUB_ROWS = 1024
VMEM_LIMIT = 60 * 1024 * 1024

LOG2E = math.log2(math.e)
QK_SCALE = LOG2E / math.sqrt(HEAD_DIM)

BF16 = jnp.bfloat16
F32 = jnp.float32

assert DEPTH == 2 and SEQ == FFT_N * FFT_N and D_FF % FF_CHUNK == 0


def _params(*sem):
    return pltpu.CompilerParams(dimension_semantics=sem, vmem_limit_bytes=VMEM_LIMIT)


def _dot(a, b):
    return jnp.dot(a, b, preferred_element_type=F32)


def _dot_nt(a, b):
    return lax.dot_general(a, b, (((1,), (1,)), ((), ())), preferred_element_type=F32)


def _silu(x):
    return x / (1.0 + jnp.exp(-x))


def _const_spec(shape):
    nd = len(shape)
    return pl.BlockSpec(shape, lambda *_: (0,) * nd, pipeline_mode=pl.Buffered(1))


def _dft_cos_sin(n):
    idx = (np.arange(n)[:, None] * np.arange(n)[None, :]) % n
    ang = 2.0 * np.pi * idx / n
    return np.cos(ang), np.sin(ang)


def _fourier_tables():
    cc, sc = _dft_cos_sin(FOURIER_GROUP_CH)
    wc = np.concatenate([cc, -sc], axis=1) / math.sqrt(FOURIER_GROUP_CH)
    c64, s64 = _dft_cos_sin(FFT_N)
    m1 = np.concatenate([c64, -s64], axis=0) / math.sqrt(FFT_N)
    m3 = np.concatenate([c64, s64], axis=1) / math.sqrt(FFT_N)
    tw = (np.arange(FFT_N)[:, None] * np.arange(FFT_N)[None, :]) % SEQ
    tw = 2.0 * np.pi * tw / SEQ
    twc = np.repeat(np.cos(tw)[:, :, None], LANES, axis=2)
    tws = np.repeat(np.sin(tw)[:, :, None], LANES, axis=2)
    cx, sx = _dft_cos_sin(CTX_LEN)
    mctx = np.concatenate([cx, sx], axis=1) / math.sqrt(CTX_LEN)
    as32 = lambda a: jnp.asarray(a, F32)
    return (as32(wc).astype(BF16), as32(m1).astype(BF16), as32(m3).astype(BF16),
            as32(twc), as32(tws), as32(mctx).astype(BF16))


def _head_mean_matrix():
    blk = np.kron(np.eye(RMS_W // HEAD_DIM), np.ones((HEAD_DIM, HEAD_DIM))) / HEAD_DIM
    return jnp.asarray(blk, BF16)


def _rope_tables():
    t = np.arange(SEQ)
    row = (t // GRID_W).astype(np.float32)
    col = (t % GRID_W).astype(np.float32)
    inv = np.float32(ROPE_THETA) ** (-np.arange(ROPE_FREQS, dtype=np.float32) / np.float32(ROPE_FREQS))
    ang_row = (row[:, None] * inv[None, :]).astype(np.float32).astype(np.float64)
    ang_col = (col[:, None] * inv[None, :]).astype(np.float32).astype(np.float64)
    zero = np.zeros_like(ang_row)
    cos = np.concatenate([np.cos(ang_row)] * 2 + [np.cos(ang_col)] * 2, axis=1)
    sin_hi = np.concatenate([-np.sin(ang_row), zero, -np.sin(ang_col), zero], axis=1)
    sin_lo = np.concatenate([zero, np.sin(ang_row), zero, np.sin(ang_col)], axis=1)
    rep = LANES // HEAD_DIM
    return tuple(jnp.asarray(np.tile(a, (1, rep)), F32) for a in (cos, sin_hi, sin_lo))


def _na_bias_index():
    cq = np.arange(GRID_W)
    dc = np.clip(cq[None, :] - cq[:, None] + NA_KW - 1, 0, 2 * NA_KW - 2)
    return jnp.asarray(np.concatenate([dc, dc + LANES // 2], axis=1), jnp.int32)


def _mod_kernel(cs_ref, w_ref, b_ref, o_ref):
    s = _silu(cs_ref[...]).astype(BF16)
    o_ref[0] = _dot(s, w_ref[0].astype(BF16)) + b_ref[0]


def _modulation(cs, ada_w, ada_b):
    tn = 3072
    return pl.pallas_call(
        _mod_kernel,
        grid=(DEPTH, 6 * D_MODEL // tn),
        in_specs=[pl.BlockSpec((MOD_ROWS, D_MODEL), lambda i, j: (0, 0)),
                  pl.BlockSpec((1, D_MODEL, tn), lambda i, j: (i, 0, j)),
                  pl.BlockSpec((1, 1, tn), lambda i, j: (i, 0, j))],
        out_specs=pl.BlockSpec((1, MOD_ROWS, tn), lambda i, j: (i, 0, j)),
        out_shape=jax.ShapeDtypeStruct((DEPTH, MOD_ROWS, 6 * D_MODEL), F32),
        compiler_params=_params("arbitrary", "arbitrary"),
        name="ada_modulation",
    )(cs, ada_w, ada_b.reshape(DEPTH, 1, 6 * D_MODEL))


def _mod_slice(mod_ref, k):
    return mod_ref[0, :, k * D_MODEL:(k + 1) * D_MODEL]


def _rms_mod(x, g, scale, shift):
    y = x * lax.rsqrt(jnp.mean(x * x, axis=-1, keepdims=True) + EPS)
    return (y * g) * (1.0 + scale) + shift


def _head_rms(t, gain_ref, hm_ref):
    w = t.shape[1]
    ms = _dot((t * t).astype(BF16), hm_ref[:w, :w])
    return t * lax.rsqrt(ms + EPS) * gain_ref[:, :w]


def _rope(t, cos, sin_hi, sin_lo):
    up = pltpu.roll(t, LANES - ROPE_FREQS, axis=1)
    dn = pltpu.roll(t, ROPE_FREQS, axis=1)
    return t * cos + up * sin_hi + dn * sin_lo


def _cast_once(w32_ref, w_ref):
    @pl.when((pl.program_id(0) == 0) & (pl.program_id(1) == 0))
    def _():
        w_ref[...] = w32_ref[...].astype(BF16)


def _in_even_kernel(latent, x_ref, mod_ref, g_ref, w32_ref, qg_ref, kg_ref, hm_ref, wc_ref, *rest):
    if latent:
        cos_ref, sh_ref, sl_ref, y_ref, q_ref, k_ref, v_ref, w_ref = rest
    else:
        y_ref, q_ref, k_ref, v_ref, w_ref = rest
    _cast_once(w32_ref, w_ref)
    F = FOURIER_WIDTH
    lo_lanes = lax.broadcasted_iota(jnp.int32, (1, LANES), 1) < HEAD_DIM

    for r in range(x_ref.shape[1] // SUB_ROWS):
        rows = slice(r * SUB_ROWS, (r + 1) * SUB_ROWS)
        h = _rms_mod(x_ref[0, rows], g_ref[...], _mod_slice(mod_ref, 1), _mod_slice(mod_ref, 0)).astype(BF16)
        f = _dot(h, w_ref[:, :F]).astype(BF16)
        for g in range(FOURIER_GROUPS):
            yg = _dot(f[:, g * LANES:(g + 1) * LANES], wc_ref[...])
            y_ref[0, rows, g * LANES:(g + 1) * LANES] = yg[:, :LANES].astype(y_ref.dtype)
            y_ref[0, rows, F + g * LANES:F + (g + 1) * LANES] = yg[:, LANES:].astype(y_ref.dtype)

        def qk(t, gain_ref):
            t = _head_rms(t, gain_ref, hm_ref)
            if latent:
                t = jnp.concatenate([_rope(t[:, i * LANES:(i + 1) * LANES], cos_ref[rows], sh_ref[rows], sl_ref[rows])
                                     for i in range(t.shape[1] // LANES)], axis=1)
            return t

        q = _dot(h, w_ref[:, F:F + QW])
        for j in range(QW // RMS_W):
            qj = qk(q[:, j * RMS_W:(j + 1) * RMS_W], qg_ref) * QK_SCALE
            q_ref[0, rows, j * RMS_W:(j + 1) * RMS_W] = qj.astype(BF16)
        kv = _dot(h, w_ref[:, F + QW:])

        def store_dup(t, o_ref):
            sw = pltpu.roll(t, HEAD_DIM, axis=1)
            o_ref[0, 0, rows] = jnp.where(lo_lanes, t, sw).astype(BF16)
            o_ref[0, 1, rows] = jnp.where(lo_lanes, sw, t).astype(BF16)

        store_dup(qk(kv[:, :KW], kg_ref), k_ref)
        store_dup(kv[:, KW:], v_ref)


def _in_even(x, mod, mod_row, g, w_in, q_g, k_g, hm, wc, rope_tabs, tm):
    B, L, _ = x.shape
    latent = rope_tabs is not None
    in_specs = [pl.BlockSpec((1, tm, D_MODEL), lambda b, t: (b, t, 0)),
                pl.BlockSpec((1, 1, 6 * D_MODEL), lambda b, t: (mod_row(b), 0, 0)),
                _const_spec((1, D_MODEL)),
                _const_spec((D_MODEL, EV_IN_WIDTH)),
                _const_spec((1, RMS_W)), _const_spec((1, RMS_W)),
                _const_spec((RMS_W, RMS_W)), _const_spec((LANES, 2 * LANES))]
    args = [x, mod, g, w_in, q_g, k_g, hm, wc]
    tok = lambda w: pl.BlockSpec((1, tm, w), lambda b, t: (b, t, 0))
    kv_spec = pl.BlockSpec((1, WIN_KV_HEADS, tm, LANES), lambda b, t: (b, 0, t, 0))
    if latent:
        in_specs += [pl.BlockSpec((tm, LANES), lambda b, t: (t, 0))] * 3
        args += list(rope_tabs)
    return pl.pallas_call(
        functools.partial(_in_even_kernel, latent),
        grid=(B, L // tm),
        in_specs=in_specs,
        out_specs=[tok(2 * FOURIER_WIDTH), tok(QW), kv_spec, kv_spec],
        out_shape=[jax.ShapeDtypeStruct((B, L, 2 * FOURIER_WIDTH), F32 if latent else BF16),
                   jax.ShapeDtypeStruct((B, L, QW), BF16),
                   jax.ShapeDtypeStruct((B, WIN_KV_HEADS, L, LANES), BF16),
                   jax.ShapeDtypeStruct((B, WIN_KV_HEADS, L, LANES), BF16)],
        compiler_params=_params("arbitrary", "arbitrary"),
        scratch_shapes=[pltpu.VMEM((D_MODEL, EV_IN_WIDTH), BF16)],
        name="in_even_latent" if latent else "in_even_ctx",
    )(*args)


def _fft_stage1_kernel(n2t, y_ref, m1_ref, twc_ref, tws_ref, o_ref, row_ref):
    F = FOURIER_WIDTH
    for j in range(n2t):
        row_ref[j % 2] = y_ref[0, :, j, :]
        a = _dot(m1_ref[...], row_ref[j % 2].astype(BF16))
        top, bot = a[:FFT_N], a[FFT_N:]
        ar = top[:, :F] - bot[:, F:]
        ai = top[:, F:] + bot[:, :F]
        tc = jnp.tile(twc_ref[j], (1, F // LANES))
        ts = jnp.tile(tws_ref[j], (1, F // LANES))
        o_ref[0, 0, :, j * F:(j + 1) * F] = (ar * tc + ai * ts).astype(BF16)
        o_ref[0, 1, :, j * F:(j + 1) * F] = (ai * tc - ar * ts).astype(BF16)


def _fft_stage2_kernel(k1t, b_ref, m3_ref, o_ref):
    for j in range(k1t):
        rhs = jnp.concatenate([b_ref[0, 0, j], b_ref[0, 1, j]], axis=0)
        o_ref[0, j] = _dot(m3_ref[...], rhs)


def _fourier_latent(y, m1, m3, twc, tws):
    B = y.shape[0]
    F = FOURIER_WIDTH
    n2t = 16
    yv = y.reshape(B, FFT_N, FFT_N, 2 * F)
    bh = pl.pallas_call(
        functools.partial(_fft_stage1_kernel, n2t),
        grid=(B, FFT_N // n2t),
        in_specs=[pl.BlockSpec((1, FFT_N, n2t, 2 * F), lambda b, t: (b, 0, t, 0)),
                  _const_spec((2 * FFT_N, FFT_N)),
                  pl.BlockSpec((n2t, FFT_N, LANES), lambda b, t: (t, 0, 0)),
                  pl.BlockSpec((n2t, FFT_N, LANES), lambda b, t: (t, 0, 0))],
        out_specs=pl.BlockSpec((1, 2, FFT_N, n2t * F), lambda b, t: (b, 0, 0, t)),
        out_shape=jax.ShapeDtypeStruct((B, 2, FFT_N, FFT_N * F), BF16),
        scratch_shapes=[pltpu.VMEM((2, FFT_N, 2 * F), F32)],
        compiler_params=_params("arbitrary", "arbitrary"),
        name="fft_stage1",
    )(yv, m1, twc, tws)
    k1t = 16
    bv = bh.reshape(B, 2, FFT_N, FFT_N, F)
    return pl.pallas_call(
        functools.partial(_fft_stage2_kernel, k1t),
        grid=(B, FFT_N // k1t),
        in_specs=[pl.BlockSpec((1, 2, k1t, FFT_N, F), lambda b, t: (b, 0, t, 0, 0)),
                  _const_spec((FFT_N, 2 * FFT_N))],
        out_specs=pl.BlockSpec((1, k1t, FFT_N, F), lambda b, t: (b, t, 0, 0)),
        out_shape=jax.ShapeDtypeStruct((B, FFT_N, FFT_N, F), F32),
        compiler_params=_params("arbitrary", "arbitrary"),
        name="fft_stage2",
    )(bv, m3)


def _fourier_ctx_kernel(y_ref, m_ref, o_ref):
    F = FOURIER_WIDTH
    for b in range(y_ref.shape[0]):
        rhs = jnp.concatenate([y_ref[b, :, :F], y_ref[b, :, F:]], axis=0)
        o_ref[b] = _dot(m_ref[...], rhs)


def _fourier_ctx(y, mctx):
    B = y.shape[0]
    return pl.pallas_call(
        _fourier_ctx_kernel,
        grid=(1,),
        in_specs=[pl.BlockSpec((B, CTX_LEN, 2 * FOURIER_WIDTH), lambda i: (0, 0, 0)),
                  _const_spec((CTX_LEN, 2 * CTX_LEN))],
        out_specs=pl.BlockSpec((B, CTX_LEN, FOURIER_WIDTH), lambda i: (0, 0, 0)),
        out_shape=jax.ShapeDtypeStruct((B, CTX_LEN, FOURIER_WIDTH), F32),
        compiler_params=_params("arbitrary"),
        name="fourier_ctx",
    )(y, mctx)


def _gqa_scores(q_slabs, keys, rows, s_ref):
    lo_lanes = lax.broadcasted_iota(jnp.int32, (rows, LANES), 1) < HEAD_DIM
    zero = jnp.zeros((rows, LANES), BF16)
    qs = jnp.concatenate([jnp.where(lo_lanes if half == 0 else ~lo_lanes, qslab, zero)
                          for qslab in q_slabs for half in range(2)], axis=0)
    s_ref[...] = _dot_nt(qs, keys)


def _gqa_attend(vals, masks, sinks, rows, out_slab, first_slab, scratch):
    s_ref, p_ref, inv_ref = scratch
    lo_lanes = lax.broadcasted_iota(jnp.int32, (rows, LANES), 1) < HEAD_DIM
    for c in range(WIN_GROUP * rows // SOFTMAX_ROWS):
        rs = slice(c * SOFTMAX_ROWS, (c + 1) * SOFTMAX_ROWS)
        s = s_ref[rs, :]
        if masks is not None:
            band, w = masks
            r0 = (c * SOFTMAX_ROWS) % rows
            s = jnp.concatenate([s[:, :w] + band[r0:r0 + SOFTMAX_ROWS], s[:, w:]], axis=1)
        sk = sinks[(c * SOFTMAX_ROWS) // rows]
        m = jnp.maximum(jnp.max(s, axis=-1, keepdims=True), sk)
        e = jnp.exp2(s - m)
        inv_ref[rs, :] = 1.0 / (jnp.sum(e, axis=-1, keepdims=True) + jnp.exp2(sk - m))
        p_ref[rs, :] = e.astype(BF16)
    o = _dot(p_ref[...], vals) * inv_ref[...]
    for i in range(WIN_GROUP // 2):
        a = o[(2 * i) * rows:(2 * i + 1) * rows]
        b = o[(2 * i + 1) * rows:(2 * i + 2) * rows]
        sl = slice((first_slab + i) * LANES, (first_slab + i + 1) * LANES)
        out_slab(sl)[...] = jnp.where(lo_lanes, a, b).astype(BF16)


def _gqa_scratch(rows, n_keys):
    one = [pltpu.VMEM((WIN_GROUP * rows, n_keys), F32), pltpu.VMEM((WIN_GROUP * rows, n_keys), BF16),
           pltpu.VMEM((WIN_GROUP * rows, 1), F32)]
    return one * WIN_KV_HEADS


def _group_sinks(sink_ref, kv):
    return [sink_ref[kv * WIN_GROUP + g] * LOG2E for g in range(WIN_GROUP)]


def _group_slab(kv, i):
    return slice((kv * (WIN_GROUP // 2) + i) * LANES, (kv * (WIN_GROUP // 2) + i + 1) * LANES)


def _window_bands():
    i = np.arange(WIN_BLOCK)[:, None]
    j = np.arange(3 * WIN_BLOCK)[None, :]
    bands = [np.where(np.abs(j - off - i) <= WIN_RADIUS, 0.0, -np.inf) for off in (0, WIN_BLOCK, 2 * WIN_BLOCK)]
    return jnp.asarray(np.stack(bands).astype(np.float32))


def _win_attn_kernel(blocks_per_step, sink_ref, band_ref, q_ref, k_ref, v_ref, kx_ref, vx_ref, o_ref,
                     s0_ref, s1_ref, p_ref, inv_ref):
    t = pl.program_id(1)
    nb = SEQ // WIN_BLOCK
    nw = 3 * WIN_BLOCK

    def place(j):
        n = t * blocks_per_step + j
        start = pl.multiple_of(jnp.clip((n - 1) * WIN_BLOCK, 0, SEQ - nw), WIN_BLOCK)
        sel = jnp.where(n == 0, 0, jnp.where(n == nb - 1, 2, 1))
        return pl.multiple_of(j * WIN_BLOCK, WIN_BLOCK), start, sel

    def scores(j, kv, s_ref):
        qoff, start, _ = place(j)
        keys = jnp.concatenate([k_ref[0, kv, pl.ds(start, nw), :], kx_ref[0, kv]], axis=0)
        q_slabs = [q_ref[0, pl.ds(qoff, WIN_BLOCK), _group_slab(kv, i)] for i in range(WIN_GROUP // 2)]
        _gqa_scores(q_slabs, keys, WIN_BLOCK, s_ref)

    def attend(j, kv, s_ref):
        qoff, start, sel = place(j)
        vals = jnp.concatenate([v_ref[0, kv, pl.ds(start, nw), :], vx_ref[0, kv]], axis=0)
        _gqa_attend(vals, (band_ref[sel], nw), _group_sinks(sink_ref, kv), WIN_BLOCK,
                    lambda sl: o_ref.at[0, pl.ds(qoff, WIN_BLOCK), sl], kv * (WIN_GROUP // 2),
                    (s_ref, p_ref, inv_ref))

    scores(0, 0, s0_ref)

    def one_block(j, carry):
        scores(j, 1, s1_ref)
        attend(j, 0, s0_ref)
        scores(jnp.minimum(j + 1, blocks_per_step - 1), 0, s0_ref)
        attend(j, 1, s1_ref)
        return carry

    lax.fori_loop(0, blocks_per_step, one_block, 0)


def _win_attention(q, k, v, kx, vx, sink):
    B, L, _ = q.shape
    blocks_per_step = 16
    rows = blocks_per_step * WIN_BLOCK
    n_keys = 3 * WIN_BLOCK + CTX_LEN
    full = pl.BlockSpec((1, WIN_KV_HEADS, L, LANES), lambda b, t: (b, 0, 0, 0))
    ctx_spec = pl.BlockSpec((1, WIN_KV_HEADS, CTX_LEN, LANES), lambda b, t: (0, 0, b, 0))
    tok = pl.BlockSpec((1, rows, QW), lambda b, t: (b, t, 0))
    return pl.pallas_call(
        functools.partial(_win_attn_kernel, blocks_per_step),
        grid=(B, L // rows),
        in_specs=[pl.BlockSpec(memory_space=pltpu.SMEM),
                  _const_spec((3, WIN_BLOCK, 3 * WIN_BLOCK)),
                  tok, full, full, ctx_spec, ctx_spec],
        out_specs=tok,
        out_shape=jax.ShapeDtypeStruct((B, L, QW), BF16),
        scratch_shapes=[pltpu.VMEM((WIN_GROUP * WIN_BLOCK, n_keys), F32), pltpu.VMEM((WIN_GROUP * WIN_BLOCK, n_keys), F32),
                        pltpu.VMEM((WIN_GROUP * WIN_BLOCK, n_keys), BF16), pltpu.VMEM((WIN_GROUP * WIN_BLOCK, 1), F32)],
        compiler_params=_params("arbitrary", "arbitrary"),
        name="window_attention",
    )(sink, _window_bands(), q, k, v, kx, vx)


def _ctx_attn_even_kernel(sink_ref, q_ref, k_ref, v_ref, o_ref, *scratch):
    for b in range(q_ref.shape[0]):
        tokens = slice(b * CTX_LEN, (b + 1) * CTX_LEN)
        for kv in range(WIN_KV_HEADS):
            unit = scratch[3 * (b * WIN_KV_HEADS + kv):3 * (b * WIN_KV_HEADS + kv) + 3]
            q_slabs = [q_ref[b, :, _group_slab(kv, i)] for i in range(WIN_GROUP // 2)]
            _gqa_scores(q_slabs, k_ref[0, kv, tokens, :], CTX_LEN, unit[0])
            _gqa_attend(v_ref[0, kv, tokens, :], None, _group_sinks(sink_ref, kv), CTX_LEN,
                        lambda sl, b=b: o_ref.at[b, :, sl], kv * (WIN_GROUP // 2), unit)


def _ctx_attention_even(q, k, v, sink):
    B = q.shape[0]
    whole = lambda a: pl.BlockSpec(a.shape, lambda i: (0,) * a.ndim)
    return pl.pallas_call(
        _ctx_attn_even_kernel,
        grid=(1,),
        in_specs=[pl.BlockSpec(memory_space=pltpu.SMEM), whole(q), whole(k), whole(v)],
        out_specs=whole(q),
        out_shape=jax.ShapeDtypeStruct((B, CTX_LEN, QW), BF16),
        scratch_shapes=_gqa_scratch(CTX_LEN, CTX_LEN) * B,
        compiler_params=_params("arbitrary"),
        name="ctx_attention_even",
    )(sink, q, k, v)


def _out_ffn_kernel(mode, *refs):
    if mode == "odd":
        a_ref, x_ref, mod_ref, g_ref, wo_ref, wg_ref, wu_ref, wd_ref, o_ref = refs
    else:
        f_ref, a_ref, x_ref, mod_ref, g_ref, wo_ref, wg_ref, wu_ref, wd_ref, o_ref = refs
    for r in range(x_ref.shape[1] // FFN_SUB_ROWS):
        rows = slice(r * FFN_SUB_ROWS, (r + 1) * FFN_SUB_ROWS)
        if mode == "even_latent":
            planes = range(r * FFN_SUB_ROWS // FFT_N, (r + 1) * FFN_SUB_ROWS // FFT_N)
            fm = jnp.concatenate([f_ref[0, :, j, :] for j in planes], axis=0).astype(BF16)
            o = _dot(fm, wo_ref[:FOURIER_WIDTH]) + _dot(a_ref[0, rows], wo_ref[FOURIER_WIDTH:])
        else:
            o = _dot(jnp.concatenate([a_ref[0, j, rows] for j in range(a_ref.shape[1])], axis=1), wo_ref[...])
        x1 = x_ref[0, rows] + _mod_slice(mod_ref, 2) * o
        h = _rms_mod(x1, g_ref[...], _mod_slice(mod_ref, 4), _mod_slice(mod_ref, 3)).astype(BF16)
        acc = jnp.zeros_like(x1)
        for c in range(D_FF // FF_CHUNK):
            cs = slice(c * FF_CHUNK, (c + 1) * FF_CHUNK)
            a = _silu(_dot(h, wg_ref[0, :, cs])) * _dot(h, wu_ref[0, :, cs])
            acc = acc + _dot(a.astype(BF16), wd_ref[0, cs, :])
        o_ref[0, rows] = x1 + _mod_slice(mod_ref, 5) * acc


def _out_ffn(mode, mix, x, mod, mod_row, g, w_out, layer, wg, wu, wd, tm):
    B, L, _ = x.shape
    tok = lambda w: pl.BlockSpec((1, tm, w), lambda b, t: (b, t, 0))
    ffn_spec = lambda r, c: pl.BlockSpec((1, r, c), lambda b, t: (layer, 0, 0), pipeline_mode=pl.Buffered(1))
    assert mode in ("even_latent", "odd") and tm % FFN_SUB_ROWS == 0
    if mode == "even_latent":
        mix_specs = [pl.BlockSpec((1, FFT_N, tm // FFT_N, FOURIER_WIDTH), lambda b, t: (b, 0, t, 0)), tok(QW)]
    else:
        mix_specs = [pl.BlockSpec((1, NA_WIDTH // LANES, tm, LANES), lambda b, t: (b, 0, t, 0))]
    return pl.pallas_call(
        functools.partial(_out_ffn_kernel, mode),
        grid=(B, L // tm),
        in_specs=mix_specs + [tok(D_MODEL),
                              pl.BlockSpec((1, 1, 6 * D_MODEL), lambda b, t: (mod_row(b), 0, 0)),
                              _const_spec((1, D_MODEL)),
                              _const_spec((D_MODEL, D_MODEL)),
                              ffn_spec(D_MODEL, D_FF), ffn_spec(D_MODEL, D_FF), ffn_spec(D_FF, D_MODEL)],
        out_specs=tok(D_MODEL),
        out_shape=jax.ShapeDtypeStruct((B, L, D_MODEL), F32),
        compiler_params=_params("arbitrary", "arbitrary"),
        name="out_ffn_" + mode,
    )(*mix, x, mod, g, w_out, wg, wu, wd)


def _ctx_ffn_kernel(f_ref, a_ref, x_ref, mod_ref, g_ref, wo_ref, wg32_ref, wu32_ref, wd32_ref,
                    o_ref, wg_ref, wu_ref, wd_ref, x1_ref, h_ref, acc_ref):
    c = pl.program_id(0)
    wg_ref[...] = wg32_ref[...].astype(BF16)
    wu_ref[...] = wu32_ref[...].astype(BF16)
    wd_ref[...] = wd32_ref[...].astype(BF16)

    @pl.when(c == 0)
    def _():
        o = _dot(f_ref[0].astype(BF16), wo_ref[:FOURIER_WIDTH]) + _dot(a_ref[0], wo_ref[FOURIER_WIDTH:])
        x1 = x_ref[0] + _mod_slice(mod_ref, 2) * o
        x1_ref[...] = x1
        h_ref[...] = _rms_mod(x1, g_ref[...], _mod_slice(mod_ref, 4), _mod_slice(mod_ref, 3)).astype(BF16)
        acc_ref[...] = jnp.zeros_like(acc_ref)

    h = h_ref[...]
    a = _silu(_dot(h, wg_ref[0])) * _dot(h, wu_ref[0])
    acc_ref[...] += _dot(a.astype(BF16), wd_ref[0])

    @pl.when(c == pl.num_programs(0) - 1)
    def _():
        o_ref[0] = x1_ref[...] + _mod_slice(mod_ref, 5) * acc_ref[...]


def _ctx_ffn_and_weights(f, a, x, mod, mod_row, g, w_out, wg32, wu32, wd32):
    _, n, _ = x.shape
    nc = D_FF // FF_CHUNK
    whole = lambda w: pl.BlockSpec((1, n, w), lambda c: (0, 0, 0))
    col_chunk = pl.BlockSpec((DEPTH, D_MODEL, FF_CHUNK), lambda c: (0, 0, c))
    row_chunk = pl.BlockSpec((DEPTH, FF_CHUNK, D_MODEL), lambda c: (0, c, 0))
    return pl.pallas_call(
        _ctx_ffn_kernel,
        grid=(nc,),
        in_specs=[whole(FOURIER_WIDTH), whole(QW), whole(D_MODEL),
                  pl.BlockSpec((1, 1, 6 * D_MODEL), lambda c: (mod_row(0), 0, 0)),
                  pl.BlockSpec((1, D_MODEL), lambda c: (0, 0)),
                  pl.BlockSpec((D_MODEL, D_MODEL), lambda c: (0, 0)),
                  col_chunk, col_chunk, row_chunk],
        out_specs=[whole(D_MODEL), col_chunk, col_chunk, row_chunk],
        out_shape=[jax.ShapeDtypeStruct((1, n, D_MODEL), F32),
                   jax.ShapeDtypeStruct((DEPTH, D_MODEL, D_FF), BF16),
                   jax.ShapeDtypeStruct((DEPTH, D_MODEL, D_FF), BF16),
                   jax.ShapeDtypeStruct((DEPTH, D_FF, D_MODEL), BF16)],
        scratch_shapes=[pltpu.VMEM((n, D_MODEL), F32), pltpu.VMEM((n, D_MODEL), BF16), pltpu.VMEM((n, D_MODEL), F32)],
        compiler_params=_params("arbitrary"),
        name="ctx_ffn_and_weights",
    )(f, a, x, mod, g, w_out, wg32, wu32, wd32)


def _in_odd_kernel(with_q, x_ref, mod_ref, g_ref, w32_ref, qg_ref, kg_ref, hm_ref, *outs):
    W = NA_WIDTH
    slabs = RMS_W // LANES
    if with_q:
        q_ref, k_ref, v_ref, w_ref = outs
    else:
        k_ref, v_ref, w_ref = outs
    _cast_once(w32_ref, w_ref)

    for r in range(x_ref.shape[1] // SUB_ROWS):
        rows = slice(r * SUB_ROWS, (r + 1) * SUB_ROWS)
        h = _rms_mod(x_ref[0, rows], g_ref[...], _mod_slice(mod_ref, 1), _mod_slice(mod_ref, 0)).astype(BF16)

        def normed(t, gain_ref, scale, o_ref):
            for j in range(W // RMS_W):
                tj = (_head_rms(t[:, j * RMS_W:(j + 1) * RMS_W], gain_ref, hm_ref) * scale).astype(BF16)
                for i in range(slabs):
                    o_ref[0, j * slabs + i, rows] = tj[:, i * LANES:(i + 1) * LANES]

        if with_q:
            normed(_dot(h, w_ref[:, :W]), qg_ref, QK_SCALE, q_ref)
        normed(_dot(h, w_ref[:, W:2 * W]), kg_ref, 1.0, k_ref)
        v = _dot(h, w_ref[:, 2 * W:]).astype(BF16)
        for j in range(W // LANES):
            v_ref[0, j, rows] = v[:, j * LANES:(j + 1) * LANES]


def _in_odd(x, mod, mod_row, g, w_in, q_g, k_g, hm, with_q, tm):
    B, L, _ = x.shape
    pairs = NA_WIDTH // LANES
    tok = pl.BlockSpec((1, pairs, tm, LANES), lambda b, t: (b, 0, t, 0))
    n_out = 3 if with_q else 2
    return pl.pallas_call(
        functools.partial(_in_odd_kernel, with_q),
        grid=(B, L // tm),
        in_specs=[pl.BlockSpec((1, tm, D_MODEL), lambda b, t: (b, t, 0)),
                  pl.BlockSpec((1, 1, 6 * D_MODEL), lambda b, t: (mod_row(b), 0, 0)),
                  _const_spec((1, D_MODEL)),
                  _const_spec((D_MODEL, 3 * NA_WIDTH)),
                  _const_spec((1, RMS_W)), _const_spec((1, RMS_W)), _const_spec((RMS_W, RMS_W))],
        out_specs=[tok] * n_out,
        out_shape=[jax.ShapeDtypeStruct((B, pairs, L, LANES), BF16)] * n_out,
        compiler_params=_params("arbitrary", "arbitrary"),
        scratch_shapes=[pltpu.VMEM((D_MODEL, 3 * NA_WIDTH), BF16)],
        name="in_odd" if with_q else "in_odd_ctx",
    )(x, mod, g, w_in, q_g, k_g, hm)


def _bias_table_kernel(rb_ref, idx_ref, o_ref):
    idx = idx_ref[...]
    cq = lax.broadcasted_iota(jnp.int32, idx.shape, 0)
    ck = lax.broadcasted_iota(jnp.int32, idx.shape, 1) % GRID_W
    c0 = jnp.clip(cq - NA_KW // 2, 0, GRID_W - NA_KW)
    inside = (ck >= c0) & (ck < c0 + NA_KW)
    for h in range(o_ref.shape[0]):
        for dr in range(o_ref.shape[1]):
            row = jnp.broadcast_to(rb_ref[h, dr:dr + 1, :], idx.shape)
            o_ref[h, dr] = jnp.where(inside, jnp.take_along_axis(row, idx, axis=1) * LOG2E, -jnp.inf)


def _na_bias_table(rel_bias, idx):
    H = NA_HEADS
    npair = 2 * NA_KH - 2
    half = LANES // 2
    hb = 16
    pad = lambda a: jnp.pad(a, ((0, 0), (0, 0), (0, half - a.shape[-1])))
    rb2 = jnp.concatenate([pad(rel_bias[:, :-1]), pad(rel_bias[:, 1:])], axis=-1)
    return pl.pallas_call(
        _bias_table_kernel,
        grid=(H // hb,),
        in_specs=[pl.BlockSpec((hb, npair, LANES), lambda h: (h, 0, 0)), _const_spec((GRID_W, LANES))],
        out_specs=pl.BlockSpec((hb, npair, GRID_W, LANES), lambda h: (h, 0, 0, 0)),
        out_shape=jax.ShapeDtypeStruct((H, npair, GRID_W, LANES), F32),
        compiler_params=_params("arbitrary"),
        name="na_bias_table",
    )(rb2, idx)


def _na_kernel(rows_per_step, q_ref, qn_ref, k_ref, v_ref, kx_ref, vx_ref, bias_ref, o_ref,
               s0_ref, s1_ref, p_ref, inv_ref):
    t = pl.program_id(1)
    n = NA_KH * GRID_W
    lo_lanes = lax.broadcasted_iota(jnp.int32, (GRID_W, LANES), 1) < HEAD_DIM

    def window(i):
        r = jnp.minimum(t * rows_per_step + i, GRID_ROWS - 1)
        r0 = jnp.clip(r - NA_KH // 2, 0, GRID_ROWS - NA_KH)
        qoff = jnp.minimum(i, rows_per_step - 1) * GRID_W
        return r0 - r + NA_KH - 1, pl.multiple_of(r0 * GRID_W, GRID_W), pl.multiple_of(qoff, GRID_W)

    def scores(i, s_ref):
        dr0, start, qoff = window(i)
        for hp in range(NA_HEADS // 2):
            qs = jnp.where(i < rows_per_step, q_ref[0, hp, pl.ds(qoff, GRID_W), :], qn_ref[0, hp])
            keys = jnp.concatenate([k_ref[0, hp, pl.ds(start, n), :], kx_ref[0, hp]], axis=0)
            qm = jnp.concatenate([jnp.where(lo_lanes, qs, jnp.zeros_like(qs)),
                                  jnp.where(lo_lanes, jnp.zeros_like(qs), qs)], axis=0)
            bias = jnp.concatenate(
                [jnp.concatenate([bias_ref[2 * hp + half, dr0 + 2 * p] for p in range(NA_KH // 2)], axis=1)
                 for half in range(2)], axis=0)
            rs = slice(2 * hp * GRID_W, (2 * hp + 2) * GRID_W)
            s_ref[rs, :n] = _dot_nt(qm, keys[:n]) + bias
            s_ref[rs, n:] = _dot_nt(qm, keys[n:])

    def attend(i, s_ref):
        _, start, qoff = window(i)
        for c in range(NA_HEADS * GRID_W // SOFTMAX_ROWS):
            cs = slice(c * SOFTMAX_ROWS, (c + 1) * SOFTMAX_ROWS)
            s = s_ref[cs, :]
            e = jnp.exp2(s - jnp.max(s, axis=-1, keepdims=True))
            inv_ref[cs, :] = 1.0 / jnp.sum(e, axis=-1, keepdims=True)
            p_ref[cs, :] = e.astype(BF16)
        for hp in range(NA_HEADS // 2):
            vals = jnp.concatenate([v_ref[0, hp, pl.ds(start, n), :], vx_ref[0, hp]], axis=0)
            rs = slice(2 * hp * GRID_W, (2 * hp + 2) * GRID_W)
            res = _dot(p_ref[rs, :], vals) * inv_ref[rs, :]
            o_ref[0, hp, pl.ds(qoff, GRID_W), :] = jnp.where(lo_lanes, res[:GRID_W], res[GRID_W:]).astype(BF16)

    @pl.when(t == 0)
    def _():
        scores(0, s0_ref)

    def two_rows(j, carry):
        scores(2 * j + 1, s1_ref)
        attend(2 * j, s0_ref)
        scores(2 * j + 2, s0_ref)
        attend(2 * j + 1, s1_ref)
        return carry

    lax.fori_loop(0, rows_per_step // 2, two_rows, 0)


def _neighbourhood_attention(q, k, v, kx, vx, bias_tab):
    B, pairs, L, _ = q.shape
    rows_per_step = 16
    full = pl.BlockSpec((1, pairs, L, LANES), lambda b, t: (b, 0, 0, 0))
    ctx_spec = pl.BlockSpec((1, pairs, CTX_LEN, LANES), lambda b, t: (0, 0, b, 0))
    q_rows = pl.BlockSpec((1, pairs, rows_per_step * GRID_W, LANES), lambda b, t: (b, 0, t, 0))
    q_next = pl.BlockSpec((1, pairs, GRID_W, LANES),
                          lambda b, t: (b, 0, jnp.minimum((t + 1) * rows_per_step, GRID_ROWS - 1), 0))
    n_keys = NA_KH * GRID_W + CTX_LEN
    all_rows = NA_HEADS * GRID_W
    return pl.pallas_call(
        functools.partial(_na_kernel, rows_per_step),
        grid=(B, GRID_ROWS // rows_per_step),
        in_specs=[q_rows, q_next, full, full, ctx_spec, ctx_spec, _const_spec(bias_tab.shape)],
        out_specs=q_rows,
        out_shape=jax.ShapeDtypeStruct((B, pairs, L, LANES), BF16),
        scratch_shapes=[pltpu.VMEM((all_rows, n_keys), F32), pltpu.VMEM((all_rows, n_keys), F32),
                        pltpu.VMEM((all_rows, n_keys), BF16), pltpu.VMEM((all_rows, 1), F32)],
        compiler_params=_params("arbitrary", "arbitrary"),
        name="neighbourhood_attention",
    )(q, q, k, v, kx, vx, bias_tab)


def kernel(x, c, ctx, c_ctx, ada_w, ada_b, norm1_g, norm2_g, ffn_w_gate, ffn_w_up, ffn_w_down,
           ev_w_in, ev_w_out, ev_q_norm, ev_k_norm, ev_sink,
           od_w_in, od_w_out, od_q_norm, od_k_norm, od_rel_bias):
    assert x.shape == (BATCH, SEQ, D_MODEL) and ctx.shape == (BATCH, CTX_LEN, D_MODEL)
    wc, m1, m3, twc, tws, mctx = _fourier_tables()
    hm = _head_mean_matrix()
    rope_tabs = _rope_tables()
    lane_gain = lambda gvec: jnp.tile(gvec, RMS_W // HEAD_DIM).reshape(1, RMS_W)
    lat_row = lambda b: b
    ctx_row = lambda b: CTX_MOD_ROW
    tm = 512
    n_ctx = BATCH * CTX_LEN
    as_seq = lambda a: a.reshape(1, n_ctx, a.shape[-1])
    per_batch = lambda a: a.reshape(BATCH, CTX_LEN, a.shape[-1])

    cs = jnp.concatenate([c, c_ctx[None, :], jnp.zeros((MOD_ROWS - BATCH - 1, D_MODEL), F32)], axis=0)
    mod = _modulation(cs, ada_w, ada_b).reshape(DEPTH, MOD_ROWS, 1, 6 * D_MODEL)

    w_in0 = ev_w_in[0]
    w_out0 = ev_w_out[0].astype(BF16)
    g1 = norm1_g[0].reshape(1, D_MODEL)
    g2 = norm2_g[0].reshape(1, D_MODEL)
    qg, kg = lane_gain(ev_q_norm[0]), lane_gain(ev_k_norm[0])
    y_l, q_l, k_l, v_l = _in_even(x, mod[0], lat_row, g1, w_in0, qg, kg, hm, wc, rope_tabs, 2 * tm)
    y_c, q_c, k_c, v_c = _in_even(as_seq(ctx), mod[0], ctx_row, g1, w_in0, qg, kg, hm, wc, None, tm)
    y_c, q_c = per_batch(y_c), per_batch(q_c)
    f_l = _fourier_latent(y_l, m1, m3, twc, tws)
    f_c = _fourier_ctx(y_c, mctx)
    a_l = _win_attention(q_l, k_l, v_l, k_c, v_c, ev_sink[0])
    a_c = _ctx_attention_even(q_c, k_c, v_c, ev_sink[0])
    y1, *ffn = _ctx_ffn_and_weights(as_seq(f_c), as_seq(a_c), as_seq(ctx), mod[0], ctx_row, g2, w_out0,
                                    ffn_w_gate, ffn_w_up, ffn_w_down)
    x1 = _out_ffn("even_latent", (f_l, a_l), x, mod[0], lat_row, g2, w_out0, 0, *ffn, 2 * tm)

    w_in1 = od_w_in[0]
    w_out1 = od_w_out[0].astype(BF16)
    g1 = norm1_g[1].reshape(1, D_MODEL)
    g2 = norm2_g[1].reshape(1, D_MODEL)
    qg, kg = lane_gain(od_q_norm[0]), lane_gain(od_k_norm[0])
    q_l, k_l, v_l = _in_odd(x1, mod[1], lat_row, g1, w_in1, qg, kg, hm, True, 2 * tm)
    k_c, v_c = _in_odd(y1, mod[1], ctx_row, g1, w_in1, qg, kg, hm, False, n_ctx)
    bias_tab = _na_bias_table(od_rel_bias[0], _na_bias_index())
    a_l = _neighbourhood_attention(q_l, k_l, v_l, k_c, v_c, bias_tab)
    return _out_ffn("odd", (a_l,), x1, mod[1], lat_row, g2, w_out1, 1, *ffn, 2 * tm)
```

```python
import functools
import math

import numpy as np
import jax
import jax.numpy as jnp
from jax import lax
from jax.experimental import pallas as pl
from jax.experimental.pallas import tpu as pltpu

D_MODEL = 1024
BATCH = 4
SEQ = 4096
DEPTH = 2
GRID_W = 64
CTX_LEN = 256
HEAD_DIM = 64
EPS = 1e-6
FOURIER_WIDTH = D_MODEL // 2
FOURIER_GROUPS = 4
FOURIER_GROUP_CH = FOURIER_WIDTH // FOURIER_GROUPS
WIN_Q_HEADS = (D_MODEL // 2) // HEAD_DIM
WIN_KV_HEADS = 2
WIN_GROUP = WIN_Q_HEADS // WIN_KV_HEADS
WIN_RADIUS = 128
WIN_BLOCK = 128
QW = WIN_Q_HEADS * HEAD_DIM
KW = WIN_KV_HEADS * HEAD_DIM
EV_IN_WIDTH = FOURIER_WIDTH + QW + 2 * KW
NA_HEADS = D_MODEL // HEAD_DIM
NA_KH = 8
NA_KW = 16
NA_WIDTH = NA_HEADS * HEAD_DIM
ROPE_THETA = 10000.0
ROPE_FREQS = HEAD_DIM // 4
D_FF = ((8 * D_MODEL // 3 + 255) // 256) * 256
GRID_ROWS = SEQ // GRID_W

LANES = 128
MOD_ROWS = 8
CTX_MOD_ROW = BATCH
FFT_N = 64
FF_CHUNK = 256
SOFTMAX_ROWS = 64
RMS_W = 256
SUB_ROWS = 512
FFN_SUB_ROWS = 1024
STREAM_BUFFERS = 2
VMEM_LIMIT = 60 * 1024 * 1024

LOG2E = math.log2(math.e)
QK_SCALE = LOG2E / math.sqrt(HEAD_DIM)

BF16 = jnp.bfloat16
F32 = jnp.float32

assert DEPTH == 2 and SEQ == FFT_N * FFT_N and D_FF % FF_CHUNK == 0


def _params(*sem):
    return pltpu.CompilerParams(dimension_semantics=sem, vmem_limit_bytes=VMEM_LIMIT)


def _dot(a, b):
    return jnp.dot(a, b, preferred_element_type=F32)


def _dot_nt(a, b):
    return lax.dot_general(a, b, (((1,), (1,)), ((), ())), preferred_element_type=F32)


def _silu(x):
    return x / (1.0 + jnp.exp(-x))


def _const_spec(shape):
    nd = len(shape)
    return pl.BlockSpec(shape, lambda *_: (0,) * nd, pipeline_mode=pl.Buffered(1))


def _dft_cos_sin(n):
    idx = (np.arange(n)[:, None] * np.arange(n)[None, :]) % n
    ang = 2.0 * np.pi * idx / n
    return np.cos(ang), np.sin(ang)


def _fourier_tables():
    cc, sc = _dft_cos_sin(FOURIER_GROUP_CH)
    wc = np.concatenate([cc, -sc], axis=1) / math.sqrt(FOURIER_GROUP_CH)
    c64, s64 = _dft_cos_sin(FFT_N)
    m1 = np.concatenate([c64, -s64], axis=0) / math.sqrt(FFT_N)
    m3 = np.concatenate([c64, s64], axis=1) / math.sqrt(FFT_N)
    tw = (np.arange(FFT_N)[:, None] * np.arange(FFT_N)[None, :]) % SEQ
    tw = 2.0 * np.pi * tw / SEQ
    twc = np.repeat(np.cos(tw)[:, :, None], LANES, axis=2)
    tws = np.repeat(np.sin(tw)[:, :, None], LANES, axis=2)
    cx, sx = _dft_cos_sin(CTX_LEN)
    mctx = np.concatenate([cx, sx], axis=1) / math.sqrt(CTX_LEN)
    as32 = lambda a: jnp.asarray(a, F32)
    return (as32(wc).astype(BF16), as32(m1).astype(BF16), as32(m3).astype(BF16),
            as32(twc), as32(tws), as32(mctx).astype(BF16))


def _head_mean_matrix():
    blk = np.kron(np.eye(RMS_W // HEAD_DIM), np.ones((HEAD_DIM, HEAD_DIM))) / HEAD_DIM
    return jnp.asarray(blk, BF16)


def _rope_tables():
    t = np.arange(SEQ)
    row = (t // GRID_W).astype(np.float32)
    col = (t % GRID_W).astype(np.float32)
    inv = np.float32(ROPE_THETA) ** (-np.arange(ROPE_FREQS, dtype=np.float32) / np.float32(ROPE_FREQS))
    ang_row = (row[:, None] * inv[None, :]).astype(np.float32).astype(np.float64)
    ang_col = (col[:, None] * inv[None, :]).astype(np.float32).astype(np.float64)
    zero = np.zeros_like(ang_row)
    cos = np.concatenate([np.cos(ang_row)] * 2 + [np.cos(ang_col)] * 2, axis=1)
    sin_hi = np.concatenate([-np.sin(ang_row), zero, -np.sin(ang_col), zero], axis=1)
    sin_lo = np.concatenate([zero, np.sin(ang_row), zero, np.sin(ang_col)], axis=1)
    rep = LANES // HEAD_DIM
    return tuple(jnp.asarray(np.tile(a, (1, rep)), F32) for a in (cos, sin_hi, sin_lo))


def _na_bias_index():
    cq = np.arange(GRID_W)
    dc = np.clip(cq[None, :] - cq[:, None] + NA_KW - 1, 0, 2 * NA_KW - 2)
    return jnp.asarray(np.concatenate([dc, dc + LANES // 2], axis=1), jnp.int32)


def _mod_kernel(cs_ref, w_ref, b_ref, o_ref):
    s = _silu(cs_ref[...]).astype(BF16)
    o_ref[0] = _dot(s, w_ref[0].astype(BF16)) + b_ref[0]


def _modulation(cs, ada_w, ada_b):
    tn = 1536
    return pl.pallas_call(
        _mod_kernel,
        grid=(DEPTH, 6 * D_MODEL // tn),
        in_specs=[pl.BlockSpec((MOD_ROWS, D_MODEL), lambda i, j: (0, 0)),
                  pl.BlockSpec((1, D_MODEL, tn), lambda i, j: (i, 0, j), pipeline_mode=pl.Buffered(STREAM_BUFFERS)),
                  pl.BlockSpec((1, 1, tn), lambda i, j: (i, 0, j))],
        out_specs=pl.BlockSpec((1, MOD_ROWS, tn), lambda i, j: (i, 0, j)),
        out_shape=jax.ShapeDtypeStruct((DEPTH, MOD_ROWS, 6 * D_MODEL), F32),
        compiler_params=_params("arbitrary", "arbitrary"),
        name="ada_modulation",
    )(cs, ada_w, ada_b.reshape(DEPTH, 1, 6 * D_MODEL))


def _mod_slice(mod_ref, k):
    return mod_ref[0, :, k * D_MODEL:(k + 1) * D_MODEL]


def _rms_mod(x, g, scale, shift):
    y = x * lax.rsqrt(jnp.mean(x * x, axis=-1, keepdims=True) + EPS)
    return (y * g) * (1.0 + scale) + shift


def _head_rms(t, gain_ref, hm_ref):
    w = t.shape[1]
    ms = _dot((t * t).astype(BF16), hm_ref[:w, :w])
    return t * lax.rsqrt(ms + EPS) * gain_ref[:, :w]


def _rope(t, cos, sin_hi, sin_lo):
    up = pltpu.roll(t, LANES - ROPE_FREQS, axis=1)
    dn = pltpu.roll(t, ROPE_FREQS, axis=1)
    return t * cos + up * sin_hi + dn * sin_lo


def _cast_once(w32_ref, w_ref):
    @pl.when((pl.program_id(0) == 0) & (pl.program_id(1) == 0))
    def _():
        w_ref[...] = w32_ref[...].astype(BF16)


def _in_even_kernel(latent, x_ref, mod_ref, g_ref, w32_ref, qg_ref, kg_ref, hm_ref, wc_ref, *rest):
    if latent:
        cos_ref, sh_ref, sl_ref, y_ref, q_ref, k_ref, v_ref, w_ref = rest
    else:
        y_ref, q_ref, k_ref, v_ref, w_ref = rest
    _cast_once(w32_ref, w_ref)
    F = FOURIER_WIDTH
    lo_lanes = lax.broadcasted_iota(jnp.int32, (1, LANES), 1) < HEAD_DIM

    for r in range(x_ref.shape[1] // SUB_ROWS):
        rows = slice(r * SUB_ROWS, (r + 1) * SUB_ROWS)
        h = _rms_mod(x_ref[0, rows], g_ref[...], _mod_slice(mod_ref, 1), _mod_slice(mod_ref, 0)).astype(BF16)
        f = _dot(h, w_ref[:, :F]).astype(BF16)
        for g in range(FOURIER_GROUPS):
            yg = _dot(f[:, g * LANES:(g + 1) * LANES], wc_ref[...])
            y_ref[0, rows, g * LANES:(g + 1) * LANES] = yg[:, :LANES].astype(y_ref.dtype)
            y_ref[0, rows, F + g * LANES:F + (g + 1) * LANES] = yg[:, LANES:].astype(y_ref.dtype)

        def qk(t, gain_ref):
            t = _head_rms(t, gain_ref, hm_ref)
            if latent:
                t = jnp.concatenate([_rope(t[:, i * LANES:(i + 1) * LANES], cos_ref[rows], sh_ref[rows], sl_ref[rows])
                                     for i in range(t.shape[1] // LANES)], axis=1)
            return t

        q = _dot(h, w_ref[:, F:F + QW])
        for j in range(QW // RMS_W):
            qj = qk(q[:, j * RMS_W:(j + 1) * RMS_W], qg_ref) * QK_SCALE
            q_ref[0, rows, j * RMS_W:(j + 1) * RMS_W] = qj.astype(BF16)
        kv = _dot(h, w_ref[:, F + QW:])

        def store_dup(t, o_ref):
            sw = pltpu.roll(t, HEAD_DIM, axis=1)
            o_ref[0, 0, rows] = jnp.where(lo_lanes, t, sw).astype(BF16)
            o_ref[0, 1, rows] = jnp.where(lo_lanes, sw, t).astype(BF16)

        store_dup(qk(kv[:, :KW], kg_ref), k_ref)
        store_dup(kv[:, KW:], v_ref)


def _in_even(x, mod, mod_row, g, w_in, q_g, k_g, hm, wc, rope_tabs, tm):
    B, L, _ = x.shape
    latent = rope_tabs is not None
    in_specs = [pl.BlockSpec((1, tm, D_MODEL), lambda b, t: (b, t, 0)),
                pl.BlockSpec((1, 1, 6 * D_MODEL), lambda b, t: (mod_row(b), 0, 0)),
                _const_spec((1, D_MODEL)),
                _const_spec((D_MODEL, EV_IN_WIDTH)),
                _const_spec((1, RMS_W)), _const_spec((1, RMS_W)),
                _const_spec((RMS_W, RMS_W)), _const_spec((LANES, 2 * LANES))]
    args = [x, mod, g, w_in, q_g, k_g, hm, wc]
    tok = lambda w: pl.BlockSpec((1, tm, w), lambda b, t: (b, t, 0))
    kv_spec = pl.BlockSpec((1, WIN_KV_HEADS, tm, LANES), lambda b, t: (b, 0, t, 0))
    if latent:
        in_specs += [pl.BlockSpec((tm, LANES), lambda b, t: (t, 0))] * 3
        args += list(rope_tabs)
    return pl.pallas_call(
        functools.partial(_in_even_kernel, latent),
        grid=(B, L // tm),
        in_specs=in_specs,
        out_specs=[tok(2 * FOURIER_WIDTH), tok(QW), kv_spec, kv_spec],
        out_shape=[jax.ShapeDtypeStruct((B, L, 2 * FOURIER_WIDTH), F32 if latent else BF16),
                   jax.ShapeDtypeStruct((B, L, QW), BF16),
                   jax.ShapeDtypeStruct((B, WIN_KV_HEADS, L, LANES), BF16),
                   jax.ShapeDtypeStruct((B, WIN_KV_HEADS, L, LANES), BF16)],
        compiler_params=_params("arbitrary", "arbitrary"),
        scratch_shapes=[pltpu.VMEM((D_MODEL, EV_IN_WIDTH), BF16)],
        name="in_even_latent" if latent else "in_even_ctx",
    )(*args)


def _fft_stage1_kernel(n2t, y_ref, m1_ref, twc_ref, tws_ref, o_ref, row_ref):
    F = FOURIER_WIDTH
    for j in range(n2t):
        row_ref[j % 2] = y_ref[0, :, j, :]
        a = _dot(m1_ref[...], row_ref[j % 2].astype(BF16))
        top, bot = a[:FFT_N], a[FFT_N:]
        ar = top[:, :F] - bot[:, F:]
        ai = top[:, F:] + bot[:, :F]
        tc = jnp.tile(twc_ref[j], (1, F // LANES))
        ts = jnp.tile(tws_ref[j], (1, F // LANES))
        o_ref[0, 0, :, j * F:(j + 1) * F] = (ar * tc + ai * ts).astype(BF16)
        o_ref[0, 1, :, j * F:(j + 1) * F] = (ai * tc - ar * ts).astype(BF16)


def _fft_stage2_kernel(k1t, b_ref, m3_ref, o_ref):
    for j in range(k1t):
        rhs = jnp.concatenate([b_ref[0, 0, j], b_ref[0, 1, j]], axis=0)
        o_ref[0, j] = _dot(m3_ref[...], rhs)


def _fourier_latent(y, m1, m3, twc, tws):
    B = y.shape[0]
    F = FOURIER_WIDTH
    n2t = 16
    yv = y.reshape(B, FFT_N, FFT_N, 2 * F)
    bh = pl.pallas_call(
        functools.partial(_fft_stage1_kernel, n2t),
        grid=(B, FFT_N // n2t),
        in_specs=[pl.BlockSpec((1, FFT_N, n2t, 2 * F), lambda b, t: (b, 0, t, 0),
                               pipeline_mode=pl.Buffered(STREAM_BUFFERS)),
                  _const_spec((2 * FFT_N, FFT_N)),
                  pl.BlockSpec((n2t, FFT_N, LANES), lambda b, t: (t, 0, 0)),
                  pl.BlockSpec((n2t, FFT_N, LANES), lambda b, t: (t, 0, 0))],
        out_specs=pl.BlockSpec((1, 2, FFT_N, n2t * F), lambda b, t: (b, 0, 0, t)),
        out_shape=jax.ShapeDtypeStruct((B, 2, FFT_N, FFT_N * F), BF16),
        scratch_shapes=[pltpu.VMEM((2, FFT_N, 2 * F), F32)],
        compiler_params=_params("arbitrary", "arbitrary"),
        name="fft_stage1",
    )(yv, m1, twc, tws)
    k1t = 16
    bv = bh.reshape(B, 2, FFT_N, FFT_N, F)
    return pl.pallas_call(
        functools.partial(_fft_stage2_kernel, k1t),
        grid=(B, FFT_N // k1t),
        in_specs=[pl.BlockSpec((1, 2, k1t, FFT_N, F), lambda b, t: (b, 0, t, 0, 0),
                               pipeline_mode=pl.Buffered(STREAM_BUFFERS)),
                  _const_spec((FFT_N, 2 * FFT_N))],
        out_specs=pl.BlockSpec((1, k1t, FFT_N, F), lambda b, t: (b, t, 0, 0)),
        out_shape=jax.ShapeDtypeStruct((B, FFT_N, FFT_N, F), F32),
        compiler_params=_params("arbitrary", "arbitrary"),
        name="fft_stage2",
    )(bv, m3)


def _fourier_ctx_kernel(y_ref, m_ref, o_ref):
    F = FOURIER_WIDTH
    rhs = jnp.concatenate([y_ref[0, :, :F], y_ref[0, :, F:]], axis=0)
    o_ref[0] = _dot(m_ref[...], rhs)


def _fourier_ctx(y, mctx):
    B = y.shape[0]
    return pl.pallas_call(
        _fourier_ctx_kernel,
        grid=(B,),
        in_specs=[pl.BlockSpec((1, CTX_LEN, 2 * FOURIER_WIDTH), lambda b: (b, 0, 0)),
                  _const_spec((CTX_LEN, 2 * CTX_LEN))],
        out_specs=pl.BlockSpec((1, CTX_LEN, FOURIER_WIDTH), lambda b: (b, 0, 0)),
        out_shape=jax.ShapeDtypeStruct((B, CTX_LEN, FOURIER_WIDTH), F32),
        compiler_params=_params("arbitrary"),
        name="fourier_ctx",
    )(y, mctx)


def _gqa_scores(q_slabs, keys, rows, s_ref):
    lo_lanes = lax.broadcasted_iota(jnp.int32, (rows, LANES), 1) < HEAD_DIM
    zero = jnp.zeros((rows, LANES), BF16)
    qs = jnp.concatenate([jnp.where(lo_lanes if half == 0 else ~lo_lanes, qslab, zero)
                          for qslab in q_slabs for half in range(2)], axis=0)
    s_ref[...] = _dot_nt(qs, keys)


def _gqa_attend(vals, masks, sinks, rows, out_slab, first_slab, scratch):
    s_ref, p_ref, inv_ref = scratch
    lo_lanes = lax.broadcasted_iota(jnp.int32, (rows, LANES), 1) < HEAD_DIM
    for c in range(WIN_GROUP * rows // SOFTMAX_ROWS):
        rs = slice(c * SOFTMAX_ROWS, (c + 1) * SOFTMAX_ROWS)
        s = s_ref[rs, :]
        if masks is not None:
            band, w = masks
            r0 = (c * SOFTMAX_ROWS) % rows
            s = jnp.concatenate([s[:, :w] + band[r0:r0 + SOFTMAX_ROWS], s[:, w:]], axis=1)
        sk = sinks[(c * SOFTMAX_ROWS) // rows]
        m = jnp.maximum(jnp.max(s, axis=-1, keepdims=True), sk)
        e = jnp.exp2(s - m)
        inv_ref[rs, :] = 1.0 / (jnp.sum(e, axis=-1, keepdims=True) + jnp.exp2(sk - m))
        p_ref[rs, :] = e.astype(BF16)
    o = _dot(p_ref[...], vals) * inv_ref[...]
    for i in range(WIN_GROUP // 2):
        a = o[(2 * i) * rows:(2 * i + 1) * rows]
        b = o[(2 * i + 1) * rows:(2 * i + 2) * rows]
        sl = slice((first_slab + i) * LANES, (first_slab + i + 1) * LANES)
        out_slab(sl)[...] = jnp.where(lo_lanes, a, b).astype(BF16)


def _gqa_scratch(rows, n_keys):
    one = [pltpu.VMEM((WIN_GROUP * rows, n_keys), F32), pltpu.VMEM((WIN_GROUP * rows, n_keys), BF16),
           pltpu.VMEM((WIN_GROUP * rows, 1), F32)]
    return one * WIN_KV_HEADS


def _group_sinks(sink_ref, kv):
    return [sink_ref[kv * WIN_GROUP + g] * LOG2E for g in range(WIN_GROUP)]


def _group_slab(kv, i):
    return slice((kv * (WIN_GROUP // 2) + i) * LANES, (kv * (WIN_GROUP // 2) + i + 1) * LANES)


def _window_bands():
    i = np.arange(WIN_BLOCK)[:, None]
    j = np.arange(3 * WIN_BLOCK)[None, :]
    bands = [np.where(np.abs(j - off - i) <= WIN_RADIUS, 0.0, -np.inf) for off in (0, WIN_BLOCK, 2 * WIN_BLOCK)]
    return jnp.asarray(np.stack(bands).astype(np.float32))


def _win_attn_kernel(blocks_per_step, sink_ref, band_ref, q_ref, k_ref, v_ref, kx_ref, vx_ref, o_ref,
                     s0_ref, s1_ref, p_ref, inv_ref):
    t = pl.program_id(1)
    nb = SEQ // WIN_BLOCK
    nw = 3 * WIN_BLOCK

    def place(j):
        n = t * blocks_per_step + j
        start = pl.multiple_of(jnp.clip((n - 1) * WIN_BLOCK, 0, SEQ - nw), WIN_BLOCK)
        sel = jnp.where(n == 0, 0, jnp.where(n == nb - 1, 2, 1))
        return pl.multiple_of(j * WIN_BLOCK, WIN_BLOCK), start, sel

    def scores(j, kv, s_ref):
        qoff, start, _ = place(j)
        keys = jnp.concatenate([k_ref[0, kv, pl.ds(start, nw), :], kx_ref[0, kv]], axis=0)
        q_slabs = [q_ref[0, pl.ds(qoff, WIN_BLOCK), _group_slab(kv, i)] for i in range(WIN_GROUP // 2)]
        _gqa_scores(q_slabs, keys, WIN_BLOCK, s_ref)

    def attend(j, kv, s_ref):
        qoff, start, sel = place(j)
        vals = jnp.concatenate([v_ref[0, kv, pl.ds(start, nw), :], vx_ref[0, kv]], axis=0)
        _gqa_attend(vals, (band_ref[sel], nw), _group_sinks(sink_ref, kv), WIN_BLOCK,
                    lambda sl: o_ref.at[0, pl.ds(qoff, WIN_BLOCK), sl], kv * (WIN_GROUP // 2),
                    (s_ref, p_ref, inv_ref))

    scores(0, 0, s0_ref)

    def one_block(j, carry):
        scores(j, 1, s1_ref)
        attend(j, 0, s0_ref)
        scores(jnp.minimum(j + 1, blocks_per_step - 1), 0, s0_ref)
        attend(j, 1, s1_ref)
        return carry

    lax.fori_loop(0, blocks_per_step, one_block, 0)


def _win_attention(q, k, v, kx, vx, sink):
    B, L, _ = q.shape
    blocks_per_step = 16
    rows = blocks_per_step * WIN_BLOCK
    n_keys = 3 * WIN_BLOCK + CTX_LEN
    full = pl.BlockSpec((1, WIN_KV_HEADS, L, LANES), lambda b, t: (b, 0, 0, 0))
    ctx_spec = pl.BlockSpec((1, WIN_KV_HEADS, CTX_LEN, LANES), lambda b, t: (0, 0, b, 0))
    tok = pl.BlockSpec((1, rows, QW), lambda b, t: (b, t, 0))
    return pl.pallas_call(
        functools.partial(_win_attn_kernel, blocks_per_step),
        grid=(B, L // rows),
        in_specs=[pl.BlockSpec(memory_space=pltpu.SMEM),
                  _const_spec((3, WIN_BLOCK, 3 * WIN_BLOCK)),
                  tok, full, full, ctx_spec, ctx_spec],
        out_specs=tok,
        out_shape=jax.ShapeDtypeStruct((B, L, QW), BF16),
        scratch_shapes=[pltpu.VMEM((WIN_GROUP * WIN_BLOCK, n_keys), F32), pltpu.VMEM((WIN_GROUP * WIN_BLOCK, n_keys), F32),
                        pltpu.VMEM((WIN_GROUP * WIN_BLOCK, n_keys), BF16), pltpu.VMEM((WIN_GROUP * WIN_BLOCK, 1), F32)],
        compiler_params=_params("arbitrary", "arbitrary"),
        name="window_attention",
    )(sink, _window_bands(), q, k, v, kx, vx)


def _ctx_attn_even_kernel(sink_ref, q_ref, k_ref, v_ref, o_ref, *scratch):
    for b in range(q_ref.shape[0]):
        tokens = slice(b * CTX_LEN, (b + 1) * CTX_LEN)
        for kv in range(WIN_KV_HEADS):
            unit = scratch[3 * (b * WIN_KV_HEADS + kv):3 * (b * WIN_KV_HEADS + kv) + 3]
            q_slabs = [q_ref[b, :, _group_slab(kv, i)] for i in range(WIN_GROUP // 2)]
            _gqa_scores(q_slabs, k_ref[0, kv, tokens, :], CTX_LEN, unit[0])
            _gqa_attend(v_ref[0, kv, tokens, :], None, _group_sinks(sink_ref, kv), CTX_LEN,
                        lambda sl, b=b: o_ref.at[b, :, sl], kv * (WIN_GROUP // 2), unit)


def _ctx_attention_even(q, k, v, sink):
    B = q.shape[0]
    whole = lambda a: pl.BlockSpec(a.shape, lambda i: (0,) * a.ndim)
    return pl.pallas_call(
        _ctx_attn_even_kernel,
        grid=(1,),
        in_specs=[pl.BlockSpec(memory_space=pltpu.SMEM), whole(q), whole(k), whole(v)],
        out_specs=whole(q),
        out_shape=jax.ShapeDtypeStruct((B, CTX_LEN, QW), BF16),
        scratch_shapes=_gqa_scratch(CTX_LEN, CTX_LEN) * B,
        compiler_params=_params("arbitrary"),
        name="ctx_attention_even",
    )(sink, q, k, v)


def _out_ffn_kernel(mode, *refs):
    if mode == "odd":
        a_ref, x_ref, mod_ref, g_ref, wo_ref, wg_ref, wu_ref, wd_ref, o_ref, hid_ref = refs
    else:
        f_ref, a_ref, x_ref, mod_ref, g_ref, wo_ref, wg_ref, wu_ref, wd_ref, o_ref, hid_ref = refs
    for r in range(x_ref.shape[1] // FFN_SUB_ROWS):
        rows = slice(r * FFN_SUB_ROWS, (r + 1) * FFN_SUB_ROWS)
        if mode == "even_latent":
            planes = range(r * FFN_SUB_ROWS // FFT_N, (r + 1) * FFN_SUB_ROWS // FFT_N)
            fm = jnp.concatenate([f_ref[0, :, j, :] for j in planes], axis=0).astype(BF16)
            o = _dot(fm, wo_ref[:FOURIER_WIDTH]) + _dot(a_ref[0, rows], wo_ref[FOURIER_WIDTH:])
        else:
            o = _dot(jnp.concatenate([a_ref[0, j, rows] for j in range(a_ref.shape[1])], axis=1), wo_ref[...])
        x1 = x_ref[0, rows] + _mod_slice(mod_ref, 2) * o
        h = _rms_mod(x1, g_ref[...], _mod_slice(mod_ref, 4), _mod_slice(mod_ref, 3)).astype(BF16)
        for c in range(D_FF // FF_CHUNK):
            cs = slice(c * FF_CHUNK, (c + 1) * FF_CHUNK)
            hid_ref[:, cs] = (_silu(_dot(h, wg_ref[0, :, cs])) * _dot(h, wu_ref[0, :, cs])).astype(BF16)
        o_ref[0, rows] = x1 + _mod_slice(mod_ref, 5) * _dot(hid_ref[...], wd_ref[0])


def _out_ffn(mode, mix, x, mod, mod_row, g, w_out, layer, wg, wu, wd, tm):
    B, L, _ = x.shape
    tok = lambda w: pl.BlockSpec((1, tm, w), lambda b, t: (b, t, 0))
    ffn_spec = lambda r, c: pl.BlockSpec((1, r, c), lambda b, t: (layer, 0, 0), pipeline_mode=pl.Buffered(1))
    assert mode in ("even_latent", "odd") and tm % FFN_SUB_ROWS == 0
    if mode == "even_latent":
        mix_specs = [pl.BlockSpec((1, FFT_N, tm // FFT_N, FOURIER_WIDTH), lambda b, t: (b, 0, t, 0)), tok(QW)]
    else:
        mix_specs = [pl.BlockSpec((1, NA_WIDTH // LANES, tm, LANES), lambda b, t: (b, 0, t, 0))]
    return pl.pallas_call(
        functools.partial(_out_ffn_kernel, mode),
        grid=(B, L // tm),
        in_specs=mix_specs + [tok(D_MODEL),
                              pl.BlockSpec((1, 1, 6 * D_MODEL), lambda b, t: (mod_row(b), 0, 0)),
                              _const_spec((1, D_MODEL)),
                              _const_spec((D_MODEL, D_MODEL)),
                              ffn_spec(D_MODEL, D_FF), ffn_spec(D_MODEL, D_FF), ffn_spec(D_FF, D_MODEL)],
        out_specs=tok(D_MODEL),
        out_shape=jax.ShapeDtypeStruct((B, L, D_MODEL), F32),
        scratch_shapes=[pltpu.VMEM((FFN_SUB_ROWS, D_FF), BF16)],
        compiler_params=_params("arbitrary", "arbitrary"),
        name="out_ffn_" + mode,
    )(*mix, x, mod, g, w_out, wg, wu, wd)


def _ctx_ffn_kernel(f_ref, a_ref, x_ref, mod_ref, g_ref, wo_ref, wg32_ref, wu32_ref, wd32_ref,
                    o_ref, wg_ref, wu_ref, wd_ref, x1_ref, h_ref, acc_ref):
    c = pl.program_id(0)
    wg_ref[...] = wg32_ref[...].astype(BF16)
    wu_ref[...] = wu32_ref[...].astype(BF16)
    wd_ref[...] = wd32_ref[...].astype(BF16)

    @pl.when(c == 0)
    def _():
        o = _dot(f_ref[0].astype(BF16), wo_ref[:FOURIER_WIDTH]) + _dot(a_ref[0], wo_ref[FOURIER_WIDTH:])
        x1 = x_ref[0] + _mod_slice(mod_ref, 2) * o
        x1_ref[...] = x1
        h_ref[...] = _rms_mod(x1, g_ref[...], _mod_slice(mod_ref, 4), _mod_slice(mod_ref, 3)).astype(BF16)
        acc_ref[...] = jnp.zeros_like(acc_ref)

    h = h_ref[...]
    a = _silu(_dot(h, wg_ref[0])) * _dot(h, wu_ref[0])
    acc_ref[...] += _dot(a.astype(BF16), wd_ref[0])

    @pl.when(c == pl.num_programs(0) - 1)
    def _():
        o_ref[0] = x1_ref[...] + _mod_slice(mod_ref, 5) * acc_ref[...]


def _ctx_ffn_and_weights(f, a, x, mod, mod_row, g, w_out, wg32, wu32, wd32):
    _, n, _ = x.shape
    nc = D_FF // FF_CHUNK
    whole = lambda w: pl.BlockSpec((1, n, w), lambda c: (0, 0, 0))
    col_chunk = pl.BlockSpec((DEPTH, D_MODEL, FF_CHUNK), lambda c: (0, 0, c))
    row_chunk = pl.BlockSpec((DEPTH, FF_CHUNK, D_MODEL), lambda c: (0, c, 0))
    deep = dict(pipeline_mode=pl.Buffered(STREAM_BUFFERS))
    col_in = pl.BlockSpec((DEPTH, D_MODEL, FF_CHUNK), lambda c: (0, 0, c), **deep)
    row_in = pl.BlockSpec((DEPTH, FF_CHUNK, D_MODEL), lambda c: (0, c, 0), **deep)
    return pl.pallas_call(
        _ctx_ffn_kernel,
        grid=(nc,),
        in_specs=[whole(FOURIER_WIDTH), whole(QW), whole(D_MODEL),
                  pl.BlockSpec((1, 1, 6 * D_MODEL), lambda c: (mod_row(0), 0, 0)),
                  pl.BlockSpec((1, D_MODEL), lambda c: (0, 0)),
                  pl.BlockSpec((D_MODEL, D_MODEL), lambda c: (0, 0)),
                  col_in, col_in, row_in],
        out_specs=[whole(D_MODEL), col_chunk, col_chunk, row_chunk],
        out_shape=[jax.ShapeDtypeStruct((1, n, D_MODEL), F32),
                   jax.ShapeDtypeStruct((DEPTH, D_MODEL, D_FF), BF16),
                   jax.ShapeDtypeStruct((DEPTH, D_MODEL, D_FF), BF16),
                   jax.ShapeDtypeStruct((DEPTH, D_FF, D_MODEL), BF16)],
        scratch_shapes=[pltpu.VMEM((n, D_MODEL), F32), pltpu.VMEM((n, D_MODEL), BF16), pltpu.VMEM((n, D_MODEL), F32)],
        compiler_params=_params("arbitrary"),
        name="ctx_ffn_and_weights",
    )(f, a, x, mod, g, w_out, wg32, wu32, wd32)


def _in_odd_kernel(with_q, x_ref, mod_ref, g_ref, w32_ref, qg_ref, kg_ref, hm_ref, *outs):
    W = NA_WIDTH
    slabs = RMS_W // LANES
    if with_q:
        q_ref, k_ref, v_ref, w_ref = outs
    else:
        k_ref, v_ref, w_ref = outs
    _cast_once(w32_ref, w_ref)

    for r in range(x_ref.shape[1] // SUB_ROWS):
        rows = slice(r * SUB_ROWS, (r + 1) * SUB_ROWS)
        h = _rms_mod(x_ref[0, rows], g_ref[...], _mod_slice(mod_ref, 1), _mod_slice(mod_ref, 0)).astype(BF16)

        def normed(t, gain_ref, scale, o_ref):
            for j in range(W // RMS_W):
                tj = (_head_rms(t[:, j * RMS_W:(j + 1) * RMS_W], gain_ref, hm_ref) * scale).astype(BF16)
                for i in range(slabs):
                    o_ref[0, j * slabs + i, rows] = tj[:, i * LANES:(i + 1) * LANES]

        if with_q:
            normed(_dot(h, w_ref[:, :W]), qg_ref, QK_SCALE, q_ref)
        normed(_dot(h, w_ref[:, W:2 * W]), kg_ref, 1.0, k_ref)
        v = _dot(h, w_ref[:, 2 * W:]).astype(BF16)
        for j in range(W // LANES):
            v_ref[0, j, rows] = v[:, j * LANES:(j + 1) * LANES]


def _in_odd(x, mod, mod_row, g, w_in, q_g, k_g, hm, with_q, tm):
    B, L, _ = x.shape
    pairs = NA_WIDTH // LANES
    tok = pl.BlockSpec((1, pairs, tm, LANES), lambda b, t: (b, 0, t, 0))
    n_out = 3 if with_q else 2
    return pl.pallas_call(
        functools.partial(_in_odd_kernel, with_q),
        grid=(B, L // tm),
        in_specs=[pl.BlockSpec((1, tm, D_MODEL), lambda b, t: (b, t, 0)),
                  pl.BlockSpec((1, 1, 6 * D_MODEL), lambda b, t: (mod_row(b), 0, 0)),
                  _const_spec((1, D_MODEL)),
                  _const_spec((D_MODEL, 3 * NA_WIDTH)),
                  _const_spec((1, RMS_W)), _const_spec((1, RMS_W)), _const_spec((RMS_W, RMS_W))],
        out_specs=[tok] * n_out,
        out_shape=[jax.ShapeDtypeStruct((B, pairs, L, LANES), BF16)] * n_out,
        compiler_params=_params("arbitrary", "arbitrary"),
        scratch_shapes=[pltpu.VMEM((D_MODEL, 3 * NA_WIDTH), BF16)],
        name="in_odd" if with_q else "in_odd_ctx",
    )(x, mod, g, w_in, q_g, k_g, hm)


def _bias_table_kernel(rb_ref, idx_ref, o_ref):
    idx = idx_ref[...]
    cq = lax.broadcasted_iota(jnp.int32, idx.shape, 0)
    ck = lax.broadcasted_iota(jnp.int32, idx.shape, 1) % GRID_W
    c0 = jnp.clip(cq - NA_KW // 2, 0, GRID_W - NA_KW)
    inside = (ck >= c0) & (ck < c0 + NA_KW)
    for h in range(o_ref.shape[0]):
        for dr in range(o_ref.shape[1]):
            row = jnp.broadcast_to(rb_ref[h, dr:dr + 1, :], idx.shape)
            o_ref[h, dr] = jnp.where(inside, jnp.take_along_axis(row, idx, axis=1) * LOG2E, -jnp.inf)


def _na_bias_table(rel_bias, idx):
    H = NA_HEADS
    npair = 2 * NA_KH - 2
    half = LANES // 2
    hb = 8
    pad = lambda a: jnp.pad(a, ((0, 0), (0, 0), (0, half - a.shape[-1])))
    rb2 = jnp.concatenate([pad(rel_bias[:, :-1]), pad(rel_bias[:, 1:])], axis=-1)
    return pl.pallas_call(
        _bias_table_kernel,
        grid=(H // hb,),
        in_specs=[pl.BlockSpec((hb, npair, LANES), lambda h: (h, 0, 0)), _const_spec((GRID_W, LANES))],
        out_specs=pl.BlockSpec((hb, npair, GRID_W, LANES), lambda h: (h, 0, 0, 0)),
        out_shape=jax.ShapeDtypeStruct((H, npair, GRID_W, LANES), F32),
        compiler_params=_params("arbitrary"),
        name="na_bias_table",
    )(rb2, idx)


def _na_kernel(rows_per_step, q_ref, qn_ref, k_ref, v_ref, kx_ref, vx_ref, bias_ref, o_ref,
               s0_ref, s1_ref, p_ref, inv_ref):
    t = pl.program_id(1)
    n = NA_KH * GRID_W
    lo_lanes = lax.broadcasted_iota(jnp.int32, (GRID_W, LANES), 1) < HEAD_DIM

    def window(i):
        r = jnp.minimum(t * rows_per_step + i, GRID_ROWS - 1)
        r0 = jnp.clip(r - NA_KH // 2, 0, GRID_ROWS - NA_KH)
        qoff = jnp.minimum(i, rows_per_step - 1) * GRID_W
        return r0 - r + NA_KH - 1, pl.multiple_of(r0 * GRID_W, GRID_W), pl.multiple_of(qoff, GRID_W)

    def scores(i, s_ref):
        dr0, start, qoff = window(i)
        for hp in range(NA_HEADS // 2):
            qs = jnp.where(i < rows_per_step, q_ref[0, hp, pl.ds(qoff, GRID_W), :], qn_ref[0, hp])
            keys = jnp.concatenate([k_ref[0, hp, pl.ds(start, n), :], kx_ref[0, hp]], axis=0)
            qm = jnp.concatenate([jnp.where(lo_lanes, qs, jnp.zeros_like(qs)),
                                  jnp.where(lo_lanes, jnp.zeros_like(qs), qs)], axis=0)
            bias = jnp.concatenate(
                [jnp.concatenate([bias_ref[2 * hp + half, dr0 + 2 * p] for p in range(NA_KH // 2)], axis=1)
                 for half in range(2)], axis=0)
            rs = slice(2 * hp * GRID_W, (2 * hp + 2) * GRID_W)
            s_ref[rs, :n] = _dot_nt(qm, keys[:n]) + bias
            s_ref[rs, n:] = _dot_nt(qm, keys[n:])

    def attend(i, s_ref):
        _, start, qoff = window(i)
        for c in range(NA_HEADS * GRID_W // SOFTMAX_ROWS):
            cs = slice(c * SOFTMAX_ROWS, (c + 1) * SOFTMAX_ROWS)
            s = s_ref[cs, :]
            e = jnp.exp2(s - jnp.max(s, axis=-1, keepdims=True))
            inv_ref[cs, :] = 1.0 / jnp.sum(e, axis=-1, keepdims=True)
            p_ref[cs, :] = e.astype(BF16)
        for hp in range(NA_HEADS // 2):
            vals = jnp.concatenate([v_ref[0, hp, pl.ds(start, n), :], vx_ref[0, hp]], axis=0)
            rs = slice(2 * hp * GRID_W, (2 * hp + 2) * GRID_W)
            res = _dot(p_ref[rs, :], vals) * inv_ref[rs, :]
            o_ref[0, hp, pl.ds(qoff, GRID_W), :] = jnp.where(lo_lanes, res[:GRID_W], res[GRID_W:]).astype(BF16)

    @pl.when(t == 0)
    def _():
        scores(0, s0_ref)

    def two_rows(j, carry):
        scores(2 * j + 1, s1_ref)
        attend(2 * j, s0_ref)
        scores(2 * j + 2, s0_ref)
        attend(2 * j + 1, s1_ref)
        return carry

    lax.fori_loop(0, rows_per_step // 2, two_rows, 0)


def _neighbourhood_attention(q, k, v, kx, vx, bias_tab):
    B, pairs, L, _ = q.shape
    rows_per_step = 16
    full = pl.BlockSpec((1, pairs, L, LANES), lambda b, t: (b, 0, 0, 0))
    ctx_spec = pl.BlockSpec((1, pairs, CTX_LEN, LANES), lambda b, t: (0, 0, b, 0))
    q_rows = pl.BlockSpec((1, pairs, rows_per_step * GRID_W, LANES), lambda b, t: (b, 0, t, 0))
    q_next = pl.BlockSpec((1, pairs, GRID_W, LANES),
                          lambda b, t: (b, 0, jnp.minimum((t + 1) * rows_per_step, GRID_ROWS - 1), 0))
    n_keys = NA_KH * GRID_W + CTX_LEN
    all_rows = NA_HEADS * GRID_W
    return pl.pallas_call(
        functools.partial(_na_kernel, rows_per_step),
        grid=(B, GRID_ROWS // rows_per_step),
        in_specs=[q_rows, q_next, full, full, ctx_spec, ctx_spec, _const_spec(bias_tab.shape)],
        out_specs=q_rows,
        out_shape=jax.ShapeDtypeStruct((B, pairs, L, LANES), BF16),
        scratch_shapes=[pltpu.VMEM((all_rows, n_keys), F32), pltpu.VMEM((all_rows, n_keys), F32),
                        pltpu.VMEM((all_rows, n_keys), BF16), pltpu.VMEM((all_rows, 1), F32)],
        compiler_params=_params("arbitrary", "arbitrary"),
        name="neighbourhood_attention",
    )(q, q, k, v, kx, vx, bias_tab)


def kernel(x, c, ctx, c_ctx, ada_w, ada_b, norm1_g, norm2_g, ffn_w_gate, ffn_w_up, ffn_w_down,
           ev_w_in, ev_w_out, ev_q_norm, ev_k_norm, ev_sink,
           od_w_in, od_w_out, od_q_norm, od_k_norm, od_rel_bias):
    assert x.shape == (BATCH, SEQ, D_MODEL) and ctx.shape == (BATCH, CTX_LEN, D_MODEL)
    wc, m1, m3, twc, tws, mctx = _fourier_tables()
    hm = _head_mean_matrix()
    rope_tabs = _rope_tables()
    lane_gain = lambda gvec: jnp.tile(gvec, RMS_W // HEAD_DIM).reshape(1, RMS_W)
    lat_row = lambda b: b
    ctx_row = lambda b: CTX_MOD_ROW
    tm = 512
    n_ctx = BATCH * CTX_LEN
    as_seq = lambda a: a.reshape(1, n_ctx, a.shape[-1])
    per_batch = lambda a: a.reshape(BATCH, CTX_LEN, a.shape[-1])

    cs = jnp.concatenate([c, c_ctx[None, :], jnp.zeros((MOD_ROWS - BATCH - 1, D_MODEL), F32)], axis=0)
    mod = _modulation(cs, ada_w, ada_b).reshape(DEPTH, MOD_ROWS, 1, 6 * D_MODEL)

    w_in0 = ev_w_in[0]
    w_out0 = ev_w_out[0].astype(BF16)
    g1 = norm1_g[0].reshape(1, D_MODEL)
    g2 = norm2_g[0].reshape(1, D_MODEL)
    qg, kg = lane_gain(ev_q_norm[0]), lane_gain(ev_k_norm[0])
    y_l, q_l, k_l, v_l = _in_even(x, mod[0], lat_row, g1, w_in0, qg, kg, hm, wc, rope_tabs, 2 * tm)
    y_c, q_c, k_c, v_c = _in_even(as_seq(ctx), mod[0], ctx_row, g1, w_in0, qg, kg, hm, wc, None, tm)
    y_c, q_c = per_batch(y_c), per_batch(q_c)
    f_l = _fourier_latent(y_l, m1, m3, twc, tws)
    f_c = _fourier_ctx(y_c, mctx)
    a_l = _win_attention(q_l, k_l, v_l, k_c, v_c, ev_sink[0])
    a_c = _ctx_attention_even(q_c, k_c, v_c, ev_sink[0])
    y1, *ffn = _ctx_ffn_and_weights(as_seq(f_c), as_seq(a_c), as_seq(ctx), mod[0], ctx_row, g2, w_out0,
                                    ffn_w_gate, ffn_w_up, ffn_w_down)
    x1 = _out_ffn("even_latent", (f_l, a_l), x, mod[0], lat_row, g2, w_out0, 0, *ffn, 2 * tm)

    w_in1 = od_w_in[0]
    w_out1 = od_w_out[0].astype(BF16)
    g1 = norm1_g[1].reshape(1, D_MODEL)
    g2 = norm2_g[1].reshape(1, D_MODEL)
    qg, kg = lane_gain(od_q_norm[0]), lane_gain(od_k_norm[0])
    q_l, k_l, v_l = _in_odd(x1, mod[1], lat_row, g1, w_in1, qg, kg, hm, True, 2 * tm)
    k_c, v_c = _in_odd(y1, mod[1], ctx_row, g1, w_in1, qg, kg, hm, False, n_ctx)
    bias_tab = _na_bias_table(od_rel_bias[0], _na_bias_index())
    a_l = _neighbourhood_attention(q_l, k_l, v_l, k_c, v_c, bias_tab)
    return _out_ffn("odd", (a_l,), x1, mod[1], lat_row, g2, w_out1, 1, *ffn, 2 * tm)
```

```python
import functools
import math

import numpy as np
import jax
import jax.numpy as jnp
from jax import lax
from jax.experimental import pallas as pl
from jax.experimental.pallas import tpu as pltpu

D_MODEL = 1024
BATCH = 4
SEQ = 4096
DEPTH = 2
GRID_W = 64
CTX_LEN = 256
HEAD_DIM = 64
EPS = 1e-6
FOURIER_WIDTH = D_MODEL // 2
FOURIER_GROUPS = 4
FOURIER_GROUP_CH = FOURIER_WIDTH // FOURIER_GROUPS
WIN_Q_HEADS = (D_MODEL // 2) // HEAD_DIM
WIN_KV_HEADS = 2
WIN_GROUP = WIN_Q_HEADS // WIN_KV_HEADS
WIN_RADIUS = 128
WIN_BLOCK = 128
QW = WIN_Q_HEADS * HEAD_DIM
KW = WIN_KV_HEADS * HEAD_DIM
EV_IN_WIDTH = FOURIER_WIDTH + QW + 2 * KW
NA_HEADS = D_MODEL // HEAD_DIM
NA_KH = 8
NA_KW = 16
NA_WIDTH = NA_HEADS * HEAD_DIM
ROPE_THETA = 10000.0
ROPE_FREQS = HEAD_DIM // 4
D_FF = ((8 * D_MODEL // 3 + 255) // 256) * 256
GRID_ROWS = SEQ // GRID_W

LANES = 128
MOD_ROWS = 8
CTX_MOD_ROW = BATCH
FFT_N = 64
FF_CHUNK = 256
SOFTMAX_ROWS = 64
RMS_W = 256
SUB_ROWS = 512
FFN_SUB_ROWS = 1024
STREAM_BUFFERS = 2
VMEM_LIMIT = 60 * 1024 * 1024

LOG2E = math.log2(math.e)
QK_SCALE = LOG2E / math.sqrt(HEAD_DIM)

BF16 = jnp.bfloat16
F32 = jnp.float32

assert DEPTH == 2 and SEQ == FFT_N * FFT_N and D_FF % FF_CHUNK == 0


def _params(*sem):
    return pltpu.CompilerParams(dimension_semantics=sem, vmem_limit_bytes=VMEM_LIMIT)


def _dot(a, b):
    return jnp.dot(a, b, preferred_element_type=F32)


def _dot_nt(a, b):
    return lax.dot_general(a, b, (((1,), (1,)), ((), ())), preferred_element_type=F32)


def _silu(x):
    return x / (1.0 + jnp.exp(-x))


def _const_spec(shape):
    nd = len(shape)
    return pl.BlockSpec(shape, lambda *_: (0,) * nd, pipeline_mode=pl.Buffered(1))


def _dft_cos_sin(n):
    idx = (np.arange(n)[:, None] * np.arange(n)[None, :]) % n
    ang = 2.0 * np.pi * idx / n
    return np.cos(ang), np.sin(ang)


def _fourier_tables():
    cc, sc = _dft_cos_sin(FOURIER_GROUP_CH)
    wc = np.concatenate([cc, -sc], axis=1) / math.sqrt(FOURIER_GROUP_CH)
    c64, s64 = _dft_cos_sin(FFT_N)
    m1 = np.concatenate([c64, -s64], axis=0) / math.sqrt(FFT_N)
    m3 = np.concatenate([c64, s64], axis=1) / math.sqrt(FFT_N)
    tw = (np.arange(FFT_N)[:, None] * np.arange(FFT_N)[None, :]) % SEQ
    tw = 2.0 * np.pi * tw / SEQ
    twc = np.repeat(np.cos(tw)[:, :, None], LANES, axis=2)
    tws = np.repeat(np.sin(tw)[:, :, None], LANES, axis=2)
    cx, sx = _dft_cos_sin(CTX_LEN)
    mctx = np.concatenate([cx, sx], axis=1) / math.sqrt(CTX_LEN)
    as32 = lambda a: jnp.asarray(a, F32)
    return (as32(wc).astype(BF16), as32(m1).astype(BF16), as32(m3).astype(BF16),
            as32(twc), as32(tws), as32(mctx).astype(BF16))


def _head_mean_matrix():
    blk = np.kron(np.eye(RMS_W // HEAD_DIM), np.ones((HEAD_DIM, HEAD_DIM))) / HEAD_DIM
    return jnp.asarray(blk, BF16)


def _rope_tables():
    t = np.arange(SEQ)
    row = (t // GRID_W).astype(np.float32)
    col = (t % GRID_W).astype(np.float32)
    inv = np.float32(ROPE_THETA) ** (-np.arange(ROPE_FREQS, dtype=np.float32) / np.float32(ROPE_FREQS))
    ang_row = (row[:, None] * inv[None, :]).astype(np.float32).astype(np.float64)
    ang_col = (col[:, None] * inv[None, :]).astype(np.float32).astype(np.float64)
    zero = np.zeros_like(ang_row)
    cos = np.concatenate([np.cos(ang_row)] * 2 + [np.cos(ang_col)] * 2, axis=1)
    sin_hi = np.concatenate([-np.sin(ang_row), zero, -np.sin(ang_col), zero], axis=1)
    sin_lo = np.concatenate([zero, np.sin(ang_row), zero, np.sin(ang_col)], axis=1)
    rep = LANES // HEAD_DIM
    return tuple(jnp.asarray(np.tile(a, (1, rep)), F32) for a in (cos, sin_hi, sin_lo))


def _na_bias_index():
    cq = np.arange(GRID_W)
    dc = np.clip(cq[None, :] - cq[:, None] + NA_KW - 1, 0, 2 * NA_KW - 2)
    return jnp.asarray(np.concatenate([dc, dc + LANES // 2], axis=1), jnp.int32)


def _mod_kernel(cs_ref, w_ref, b_ref, o_ref):
    s = _silu(cs_ref[...]).astype(BF16)
    o_ref[0] = _dot(s, w_ref[0].astype(BF16)) + b_ref[0]


def _modulation(cs, ada_w, ada_b):
    tn = 1536
    return pl.pallas_call(
        _mod_kernel,
        grid=(DEPTH, 6 * D_MODEL // tn),
        in_specs=[pl.BlockSpec((MOD_ROWS, D_MODEL), lambda i, j: (0, 0)),
                  pl.BlockSpec((1, D_MODEL, tn), lambda i, j: (i, 0, j), pipeline_mode=pl.Buffered(STREAM_BUFFERS)),
                  pl.BlockSpec((1, 1, tn), lambda i, j: (i, 0, j))],
        out_specs=pl.BlockSpec((1, MOD_ROWS, tn), lambda i, j: (i, 0, j)),
        out_shape=jax.ShapeDtypeStruct((DEPTH, MOD_ROWS, 6 * D_MODEL), F32),
        compiler_params=_params("arbitrary", "arbitrary"),
        name="ada_modulation",
    )(cs, ada_w, ada_b.reshape(DEPTH, 1, 6 * D_MODEL))


def _mod_slice(mod_ref, k):
    return mod_ref[0, :, k * D_MODEL:(k + 1) * D_MODEL]


def _rms_mod(x, g, scale, shift):
    y = x * lax.rsqrt(jnp.mean(x * x, axis=-1, keepdims=True) + EPS)
    return (y * g) * (1.0 + scale) + shift


def _head_rms(t, gain_ref, hm_ref):
    w = t.shape[1]
    ms = _dot((t * t).astype(BF16), hm_ref[:w, :w])
    return t * lax.rsqrt(ms + EPS) * gain_ref[:, :w]


def _rope(t, cos, sin_hi, sin_lo):
    up = pltpu.roll(t, LANES - ROPE_FREQS, axis=1)
    dn = pltpu.roll(t, ROPE_FREQS, axis=1)
    return t * cos + up * sin_hi + dn * sin_lo


def _cast_once(w32_ref, w_ref):
    @pl.when((pl.program_id(0) == 0) & (pl.program_id(1) == 0))
    def _():
        w_ref[...] = w32_ref[...].astype(BF16)


def _in_even_kernel(latent, x_ref, mod_ref, g_ref, w32_ref, qg_ref, kg_ref, hm_ref, wc_ref, *rest):
    if latent:
        cos_ref, sh_ref, sl_ref, y_ref, q_ref, k_ref, v_ref, w_ref = rest
    else:
        y_ref, q_ref, k_ref, v_ref, w_ref = rest
    _cast_once(w32_ref, w_ref)
    F = FOURIER_WIDTH
    lo_lanes = lax.broadcasted_iota(jnp.int32, (1, LANES), 1) < HEAD_DIM

    for r in range(x_ref.shape[1] // SUB_ROWS):
        rows = slice(r * SUB_ROWS, (r + 1) * SUB_ROWS)
        h = _rms_mod(x_ref[0, rows], g_ref[...], _mod_slice(mod_ref, 1), _mod_slice(mod_ref, 0)).astype(BF16)
        f = _dot(h, w_ref[:, :F]).astype(BF16)
        for g in range(FOURIER_GROUPS):
            yg = _dot(f[:, g * LANES:(g + 1) * LANES], wc_ref[...])
            y_ref[0, rows, g * LANES:(g + 1) * LANES] = yg[:, :LANES].astype(y_ref.dtype)
            y_ref[0, rows, F + g * LANES:F + (g + 1) * LANES] = yg[:, LANES:].astype(y_ref.dtype)

        def qk(t, gain_ref):
            t = _head_rms(t, gain_ref, hm_ref)
            if latent:
                t = jnp.concatenate([_rope(t[:, i * LANES:(i + 1) * LANES], cos_ref[rows], sh_ref[rows], sl_ref[rows])
                                     for i in range(t.shape[1] // LANES)], axis=1)
            return t

        q = _dot(h, w_ref[:, F:F + QW])
        for j in range(QW // RMS_W):
            qj = qk(q[:, j * RMS_W:(j + 1) * RMS_W], qg_ref) * QK_SCALE
            q_ref[0, rows, j * RMS_W:(j + 1) * RMS_W] = qj.astype(BF16)
        kv = _dot(h, w_ref[:, F + QW:])

        def store_dup(t, o_ref):
            sw = pltpu.roll(t, HEAD_DIM, axis=1)
            o_ref[0, 0, rows] = jnp.where(lo_lanes, t, sw).astype(BF16)
            o_ref[0, 1, rows] = jnp.where(lo_lanes, sw, t).astype(BF16)

        store_dup(qk(kv[:, :KW], kg_ref), k_ref)
        store_dup(kv[:, KW:], v_ref)


def _in_even(x, mod, mod_row, g, w_in, q_g, k_g, hm, wc, rope_tabs, tm):
    B, L, _ = x.shape
    latent = rope_tabs is not None
    in_specs = [pl.BlockSpec((1, tm, D_MODEL), lambda b, t: (b, t, 0)),
                pl.BlockSpec((1, 1, 6 * D_MODEL), lambda b, t: (mod_row(b), 0, 0)),
                _const_spec((1, D_MODEL)),
                _const_spec((D_MODEL, EV_IN_WIDTH)),
                _const_spec((1, RMS_W)), _const_spec((1, RMS_W)),
                _const_spec((RMS_W, RMS_W)), _const_spec((LANES, 2 * LANES))]
    args = [x, mod, g, w_in, q_g, k_g, hm, wc]
    tok = lambda w: pl.BlockSpec((1, tm, w), lambda b, t: (b, t, 0))
    kv_spec = pl.BlockSpec((1, WIN_KV_HEADS, tm, LANES), lambda b, t: (b, 0, t, 0))
    if latent:
        in_specs += [pl.BlockSpec((tm, LANES), lambda b, t: (t, 0))] * 3
        args += list(rope_tabs)
    return pl.pallas_call(
        functools.partial(_in_even_kernel, latent),
        grid=(B, L // tm),
        in_specs=in_specs,
        out_specs=[tok(2 * FOURIER_WIDTH), tok(QW), kv_spec, kv_spec],
        out_shape=[jax.ShapeDtypeStruct((B, L, 2 * FOURIER_WIDTH), F32 if latent else BF16),
                   jax.ShapeDtypeStruct((B, L, QW), BF16),
                   jax.ShapeDtypeStruct((B, WIN_KV_HEADS, L, LANES), BF16),
                   jax.ShapeDtypeStruct((B, WIN_KV_HEADS, L, LANES), BF16)],
        compiler_params=_params("arbitrary", "arbitrary"),
        scratch_shapes=[pltpu.VMEM((D_MODEL, EV_IN_WIDTH), BF16)],
        name="in_even_latent" if latent else "in_even_ctx",
    )(*args)


def _fft_stage1_kernel(n2t, y_ref, m1_ref, twc_ref, tws_ref, o_ref, row_ref):
    F = FOURIER_WIDTH
    for j in range(n2t):
        row_ref[j % 2] = y_ref[0, :, j, :]
        a = _dot(m1_ref[...], row_ref[j % 2].astype(BF16))
        top, bot = a[:FFT_N], a[FFT_N:]
        ar = top[:, :F] - bot[:, F:]
        ai = top[:, F:] + bot[:, :F]
        tc = jnp.tile(twc_ref[j], (1, F // LANES))
        ts = jnp.tile(tws_ref[j], (1, F // LANES))
        o_ref[0, 0, :, j * F:(j + 1) * F] = (ar * tc + ai * ts).astype(BF16)
        o_ref[0, 1, :, j * F:(j + 1) * F] = (ai * tc - ar * ts).astype(BF16)


def _fft_stage2_kernel(k1t, b_ref, m3_ref, o_ref):
    for j in range(k1t):
        rhs = jnp.concatenate([b_ref[0, 0, j], b_ref[0, 1, j]], axis=0)
        o_ref[0, j] = _dot(m3_ref[...], rhs)


def _fourier_latent(y, m1, m3, twc, tws):
    B = y.shape[0]
    F = FOURIER_WIDTH
    n2t = 16
    yv = y.reshape(B, FFT_N, FFT_N, 2 * F)
    bh = pl.pallas_call(
        functools.partial(_fft_stage1_kernel, n2t),
        grid=(B, FFT_N // n2t),
        in_specs=[pl.BlockSpec((1, FFT_N, n2t, 2 * F), lambda b, t: (b, 0, t, 0),
                               pipeline_mode=pl.Buffered(STREAM_BUFFERS)),
                  _const_spec((2 * FFT_N, FFT_N)),
                  pl.BlockSpec((n2t, FFT_N, LANES), lambda b, t: (t, 0, 0)),
                  pl.BlockSpec((n2t, FFT_N, LANES), lambda b, t: (t, 0, 0))],
        out_specs=pl.BlockSpec((1, 2, FFT_N, n2t * F), lambda b, t: (b, 0, 0, t)),
        out_shape=jax.ShapeDtypeStruct((B, 2, FFT_N, FFT_N * F), BF16),
        scratch_shapes=[pltpu.VMEM((2, FFT_N, 2 * F), F32)],
        compiler_params=_params("arbitrary", "arbitrary"),
        name="fft_stage1",
    )(yv, m1, twc, tws)
    k1t = 16
    bv = bh.reshape(B, 2, FFT_N, FFT_N, F)
    return pl.pallas_call(
        functools.partial(_fft_stage2_kernel, k1t),
        grid=(B, FFT_N // k1t),
        in_specs=[pl.BlockSpec((1, 2, k1t, FFT_N, F), lambda b, t: (b, 0, t, 0, 0),
                               pipeline_mode=pl.Buffered(STREAM_BUFFERS)),
                  _const_spec((FFT_N, 2 * FFT_N))],
        out_specs=pl.BlockSpec((1, k1t, FFT_N, F), lambda b, t: (b, t, 0, 0)),
        out_shape=jax.ShapeDtypeStruct((B, FFT_N, FFT_N, F), F32),
        compiler_params=_params("arbitrary", "arbitrary"),
        name="fft_stage2",
    )(bv, m3)


def _fourier_ctx_kernel(y_ref, m_ref, o_ref):
    F = FOURIER_WIDTH
    rhs = jnp.concatenate([y_ref[0, :, :F], y_ref[0, :, F:]], axis=0)
    o_ref[0] = _dot(m_ref[...], rhs)


def _fourier_ctx(y, mctx):
    B = y.shape[0]
    return pl.pallas_call(
        _fourier_ctx_kernel,
        grid=(B,),
        in_specs=[pl.BlockSpec((1, CTX_LEN, 2 * FOURIER_WIDTH), lambda b: (b, 0, 0)),
                  _const_spec((CTX_LEN, 2 * CTX_LEN))],
        out_specs=pl.BlockSpec((1, CTX_LEN, FOURIER_WIDTH), lambda b: (b, 0, 0)),
        out_shape=jax.ShapeDtypeStruct((B, CTX_LEN, FOURIER_WIDTH), F32),
        compiler_params=_params("arbitrary"),
        name="fourier_ctx",
    )(y, mctx)


def _gqa_scores(q_slabs, keys, rows, s_ref):
    lo_lanes = lax.broadcasted_iota(jnp.int32, (rows, LANES), 1) < HEAD_DIM
    zero = jnp.zeros((rows, LANES), BF16)
    qs = jnp.concatenate([jnp.where(lo_lanes if half == 0 else ~lo_lanes, qslab, zero)
                          for qslab in q_slabs for half in range(2)], axis=0)
    s_ref[...] = _dot_nt(qs, keys)


def _gqa_attend(vals, masks, sinks, rows, out_slab, first_slab, scratch):
    s_ref, p_ref, inv_ref = scratch
    lo_lanes = lax.broadcasted_iota(jnp.int32, (rows, LANES), 1) < HEAD_DIM
    for c in range(WIN_GROUP * rows // SOFTMAX_ROWS):
        rs = slice(c * SOFTMAX_ROWS, (c + 1) * SOFTMAX_ROWS)
        s = s_ref[rs, :]
        if masks is not None:
            band, w = masks
            r0 = (c * SOFTMAX_ROWS) % rows
            s = jnp.concatenate([s[:, :w] + band[r0:r0 + SOFTMAX_ROWS], s[:, w:]], axis=1)
        sk = sinks[(c * SOFTMAX_ROWS) // rows]
        m = jnp.maximum(jnp.max(s, axis=-1, keepdims=True), sk)
        e = jnp.exp2(s - m)
        inv_ref[rs, :] = 1.0 / (jnp.sum(e, axis=-1, keepdims=True) + jnp.exp2(sk - m))
        p_ref[rs, :] = e.astype(BF16)
    o = _dot(p_ref[...], vals) * inv_ref[...]
    for i in range(WIN_GROUP // 2):
        a = o[(2 * i) * rows:(2 * i + 1) * rows]
        b = o[(2 * i + 1) * rows:(2 * i + 2) * rows]
        sl = slice((first_slab + i) * LANES, (first_slab + i + 1) * LANES)
        out_slab(sl)[...] = jnp.where(lo_lanes, a, b).astype(BF16)


def _gqa_scratch(rows, n_keys):
    one = [pltpu.VMEM((WIN_GROUP * rows, n_keys), F32), pltpu.VMEM((WIN_GROUP * rows, n_keys), BF16),
           pltpu.VMEM((WIN_GROUP * rows, 1), F32)]
    return one * WIN_KV_HEADS


def _group_sinks(sink_ref, kv):
    return [sink_ref[kv * WIN_GROUP + g] * LOG2E for g in range(WIN_GROUP)]


def _group_slab(kv, i):
    return slice((kv * (WIN_GROUP // 2) + i) * LANES, (kv * (WIN_GROUP // 2) + i + 1) * LANES)


def _window_bands():
    i = np.arange(WIN_BLOCK)[:, None]
    j = np.arange(3 * WIN_BLOCK)[None, :]
    bands = [np.where(np.abs(j - off - i) <= WIN_RADIUS, 0.0, -np.inf) for off in (0, WIN_BLOCK, 2 * WIN_BLOCK)]
    return jnp.asarray(np.stack(bands).astype(np.float32))


def _win_attn_kernel(blocks_per_step, sink_ref, band_ref, q_ref, k_ref, v_ref, kx_ref, vx_ref, o_ref,
                     s0_ref, s1_ref, p_ref, inv_ref):
    t = pl.program_id(1)
    nb = SEQ // WIN_BLOCK
    nw = 3 * WIN_BLOCK

    def place(j):
        n = t * blocks_per_step + j
        start = pl.multiple_of(jnp.clip((n - 1) * WIN_BLOCK, 0, SEQ - nw), WIN_BLOCK)
        sel = jnp.where(n == 0, 0, jnp.where(n == nb - 1, 2, 1))
        return pl.multiple_of(j * WIN_BLOCK, WIN_BLOCK), start, sel

    def scores(j, kv, s_ref):
        qoff, start, _ = place(j)
        keys = jnp.concatenate([k_ref[0, kv, pl.ds(start, nw), :], kx_ref[0, kv]], axis=0)
        q_slabs = [q_ref[0, pl.ds(qoff, WIN_BLOCK), _group_slab(kv, i)] for i in range(WIN_GROUP // 2)]
        _gqa_scores(q_slabs, keys, WIN_BLOCK, s_ref)

    def attend(j, kv, s_ref):
        qoff, start, sel = place(j)
        vals = jnp.concatenate([v_ref[0, kv, pl.ds(start, nw), :], vx_ref[0, kv]], axis=0)
        _gqa_attend(vals, (band_ref[sel], nw), _group_sinks(sink_ref, kv), WIN_BLOCK,
                    lambda sl: o_ref.at[0, pl.ds(qoff, WIN_BLOCK), sl], kv * (WIN_GROUP // 2),
                    (s_ref, p_ref, inv_ref))

    scores(0, 0, s0_ref)

    def one_block(j, carry):
        scores(j, 1, s1_ref)
        attend(j, 0, s0_ref)
        scores(jnp.minimum(j + 1, blocks_per_step - 1), 0, s0_ref)
        attend(j, 1, s1_ref)
        return carry

    lax.fori_loop(0, blocks_per_step, one_block, 0)


def _win_attention(q, k, v, kx, vx, sink):
    B, L, _ = q.shape
    blocks_per_step = 16
    rows = blocks_per_step * WIN_BLOCK
    n_keys = 3 * WIN_BLOCK + CTX_LEN
    full = pl.BlockSpec((1, WIN_KV_HEADS, L, LANES), lambda b, t: (b, 0, 0, 0))
    ctx_spec = pl.BlockSpec((1, WIN_KV_HEADS, CTX_LEN, LANES), lambda b, t: (0, 0, b, 0))
    tok = pl.BlockSpec((1, rows, QW), lambda b, t: (b, t, 0))
    return pl.pallas_call(
        functools.partial(_win_attn_kernel, blocks_per_step),
        grid=(B, L // rows),
        in_specs=[pl.BlockSpec(memory_space=pltpu.SMEM),
                  _const_spec((3, WIN_BLOCK, 3 * WIN_BLOCK)),
                  tok, full, full, ctx_spec, ctx_spec],
        out_specs=tok,
        out_shape=jax.ShapeDtypeStruct((B, L, QW), BF16),
        scratch_shapes=[pltpu.VMEM((WIN_GROUP * WIN_BLOCK, n_keys), F32), pltpu.VMEM((WIN_GROUP * WIN_BLOCK, n_keys), F32),
                        pltpu.VMEM((WIN_GROUP * WIN_BLOCK, n_keys), BF16), pltpu.VMEM((WIN_GROUP * WIN_BLOCK, 1), F32)],
        compiler_params=_params("arbitrary", "arbitrary"),
        name="window_attention",
    )(sink, _window_bands(), q, k, v, kx, vx)


def _ctx_attn_even_kernel(sink_ref, q_ref, k_ref, v_ref, o_ref, *scratch):
    for b in range(q_ref.shape[0]):
        tokens = slice(b * CTX_LEN, (b + 1) * CTX_LEN)
        for kv in range(WIN_KV_HEADS):
            unit = scratch[3 * (b * WIN_KV_HEADS + kv):3 * (b * WIN_KV_HEADS + kv) + 3]
            q_slabs = [q_ref[b, :, _group_slab(kv, i)] for i in range(WIN_GROUP // 2)]
            _gqa_scores(q_slabs, k_ref[0, kv, tokens, :], CTX_LEN, unit[0])
            _gqa_attend(v_ref[0, kv, tokens, :], None, _group_sinks(sink_ref, kv), CTX_LEN,
                        lambda sl, b=b: o_ref.at[b, :, sl], kv * (WIN_GROUP // 2), unit)


def _ctx_attention_even(q, k, v, sink):
    B = q.shape[0]
    whole = lambda a: pl.BlockSpec(a.shape, lambda i: (0,) * a.ndim)
    return pl.pallas_call(
        _ctx_attn_even_kernel,
        grid=(1,),
        in_specs=[pl.BlockSpec(memory_space=pltpu.SMEM), whole(q), whole(k), whole(v)],
        out_specs=whole(q),
        out_shape=jax.ShapeDtypeStruct((B, CTX_LEN, QW), BF16),
        scratch_shapes=_gqa_scratch(CTX_LEN, CTX_LEN) * B,
        compiler_params=_params("arbitrary"),
        name="ctx_attention_even",
    )(sink, q, k, v)


def _out_ffn_kernel(mode, *refs):
    if mode == "odd":
        a_ref, x_ref, mod_ref, g_ref, wo_ref, wg_ref, wu_ref, wd_ref, o_ref, hid_ref = refs
    else:
        f_ref, a_ref, x_ref, mod_ref, g_ref, wo_ref, wg_ref, wu_ref, wd_ref, o_ref, hid_ref = refs
    for r in range(x_ref.shape[1] // FFN_SUB_ROWS):
        rows = slice(r * FFN_SUB_ROWS, (r + 1) * FFN_SUB_ROWS)
        if mode == "even_latent":
            planes = range(r * FFN_SUB_ROWS // FFT_N, (r + 1) * FFN_SUB_ROWS // FFT_N)
            fm = jnp.concatenate([f_ref[0, :, j, :] for j in planes], axis=0).astype(BF16)
            o = _dot(jnp.concatenate([fm, a_ref[0, rows]], axis=1), wo_ref[...])
        else:
            o = _dot(jnp.concatenate([a_ref[0, j, rows] for j in range(a_ref.shape[1])], axis=1), wo_ref[...])
        x1 = x_ref[0, rows] + _mod_slice(mod_ref, 2) * o
        h = _rms_mod(x1, g_ref[...], _mod_slice(mod_ref, 4), _mod_slice(mod_ref, 3)).astype(BF16)
        for c in range(D_FF // FF_CHUNK):
            cs = slice(c * FF_CHUNK, (c + 1) * FF_CHUNK)
            hid_ref[:, cs] = (_silu(_dot(h, wg_ref[0, :, cs])) * _dot(h, wu_ref[0, :, cs])).astype(BF16)
        o_ref[0, rows] = x1 + _mod_slice(mod_ref, 5) * _dot(hid_ref[...], wd_ref[0])


def _out_ffn(mode, mix, x, mod, mod_row, g, w_out, layer, wg, wu, wd, tm):
    B, L, _ = x.shape
    tok = lambda w: pl.BlockSpec((1, tm, w), lambda b, t: (b, t, 0))
    ffn_spec = lambda r, c: pl.BlockSpec((1, r, c), lambda b, t: (layer, 0, 0), pipeline_mode=pl.Buffered(1))
    assert mode in ("even_latent", "odd") and tm % FFN_SUB_ROWS == 0
    if mode == "even_latent":
        mix_specs = [pl.BlockSpec((1, FFT_N, tm // FFT_N, FOURIER_WIDTH), lambda b, t: (b, 0, t, 0)), tok(QW)]
    else:
        mix_specs = [pl.BlockSpec((1, NA_WIDTH // LANES, tm, LANES), lambda b, t: (b, 0, t, 0))]
    return pl.pallas_call(
        functools.partial(_out_ffn_kernel, mode),
        grid=(B, L // tm),
        in_specs=mix_specs + [tok(D_MODEL),
                              pl.BlockSpec((1, 1, 6 * D_MODEL), lambda b, t: (mod_row(b), 0, 0)),
                              _const_spec((1, D_MODEL)),
                              _const_spec((D_MODEL, D_MODEL)),
                              ffn_spec(D_MODEL, D_FF), ffn_spec(D_MODEL, D_FF), ffn_spec(D_FF, D_MODEL)],
        out_specs=tok(D_MODEL),
        out_shape=jax.ShapeDtypeStruct((B, L, D_MODEL), F32),
        scratch_shapes=[pltpu.VMEM((FFN_SUB_ROWS, D_FF), BF16)],
        compiler_params=_params("arbitrary", "arbitrary"),
        name="out_ffn_" + mode,
    )(*mix, x, mod, g, w_out, wg, wu, wd)


def _ctx_ffn_kernel(f_ref, a_ref, x_ref, mod_ref, g_ref, wo_ref, wg32_ref, wu32_ref, wd32_ref,
                    o_ref, wg_ref, wu_ref, wd_ref, x1_ref, h_ref, acc_ref):
    c = pl.program_id(0)
    wg_ref[...] = wg32_ref[...].astype(BF16)
    wu_ref[...] = wu32_ref[...].astype(BF16)
    wd_ref[...] = wd32_ref[...].astype(BF16)

    @pl.when(c == 0)
    def _():
        o = _dot(f_ref[0].astype(BF16), wo_ref[:FOURIER_WIDTH]) + _dot(a_ref[0], wo_ref[FOURIER_WIDTH:])
        x1 = x_ref[0] + _mod_slice(mod_ref, 2) * o
        x1_ref[...] = x1
        h_ref[...] = _rms_mod(x1, g_ref[...], _mod_slice(mod_ref, 4), _mod_slice(mod_ref, 3)).astype(BF16)
        acc_ref[...] = jnp.zeros_like(acc_ref)

    h = h_ref[...]
    a = _silu(_dot(h, wg_ref[0])) * _dot(h, wu_ref[0])
    acc_ref[...] += _dot(a.astype(BF16), wd_ref[0])

    @pl.when(c == pl.num_programs(0) - 1)
    def _():
        o_ref[0] = x1_ref[...] + _mod_slice(mod_ref, 5) * acc_ref[...]


def _ctx_ffn_and_weights(f, a, x, mod, mod_row, g, w_out, wg32, wu32, wd32):
    _, n, _ = x.shape
    nc = D_FF // FF_CHUNK
    whole = lambda w: pl.BlockSpec((1, n, w), lambda c: (0, 0, 0))
    col_chunk = pl.BlockSpec((DEPTH, D_MODEL, FF_CHUNK), lambda c: (0, 0, c))
    row_chunk = pl.BlockSpec((DEPTH, FF_CHUNK, D_MODEL), lambda c: (0, c, 0))
    deep = dict(pipeline_mode=pl.Buffered(STREAM_BUFFERS))
    col_in = pl.BlockSpec((DEPTH, D_MODEL, FF_CHUNK), lambda c: (0, 0, c), **deep)
    row_in = pl.BlockSpec((DEPTH, FF_CHUNK, D_MODEL), lambda c: (0, c, 0), **deep)
    return pl.pallas_call(
        _ctx_ffn_kernel,
        grid=(nc,),
        in_specs=[whole(FOURIER_WIDTH), whole(QW), whole(D_MODEL),
                  pl.BlockSpec((1, 1, 6 * D_MODEL), lambda c: (mod_row(0), 0, 0)),
                  pl.BlockSpec((1, D_MODEL), lambda c: (0, 0)),
                  pl.BlockSpec((D_MODEL, D_MODEL), lambda c: (0, 0)),
                  col_in, col_in, row_in],
        out_specs=[whole(D_MODEL), col_chunk, col_chunk, row_chunk],
        out_shape=[jax.ShapeDtypeStruct((1, n, D_MODEL), F32),
                   jax.ShapeDtypeStruct((DEPTH, D_MODEL, D_FF), BF16),
                   jax.ShapeDtypeStruct((DEPTH, D_MODEL, D_FF), BF16),
                   jax.ShapeDtypeStruct((DEPTH, D_FF, D_MODEL), BF16)],
        scratch_shapes=[pltpu.VMEM((n, D_MODEL), F32), pltpu.VMEM((n, D_MODEL), BF16), pltpu.VMEM((n, D_MODEL), F32)],
        compiler_params=_params("arbitrary"),
        name="ctx_ffn_and_weights",
    )(f, a, x, mod, g, w_out, wg32, wu32, wd32)


def _in_odd_kernel(with_q, x_ref, mod_ref, g_ref, w32_ref, qg_ref, kg_ref, hm_ref, *outs):
    W = NA_WIDTH
    slabs = RMS_W // LANES
    if with_q:
        q_ref, k_ref, v_ref, w_ref = outs
    else:
        k_ref, v_ref, w_ref = outs
    _cast_once(w32_ref, w_ref)

    for r in range(x_ref.shape[1] // SUB_ROWS):
        rows = slice(r * SUB_ROWS, (r + 1) * SUB_ROWS)
        h = _rms_mod(x_ref[0, rows], g_ref[...], _mod_slice(mod_ref, 1), _mod_slice(mod_ref, 0)).astype(BF16)

        def normed(t, gain_ref, scale, o_ref):
            for j in range(W // RMS_W):
                tj = (_head_rms(t[:, j * RMS_W:(j + 1) * RMS_W], gain_ref, hm_ref) * scale).astype(BF16)
                for i in range(slabs):
                    o_ref[0, j * slabs + i, rows] = tj[:, i * LANES:(i + 1) * LANES]

        if with_q:
            normed(_dot(h, w_ref[:, :W]), qg_ref, QK_SCALE, q_ref)
        normed(_dot(h, w_ref[:, W:2 * W]), kg_ref, 1.0, k_ref)
        v = _dot(h, w_ref[:, 2 * W:]).astype(BF16)
        for j in range(W // LANES):
            v_ref[0, j, rows] = v[:, j * LANES:(j + 1) * LANES]


def _in_odd(x, mod, mod_row, g, w_in, q_g, k_g, hm, with_q, tm):
    B, L, _ = x.shape
    pairs = NA_WIDTH // LANES
    tok = pl.BlockSpec((1, pairs, tm, LANES), lambda b, t: (b, 0, t, 0))
    n_out = 3 if with_q else 2
    return pl.pallas_call(
        functools.partial(_in_odd_kernel, with_q),
        grid=(B, L // tm),
        in_specs=[pl.BlockSpec((1, tm, D_MODEL), lambda b, t: (b, t, 0)),
                  pl.BlockSpec((1, 1, 6 * D_MODEL), lambda b, t: (mod_row(b), 0, 0)),
                  _const_spec((1, D_MODEL)),
                  _const_spec((D_MODEL, 3 * NA_WIDTH)),
                  _const_spec((1, RMS_W)), _const_spec((1, RMS_W)), _const_spec((RMS_W, RMS_W))],
        out_specs=[tok] * n_out,
        out_shape=[jax.ShapeDtypeStruct((B, pairs, L, LANES), BF16)] * n_out,
        compiler_params=_params("arbitrary", "arbitrary"),
        scratch_shapes=[pltpu.VMEM((D_MODEL, 3 * NA_WIDTH), BF16)],
        name="in_odd" if with_q else "in_odd_ctx",
    )(x, mod, g, w_in, q_g, k_g, hm)


def _bias_table_kernel(rb_ref, idx_ref, o_ref):
    idx = idx_ref[...]
    cq = lax.broadcasted_iota(jnp.int32, idx.shape, 0)
    ck = lax.broadcasted_iota(jnp.int32, idx.shape, 1) % GRID_W
    c0 = jnp.clip(cq - NA_KW // 2, 0, GRID_W - NA_KW)
    inside = (ck >= c0) & (ck < c0 + NA_KW)
    for h in range(o_ref.shape[0]):
        for dr in range(o_ref.shape[1]):
            row = jnp.broadcast_to(rb_ref[h, dr:dr + 1, :], idx.shape)
            o_ref[h, dr] = jnp.where(inside, jnp.take_along_axis(row, idx, axis=1) * LOG2E, -jnp.inf)


def _na_bias_table(rel_bias, idx):
    H = NA_HEADS
    npair = 2 * NA_KH - 2
    half = LANES // 2
    hb = 8
    pad = lambda a: jnp.pad(a, ((0, 0), (0, 0), (0, half - a.shape[-1])))
    rb2 = jnp.concatenate([pad(rel_bias[:, :-1]), pad(rel_bias[:, 1:])], axis=-1)
    return pl.pallas_call(
        _bias_table_kernel,
        grid=(H // hb,),
        in_specs=[pl.BlockSpec((hb, npair, LANES), lambda h: (h, 0, 0)), _const_spec((GRID_W, LANES))],
        out_specs=pl.BlockSpec((hb, npair, GRID_W, LANES), lambda h: (h, 0, 0, 0)),
        out_shape=jax.ShapeDtypeStruct((H, npair, GRID_W, LANES), F32),
        compiler_params=_params("arbitrary"),
        name="na_bias_table",
    )(rb2, idx)


def _na_kernel(rows_per_step, q_ref, qn_ref, k_ref, v_ref, kx_ref, vx_ref, bias_ref, o_ref,
               s0_ref, s1_ref, p_ref, inv_ref):
    t = pl.program_id(1)
    n = NA_KH * GRID_W
    lo_lanes = lax.broadcasted_iota(jnp.int32, (GRID_W, LANES), 1) < HEAD_DIM

    def window(i):
        r = jnp.minimum(t * rows_per_step + i, GRID_ROWS - 1)
        r0 = jnp.clip(r - NA_KH // 2, 0, GRID_ROWS - NA_KH)
        qoff = jnp.minimum(i, rows_per_step - 1) * GRID_W
        return r0 - r + NA_KH - 1, pl.multiple_of(r0 * GRID_W, GRID_W), pl.multiple_of(qoff, GRID_W)

    def scores(i, s_ref):
        dr0, start, qoff = window(i)
        for hp in range(NA_HEADS // 2):
            qs = jnp.where(i < rows_per_step, q_ref[0, hp, pl.ds(qoff, GRID_W), :], qn_ref[0, hp])
            keys = jnp.concatenate([k_ref[0, hp, pl.ds(start, n), :], kx_ref[0, hp]], axis=0)
            qm = jnp.concatenate([jnp.where(lo_lanes, qs, jnp.zeros_like(qs)),
                                  jnp.where(lo_lanes, jnp.zeros_like(qs), qs)], axis=0)
            bias = jnp.concatenate(
                [jnp.concatenate([bias_ref[2 * hp + half, dr0 + 2 * p] for p in range(NA_KH // 2)], axis=1)
                 for half in range(2)], axis=0)
            rs = slice(2 * hp * GRID_W, (2 * hp + 2) * GRID_W)
            s_ref[rs, :n] = _dot_nt(qm, keys[:n]) + bias
            s_ref[rs, n:] = _dot_nt(qm, keys[n:])

    def attend(i, s_ref):
        _, start, qoff = window(i)
        for c in range(NA_HEADS * GRID_W // SOFTMAX_ROWS):
            cs = slice(c * SOFTMAX_ROWS, (c + 1) * SOFTMAX_ROWS)
            s = s_ref[cs, :]
            e = jnp.exp2(s - jnp.max(s, axis=-1, keepdims=True))
            inv_ref[cs, :] = 1.0 / jnp.sum(e, axis=-1, keepdims=True)
            p_ref[cs, :] = e.astype(BF16)
        for hp in range(NA_HEADS // 2):
            vals = jnp.concatenate([v_ref[0, hp, pl.ds(start, n), :], vx_ref[0, hp]], axis=0)
            rs = slice(2 * hp * GRID_W, (2 * hp + 2) * GRID_W)
            res = _dot(p_ref[rs, :], vals) * inv_ref[rs, :]
            o_ref[0, hp, pl.ds(qoff, GRID_W), :] = jnp.where(lo_lanes, res[:GRID_W], res[GRID_W:]).astype(BF16)

    @pl.when(t == 0)
    def _():
        scores(0, s0_ref)

    def two_rows(j, carry):
        scores(2 * j + 1, s1_ref)
        attend(2 * j, s0_ref)
        scores(2 * j + 2, s0_ref)
        attend(2 * j + 1, s1_ref)
        return carry

    lax.fori_loop(0, rows_per_step // 2, two_rows, 0)


def _neighbourhood_attention(q, k, v, kx, vx, bias_tab):
    B, pairs, L, _ = q.shape
    rows_per_step = 16
    full = pl.BlockSpec((1, pairs, L, LANES), lambda b, t: (b, 0, 0, 0))
    ctx_spec = pl.BlockSpec((1, pairs, CTX_LEN, LANES), lambda b, t: (0, 0, b, 0))
    q_rows = pl.BlockSpec((1, pairs, rows_per_step * GRID_W, LANES), lambda b, t: (b, 0, t, 0))
    q_next = pl.BlockSpec((1, pairs, GRID_W, LANES),
                          lambda b, t: (b, 0, jnp.minimum((t + 1) * rows_per_step, GRID_ROWS - 1), 0))
    n_keys = NA_KH * GRID_W + CTX_LEN
    all_rows = NA_HEADS * GRID_W
    return pl.pallas_call(
        functools.partial(_na_kernel, rows_per_step),
        grid=(B, GRID_ROWS // rows_per_step),
        in_specs=[q_rows, q_next, full, full, ctx_spec, ctx_spec, _const_spec(bias_tab.shape)],
        out_specs=q_rows,
        out_shape=jax.ShapeDtypeStruct((B, pairs, L, LANES), BF16),
        scratch_shapes=[pltpu.VMEM((all_rows, n_keys), F32), pltpu.VMEM((all_rows, n_keys), F32),
                        pltpu.VMEM((all_rows, n_keys), BF16), pltpu.VMEM((all_rows, 1), F32)],
        compiler_params=_params("arbitrary", "arbitrary"),
        name="neighbourhood_attention",
    )(q, q, k, v, kx, vx, bias_tab)


def kernel(x, c, ctx, c_ctx, ada_w, ada_b, norm1_g, norm2_g, ffn_w_gate, ffn_w_up, ffn_w_down,
           ev_w_in, ev_w_out, ev_q_norm, ev_k_norm, ev_sink,
           od_w_in, od_w_out, od_q_norm, od_k_norm, od_rel_bias):
    assert x.shape == (BATCH, SEQ, D_MODEL) and ctx.shape == (BATCH, CTX_LEN, D_MODEL)
    wc, m1, m3, twc, tws, mctx = _fourier_tables()
    hm = _head_mean_matrix()
    rope_tabs = _rope_tables()
    lane_gain = lambda gvec: jnp.tile(gvec, RMS_W // HEAD_DIM).reshape(1, RMS_W)
    lat_row = lambda b: b
    ctx_row = lambda b: CTX_MOD_ROW
    tm = 512
    n_ctx = BATCH * CTX_LEN
    as_seq = lambda a: a.reshape(1, n_ctx, a.shape[-1])
    per_batch = lambda a: a.reshape(BATCH, CTX_LEN, a.shape[-1])

    cs = jnp.concatenate([c, c_ctx[None, :], jnp.zeros((MOD_ROWS - BATCH - 1, D_MODEL), F32)], axis=0)
    mod = _modulation(cs, ada_w, ada_b).reshape(DEPTH, MOD_ROWS, 1, 6 * D_MODEL)

    w_in0 = ev_w_in[0]
    w_out0 = ev_w_out[0].astype(BF16)
    g1 = norm1_g[0].reshape(1, D_MODEL)
    g2 = norm2_g[0].reshape(1, D_MODEL)
    qg, kg = lane_gain(ev_q_norm[0]), lane_gain(ev_k_norm[0])
    y_l, q_l, k_l, v_l = _in_even(x, mod[0], lat_row, g1, w_in0, qg, kg, hm, wc, rope_tabs, 2 * tm)
    y_c, q_c, k_c, v_c = _in_even(as_seq(ctx), mod[0], ctx_row, g1, w_in0, qg, kg, hm, wc, None, tm)
    y_c, q_c = per_batch(y_c), per_batch(q_c)
    f_l = _fourier_latent(y_l, m1, m3, twc, tws)
    f_c = _fourier_ctx(y_c, mctx)
    a_l = _win_attention(q_l, k_l, v_l, k_c, v_c, ev_sink[0])
    a_c = _ctx_attention_even(q_c, k_c, v_c, ev_sink[0])
    y1, *ffn = _ctx_ffn_and_weights(as_seq(f_c), as_seq(a_c), as_seq(ctx), mod[0], ctx_row, g2, w_out0,
                                    ffn_w_gate, ffn_w_up, ffn_w_down)
    x1 = _out_ffn("even_latent", (f_l, a_l), x, mod[0], lat_row, g2, w_out0, 0, *ffn, 2 * tm)

    w_in1 = od_w_in[0]
    w_out1 = od_w_out[0].astype(BF16)
    g1 = norm1_g[1].reshape(1, D_MODEL)
    g2 = norm2_g[1].reshape(1, D_MODEL)
    qg, kg = lane_gain(od_q_norm[0]), lane_gain(od_k_norm[0])
    q_l, k_l, v_l = _in_odd(x1, mod[1], lat_row, g1, w_in1, qg, kg, hm, True, 2 * tm)
    k_c, v_c = _in_odd(y1, mod[1], ctx_row, g1, w_in1, qg, kg, hm, False, n_ctx)
    bias_tab = _na_bias_table(od_rel_bias[0], _na_bias_index())
    a_l = _neighbourhood_attention(q_l, k_l, v_l, k_c, v_c, bias_tab)
    return _out_ffn("odd", (a_l,), x1, mod[1], lat_row, g2, w_out1, 1, *ffn, 2 * tm)
```
